```python
import jax, jax.numpy as jnp
from jax import lax
import numpy as np

D_MODEL = 1024
BATCH = 8
SEQ = 16384
DEPTH = 4

MEM_LEN = 256
N_HEADS_MLA = 8
QK_NOPE_DIM = 64
QK_ROPE_DIM = 32
QK_HEAD_DIM = QK_NOPE_DIM + QK_ROPE_DIM
V_HEAD_DIM = 64
Q_LORA_RANK = 3 * D_MODEL // 8
KV_LORA_RANK = D_MODEL // 4
MLA_WIDTH = N_HEADS_MLA * V_HEAD_DIM
CONV_WIDTH = D_MODEL // 2
CONV_K = 3
N_HEADS_MEM = 4
MEM_HEAD_DIM = 128
MEM_WIDTH = N_HEADS_MEM * MEM_HEAD_DIM

N_BRANCH = 3
ROPE_BASE = 10000.0
Q_BLOCK = 128
EPS = 1e-6

IN_SIZES = (Q_LORA_RANK, KV_LORA_RANK, QK_ROPE_DIM,
            CONV_WIDTH, CONV_WIDTH, CONV_WIDTH,
            MEM_WIDTH,
            MLA_WIDTH, CONV_WIDTH, MEM_WIDTH,
            N_BRANCH * D_MODEL)
IN_WIDTH = sum(IN_SIZES)

kernel_name = 'hybrid_mla_shortconv_memxattn_encoder'


def rmsnorm(t, g):
    tf = t.astype(jnp.float32)
    tf = tf * lax.rsqrt(jnp.mean(tf * tf, axis=-1, keepdims=True) + EPS)
    return tf.astype(t.dtype) * g


def split_cols(t, sizes):
    return jnp.split(t, np.cumsum(sizes)[:-1].tolist(), axis=-1)


def rope_tables(positions, dtype):
    inv_freq = ROPE_BASE ** (-jnp.arange(0, QK_ROPE_DIM, 2, dtype=jnp.float32) / QK_ROPE_DIM)
    ang = positions.astype(jnp.float32)[..., None] * inv_freq
    return (jnp.cos(ang)[:, :, None, :].astype(dtype),
            jnp.sin(ang)[:, :, None, :].astype(dtype))


def rope_tail(t, cos, sin):
    t_nope, t1, t2 = split_cols(t, (QK_NOPE_DIM, QK_ROPE_DIM // 2, QK_ROPE_DIM // 2))
    return jnp.concatenate([t_nope, t1 * cos - t2 * sin, t2 * cos + t1 * sin], axis=-1)


def blocked_bidirectional_attention(q, k, v):
    B, S, H, Dh = q.shape
    nblk = S // Q_BLOCK
    qb = q.reshape(B, nblk, Q_BLOCK, H, Dh).transpose(1, 0, 2, 3, 4)
    scale = Dh ** -0.5

    def one_block(q_blk):
        s = jnp.einsum('bqhd,bkhd->bhqk', q_blk, k).astype(jnp.float32) * scale
        p = jax.nn.softmax(s, axis=-1).astype(v.dtype)
        return jnp.einsum('bhqk,bkhd->bqhd', p, v)

    out = lax.map(one_block, qb)
    return out.transpose(1, 0, 2, 3, 4).reshape(B, S, H * v.shape[-1])


def centred_short_conv(z, w, b):
    out = lax.conv_general_dilated(
        z, w[:, None, :], window_strides=(1,),
        padding=((CONV_K // 2, CONV_K // 2),),
        dimension_numbers=('NWC', 'WIO', 'NWC'),
        feature_group_count=z.shape[-1])
    return out + b


def hybrid_layer(x, mem, cos, sin, norm_g, w_in, b_gate, q_norm_g, w_uq, kv_norm_g, w_ukv,
                 q_head_g, k_head_g, conv_w, conv_b, mem_norm_g, w_mkv, mem_q_g, mem_k_g,
                 w_br_attn, w_br_conv, w_br_mem, w_out):
    B, S, _ = x.shape
    M = mem.shape[1]
    h = rmsnorm(x, norm_g)
    proj = h @ w_in
    (q_lat, kv_lat, k_pe, c_b, c_c, c_u, q_mem,
     g_attn, g_conv, g_mem, r) = split_cols(proj, IN_SIZES)

    q = (rmsnorm(q_lat, q_norm_g) @ w_uq).reshape(B, S, N_HEADS_MLA, QK_HEAD_DIM)
    kv = (rmsnorm(kv_lat, kv_norm_g) @ w_ukv).reshape(B, S, N_HEADS_MLA, QK_NOPE_DIM + V_HEAD_DIM)
    k_nope, v = split_cols(kv, (QK_NOPE_DIM, V_HEAD_DIM))
    k_rope = jnp.broadcast_to(k_pe[:, :, None, :], (B, S, N_HEADS_MLA, QK_ROPE_DIM))
    k = jnp.concatenate([k_nope, k_rope], axis=-1)
    q = rope_tail(rmsnorm(q, q_head_g), cos, sin)
    k = rope_tail(rmsnorm(k, k_head_g), cos, sin)
    o_attn = blocked_bidirectional_attention(q, k, v) * jax.nn.silu(g_attn)

    o_conv = c_b * centred_short_conv(c_c * c_u, conv_w, conv_b) * jax.nn.silu(g_conv)

    mkv = (rmsnorm(mem, mem_norm_g) @ w_mkv).reshape(B, M, N_HEADS_MEM, 2 * MEM_HEAD_DIM)
    m_k, m_v = split_cols(mkv, (MEM_HEAD_DIM, MEM_HEAD_DIM))
    mq = rmsnorm(q_mem.reshape(B, S, N_HEADS_MEM, MEM_HEAD_DIM), mem_q_g)
    m_k = rmsnorm(m_k, mem_k_g)
    s = jnp.einsum('bshd,bmhd->bhsm', mq, m_k).astype(jnp.float32) * (MEM_HEAD_DIM ** -0.5)
    p = jax.nn.softmax(s, axis=-1).astype(m_v.dtype)
    o_mem = jnp.einsum('bhsm,bmhd->bshd', p, m_v).reshape(B, S, MEM_WIDTH) * jax.nn.silu(g_mem)

    r_attn, r_conv, r_mem = split_cols(jax.nn.sigmoid(r + b_gate), (D_MODEL, D_MODEL, D_MODEL))
    y = r_attn * (o_attn @ w_br_attn) + r_conv * (o_conv @ w_br_conv) + r_mem * (o_mem @ w_br_mem)
    return x + y @ w_out


def _fwd_setup_inputs(seed: int = 0) -> dict:
    key = jax.random.key(seed)
    ks = jax.random.split(key, 24)

    def nrm(k, shape, scale):
        return jax.random.normal(k, shape, jnp.float32) * scale

    def gain(k, shape):
        return 1.0 + 0.1 * jax.random.normal(k, shape, jnp.float32)

    x = nrm(ks[0], (BATCH, SEQ, D_MODEL), 1.0)
    mem = nrm(ks[1], (BATCH, MEM_LEN, D_MODEL), 1.0)
    offset = jax.random.randint(ks[2], (BATCH, 1), 0, 1024, dtype=jnp.int32)
    positions = jnp.arange(SEQ, dtype=jnp.int32)[None, :] + offset
    return {
        'x': x,
        'mem': mem,
        'positions': positions,
        'norm_g': gain(ks[3], (DEPTH, D_MODEL)),
        'w_in': nrm(ks[4], (DEPTH, D_MODEL, IN_WIDTH), D_MODEL ** -0.5),
        'b_gate': nrm(ks[5], (DEPTH, N_BRANCH * D_MODEL), 0.1),
        'q_norm_g': gain(ks[6], (DEPTH, Q_LORA_RANK)),
        'w_uq': nrm(ks[7], (DEPTH, Q_LORA_RANK, N_HEADS_MLA * QK_HEAD_DIM), Q_LORA_RANK ** -0.5),
        'kv_norm_g': gain(ks[8], (DEPTH, KV_LORA_RANK)),
        'w_ukv': nrm(ks[9], (DEPTH, KV_LORA_RANK, N_HEADS_MLA * (QK_NOPE_DIM + V_HEAD_DIM)), KV_LORA_RANK ** -0.5),
        'q_head_g': gain(ks[10], (DEPTH, QK_HEAD_DIM)),
        'k_head_g': gain(ks[11], (DEPTH, QK_HEAD_DIM)),
        'conv_w': nrm(ks[12], (DEPTH, CONV_K, CONV_WIDTH), CONV_K ** -0.5),
        'conv_b': nrm(ks[13], (DEPTH, CONV_WIDTH), 0.1),
        'mem_norm_g': gain(ks[14], (DEPTH, D_MODEL)),
        'w_mkv': nrm(ks[15], (DEPTH, D_MODEL, 2 * MEM_WIDTH), D_MODEL ** -0.5),
        'mem_q_g': gain(ks[16], (DEPTH, MEM_HEAD_DIM)),
        'mem_k_g': gain(ks[17], (DEPTH, MEM_HEAD_DIM)),
        'w_br_attn': nrm(ks[18], (DEPTH, MLA_WIDTH, D_MODEL), MLA_WIDTH ** -0.5),
        'w_br_conv': nrm(ks[19], (DEPTH, CONV_WIDTH, D_MODEL), CONV_WIDTH ** -0.5),
        'w_br_mem': nrm(ks[20], (DEPTH, MEM_WIDTH, D_MODEL), MEM_WIDTH ** -0.5),
        'w_out': nrm(ks[21], (DEPTH, D_MODEL, D_MODEL), D_MODEL ** -0.5),
    }


def _fwd_reference(x, mem, positions, norm_g, w_in, b_gate, q_norm_g, w_uq, kv_norm_g, w_ukv,
              q_head_g, k_head_g, conv_w, conv_b, mem_norm_g, w_mkv, mem_q_g, mem_k_g,
              w_br_attn, w_br_conv, w_br_mem, w_out):
    cos, sin = rope_tables(positions, x.dtype)
    for i in range(DEPTH):
        x = hybrid_layer(x, mem, cos, sin, norm_g[i], w_in[i], b_gate[i], q_norm_g[i], w_uq[i],
                         kv_norm_g[i], w_ukv[i], q_head_g[i], k_head_g[i], conv_w[i], conv_b[i],
                         mem_norm_g[i], w_mkv[i], mem_q_g[i], mem_k_g[i],
                         w_br_attn[i], w_br_conv[i], w_br_mem[i], w_out[i])
    return x


import jax as _jax
import jax.numpy as _jnp

TWIN_FORMAT = 'train_step'
FWD_PARAMS = ['x', 'mem', 'positions', 'norm_g', 'w_in', 'b_gate', 'q_norm_g', 'w_uq', 'kv_norm_g', 'w_ukv', 'q_head_g', 'k_head_g', 'conv_w', 'conv_b', 'mem_norm_g', 'w_mkv', 'mem_q_g', 'mem_k_g', 'w_br_attn', 'w_br_conv', 'w_br_mem', 'w_out']
TWIN_WEIGHTS = ['norm_g', 'w_in', 'b_gate', 'q_norm_g', 'w_uq', 'kv_norm_g', 'w_ukv', 'q_head_g', 'k_head_g', 'conv_w', 'conv_b', 'mem_norm_g', 'w_mkv', 'mem_q_g', 'mem_k_g', 'w_br_attn', 'w_br_conv', 'w_br_mem', 'w_out']
TWIN_DIFF_INPUT = 'x'
TWIN_INPUTS = ['x', 'mem', 'positions', 'norm_g', 'w_in', 'b_gate', 'q_norm_g', 'w_uq', 'kv_norm_g', 'w_ukv', 'q_head_g', 'k_head_g', 'conv_w', 'conv_b', 'mem_norm_g', 'w_mkv', 'mem_q_g', 'mem_k_g', 'w_br_attn', 'w_br_conv', 'w_br_mem', 'w_out', 'loss_target', 'm_norm_g', 'm_w_in', 'm_b_gate', 'm_q_norm_g', 'm_w_uq', 'm_kv_norm_g', 'm_w_ukv', 'm_q_head_g', 'm_k_head_g', 'm_conv_w', 'm_conv_b', 'm_mem_norm_g', 'm_w_mkv', 'm_mem_q_g', 'm_mem_k_g', 'm_w_br_attn', 'm_w_br_conv', 'm_w_br_mem', 'm_w_out', 'v_norm_g', 'v_w_in', 'v_b_gate', 'v_q_norm_g', 'v_w_uq', 'v_kv_norm_g', 'v_w_ukv', 'v_q_head_g', 'v_k_head_g', 'v_conv_w', 'v_conv_b', 'v_mem_norm_g', 'v_w_mkv', 'v_mem_q_g', 'v_mem_k_g', 'v_w_br_attn', 'v_w_br_conv', 'v_w_br_mem', 'v_w_out']
TWIN_OUTPUTS = ['loss', 'grad_x', 'grad_norm_g', 'grad_w_in', 'grad_b_gate', 'grad_q_norm_g', 'grad_w_uq', 'grad_kv_norm_g', 'grad_w_ukv', 'grad_q_head_g', 'grad_k_head_g', 'grad_conv_w', 'grad_conv_b', 'grad_mem_norm_g', 'grad_w_mkv', 'grad_mem_q_g', 'grad_mem_k_g', 'grad_w_br_attn', 'grad_w_br_conv', 'grad_w_br_mem', 'grad_w_out', 'delta_norm_g', 'delta_w_in', 'delta_b_gate', 'delta_q_norm_g', 'delta_w_uq', 'delta_kv_norm_g', 'delta_w_ukv', 'delta_q_head_g', 'delta_k_head_g', 'delta_conv_w', 'delta_conv_b', 'delta_mem_norm_g', 'delta_w_mkv', 'delta_mem_q_g', 'delta_mem_k_g', 'delta_w_br_attn', 'delta_w_br_conv', 'delta_w_br_mem', 'delta_w_out', 'new_m_norm_g', 'new_m_w_in', 'new_m_b_gate', 'new_m_q_norm_g', 'new_m_w_uq', 'new_m_kv_norm_g', 'new_m_w_ukv', 'new_m_q_head_g', 'new_m_k_head_g', 'new_m_conv_w', 'new_m_conv_b', 'new_m_mem_norm_g', 'new_m_w_mkv', 'new_m_mem_q_g', 'new_m_mem_k_g', 'new_m_w_br_attn', 'new_m_w_br_conv', 'new_m_w_br_mem', 'new_m_w_out', 'new_v_norm_g', 'new_v_w_in', 'new_v_b_gate', 'new_v_q_norm_g', 'new_v_w_uq', 'new_v_kv_norm_g', 'new_v_w_ukv', 'new_v_q_head_g', 'new_v_k_head_g', 'new_v_conv_w', 'new_v_conv_b', 'new_v_mem_norm_g', 'new_v_w_mkv', 'new_v_mem_q_g', 'new_v_mem_k_g', 'new_v_w_br_attn', 'new_v_w_br_conv', 'new_v_w_br_mem', 'new_v_w_out']
TWIN_LEAF_KINDS = {'loss': 'loss', 'grad_x': 'grad_x', 'grad_norm_g': 'grad_w', 'grad_w_in': 'grad_w', 'grad_b_gate': 'grad_w', 'grad_q_norm_g': 'grad_w', 'grad_w_uq': 'grad_w', 'grad_kv_norm_g': 'grad_w', 'grad_w_ukv': 'grad_w', 'grad_q_head_g': 'grad_w', 'grad_k_head_g': 'grad_w', 'grad_conv_w': 'grad_w', 'grad_conv_b': 'grad_w', 'grad_mem_norm_g': 'grad_w', 'grad_w_mkv': 'grad_w', 'grad_mem_q_g': 'grad_w', 'grad_mem_k_g': 'grad_w', 'grad_w_br_attn': 'grad_w', 'grad_w_br_conv': 'grad_w', 'grad_w_br_mem': 'grad_w', 'grad_w_out': 'grad_w', 'delta_norm_g': 'delta_w', 'delta_w_in': 'delta_w', 'delta_b_gate': 'delta_w', 'delta_q_norm_g': 'delta_w', 'delta_w_uq': 'delta_w', 'delta_kv_norm_g': 'delta_w', 'delta_w_ukv': 'delta_w', 'delta_q_head_g': 'delta_w', 'delta_k_head_g': 'delta_w', 'delta_conv_w': 'delta_w', 'delta_conv_b': 'delta_w', 'delta_mem_norm_g': 'delta_w', 'delta_w_mkv': 'delta_w', 'delta_mem_q_g': 'delta_w', 'delta_mem_k_g': 'delta_w', 'delta_w_br_attn': 'delta_w', 'delta_w_br_conv': 'delta_w', 'delta_w_br_mem': 'delta_w', 'delta_w_out': 'delta_w', 'new_m_norm_g': 'new_m', 'new_m_w_in': 'new_m', 'new_m_b_gate': 'new_m', 'new_m_q_norm_g': 'new_m', 'new_m_w_uq': 'new_m', 'new_m_kv_norm_g': 'new_m', 'new_m_w_ukv': 'new_m', 'new_m_q_head_g': 'new_m', 'new_m_k_head_g': 'new_m', 'new_m_conv_w': 'new_m', 'new_m_conv_b': 'new_m', 'new_m_mem_norm_g': 'new_m', 'new_m_w_mkv': 'new_m', 'new_m_mem_q_g': 'new_m', 'new_m_mem_k_g': 'new_m', 'new_m_w_br_attn': 'new_m', 'new_m_w_br_conv': 'new_m', 'new_m_w_br_mem': 'new_m', 'new_m_w_out': 'new_m', 'new_v_norm_g': 'new_v', 'new_v_w_in': 'new_v', 'new_v_b_gate': 'new_v', 'new_v_q_norm_g': 'new_v', 'new_v_w_uq': 'new_v', 'new_v_kv_norm_g': 'new_v', 'new_v_w_ukv': 'new_v', 'new_v_q_head_g': 'new_v', 'new_v_k_head_g': 'new_v', 'new_v_conv_w': 'new_v', 'new_v_conv_b': 'new_v', 'new_v_mem_norm_g': 'new_v', 'new_v_w_mkv': 'new_v', 'new_v_mem_q_g': 'new_v', 'new_v_mem_k_g': 'new_v', 'new_v_w_br_attn': 'new_v', 'new_v_w_br_conv': 'new_v', 'new_v_w_br_mem': 'new_v', 'new_v_w_out': 'new_v'}


def _forward(args):
    return _fwd_reference(*[args[k] for k in FWD_PARAMS])


def _output_shape():
    def fwd():
        inp = _fwd_setup_inputs(0)
        return _fwd_reference(*[inp[k] for k in FWD_PARAMS])
    out = _jax.eval_shape(fwd)
    return out.shape, out.dtype

N_MICROBATCH = 1
ADAM_LR = 0.001
ADAM_B1 = 0.9
ADAM_B2 = 0.999
ADAM_EPS = 1e-08
ADAM_WD = 0.01
ADAM_STEP = 10
PER_EXAMPLE_BATCH_AXIS = {'x': 0, 'mem': 0, 'positions': 0, 'loss_target': 0}
SHARED_INPUTS = []
_WEIGHT_DTYPES = {'norm_g': _jnp.float32, 'w_in': _jnp.float32, 'b_gate': _jnp.float32, 'q_norm_g': _jnp.float32, 'w_uq': _jnp.float32, 'kv_norm_g': _jnp.float32, 'w_ukv': _jnp.float32, 'q_head_g': _jnp.float32, 'k_head_g': _jnp.float32, 'conv_w': _jnp.float32, 'conv_b': _jnp.float32, 'mem_norm_g': _jnp.float32, 'w_mkv': _jnp.float32, 'mem_q_g': _jnp.float32, 'mem_k_g': _jnp.float32, 'w_br_attn': _jnp.float32, 'w_br_conv': _jnp.float32, 'w_br_mem': _jnp.float32, 'w_out': _jnp.float32}
MOMENT_SCALE = {'norm_g': 5.811365e+01, 'w_in': 5.811747e-01, 'b_gate': 2.704162e+00, 'q_norm_g': 8.333558e-02, 'w_uq': 6.132450e-02, 'kv_norm_g': 1.795090e-01, 'w_ukv': 7.155106e-02, 'q_head_g': 2.788630e-01, 'k_head_g': 2.782695e-01, 'conv_w': 1.422857e+01, 'conv_b': 1.857876e+00, 'mem_norm_g': 1.036611e-01, 'w_mkv': 7.648652e-02, 'mem_q_g': 8.823437e-01, 'mem_k_g': 8.857398e-01, 'w_br_attn': 5.511148e-02, 'w_br_conv': 6.888971e-01, 'w_br_mem': 5.778069e-02, 'w_out': 6.211975e-01}


def _to_microbatches(a, axis):
    t = _jnp.moveaxis(a, axis, 0)
    t = t.reshape((N_MICROBATCH, t.shape[0] // N_MICROBATCH) + t.shape[1:])
    return _jnp.moveaxis(t, 1, axis + 1)


def setup_inputs(seed: int = 0) -> dict:
    inp = _fwd_setup_inputs(seed)
    key = _jax.random.fold_in(_jax.random.key(seed), 7919)
    shape, _ = _output_shape()
    out = dict(inp)
    out["loss_target"] = _jax.random.normal(_jax.random.fold_in(key, 0), shape, _jnp.float32)
    for i, name in enumerate(TWIN_WEIGHTS):
        w = inp[name].astype(_jnp.float32)
        if MOMENT_SCALE is None:
            s = _jnp.sqrt(_jnp.mean(_jnp.square(w)) + 1e-30)
        else:
            s = MOMENT_SCALE[name]
        km, kv = _jax.random.split(_jax.random.fold_in(key, i + 1))
        out[name] = w
        out["m_" + name] = s * _jax.random.normal(km, w.shape, _jnp.float32)
        out["v_" + name] = (s * s) * _jax.random.uniform(kv, w.shape, _jnp.float32, 0.5, 1.5)
    if N_MICROBATCH > 1:
        for name, axis in PER_EXAMPLE_BATCH_AXIS.items():
            out[name] = _to_microbatches(out[name], axis)
    return {'x': out['x'], 'mem': out['mem'], 'positions': out['positions'], 'norm_g': out['norm_g'], 'w_in': out['w_in'], 'b_gate': out['b_gate'], 'q_norm_g': out['q_norm_g'], 'w_uq': out['w_uq'], 'kv_norm_g': out['kv_norm_g'], 'w_ukv': out['w_ukv'], 'q_head_g': out['q_head_g'], 'k_head_g': out['k_head_g'], 'conv_w': out['conv_w'], 'conv_b': out['conv_b'], 'mem_norm_g': out['mem_norm_g'], 'w_mkv': out['w_mkv'], 'mem_q_g': out['mem_q_g'], 'mem_k_g': out['mem_k_g'], 'w_br_attn': out['w_br_attn'], 'w_br_conv': out['w_br_conv'], 'w_br_mem': out['w_br_mem'], 'w_out': out['w_out'], 'loss_target': out['loss_target'], 'm_norm_g': out['m_norm_g'], 'm_w_in': out['m_w_in'], 'm_b_gate': out['m_b_gate'], 'm_q_norm_g': out['m_q_norm_g'], 'm_w_uq': out['m_w_uq'], 'm_kv_norm_g': out['m_kv_norm_g'], 'm_w_ukv': out['m_w_ukv'], 'm_q_head_g': out['m_q_head_g'], 'm_k_head_g': out['m_k_head_g'], 'm_conv_w': out['m_conv_w'], 'm_conv_b': out['m_conv_b'], 'm_mem_norm_g': out['m_mem_norm_g'], 'm_w_mkv': out['m_w_mkv'], 'm_mem_q_g': out['m_mem_q_g'], 'm_mem_k_g': out['m_mem_k_g'], 'm_w_br_attn': out['m_w_br_attn'], 'm_w_br_conv': out['m_w_br_conv'], 'm_w_br_mem': out['m_w_br_mem'], 'm_w_out': out['m_w_out'], 'v_norm_g': out['v_norm_g'], 'v_w_in': out['v_w_in'], 'v_b_gate': out['v_b_gate'], 'v_q_norm_g': out['v_q_norm_g'], 'v_w_uq': out['v_w_uq'], 'v_kv_norm_g': out['v_kv_norm_g'], 'v_w_ukv': out['v_w_ukv'], 'v_q_head_g': out['v_q_head_g'], 'v_k_head_g': out['v_k_head_g'], 'v_conv_w': out['v_conv_w'], 'v_conv_b': out['v_conv_b'], 'v_mem_norm_g': out['v_mem_norm_g'], 'v_w_mkv': out['v_w_mkv'], 'v_mem_q_g': out['v_mem_q_g'], 'v_mem_k_g': out['v_mem_k_g'], 'v_w_br_attn': out['v_w_br_attn'], 'v_w_br_conv': out['v_w_br_conv'], 'v_w_br_mem': out['v_w_br_mem'], 'v_w_out': out['v_w_out']}


def _loss(weights, diff, rest, loss_target):
    with _jax.named_scope("forward"):
        args = {**rest, TWIN_DIFF_INPUT: diff, **{k: w.astype(_WEIGHT_DTYPES[k]) for k, w in weights.items()}}
        y = _forward(args)
    with _jax.named_scope("loss_head"):
        err = _jnp.square(y.astype(_jnp.float32) - loss_target)
        return 0.5 * _jnp.sum(_jnp.mean(err, axis=-1)) if err.ndim else 0.5 * err


def _adamw(w, g, m, v):
    m = ADAM_B1 * m + (1.0 - ADAM_B1) * g
    v = ADAM_B2 * v + (1.0 - ADAM_B2) * _jnp.square(g)
    m_hat = m / (1.0 - ADAM_B1 ** ADAM_STEP)
    v_hat = v / (1.0 - ADAM_B2 ** ADAM_STEP)
    delta = -ADAM_LR * (m_hat / (_jnp.sqrt(v_hat) + ADAM_EPS) + ADAM_WD * w)
    return delta, m, v


def reference(x, mem, positions, norm_g, w_in, b_gate, q_norm_g, w_uq, kv_norm_g, w_ukv, q_head_g, k_head_g, conv_w, conv_b, mem_norm_g, w_mkv, mem_q_g, mem_k_g, w_br_attn, w_br_conv, w_br_mem, w_out, loss_target, m_norm_g, m_w_in, m_b_gate, m_q_norm_g, m_w_uq, m_kv_norm_g, m_w_ukv, m_q_head_g, m_k_head_g, m_conv_w, m_conv_b, m_mem_norm_g, m_w_mkv, m_mem_q_g, m_mem_k_g, m_w_br_attn, m_w_br_conv, m_w_br_mem, m_w_out, v_norm_g, v_w_in, v_b_gate, v_q_norm_g, v_w_uq, v_kv_norm_g, v_w_ukv, v_q_head_g, v_k_head_g, v_conv_w, v_conv_b, v_mem_norm_g, v_w_mkv, v_mem_q_g, v_mem_k_g, v_w_br_attn, v_w_br_conv, v_w_br_mem, v_w_out):
    given = dict(x=x, mem=mem, positions=positions, norm_g=norm_g, w_in=w_in, b_gate=b_gate, q_norm_g=q_norm_g, w_uq=w_uq, kv_norm_g=kv_norm_g, w_ukv=w_ukv, q_head_g=q_head_g, k_head_g=k_head_g, conv_w=conv_w, conv_b=conv_b, mem_norm_g=mem_norm_g, w_mkv=w_mkv, mem_q_g=mem_q_g, mem_k_g=mem_k_g, w_br_attn=w_br_attn, w_br_conv=w_br_conv, w_br_mem=w_br_mem, w_out=w_out, loss_target=loss_target, m_norm_g=m_norm_g, m_w_in=m_w_in, m_b_gate=m_b_gate, m_q_norm_g=m_q_norm_g, m_w_uq=m_w_uq, m_kv_norm_g=m_kv_norm_g, m_w_ukv=m_w_ukv, m_q_head_g=m_q_head_g, m_k_head_g=m_k_head_g, m_conv_w=m_conv_w, m_conv_b=m_conv_b, m_mem_norm_g=m_mem_norm_g, m_w_mkv=m_w_mkv, m_mem_q_g=m_mem_q_g, m_mem_k_g=m_mem_k_g, m_w_br_attn=m_w_br_attn, m_w_br_conv=m_w_br_conv, m_w_br_mem=m_w_br_mem, m_w_out=m_w_out, v_norm_g=v_norm_g, v_w_in=v_w_in, v_b_gate=v_b_gate, v_q_norm_g=v_q_norm_g, v_w_uq=v_w_uq, v_kv_norm_g=v_kv_norm_g, v_w_ukv=v_w_ukv, v_q_head_g=v_q_head_g, v_k_head_g=v_k_head_g, v_conv_w=v_conv_w, v_conv_b=v_conv_b, v_mem_norm_g=v_mem_norm_g, v_w_mkv=v_w_mkv, v_mem_q_g=v_mem_q_g, v_mem_k_g=v_mem_k_g, v_w_br_attn=v_w_br_attn, v_w_br_conv=v_w_br_conv, v_w_br_mem=v_w_br_mem, v_w_out=v_w_out)
    weights = {n: given[n] for n in TWIN_WEIGHTS}
    shared = {n: given[n] for n in SHARED_INPUTS}
    per_example = {n: given[n] for n in ['x', 'mem', 'positions']}
    grad_fn = _jax.value_and_grad(_loss, argnums=(0, 1))

    def one_microbatch(ex, loss_target):
        ex = dict(ex)
        diff = ex.pop(TWIN_DIFF_INPUT)
        return grad_fn(weights, diff, {**shared, **ex}, loss_target)

    if N_MICROBATCH == 1:
        loss, (grad_w, grad_x) = one_microbatch(per_example, given["loss_target"])
    else:
        def body(carry, xs):
            loss_sum, grad_sum = carry
            l_k, (gw_k, gx_k) = one_microbatch(xs[0], xs[1])
            with _jax.named_scope("update"):
                return (loss_sum + l_k, _jax.tree.map(_jnp.add, grad_sum, gw_k)), gx_k

        init = (_jnp.zeros((), _jnp.float32), _jax.tree.map(_jnp.zeros_like, weights))
        (loss, grad_w), grad_x = _jax.lax.scan(body, init, (per_example, given["loss_target"]))
    with _jax.named_scope("update"):
        delta_w, new_m, new_v = {}, {}, {}
        for n in TWIN_WEIGHTS:
            delta_w[n], new_m[n], new_v[n] = _adamw(weights[n], grad_w[n], given["m_" + n], given["v_" + n])
    return (loss, grad_x, *[grad_w[n] for n in TWIN_WEIGHTS], *[delta_w[n] for n in TWIN_WEIGHTS],
            *[new_m[n] for n in TWIN_WEIGHTS], *[new_v[n] for n in TWIN_WEIGHTS])
```

```python
import functools

import numpy as np
import jax
import jax.numpy as jnp
from jax import lax
from jax.experimental import pallas as pl
from jax.experimental.pallas import tpu as pltpu

F32 = jnp.float32
BF16 = jnp.bfloat16

D_MODEL = 1024
DEPTH = 4
N_HEADS = 8
QK_DIM = 96
NOPE_DIM = 64
ROPE_DIM = 32
V_DIM = 64
Q_RANK = 384
KV_RANK = 256
CONV_W = 512
MEM_HEADS = 4
MEM_HD = 128
MEM_W = 512
IN_WIDTH = 7328
PW = 8192
HP = 128
EPS = 1e-6
ROPE_BASE = 10000.0
SCALE = QK_DIM ** -0.5
MEM_SCALE = MEM_HD ** -0.5

ADAM_LR = 0.001
ADAM_B1 = 0.9
ADAM_B2 = 0.999
ADAM_EPS = 1e-08
ADAM_WD = 0.01
ADAM_STEP = 10

VMEM_LIMIT_V7X = 56 * 1024 * 1024

O_R, O_GA, O_CB, O_QM, O_GC, O_GM, O_CC, O_CU, O_QL, O_KVL, O_KPE = (
    0, 3072, 4096, 4608, 5120, 5632, 6144, 6656, 7168, 7552, 7808)


def _params(sem, vmem=VMEM_LIMIT_V7X):
    return pltpu.CompilerParams(dimension_semantics=sem, vmem_limit_bytes=vmem)


def _sigmoid(t):
    return 1.0 / (1.0 + jnp.exp(-t))


def _silu_and_grad(g):
    sg = _sigmoid(g)
    return g * sg, sg * (1.0 + g * (1.0 - sg))


def _rms(t, g, n=None):
    n = t.shape[-1] if n is None else n
    r = lax.rsqrt(jnp.sum(t * t, axis=-1, keepdims=True) * (1.0 / n) + EPS)
    return (t * r) * g


def _rms_parts(t, n=None):
    n = t.shape[-1] if n is None else n
    r = lax.rsqrt(jnp.sum(t * t, axis=-1, keepdims=True) * (1.0 / n) + EPS)
    return r, t * r


def _rms_bwd(dhat, hat, r, n):
    return r * (dhat - hat * (jnp.sum(dhat * hat, axis=-1, keepdims=True) * (1.0 / n)))


def _rope(t, c, sa, sb):
    return t * c + pltpu.roll(t, HP - 16, 1) * sa + pltpu.roll(t, 16, 1) * sb


def _rope_t(d, c, sa, sb):
    return d * c + pltpu.roll(d * sa, 16, 1) + pltpu.roll(d * sb, HP - 16, 1)


def _dot(a, b):
    return jnp.dot(a, b, preferred_element_type=F32)


def _dot_nt(a, b):
    return lax.dot_general(a, b, (((1,), (1,)), ((), ())), preferred_element_type=F32)


def _dot_tn(a, b):
    return lax.dot_general(a, b, (((0,), (0,)), ((), ())), preferred_element_type=F32)


def _colsum(t):
    return jnp.sum(t, axis=0, keepdims=True)


def _tile(n, t):
    t = min(n, t)
    assert n % t == 0, (n, t)
    return t


def _rope_tables(pos_b, invf):
    S = pos_b.shape[0]
    tm = _tile(S, 1024)

    def body(pos_ref, invf_ref, c_ref, sa_ref, sb_ref):
        ang = pos_ref[...].astype(F32) * invf_ref[...]
        lane = lax.broadcasted_iota(jnp.int32, ang.shape, 1)
        cs = jnp.cos(ang)
        sn = jnp.sin(ang)
        c_ref[...] = jnp.where(lane < NOPE_DIM, 1.0, jnp.where(lane < QK_DIM, cs, 0.0))
        sa_ref[...] = jnp.where((lane >= NOPE_DIM) & (lane < NOPE_DIM + 16), -sn, 0.0)
        sb_ref[...] = jnp.where((lane >= NOPE_DIM + 16) & (lane < QK_DIM), sn, 0.0)

    blk = pl.BlockSpec((tm, HP), lambda i: (i, 0))
    return pl.pallas_call(
        body, name="rope_tables", grid=(S // tm,),
        in_specs=[blk, pl.BlockSpec((1, HP), lambda i: (0, 0))],
        out_specs=[blk, blk, blk],
        out_shape=[jax.ShapeDtypeStruct((S, HP), F32)] * 3,
        compiler_params=_params(("parallel",)),
    )(pos_b, invf)


def _inproj(x, g, w):
    S = x.shape[0]
    tm, tn = _tile(S, 512), 2048

    def body(x_ref, g_ref, w_ref, proj_ref, xn_ref):
        @pl.when(pl.program_id(1) == 0)
        def _():
            xn_ref[...] = _rms(x_ref[...], g_ref[...]).astype(BF16)

        proj_ref[...] = _dot(xn_ref[...], w_ref[...])

    return pl.pallas_call(
        body, name="inproj", grid=(S // tm, PW // tn),
        in_specs=[pl.BlockSpec((tm, D_MODEL), lambda i, j: (i, 0)),
                  pl.BlockSpec((1, D_MODEL), lambda i, j: (0, 0)),
                  pl.BlockSpec((D_MODEL, tn), lambda i, j: (0, j))],
        out_specs=[pl.BlockSpec((tm, tn), lambda i, j: (i, j)),
                   pl.BlockSpec((tm, D_MODEL), lambda i, j: (i, 0))],
        out_shape=[jax.ShapeDtypeStruct((S, PW), F32), jax.ShapeDtypeStruct((S, D_MODEL), BF16)],
        compiler_params=_params(("parallel", "arbitrary")),
    )(x, g, w)


def _mla_prep(proj, tabs, qg, kvg, qhg, khg, wuq, wuk, wuv):
    S = proj.shape[0]
    tm = _tile(S, 512)

    def body(a_ref, c_ref, sa_ref, sb_ref, qg_ref, kvg_ref, qhg_ref, khg_ref, wuq_ref, wuk_ref, wuv_ref,
             q_ref, k_ref, v_ref, qn_ref, kvn_ref):
        ql = a_ref[:, 0:Q_RANK]
        kvl = a_ref[:, Q_RANK:Q_RANK + KV_RANK]
        kpe = a_ref[:, Q_RANK + KV_RANK:Q_RANK + KV_RANK + HP]
        qn = _rms(ql, qg_ref[...]).astype(BF16)
        kvn = _rms(kvl, kvg_ref[...]).astype(BF16)
        qn_ref[...] = qn
        kvn_ref[...] = kvn
        qraw = _dot(qn, wuq_ref[...])
        kn = _dot(kvn, wuk_ref[...])
        v_ref[...] = _dot(kvn, wuv_ref[...]).astype(BF16)
        c, sa, sb = c_ref[...], sa_ref[...], sb_ref[...]
        for h in range(N_HEADS):
            sl = slice(h * HP, (h + 1) * HP)
            tq = _rms(qraw[:, sl], qhg_ref[...], QK_DIM)
            q_ref[:, sl] = (_rope(tq, c, sa, sb) * SCALE).astype(BF16)
            tk = _rms(kn[:, sl] + kpe, khg_ref[...], QK_DIM)
            k_ref[:, sl] = _rope(tk, c, sa, sb).astype(BF16)

    row = lambda w: pl.BlockSpec((tm, w), lambda i: (i, 0))
    full = lambda a: pl.BlockSpec(a.shape, lambda i: (0,) * a.ndim)
    return pl.pallas_call(
        body, name="mla_prep", grid=(S // tm,),
        in_specs=[pl.BlockSpec((tm, 1024), lambda i: (i, 7)), row(HP), row(HP), row(HP),
                  full(qg), full(kvg), full(qhg), full(khg), full(wuq), full(wuk), full(wuv)],
        out_specs=[row(1024), row(1024), row(1024), row(Q_RANK), row(KV_RANK)],
        out_shape=[jax.ShapeDtypeStruct((S, 1024), BF16)] * 3
        + [jax.ShapeDtypeStruct((S, Q_RANK), BF16), jax.ShapeDtypeStruct((S, KV_RANK), BF16)],
        compiler_params=_params(("parallel",)),
    )(proj, *tabs, qg, kvg, qhg, khg, wuq, wuk, wuv)


def _attn_fwd(q, k, v):
    S = q.shape[0]
    tq, tk = _tile(S, 512), _tile(S, 512)
    nk = S // tk

    def body(q_ref, k_ref, v_ref, o_ref, lse_ref, m_s, l_s, acc_s):
        m_s[...] = jnp.full(m_s.shape, -jnp.inf, F32)
        l_s[...] = jnp.zeros(l_s.shape, F32)
        acc_s[...] = jnp.zeros(acc_s.shape, F32)
        qv = q_ref[...]

        def step(c, carry):
            rows = pl.ds(pl.multiple_of(c * tk, tk), tk)
            s = _dot_nt(qv, k_ref[rows, :])
            m_prev = m_s[...]
            m_new = jnp.maximum(m_prev, jnp.max(s, axis=-1, keepdims=True))
            alpha = jnp.exp(m_prev - m_new)
            p = jnp.exp(s - m_new)
            l_s[...] = alpha * l_s[...] + jnp.sum(p, axis=-1, keepdims=True)
            acc_s[...] = alpha * acc_s[...] + _dot(p.astype(BF16), v_ref[rows, :])
            m_s[...] = m_new
            return carry

        lax.fori_loop(0, nk, step, 0)
        o_ref[...] = acc_s[...] / l_s[...]
        lse_ref[0] = m_s[...] + jnp.log(l_s[...])

    return pl.pallas_call(
        body, name="attn_fwd", grid=(N_HEADS, S // tq),
        in_specs=[pl.BlockSpec((tq, HP), lambda h, i: (i, h)),
                  pl.BlockSpec((S, HP), lambda h, i: (0, h)),
                  pl.BlockSpec((S, HP), lambda h, i: (0, h))],
        out_specs=[pl.BlockSpec((tq, HP), lambda h, i: (i, h)),
                   pl.BlockSpec((1, tq, 1), lambda h, i: (h, i, 0))],
        out_shape=[jax.ShapeDtypeStruct((S, N_HEADS * HP), F32),
                   jax.ShapeDtypeStruct((N_HEADS, S, 1), F32)],
        scratch_shapes=[pltpu.VMEM((tq, 1), F32), pltpu.VMEM((tq, 1), F32), pltpu.VMEM((tq, HP), F32)],
        compiler_params=_params(("parallel", "parallel")),
    )(q, k, v)


def _memkv(mem, mng, mkg, wmkv):
    M = mem.shape[0]

    def body(mem_ref, mng_ref, mkg_ref, w_ref, mk_ref, mv_ref):
        mn = _rms(mem_ref[...], mng_ref[...]).astype(BF16)
        mkv = _dot(mn, w_ref[...])
        for h in range(MEM_HEADS):
            kraw = mkv[:, 2 * MEM_HD * h:2 * MEM_HD * h + MEM_HD]
            mk_ref[:, MEM_HD * h:MEM_HD * (h + 1)] = _rms(kraw, mkg_ref[...]).astype(BF16)
            mv_ref[:, MEM_HD * h:MEM_HD * (h + 1)] = mkv[:, 2 * MEM_HD * h + MEM_HD:2 * MEM_HD * (h + 1)].astype(BF16)

    return pl.pallas_call(
        body, name="memkv",
        out_shape=[jax.ShapeDtypeStruct((M, MEM_W), BF16)] * 2,
        compiler_params=pltpu.CompilerParams(vmem_limit_bytes=VMEM_LIMIT_V7X),
    )(mem, mng, mkg, wmkv)


def _conv_shifts(cc, cu, hp_ref, hn_ref, i, n_tiles, tm):
    z = cc * cu
    zp = hp_ref[7:8, 0:CONV_W] * hp_ref[7:8, CONV_W:2 * CONV_W]
    zn = hn_ref[0:1, 0:CONV_W] * hn_ref[0:1, CONV_W:2 * CONV_W]
    zp = jnp.where(i == 0, 0.0, zp)
    zn = jnp.where(i == n_tiles - 1, 0.0, zn)
    row = lax.broadcasted_iota(jnp.int32, z.shape, 0)
    z_up = jnp.where(row == 0, zp, pltpu.roll(z, 1, 0))
    z_dn = jnp.where(row == tm - 1, zn, pltpu.roll(z, tm - 1, 0))
    return z, z_up, z_dn


def _halo_specs(tm, S, width, col):
    r8 = tm // 8
    prev = pl.BlockSpec((8, width), lambda i: (jnp.maximum(i * r8 - 1, 0), col))
    nxt = pl.BlockSpec((8, width), lambda i: (jnp.minimum((i + 1) * r8, S // 8 - 1), col))
    return prev, nxt


def _mem_attend(qm, mqg, mk_h, mv_h):
    r, qhat = _rms_parts(qm)
    mq = (qhat * mqg).astype(BF16)
    s = _dot_nt(mq, mk_h) * MEM_SCALE
    e = jnp.exp(s - jnp.max(s, axis=-1, keepdims=True))
    p = e / jnp.sum(e, axis=-1, keepdims=True)
    pv = _dot(p.astype(BF16), mv_h)
    return r, qhat, mq, p, pv


def _merge(proj, o, x, bg, convw, mqg, mk, mv, wba, wbc, wbm, wo):
    S = x.shape[0]
    tm = _tile(S, 256)
    nt = S // tm

    def body(main_ref, ccu_ref, hp_ref, hn_ref, o_ref, x_ref, bg_ref, cw_ref, mqg_ref, mk_ref, mv_ref,
             wba_ref, wbc_ref, wbm_ref, wo_ref, xn_ref, oa_ref, oc_ref, om_ref, u_ref, y_ref):
        i = pl.program_id(0)
        sil_a, _ = _silu_and_grad(main_ref[:, O_GA:O_GA + 1024])
        oa = (o_ref[...] * sil_a).astype(BF16)
        oa_ref[...] = oa
        z, z_up, z_dn = _conv_shifts(ccu_ref[:, 0:CONV_W], ccu_ref[:, CONV_W:], hp_ref, hn_ref, i, nt, tm)
        cv = cw_ref[0:1, :] * z_up + cw_ref[1:2, :] * z + cw_ref[2:3, :] * z_dn + cw_ref[3:4, :]
        sil_c, _ = _silu_and_grad(main_ref[:, O_GC:O_GC + CONV_W])
        oc = (main_ref[:, O_CB:O_CB + CONV_W] * cv * sil_c).astype(BF16)
        oc_ref[...] = oc
        sil_m, _ = _silu_and_grad(main_ref[:, O_GM:O_GM + MEM_W])
        for h in range(MEM_HEADS):
            sl = slice(h * MEM_HD, (h + 1) * MEM_HD)
            qm = main_ref[:, O_QM + h * MEM_HD:O_QM + (h + 1) * MEM_HD]
            pv = _mem_attend(qm, mqg_ref[...], mk_ref[:, sl], mv_ref[:, sl])[4]
            om_ref[:, sl] = (pv * sil_m[:, sl]).astype(BF16)
        ua = _dot(oa, wba_ref[...])
        uc = _dot(oc, wbc_ref[...])
        um = _dot(om_ref[...], wbm_ref[...])
        u_ref[:, 0:1024] = ua.astype(BF16)
        u_ref[:, 1024:2048] = uc.astype(BF16)
        u_ref[:, 2048:3072] = um.astype(BF16)
        rg = _sigmoid(main_ref[:, O_R:O_R + 3072] + bg_ref[...])
        y = (rg[:, 0:1024] * ua + rg[:, 1024:2048] * uc + rg[:, 2048:3072] * um).astype(BF16)
        y_ref[...] = y
        xn_ref[...] = x_ref[...] + _dot(y, wo_ref[...])

    row = lambda w: pl.BlockSpec((tm, w), lambda i: (i, 0))
    full = lambda a: pl.BlockSpec(a.shape, lambda i: (0,) * a.ndim)
    hp, hn = _halo_specs(tm, S, 1024, 6)
    return pl.pallas_call(
        body, name="merge", grid=(nt,),
        in_specs=[row(6144), pl.BlockSpec((tm, 1024), lambda i: (i, 6)), hp, hn, row(1024), row(1024),
                  full(bg), full(convw), full(mqg), full(mk), full(mv), full(wba), full(wbc), full(wbm), full(wo)],
        out_specs=[row(1024), row(1024), row(CONV_W), row(MEM_W), row(3072), row(1024)],
        out_shape=[jax.ShapeDtypeStruct((S, 1024), F32), jax.ShapeDtypeStruct((S, 1024), BF16),
                   jax.ShapeDtypeStruct((S, CONV_W), BF16), jax.ShapeDtypeStruct((S, MEM_W), BF16),
                   jax.ShapeDtypeStruct((S, 3072), BF16), jax.ShapeDtypeStruct((S, 1024), BF16)],
        compiler_params=_params(("parallel",)),
    )(proj, proj, proj, proj, o, x, bg, convw, mqg, mk, mv, wba, wbc, wbm, wo)


def _loss_head(xf, tgt):
    S = xf.shape[0]
    tm = _tile(S, 1024)

    def body(x_ref, t_ref, g_ref, acc_ref):
        @pl.when(pl.program_id(0) == 0)
        def _():
            acc_ref[...] = jnp.zeros(acc_ref.shape, F32)

        e = x_ref[...] - t_ref[...]
        g_ref[...] = e * (1.0 / D_MODEL)
        part = jnp.sum((e * e).reshape(tm // 8, 8, D_MODEL), axis=0)
        tot = part[:, 0:128]
        for k in range(1, D_MODEL // 128):
            tot = tot + part[:, 128 * k:128 * (k + 1)]
        acc_ref[...] += tot

    row = pl.BlockSpec((tm, D_MODEL), lambda i: (i, 0))
    return pl.pallas_call(
        body, name="loss_head", grid=(S // tm,),
        in_specs=[row, row],
        out_specs=[row, pl.BlockSpec((8, 128), lambda i: (0, 0))],
        out_shape=[jax.ShapeDtypeStruct((S, D_MODEL), F32), jax.ShapeDtypeStruct((8, 128), F32)],
        compiler_params=_params(("arbitrary",)),
    )(xf, tgt)


def _merge_bwd(g, proj, o, u, bg, convw, mqg, mk, mv, wot, wbat, wbct, wbmt):
    S = g.shape[0]
    tm = _tile(S, 256)
    nt = S // tm
    M = mk.shape[0]

    def body(g_ref, main_ref, ccu_ref, hp_ref, hn_ref, o_ref, u_ref, bg_ref, cw_ref, mqg_ref, mk_ref, mv_ref,
             wot_ref, wbat_ref, wbct_ref, wbmt_ref,
             dmain_ref, dcv_ref, do_ref, delta_ref, du_ref, dbg_ref, dmk_ref, dmv_ref, dmqg_ref):
        i = pl.program_id(0)

        @pl.when(i == 0)
        def _():
            dbg_ref[...] = jnp.zeros(dbg_ref.shape, F32)
            dmk_ref[...] = jnp.zeros(dmk_ref.shape, F32)
            dmv_ref[...] = jnp.zeros(dmv_ref.shape, F32)
            dmqg_ref[...] = jnp.zeros(dmqg_ref.shape, F32)

        dy = _dot(g_ref[...].astype(BF16), wot_ref[...])
        rg = _sigmoid(main_ref[:, O_R:O_R + 3072] + bg_ref[...])
        dyt = jnp.concatenate([dy, dy, dy], axis=1)
        dr = dyt * u_ref[...].astype(F32) * rg * (1.0 - rg)
        dmain_ref[:, O_R:O_R + 3072] = dr.astype(BF16)
        dbg_ref[...] += _colsum(dr)
        du = (dyt * rg).astype(BF16)
        du_ref[...] = du
        do_a = _dot(du[:, 0:1024], wbat_ref[...])
        do_c = _dot(du[:, 1024:2048], wbct_ref[...])
        do_m = _dot(du[:, 2048:3072], wbmt_ref[...])

        sil_a, dsil_a = _silu_and_grad(main_ref[:, O_GA:O_GA + 1024])
        ov = o_ref[...]
        d_o = do_a * sil_a
        do_ref[...] = d_o.astype(BF16)
        dmain_ref[:, O_GA:O_GA + 1024] = (do_a * ov * dsil_a).astype(BF16)
        prod = d_o * ov
        for h in range(N_HEADS):
            delta_ref[h] = jnp.sum(prod[:, h * HP:(h + 1) * HP], axis=-1, keepdims=True)

        z, z_up, z_dn = _conv_shifts(ccu_ref[:, 0:CONV_W], ccu_ref[:, CONV_W:], hp_ref, hn_ref, i, nt, tm)
        cv = cw_ref[0:1, :] * z_up + cw_ref[1:2, :] * z + cw_ref[2:3, :] * z_dn + cw_ref[3:4, :]
        sil_c, dsil_c = _silu_and_grad(main_ref[:, O_GC:O_GC + CONV_W])
        cb = main_ref[:, O_CB:O_CB + CONV_W]
        dmain_ref[:, O_CB:O_CB + CONV_W] = (do_c * cv * sil_c).astype(BF16)
        dmain_ref[:, O_GC:O_GC + CONV_W] = (do_c * cb * cv * dsil_c).astype(BF16)
        dcv_ref[...] = do_c * cb * sil_c

        sil_m, dsil_m = _silu_and_grad(main_ref[:, O_GM:O_GM + MEM_W])
        for h in range(MEM_HEADS):
            sl = slice(h * MEM_HD, (h + 1) * MEM_HD)
            qm = main_ref[:, O_QM + h * MEM_HD:O_QM + (h + 1) * MEM_HD]
            mk_h, mv_h = mk_ref[:, sl], mv_ref[:, sl]
            r, qhat, mq, p, pv = _mem_attend(qm, mqg_ref[...], mk_h, mv_h)
            dom = do_m[:, sl]
            dmain_ref[:, O_GM + h * MEM_HD:O_GM + (h + 1) * MEM_HD] = (dom * pv * dsil_m[:, sl]).astype(BF16)
            dpv = (dom * sil_m[:, sl]).astype(BF16)
            dp = _dot_nt(dpv, mv_h)
            ds = (p * (dp - jnp.sum(dp * p, axis=-1, keepdims=True)) * MEM_SCALE).astype(BF16)
            dmq = _dot(ds, mk_h)
            dmk_ref[:, sl] += _dot_tn(ds, mq)
            dmv_ref[:, sl] += _dot_tn(p.astype(BF16), dpv)
            dmqg_ref[...] += _colsum(dmq * qhat)
            dqm = _rms_bwd(dmq * mqg_ref[...], qhat, r, MEM_HD)
            dmain_ref[:, O_QM + h * MEM_HD:O_QM + (h + 1) * MEM_HD] = dqm.astype(BF16)

    row = lambda w: pl.BlockSpec((tm, w), lambda i: (i, 0))
    full = lambda a: pl.BlockSpec(a.shape, lambda i: (0,) * a.ndim)
    acc = lambda r, c: pl.BlockSpec((r, c), lambda i: (0, 0))
    hp, hn = _halo_specs(tm, S, 1024, 6)
    return pl.pallas_call(
        body, name="merge_bwd", grid=(nt,),
        in_specs=[row(1024), row(6144), pl.BlockSpec((tm, 1024), lambda i: (i, 6)), hp, hn, row(1024), row(3072),
                  full(bg), full(convw), full(mqg), full(mk), full(mv), full(wot), full(wbat), full(wbct), full(wbmt)],
        out_specs=[row(6144), row(CONV_W), row(1024), pl.BlockSpec((N_HEADS, tm, 1), lambda i: (0, i, 0)), row(3072),
                   acc(1, 3072), acc(M, MEM_W), acc(M, MEM_W), acc(1, MEM_HD)],
        out_shape=[jax.ShapeDtypeStruct((S, 6144), BF16), jax.ShapeDtypeStruct((S, CONV_W), F32),
                   jax.ShapeDtypeStruct((S, 1024), BF16), jax.ShapeDtypeStruct((N_HEADS, S, 1), F32),
                   jax.ShapeDtypeStruct((S, 3072), BF16), jax.ShapeDtypeStruct((1, 3072), F32),
                   jax.ShapeDtypeStruct((M, MEM_W), F32), jax.ShapeDtypeStruct((M, MEM_W), F32),
                   jax.ShapeDtypeStruct((1, MEM_HD), F32)],
        compiler_params=_params(("arbitrary",)),
    )(g, proj, proj, proj, proj, o, u, bg, convw, mqg, mk, mv, wot, wbat, wbct, wbmt)


def _conv_bwd(dcv, proj, convw):
    S = dcv.shape[0]
    tm = _tile(S, 512)
    nt = S // tm

    def body(d_ref, dp_ref, dn_ref, ccu_ref, hp_ref, hn_ref, cw_ref, dccu_ref, dcw_ref):
        i = pl.program_id(0)

        @pl.when(i == 0)
        def _():
            dcw_ref[...] = jnp.zeros(dcw_ref.shape, F32)

        cc, cu = ccu_ref[:, 0:CONV_W], ccu_ref[:, CONV_W:]
        z, z_up, z_dn = _conv_shifts(cc, cu, hp_ref, hn_ref, i, nt, tm)
        d = d_ref[...]
        dprev = jnp.where(i == 0, 0.0, dp_ref[7:8, :])
        dnext = jnp.where(i == nt - 1, 0.0, dn_ref[0:1, :])
        row = lax.broadcasted_iota(jnp.int32, d.shape, 0)
        d_up = jnp.where(row == 0, dprev, pltpu.roll(d, 1, 0))
        d_dn = jnp.where(row == tm - 1, dnext, pltpu.roll(d, tm - 1, 0))
        dz = cw_ref[0:1, :] * d_dn + cw_ref[1:2, :] * d + cw_ref[2:3, :] * d_up
        dccu_ref[:, 0:CONV_W] = (dz * cu).astype(BF16)
        dccu_ref[:, CONV_W:] = (dz * cc).astype(BF16)
        dcw_ref[0:1, :] += _colsum(d * z_up)
        dcw_ref[1:2, :] += _colsum(d * z)
        dcw_ref[2:3, :] += _colsum(d * z_dn)
        dcw_ref[3:4, :] += _colsum(d)

    hp, hn = _halo_specs(tm, S, 1024, 6)
    dp, dn = _halo_specs(tm, S, CONV_W, 0)
    return pl.pallas_call(
        body, name="conv_bwd", grid=(nt,),
        in_specs=[pl.BlockSpec((tm, CONV_W), lambda i: (i, 0)), dp, dn,
                  pl.BlockSpec((tm, 1024), lambda i: (i, 6)), hp, hn,
                  pl.BlockSpec((8, CONV_W), lambda i: (0, 0))],
        out_specs=[pl.BlockSpec((tm, 1024), lambda i: (i, 0)), pl.BlockSpec((8, CONV_W), lambda i: (0, 0))],
        out_shape=[jax.ShapeDtypeStruct((S, 1024), BF16), jax.ShapeDtypeStruct((8, CONV_W), F32)],
        compiler_params=_params(("arbitrary",)),
    )(dcv, dcv, dcv, proj, proj, proj, convw)


def _attn_bwd_dq(q, k, v, do, lse, delta):
    S = q.shape[0]
    tq, tk = _tile(S, 512), _tile(S, 512)
    nk = S // tk

    def body(q_ref, k_ref, v_ref, do_ref, lse_ref, delta_ref, dq_ref, acc_s):
        acc_s[...] = jnp.zeros(acc_s.shape, F32)
        qv, dov = q_ref[...], do_ref[...]
        lse_c, delta_c = lse_ref[0], delta_ref[0]

        def step(c, carry):
            rows = pl.ds(pl.multiple_of(c * tk, tk), tk)
            kv = k_ref[rows, :]
            p = jnp.exp(_dot_nt(qv, kv) - lse_c)
            dp = _dot_nt(dov, v_ref[rows, :])
            ds = (p * (dp - delta_c)).astype(BF16)
            acc_s[...] += _dot(ds, kv)
            return carry

        lax.fori_loop(0, nk, step, 0)
        dq_ref[...] = acc_s[...]

    col = pl.BlockSpec((1, tq, 1), lambda h, i: (h, i, 0))
    return pl.pallas_call(
        body, name="attn_bwd_dq", grid=(N_HEADS, S // tq),
        in_specs=[pl.BlockSpec((tq, HP), lambda h, i: (i, h)),
                  pl.BlockSpec((S, HP), lambda h, i: (0, h)),
                  pl.BlockSpec((S, HP), lambda h, i: (0, h)),
                  pl.BlockSpec((tq, HP), lambda h, i: (i, h)), col, col],
        out_specs=pl.BlockSpec((tq, HP), lambda h, i: (i, h)),
        out_shape=jax.ShapeDtypeStruct((S, N_HEADS * HP), F32),
        scratch_shapes=[pltpu.VMEM((tq, HP), F32)],
        compiler_params=_params(("parallel", "parallel")),
    )(q, k, v, do, lse, delta)


def _attn_bwd_dkv(q, k, v, do, lse_r, delta_r):
    S = q.shape[0]
    tk = _tile(S, 512)
    nq, tq = lse_r.shape[1], lse_r.shape[3]

    def body(k_ref, v_ref, q_ref, do_ref, lse_ref, delta_ref, dk_ref, dv_ref, dk_s, dv_s):
        dk_s[...] = jnp.zeros(dk_s.shape, F32)
        dv_s[...] = jnp.zeros(dv_s.shape, F32)
        kv, vv = k_ref[...], v_ref[...]

        def step(c, carry):
            rows = pl.ds(pl.multiple_of(c * tq, tq), tq)
            qv, dov = q_ref[rows, :], do_ref[rows, :]
            pt = jnp.exp(_dot_nt(kv, qv) - lse_ref[0, c])
            dv_s[...] += _dot(pt.astype(BF16), dov)
            dpt = _dot_nt(vv, dov)
            dst = (pt * (dpt - delta_ref[0, c])).astype(BF16)
            dk_s[...] += _dot(dst, qv)
            return carry

        lax.fori_loop(0, nq, step, 0)
        dk_ref[...] = dk_s[...]
        dv_ref[...] = dv_s[...].astype(BF16)

    rowv = pl.BlockSpec((1, nq, 1, tq), lambda h, j: (h, 0, 0, 0))
    return pl.pallas_call(
        body, name="attn_bwd_dkv", grid=(N_HEADS, S // tk),
        in_specs=[pl.BlockSpec((tk, HP), lambda h, j: (j, h)),
                  pl.BlockSpec((tk, HP), lambda h, j: (j, h)),
                  pl.BlockSpec((S, HP), lambda h, j: (0, h)),
                  pl.BlockSpec((S, HP), lambda h, j: (0, h)), rowv, rowv],
        out_specs=[pl.BlockSpec((tk, HP), lambda h, j: (j, h)), pl.BlockSpec((tk, HP), lambda h, j: (j, h))],
        out_shape=[jax.ShapeDtypeStruct((S, N_HEADS * HP), F32), jax.ShapeDtypeStruct((S, N_HEADS * HP), BF16)],
        scratch_shapes=[pltpu.VMEM((tk, HP), F32), pltpu.VMEM((tk, HP), F32)],
        compiler_params=_params(("parallel", "parallel")),
    )(k, v, q, do, lse_r, delta_r)


def _mla_prep_bwd(proj, tabs, dq, dk, dv, qg, kvg, qhg, khg, wuq, wuk, wuqt, wukt, wuvt):
    S = proj.shape[0]
    tm = _tile(S, 256)

    def body(a_ref, c_ref, sa_ref, sb_ref, dq_ref, dk_ref, dv_ref, qg_ref, kvg_ref, qhg_ref, khg_ref,
             wuq_ref, wuk_ref, wuqt_ref, wukt_ref, wuvt_ref,
             da_ref, dqraw_ref, dkraw_ref, dqg_ref, dkvg_ref, dqhg_ref, dkhg_ref):
        @pl.when(pl.program_id(0) == 0)
        def _():
            dqg_ref[...] = jnp.zeros(dqg_ref.shape, F32)
            dkvg_ref[...] = jnp.zeros(dkvg_ref.shape, F32)
            dqhg_ref[...] = jnp.zeros(dqhg_ref.shape, F32)
            dkhg_ref[...] = jnp.zeros(dkhg_ref.shape, F32)

        ql = a_ref[:, 0:Q_RANK]
        kvl = a_ref[:, Q_RANK:Q_RANK + KV_RANK]
        kpe = a_ref[:, Q_RANK + KV_RANK:Q_RANK + KV_RANK + HP]
        rq, qhat = _rms_parts(ql)
        rkv, kvhat = _rms_parts(kvl)
        qraw = _dot((qhat * qg_ref[...]).astype(BF16), wuq_ref[...])
        kn = _dot((kvhat * kvg_ref[...]).astype(BF16), wuk_ref[...])
        c, sa, sb = c_ref[...], sa_ref[...], sb_ref[...]
        dkpe = jnp.zeros(kpe.shape, F32)
        dqhg = jnp.zeros((1, HP), F32)
        dkhg = jnp.zeros((1, HP), F32)
        for h in range(N_HEADS):
            sl = slice(h * HP, (h + 1) * HP)
            r, that = _rms_parts(qraw[:, sl], QK_DIM)
            dtn = _rope_t(dq_ref[:, sl], c, sa, sb) * SCALE
            dqhg = dqhg + _colsum(dtn * that)
            dqraw_ref[:, sl] = _rms_bwd(dtn * qhg_ref[...], that, r, QK_DIM).astype(BF16)
            r, that = _rms_parts(kn[:, sl] + kpe, QK_DIM)
            dtn = _rope_t(dk_ref[:, sl], c, sa, sb)
            dkhg = dkhg + _colsum(dtn * that)
            dkr = _rms_bwd(dtn * khg_ref[...], that, r, QK_DIM)
            dkraw_ref[:, sl] = dkr.astype(BF16)
            dkpe = dkpe + dkr
        dqhg_ref[...] += dqhg
        dkhg_ref[...] += dkhg
        dqn = _dot(dqraw_ref[...], wuqt_ref[...])
        dqg_ref[...] += _colsum(dqn * qhat)
        da_ref[:, 0:Q_RANK] = _rms_bwd(dqn * qg_ref[...], qhat, rq, Q_RANK).astype(BF16)
        dkvn = _dot(dkraw_ref[...], wukt_ref[...]) + _dot(dv_ref[...], wuvt_ref[...])
        dkvg_ref[...] += _colsum(dkvn * kvhat)
        da_ref[:, Q_RANK:Q_RANK + KV_RANK] = _rms_bwd(dkvn * kvg_ref[...], kvhat, rkv, KV_RANK).astype(BF16)
        da_ref[:, Q_RANK + KV_RANK:Q_RANK + KV_RANK + HP] = dkpe.astype(BF16)
        da_ref[:, Q_RANK + KV_RANK + HP:] = jnp.zeros((tm, 1024 - Q_RANK - KV_RANK - HP), BF16)

    row = lambda w: pl.BlockSpec((tm, w), lambda i: (i, 0))
    full = lambda a: pl.BlockSpec(a.shape, lambda i: (0,) * a.ndim)
    acc = lambda c: pl.BlockSpec((1, c), lambda i: (0, 0))
    return pl.pallas_call(
        body, name="mla_prep_bwd", grid=(S // tm,),
        in_specs=[pl.BlockSpec((tm, 1024), lambda i: (i, 7)), row(HP), row(HP), row(HP),
                  row(1024), row(1024), row(1024), full(qg), full(kvg), full(qhg), full(khg),
                  full(wuq), full(wuk), full(wuqt), full(wukt), full(wuvt)],
        out_specs=[row(1024), row(1024), row(1024), acc(Q_RANK), acc(KV_RANK), acc(HP), acc(HP)],
        out_shape=[jax.ShapeDtypeStruct((S, 1024), BF16)] * 3
        + [jax.ShapeDtypeStruct((1, Q_RANK), F32), jax.ShapeDtypeStruct((1, KV_RANK), F32),
           jax.ShapeDtypeStruct((1, HP), F32), jax.ShapeDtypeStruct((1, HP), F32)],
        compiler_params=_params(("arbitrary",)),
    )(proj, *tabs, dq, dk, dv, qg, kvg, qhg, khg, wuq, wuk, wuqt, wukt, wuvt)


def _inproj_bwd(dproj, wint, x, g, ng):
    S = x.shape[0]
    tm, tk = _tile(S, 512), 2048
    nk = PW // tk

    def body(dp_ref, w_ref, x_ref, g_ref, ng_ref, dx_ref, dng_ref, acc_s):
        i, k = pl.program_id(0), pl.program_id(1)

        @pl.when((i == 0) & (k == 0))
        def _():
            dng_ref[...] = jnp.zeros(dng_ref.shape, F32)

        @pl.when(k == 0)
        def _():
            acc_s[...] = jnp.zeros(acc_s.shape, F32)

        acc_s[...] += _dot(dp_ref[...], w_ref[...])

        @pl.when(k == nk - 1)
        def _():
            r, xhat = _rms_parts(x_ref[...])
            dh = acc_s[...]
            dng_ref[...] += _colsum(dh * xhat)
            dx_ref[...] = g_ref[...] + _rms_bwd(dh * ng_ref[...], xhat, r, D_MODEL)

    return pl.pallas_call(
        body, name="inproj_bwd", grid=(S // tm, nk),
        in_specs=[pl.BlockSpec((tm, tk), lambda i, k: (i, k)),
                  pl.BlockSpec((tk, D_MODEL), lambda i, k: (k, 0)),
                  pl.BlockSpec((tm, D_MODEL), lambda i, k: (i, 0)),
                  pl.BlockSpec((tm, D_MODEL), lambda i, k: (i, 0)),
                  pl.BlockSpec((1, D_MODEL), lambda i, k: (0, 0))],
        out_specs=[pl.BlockSpec((tm, D_MODEL), lambda i, k: (i, 0)),
                   pl.BlockSpec((1, D_MODEL), lambda i, k: (0, 0))],
        out_shape=[jax.ShapeDtypeStruct((S, D_MODEL), F32), jax.ShapeDtypeStruct((1, D_MODEL), F32)],
        scratch_shapes=[pltpu.VMEM((tm, D_MODEL), F32)],
        compiler_params=_params(("arbitrary", "arbitrary")),
    )(dproj, wint, x, g, ng)


def _memkv_bwd(mem, mng, mkg, wmkv, wmkvt, dmk, dmv):
    M = mem.shape[0]

    def body(mem_ref, mng_ref, mkg_ref, w_ref, wt_ref, dmk_ref, dmv_ref, dw_ref, dmng_ref, dmkg_ref, d_s):
        r, mhat = _rms_parts(mem_ref[...])
        mn = (mhat * mng_ref[...]).astype(BF16)
        mkv = _dot(mn, w_ref[...])
        dmkg = jnp.zeros((1, MEM_HD), F32)
        for h in range(MEM_HEADS):
            sl = slice(h * MEM_HD, (h + 1) * MEM_HD)
            rk, khat = _rms_parts(mkv[:, 2 * MEM_HD * h:2 * MEM_HD * h + MEM_HD])
            dkn = dmk_ref[:, sl]
            dmkg = dmkg + _colsum(dkn * khat)
            d_s[:, 2 * MEM_HD * h:2 * MEM_HD * h + MEM_HD] = _rms_bwd(dkn * mkg_ref[...], khat, rk, MEM_HD).astype(BF16)
            d_s[:, 2 * MEM_HD * h + MEM_HD:2 * MEM_HD * (h + 1)] = dmv_ref[:, sl].astype(BF16)
        dmkg_ref[...] = dmkg
        dw_ref[...] = _dot_tn(mn, d_s[...])
        dmn = _dot(d_s[...], wt_ref[...])
        dmng_ref[...] = _colsum(dmn * mhat)

    return pl.pallas_call(
        body, name="memkv_bwd",
        out_shape=[jax.ShapeDtypeStruct((D_MODEL, 2 * MEM_W), F32), jax.ShapeDtypeStruct((1, D_MODEL), F32),
                   jax.ShapeDtypeStruct((1, MEM_HD), F32)],
        scratch_shapes=[pltpu.VMEM((M, 2 * MEM_W), BF16)],
        compiler_params=pltpu.CompilerParams(vmem_limit_bytes=VMEM_LIMIT_V7X),
    )(mem, mng, mkg, wmkv, wmkvt, dmk, dmv)


def _mm_tn(a, b, name):
    S, M = a.shape
    N = b.shape[1]
    tm, tn, ts = _tile(M, 1024), _tile(N, 1024), _tile(S, 512)

    def body(a_ref, b_ref, o_ref):
        @pl.when(pl.program_id(2) == 0)
        def _():
            o_ref[...] = jnp.zeros(o_ref.shape, F32)

        o_ref[...] += _dot_tn(a_ref[...].astype(BF16), b_ref[...].astype(BF16))

    return pl.pallas_call(
        body, name=name, grid=(M // tm, N // tn, S // ts),
        in_specs=[pl.BlockSpec((ts, tm), lambda i, j, k: (k, i)),
                  pl.BlockSpec((ts, tn), lambda i, j, k: (k, j))],
        out_specs=pl.BlockSpec((tm, tn), lambda i, j, k: (i, j)),
        out_shape=jax.ShapeDtypeStruct((M, N), F32),
        compiler_params=_params(("parallel", "parallel", "arbitrary")),
    )(a, b)


def _adamw(w, g0, g1, m, v, name):
    R, C = w.shape
    tr = R
    for cand in (512, 256, 128, 64, 32, 16, 8):
        if R % cand == 0 and cand * C * 4 <= (1 << 20):
            tr = cand
            break
    c1 = 1.0 / (1.0 - ADAM_B1 ** ADAM_STEP)
    c2 = 1.0 / (1.0 - ADAM_B2 ** ADAM_STEP)

    def body(w_ref, g0_ref, g1_ref, m_ref, v_ref, g_ref, d_ref, nm_ref, nv_ref):
        g = g0_ref[...] + g1_ref[...]
        nm = ADAM_B1 * m_ref[...] + (1.0 - ADAM_B1) * g
        nv = ADAM_B2 * v_ref[...] + (1.0 - ADAM_B2) * (g * g)
        g_ref[...] = g
        nm_ref[...] = nm
        nv_ref[...] = nv
        d_ref[...] = -ADAM_LR * ((nm * c1) / (jnp.sqrt(nv * c2) + ADAM_EPS) + ADAM_WD * w_ref[...])

    blk = pl.BlockSpec((tr, C), lambda i: (i, 0))
    return pl.pallas_call(
        body, name=name, grid=(R // tr,),
        in_specs=[blk] * 5, out_specs=[blk] * 4,
        out_shape=[jax.ShapeDtypeStruct((R, C), F32)] * 4,
        compiler_params=_params(("parallel",)),
    )(w, g0, g1, m, v)


def _sum_slabs(a, name):
    K, R, C = a.shape
    tr = R
    for cand in (512, 256, 128, 64, 32, 16, 8):
        if R % cand == 0 and cand * C * 4 * K <= (4 << 20):
            tr = cand
            break

    def body(a_ref, o_ref):
        t = a_ref[0]
        for k in range(1, K):
            t = t + a_ref[k]
        o_ref[...] = t

    return pl.pallas_call(
        body, name=name, grid=(R // tr,),
        in_specs=[pl.BlockSpec((K, tr, C), lambda i: (0, i, 0))],
        out_specs=pl.BlockSpec((tr, C), lambda i: (i, 0)),
        out_shape=jax.ShapeDtypeStruct((R, C), F32),
        compiler_params=_params(("parallel",)),
    )(a)


MESH = pl.DeviceIdType.MESH
HBM = pl.BlockSpec(memory_space=pltpu.HBM)


def _chip_peer(x, y, c, k):
    return (x ^ (k >> 1), y ^ (k & 1), c)


def _gather_chips(arrs):
    n = len(arrs)

    def body(*refs):
        ins, outs = refs[:n], refs[n:2 * n]
        send, recv, loc = refs[2 * n:]
        x, y, c = lax.axis_index("x"), lax.axis_index("y"), lax.axis_index("c")
        me = 2 * x + y
        copies = []
        for a in range(n):
            own = pltpu.make_async_copy(ins[a], outs[a].at[me], loc.at[a])
            own.start()
            copies.append(own)
        for a in range(n):
            for k in (1, 2, 3):
                cp = pltpu.make_async_remote_copy(
                    src_ref=ins[a], dst_ref=outs[a].at[me], send_sem=send.at[3 * a + k - 1],
                    recv_sem=recv.at[3 * a + k - 1], device_id=_chip_peer(x, y, c, k), device_id_type=MESH)
                cp.start()
                copies.append(cp)
        for cp in copies:
            cp.wait()

    return pl.pallas_call(
        body, name="gather_weights",
        in_specs=[HBM] * n, out_specs=[HBM] * n,
        out_shape=[jax.ShapeDtypeStruct((4,) + a.shape, a.dtype) for a in arrs],
        scratch_shapes=[pltpu.SemaphoreType.DMA((3 * n,)), pltpu.SemaphoreType.DMA((3 * n,)),
                        pltpu.SemaphoreType.DMA((n,))],
    )(*arrs)


def _scatter_chips(arrs, small):
    n = len(arrs)

    def body(*refs):
        ins, small_in = refs[:n], refs[n]
        outs, small_out = refs[n + 1:2 * n + 1], refs[2 * n + 1]
        send, recv, loc, ssend, srecv = refs[2 * n + 2:]
        x, y, c = lax.axis_index("x"), lax.axis_index("y"), lax.axis_index("c")
        me = 2 * x + y
        me8 = 4 * x + 2 * y + c
        copies = []
        for a in range(n):
            own = pltpu.make_async_copy(ins[a].at[me], outs[a].at[me], loc.at[a])
            own.start()
            copies.append(own)
        own = pltpu.make_async_copy(small_in, small_out.at[me8], loc.at[n])
        own.start()
        copies.append(own)
        for k in range(1, 8):
            cp = pltpu.make_async_remote_copy(
                src_ref=small_in, dst_ref=small_out.at[me8], send_sem=ssend.at[k - 1], recv_sem=srecv.at[k - 1],
                device_id=(x ^ (k >> 2), y ^ ((k >> 1) & 1), c ^ (k & 1)), device_id_type=MESH)
            cp.start()
            copies.append(cp)
        for a in range(n):
            for k in (1, 2, 3):
                cp = pltpu.make_async_remote_copy(
                    src_ref=ins[a].at[me ^ k], dst_ref=outs[a].at[me], send_sem=send.at[3 * a + k - 1],
                    recv_sem=recv.at[3 * a + k - 1], device_id=_chip_peer(x, y, c, k), device_id_type=MESH)
                cp.start()
                copies.append(cp)
        for cp in copies:
            cp.wait()

    return pl.pallas_call(
        body, name="scatter_grads",
        in_specs=[HBM] * (n + 1), out_specs=[HBM] * (n + 1),
        out_shape=[jax.ShapeDtypeStruct(a.shape, a.dtype) for a in arrs]
        + [jax.ShapeDtypeStruct((8,) + small.shape, small.dtype)],
        scratch_shapes=[pltpu.SemaphoreType.DMA((3 * n,)), pltpu.SemaphoreType.DMA((3 * n,)),
                        pltpu.SemaphoreType.DMA((n + 1,)), pltpu.SemaphoreType.DMA((7,)),
                        pltpu.SemaphoreType.DMA((7,))],
    )(*arrs, small)


def _swap_cores(arrs):
    n = len(arrs)

    def body(*refs):
        ins, outs = refs[:n], refs[n:2 * n]
        send, recv = refs[2 * n:]
        x, y, c = lax.axis_index("x"), lax.axis_index("y"), lax.axis_index("c")
        copies = []
        for a in range(n):
            cp = pltpu.make_async_remote_copy(
                src_ref=ins[a], dst_ref=outs[a], send_sem=send.at[a], recv_sem=recv.at[a],
                device_id=(x, y, 1 - c), device_id_type=MESH)
            cp.start()
            copies.append(cp)
        for cp in copies:
            cp.wait()

    return pl.pallas_call(
        body, name="swap_cores",
        in_specs=[HBM] * n, out_specs=[HBM] * n,
        out_shape=[jax.ShapeDtypeStruct(a.shape, a.dtype) for a in arrs],
        scratch_shapes=[pltpu.SemaphoreType.DMA((n,)), pltpu.SemaphoreType.DMA((n,))],
    )(*arrs)


def _pad_last(a, n):
    return jnp.pad(a, [(0, 0)] * (a.ndim - 1) + [(0, n - a.shape[-1])])


def _pad_w_in(w):
    lead = w.shape[:-1]
    seg = lambda a, b: w[..., a:b]
    ga = _pad_last(seg(2720, 3232).reshape(lead + (N_HEADS, V_DIM)), HP).reshape(lead + (1024,))
    kpe = jnp.pad(seg(640, 672), [(0, 0)] * len(lead) + [(NOPE_DIM, HP - QK_DIM)])
    zero = jnp.zeros(lead + (PW - 7936,), w.dtype)
    return jnp.concatenate(
        [seg(4256, 7328), ga, seg(672, 1184), seg(2208, 2720), seg(3232, 3744), seg(3744, 4256),
         seg(1184, 1696), seg(1696, 2208), seg(0, 384), seg(384, 640), kpe, zero], axis=-1)


def _unpad_w_in(w):
    lead = w.shape[:-1]
    seg = lambda a, n: w[..., a:a + n]
    ga = seg(O_GA, 1024).reshape(lead + (N_HEADS, HP))[..., :V_DIM].reshape(lead + (N_HEADS * V_DIM,))
    return jnp.concatenate(
        [seg(O_QL, 384), seg(O_KVL, 256), seg(O_KPE + NOPE_DIM, ROPE_DIM), seg(O_CB, 512), seg(O_CC, 512),
         seg(O_CU, 512), seg(O_QM, 512), ga, seg(O_GC, 512), seg(O_GM, 512), seg(O_R, 3072)], axis=-1)


def _cols_from_shards(g):
    _, L, R, C = g.shape
    return jnp.transpose(g, (1, 2, 0, 3)).reshape(L, R, 4 * C)


def _cols_to_shards(w):
    L, R, C4 = w.shape
    return jnp.transpose(w.reshape(L, R, 4, C4 // 4), (2, 0, 1, 3))


def _rows_from_shards(g):
    _, L, R, C = g.shape
    return jnp.transpose(g, (1, 0, 2, 3)).reshape(L, 4 * R, C)


def _rows_to_shards(w):
    L, R4, C = w.shape
    return jnp.transpose(w.reshape(L, 4, R4 // 4, C), (1, 0, 2, 3))


def _t(w):
    return jnp.swapaxes(w, -1, -2)


def _layer_fwd(x, mem, tabs, p):
    proj, xn = _inproj(x, p["norm_g"], p["w_in"])
    q, k, v, qn, kvn = _mla_prep(proj, tabs, p["q_norm_g"], p["kv_norm_g"], p["q_head_g"], p["k_head_g"],
                                 p["w_uq"], p["w_uk"], p["w_uv"])
    o, lse = _attn_fwd(q, k, v)
    mk, mv = _memkv(mem, p["mem_norm_g"], p["mem_k_g"], p["w_mkv"])
    x_new, oa, oc, om, u, y = _merge(proj, o, x, p["b_gate"], p["conv_wb"], p["mem_q_g"], mk, mv,
                                     p["w_br_attn"], p["w_br_conv"], p["w_br_mem"], p["w_out"])
    saved = dict(x=x, proj=proj, xn=xn, q=q, k=k, v=v, qn=qn, kvn=kvn, o=o, lse=lse, mk=mk, mv=mv,
                 oa=oa, oc=oc, om=om, u=u, y=y)
    return x_new, saved


def _layer_bwd(g, mem, tabs, p, s):
    S = g.shape[0]
    tq = _tile(S, 512)
    dmain, dcv, d_o, delta, du, dbg, dmk, dmv, dmqg = _merge_bwd(
        g, s["proj"], s["o"], s["u"], p["b_gate"], p["conv_wb"], p["mem_q_g"], s["mk"], s["mv"],
        p["w_out_t"], p["w_br_attn_t"], p["w_br_conv_t"], p["w_br_mem_t"])
    dccu, dconv = _conv_bwd(dcv, s["proj"], p["conv_wb"])
    dq = _attn_bwd_dq(s["q"], s["k"], s["v"], d_o, s["lse"], delta)
    as_rows = lambda t: t.reshape(N_HEADS, S // tq, 1, tq)
    dk, dv = _attn_bwd_dkv(s["q"], s["k"], s["v"], d_o, as_rows(s["lse"]), as_rows(delta))
    dsega, dqraw, dkraw, dqg, dkvg, dqhg, dkhg = _mla_prep_bwd(
        s["proj"], tabs, dq, dk, dv, p["q_norm_g"], p["kv_norm_g"], p["q_head_g"], p["k_head_g"],
        p["w_uq"], p["w_uk"], p["w_uq_t"], p["w_uk_t"], p["w_uv_t"])
    dproj = jnp.concatenate([dmain, dccu, dsega], axis=1)
    dx, dng = _inproj_bwd(dproj, p["w_in_t"], s["x"], g, p["norm_g"])
    dwmkv, dmng, dmkg = _memkv_bwd(mem, p["mem_norm_g"], p["mem_k_g"], p["w_mkv"], p["w_mkv_t"], dmk, dmv)
    grads = dict(
        norm_g=dng, b_gate=dbg, q_norm_g=dqg, kv_norm_g=dkvg, q_head_g=dqhg, k_head_g=dkhg,
        conv_wb=dconv, mem_norm_g=dmng, mem_q_g=dmqg, mem_k_g=dmkg, w_mkv=dwmkv,
        w_in=_mm_tn(s["xn"], dproj, "grad_w_in"),
        w_uq=_mm_tn(s["qn"], dqraw, "grad_w_uq"),
        w_uk=_mm_tn(s["kvn"], dkraw, "grad_w_uk"),
        w_uv=_mm_tn(s["kvn"], dv, "grad_w_uv"),
        w_br_attn=_mm_tn(s["oa"], du[:, 0:1024], "grad_w_br_attn"),
        w_br_conv=_mm_tn(s["oc"], du[:, 1024:2048], "grad_w_br_conv"),
        w_br_mem=_mm_tn(s["om"], du[:, 2048:3072], "grad_w_br_mem"),
        w_out=_mm_tn(s["y"], g, "grad_w_out"),
    )
    return dx, grads


def _layer_params(full, l):
    p = {}
    w_in = _pad_w_in(full["w_in"][l])
    w_uq = _pad_last(full["w_uq"][l].reshape(Q_RANK, N_HEADS, QK_DIM), HP).reshape(Q_RANK, 1024)
    ukv = full["w_ukv"][l].reshape(KV_RANK, N_HEADS, NOPE_DIM + V_DIM)
    w_uk = _pad_last(ukv[..., :NOPE_DIM], HP).reshape(KV_RANK, 1024)
    w_uv = _pad_last(ukv[..., NOPE_DIM:], HP).reshape(KV_RANK, 1024)
    w_ba = jnp.pad(full["w_br_attn"][l].reshape(N_HEADS, V_DIM, D_MODEL), ((0, 0), (0, HP - V_DIM), (0, 0)))
    w_ba = w_ba.reshape(1024, D_MODEL)
    p.update(w_in=w_in, w_uq=w_uq, w_uk=w_uk, w_uv=w_uv, w_br_attn=w_ba, w_br_conv=full["w_br_conv"][l],
             w_br_mem=full["w_br_mem"][l], w_out=full["w_out"][l], w_mkv=full["w_mkv"][l])
    for n in ("w_in", "w_uq", "w_uk", "w_uv", "w_br_attn", "w_br_conv", "w_br_mem", "w_out", "w_mkv"):
        p[n + "_t"] = _t(p[n])
    for n in ("norm_g", "b_gate", "q_norm_g", "kv_norm_g", "mem_norm_g", "mem_q_g", "mem_k_g"):
        p[n] = full[n][l][None, :]
    p["q_head_g"] = _pad_last(full["q_head_g"][l][None, :], HP)
    p["k_head_g"] = _pad_last(full["k_head_g"][l][None, :], HP)
    p["conv_wb"] = jnp.concatenate(
        [full["conv_w"][l], full["conv_b"][l][None, :], jnp.zeros((4, CONV_W), F32)], axis=0)
    return p


def _local_step(x, mem, positions, full, target):
    S = x.shape[0]
    invf16 = ROPE_BASE ** (-jnp.arange(0, ROPE_DIM, 2, dtype=F32) / ROPE_DIM)
    invf = jnp.concatenate([jnp.zeros((NOPE_DIM,), F32), invf16, invf16, jnp.zeros((HP - QK_DIM,), F32)])[None, :]
    tabs = _rope_tables(jnp.broadcast_to(positions.reshape(S, 1), (S, HP)), invf)
    params = [_layer_params(full, l) for l in range(DEPTH)]
    saved = []
    h = x
    for l in range(DEPTH):
        h, s = _layer_fwd(h, mem, tabs, params[l])
        saved.append(s)
    g, loss_part = _loss_head(h, target)
    per_layer = [None] * DEPTH
    for l in reversed(range(DEPTH)):
        g, per_layer[l] = _layer_bwd(g, mem, tabs, params[l], saved[l])
    st = lambda n: jnp.stack([per_layer[l][n] for l in range(DEPTH)])
    out = {}
    out["w_in"] = _unpad_w_in(st("w_in"))
    out["w_uq"] = st("w_uq").reshape(DEPTH, Q_RANK, N_HEADS, HP)[..., :QK_DIM].reshape(DEPTH, Q_RANK, N_HEADS * QK_DIM)
    duk = st("w_uk").reshape(DEPTH, KV_RANK, N_HEADS, HP)[..., :NOPE_DIM]
    duv = st("w_uv").reshape(DEPTH, KV_RANK, N_HEADS, HP)[..., :V_DIM]
    out["w_ukv"] = jnp.concatenate([duk, duv], axis=-1).reshape(DEPTH, KV_RANK, 1024)
    out["w_br_attn"] = st("w_br_attn").reshape(DEPTH, N_HEADS, HP, D_MODEL)[:, :, :V_DIM].reshape(DEPTH, 512, D_MODEL)
    for n in ("w_br_conv", "w_br_mem", "w_out", "w_mkv"):
        out[n] = st(n)
    for n in ("norm_g", "b_gate", "q_norm_g", "kv_norm_g", "mem_norm_g", "mem_q_g", "mem_k_g"):
        out[n] = st(n)[:, 0, :]
    out["q_head_g"] = st("q_head_g")[:, 0, :QK_DIM]
    out["k_head_g"] = st("k_head_g")[:, 0, :QK_DIM]
    cwb = st("conv_wb")
    out["conv_w"] = cwb[:, 0:3, :]
    out["conv_b"] = cwb[:, 3, :]
    return loss_part, g, out


_COL_SHARDED = ("w_in", "w_uq", "w_ukv", "w_br_attn", "w_br_conv", "w_br_mem")
_ROW_SHARDED = ("w_mkv", "w_out")
_BIG = _COL_SHARDED + _ROW_SHARDED
_SMALL = ("norm_g", "b_gate", "q_norm_g", "kv_norm_g", "q_head_g", "k_head_g", "conv_w", "conv_b",
          "mem_norm_g", "mem_q_g", "mem_k_g")
_ORDER = ("norm_g", "w_in", "b_gate", "q_norm_g", "w_uq", "kv_norm_g", "w_ukv", "q_head_g", "k_head_g",
          "conv_w", "conv_b", "mem_norm_g", "w_mkv", "mem_q_g", "mem_k_g", "w_br_attn", "w_br_conv",
          "w_br_mem", "w_out")


def _pack_small(d, extra):
    flat = jnp.concatenate([d[n].reshape(-1) for n in _SMALL] + [extra.reshape(-1)])
    n = flat.shape[0]
    rows = -(-n // 1024) * 8
    return jnp.pad(flat, (0, rows * 128 - n)).reshape(rows, 128)


def _unpack_small(packed, like):
    flat = packed.reshape(-1)
    out, off = {}, 0
    for n in _SMALL:
        sz = int(np.prod(like[n].shape))
        out[n] = flat[off:off + sz].reshape(like[n].shape)
        off += sz
    return out, flat[off:]


def kernel(x, mem, positions, norm_g, w_in, b_gate, q_norm_g, w_uq, kv_norm_g, w_ukv, q_head_g, k_head_g, conv_w, conv_b, mem_norm_g, w_mkv, mem_q_g, mem_k_g, w_br_attn, w_br_conv, w_br_mem, w_out, loss_target, m_norm_g, m_w_in, m_b_gate, m_q_norm_g, m_w_uq, m_kv_norm_g, m_w_ukv, m_q_head_g, m_k_head_g, m_conv_w, m_conv_b, m_mem_norm_g, m_w_mkv, m_mem_q_g, m_mem_k_g, m_w_br_attn, m_w_br_conv, m_w_br_mem, m_w_out, v_norm_g, v_w_in, v_b_gate, v_q_norm_g, v_w_uq, v_kv_norm_g, v_w_ukv, v_q_head_g, v_k_head_g, v_conv_w, v_conv_b, v_mem_norm_g, v_w_mkv, v_mem_q_g, v_mem_k_g, v_w_br_attn, v_w_br_conv, v_w_br_mem, v_w_out):
    w = dict(norm_g=norm_g, w_in=w_in, b_gate=b_gate, q_norm_g=q_norm_g, w_uq=w_uq, kv_norm_g=kv_norm_g,
             w_ukv=w_ukv, q_head_g=q_head_g, k_head_g=k_head_g, conv_w=conv_w, conv_b=conv_b,
             mem_norm_g=mem_norm_g, w_mkv=w_mkv, mem_q_g=mem_q_g, mem_k_g=mem_k_g, w_br_attn=w_br_attn,
             w_br_conv=w_br_conv, w_br_mem=w_br_mem, w_out=w_out)
    m = dict(norm_g=m_norm_g, w_in=m_w_in, b_gate=m_b_gate, q_norm_g=m_q_norm_g, w_uq=m_w_uq,
             kv_norm_g=m_kv_norm_g, w_ukv=m_w_ukv, q_head_g=m_q_head_g, k_head_g=m_k_head_g, conv_w=m_conv_w,
             conv_b=m_conv_b, mem_norm_g=m_mem_norm_g, w_mkv=m_w_mkv, mem_q_g=m_mem_q_g, mem_k_g=m_mem_k_g,
             w_br_attn=m_w_br_attn, w_br_conv=m_w_br_conv, w_br_mem=m_w_br_mem, w_out=m_w_out)
    v = dict(norm_g=v_norm_g, w_in=v_w_in, b_gate=v_b_gate, q_norm_g=v_q_norm_g, w_uq=v_w_uq,
             kv_norm_g=v_kv_norm_g, w_ukv=v_w_ukv, q_head_g=v_q_head_g, k_head_g=v_k_head_g, conv_w=v_conv_w,
             conv_b=v_conv_b, mem_norm_g=v_mem_norm_g, w_mkv=v_w_mkv, mem_q_g=v_mem_q_g, mem_k_g=v_mem_k_g,
             w_br_attn=v_w_br_attn, w_br_conv=v_w_br_conv, w_br_mem=v_w_br_mem, w_out=v_w_out)
    chip = 2 * lax.axis_index("x") + lax.axis_index("y")

    gathered = _gather_chips([w[n].astype(BF16) for n in _BIG] + [w["conv_w"]])
    full = {n: w[n] for n in _SMALL}
    for n, gth in zip(_BIG, gathered):
        full[n] = _cols_from_shards(gth) if n in _COL_SHARDED else _rows_from_shards(gth)
    full["conv_w"] = _cols_from_shards(gathered[-1])

    loss_part, grad_x, grads = _local_step(x[0], mem[0], positions[0], full, loss_target[0])

    to_owner = [_cols_to_shards(grads[n]) if n in _COL_SHARDED else _rows_to_shards(grads[n]) for n in _BIG]
    loss_vec = jnp.zeros((128,), F32).at[0].set(0.5 / D_MODEL * jnp.sum(loss_part))
    small = _pack_small(grads, loss_vec)
    scattered = _scatter_chips(to_owner, small)
    small_sum = _sum_slabs(scattered[-1], "sum_small")
    partial = []
    for n, slabs in zip(_BIG, scattered[:-1]):
        sh = slabs.shape
        partial.append(_sum_slabs(slabs.reshape(4, sh[1] * sh[2], sh[3]), "sum_" + n))
    other = _swap_cores(partial)

    small_g, tail = _unpack_small(small_sum, {n: (grads[n]) for n in _SMALL})
    loss = tail[0]
    small_g["conv_w"] = lax.dynamic_slice_in_dim(small_g["conv_w"], chip * (CONV_W // 4), CONV_W // 4, axis=2)

    outs_g, outs_d, outs_m, outs_v = {}, {}, {}, {}
    for n, p0, p1 in zip(_BIG, partial, other):
        shape = w[n].shape
        flat = lambda t: t.reshape(p0.shape)
        g_, d_, m_, v_ = _adamw(flat(w[n]), p0, p1, flat(m[n]), flat(v[n]), "adamw_" + n)
        outs_g[n], outs_d[n], outs_m[n], outs_v[n] = (t.reshape(shape) for t in (g_, d_, m_, v_))
    zero_small = jnp.zeros_like(small_sum)
    pk = lambda d: _pack_small(d, jnp.zeros((128,), F32))
    g_, d_, m_, v_ = _adamw(pk(w), _pack_small(small_g, jnp.zeros((128,), F32)), zero_small, pk(m), pk(v),
                            "adamw_small")
    like = {n: w[n] for n in _SMALL}
    for dst, packed in ((outs_g, g_), (outs_d, d_), (outs_m, m_), (outs_v, v_)):
        dst.update(_unpack_small(packed, like)[0])

    return (loss, grad_x[None], *[outs_g[n] for n in _ORDER], *[outs_d[n] for n in _ORDER],
            *[outs_m[n] for n in _ORDER], *[outs_v[n] for n in _ORDER])
```

```python
import functools

import numpy as np
import jax
import jax.numpy as jnp
from jax import lax
from jax.experimental import pallas as pl
from jax.experimental.pallas import tpu as pltpu

F32 = jnp.float32
BF16 = jnp.bfloat16

D_MODEL = 1024
DEPTH = 4
N_HEADS = 8
QK_DIM = 96
NOPE_DIM = 64
ROPE_DIM = 32
V_DIM = 64
Q_RANK = 384
KV_RANK = 256
CONV_W = 512
MEM_HEADS = 4
MEM_HD = 128
MEM_W = 512
IN_WIDTH = 7328
PW = 8192
HP = 128
EPS = 1e-6
ROPE_BASE = 10000.0
SCALE = QK_DIM ** -0.5
MEM_SCALE = MEM_HD ** -0.5

ADAM_LR = 0.001
ADAM_B1 = 0.9
ADAM_B2 = 0.999
ADAM_EPS = 1e-08
ADAM_WD = 0.01
ADAM_STEP = 10

VMEM_LIMIT_V7X = 56 * 1024 * 1024

O_R, O_GA, O_CB, O_QM, O_GC, O_GM, O_CC, O_CU, O_QL, O_KVL, O_KPE = (
    0, 3072, 4096, 4608, 5120, 5632, 6144, 6656, 7168, 7552, 7808)


def _params(sem, vmem=VMEM_LIMIT_V7X):
    return pltpu.CompilerParams(dimension_semantics=sem, vmem_limit_bytes=vmem)


def _sigmoid(t):
    return 1.0 / (1.0 + jnp.exp(-t))


def _silu_and_grad(g):
    sg = _sigmoid(g)
    return g * sg, sg * (1.0 + g * (1.0 - sg))


def _rms(t, g, n=None):
    n = t.shape[-1] if n is None else n
    r = lax.rsqrt(jnp.sum(t * t, axis=-1, keepdims=True) * (1.0 / n) + EPS)
    return (t * r) * g


def _rms_parts(t, n=None):
    n = t.shape[-1] if n is None else n
    r = lax.rsqrt(jnp.sum(t * t, axis=-1, keepdims=True) * (1.0 / n) + EPS)
    return r, t * r


def _rms_bwd(dhat, hat, r, n):
    return r * (dhat - hat * (jnp.sum(dhat * hat, axis=-1, keepdims=True) * (1.0 / n)))


def _rope(t, c, sa, sb):
    return t * c + pltpu.roll(t, HP - 16, 1) * sa + pltpu.roll(t, 16, 1) * sb


def _rope_t(d, c, sa, sb):
    return d * c + pltpu.roll(d * sa, 16, 1) + pltpu.roll(d * sb, HP - 16, 1)


def _dot(a, b):
    return jnp.dot(a, b, preferred_element_type=F32)


def _dot_nt(a, b):
    return lax.dot_general(a, b, (((1,), (1,)), ((), ())), preferred_element_type=F32)


def _dot_tn(a, b):
    return lax.dot_general(a, b, (((0,), (0,)), ((), ())), preferred_element_type=F32)


def _colsum(t):
    return jnp.sum(t, axis=0, keepdims=True)


def _tile(n, t):
    t = min(n, t)
    assert n % t == 0, (n, t)
    return t


def _rope_tables(pos_b, invf):
    S = pos_b.shape[0]
    tm = _tile(S, 1024)

    def body(pos_ref, invf_ref, c_ref, sa_ref, sb_ref):
        ang = pos_ref[...].astype(F32) * invf_ref[...]
        lane = lax.broadcasted_iota(jnp.int32, ang.shape, 1)
        cs = jnp.cos(ang)
        sn = jnp.sin(ang)
        c_ref[...] = jnp.where(lane < NOPE_DIM, 1.0, jnp.where(lane < QK_DIM, cs, 0.0))
        sa_ref[...] = jnp.where((lane >= NOPE_DIM) & (lane < NOPE_DIM + 16), -sn, 0.0)
        sb_ref[...] = jnp.where((lane >= NOPE_DIM + 16) & (lane < QK_DIM), sn, 0.0)

    blk = pl.BlockSpec((tm, HP), lambda i: (i, 0))
    return pl.pallas_call(
        body, name="rope_tables", grid=(S // tm,),
        in_specs=[blk, pl.BlockSpec((1, HP), lambda i: (0, 0))],
        out_specs=[blk, blk, blk],
        out_shape=[jax.ShapeDtypeStruct((S, HP), F32)] * 3,
        compiler_params=_params(("parallel",)),
    )(pos_b, invf)


def _inproj(x, g, w):
    S = x.shape[0]
    tm, tn = _tile(S, 512), 2048

    def body(x_ref, g_ref, w_ref, proj_ref, xn_ref):
        @pl.when(pl.program_id(1) == 0)
        def _():
            xn_ref[...] = _rms(x_ref[...], g_ref[...]).astype(BF16)

        proj_ref[...] = _dot(xn_ref[...], w_ref[...])

    return pl.pallas_call(
        body, name="inproj", grid=(S // tm, PW // tn),
        in_specs=[pl.BlockSpec((tm, D_MODEL), lambda i, j: (i, 0)),
                  pl.BlockSpec((1, D_MODEL), lambda i, j: (0, 0)),
                  pl.BlockSpec((D_MODEL, tn), lambda i, j: (0, j))],
        out_specs=[pl.BlockSpec((tm, tn), lambda i, j: (i, j)),
                   pl.BlockSpec((tm, D_MODEL), lambda i, j: (i, 0))],
        out_shape=[jax.ShapeDtypeStruct((S, PW), F32), jax.ShapeDtypeStruct((S, D_MODEL), BF16)],
        compiler_params=_params(("parallel", "arbitrary")),
    )(x, g, w)


def _mla_prep(proj, tabs, qg, kvg, qhg, khg, wuq, wuk, wuv):
    S = proj.shape[0]
    tm = _tile(S, 512)

    def body(a_ref, c_ref, sa_ref, sb_ref, qg_ref, kvg_ref, qhg_ref, khg_ref, wuq_ref, wuk_ref, wuv_ref,
             q_ref, k_ref, v_ref, qn_ref, kvn_ref):
        ql = a_ref[:, 0:Q_RANK]
        kvl = a_ref[:, Q_RANK:Q_RANK + KV_RANK]
        kpe = a_ref[:, Q_RANK + KV_RANK:Q_RANK + KV_RANK + HP]
        qn = _rms(ql, qg_ref[...]).astype(BF16)
        kvn = _rms(kvl, kvg_ref[...]).astype(BF16)
        qn_ref[...] = qn
        kvn_ref[...] = kvn
        qraw = _dot(qn, wuq_ref[...])
        kn = _dot(kvn, wuk_ref[...])
        v_ref[...] = _dot(kvn, wuv_ref[...]).astype(BF16)
        c, sa, sb = c_ref[...], sa_ref[...], sb_ref[...]
        for h in range(N_HEADS):
            sl = slice(h * HP, (h + 1) * HP)
            tq = _rms(qraw[:, sl], qhg_ref[...], QK_DIM)
            q_ref[:, sl] = (_rope(tq, c, sa, sb) * SCALE).astype(BF16)
            tk = _rms(kn[:, sl] + kpe, khg_ref[...], QK_DIM)
            k_ref[:, sl] = _rope(tk, c, sa, sb).astype(BF16)

    row = lambda w: pl.BlockSpec((tm, w), lambda i: (i, 0))
    full = lambda a: pl.BlockSpec(a.shape, lambda i: (0,) * a.ndim)
    return pl.pallas_call(
        body, name="mla_prep", grid=(S // tm,),
        in_specs=[pl.BlockSpec((tm, 1024), lambda i: (i, 7)), row(HP), row(HP), row(HP),
                  full(qg), full(kvg), full(qhg), full(khg), full(wuq), full(wuk), full(wuv)],
        out_specs=[row(1024), row(1024), row(1024), row(Q_RANK), row(KV_RANK)],
        out_shape=[jax.ShapeDtypeStruct((S, 1024), BF16)] * 3
        + [jax.ShapeDtypeStruct((S, Q_RANK), BF16), jax.ShapeDtypeStruct((S, KV_RANK), BF16)],
        compiler_params=_params(("parallel",)),
    )(proj, *tabs, qg, kvg, qhg, khg, wuq, wuk, wuv)


def _attn_fwd(q, k, v):
    S = q.shape[0]
    tq, tk = _tile(S, 512), _tile(S, 2048)
    nk, nb = S // tk, tk // HP

    def body(q_ref, k_ref, v_ref, o_ref, lse_ref, m_s, l_s, acc_s):
        m_s[...] = jnp.full(m_s.shape, -jnp.inf, F32)
        l_s[...] = jnp.zeros(l_s.shape, F32)
        acc_s[...] = jnp.zeros(acc_s.shape, F32)
        qv = q_ref[...]

        def step(c, carry):
            rows = pl.ds(pl.multiple_of(c * tk, tk), tk)
            s = _dot_nt(qv, k_ref[rows, :])
            cm = s[:, 0:HP]
            for j in range(1, nb):
                cm = jnp.maximum(cm, s[:, j * HP:(j + 1) * HP])
            m_prev = m_s[...]
            m_new = jnp.maximum(m_prev, jnp.max(cm, axis=-1, keepdims=True))
            alpha = jnp.exp(m_prev - m_new)
            lsum = alpha * l_s[...]
            ps = []
            for j in range(nb):
                pj = jnp.exp(s[:, j * HP:(j + 1) * HP] - m_new)
                lsum = lsum + pj
                ps.append(pj.astype(BF16))
            l_s[...] = lsum
            acc_s[...] = alpha * acc_s[...] + _dot(jnp.concatenate(ps, axis=1), v_ref[rows, :])
            m_s[...] = m_new
            return carry

        lax.fori_loop(0, nk, step, 0)
        l = jnp.sum(l_s[...], axis=-1, keepdims=True)
        o_ref[...] = acc_s[...] / l
        lse_ref[0] = m_s[:, 0:1] + jnp.log(l)

    return pl.pallas_call(
        body, name="attn_fwd", grid=(N_HEADS, S // tq),
        in_specs=[pl.BlockSpec((tq, HP), lambda h, i: (i, h)),
                  pl.BlockSpec((S, HP), lambda h, i: (0, h)),
                  pl.BlockSpec((S, HP), lambda h, i: (0, h))],
        out_specs=[pl.BlockSpec((tq, HP), lambda h, i: (i, h)),
                   pl.BlockSpec((1, tq, 1), lambda h, i: (h, i, 0))],
        out_shape=[jax.ShapeDtypeStruct((S, N_HEADS * HP), F32),
                   jax.ShapeDtypeStruct((N_HEADS, S, 1), F32)],
        scratch_shapes=[pltpu.VMEM((tq, HP), F32), pltpu.VMEM((tq, HP), F32), pltpu.VMEM((tq, HP), F32)],
        compiler_params=_params(("parallel", "parallel")),
    )(q, k, v)


def _memkv(mem, mng, mkg, wmkv):
    M = mem.shape[0]

    def body(mem_ref, mng_ref, mkg_ref, w_ref, mk_ref, mv_ref):
        mn = _rms(mem_ref[...], mng_ref[...]).astype(BF16)
        mkv = _dot(mn, w_ref[...])
        for h in range(MEM_HEADS):
            kraw = mkv[:, 2 * MEM_HD * h:2 * MEM_HD * h + MEM_HD]
            mk_ref[:, MEM_HD * h:MEM_HD * (h + 1)] = _rms(kraw, mkg_ref[...]).astype(BF16)
            mv_ref[:, MEM_HD * h:MEM_HD * (h + 1)] = mkv[:, 2 * MEM_HD * h + MEM_HD:2 * MEM_HD * (h + 1)].astype(BF16)

    return pl.pallas_call(
        body, name="memkv",
        out_shape=[jax.ShapeDtypeStruct((M, MEM_W), BF16)] * 2,
        compiler_params=pltpu.CompilerParams(vmem_limit_bytes=VMEM_LIMIT_V7X),
    )(mem, mng, mkg, wmkv)


def _conv_shifts(cc, cu, hp_ref, hn_ref, i, n_tiles, tm):
    z = cc * cu
    zp = hp_ref[7:8, 0:CONV_W] * hp_ref[7:8, CONV_W:2 * CONV_W]
    zn = hn_ref[0:1, 0:CONV_W] * hn_ref[0:1, CONV_W:2 * CONV_W]
    zp = jnp.where(i == 0, 0.0, zp)
    zn = jnp.where(i == n_tiles - 1, 0.0, zn)
    row = lax.broadcasted_iota(jnp.int32, z.shape, 0)
    z_up = jnp.where(row == 0, zp, pltpu.roll(z, 1, 0))
    z_dn = jnp.where(row == tm - 1, zn, pltpu.roll(z, tm - 1, 0))
    return z, z_up, z_dn


def _halo_specs(tm, S, width, col):
    r8 = tm // 8
    prev = pl.BlockSpec((8, width), lambda i: (jnp.maximum(i * r8 - 1, 0), col))
    nxt = pl.BlockSpec((8, width), lambda i: (jnp.minimum((i + 1) * r8, S // 8 - 1), col))
    return prev, nxt


def _mem_attend(qm, mqg, mk_h, mv_h):
    r, qhat = _rms_parts(qm)
    mq = (qhat * mqg).astype(BF16)
    s = _dot_nt(mq, mk_h) * MEM_SCALE
    e = jnp.exp(s - jnp.max(s, axis=-1, keepdims=True))
    p = e / jnp.sum(e, axis=-1, keepdims=True)
    pv = _dot(p.astype(BF16), mv_h)
    return r, qhat, mq, p, pv


def _merge(proj, o, x, bg, convw, mqg, mk, mv, wba, wbc, wbm, wo):
    S = x.shape[0]
    tm = _tile(S, 256)
    nt = S // tm

    def body(main_ref, ccu_ref, hp_ref, hn_ref, o_ref, x_ref, bg_ref, cw_ref, mqg_ref, mk_ref, mv_ref,
             wba_ref, wbc_ref, wbm_ref, wo_ref, xn_ref, oa_ref, oc_ref, om_ref, u_ref, y_ref):
        i = pl.program_id(0)
        sil_a, _ = _silu_and_grad(main_ref[:, O_GA:O_GA + 1024])
        oa = (o_ref[...] * sil_a).astype(BF16)
        oa_ref[...] = oa
        z, z_up, z_dn = _conv_shifts(ccu_ref[:, 0:CONV_W], ccu_ref[:, CONV_W:], hp_ref, hn_ref, i, nt, tm)
        cv = cw_ref[0:1, :] * z_up + cw_ref[1:2, :] * z + cw_ref[2:3, :] * z_dn + cw_ref[3:4, :]
        sil_c, _ = _silu_and_grad(main_ref[:, O_GC:O_GC + CONV_W])
        oc = (main_ref[:, O_CB:O_CB + CONV_W] * cv * sil_c).astype(BF16)
        oc_ref[...] = oc
        sil_m, _ = _silu_and_grad(main_ref[:, O_GM:O_GM + MEM_W])
        for h in range(MEM_HEADS):
            sl = slice(h * MEM_HD, (h + 1) * MEM_HD)
            qm = main_ref[:, O_QM + h * MEM_HD:O_QM + (h + 1) * MEM_HD]
            pv = _mem_attend(qm, mqg_ref[...], mk_ref[:, sl], mv_ref[:, sl])[4]
            om_ref[:, sl] = (pv * sil_m[:, sl]).astype(BF16)
        ua = _dot(oa, wba_ref[...])
        uc = _dot(oc, wbc_ref[...])
        um = _dot(om_ref[...], wbm_ref[...])
        u_ref[:, 0:1024] = ua.astype(BF16)
        u_ref[:, 1024:2048] = uc.astype(BF16)
        u_ref[:, 2048:3072] = um.astype(BF16)
        rg = _sigmoid(main_ref[:, O_R:O_R + 3072] + bg_ref[...])
        y = (rg[:, 0:1024] * ua + rg[:, 1024:2048] * uc + rg[:, 2048:3072] * um).astype(BF16)
        y_ref[...] = y
        xn_ref[...] = x_ref[...] + _dot(y, wo_ref[...])

    row = lambda w: pl.BlockSpec((tm, w), lambda i: (i, 0))
    full = lambda a: pl.BlockSpec(a.shape, lambda i: (0,) * a.ndim)
    hp, hn = _halo_specs(tm, S, 1024, 6)
    return pl.pallas_call(
        body, name="merge", grid=(nt,),
        in_specs=[row(6144), pl.BlockSpec((tm, 1024), lambda i: (i, 6)), hp, hn, row(1024), row(1024),
                  full(bg), full(convw), full(mqg), full(mk), full(mv), full(wba), full(wbc), full(wbm), full(wo)],
        out_specs=[row(1024), row(1024), row(CONV_W), row(MEM_W), row(3072), row(1024)],
        out_shape=[jax.ShapeDtypeStruct((S, 1024), F32), jax.ShapeDtypeStruct((S, 1024), BF16),
                   jax.ShapeDtypeStruct((S, CONV_W), BF16), jax.ShapeDtypeStruct((S, MEM_W), BF16),
                   jax.ShapeDtypeStruct((S, 3072), BF16), jax.ShapeDtypeStruct((S, 1024), BF16)],
        compiler_params=_params(("parallel",)),
    )(proj, proj, proj, proj, o, x, bg, convw, mqg, mk, mv, wba, wbc, wbm, wo)


def _loss_head(xf, tgt):
    S = xf.shape[0]
    tm = _tile(S, 1024)

    def body(x_ref, t_ref, g_ref, acc_ref):
        @pl.when(pl.program_id(0) == 0)
        def _():
            acc_ref[...] = jnp.zeros(acc_ref.shape, F32)

        e = x_ref[...] - t_ref[...]
        g_ref[...] = e * (1.0 / D_MODEL)
        part = jnp.sum((e * e).reshape(tm // 8, 8, D_MODEL), axis=0)
        tot = part[:, 0:128]
        for k in range(1, D_MODEL // 128):
            tot = tot + part[:, 128 * k:128 * (k + 1)]
        acc_ref[...] += tot

    row = pl.BlockSpec((tm, D_MODEL), lambda i: (i, 0))
    return pl.pallas_call(
        body, name="loss_head", grid=(S // tm,),
        in_specs=[row, row],
        out_specs=[row, pl.BlockSpec((8, 128), lambda i: (0, 0))],
        out_shape=[jax.ShapeDtypeStruct((S, D_MODEL), F32), jax.ShapeDtypeStruct((8, 128), F32)],
        compiler_params=_params(("arbitrary",)),
    )(xf, tgt)


def _merge_bwd(g, proj, o, u, bg, convw, mqg, mk, mv, wot, wbat, wbct, wbmt):
    S = g.shape[0]
    tm = _tile(S, 256)
    nt = S // tm
    M = mk.shape[0]

    def body(g_ref, main_ref, ccu_ref, hp_ref, hn_ref, o_ref, u_ref, bg_ref, cw_ref, mqg_ref, mk_ref, mv_ref,
             wot_ref, wbat_ref, wbct_ref, wbmt_ref,
             dmain_ref, dcv_ref, do_ref, delta_ref, du_ref, dbg_ref, dmk_ref, dmv_ref, dmqg_ref):
        i = pl.program_id(0)

        @pl.when(i == 0)
        def _():
            dbg_ref[...] = jnp.zeros(dbg_ref.shape, F32)
            dmk_ref[...] = jnp.zeros(dmk_ref.shape, F32)
            dmv_ref[...] = jnp.zeros(dmv_ref.shape, F32)
            dmqg_ref[...] = jnp.zeros(dmqg_ref.shape, F32)

        dy = _dot(g_ref[...].astype(BF16), wot_ref[...])
        rg = _sigmoid(main_ref[:, O_R:O_R + 3072] + bg_ref[...])
        dyt = jnp.concatenate([dy, dy, dy], axis=1)
        dr = dyt * u_ref[...].astype(F32) * rg * (1.0 - rg)
        dmain_ref[:, O_R:O_R + 3072] = dr.astype(BF16)
        dbg_ref[...] += _colsum(dr)
        du = (dyt * rg).astype(BF16)
        du_ref[...] = du
        do_a = _dot(du[:, 0:1024], wbat_ref[...])
        do_c = _dot(du[:, 1024:2048], wbct_ref[...])
        do_m = _dot(du[:, 2048:3072], wbmt_ref[...])

        sil_a, dsil_a = _silu_and_grad(main_ref[:, O_GA:O_GA + 1024])
        ov = o_ref[...]
        d_o = do_a * sil_a
        do_ref[...] = d_o.astype(BF16)
        dmain_ref[:, O_GA:O_GA + 1024] = (do_a * ov * dsil_a).astype(BF16)
        prod = d_o * ov
        for h in range(N_HEADS):
            delta_ref[h] = jnp.sum(prod[:, h * HP:(h + 1) * HP], axis=-1, keepdims=True)

        z, z_up, z_dn = _conv_shifts(ccu_ref[:, 0:CONV_W], ccu_ref[:, CONV_W:], hp_ref, hn_ref, i, nt, tm)
        cv = cw_ref[0:1, :] * z_up + cw_ref[1:2, :] * z + cw_ref[2:3, :] * z_dn + cw_ref[3:4, :]
        sil_c, dsil_c = _silu_and_grad(main_ref[:, O_GC:O_GC + CONV_W])
        cb = main_ref[:, O_CB:O_CB + CONV_W]
        dmain_ref[:, O_CB:O_CB + CONV_W] = (do_c * cv * sil_c).astype(BF16)
        dmain_ref[:, O_GC:O_GC + CONV_W] = (do_c * cb * cv * dsil_c).astype(BF16)
        dcv_ref[...] = do_c * cb * sil_c

        sil_m, dsil_m = _silu_and_grad(main_ref[:, O_GM:O_GM + MEM_W])
        for h in range(MEM_HEADS):
            sl = slice(h * MEM_HD, (h + 1) * MEM_HD)
            qm = main_ref[:, O_QM + h * MEM_HD:O_QM + (h + 1) * MEM_HD]
            mk_h, mv_h = mk_ref[:, sl], mv_ref[:, sl]
            r, qhat, mq, p, pv = _mem_attend(qm, mqg_ref[...], mk_h, mv_h)
            dom = do_m[:, sl]
            dmain_ref[:, O_GM + h * MEM_HD:O_GM + (h + 1) * MEM_HD] = (dom * pv * dsil_m[:, sl]).astype(BF16)
            dpv = (dom * sil_m[:, sl]).astype(BF16)
            dp = _dot_nt(dpv, mv_h)
            ds = (p * (dp - jnp.sum(dp * p, axis=-1, keepdims=True)) * MEM_SCALE).astype(BF16)
            dmq = _dot(ds, mk_h)
            dmk_ref[:, sl] += _dot_tn(ds, mq)
            dmv_ref[:, sl] += _dot_tn(p.astype(BF16), dpv)
            dmqg_ref[...] += _colsum(dmq * qhat)
            dqm = _rms_bwd(dmq * mqg_ref[...], qhat, r, MEM_HD)
            dmain_ref[:, O_QM + h * MEM_HD:O_QM + (h + 1) * MEM_HD] = dqm.astype(BF16)

    row = lambda w: pl.BlockSpec((tm, w), lambda i: (i, 0))
    full = lambda a: pl.BlockSpec(a.shape, lambda i: (0,) * a.ndim)
    acc = lambda r, c: pl.BlockSpec((r, c), lambda i: (0, 0))
    hp, hn = _halo_specs(tm, S, 1024, 6)
    return pl.pallas_call(
        body, name="merge_bwd", grid=(nt,),
        in_specs=[row(1024), row(6144), pl.BlockSpec((tm, 1024), lambda i: (i, 6)), hp, hn, row(1024), row(3072),
                  full(bg), full(convw), full(mqg), full(mk), full(mv), full(wot), full(wbat), full(wbct), full(wbmt)],
        out_specs=[row(6144), row(CONV_W), row(1024), pl.BlockSpec((N_HEADS, tm, 1), lambda i: (0, i, 0)), row(3072),
                   acc(1, 3072), acc(M, MEM_W), acc(M, MEM_W), acc(1, MEM_HD)],
        out_shape=[jax.ShapeDtypeStruct((S, 6144), BF16), jax.ShapeDtypeStruct((S, CONV_W), F32),
                   jax.ShapeDtypeStruct((S, 1024), BF16), jax.ShapeDtypeStruct((N_HEADS, S, 1), F32),
                   jax.ShapeDtypeStruct((S, 3072), BF16), jax.ShapeDtypeStruct((1, 3072), F32),
                   jax.ShapeDtypeStruct((M, MEM_W), F32), jax.ShapeDtypeStruct((M, MEM_W), F32),
                   jax.ShapeDtypeStruct((1, MEM_HD), F32)],
        compiler_params=_params(("arbitrary",)),
    )(g, proj, proj, proj, proj, o, u, bg, convw, mqg, mk, mv, wot, wbat, wbct, wbmt)


def _conv_bwd(dcv, proj, convw):
    S = dcv.shape[0]
    tm = _tile(S, 512)
    nt = S // tm

    def body(d_ref, dp_ref, dn_ref, ccu_ref, hp_ref, hn_ref, cw_ref, dccu_ref, dcw_ref):
        i = pl.program_id(0)

        @pl.when(i == 0)
        def _():
            dcw_ref[...] = jnp.zeros(dcw_ref.shape, F32)

        cc, cu = ccu_ref[:, 0:CONV_W], ccu_ref[:, CONV_W:]
        z, z_up, z_dn = _conv_shifts(cc, cu, hp_ref, hn_ref, i, nt, tm)
        d = d_ref[...]
        dprev = jnp.where(i == 0, 0.0, dp_ref[7:8, :])
        dnext = jnp.where(i == nt - 1, 0.0, dn_ref[0:1, :])
        row = lax.broadcasted_iota(jnp.int32, d.shape, 0)
        d_up = jnp.where(row == 0, dprev, pltpu.roll(d, 1, 0))
        d_dn = jnp.where(row == tm - 1, dnext, pltpu.roll(d, tm - 1, 0))
        dz = cw_ref[0:1, :] * d_dn + cw_ref[1:2, :] * d + cw_ref[2:3, :] * d_up
        dccu_ref[:, 0:CONV_W] = (dz * cu).astype(BF16)
        dccu_ref[:, CONV_W:] = (dz * cc).astype(BF16)
        dcw_ref[0:1, :] += _colsum(d * z_up)
        dcw_ref[1:2, :] += _colsum(d * z)
        dcw_ref[2:3, :] += _colsum(d * z_dn)
        dcw_ref[3:4, :] += _colsum(d)

    hp, hn = _halo_specs(tm, S, 1024, 6)
    dp, dn = _halo_specs(tm, S, CONV_W, 0)
    return pl.pallas_call(
        body, name="conv_bwd", grid=(nt,),
        in_specs=[pl.BlockSpec((tm, CONV_W), lambda i: (i, 0)), dp, dn,
                  pl.BlockSpec((tm, 1024), lambda i: (i, 6)), hp, hn,
                  pl.BlockSpec((8, CONV_W), lambda i: (0, 0))],
        out_specs=[pl.BlockSpec((tm, 1024), lambda i: (i, 0)), pl.BlockSpec((8, CONV_W), lambda i: (0, 0))],
        out_shape=[jax.ShapeDtypeStruct((S, 1024), BF16), jax.ShapeDtypeStruct((8, CONV_W), F32)],
        compiler_params=_params(("arbitrary",)),
    )(dcv, dcv, dcv, proj, proj, proj, convw)


ATTN_BWD_TQ = 1024


def _attn_bwd(q, k, v, do, lse_r, delta_r):
    S = q.shape[0]
    tk = _tile(S, 512)
    nq, tq = lse_r.shape[1], lse_r.shape[3]

    def body(k_ref, v_ref, q_ref, do_ref, lse_ref, delta_ref, dk_ref, dv_ref, dq_ref, dk_s, dv_s):
        @pl.when(pl.program_id(1) == 0)
        def _():
            dq_ref[...] = jnp.zeros(dq_ref.shape, F32)

        dk_s[...] = jnp.zeros(dk_s.shape, F32)
        dv_s[...] = jnp.zeros(dv_s.shape, F32)
        kv, vv = k_ref[...], v_ref[...]

        def step(c, carry):
            rows = pl.ds(pl.multiple_of(c * tq, tq), tq)
            qv, dov = q_ref[rows, :], do_ref[rows, :]
            pt = jnp.exp(_dot_nt(kv, qv) - lse_ref[0, c])
            dv_s[...] += _dot(pt.astype(BF16), dov)
            dpt = _dot_nt(vv, dov)
            dst = (pt * (dpt - delta_ref[0, c])).astype(BF16)
            dk_s[...] += _dot(dst, qv)
            dq_ref[rows, :] += _dot_tn(dst, kv)
            return carry

        lax.fori_loop(0, nq, step, 0)
        dk_ref[...] = dk_s[...]
        dv_ref[...] = dv_s[...].astype(BF16)

    rowv = pl.BlockSpec((1, nq, 1, tq), lambda h, j: (h, 0, 0, 0))
    blk = pl.BlockSpec((tk, HP), lambda h, j: (j, h))
    res = pl.BlockSpec((S, HP), lambda h, j: (0, h))
    return pl.pallas_call(
        body, name="attn_bwd", grid=(N_HEADS, S // tk),
        in_specs=[blk, blk, res, res, rowv, rowv],
        out_specs=[blk, blk, res],
        out_shape=[jax.ShapeDtypeStruct((S, N_HEADS * HP), F32), jax.ShapeDtypeStruct((S, N_HEADS * HP), BF16),
                   jax.ShapeDtypeStruct((S, N_HEADS * HP), F32)],
        scratch_shapes=[pltpu.VMEM((tk, HP), F32), pltpu.VMEM((tk, HP), F32)],
        compiler_params=_params(("parallel", "arbitrary")),
    )(k, v, q, do, lse_r, delta_r)


def _mla_prep_bwd(proj, tabs, dq, dk, dv, qg, kvg, qhg, khg, wuq, wuk, wuqt, wukt, wuvt):
    S = proj.shape[0]
    tm = _tile(S, 256)

    def body(a_ref, c_ref, sa_ref, sb_ref, dq_ref, dk_ref, dv_ref, qg_ref, kvg_ref, qhg_ref, khg_ref,
             wuq_ref, wuk_ref, wuqt_ref, wukt_ref, wuvt_ref,
             da_ref, dqraw_ref, dkraw_ref, dqg_ref, dkvg_ref, dqhg_ref, dkhg_ref):
        @pl.when(pl.program_id(0) == 0)
        def _():
            dqg_ref[...] = jnp.zeros(dqg_ref.shape, F32)
            dkvg_ref[...] = jnp.zeros(dkvg_ref.shape, F32)
            dqhg_ref[...] = jnp.zeros(dqhg_ref.shape, F32)
            dkhg_ref[...] = jnp.zeros(dkhg_ref.shape, F32)

        ql = a_ref[:, 0:Q_RANK]
        kvl = a_ref[:, Q_RANK:Q_RANK + KV_RANK]
        kpe = a_ref[:, Q_RANK + KV_RANK:Q_RANK + KV_RANK + HP]
        rq, qhat = _rms_parts(ql)
        rkv, kvhat = _rms_parts(kvl)
        qraw = _dot((qhat * qg_ref[...]).astype(BF16), wuq_ref[...])
        kn = _dot((kvhat * kvg_ref[...]).astype(BF16), wuk_ref[...])
        c, sa, sb = c_ref[...], sa_ref[...], sb_ref[...]
        dkpe = jnp.zeros(kpe.shape, F32)
        dqhg = jnp.zeros((1, HP), F32)
        dkhg = jnp.zeros((1, HP), F32)
        for h in range(N_HEADS):
            sl = slice(h * HP, (h + 1) * HP)
            r, that = _rms_parts(qraw[:, sl], QK_DIM)
            dtn = _rope_t(dq_ref[:, sl], c, sa, sb) * SCALE
            dqhg = dqhg + _colsum(dtn * that)
            dqraw_ref[:, sl] = _rms_bwd(dtn * qhg_ref[...], that, r, QK_DIM).astype(BF16)
            r, that = _rms_parts(kn[:, sl] + kpe, QK_DIM)
            dtn = _rope_t(dk_ref[:, sl], c, sa, sb)
            dkhg = dkhg + _colsum(dtn * that)
            dkr = _rms_bwd(dtn * khg_ref[...], that, r, QK_DIM)
            dkraw_ref[:, sl] = dkr.astype(BF16)
            dkpe = dkpe + dkr
        dqhg_ref[...] += dqhg
        dkhg_ref[...] += dkhg
        dqn = _dot(dqraw_ref[...], wuqt_ref[...])
        dqg_ref[...] += _colsum(dqn * qhat)
        da_ref[:, 0:Q_RANK] = _rms_bwd(dqn * qg_ref[...], qhat, rq, Q_RANK).astype(BF16)
        dkvn = _dot(dkraw_ref[...], wukt_ref[...]) + _dot(dv_ref[...], wuvt_ref[...])
        dkvg_ref[...] += _colsum(dkvn * kvhat)
        da_ref[:, Q_RANK:Q_RANK + KV_RANK] = _rms_bwd(dkvn * kvg_ref[...], kvhat, rkv, KV_RANK).astype(BF16)
        da_ref[:, Q_RANK + KV_RANK:Q_RANK + KV_RANK + HP] = dkpe.astype(BF16)
        da_ref[:, Q_RANK + KV_RANK + HP:] = jnp.zeros((tm, 1024 - Q_RANK - KV_RANK - HP), BF16)

    row = lambda w: pl.BlockSpec((tm, w), lambda i: (i, 0))
    full = lambda a: pl.BlockSpec(a.shape, lambda i: (0,) * a.ndim)
    acc = lambda c: pl.BlockSpec((1, c), lambda i: (0, 0))
    return pl.pallas_call(
        body, name="mla_prep_bwd", grid=(S // tm,),
        in_specs=[pl.BlockSpec((tm, 1024), lambda i: (i, 7)), row(HP), row(HP), row(HP),
                  row(1024), row(1024), row(1024), full(qg), full(kvg), full(qhg), full(khg),
                  full(wuq), full(wuk), full(wuqt), full(wukt), full(wuvt)],
        out_specs=[row(1024), row(1024), row(1024), acc(Q_RANK), acc(KV_RANK), acc(HP), acc(HP)],
        out_shape=[jax.ShapeDtypeStruct((S, 1024), BF16)] * 3
        + [jax.ShapeDtypeStruct((1, Q_RANK), F32), jax.ShapeDtypeStruct((1, KV_RANK), F32),
           jax.ShapeDtypeStruct((1, HP), F32), jax.ShapeDtypeStruct((1, HP), F32)],
        compiler_params=_params(("arbitrary",)),
    )(proj, *tabs, dq, dk, dv, qg, kvg, qhg, khg, wuq, wuk, wuqt, wukt, wuvt)


def _inproj_bwd(dproj, wint, x, g, ng):
    S = x.shape[0]
    tm, tk = _tile(S, 512), 2048
    nk = PW // tk

    def body(dp_ref, w_ref, x_ref, g_ref, ng_ref, dx_ref, dng_ref, acc_s):
        i, k = pl.program_id(0), pl.program_id(1)

        @pl.when((i == 0) & (k == 0))
        def _():
            dng_ref[...] = jnp.zeros(dng_ref.shape, F32)

        @pl.when(k == 0)
        def _():
            acc_s[...] = jnp.zeros(acc_s.shape, F32)

        acc_s[...] += _dot(dp_ref[...], w_ref[...])

        @pl.when(k == nk - 1)
        def _():
            r, xhat = _rms_parts(x_ref[...])
            dh = acc_s[...]
            dng_ref[...] += _colsum(dh * xhat)
            dx_ref[...] = g_ref[...] + _rms_bwd(dh * ng_ref[...], xhat, r, D_MODEL)

    return pl.pallas_call(
        body, name="inproj_bwd", grid=(S // tm, nk),
        in_specs=[pl.BlockSpec((tm, tk), lambda i, k: (i, k)),
                  pl.BlockSpec((tk, D_MODEL), lambda i, k: (k, 0)),
                  pl.BlockSpec((tm, D_MODEL), lambda i, k: (i, 0)),
                  pl.BlockSpec((tm, D_MODEL), lambda i, k: (i, 0)),
                  pl.BlockSpec((1, D_MODEL), lambda i, k: (0, 0))],
        out_specs=[pl.BlockSpec((tm, D_MODEL), lambda i, k: (i, 0)),
                   pl.BlockSpec((1, D_MODEL), lambda i, k: (0, 0))],
        out_shape=[jax.ShapeDtypeStruct((S, D_MODEL), F32), jax.ShapeDtypeStruct((1, D_MODEL), F32)],
        scratch_shapes=[pltpu.VMEM((tm, D_MODEL), F32)],
        compiler_params=_params(("arbitrary", "arbitrary")),
    )(dproj, wint, x, g, ng)


def _memkv_bwd(mem, mng, mkg, wmkv, wmkvt, dmk, dmv):
    M = mem.shape[0]

    def body(mem_ref, mng_ref, mkg_ref, w_ref, wt_ref, dmk_ref, dmv_ref, dw_ref, dmng_ref, dmkg_ref, d_s):
        r, mhat = _rms_parts(mem_ref[...])
        mn = (mhat * mng_ref[...]).astype(BF16)
        mkv = _dot(mn, w_ref[...])
        dmkg = jnp.zeros((1, MEM_HD), F32)
        for h in range(MEM_HEADS):
            sl = slice(h * MEM_HD, (h + 1) * MEM_HD)
            rk, khat = _rms_parts(mkv[:, 2 * MEM_HD * h:2 * MEM_HD * h + MEM_HD])
            dkn = dmk_ref[:, sl]
            dmkg = dmkg + _colsum(dkn * khat)
            d_s[:, 2 * MEM_HD * h:2 * MEM_HD * h + MEM_HD] = _rms_bwd(dkn * mkg_ref[...], khat, rk, MEM_HD).astype(BF16)
            d_s[:, 2 * MEM_HD * h + MEM_HD:2 * MEM_HD * (h + 1)] = dmv_ref[:, sl].astype(BF16)
        dmkg_ref[...] = dmkg
        dw_ref[...] = _dot_tn(mn, d_s[...])
        dmn = _dot(d_s[...], wt_ref[...])
        dmng_ref[...] = _colsum(dmn * mhat)

    return pl.pallas_call(
        body, name="memkv_bwd",
        out_shape=[jax.ShapeDtypeStruct((D_MODEL, 2 * MEM_W), F32), jax.ShapeDtypeStruct((1, D_MODEL), F32),
                   jax.ShapeDtypeStruct((1, MEM_HD), F32)],
        scratch_shapes=[pltpu.VMEM((M, 2 * MEM_W), BF16)],
        compiler_params=pltpu.CompilerParams(vmem_limit_bytes=VMEM_LIMIT_V7X),
    )(mem, mng, mkg, wmkv, wmkvt, dmk, dmv)


def _mm_tn(a, b, name):
    S, M = a.shape
    N = b.shape[1]
    tm, tn, ts = _tile(M, 1024), _tile(N, 1024), _tile(S, 512)

    def body(a_ref, b_ref, o_ref):
        @pl.when(pl.program_id(2) == 0)
        def _():
            o_ref[...] = jnp.zeros(o_ref.shape, F32)

        o_ref[...] += _dot_tn(a_ref[...].astype(BF16), b_ref[...].astype(BF16))

    return pl.pallas_call(
        body, name=name, grid=(M // tm, N // tn, S // ts),
        in_specs=[pl.BlockSpec((ts, tm), lambda i, j, k: (k, i)),
                  pl.BlockSpec((ts, tn), lambda i, j, k: (k, j))],
        out_specs=pl.BlockSpec((tm, tn), lambda i, j, k: (i, j)),
        out_shape=jax.ShapeDtypeStruct((M, N), F32),
        compiler_params=_params(("parallel", "parallel", "arbitrary")),
    )(a, b)


def _adamw(w, g0, g1, m, v, name):
    R, C = w.shape
    tr = R
    for cand in (512, 256, 128, 64, 32, 16, 8):
        if R % cand == 0 and cand * C * 4 <= (1 << 20):
            tr = cand
            break
    c1 = 1.0 / (1.0 - ADAM_B1 ** ADAM_STEP)
    c2 = 1.0 / (1.0 - ADAM_B2 ** ADAM_STEP)

    def body(w_ref, g0_ref, g1_ref, m_ref, v_ref, g_ref, d_ref, nm_ref, nv_ref):
        g = g0_ref[...] + g1_ref[...]
        nm = ADAM_B1 * m_ref[...] + (1.0 - ADAM_B1) * g
        nv = ADAM_B2 * v_ref[...] + (1.0 - ADAM_B2) * (g * g)
        g_ref[...] = g
        nm_ref[...] = nm
        nv_ref[...] = nv
        d_ref[...] = -ADAM_LR * ((nm * c1) / (jnp.sqrt(nv * c2) + ADAM_EPS) + ADAM_WD * w_ref[...])

    blk = pl.BlockSpec((tr, C), lambda i: (i, 0))
    return pl.pallas_call(
        body, name=name, grid=(R // tr,),
        in_specs=[blk] * 5, out_specs=[blk] * 4,
        out_shape=[jax.ShapeDtypeStruct((R, C), F32)] * 4,
        compiler_params=_params(("parallel",)),
    )(w, g0, g1, m, v)


def _sum_slabs(a, name):
    K, R, C = a.shape
    tr = R
    for cand in (512, 256, 128, 64, 32, 16, 8):
        if R % cand == 0 and cand * C * 4 * K <= (4 << 20):
            tr = cand
            break

    def body(a_ref, o_ref):
        t = a_ref[0]
        for k in range(1, K):
            t = t + a_ref[k]
        o_ref[...] = t

    return pl.pallas_call(
        body, name=name, grid=(R // tr,),
        in_specs=[pl.BlockSpec((K, tr, C), lambda i: (0, i, 0))],
        out_specs=pl.BlockSpec((tr, C), lambda i: (i, 0)),
        out_shape=jax.ShapeDtypeStruct((R, C), F32),
        compiler_params=_params(("parallel",)),
    )(a)


MESH = pl.DeviceIdType.MESH
HBM = pl.BlockSpec(memory_space=pltpu.HBM)


def _chip_peer(x, y, c, k):
    return (x ^ (k >> 1), y ^ (k & 1), c)


def _gather_chips(arrs):
    n = len(arrs)

    def body(*refs):
        ins, outs = refs[:n], refs[n:2 * n]
        send, recv, loc = refs[2 * n:]
        x, y, c = lax.axis_index("x"), lax.axis_index("y"), lax.axis_index("c")
        me = 2 * x + y
        copies = []
        for a in range(n):
            own = pltpu.make_async_copy(ins[a], outs[a].at[me], loc.at[a])
            own.start()
            copies.append(own)
        for a in range(n):
            for k in (1, 2, 3):
                cp = pltpu.make_async_remote_copy(
                    src_ref=ins[a], dst_ref=outs[a].at[me], send_sem=send.at[3 * a + k - 1],
                    recv_sem=recv.at[3 * a + k - 1], device_id=_chip_peer(x, y, c, k), device_id_type=MESH)
                cp.start()
                copies.append(cp)
        for cp in copies:
            cp.wait()

    return pl.pallas_call(
        body, name="gather_weights",
        in_specs=[HBM] * n, out_specs=[HBM] * n,
        out_shape=[jax.ShapeDtypeStruct((4,) + a.shape, a.dtype) for a in arrs],
        scratch_shapes=[pltpu.SemaphoreType.DMA((3 * n,)), pltpu.SemaphoreType.DMA((3 * n,)),
                        pltpu.SemaphoreType.DMA((n,))],
    )(*arrs)


def _scatter_chips(arrs, small):
    n = len(arrs)

    def body(*refs):
        ins, small_in = refs[:n], refs[n]
        outs, small_out = refs[n + 1:2 * n + 1], refs[2 * n + 1]
        send, recv, loc, ssend, srecv = refs[2 * n + 2:]
        x, y, c = lax.axis_index("x"), lax.axis_index("y"), lax.axis_index("c")
        me = 2 * x + y
        me8 = 4 * x + 2 * y + c
        copies = []
        for a in range(n):
            own = pltpu.make_async_copy(ins[a].at[me], outs[a].at[me], loc.at[a])
            own.start()
            copies.append(own)
        own = pltpu.make_async_copy(small_in, small_out.at[me8], loc.at[n])
        own.start()
        copies.append(own)
        for k in range(1, 8):
            cp = pltpu.make_async_remote_copy(
                src_ref=small_in, dst_ref=small_out.at[me8], send_sem=ssend.at[k - 1], recv_sem=srecv.at[k - 1],
                device_id=(x ^ (k >> 2), y ^ ((k >> 1) & 1), c ^ (k & 1)), device_id_type=MESH)
            cp.start()
            copies.append(cp)
        for a in range(n):
            for k in (1, 2, 3):
                cp = pltpu.make_async_remote_copy(
                    src_ref=ins[a].at[me ^ k], dst_ref=outs[a].at[me], send_sem=send.at[3 * a + k - 1],
                    recv_sem=recv.at[3 * a + k - 1], device_id=_chip_peer(x, y, c, k), device_id_type=MESH)
                cp.start()
                copies.append(cp)
        for cp in copies:
            cp.wait()

    return pl.pallas_call(
        body, name="scatter_grads",
        in_specs=[HBM] * (n + 1), out_specs=[HBM] * (n + 1),
        out_shape=[jax.ShapeDtypeStruct(a.shape, a.dtype) for a in arrs]
        + [jax.ShapeDtypeStruct((8,) + small.shape, small.dtype)],
        scratch_shapes=[pltpu.SemaphoreType.DMA((3 * n,)), pltpu.SemaphoreType.DMA((3 * n,)),
                        pltpu.SemaphoreType.DMA((n + 1,)), pltpu.SemaphoreType.DMA((7,)),
                        pltpu.SemaphoreType.DMA((7,))],
    )(*arrs, small)


def _swap_cores(arrs):
    n = len(arrs)

    def body(*refs):
        ins, outs = refs[:n], refs[n:2 * n]
        send, recv = refs[2 * n:]
        x, y, c = lax.axis_index("x"), lax.axis_index("y"), lax.axis_index("c")
        copies = []
        for a in range(n):
            cp = pltpu.make_async_remote_copy(
                src_ref=ins[a], dst_ref=outs[a], send_sem=send.at[a], recv_sem=recv.at[a],
                device_id=(x, y, 1 - c), device_id_type=MESH)
            cp.start()
            copies.append(cp)
        for cp in copies:
            cp.wait()

    return pl.pallas_call(
        body, name="swap_cores",
        in_specs=[HBM] * n, out_specs=[HBM] * n,
        out_shape=[jax.ShapeDtypeStruct(a.shape, a.dtype) for a in arrs],
        scratch_shapes=[pltpu.SemaphoreType.DMA((n,)), pltpu.SemaphoreType.DMA((n,))],
    )(*arrs)


def _pad_last(a, n):
    return jnp.pad(a, [(0, 0)] * (a.ndim - 1) + [(0, n - a.shape[-1])])


def _pad_w_in(w):
    lead = w.shape[:-1]
    seg = lambda a, b: w[..., a:b]
    ga = _pad_last(seg(2720, 3232).reshape(lead + (N_HEADS, V_DIM)), HP).reshape(lead + (1024,))
    kpe = jnp.pad(seg(640, 672), [(0, 0)] * len(lead) + [(NOPE_DIM, HP - QK_DIM)])
    zero = jnp.zeros(lead + (PW - 7936,), w.dtype)
    return jnp.concatenate(
        [seg(4256, 7328), ga, seg(672, 1184), seg(2208, 2720), seg(3232, 3744), seg(3744, 4256),
         seg(1184, 1696), seg(1696, 2208), seg(0, 384), seg(384, 640), kpe, zero], axis=-1)


def _unpad_w_in(w):
    lead = w.shape[:-1]
    seg = lambda a, n: w[..., a:a + n]
    ga = seg(O_GA, 1024).reshape(lead + (N_HEADS, HP))[..., :V_DIM].reshape(lead + (N_HEADS * V_DIM,))
    return jnp.concatenate(
        [seg(O_QL, 384), seg(O_KVL, 256), seg(O_KPE + NOPE_DIM, ROPE_DIM), seg(O_CB, 512), seg(O_CC, 512),
         seg(O_CU, 512), seg(O_QM, 512), ga, seg(O_GC, 512), seg(O_GM, 512), seg(O_R, 3072)], axis=-1)


def _cols_from_shards(g):
    _, L, R, C = g.shape
    return jnp.transpose(g, (1, 2, 0, 3)).reshape(L, R, 4 * C)


def _cols_to_shards(w):
    L, R, C4 = w.shape
    return jnp.transpose(w.reshape(L, R, 4, C4 // 4), (2, 0, 1, 3))


def _rows_from_shards(g):
    _, L, R, C = g.shape
    return jnp.transpose(g, (1, 0, 2, 3)).reshape(L, 4 * R, C)


def _rows_to_shards(w):
    L, R4, C = w.shape
    return jnp.transpose(w.reshape(L, 4, R4 // 4, C), (1, 0, 2, 3))


def _t(w):
    return jnp.swapaxes(w, -1, -2)


def _layer_fwd(x, mem, tabs, p):
    proj, xn = _inproj(x, p["norm_g"], p["w_in"])
    q, k, v, qn, kvn = _mla_prep(proj, tabs, p["q_norm_g"], p["kv_norm_g"], p["q_head_g"], p["k_head_g"],
                                 p["w_uq"], p["w_uk"], p["w_uv"])
    o, lse = _attn_fwd(q, k, v)
    mk, mv = _memkv(mem, p["mem_norm_g"], p["mem_k_g"], p["w_mkv"])
    x_new, oa, oc, om, u, y = _merge(proj, o, x, p["b_gate"], p["conv_wb"], p["mem_q_g"], mk, mv,
                                     p["w_br_attn"], p["w_br_conv"], p["w_br_mem"], p["w_out"])
    saved = dict(x=x, proj=proj, xn=xn, q=q, k=k, v=v, qn=qn, kvn=kvn, o=o, lse=lse, mk=mk, mv=mv,
                 oa=oa, oc=oc, om=om, u=u, y=y)
    return x_new, saved


def _layer_bwd(g, mem, tabs, p, s):
    S = g.shape[0]
    tq = _tile(S, ATTN_BWD_TQ)
    dmain, dcv, d_o, delta, du, dbg, dmk, dmv, dmqg = _merge_bwd(
        g, s["proj"], s["o"], s["u"], p["b_gate"], p["conv_wb"], p["mem_q_g"], s["mk"], s["mv"],
        p["w_out_t"], p["w_br_attn_t"], p["w_br_conv_t"], p["w_br_mem_t"])
    dccu, dconv = _conv_bwd(dcv, s["proj"], p["conv_wb"])
    as_rows = lambda t: t.reshape(N_HEADS, S // tq, 1, tq)
    dk, dv, dq = _attn_bwd(s["q"], s["k"], s["v"], d_o, as_rows(s["lse"]), as_rows(delta))
    dsega, dqraw, dkraw, dqg, dkvg, dqhg, dkhg = _mla_prep_bwd(
        s["proj"], tabs, dq, dk, dv, p["q_norm_g"], p["kv_norm_g"], p["q_head_g"], p["k_head_g"],
        p["w_uq"], p["w_uk"], p["w_uq_t"], p["w_uk_t"], p["w_uv_t"])
    dproj = jnp.concatenate([dmain, dccu, dsega], axis=1)
    dx, dng = _inproj_bwd(dproj, p["w_in_t"], s["x"], g, p["norm_g"])
    dwmkv, dmng, dmkg = _memkv_bwd(mem, p["mem_norm_g"], p["mem_k_g"], p["w_mkv"], p["w_mkv_t"], dmk, dmv)
    grads = dict(
        norm_g=dng, b_gate=dbg, q_norm_g=dqg, kv_norm_g=dkvg, q_head_g=dqhg, k_head_g=dkhg,
        conv_wb=dconv, mem_norm_g=dmng, mem_q_g=dmqg, mem_k_g=dmkg, w_mkv=dwmkv,
        w_in=_mm_tn(s["xn"], dproj, "grad_w_in"),
        w_uq=_mm_tn(s["qn"], dqraw, "grad_w_uq"),
        w_uk=_mm_tn(s["kvn"], dkraw, "grad_w_uk"),
        w_uv=_mm_tn(s["kvn"], dv, "grad_w_uv"),
        w_br_attn=_mm_tn(s["oa"], du[:, 0:1024], "grad_w_br_attn"),
        w_br_conv=_mm_tn(s["oc"], du[:, 1024:2048], "grad_w_br_conv"),
        w_br_mem=_mm_tn(s["om"], du[:, 2048:3072], "grad_w_br_mem"),
        w_out=_mm_tn(s["y"], g, "grad_w_out"),
    )
    return dx, grads


def _layer_params(full, l):
    p = {}
    w_in = _pad_w_in(full["w_in"][l])
    w_uq = _pad_last(full["w_uq"][l].reshape(Q_RANK, N_HEADS, QK_DIM), HP).reshape(Q_RANK, 1024)
    ukv = full["w_ukv"][l].reshape(KV_RANK, N_HEADS, NOPE_DIM + V_DIM)
    w_uk = _pad_last(ukv[..., :NOPE_DIM], HP).reshape(KV_RANK, 1024)
    w_uv = _pad_last(ukv[..., NOPE_DIM:], HP).reshape(KV_RANK, 1024)
    w_ba = jnp.pad(full["w_br_attn"][l].reshape(N_HEADS, V_DIM, D_MODEL), ((0, 0), (0, HP - V_DIM), (0, 0)))
    w_ba = w_ba.reshape(1024, D_MODEL)
    p.update(w_in=w_in, w_uq=w_uq, w_uk=w_uk, w_uv=w_uv, w_br_attn=w_ba, w_br_conv=full["w_br_conv"][l],
             w_br_mem=full["w_br_mem"][l], w_out=full["w_out"][l], w_mkv=full["w_mkv"][l])
    for n in ("w_in", "w_uq", "w_uk", "w_uv", "w_br_attn", "w_br_conv", "w_br_mem", "w_out", "w_mkv"):
        p[n + "_t"] = _t(p[n])
    for n in ("norm_g", "b_gate", "q_norm_g", "kv_norm_g", "mem_norm_g", "mem_q_g", "mem_k_g"):
        p[n] = full[n][l][None, :]
    p["q_head_g"] = _pad_last(full["q_head_g"][l][None, :], HP)
    p["k_head_g"] = _pad_last(full["k_head_g"][l][None, :], HP)
    p["conv_wb"] = jnp.concatenate(
        [full["conv_w"][l], full["conv_b"][l][None, :], jnp.zeros((4, CONV_W), F32)], axis=0)
    return p


def _local_step(x, mem, positions, full, target):
    S = x.shape[0]
    invf16 = ROPE_BASE ** (-jnp.arange(0, ROPE_DIM, 2, dtype=F32) / ROPE_DIM)
    invf = jnp.concatenate([jnp.zeros((NOPE_DIM,), F32), invf16, invf16, jnp.zeros((HP - QK_DIM,), F32)])[None, :]
    tabs = _rope_tables(jnp.broadcast_to(positions.reshape(S, 1), (S, HP)), invf)
    params = [_layer_params(full, l) for l in range(DEPTH)]
    saved = []
    h = x
    for l in range(DEPTH):
        h, s = _layer_fwd(h, mem, tabs, params[l])
        saved.append(s)
    g, loss_part = _loss_head(h, target)
    per_layer = [None] * DEPTH
    for l in reversed(range(DEPTH)):
        g, per_layer[l] = _layer_bwd(g, mem, tabs, params[l], saved[l])
    st = lambda n: jnp.stack([per_layer[l][n] for l in range(DEPTH)])
    out = {}
    out["w_in"] = _unpad_w_in(st("w_in"))
    out["w_uq"] = st("w_uq").reshape(DEPTH, Q_RANK, N_HEADS, HP)[..., :QK_DIM].reshape(DEPTH, Q_RANK, N_HEADS * QK_DIM)
    duk = st("w_uk").reshape(DEPTH, KV_RANK, N_HEADS, HP)[..., :NOPE_DIM]
    duv = st("w_uv").reshape(DEPTH, KV_RANK, N_HEADS, HP)[..., :V_DIM]
    out["w_ukv"] = jnp.concatenate([duk, duv], axis=-1).reshape(DEPTH, KV_RANK, 1024)
    out["w_br_attn"] = st("w_br_attn").reshape(DEPTH, N_HEADS, HP, D_MODEL)[:, :, :V_DIM].reshape(DEPTH, 512, D_MODEL)
    for n in ("w_br_conv", "w_br_mem", "w_out", "w_mkv"):
        out[n] = st(n)
    for n in ("norm_g", "b_gate", "q_norm_g", "kv_norm_g", "mem_norm_g", "mem_q_g", "mem_k_g"):
        out[n] = st(n)[:, 0, :]
    out["q_head_g"] = st("q_head_g")[:, 0, :QK_DIM]
    out["k_head_g"] = st("k_head_g")[:, 0, :QK_DIM]
    cwb = st("conv_wb")
    out["conv_w"] = cwb[:, 0:3, :]
    out["conv_b"] = cwb[:, 3, :]
    return loss_part, g, out


_COL_SHARDED = ("w_in", "w_uq", "w_ukv", "w_br_attn", "w_br_conv", "w_br_mem")
_ROW_SHARDED = ("w_mkv", "w_out")
_BIG = _COL_SHARDED + _ROW_SHARDED
_SMALL = ("norm_g", "b_gate", "q_norm_g", "kv_norm_g", "q_head_g", "k_head_g", "conv_w", "conv_b",
          "mem_norm_g", "mem_q_g", "mem_k_g")
_ORDER = ("norm_g", "w_in", "b_gate", "q_norm_g", "w_uq", "kv_norm_g", "w_ukv", "q_head_g", "k_head_g",
          "conv_w", "conv_b", "mem_norm_g", "w_mkv", "mem_q_g", "mem_k_g", "w_br_attn", "w_br_conv",
          "w_br_mem", "w_out")


def _pack_small(d, extra):
    flat = jnp.concatenate([d[n].reshape(-1) for n in _SMALL] + [extra.reshape(-1)])
    n = flat.shape[0]
    rows = -(-n // 1024) * 8
    return jnp.pad(flat, (0, rows * 128 - n)).reshape(rows, 128)


def _unpack_small(packed, like):
    flat = packed.reshape(-1)
    out, off = {}, 0
    for n in _SMALL:
        sz = int(np.prod(like[n].shape))
        out[n] = flat[off:off + sz].reshape(like[n].shape)
        off += sz
    return out, flat[off:]


def kernel(x, mem, positions, norm_g, w_in, b_gate, q_norm_g, w_uq, kv_norm_g, w_ukv, q_head_g, k_head_g, conv_w, conv_b, mem_norm_g, w_mkv, mem_q_g, mem_k_g, w_br_attn, w_br_conv, w_br_mem, w_out, loss_target, m_norm_g, m_w_in, m_b_gate, m_q_norm_g, m_w_uq, m_kv_norm_g, m_w_ukv, m_q_head_g, m_k_head_g, m_conv_w, m_conv_b, m_mem_norm_g, m_w_mkv, m_mem_q_g, m_mem_k_g, m_w_br_attn, m_w_br_conv, m_w_br_mem, m_w_out, v_norm_g, v_w_in, v_b_gate, v_q_norm_g, v_w_uq, v_kv_norm_g, v_w_ukv, v_q_head_g, v_k_head_g, v_conv_w, v_conv_b, v_mem_norm_g, v_w_mkv, v_mem_q_g, v_mem_k_g, v_w_br_attn, v_w_br_conv, v_w_br_mem, v_w_out):
    w = dict(norm_g=norm_g, w_in=w_in, b_gate=b_gate, q_norm_g=q_norm_g, w_uq=w_uq, kv_norm_g=kv_norm_g,
             w_ukv=w_ukv, q_head_g=q_head_g, k_head_g=k_head_g, conv_w=conv_w, conv_b=conv_b,
             mem_norm_g=mem_norm_g, w_mkv=w_mkv, mem_q_g=mem_q_g, mem_k_g=mem_k_g, w_br_attn=w_br_attn,
             w_br_conv=w_br_conv, w_br_mem=w_br_mem, w_out=w_out)
    m = dict(norm_g=m_norm_g, w_in=m_w_in, b_gate=m_b_gate, q_norm_g=m_q_norm_g, w_uq=m_w_uq,
             kv_norm_g=m_kv_norm_g, w_ukv=m_w_ukv, q_head_g=m_q_head_g, k_head_g=m_k_head_g, conv_w=m_conv_w,
             conv_b=m_conv_b, mem_norm_g=m_mem_norm_g, w_mkv=m_w_mkv, mem_q_g=m_mem_q_g, mem_k_g=m_mem_k_g,
             w_br_attn=m_w_br_attn, w_br_conv=m_w_br_conv, w_br_mem=m_w_br_mem, w_out=m_w_out)
    v = dict(norm_g=v_norm_g, w_in=v_w_in, b_gate=v_b_gate, q_norm_g=v_q_norm_g, w_uq=v_w_uq,
             kv_norm_g=v_kv_norm_g, w_ukv=v_w_ukv, q_head_g=v_q_head_g, k_head_g=v_k_head_g, conv_w=v_conv_w,
             conv_b=v_conv_b, mem_norm_g=v_mem_norm_g, w_mkv=v_w_mkv, mem_q_g=v_mem_q_g, mem_k_g=v_mem_k_g,
             w_br_attn=v_w_br_attn, w_br_conv=v_w_br_conv, w_br_mem=v_w_br_mem, w_out=v_w_out)
    chip = 2 * lax.axis_index("x") + lax.axis_index("y")

    gathered = _gather_chips([w[n].astype(BF16) for n in _BIG] + [w["conv_w"]])
    full = {n: w[n] for n in _SMALL}
    for n, gth in zip(_BIG, gathered):
        full[n] = _cols_from_shards(gth) if n in _COL_SHARDED else _rows_from_shards(gth)
    full["conv_w"] = _cols_from_shards(gathered[-1])

    loss_part, grad_x, grads = _local_step(x[0], mem[0], positions[0], full, loss_target[0])

    to_owner = [_cols_to_shards(grads[n]) if n in _COL_SHARDED else _rows_to_shards(grads[n]) for n in _BIG]
    loss_vec = jnp.zeros((128,), F32).at[0].set(0.5 / D_MODEL * jnp.sum(loss_part))
    small = _pack_small(grads, loss_vec)
    scattered = _scatter_chips(to_owner, small)
    small_sum = _sum_slabs(scattered[-1], "sum_small")
    partial = []
    for n, slabs in zip(_BIG, scattered[:-1]):
        sh = slabs.shape
        partial.append(_sum_slabs(slabs.reshape(4, sh[1] * sh[2], sh[3]), "sum_" + n))
    other = _swap_cores(partial)

    small_g, tail = _unpack_small(small_sum, {n: (grads[n]) for n in _SMALL})
    loss = tail[0]
    small_g["conv_w"] = lax.dynamic_slice_in_dim(small_g["conv_w"], chip * (CONV_W // 4), CONV_W // 4, axis=2)

    outs_g, outs_d, outs_m, outs_v = {}, {}, {}, {}
    for n, p0, p1 in zip(_BIG, partial, other):
        shape = w[n].shape
        flat = lambda t: t.reshape(p0.shape)
        g_, d_, m_, v_ = _adamw(flat(w[n]), p0, p1, flat(m[n]), flat(v[n]), "adamw_" + n)
        outs_g[n], outs_d[n], outs_m[n], outs_v[n] = (t.reshape(shape) for t in (g_, d_, m_, v_))
    zero_small = jnp.zeros_like(small_sum)
    pk = lambda d: _pack_small(d, jnp.zeros((128,), F32))
    g_, d_, m_, v_ = _adamw(pk(w), _pack_small(small_g, jnp.zeros((128,), F32)), zero_small, pk(m), pk(v),
                            "adamw_small")
    like = {n: w[n] for n in _SMALL}
    for dst, packed in ((outs_g, g_), (outs_d, d_), (outs_m, m_), (outs_v, v_)):
        dst.update(_unpack_small(packed, like)[0])

    return (loss, grad_x[None], *[outs_g[n] for n in _ORDER], *[outs_d[n] for n in _ORDER],
            *[outs_m[n] for n in _ORDER], *[outs_v[n] for n in _ORDER])
```

```python
import functools

import numpy as np
import jax
import jax.numpy as jnp
from jax import lax
from jax.experimental import pallas as pl
from jax.experimental.pallas import tpu as pltpu

F32 = jnp.float32
BF16 = jnp.bfloat16

D_MODEL = 1024
DEPTH = 4
N_HEADS = 8
QK_DIM = 96
NOPE_DIM = 64
ROPE_DIM = 32
V_DIM = 64
Q_RANK = 384
KV_RANK = 256
CONV_W = 512
MEM_HEADS = 4
MEM_HD = 128
MEM_W = 512
IN_WIDTH = 7328
PW = 8192
HP = 128
EPS = 1e-6
ROPE_BASE = 10000.0
SCALE = QK_DIM ** -0.5
MEM_SCALE = MEM_HD ** -0.5

ADAM_LR = 0.001
ADAM_B1 = 0.9
ADAM_B2 = 0.999
ADAM_EPS = 1e-08
ADAM_WD = 0.01
ADAM_STEP = 10

VMEM_LIMIT_V7X = 56 * 1024 * 1024

O_R, O_GA, O_CB, O_QM, O_GC, O_GM, O_CC, O_CU, O_QL, O_KVL, O_KPE = (
    0, 3072, 4096, 4608, 5120, 5632, 6144, 6656, 7168, 7552, 7808)


def _params(sem, vmem=VMEM_LIMIT_V7X):
    return pltpu.CompilerParams(dimension_semantics=sem, vmem_limit_bytes=vmem)


def _sigmoid(t):
    return 1.0 / (1.0 + jnp.exp(-t))


def _silu_and_grad(g):
    sg = _sigmoid(g)
    return g * sg, sg * (1.0 + g * (1.0 - sg))


def _rms(t, g, n=None):
    n = t.shape[-1] if n is None else n
    r = lax.rsqrt(jnp.sum(t * t, axis=-1, keepdims=True) * (1.0 / n) + EPS)
    return (t * r) * g


def _rms_parts(t, n=None):
    n = t.shape[-1] if n is None else n
    r = lax.rsqrt(jnp.sum(t * t, axis=-1, keepdims=True) * (1.0 / n) + EPS)
    return r, t * r


def _rms_bwd(dhat, hat, r, n):
    return r * (dhat - hat * (jnp.sum(dhat * hat, axis=-1, keepdims=True) * (1.0 / n)))


def _rope(t, c, sa, sb):
    return t * c + pltpu.roll(t, HP - 16, 1) * sa + pltpu.roll(t, 16, 1) * sb


def _rope_t(d, c, sa, sb):
    return d * c + pltpu.roll(d * sa, 16, 1) + pltpu.roll(d * sb, HP - 16, 1)


def _dot(a, b):
    return jnp.dot(a, b, preferred_element_type=F32)


def _dot_nt(a, b):
    return lax.dot_general(a, b, (((1,), (1,)), ((), ())), preferred_element_type=F32)


def _dot_tn(a, b):
    return lax.dot_general(a, b, (((0,), (0,)), ((), ())), preferred_element_type=F32)


def _colsum(t):
    return jnp.sum(t, axis=0, keepdims=True)


def _tile(n, t):
    t = min(n, t)
    assert n % t == 0, (n, t)
    return t


def _rope_tables(pos_b, invf):
    S = pos_b.shape[0]
    tm = _tile(S, 1024)

    def body(pos_ref, invf_ref, c_ref, sa_ref, sb_ref):
        ang = pos_ref[...].astype(F32) * invf_ref[...]
        lane = lax.broadcasted_iota(jnp.int32, ang.shape, 1)
        cs = jnp.cos(ang)
        sn = jnp.sin(ang)
        c_ref[...] = jnp.where(lane < NOPE_DIM, 1.0, jnp.where(lane < QK_DIM, cs, 0.0))
        sa_ref[...] = jnp.where((lane >= NOPE_DIM) & (lane < NOPE_DIM + 16), -sn, 0.0)
        sb_ref[...] = jnp.where((lane >= NOPE_DIM + 16) & (lane < QK_DIM), sn, 0.0)

    blk = pl.BlockSpec((tm, HP), lambda i: (i, 0))
    return pl.pallas_call(
        body, name="rope_tables", grid=(S // tm,),
        in_specs=[blk, pl.BlockSpec((1, HP), lambda i: (0, 0))],
        out_specs=[blk, blk, blk],
        out_shape=[jax.ShapeDtypeStruct((S, HP), F32)] * 3,
        compiler_params=_params(("parallel",)),
    )(pos_b, invf)


def _inproj(x, g, w):
    S = x.shape[0]
    tm, tn = _tile(S, 1024), 2048

    def body(x_ref, g_ref, w_ref, proj_ref, xn_ref):
        @pl.when(pl.program_id(1) == 0)
        def _():
            xn_ref[...] = _rms(x_ref[...], g_ref[...]).astype(BF16)

        proj_ref[...] = _dot(xn_ref[...], w_ref[...])

    return pl.pallas_call(
        body, name="inproj", grid=(S // tm, PW // tn),
        in_specs=[pl.BlockSpec((tm, D_MODEL), lambda i, j: (i, 0)),
                  pl.BlockSpec((1, D_MODEL), lambda i, j: (0, 0)),
                  pl.BlockSpec((D_MODEL, tn), lambda i, j: (0, j))],
        out_specs=[pl.BlockSpec((tm, tn), lambda i, j: (i, j)),
                   pl.BlockSpec((tm, D_MODEL), lambda i, j: (i, 0))],
        out_shape=[jax.ShapeDtypeStruct((S, PW), F32), jax.ShapeDtypeStruct((S, D_MODEL), BF16)],
        compiler_params=_params(("parallel", "arbitrary")),
    )(x, g, w)


def _mla_prep(proj, tabs, qg, kvg, qhg, khg, wuq, wuk, wuv):
    S = proj.shape[0]
    tm = _tile(S, 512)

    def body(a_ref, c_ref, sa_ref, sb_ref, qg_ref, kvg_ref, qhg_ref, khg_ref, wuq_ref, wuk_ref, wuv_ref,
             q_ref, k_ref, v_ref, qn_ref, kvn_ref):
        ql = a_ref[:, 0:Q_RANK]
        kvl = a_ref[:, Q_RANK:Q_RANK + KV_RANK]
        kpe = a_ref[:, Q_RANK + KV_RANK:Q_RANK + KV_RANK + HP]
        qn = _rms(ql, qg_ref[...]).astype(BF16)
        kvn = _rms(kvl, kvg_ref[...]).astype(BF16)
        qn_ref[...] = qn
        kvn_ref[...] = kvn
        qraw = _dot(qn, wuq_ref[...])
        kn = _dot(kvn, wuk_ref[...])
        v_ref[...] = _dot(kvn, wuv_ref[...]).astype(BF16)
        c, sa, sb = c_ref[...], sa_ref[...], sb_ref[...]
        for h in range(N_HEADS):
            sl = slice(h * HP, (h + 1) * HP)
            tq = _rms(qraw[:, sl], qhg_ref[...], QK_DIM)
            q_ref[:, sl] = (_rope(tq, c, sa, sb) * SCALE).astype(BF16)
            tk = _rms(kn[:, sl] + kpe, khg_ref[...], QK_DIM)
            k_ref[:, sl] = _rope(tk, c, sa, sb).astype(BF16)

    row = lambda w: pl.BlockSpec((tm, w), lambda i: (i, 0))
    full = lambda a: pl.BlockSpec(a.shape, lambda i: (0,) * a.ndim)
    return pl.pallas_call(
        body, name="mla_prep", grid=(S // tm,),
        in_specs=[pl.BlockSpec((tm, 1024), lambda i: (i, 7)), row(HP), row(HP), row(HP),
                  full(qg), full(kvg), full(qhg), full(khg), full(wuq), full(wuk), full(wuv)],
        out_specs=[row(1024), row(1024), row(1024), row(Q_RANK), row(KV_RANK)],
        out_shape=[jax.ShapeDtypeStruct((S, 1024), BF16)] * 3
        + [jax.ShapeDtypeStruct((S, Q_RANK), BF16), jax.ShapeDtypeStruct((S, KV_RANK), BF16)],
        compiler_params=_params(("parallel",)),
    )(proj, *tabs, qg, kvg, qhg, khg, wuq, wuk, wuv)


def _attn_fwd(q, k, v):
    S = q.shape[0]
    tq, tk = _tile(S, 1024), _tile(S, 2048)
    nk, nb = S // tk, tk // HP

    def body(q_ref, k_ref, v_ref, o_ref, lse_ref, m_s, l_s, acc_s):
        m_s[...] = jnp.full(m_s.shape, -jnp.inf, F32)
        l_s[...] = jnp.zeros(l_s.shape, F32)
        acc_s[...] = jnp.zeros(acc_s.shape, F32)
        qv = q_ref[...]

        def step(c, carry):
            rows = pl.ds(pl.multiple_of(c * tk, tk), tk)
            s = _dot_nt(qv, k_ref[rows, :])
            cm = s[:, 0:HP]
            for j in range(1, nb):
                cm = jnp.maximum(cm, s[:, j * HP:(j + 1) * HP])
            m_prev = m_s[...]
            m_new = jnp.maximum(m_prev, jnp.max(cm, axis=-1, keepdims=True))
            alpha = jnp.exp(m_prev - m_new)
            lsum = alpha * l_s[...]
            ps = []
            for j in range(nb):
                pj = jnp.exp(s[:, j * HP:(j + 1) * HP] - m_new)
                lsum = lsum + pj
                ps.append(pj.astype(BF16))
            l_s[...] = lsum
            acc_s[...] = alpha * acc_s[...] + _dot(jnp.concatenate(ps, axis=1), v_ref[rows, :])
            m_s[...] = m_new
            return carry

        lax.fori_loop(0, nk, step, 0)
        l = jnp.sum(l_s[...], axis=-1, keepdims=True)
        o_ref[...] = acc_s[...] / l
        lse_ref[0] = m_s[:, 0:1] + jnp.log(l)

    return pl.pallas_call(
        body, name="attn_fwd", grid=(N_HEADS, S // tq),
        in_specs=[pl.BlockSpec((tq, HP), lambda h, i: (i, h)),
                  pl.BlockSpec((S, HP), lambda h, i: (0, h)),
                  pl.BlockSpec((S, HP), lambda h, i: (0, h))],
        out_specs=[pl.BlockSpec((tq, HP), lambda h, i: (i, h)),
                   pl.BlockSpec((1, tq, 1), lambda h, i: (h, i, 0))],
        out_shape=[jax.ShapeDtypeStruct((S, N_HEADS * HP), F32),
                   jax.ShapeDtypeStruct((N_HEADS, S, 1), F32)],
        scratch_shapes=[pltpu.VMEM((tq, HP), F32), pltpu.VMEM((tq, HP), F32), pltpu.VMEM((tq, HP), F32)],
        compiler_params=_params(("parallel", "parallel")),
    )(q, k, v)


def _memkv(mem, mng, mkg, wmkv):
    M = mem.shape[0]

    def body(mem_ref, mng_ref, mkg_ref, w_ref, mk_ref, mv_ref):
        mn = _rms(mem_ref[...], mng_ref[...]).astype(BF16)
        mkv = _dot(mn, w_ref[...])
        for h in range(MEM_HEADS):
            kraw = mkv[:, 2 * MEM_HD * h:2 * MEM_HD * h + MEM_HD]
            mk_ref[:, MEM_HD * h:MEM_HD * (h + 1)] = _rms(kraw, mkg_ref[...]).astype(BF16)
            mv_ref[:, MEM_HD * h:MEM_HD * (h + 1)] = mkv[:, 2 * MEM_HD * h + MEM_HD:2 * MEM_HD * (h + 1)].astype(BF16)

    return pl.pallas_call(
        body, name="memkv",
        out_shape=[jax.ShapeDtypeStruct((M, MEM_W), BF16)] * 2,
        compiler_params=pltpu.CompilerParams(vmem_limit_bytes=VMEM_LIMIT_V7X),
    )(mem, mng, mkg, wmkv)


def _conv_shifts(cc, cu, hp_ref, hn_ref, i, n_tiles, tm):
    z = cc * cu
    zp = hp_ref[7:8, 0:CONV_W] * hp_ref[7:8, CONV_W:2 * CONV_W]
    zn = hn_ref[0:1, 0:CONV_W] * hn_ref[0:1, CONV_W:2 * CONV_W]
    zp = jnp.where(i == 0, 0.0, zp)
    zn = jnp.where(i == n_tiles - 1, 0.0, zn)
    row = lax.broadcasted_iota(jnp.int32, z.shape, 0)
    z_up = jnp.where(row == 0, zp, pltpu.roll(z, 1, 0))
    z_dn = jnp.where(row == tm - 1, zn, pltpu.roll(z, tm - 1, 0))
    return z, z_up, z_dn


def _halo_specs(tm, S, width, col):
    r8 = tm // 8
    prev = pl.BlockSpec((8, width), lambda i: (jnp.maximum(i * r8 - 1, 0), col))
    nxt = pl.BlockSpec((8, width), lambda i: (jnp.minimum((i + 1) * r8, S // 8 - 1), col))
    return prev, nxt


def _mem_attend(qm, mqg, mk_h, mv_h):
    r, qhat = _rms_parts(qm)
    mq = (qhat * mqg).astype(BF16)
    s = _dot_nt(mq, mk_h) * MEM_SCALE
    e = jnp.exp(s - jnp.max(s, axis=-1, keepdims=True))
    p = e / jnp.sum(e, axis=-1, keepdims=True)
    pv = _dot(p.astype(BF16), mv_h)
    return r, qhat, mq, p, pv


def _merge(proj, o, x, bg, convw, mqg, mk, mv, wba, wbc, wbm, wo):
    S = x.shape[0]
    tm = _tile(S, 256)
    nt = S // tm

    def body(main_ref, ccu_ref, hp_ref, hn_ref, o_ref, x_ref, bg_ref, cw_ref, mqg_ref, mk_ref, mv_ref,
             wba_ref, wbc_ref, wbm_ref, wo_ref, xn_ref, oa_ref, oc_ref, om_ref, u_ref, y_ref):
        i = pl.program_id(0)
        sil_a, _ = _silu_and_grad(main_ref[:, O_GA:O_GA + 1024])
        oa = (o_ref[...] * sil_a).astype(BF16)
        oa_ref[...] = oa
        z, z_up, z_dn = _conv_shifts(ccu_ref[:, 0:CONV_W], ccu_ref[:, CONV_W:], hp_ref, hn_ref, i, nt, tm)
        cv = cw_ref[0:1, :] * z_up + cw_ref[1:2, :] * z + cw_ref[2:3, :] * z_dn + cw_ref[3:4, :]
        sil_c, _ = _silu_and_grad(main_ref[:, O_GC:O_GC + CONV_W])
        oc = (main_ref[:, O_CB:O_CB + CONV_W] * cv * sil_c).astype(BF16)
        oc_ref[...] = oc
        sil_m, _ = _silu_and_grad(main_ref[:, O_GM:O_GM + MEM_W])
        for h in range(MEM_HEADS):
            sl = slice(h * MEM_HD, (h + 1) * MEM_HD)
            qm = main_ref[:, O_QM + h * MEM_HD:O_QM + (h + 1) * MEM_HD]
            pv = _mem_attend(qm, mqg_ref[...], mk_ref[:, sl], mv_ref[:, sl])[4]
            om_ref[:, sl] = (pv * sil_m[:, sl]).astype(BF16)
        ua = _dot(oa, wba_ref[...])
        uc = _dot(oc, wbc_ref[...])
        um = _dot(om_ref[...], wbm_ref[...])
        u_ref[:, 0:1024] = ua.astype(BF16)
        u_ref[:, 1024:2048] = uc.astype(BF16)
        u_ref[:, 2048:3072] = um.astype(BF16)
        rg = _sigmoid(main_ref[:, O_R:O_R + 3072] + bg_ref[...])
        y = (rg[:, 0:1024] * ua + rg[:, 1024:2048] * uc + rg[:, 2048:3072] * um).astype(BF16)
        y_ref[...] = y
        xn_ref[...] = x_ref[...] + _dot(y, wo_ref[...])

    row = lambda w: pl.BlockSpec((tm, w), lambda i: (i, 0))
    full = lambda a: pl.BlockSpec(a.shape, lambda i: (0,) * a.ndim)
    hp, hn = _halo_specs(tm, S, 1024, 6)
    return pl.pallas_call(
        body, name="merge", grid=(nt,),
        in_specs=[row(6144), pl.BlockSpec((tm, 1024), lambda i: (i, 6)), hp, hn, row(1024), row(1024),
                  full(bg), full(convw), full(mqg), full(mk), full(mv), full(wba), full(wbc), full(wbm), full(wo)],
        out_specs=[row(1024), row(1024), row(CONV_W), row(MEM_W), row(3072), row(1024)],
        out_shape=[jax.ShapeDtypeStruct((S, 1024), F32), jax.ShapeDtypeStruct((S, 1024), BF16),
                   jax.ShapeDtypeStruct((S, CONV_W), BF16), jax.ShapeDtypeStruct((S, MEM_W), BF16),
                   jax.ShapeDtypeStruct((S, 3072), BF16), jax.ShapeDtypeStruct((S, 1024), BF16)],
        compiler_params=_params(("parallel",)),
    )(proj, proj, proj, proj, o, x, bg, convw, mqg, mk, mv, wba, wbc, wbm, wo)


def _loss_head(xf, tgt):
    S = xf.shape[0]
    tm = _tile(S, 1024)

    def body(x_ref, t_ref, g_ref, acc_ref):
        @pl.when(pl.program_id(0) == 0)
        def _():
            acc_ref[...] = jnp.zeros(acc_ref.shape, F32)

        e = x_ref[...] - t_ref[...]
        g_ref[...] = e * (1.0 / D_MODEL)
        part = jnp.sum((e * e).reshape(tm // 8, 8, D_MODEL), axis=0)
        tot = part[:, 0:128]
        for k in range(1, D_MODEL // 128):
            tot = tot + part[:, 128 * k:128 * (k + 1)]
        acc_ref[...] += tot

    row = pl.BlockSpec((tm, D_MODEL), lambda i: (i, 0))
    return pl.pallas_call(
        body, name="loss_head", grid=(S // tm,),
        in_specs=[row, row],
        out_specs=[row, pl.BlockSpec((8, 128), lambda i: (0, 0))],
        out_shape=[jax.ShapeDtypeStruct((S, D_MODEL), F32), jax.ShapeDtypeStruct((8, 128), F32)],
        compiler_params=_params(("arbitrary",)),
    )(xf, tgt)


def _merge_bwd(g, proj, o, u, bg, convw, mqg, mk, mv, wot, wbat, wbct, wbmt):
    S = g.shape[0]
    tm = _tile(S, 256)
    nt = S // tm
    M = mk.shape[0]

    def body(g_ref, main_ref, ccu_ref, hp_ref, hn_ref, o_ref, u_ref, bg_ref, cw_ref, mqg_ref, mk_ref, mv_ref,
             wot_ref, wbat_ref, wbct_ref, wbmt_ref,
             dmain_ref, dcv_ref, do_ref, delta_ref, du_ref, dbg_ref, dmk_ref, dmv_ref, dmqg_ref):
        i = pl.program_id(0)

        @pl.when(i == 0)
        def _():
            dbg_ref[...] = jnp.zeros(dbg_ref.shape, F32)
            dmk_ref[...] = jnp.zeros(dmk_ref.shape, F32)
            dmv_ref[...] = jnp.zeros(dmv_ref.shape, F32)
            dmqg_ref[...] = jnp.zeros(dmqg_ref.shape, F32)

        dy = _dot(g_ref[...].astype(BF16), wot_ref[...])
        rg = _sigmoid(main_ref[:, O_R:O_R + 3072] + bg_ref[...])
        dyt = jnp.concatenate([dy, dy, dy], axis=1)
        dr = dyt * u_ref[...].astype(F32) * rg * (1.0 - rg)
        dmain_ref[:, O_R:O_R + 3072] = dr.astype(BF16)
        dbg_ref[...] += _colsum(dr)
        du = (dyt * rg).astype(BF16)
        du_ref[...] = du
        do_a = _dot(du[:, 0:1024], wbat_ref[...])
        do_c = _dot(du[:, 1024:2048], wbct_ref[...])
        do_m = _dot(du[:, 2048:3072], wbmt_ref[...])

        sil_a, dsil_a = _silu_and_grad(main_ref[:, O_GA:O_GA + 1024])
        ov = o_ref[...]
        d_o = do_a * sil_a
        do_ref[...] = d_o.astype(BF16)
        dmain_ref[:, O_GA:O_GA + 1024] = (do_a * ov * dsil_a).astype(BF16)
        prod = d_o * ov
        for h in range(N_HEADS):
            delta_ref[h] = jnp.sum(prod[:, h * HP:(h + 1) * HP], axis=-1, keepdims=True)

        z, z_up, z_dn = _conv_shifts(ccu_ref[:, 0:CONV_W], ccu_ref[:, CONV_W:], hp_ref, hn_ref, i, nt, tm)
        cv = cw_ref[0:1, :] * z_up + cw_ref[1:2, :] * z + cw_ref[2:3, :] * z_dn + cw_ref[3:4, :]
        sil_c, dsil_c = _silu_and_grad(main_ref[:, O_GC:O_GC + CONV_W])
        cb = main_ref[:, O_CB:O_CB + CONV_W]
        dmain_ref[:, O_CB:O_CB + CONV_W] = (do_c * cv * sil_c).astype(BF16)
        dmain_ref[:, O_GC:O_GC + CONV_W] = (do_c * cb * cv * dsil_c).astype(BF16)
        dcv_ref[...] = do_c * cb * sil_c

        sil_m, dsil_m = _silu_and_grad(main_ref[:, O_GM:O_GM + MEM_W])
        for h in range(MEM_HEADS):
            sl = slice(h * MEM_HD, (h + 1) * MEM_HD)
            qm = main_ref[:, O_QM + h * MEM_HD:O_QM + (h + 1) * MEM_HD]
            mk_h, mv_h = mk_ref[:, sl], mv_ref[:, sl]
            r, qhat, mq, p, pv = _mem_attend(qm, mqg_ref[...], mk_h, mv_h)
            dom = do_m[:, sl]
            dmain_ref[:, O_GM + h * MEM_HD:O_GM + (h + 1) * MEM_HD] = (dom * pv * dsil_m[:, sl]).astype(BF16)
            dpv = (dom * sil_m[:, sl]).astype(BF16)
            dp = _dot_nt(dpv, mv_h)
            ds = (p * (dp - jnp.sum(dp * p, axis=-1, keepdims=True)) * MEM_SCALE).astype(BF16)
            dmq = _dot(ds, mk_h)
            dmk_ref[:, sl] += _dot_tn(ds, mq)
            dmv_ref[:, sl] += _dot_tn(p.astype(BF16), dpv)
            dmqg_ref[...] += _colsum(dmq * qhat)
            dqm = _rms_bwd(dmq * mqg_ref[...], qhat, r, MEM_HD)
            dmain_ref[:, O_QM + h * MEM_HD:O_QM + (h + 1) * MEM_HD] = dqm.astype(BF16)

    row = lambda w: pl.BlockSpec((tm, w), lambda i: (i, 0))
    full = lambda a: pl.BlockSpec(a.shape, lambda i: (0,) * a.ndim)
    acc = lambda r, c: pl.BlockSpec((r, c), lambda i: (0, 0))
    hp, hn = _halo_specs(tm, S, 1024, 6)
    return pl.pallas_call(
        body, name="merge_bwd", grid=(nt,),
        in_specs=[row(1024), row(6144), pl.BlockSpec((tm, 1024), lambda i: (i, 6)), hp, hn, row(1024), row(3072),
                  full(bg), full(convw), full(mqg), full(mk), full(mv), full(wot), full(wbat), full(wbct), full(wbmt)],
        out_specs=[row(6144), row(CONV_W), row(1024), pl.BlockSpec((N_HEADS, tm, 1), lambda i: (0, i, 0)), row(3072),
                   acc(1, 3072), acc(M, MEM_W), acc(M, MEM_W), acc(1, MEM_HD)],
        out_shape=[jax.ShapeDtypeStruct((S, 6144), BF16), jax.ShapeDtypeStruct((S, CONV_W), F32),
                   jax.ShapeDtypeStruct((S, 1024), BF16), jax.ShapeDtypeStruct((N_HEADS, S, 1), F32),
                   jax.ShapeDtypeStruct((S, 3072), BF16), jax.ShapeDtypeStruct((1, 3072), F32),
                   jax.ShapeDtypeStruct((M, MEM_W), F32), jax.ShapeDtypeStruct((M, MEM_W), F32),
                   jax.ShapeDtypeStruct((1, MEM_HD), F32)],
        compiler_params=_params(("arbitrary",)),
    )(g, proj, proj, proj, proj, o, u, bg, convw, mqg, mk, mv, wot, wbat, wbct, wbmt)


def _conv_bwd(dcv, proj, convw):
    S = dcv.shape[0]
    tm = _tile(S, 512)
    nt = S // tm

    def body(d_ref, dp_ref, dn_ref, ccu_ref, hp_ref, hn_ref, cw_ref, dccu_ref, dcw_ref):
        i = pl.program_id(0)

        @pl.when(i == 0)
        def _():
            dcw_ref[...] = jnp.zeros(dcw_ref.shape, F32)

        cc, cu = ccu_ref[:, 0:CONV_W], ccu_ref[:, CONV_W:]
        z, z_up, z_dn = _conv_shifts(cc, cu, hp_ref, hn_ref, i, nt, tm)
        d = d_ref[...]
        dprev = jnp.where(i == 0, 0.0, dp_ref[7:8, :])
        dnext = jnp.where(i == nt - 1, 0.0, dn_ref[0:1, :])
        row = lax.broadcasted_iota(jnp.int32, d.shape, 0)
        d_up = jnp.where(row == 0, dprev, pltpu.roll(d, 1, 0))
        d_dn = jnp.where(row == tm - 1, dnext, pltpu.roll(d, tm - 1, 0))
        dz = cw_ref[0:1, :] * d_dn + cw_ref[1:2, :] * d + cw_ref[2:3, :] * d_up
        dccu_ref[:, 0:CONV_W] = (dz * cu).astype(BF16)
        dccu_ref[:, CONV_W:] = (dz * cc).astype(BF16)
        dcw_ref[0:1, :] += _colsum(d * z_up)
        dcw_ref[1:2, :] += _colsum(d * z)
        dcw_ref[2:3, :] += _colsum(d * z_dn)
        dcw_ref[3:4, :] += _colsum(d)

    hp, hn = _halo_specs(tm, S, 1024, 6)
    dp, dn = _halo_specs(tm, S, CONV_W, 0)
    return pl.pallas_call(
        body, name="conv_bwd", grid=(nt,),
        in_specs=[pl.BlockSpec((tm, CONV_W), lambda i: (i, 0)), dp, dn,
                  pl.BlockSpec((tm, 1024), lambda i: (i, 6)), hp, hn,
                  pl.BlockSpec((8, CONV_W), lambda i: (0, 0))],
        out_specs=[pl.BlockSpec((tm, 1024), lambda i: (i, 0)), pl.BlockSpec((8, CONV_W), lambda i: (0, 0))],
        out_shape=[jax.ShapeDtypeStruct((S, 1024), BF16), jax.ShapeDtypeStruct((8, CONV_W), F32)],
        compiler_params=_params(("arbitrary",)),
    )(dcv, dcv, dcv, proj, proj, proj, convw)


ATTN_BWD_TQ = 2048


def _attn_bwd(q, k, v, do, lse_r, delta_r):
    S = q.shape[0]
    tk = _tile(S, 512)
    nq, tq = lse_r.shape[1], lse_r.shape[3]

    def body(k_ref, v_ref, q_ref, do_ref, lse_ref, delta_ref, dk_ref, dv_ref, dq_ref, dk_s, dv_s):
        @pl.when(pl.program_id(1) == 0)
        def _():
            dq_ref[...] = jnp.zeros(dq_ref.shape, F32)

        dk_s[...] = jnp.zeros(dk_s.shape, F32)
        dv_s[...] = jnp.zeros(dv_s.shape, F32)
        kv, vv = k_ref[...], v_ref[...]

        def step(c, carry):
            rows = pl.ds(pl.multiple_of(c * tq, tq), tq)
            qv, dov = q_ref[rows, :], do_ref[rows, :]
            pt = jnp.exp(_dot_nt(kv, qv) - lse_ref[0, c])
            dv_s[...] += _dot(pt.astype(BF16), dov)
            dpt = _dot_nt(vv, dov)
            dst = (pt * (dpt - delta_ref[0, c])).astype(BF16)
            dk_s[...] += _dot(dst, qv)
            dq_ref[rows, :] += _dot_tn(dst, kv)
            return carry

        lax.fori_loop(0, nq, step, 0)
        dk_ref[...] = dk_s[...]
        dv_ref[...] = dv_s[...].astype(BF16)

    rowv = pl.BlockSpec((1, nq, 1, tq), lambda h, j: (h, 0, 0, 0))
    blk = pl.BlockSpec((tk, HP), lambda h, j: (j, h))
    res = pl.BlockSpec((S, HP), lambda h, j: (0, h))
    return pl.pallas_call(
        body, name="attn_bwd", grid=(N_HEADS, S // tk),
        in_specs=[blk, blk, res, res, rowv, rowv],
        out_specs=[blk, blk, res],
        out_shape=[jax.ShapeDtypeStruct((S, N_HEADS * HP), F32), jax.ShapeDtypeStruct((S, N_HEADS * HP), BF16),
                   jax.ShapeDtypeStruct((S, N_HEADS * HP), F32)],
        scratch_shapes=[pltpu.VMEM((tk, HP), F32), pltpu.VMEM((tk, HP), F32)],
        compiler_params=_params(("parallel", "arbitrary")),
    )(k, v, q, do, lse_r, delta_r)


def _mla_prep_bwd(proj, tabs, dq, dk, dv, qg, kvg, qhg, khg, wuq, wuk, wuqt, wukt, wuvt):
    S = proj.shape[0]
    tm = _tile(S, 256)

    def body(a_ref, c_ref, sa_ref, sb_ref, dq_ref, dk_ref, dv_ref, qg_ref, kvg_ref, qhg_ref, khg_ref,
             wuq_ref, wuk_ref, wuqt_ref, wukt_ref, wuvt_ref,
             da_ref, dqraw_ref, dkraw_ref, dqg_ref, dkvg_ref, dqhg_ref, dkhg_ref):
        @pl.when(pl.program_id(0) == 0)
        def _():
            dqg_ref[...] = jnp.zeros(dqg_ref.shape, F32)
            dkvg_ref[...] = jnp.zeros(dkvg_ref.shape, F32)
            dqhg_ref[...] = jnp.zeros(dqhg_ref.shape, F32)
            dkhg_ref[...] = jnp.zeros(dkhg_ref.shape, F32)

        ql = a_ref[:, 0:Q_RANK]
        kvl = a_ref[:, Q_RANK:Q_RANK + KV_RANK]
        kpe = a_ref[:, Q_RANK + KV_RANK:Q_RANK + KV_RANK + HP]
        rq, qhat = _rms_parts(ql)
        rkv, kvhat = _rms_parts(kvl)
        qraw = _dot((qhat * qg_ref[...]).astype(BF16), wuq_ref[...])
        kn = _dot((kvhat * kvg_ref[...]).astype(BF16), wuk_ref[...])
        c, sa, sb = c_ref[...], sa_ref[...], sb_ref[...]
        dkpe = jnp.zeros(kpe.shape, F32)
        dqhg = jnp.zeros((1, HP), F32)
        dkhg = jnp.zeros((1, HP), F32)
        for h in range(N_HEADS):
            sl = slice(h * HP, (h + 1) * HP)
            r, that = _rms_parts(qraw[:, sl], QK_DIM)
            dtn = _rope_t(dq_ref[:, sl], c, sa, sb) * SCALE
            dqhg = dqhg + _colsum(dtn * that)
            dqraw_ref[:, sl] = _rms_bwd(dtn * qhg_ref[...], that, r, QK_DIM).astype(BF16)
            r, that = _rms_parts(kn[:, sl] + kpe, QK_DIM)
            dtn = _rope_t(dk_ref[:, sl], c, sa, sb)
            dkhg = dkhg + _colsum(dtn * that)
            dkr = _rms_bwd(dtn * khg_ref[...], that, r, QK_DIM)
            dkraw_ref[:, sl] = dkr.astype(BF16)
            dkpe = dkpe + dkr
        dqhg_ref[...] += dqhg
        dkhg_ref[...] += dkhg
        dqn = _dot(dqraw_ref[...], wuqt_ref[...])
        dqg_ref[...] += _colsum(dqn * qhat)
        da_ref[:, 0:Q_RANK] = _rms_bwd(dqn * qg_ref[...], qhat, rq, Q_RANK).astype(BF16)
        dkvn = _dot(dkraw_ref[...], wukt_ref[...]) + _dot(dv_ref[...], wuvt_ref[...])
        dkvg_ref[...] += _colsum(dkvn * kvhat)
        da_ref[:, Q_RANK:Q_RANK + KV_RANK] = _rms_bwd(dkvn * kvg_ref[...], kvhat, rkv, KV_RANK).astype(BF16)
        da_ref[:, Q_RANK + KV_RANK:Q_RANK + KV_RANK + HP] = dkpe.astype(BF16)
        da_ref[:, Q_RANK + KV_RANK + HP:] = jnp.zeros((tm, 1024 - Q_RANK - KV_RANK - HP), BF16)

    row = lambda w: pl.BlockSpec((tm, w), lambda i: (i, 0))
    full = lambda a: pl.BlockSpec(a.shape, lambda i: (0,) * a.ndim)
    acc = lambda c: pl.BlockSpec((1, c), lambda i: (0, 0))
    return pl.pallas_call(
        body, name="mla_prep_bwd", grid=(S // tm,),
        in_specs=[pl.BlockSpec((tm, 1024), lambda i: (i, 7)), row(HP), row(HP), row(HP),
                  row(1024), row(1024), row(1024), full(qg), full(kvg), full(qhg), full(khg),
                  full(wuq), full(wuk), full(wuqt), full(wukt), full(wuvt)],
        out_specs=[row(1024), row(1024), row(1024), acc(Q_RANK), acc(KV_RANK), acc(HP), acc(HP)],
        out_shape=[jax.ShapeDtypeStruct((S, 1024), BF16)] * 3
        + [jax.ShapeDtypeStruct((1, Q_RANK), F32), jax.ShapeDtypeStruct((1, KV_RANK), F32),
           jax.ShapeDtypeStruct((1, HP), F32), jax.ShapeDtypeStruct((1, HP), F32)],
        compiler_params=_params(("arbitrary",)),
    )(proj, *tabs, dq, dk, dv, qg, kvg, qhg, khg, wuq, wuk, wuqt, wukt, wuvt)


def _inproj_bwd(dproj, wint, x, g, ng):
    S = x.shape[0]
    tm, tk = _tile(S, 1024), 2048
    nk = PW // tk

    def body(dp_ref, w_ref, x_ref, g_ref, ng_ref, dx_ref, dng_ref, acc_s):
        i, k = pl.program_id(0), pl.program_id(1)

        @pl.when((i == 0) & (k == 0))
        def _():
            dng_ref[...] = jnp.zeros(dng_ref.shape, F32)

        @pl.when(k == 0)
        def _():
            acc_s[...] = jnp.zeros(acc_s.shape, F32)

        acc_s[...] += _dot(dp_ref[...], w_ref[...])

        @pl.when(k == nk - 1)
        def _():
            r, xhat = _rms_parts(x_ref[...])
            dh = acc_s[...]
            dng_ref[...] += _colsum(dh * xhat)
            dx_ref[...] = g_ref[...] + _rms_bwd(dh * ng_ref[...], xhat, r, D_MODEL)

    return pl.pallas_call(
        body, name="inproj_bwd", grid=(S // tm, nk),
        in_specs=[pl.BlockSpec((tm, tk), lambda i, k: (i, k)),
                  pl.BlockSpec((tk, D_MODEL), lambda i, k: (k, 0)),
                  pl.BlockSpec((tm, D_MODEL), lambda i, k: (i, 0)),
                  pl.BlockSpec((tm, D_MODEL), lambda i, k: (i, 0)),
                  pl.BlockSpec((1, D_MODEL), lambda i, k: (0, 0))],
        out_specs=[pl.BlockSpec((tm, D_MODEL), lambda i, k: (i, 0)),
                   pl.BlockSpec((1, D_MODEL), lambda i, k: (0, 0))],
        out_shape=[jax.ShapeDtypeStruct((S, D_MODEL), F32), jax.ShapeDtypeStruct((1, D_MODEL), F32)],
        scratch_shapes=[pltpu.VMEM((tm, D_MODEL), F32)],
        compiler_params=_params(("arbitrary", "arbitrary")),
    )(dproj, wint, x, g, ng)


def _memkv_bwd(mem, mng, mkg, wmkv, wmkvt, dmk, dmv):
    M = mem.shape[0]

    def body(mem_ref, mng_ref, mkg_ref, w_ref, wt_ref, dmk_ref, dmv_ref, dw_ref, dmng_ref, dmkg_ref, d_s):
        r, mhat = _rms_parts(mem_ref[...])
        mn = (mhat * mng_ref[...]).astype(BF16)
        mkv = _dot(mn, w_ref[...])
        dmkg = jnp.zeros((1, MEM_HD), F32)
        for h in range(MEM_HEADS):
            sl = slice(h * MEM_HD, (h + 1) * MEM_HD)
            rk, khat = _rms_parts(mkv[:, 2 * MEM_HD * h:2 * MEM_HD * h + MEM_HD])
            dkn = dmk_ref[:, sl]
            dmkg = dmkg + _colsum(dkn * khat)
            d_s[:, 2 * MEM_HD * h:2 * MEM_HD * h + MEM_HD] = _rms_bwd(dkn * mkg_ref[...], khat, rk, MEM_HD).astype(BF16)
            d_s[:, 2 * MEM_HD * h + MEM_HD:2 * MEM_HD * (h + 1)] = dmv_ref[:, sl].astype(BF16)
        dmkg_ref[...] = dmkg
        dw_ref[...] = _dot_tn(mn, d_s[...])
        dmn = _dot(d_s[...], wt_ref[...])
        dmng_ref[...] = _colsum(dmn * mhat)

    return pl.pallas_call(
        body, name="memkv_bwd",
        out_shape=[jax.ShapeDtypeStruct((D_MODEL, 2 * MEM_W), F32), jax.ShapeDtypeStruct((1, D_MODEL), F32),
                   jax.ShapeDtypeStruct((1, MEM_HD), F32)],
        scratch_shapes=[pltpu.VMEM((M, 2 * MEM_W), BF16)],
        compiler_params=pltpu.CompilerParams(vmem_limit_bytes=VMEM_LIMIT_V7X),
    )(mem, mng, mkg, wmkv, wmkvt, dmk, dmv)


def _mm_tn(a, b, name):
    S, M = a.shape
    N = b.shape[1]
    tm, tn, ts = _tile(M, 1024), _tile(N, 1024), _tile(S, 512)

    def body(a_ref, b_ref, o_ref):
        @pl.when(pl.program_id(2) == 0)
        def _():
            o_ref[...] = jnp.zeros(o_ref.shape, F32)

        o_ref[...] += _dot_tn(a_ref[...].astype(BF16), b_ref[...].astype(BF16))

    return pl.pallas_call(
        body, name=name, grid=(M // tm, N // tn, S // ts),
        in_specs=[pl.BlockSpec((ts, tm), lambda i, j, k: (k, i)),
                  pl.BlockSpec((ts, tn), lambda i, j, k: (k, j))],
        out_specs=pl.BlockSpec((tm, tn), lambda i, j, k: (i, j)),
        out_shape=jax.ShapeDtypeStruct((M, N), F32),
        compiler_params=_params(("parallel", "parallel", "arbitrary")),
    )(a, b)


def _adamw(w, g0, g1, m, v, name):
    R, C = w.shape
    tr = R
    for cand in (512, 256, 128, 64, 32, 16, 8):
        if R % cand == 0 and cand * C * 4 <= (1 << 20):
            tr = cand
            break
    c1 = 1.0 / (1.0 - ADAM_B1 ** ADAM_STEP)
    c2 = 1.0 / (1.0 - ADAM_B2 ** ADAM_STEP)

    def body(w_ref, g0_ref, g1_ref, m_ref, v_ref, g_ref, d_ref, nm_ref, nv_ref):
        g = g0_ref[...] + g1_ref[...]
        nm = ADAM_B1 * m_ref[...] + (1.0 - ADAM_B1) * g
        nv = ADAM_B2 * v_ref[...] + (1.0 - ADAM_B2) * (g * g)
        g_ref[...] = g
        nm_ref[...] = nm
        nv_ref[...] = nv
        d_ref[...] = -ADAM_LR * ((nm * c1) / (jnp.sqrt(nv * c2) + ADAM_EPS) + ADAM_WD * w_ref[...])

    blk = pl.BlockSpec((tr, C), lambda i: (i, 0))
    return pl.pallas_call(
        body, name=name, grid=(R // tr,),
        in_specs=[blk] * 5, out_specs=[blk] * 4,
        out_shape=[jax.ShapeDtypeStruct((R, C), F32)] * 4,
        compiler_params=_params(("parallel",)),
    )(w, g0, g1, m, v)


def _sum_slabs(a, name):
    K, R, C = a.shape
    tr = R
    for cand in (512, 256, 128, 64, 32, 16, 8):
        if R % cand == 0 and cand * C * 4 * K <= (4 << 20):
            tr = cand
            break

    def body(a_ref, o_ref):
        t = a_ref[0].astype(F32)
        for k in range(1, K):
            t = t + a_ref[k].astype(F32)
        o_ref[...] = t

    return pl.pallas_call(
        body, name=name, grid=(R // tr,),
        in_specs=[pl.BlockSpec((K, tr, C), lambda i: (0, i, 0))],
        out_specs=pl.BlockSpec((tr, C), lambda i: (i, 0)),
        out_shape=jax.ShapeDtypeStruct((R, C), F32),
        compiler_params=_params(("parallel",)),
    )(a)


MESH = pl.DeviceIdType.MESH
HBM = pl.BlockSpec(memory_space=pltpu.HBM)


def _chip_peer(x, y, c, k):
    return (x ^ (k >> 1), y ^ (k & 1), c)


def _gather_chips(arrs):
    n = len(arrs)

    def body(*refs):
        ins, outs = refs[:n], refs[n:2 * n]
        send, recv, loc = refs[2 * n:]
        x, y, c = lax.axis_index("x"), lax.axis_index("y"), lax.axis_index("c")
        me = 2 * x + y
        copies = []
        for a in range(n):
            own = pltpu.make_async_copy(ins[a], outs[a].at[me], loc.at[a])
            own.start()
            copies.append(own)
        for a in range(n):
            for k in (1, 2, 3):
                cp = pltpu.make_async_remote_copy(
                    src_ref=ins[a], dst_ref=outs[a].at[me], send_sem=send.at[3 * a + k - 1],
                    recv_sem=recv.at[3 * a + k - 1], device_id=_chip_peer(x, y, c, k), device_id_type=MESH)
                cp.start()
                copies.append(cp)
        for cp in copies:
            cp.wait()

    return pl.pallas_call(
        body, name="gather_weights",
        in_specs=[HBM] * n, out_specs=[HBM] * n,
        out_shape=[jax.ShapeDtypeStruct((4,) + a.shape, a.dtype) for a in arrs],
        scratch_shapes=[pltpu.SemaphoreType.DMA((3 * n,)), pltpu.SemaphoreType.DMA((3 * n,)),
                        pltpu.SemaphoreType.DMA((n,))],
    )(*arrs)


def _scatter_chips(arrs, small):
    n = len(arrs)

    def body(*refs):
        ins, small_in = refs[:n], refs[n]
        outs, small_out = refs[n + 1:2 * n + 1], refs[2 * n + 1]
        send, recv, loc, ssend, srecv = refs[2 * n + 2:]
        x, y, c = lax.axis_index("x"), lax.axis_index("y"), lax.axis_index("c")
        me = 2 * x + y
        me8 = 4 * x + 2 * y + c
        copies = []
        for a in range(n):
            own = pltpu.make_async_copy(ins[a].at[me], outs[a].at[me], loc.at[a])
            own.start()
            copies.append(own)
        own = pltpu.make_async_copy(small_in, small_out.at[me8], loc.at[n])
        own.start()
        copies.append(own)
        for k in range(1, 8):
            cp = pltpu.make_async_remote_copy(
                src_ref=small_in, dst_ref=small_out.at[me8], send_sem=ssend.at[k - 1], recv_sem=srecv.at[k - 1],
                device_id=(x ^ (k >> 2), y ^ ((k >> 1) & 1), c ^ (k & 1)), device_id_type=MESH)
            cp.start()
            copies.append(cp)
        for a in range(n):
            for k in (1, 2, 3):
                cp = pltpu.make_async_remote_copy(
                    src_ref=ins[a].at[me ^ k], dst_ref=outs[a].at[me], send_sem=send.at[3 * a + k - 1],
                    recv_sem=recv.at[3 * a + k - 1], device_id=_chip_peer(x, y, c, k), device_id_type=MESH)
                cp.start()
                copies.append(cp)
        for cp in copies:
            cp.wait()

    return pl.pallas_call(
        body, name="scatter_grads",
        in_specs=[HBM] * (n + 1), out_specs=[HBM] * (n + 1),
        out_shape=[jax.ShapeDtypeStruct(a.shape, a.dtype) for a in arrs]
        + [jax.ShapeDtypeStruct((8,) + small.shape, small.dtype)],
        scratch_shapes=[pltpu.SemaphoreType.DMA((3 * n,)), pltpu.SemaphoreType.DMA((3 * n,)),
                        pltpu.SemaphoreType.DMA((n + 1,)), pltpu.SemaphoreType.DMA((7,)),
                        pltpu.SemaphoreType.DMA((7,))],
    )(*arrs, small)


def _swap_cores(arrs):
    n = len(arrs)

    def body(*refs):
        ins, outs = refs[:n], refs[n:2 * n]
        send, recv = refs[2 * n:]
        x, y, c = lax.axis_index("x"), lax.axis_index("y"), lax.axis_index("c")
        copies = []
        for a in range(n):
            cp = pltpu.make_async_remote_copy(
                src_ref=ins[a], dst_ref=outs[a], send_sem=send.at[a], recv_sem=recv.at[a],
                device_id=(x, y, 1 - c), device_id_type=MESH)
            cp.start()
            copies.append(cp)
        for cp in copies:
            cp.wait()

    return pl.pallas_call(
        body, name="swap_cores",
        in_specs=[HBM] * n, out_specs=[HBM] * n,
        out_shape=[jax.ShapeDtypeStruct(a.shape, a.dtype) for a in arrs],
        scratch_shapes=[pltpu.SemaphoreType.DMA((n,)), pltpu.SemaphoreType.DMA((n,))],
    )(*arrs)


def _pad_last(a, n):
    return jnp.pad(a, [(0, 0)] * (a.ndim - 1) + [(0, n - a.shape[-1])])


def _pad_w_in(w):
    lead = w.shape[:-1]
    seg = lambda a, b: w[..., a:b]
    ga = _pad_last(seg(2720, 3232).reshape(lead + (N_HEADS, V_DIM)), HP).reshape(lead + (1024,))
    kpe = jnp.pad(seg(640, 672), [(0, 0)] * len(lead) + [(NOPE_DIM, HP - QK_DIM)])
    zero = jnp.zeros(lead + (PW - 7936,), w.dtype)
    return jnp.concatenate(
        [seg(4256, 7328), ga, seg(672, 1184), seg(2208, 2720), seg(3232, 3744), seg(3744, 4256),
         seg(1184, 1696), seg(1696, 2208), seg(0, 384), seg(384, 640), kpe, zero], axis=-1)


def _unpad_w_in(w):
    lead = w.shape[:-1]
    seg = lambda a, n: w[..., a:a + n]
    ga = seg(O_GA, 1024).reshape(lead + (N_HEADS, HP))[..., :V_DIM].reshape(lead + (N_HEADS * V_DIM,))
    return jnp.concatenate(
        [seg(O_QL, 384), seg(O_KVL, 256), seg(O_KPE + NOPE_DIM, ROPE_DIM), seg(O_CB, 512), seg(O_CC, 512),
         seg(O_CU, 512), seg(O_QM, 512), ga, seg(O_GC, 512), seg(O_GM, 512), seg(O_R, 3072)], axis=-1)


def _cols_from_shards(g):
    _, L, R, C = g.shape
    return jnp.transpose(g, (1, 2, 0, 3)).reshape(L, R, 4 * C)


def _cols_to_shards(w):
    L, R, C4 = w.shape
    return jnp.transpose(w.reshape(L, R, 4, C4 // 4), (2, 0, 1, 3))


def _rows_from_shards(g):
    _, L, R, C = g.shape
    return jnp.transpose(g, (1, 0, 2, 3)).reshape(L, 4 * R, C)


def _rows_to_shards(w):
    L, R4, C = w.shape
    return jnp.transpose(w.reshape(L, 4, R4 // 4, C), (1, 0, 2, 3))


def _t(w):
    return jnp.swapaxes(w, -1, -2)


def _layer_fwd(x, mem, tabs, p):
    proj, xn = _inproj(x, p["norm_g"], p["w_in"])
    q, k, v, qn, kvn = _mla_prep(proj, tabs, p["q_norm_g"], p["kv_norm_g"], p["q_head_g"], p["k_head_g"],
                                 p["w_uq"], p["w_uk"], p["w_uv"])
    o, lse = _attn_fwd(q, k, v)
    mk, mv = _memkv(mem, p["mem_norm_g"], p["mem_k_g"], p["w_mkv"])
    x_new, oa, oc, om, u, y = _merge(proj, o, x, p["b_gate"], p["conv_wb"], p["mem_q_g"], mk, mv,
                                     p["w_br_attn"], p["w_br_conv"], p["w_br_mem"], p["w_out"])
    saved = dict(x=x, proj=proj, xn=xn, q=q, k=k, v=v, qn=qn, kvn=kvn, o=o, lse=lse, mk=mk, mv=mv,
                 oa=oa, oc=oc, om=om, u=u, y=y)
    return x_new, saved


def _layer_bwd(g, mem, tabs, p, s):
    S = g.shape[0]
    tq = _tile(S, ATTN_BWD_TQ)
    dmain, dcv, d_o, delta, du, dbg, dmk, dmv, dmqg = _merge_bwd(
        g, s["proj"], s["o"], s["u"], p["b_gate"], p["conv_wb"], p["mem_q_g"], s["mk"], s["mv"],
        p["w_out_t"], p["w_br_attn_t"], p["w_br_conv_t"], p["w_br_mem_t"])
    dccu, dconv = _conv_bwd(dcv, s["proj"], p["conv_wb"])
    as_rows = lambda t: t.reshape(N_HEADS, S // tq, 1, tq)
    dk, dv, dq = _attn_bwd(s["q"], s["k"], s["v"], d_o, as_rows(s["lse"]), as_rows(delta))
    dsega, dqraw, dkraw, dqg, dkvg, dqhg, dkhg = _mla_prep_bwd(
        s["proj"], tabs, dq, dk, dv, p["q_norm_g"], p["kv_norm_g"], p["q_head_g"], p["k_head_g"],
        p["w_uq"], p["w_uk"], p["w_uq_t"], p["w_uk_t"], p["w_uv_t"])
    dproj = jnp.concatenate([dmain, dccu, dsega], axis=1)
    dx, dng = _inproj_bwd(dproj, p["w_in_t"], s["x"], g, p["norm_g"])
    dwmkv, dmng, dmkg = _memkv_bwd(mem, p["mem_norm_g"], p["mem_k_g"], p["w_mkv"], p["w_mkv_t"], dmk, dmv)
    grads = dict(
        norm_g=dng, b_gate=dbg, q_norm_g=dqg, kv_norm_g=dkvg, q_head_g=dqhg, k_head_g=dkhg,
        conv_wb=dconv, mem_norm_g=dmng, mem_q_g=dmqg, mem_k_g=dmkg, w_mkv=dwmkv,
        w_in=_mm_tn(s["xn"], dproj, "grad_w_in"),
        w_uq=_mm_tn(s["qn"], dqraw, "grad_w_uq"),
        w_uk=_mm_tn(s["kvn"], dkraw, "grad_w_uk"),
        w_uv=_mm_tn(s["kvn"], dv, "grad_w_uv"),
        w_br_attn=_mm_tn(s["oa"], du[:, 0:1024], "grad_w_br_attn"),
        w_br_conv=_mm_tn(s["oc"], du[:, 1024:2048], "grad_w_br_conv"),
        w_br_mem=_mm_tn(s["om"], du[:, 2048:3072], "grad_w_br_mem"),
        w_out=_mm_tn(s["y"], g, "grad_w_out"),
    )
    return dx, grads


def _layer_params(full, l):
    p = {}
    w_in = _pad_w_in(full["w_in"][l])
    w_uq = _pad_last(full["w_uq"][l].reshape(Q_RANK, N_HEADS, QK_DIM), HP).reshape(Q_RANK, 1024)
    ukv = full["w_ukv"][l].reshape(KV_RANK, N_HEADS, NOPE_DIM + V_DIM)
    w_uk = _pad_last(ukv[..., :NOPE_DIM], HP).reshape(KV_RANK, 1024)
    w_uv = _pad_last(ukv[..., NOPE_DIM:], HP).reshape(KV_RANK, 1024)
    w_ba = jnp.pad(full["w_br_attn"][l].reshape(N_HEADS, V_DIM, D_MODEL), ((0, 0), (0, HP - V_DIM), (0, 0)))
    w_ba = w_ba.reshape(1024, D_MODEL)
    p.update(w_in=w_in, w_uq=w_uq, w_uk=w_uk, w_uv=w_uv, w_br_attn=w_ba, w_br_conv=full["w_br_conv"][l],
             w_br_mem=full["w_br_mem"][l], w_out=full["w_out"][l], w_mkv=full["w_mkv"][l])
    for n in ("w_in", "w_uq", "w_uk", "w_uv", "w_br_attn", "w_br_conv", "w_br_mem", "w_out", "w_mkv"):
        p[n + "_t"] = _t(p[n])
    for n in ("norm_g", "b_gate", "q_norm_g", "kv_norm_g", "mem_norm_g", "mem_q_g", "mem_k_g"):
        p[n] = full[n][l][None, :]
    p["q_head_g"] = _pad_last(full["q_head_g"][l][None, :], HP)
    p["k_head_g"] = _pad_last(full["k_head_g"][l][None, :], HP)
    p["conv_wb"] = jnp.concatenate(
        [full["conv_w"][l], full["conv_b"][l][None, :], jnp.zeros((4, CONV_W), F32)], axis=0)
    return p


def _local_step(x, mem, positions, full, target):
    S = x.shape[0]
    invf16 = ROPE_BASE ** (-jnp.arange(0, ROPE_DIM, 2, dtype=F32) / ROPE_DIM)
    invf = jnp.concatenate([jnp.zeros((NOPE_DIM,), F32), invf16, invf16, jnp.zeros((HP - QK_DIM,), F32)])[None, :]
    tabs = _rope_tables(jnp.broadcast_to(positions.reshape(S, 1), (S, HP)), invf)
    params = [_layer_params(full, l) for l in range(DEPTH)]
    saved = []
    h = x
    for l in range(DEPTH):
        h, s = _layer_fwd(h, mem, tabs, params[l])
        saved.append(s)
    g, loss_part = _loss_head(h, target)
    per_layer = [None] * DEPTH
    for l in reversed(range(DEPTH)):
        g, per_layer[l] = _layer_bwd(g, mem, tabs, params[l], saved[l])
    st = lambda n: jnp.stack([per_layer[l][n] for l in range(DEPTH)])
    out = {}
    out["w_in"] = _unpad_w_in(st("w_in"))
    out["w_uq"] = st("w_uq").reshape(DEPTH, Q_RANK, N_HEADS, HP)[..., :QK_DIM].reshape(DEPTH, Q_RANK, N_HEADS * QK_DIM)
    duk = st("w_uk").reshape(DEPTH, KV_RANK, N_HEADS, HP)[..., :NOPE_DIM]
    duv = st("w_uv").reshape(DEPTH, KV_RANK, N_HEADS, HP)[..., :V_DIM]
    out["w_ukv"] = jnp.concatenate([duk, duv], axis=-1).reshape(DEPTH, KV_RANK, 1024)
    out["w_br_attn"] = st("w_br_attn").reshape(DEPTH, N_HEADS, HP, D_MODEL)[:, :, :V_DIM].reshape(DEPTH, 512, D_MODEL)
    for n in ("w_br_conv", "w_br_mem", "w_out", "w_mkv"):
        out[n] = st(n)
    for n in ("norm_g", "b_gate", "q_norm_g", "kv_norm_g", "mem_norm_g", "mem_q_g", "mem_k_g"):
        out[n] = st(n)[:, 0, :]
    out["q_head_g"] = st("q_head_g")[:, 0, :QK_DIM]
    out["k_head_g"] = st("k_head_g")[:, 0, :QK_DIM]
    cwb = st("conv_wb")
    out["conv_w"] = cwb[:, 0:3, :]
    out["conv_b"] = cwb[:, 3, :]
    return loss_part, g, out


_COL_SHARDED = ("w_in", "w_uq", "w_ukv", "w_br_attn", "w_br_conv", "w_br_mem")
_ROW_SHARDED = ("w_mkv", "w_out")
_BIG = _COL_SHARDED + _ROW_SHARDED
_SMALL = ("norm_g", "b_gate", "q_norm_g", "kv_norm_g", "q_head_g", "k_head_g", "conv_w", "conv_b",
          "mem_norm_g", "mem_q_g", "mem_k_g")
_ORDER = ("norm_g", "w_in", "b_gate", "q_norm_g", "w_uq", "kv_norm_g", "w_ukv", "q_head_g", "k_head_g",
          "conv_w", "conv_b", "mem_norm_g", "w_mkv", "mem_q_g", "mem_k_g", "w_br_attn", "w_br_conv",
          "w_br_mem", "w_out")


def _pack_small(d, extra):
    flat = jnp.concatenate([d[n].reshape(-1) for n in _SMALL] + [extra.reshape(-1)])
    n = flat.shape[0]
    rows = -(-n // 1024) * 8
    return jnp.pad(flat, (0, rows * 128 - n)).reshape(rows, 128)


def _unpack_small(packed, like):
    flat = packed.reshape(-1)
    out, off = {}, 0
    for n in _SMALL:
        sz = int(np.prod(like[n].shape))
        out[n] = flat[off:off + sz].reshape(like[n].shape)
        off += sz
    return out, flat[off:]


def kernel(x, mem, positions, norm_g, w_in, b_gate, q_norm_g, w_uq, kv_norm_g, w_ukv, q_head_g, k_head_g, conv_w, conv_b, mem_norm_g, w_mkv, mem_q_g, mem_k_g, w_br_attn, w_br_conv, w_br_mem, w_out, loss_target, m_norm_g, m_w_in, m_b_gate, m_q_norm_g, m_w_uq, m_kv_norm_g, m_w_ukv, m_q_head_g, m_k_head_g, m_conv_w, m_conv_b, m_mem_norm_g, m_w_mkv, m_mem_q_g, m_mem_k_g, m_w_br_attn, m_w_br_conv, m_w_br_mem, m_w_out, v_norm_g, v_w_in, v_b_gate, v_q_norm_g, v_w_uq, v_kv_norm_g, v_w_ukv, v_q_head_g, v_k_head_g, v_conv_w, v_conv_b, v_mem_norm_g, v_w_mkv, v_mem_q_g, v_mem_k_g, v_w_br_attn, v_w_br_conv, v_w_br_mem, v_w_out):
    w = dict(norm_g=norm_g, w_in=w_in, b_gate=b_gate, q_norm_g=q_norm_g, w_uq=w_uq, kv_norm_g=kv_norm_g,
             w_ukv=w_ukv, q_head_g=q_head_g, k_head_g=k_head_g, conv_w=conv_w, conv_b=conv_b,
             mem_norm_g=mem_norm_g, w_mkv=w_mkv, mem_q_g=mem_q_g, mem_k_g=mem_k_g, w_br_attn=w_br_attn,
             w_br_conv=w_br_conv, w_br_mem=w_br_mem, w_out=w_out)
    m = dict(norm_g=m_norm_g, w_in=m_w_in, b_gate=m_b_gate, q_norm_g=m_q_norm_g, w_uq=m_w_uq,
             kv_norm_g=m_kv_norm_g, w_ukv=m_w_ukv, q_head_g=m_q_head_g, k_head_g=m_k_head_g, conv_w=m_conv_w,
             conv_b=m_conv_b, mem_norm_g=m_mem_norm_g, w_mkv=m_w_mkv, mem_q_g=m_mem_q_g, mem_k_g=m_mem_k_g,
             w_br_attn=m_w_br_attn, w_br_conv=m_w_br_conv, w_br_mem=m_w_br_mem, w_out=m_w_out)
    v = dict(norm_g=v_norm_g, w_in=v_w_in, b_gate=v_b_gate, q_norm_g=v_q_norm_g, w_uq=v_w_uq,
             kv_norm_g=v_kv_norm_g, w_ukv=v_w_ukv, q_head_g=v_q_head_g, k_head_g=v_k_head_g, conv_w=v_conv_w,
             conv_b=v_conv_b, mem_norm_g=v_mem_norm_g, w_mkv=v_w_mkv, mem_q_g=v_mem_q_g, mem_k_g=v_mem_k_g,
             w_br_attn=v_w_br_attn, w_br_conv=v_w_br_conv, w_br_mem=v_w_br_mem, w_out=v_w_out)
    chip = 2 * lax.axis_index("x") + lax.axis_index("y")

    gathered = _gather_chips([w[n].astype(BF16) for n in _BIG] + [w["conv_w"]])
    full = {n: w[n] for n in _SMALL}
    for n, gth in zip(_BIG, gathered):
        full[n] = _cols_from_shards(gth) if n in _COL_SHARDED else _rows_from_shards(gth)
    full["conv_w"] = _cols_from_shards(gathered[-1])

    loss_part, grad_x, grads = _local_step(x[0], mem[0], positions[0], full, loss_target[0])

    to_owner = [(_cols_to_shards(grads[n]) if n in _COL_SHARDED else _rows_to_shards(grads[n])).astype(BF16)
                for n in _BIG]
    loss_vec = jnp.zeros((128,), F32).at[0].set(0.5 / D_MODEL * jnp.sum(loss_part))
    small = _pack_small(grads, loss_vec)
    scattered = _scatter_chips(to_owner, small)
    small_sum = _sum_slabs(scattered[-1], "sum_small")
    partial = []
    for n, slabs in zip(_BIG, scattered[:-1]):
        sh = slabs.shape
        partial.append(_sum_slabs(slabs.reshape(4, sh[1] * sh[2], sh[3]), "sum_" + n))
    other = _swap_cores(partial)

    small_g, tail = _unpack_small(small_sum, {n: (grads[n]) for n in _SMALL})
    loss = tail[0]
    small_g["conv_w"] = lax.dynamic_slice_in_dim(small_g["conv_w"], chip * (CONV_W // 4), CONV_W // 4, axis=2)

    outs_g, outs_d, outs_m, outs_v = {}, {}, {}, {}
    for n, p0, p1 in zip(_BIG, partial, other):
        shape = w[n].shape
        flat = lambda t: t.reshape(p0.shape)
        g_, d_, m_, v_ = _adamw(flat(w[n]), p0, p1, flat(m[n]), flat(v[n]), "adamw_" + n)
        outs_g[n], outs_d[n], outs_m[n], outs_v[n] = (t.reshape(shape) for t in (g_, d_, m_, v_))
    zero_small = jnp.zeros_like(small_sum)
    pk = lambda d: _pack_small(d, jnp.zeros((128,), F32))
    g_, d_, m_, v_ = _adamw(pk(w), _pack_small(small_g, jnp.zeros((128,), F32)), zero_small, pk(m), pk(v),
                            "adamw_small")
    like = {n: w[n] for n in _SMALL}
    for dst, packed in ((outs_g, g_), (outs_d, d_), (outs_m, m_), (outs_v, v_)):
        dst.update(_unpack_small(packed, like)[0])

    return (loss, grad_x[None], *[outs_g[n] for n in _ORDER], *[outs_d[n] for n in _ORDER],
            *[outs_m[n] for n in _ORDER], *[outs_v[n] for n in _ORDER])
```

```python
import functools

import numpy as np
import jax
import jax.numpy as jnp
from jax import lax
from jax.experimental import pallas as pl
from jax.experimental.pallas import tpu as pltpu

F32 = jnp.float32
BF16 = jnp.bfloat16

D_MODEL = 1024
DEPTH = 4
N_HEADS = 8
QK_DIM = 96
NOPE_DIM = 64
ROPE_DIM = 32
V_DIM = 64
Q_RANK = 384
KV_RANK = 256
CONV_W = 512
MEM_HEADS = 4
MEM_HD = 128
MEM_W = 512
IN_WIDTH = 7328
PW = 8192
HP = 128
PROJ_HALO = 16
F32_HALO = 8
EPS = 1e-6
ROPE_BASE = 10000.0
SCALE = QK_DIM ** -0.5
MEM_SCALE = MEM_HD ** -0.5

ADAM_LR = 0.001
ADAM_B1 = 0.9
ADAM_B2 = 0.999
ADAM_EPS = 1e-08
ADAM_WD = 0.01
ADAM_STEP = 10

VMEM_LIMIT_V7X = 56 * 1024 * 1024

O_R, O_GA, O_CB, O_QM, O_GC, O_GM, O_CC, O_CU, O_QL, O_KVL, O_KPE = (
    0, 3072, 4096, 4608, 5120, 5632, 6144, 6656, 7168, 7552, 7808)


def _params(sem, vmem=VMEM_LIMIT_V7X):
    return pltpu.CompilerParams(dimension_semantics=sem, vmem_limit_bytes=vmem)


def _sigmoid(t):
    return 1.0 / (1.0 + jnp.exp(-t))


def _silu_and_grad(g):
    sg = _sigmoid(g)
    return g * sg, sg * (1.0 + g * (1.0 - sg))


def _rms(t, g, n=None):
    n = t.shape[-1] if n is None else n
    r = lax.rsqrt(jnp.sum(t * t, axis=-1, keepdims=True) * (1.0 / n) + EPS)
    return (t * r) * g


def _rms_parts(t, n=None):
    n = t.shape[-1] if n is None else n
    r = lax.rsqrt(jnp.sum(t * t, axis=-1, keepdims=True) * (1.0 / n) + EPS)
    return r, t * r


def _rms_bwd(dhat, hat, r, n):
    return r * (dhat - hat * (jnp.sum(dhat * hat, axis=-1, keepdims=True) * (1.0 / n)))


def _rope(t, c, sa, sb):
    return t * c + pltpu.roll(t, HP - 16, 1) * sa + pltpu.roll(t, 16, 1) * sb


def _rope_t(d, c, sa, sb):
    return d * c + pltpu.roll(d * sa, 16, 1) + pltpu.roll(d * sb, HP - 16, 1)


def _dot(a, b):
    return jnp.dot(a, b, preferred_element_type=F32)


def _dot_nt(a, b):
    return lax.dot_general(a, b, (((1,), (1,)), ((), ())), preferred_element_type=F32)


def _dot_tn(a, b):
    return lax.dot_general(a, b, (((0,), (0,)), ((), ())), preferred_element_type=F32)


def _colsum(t):
    return jnp.sum(t, axis=0, keepdims=True)


def _tile(n, t):
    t = min(n, t)
    assert n % t == 0, (n, t)
    return t


def _rope_tables(pos_b, invf):
    S = pos_b.shape[0]
    tm = _tile(S, 1024)

    def body(pos_ref, invf_ref, c_ref, sa_ref, sb_ref):
        ang = pos_ref[...].astype(F32) * invf_ref[...]
        lane = lax.broadcasted_iota(jnp.int32, ang.shape, 1)
        cs = jnp.cos(ang)
        sn = jnp.sin(ang)
        c_ref[...] = jnp.where(lane < NOPE_DIM, 1.0, jnp.where(lane < QK_DIM, cs, 0.0))
        sa_ref[...] = jnp.where((lane >= NOPE_DIM) & (lane < NOPE_DIM + 16), -sn, 0.0)
        sb_ref[...] = jnp.where((lane >= NOPE_DIM + 16) & (lane < QK_DIM), sn, 0.0)

    blk = pl.BlockSpec((tm, HP), lambda i: (i, 0))
    return pl.pallas_call(
        body, name="rope_tables", grid=(S // tm,),
        in_specs=[blk, pl.BlockSpec((1, HP), lambda i: (0, 0))],
        out_specs=[blk, blk, blk],
        out_shape=[jax.ShapeDtypeStruct((S, HP), F32)] * 3,
        compiler_params=_params(("parallel",)),
    )(pos_b, invf)


def _inproj(x, g, w):
    S = x.shape[0]
    tm, tn = _tile(S, 1024), 2048

    def body(x_ref, g_ref, w_ref, proj_ref, xn_ref):
        @pl.when(pl.program_id(1) == 0)
        def _():
            xn_ref[...] = _rms(x_ref[...], g_ref[...]).astype(BF16)

        proj_ref[...] = _dot(xn_ref[...], w_ref[...]).astype(BF16)

    return pl.pallas_call(
        body, name="inproj", grid=(S // tm, PW // tn),
        in_specs=[pl.BlockSpec((tm, D_MODEL), lambda i, j: (i, 0)),
                  pl.BlockSpec((1, D_MODEL), lambda i, j: (0, 0)),
                  pl.BlockSpec((D_MODEL, tn), lambda i, j: (0, j))],
        out_specs=[pl.BlockSpec((tm, tn), lambda i, j: (i, j)),
                   pl.BlockSpec((tm, D_MODEL), lambda i, j: (i, 0))],
        out_shape=[jax.ShapeDtypeStruct((S, PW), BF16), jax.ShapeDtypeStruct((S, D_MODEL), BF16)],
        compiler_params=_params(("parallel", "arbitrary")),
    )(x, g, w)


def _mla_prep(proj, tabs, qg, kvg, qhg, khg, wuq, wuk, wuv):
    S = proj.shape[0]
    tm = _tile(S, 512)

    def body(a_ref, c_ref, sa_ref, sb_ref, qg_ref, kvg_ref, qhg_ref, khg_ref, wuq_ref, wuk_ref, wuv_ref,
             q_ref, k_ref, v_ref, qn_ref, kvn_ref):
        ql = a_ref[:, 0:Q_RANK].astype(F32)
        kvl = a_ref[:, Q_RANK:Q_RANK + KV_RANK].astype(F32)
        kpe = a_ref[:, Q_RANK + KV_RANK:Q_RANK + KV_RANK + HP].astype(F32)
        qn = _rms(ql, qg_ref[...]).astype(BF16)
        kvn = _rms(kvl, kvg_ref[...]).astype(BF16)
        qn_ref[...] = qn
        kvn_ref[...] = kvn
        qraw = _dot(qn, wuq_ref[...])
        kn = _dot(kvn, wuk_ref[...])
        v_ref[...] = _dot(kvn, wuv_ref[...]).astype(BF16)
        c, sa, sb = c_ref[...], sa_ref[...], sb_ref[...]
        for h in range(N_HEADS):
            sl = slice(h * HP, (h + 1) * HP)
            tq = _rms(qraw[:, sl], qhg_ref[...], QK_DIM)
            q_ref[:, sl] = (_rope(tq, c, sa, sb) * SCALE).astype(BF16)
            tk = _rms(kn[:, sl] + kpe, khg_ref[...], QK_DIM)
            k_ref[:, sl] = _rope(tk, c, sa, sb).astype(BF16)

    row = lambda w: pl.BlockSpec((tm, w), lambda i: (i, 0))
    full = lambda a: pl.BlockSpec(a.shape, lambda i: (0,) * a.ndim)
    return pl.pallas_call(
        body, name="mla_prep", grid=(S // tm,),
        in_specs=[pl.BlockSpec((tm, 1024), lambda i: (i, 7)), row(HP), row(HP), row(HP),
                  full(qg), full(kvg), full(qhg), full(khg), full(wuq), full(wuk), full(wuv)],
        out_specs=[row(1024), row(1024), row(1024), row(Q_RANK), row(KV_RANK)],
        out_shape=[jax.ShapeDtypeStruct((S, 1024), BF16)] * 3
        + [jax.ShapeDtypeStruct((S, Q_RANK), BF16), jax.ShapeDtypeStruct((S, KV_RANK), BF16)],
        compiler_params=_params(("parallel",)),
    )(proj, *tabs, qg, kvg, qhg, khg, wuq, wuk, wuv)


def _attn_fwd(q, k, v):
    S = q.shape[0]
    tq, tk = _tile(S, 1024), _tile(S, 2048)
    nk, nb = S // tk, tk // HP

    def body(q_ref, k_ref, v_ref, o_ref, lse_ref, m_s, l_s, acc_s):
        m_s[...] = jnp.full(m_s.shape, -jnp.inf, F32)
        l_s[...] = jnp.zeros(l_s.shape, F32)
        acc_s[...] = jnp.zeros(acc_s.shape, F32)
        qv = q_ref[...]

        def step(c, carry):
            rows = pl.ds(pl.multiple_of(c * tk, tk), tk)
            s = _dot_nt(qv, k_ref[rows, :])
            cm = s[:, 0:HP]
            for j in range(1, nb):
                cm = jnp.maximum(cm, s[:, j * HP:(j + 1) * HP])
            m_prev = m_s[...]
            m_new = jnp.maximum(m_prev, jnp.max(cm, axis=-1, keepdims=True))
            alpha = jnp.exp(m_prev - m_new)
            lsum = alpha * l_s[...]
            ps = []
            for j in range(nb):
                pj = jnp.exp(s[:, j * HP:(j + 1) * HP] - m_new)
                lsum = lsum + pj
                ps.append(pj.astype(BF16))
            l_s[...] = lsum
            acc_s[...] = alpha * acc_s[...] + _dot(jnp.concatenate(ps, axis=1), v_ref[rows, :])
            m_s[...] = m_new
            return carry

        lax.fori_loop(0, nk, step, 0)
        l = jnp.sum(l_s[...], axis=-1, keepdims=True)
        o_ref[...] = acc_s[...] / l
        lse_ref[0] = m_s[:, 0:1] + jnp.log(l)

    return pl.pallas_call(
        body, name="attn_fwd", grid=(N_HEADS, S // tq),
        in_specs=[pl.BlockSpec((tq, HP), lambda h, i: (i, h)),
                  pl.BlockSpec((S, HP), lambda h, i: (0, h)),
                  pl.BlockSpec((S, HP), lambda h, i: (0, h))],
        out_specs=[pl.BlockSpec((tq, HP), lambda h, i: (i, h)),
                   pl.BlockSpec((1, tq, 1), lambda h, i: (h, i, 0))],
        out_shape=[jax.ShapeDtypeStruct((S, N_HEADS * HP), F32),
                   jax.ShapeDtypeStruct((N_HEADS, S, 1), F32)],
        scratch_shapes=[pltpu.VMEM((tq, HP), F32), pltpu.VMEM((tq, HP), F32), pltpu.VMEM((tq, HP), F32)],
        compiler_params=_params(("parallel", "parallel")),
    )(q, k, v)


def _memkv(mem, mng, mkg, wmkv):
    M = mem.shape[0]

    def body(mem_ref, mng_ref, mkg_ref, w_ref, mk_ref, mv_ref):
        mn = _rms(mem_ref[...], mng_ref[...]).astype(BF16)
        mkv = _dot(mn, w_ref[...])
        for h in range(MEM_HEADS):
            kraw = mkv[:, 2 * MEM_HD * h:2 * MEM_HD * h + MEM_HD]
            mk_ref[:, MEM_HD * h:MEM_HD * (h + 1)] = _rms(kraw, mkg_ref[...]).astype(BF16)
            mv_ref[:, MEM_HD * h:MEM_HD * (h + 1)] = mkv[:, 2 * MEM_HD * h + MEM_HD:2 * MEM_HD * (h + 1)].astype(BF16)

    return pl.pallas_call(
        body, name="memkv",
        out_shape=[jax.ShapeDtypeStruct((M, MEM_W), BF16)] * 2,
        compiler_params=pltpu.CompilerParams(vmem_limit_bytes=VMEM_LIMIT_V7X),
    )(mem, mng, mkg, wmkv)


def _conv_shifts(cc, cu, hp_ref, hn_ref, i, n_tiles, tm):
    z = cc * cu
    last = PROJ_HALO - 1
    zp = hp_ref[last:last + 1, 0:CONV_W].astype(F32) * hp_ref[last:last + 1, CONV_W:2 * CONV_W].astype(F32)
    zn = hn_ref[0:1, 0:CONV_W].astype(F32) * hn_ref[0:1, CONV_W:2 * CONV_W].astype(F32)
    zp = jnp.where(i == 0, 0.0, zp)
    zn = jnp.where(i == n_tiles - 1, 0.0, zn)
    row = lax.broadcasted_iota(jnp.int32, z.shape, 0)
    z_up = jnp.where(row == 0, zp, pltpu.roll(z, 1, 0))
    z_dn = jnp.where(row == tm - 1, zn, pltpu.roll(z, tm - 1, 0))
    return z, z_up, z_dn


def _halo_specs(tm, S, width, col, rows):
    per = tm // rows
    prev = pl.BlockSpec((rows, width), lambda i: (jnp.maximum(i * per - 1, 0), col))
    nxt = pl.BlockSpec((rows, width), lambda i: (jnp.minimum((i + 1) * per, S // rows - 1), col))
    return prev, nxt


def _mem_attend(qm, mqg, mk_h, mv_h):
    r, qhat = _rms_parts(qm)
    mq = (qhat * mqg).astype(BF16)
    s = _dot_nt(mq, mk_h) * MEM_SCALE
    e = jnp.exp(s - jnp.max(s, axis=-1, keepdims=True))
    p = e / jnp.sum(e, axis=-1, keepdims=True)
    pv = _dot(p.astype(BF16), mv_h)
    return r, qhat, mq, p, pv


def _merge(proj, o, x, bg, convw, mqg, mk, mv, wba, wbc, wbm, wo):
    S = x.shape[0]
    tm = _tile(S, 256)
    nt = S // tm

    def body(main_ref, ccu_ref, hp_ref, hn_ref, o_ref, x_ref, bg_ref, cw_ref, mqg_ref, mk_ref, mv_ref,
             wba_ref, wbc_ref, wbm_ref, wo_ref, xn_ref, oa_ref, oc_ref, om_ref, u_ref, y_ref):
        i = pl.program_id(0)
        sil_a, _ = _silu_and_grad(main_ref[:, O_GA:O_GA + 1024].astype(F32))
        oa = (o_ref[...] * sil_a).astype(BF16)
        oa_ref[...] = oa
        z, z_up, z_dn = _conv_shifts(ccu_ref[:, 0:CONV_W].astype(F32), ccu_ref[:, CONV_W:].astype(F32), hp_ref, hn_ref, i, nt, tm)
        cv = cw_ref[0:1, :] * z_up + cw_ref[1:2, :] * z + cw_ref[2:3, :] * z_dn + cw_ref[3:4, :]
        sil_c, _ = _silu_and_grad(main_ref[:, O_GC:O_GC + CONV_W].astype(F32))
        oc = (main_ref[:, O_CB:O_CB + CONV_W].astype(F32) * cv * sil_c).astype(BF16)
        oc_ref[...] = oc
        sil_m, _ = _silu_and_grad(main_ref[:, O_GM:O_GM + MEM_W].astype(F32))
        for h in range(MEM_HEADS):
            sl = slice(h * MEM_HD, (h + 1) * MEM_HD)
            qm = main_ref[:, O_QM + h * MEM_HD:O_QM + (h + 1) * MEM_HD].astype(F32)
            pv = _mem_attend(qm, mqg_ref[...], mk_ref[:, sl], mv_ref[:, sl])[4]
            om_ref[:, sl] = (pv * sil_m[:, sl]).astype(BF16)
        ua = _dot(oa, wba_ref[...])
        uc = _dot(oc, wbc_ref[...])
        um = _dot(om_ref[...], wbm_ref[...])
        u_ref[:, 0:1024] = ua.astype(BF16)
        u_ref[:, 1024:2048] = uc.astype(BF16)
        u_ref[:, 2048:3072] = um.astype(BF16)
        rg = _sigmoid(main_ref[:, O_R:O_R + 3072].astype(F32) + bg_ref[...])
        y = (rg[:, 0:1024] * ua + rg[:, 1024:2048] * uc + rg[:, 2048:3072] * um).astype(BF16)
        y_ref[...] = y
        xn_ref[...] = x_ref[...] + _dot(y, wo_ref[...])

    row = lambda w: pl.BlockSpec((tm, w), lambda i: (i, 0))
    full = lambda a: pl.BlockSpec(a.shape, lambda i: (0,) * a.ndim)
    hp, hn = _halo_specs(tm, S, 1024, 6, PROJ_HALO)
    return pl.pallas_call(
        body, name="merge", grid=(nt,),
        in_specs=[row(6144), pl.BlockSpec((tm, 1024), lambda i: (i, 6)), hp, hn, row(1024), row(1024),
                  full(bg), full(convw), full(mqg), full(mk), full(mv), full(wba), full(wbc), full(wbm), full(wo)],
        out_specs=[row(1024), row(1024), row(CONV_W), row(MEM_W), row(3072), row(1024)],
        out_shape=[jax.ShapeDtypeStruct((S, 1024), F32), jax.ShapeDtypeStruct((S, 1024), BF16),
                   jax.ShapeDtypeStruct((S, CONV_W), BF16), jax.ShapeDtypeStruct((S, MEM_W), BF16),
                   jax.ShapeDtypeStruct((S, 3072), BF16), jax.ShapeDtypeStruct((S, 1024), BF16)],
        compiler_params=_params(("parallel",)),
    )(proj, proj, proj, proj, o, x, bg, convw, mqg, mk, mv, wba, wbc, wbm, wo)


def _loss_head(xf, tgt):
    S = xf.shape[0]
    tm = _tile(S, 1024)

    def body(x_ref, t_ref, g_ref, acc_ref):
        @pl.when(pl.program_id(0) == 0)
        def _():
            acc_ref[...] = jnp.zeros(acc_ref.shape, F32)

        e = x_ref[...] - t_ref[...]
        g_ref[...] = e * (1.0 / D_MODEL)
        part = jnp.sum((e * e).reshape(tm // 8, 8, D_MODEL), axis=0)
        tot = part[:, 0:128]
        for k in range(1, D_MODEL // 128):
            tot = tot + part[:, 128 * k:128 * (k + 1)]
        acc_ref[...] += tot

    row = pl.BlockSpec((tm, D_MODEL), lambda i: (i, 0))
    return pl.pallas_call(
        body, name="loss_head", grid=(S // tm,),
        in_specs=[row, row],
        out_specs=[row, pl.BlockSpec((8, 128), lambda i: (0, 0))],
        out_shape=[jax.ShapeDtypeStruct((S, D_MODEL), F32), jax.ShapeDtypeStruct((8, 128), F32)],
        compiler_params=_params(("arbitrary",)),
    )(xf, tgt)


def _merge_bwd(g, proj, o, u, bg, convw, mqg, mk, mv, wot, wbat, wbct, wbmt):
    S = g.shape[0]
    tm = _tile(S, 256)
    nt = S // tm
    M = mk.shape[0]

    def body(g_ref, main_ref, ccu_ref, hp_ref, hn_ref, o_ref, u_ref, bg_ref, cw_ref, mqg_ref, mk_ref, mv_ref,
             wot_ref, wbat_ref, wbct_ref, wbmt_ref,
             dmain_ref, dcv_ref, do_ref, delta_ref, du_ref, dbg_ref, dmk_ref, dmv_ref, dmqg_ref):
        i = pl.program_id(0)

        @pl.when(i == 0)
        def _():
            dbg_ref[...] = jnp.zeros(dbg_ref.shape, F32)
            dmk_ref[...] = jnp.zeros(dmk_ref.shape, F32)
            dmv_ref[...] = jnp.zeros(dmv_ref.shape, F32)
            dmqg_ref[...] = jnp.zeros(dmqg_ref.shape, F32)

        dy = _dot(g_ref[...].astype(BF16), wot_ref[...])
        rg = _sigmoid(main_ref[:, O_R:O_R + 3072].astype(F32) + bg_ref[...])
        dyt = jnp.concatenate([dy, dy, dy], axis=1)
        dr = dyt * u_ref[...].astype(F32) * rg * (1.0 - rg)
        dmain_ref[:, O_R:O_R + 3072] = dr.astype(BF16)
        dbg_ref[...] += _colsum(dr)
        du = (dyt * rg).astype(BF16)
        du_ref[...] = du
        do_a = _dot(du[:, 0:1024], wbat_ref[...])
        do_c = _dot(du[:, 1024:2048], wbct_ref[...])
        do_m = _dot(du[:, 2048:3072], wbmt_ref[...])

        sil_a, dsil_a = _silu_and_grad(main_ref[:, O_GA:O_GA + 1024].astype(F32))
        ov = o_ref[...]
        d_o = do_a * sil_a
        do_ref[...] = d_o.astype(BF16)
        dmain_ref[:, O_GA:O_GA + 1024] = (do_a * ov * dsil_a).astype(BF16)
        prod = d_o * ov
        for h in range(N_HEADS):
            delta_ref[h] = jnp.sum(prod[:, h * HP:(h + 1) * HP], axis=-1, keepdims=True)

        z, z_up, z_dn = _conv_shifts(ccu_ref[:, 0:CONV_W].astype(F32), ccu_ref[:, CONV_W:].astype(F32), hp_ref, hn_ref, i, nt, tm)
        cv = cw_ref[0:1, :] * z_up + cw_ref[1:2, :] * z + cw_ref[2:3, :] * z_dn + cw_ref[3:4, :]
        sil_c, dsil_c = _silu_and_grad(main_ref[:, O_GC:O_GC + CONV_W].astype(F32))
        cb = main_ref[:, O_CB:O_CB + CONV_W].astype(F32)
        dmain_ref[:, O_CB:O_CB + CONV_W] = (do_c * cv * sil_c).astype(BF16)
        dmain_ref[:, O_GC:O_GC + CONV_W] = (do_c * cb * cv * dsil_c).astype(BF16)
        dcv_ref[...] = do_c * cb * sil_c

        sil_m, dsil_m = _silu_and_grad(main_ref[:, O_GM:O_GM + MEM_W].astype(F32))
        for h in range(MEM_HEADS):
            sl = slice(h * MEM_HD, (h + 1) * MEM_HD)
            qm = main_ref[:, O_QM + h * MEM_HD:O_QM + (h + 1) * MEM_HD].astype(F32)
            mk_h, mv_h = mk_ref[:, sl], mv_ref[:, sl]
            r, qhat, mq, p, pv = _mem_attend(qm, mqg_ref[...], mk_h, mv_h)
            dom = do_m[:, sl]
            dmain_ref[:, O_GM + h * MEM_HD:O_GM + (h + 1) * MEM_HD] = (dom * pv * dsil_m[:, sl]).astype(BF16)
            dpv = (dom * sil_m[:, sl]).astype(BF16)
            dp = _dot_nt(dpv, mv_h)
            ds = (p * (dp - jnp.sum(dp * p, axis=-1, keepdims=True)) * MEM_SCALE).astype(BF16)
            dmq = _dot(ds, mk_h)
            dmk_ref[:, sl] += _dot_tn(ds, mq)
            dmv_ref[:, sl] += _dot_tn(p.astype(BF16), dpv)
            dmqg_ref[...] += _colsum(dmq * qhat)
            dqm = _rms_bwd(dmq * mqg_ref[...], qhat, r, MEM_HD)
            dmain_ref[:, O_QM + h * MEM_HD:O_QM + (h + 1) * MEM_HD] = dqm.astype(BF16)

    row = lambda w: pl.BlockSpec((tm, w), lambda i: (i, 0))
    full = lambda a: pl.BlockSpec(a.shape, lambda i: (0,) * a.ndim)
    acc = lambda r, c: pl.BlockSpec((r, c), lambda i: (0, 0))
    hp, hn = _halo_specs(tm, S, 1024, 6, PROJ_HALO)
    return pl.pallas_call(
        body, name="merge_bwd", grid=(nt,),
        in_specs=[row(1024), row(6144), pl.BlockSpec((tm, 1024), lambda i: (i, 6)), hp, hn, row(1024), row(3072),
                  full(bg), full(convw), full(mqg), full(mk), full(mv), full(wot), full(wbat), full(wbct), full(wbmt)],
        out_specs=[row(6144), row(CONV_W), row(1024), pl.BlockSpec((N_HEADS, tm, 1), lambda i: (0, i, 0)), row(3072),
                   acc(1, 3072), acc(M, MEM_W), acc(M, MEM_W), acc(1, MEM_HD)],
        out_shape=[jax.ShapeDtypeStruct((S, 6144), BF16), jax.ShapeDtypeStruct((S, CONV_W), F32),
                   jax.ShapeDtypeStruct((S, 1024), BF16), jax.ShapeDtypeStruct((N_HEADS, S, 1), F32),
                   jax.ShapeDtypeStruct((S, 3072), BF16), jax.ShapeDtypeStruct((1, 3072), F32),
                   jax.ShapeDtypeStruct((M, MEM_W), F32), jax.ShapeDtypeStruct((M, MEM_W), F32),
                   jax.ShapeDtypeStruct((1, MEM_HD), F32)],
        compiler_params=_params(("arbitrary",)),
    )(g, proj, proj, proj, proj, o, u, bg, convw, mqg, mk, mv, wot, wbat, wbct, wbmt)


def _conv_bwd(dcv, proj, convw):
    S = dcv.shape[0]
    tm = _tile(S, 512)
    nt = S // tm

    def body(d_ref, dp_ref, dn_ref, ccu_ref, hp_ref, hn_ref, cw_ref, dccu_ref, dcw_ref):
        i = pl.program_id(0)

        @pl.when(i == 0)
        def _():
            dcw_ref[...] = jnp.zeros(dcw_ref.shape, F32)

        cc, cu = ccu_ref[:, 0:CONV_W].astype(F32), ccu_ref[:, CONV_W:].astype(F32)
        z, z_up, z_dn = _conv_shifts(cc, cu, hp_ref, hn_ref, i, nt, tm)
        d = d_ref[...]
        dprev = jnp.where(i == 0, 0.0, dp_ref[7:8, :])
        dnext = jnp.where(i == nt - 1, 0.0, dn_ref[0:1, :])
        row = lax.broadcasted_iota(jnp.int32, d.shape, 0)
        d_up = jnp.where(row == 0, dprev, pltpu.roll(d, 1, 0))
        d_dn = jnp.where(row == tm - 1, dnext, pltpu.roll(d, tm - 1, 0))
        dz = cw_ref[0:1, :] * d_dn + cw_ref[1:2, :] * d + cw_ref[2:3, :] * d_up
        dccu_ref[:, 0:CONV_W] = (dz * cu).astype(BF16)
        dccu_ref[:, CONV_W:] = (dz * cc).astype(BF16)
        dcw_ref[0:1, :] += _colsum(d * z_up)
        dcw_ref[1:2, :] += _colsum(d * z)
        dcw_ref[2:3, :] += _colsum(d * z_dn)
        dcw_ref[3:4, :] += _colsum(d)

    hp, hn = _halo_specs(tm, S, 1024, 6, PROJ_HALO)
    dp, dn = _halo_specs(tm, S, CONV_W, 0, F32_HALO)
    return pl.pallas_call(
        body, name="conv_bwd", grid=(nt,),
        in_specs=[pl.BlockSpec((tm, CONV_W), lambda i: (i, 0)), dp, dn,
                  pl.BlockSpec((tm, 1024), lambda i: (i, 6)), hp, hn,
                  pl.BlockSpec((8, CONV_W), lambda i: (0, 0))],
        out_specs=[pl.BlockSpec((tm, 1024), lambda i: (i, 0)), pl.BlockSpec((8, CONV_W), lambda i: (0, 0))],
        out_shape=[jax.ShapeDtypeStruct((S, 1024), BF16), jax.ShapeDtypeStruct((8, CONV_W), F32)],
        compiler_params=_params(("arbitrary",)),
    )(dcv, dcv, dcv, proj, proj, proj, convw)


ATTN_BWD_TQ = 2048


def _attn_bwd(q, k, v, do, lse_r, delta_r):
    S = q.shape[0]
    tk = _tile(S, 512)
    nq, tq = lse_r.shape[1], lse_r.shape[3]

    def body(k_ref, v_ref, q_ref, do_ref, lse_ref, delta_ref, dk_ref, dv_ref, dq_ref, dk_s, dv_s):
        @pl.when(pl.program_id(1) == 0)
        def _():
            dq_ref[...] = jnp.zeros(dq_ref.shape, F32)

        dk_s[...] = jnp.zeros(dk_s.shape, F32)
        dv_s[...] = jnp.zeros(dv_s.shape, F32)
        kv, vv = k_ref[...], v_ref[...]

        def step(c, carry):
            rows = pl.ds(pl.multiple_of(c * tq, tq), tq)
            qv, dov = q_ref[rows, :], do_ref[rows, :]
            pt = jnp.exp(_dot_nt(kv, qv) - lse_ref[0, c])
            dv_s[...] += _dot(pt.astype(BF16), dov)
            dpt = _dot_nt(vv, dov)
            dst = (pt * (dpt - delta_ref[0, c])).astype(BF16)
            dk_s[...] += _dot(dst, qv)
            dq_ref[rows, :] += _dot_tn(dst, kv)
            return carry

        lax.fori_loop(0, nq, step, 0)
        dk_ref[...] = dk_s[...]
        dv_ref[...] = dv_s[...].astype(BF16)

    rowv = pl.BlockSpec((1, nq, 1, tq), lambda h, j: (h, 0, 0, 0))
    blk = pl.BlockSpec((tk, HP), lambda h, j: (j, h))
    res = pl.BlockSpec((S, HP), lambda h, j: (0, h))
    return pl.pallas_call(
        body, name="attn_bwd", grid=(N_HEADS, S // tk),
        in_specs=[blk, blk, res, res, rowv, rowv],
        out_specs=[blk, blk, res],
        out_shape=[jax.ShapeDtypeStruct((S, N_HEADS * HP), F32), jax.ShapeDtypeStruct((S, N_HEADS * HP), BF16),
                   jax.ShapeDtypeStruct((S, N_HEADS * HP), F32)],
        scratch_shapes=[pltpu.VMEM((tk, HP), F32), pltpu.VMEM((tk, HP), F32)],
        compiler_params=_params(("parallel", "arbitrary")),
    )(k, v, q, do, lse_r, delta_r)


def _mla_prep_bwd(proj, tabs, dq, dk, dv, qg, kvg, qhg, khg, wuq, wuk, wuqt, wukt, wuvt):
    S = proj.shape[0]
    tm = _tile(S, 256)

    def body(a_ref, c_ref, sa_ref, sb_ref, dq_ref, dk_ref, dv_ref, qg_ref, kvg_ref, qhg_ref, khg_ref,
             wuq_ref, wuk_ref, wuqt_ref, wukt_ref, wuvt_ref,
             da_ref, dqraw_ref, dkraw_ref, dqg_ref, dkvg_ref, dqhg_ref, dkhg_ref):
        @pl.when(pl.program_id(0) == 0)
        def _():
            dqg_ref[...] = jnp.zeros(dqg_ref.shape, F32)
            dkvg_ref[...] = jnp.zeros(dkvg_ref.shape, F32)
            dqhg_ref[...] = jnp.zeros(dqhg_ref.shape, F32)
            dkhg_ref[...] = jnp.zeros(dkhg_ref.shape, F32)

        ql = a_ref[:, 0:Q_RANK].astype(F32)
        kvl = a_ref[:, Q_RANK:Q_RANK + KV_RANK].astype(F32)
        kpe = a_ref[:, Q_RANK + KV_RANK:Q_RANK + KV_RANK + HP].astype(F32)
        rq, qhat = _rms_parts(ql)
        rkv, kvhat = _rms_parts(kvl)
        qraw = _dot((qhat * qg_ref[...]).astype(BF16), wuq_ref[...])
        kn = _dot((kvhat * kvg_ref[...]).astype(BF16), wuk_ref[...])
        c, sa, sb = c_ref[...], sa_ref[...], sb_ref[...]
        dkpe = jnp.zeros(kpe.shape, F32)
        dqhg = jnp.zeros((1, HP), F32)
        dkhg = jnp.zeros((1, HP), F32)
        for h in range(N_HEADS):
            sl = slice(h * HP, (h + 1) * HP)
            r, that = _rms_parts(qraw[:, sl], QK_DIM)
            dtn = _rope_t(dq_ref[:, sl], c, sa, sb) * SCALE
            dqhg = dqhg + _colsum(dtn * that)
            dqraw_ref[:, sl] = _rms_bwd(dtn * qhg_ref[...], that, r, QK_DIM).astype(BF16)
            r, that = _rms_parts(kn[:, sl] + kpe, QK_DIM)
            dtn = _rope_t(dk_ref[:, sl], c, sa, sb)
            dkhg = dkhg + _colsum(dtn * that)
            dkr = _rms_bwd(dtn * khg_ref[...], that, r, QK_DIM)
            dkraw_ref[:, sl] = dkr.astype(BF16)
            dkpe = dkpe + dkr
        dqhg_ref[...] += dqhg
        dkhg_ref[...] += dkhg
        dqn = _dot(dqraw_ref[...], wuqt_ref[...])
        dqg_ref[...] += _colsum(dqn * qhat)
        da_ref[:, 0:Q_RANK] = _rms_bwd(dqn * qg_ref[...], qhat, rq, Q_RANK).astype(BF16)
        dkvn = _dot(dkraw_ref[...], wukt_ref[...]) + _dot(dv_ref[...], wuvt_ref[...])
        dkvg_ref[...] += _colsum(dkvn * kvhat)
        da_ref[:, Q_RANK:Q_RANK + KV_RANK] = _rms_bwd(dkvn * kvg_ref[...], kvhat, rkv, KV_RANK).astype(BF16)
        da_ref[:, Q_RANK + KV_RANK:Q_RANK + KV_RANK + HP] = dkpe.astype(BF16)
        da_ref[:, Q_RANK + KV_RANK + HP:] = jnp.zeros((tm, 1024 - Q_RANK - KV_RANK - HP), BF16)

    row = lambda w: pl.BlockSpec((tm, w), lambda i: (i, 0))
    full = lambda a: pl.BlockSpec(a.shape, lambda i: (0,) * a.ndim)
    acc = lambda c: pl.BlockSpec((1, c), lambda i: (0, 0))
    return pl.pallas_call(
        body, name="mla_prep_bwd", grid=(S // tm,),
        in_specs=[pl.BlockSpec((tm, 1024), lambda i: (i, 7)), row(HP), row(HP), row(HP),
                  row(1024), row(1024), row(1024), full(qg), full(kvg), full(qhg), full(khg),
                  full(wuq), full(wuk), full(wuqt), full(wukt), full(wuvt)],
        out_specs=[row(1024), row(1024), row(1024), acc(Q_RANK), acc(KV_RANK), acc(HP), acc(HP)],
        out_shape=[jax.ShapeDtypeStruct((S, 1024), BF16)] * 3
        + [jax.ShapeDtypeStruct((1, Q_RANK), F32), jax.ShapeDtypeStruct((1, KV_RANK), F32),
           jax.ShapeDtypeStruct((1, HP), F32), jax.ShapeDtypeStruct((1, HP), F32)],
        compiler_params=_params(("arbitrary",)),
    )(proj, *tabs, dq, dk, dv, qg, kvg, qhg, khg, wuq, wuk, wuqt, wukt, wuvt)


def _inproj_bwd(dmain, dccu, dsega, wint, x, g, ng):
    S = x.shape[0]
    tm, tk = _tile(S, 512), 2048
    nk = PW // tk

    nmain = dmain.shape[1] // tk
    assert dmain.shape[1] % tk == 0 and dccu.shape[1] + dsega.shape[1] == tk and nk == nmain + 1

    def body(dm_ref, dc_ref, da_ref, w_ref, x_ref, g_ref, ng_ref, dx_ref, dng_ref, acc_s):
        i, k = pl.program_id(0), pl.program_id(1)

        @pl.when((i == 0) & (k == 0))
        def _():
            dng_ref[...] = jnp.zeros(dng_ref.shape, F32)

        @pl.when(k == 0)
        def _():
            acc_s[...] = jnp.zeros(acc_s.shape, F32)

        @pl.when(k < nmain)
        def _():
            acc_s[...] += _dot(dm_ref[...], w_ref[...])

        @pl.when(k == nk - 1)
        def _():
            acc_s[...] += _dot(dc_ref[...], w_ref[0:1024, :]) + _dot(da_ref[...], w_ref[1024:2048, :])
            r, xhat = _rms_parts(x_ref[...])
            dh = acc_s[...]
            dng_ref[...] += _colsum(dh * xhat)
            dx_ref[...] = g_ref[...] + _rms_bwd(dh * ng_ref[...], xhat, r, D_MODEL)

    return pl.pallas_call(
        body, name="inproj_bwd", grid=(S // tm, nk),
        in_specs=[pl.BlockSpec((tm, tk), lambda i, k: (i, jnp.minimum(k, nmain - 1))),
                  pl.BlockSpec((tm, 1024), lambda i, k: (i, 0)),
                  pl.BlockSpec((tm, 1024), lambda i, k: (i, 0)),
                  pl.BlockSpec((tk, D_MODEL), lambda i, k: (k, 0)),
                  pl.BlockSpec((tm, D_MODEL), lambda i, k: (i, 0)),
                  pl.BlockSpec((tm, D_MODEL), lambda i, k: (i, 0)),
                  pl.BlockSpec((1, D_MODEL), lambda i, k: (0, 0))],
        out_specs=[pl.BlockSpec((tm, D_MODEL), lambda i, k: (i, 0)),
                   pl.BlockSpec((1, D_MODEL), lambda i, k: (0, 0))],
        out_shape=[jax.ShapeDtypeStruct((S, D_MODEL), F32), jax.ShapeDtypeStruct((1, D_MODEL), F32)],
        scratch_shapes=[pltpu.VMEM((tm, D_MODEL), F32)],
        compiler_params=_params(("arbitrary", "arbitrary")),
    )(dmain, dccu, dsega, wint, x, g, ng)


def _memkv_bwd(mem, mng, mkg, wmkv, wmkvt, dmk, dmv):
    M = mem.shape[0]

    def body(mem_ref, mng_ref, mkg_ref, w_ref, wt_ref, dmk_ref, dmv_ref, dw_ref, dmng_ref, dmkg_ref, d_s):
        r, mhat = _rms_parts(mem_ref[...])
        mn = (mhat * mng_ref[...]).astype(BF16)
        mkv = _dot(mn, w_ref[...])
        dmkg = jnp.zeros((1, MEM_HD), F32)
        for h in range(MEM_HEADS):
            sl = slice(h * MEM_HD, (h + 1) * MEM_HD)
            rk, khat = _rms_parts(mkv[:, 2 * MEM_HD * h:2 * MEM_HD * h + MEM_HD])
            dkn = dmk_ref[:, sl]
            dmkg = dmkg + _colsum(dkn * khat)
            d_s[:, 2 * MEM_HD * h:2 * MEM_HD * h + MEM_HD] = _rms_bwd(dkn * mkg_ref[...], khat, rk, MEM_HD).astype(BF16)
            d_s[:, 2 * MEM_HD * h + MEM_HD:2 * MEM_HD * (h + 1)] = dmv_ref[:, sl].astype(BF16)
        dmkg_ref[...] = dmkg
        dw_ref[...] = _dot_tn(mn, d_s[...])
        dmn = _dot(d_s[...], wt_ref[...])
        dmng_ref[...] = _colsum(dmn * mhat)

    return pl.pallas_call(
        body, name="memkv_bwd",
        out_shape=[jax.ShapeDtypeStruct((D_MODEL, 2 * MEM_W), F32), jax.ShapeDtypeStruct((1, D_MODEL), F32),
                   jax.ShapeDtypeStruct((1, MEM_HD), F32)],
        scratch_shapes=[pltpu.VMEM((M, 2 * MEM_W), BF16)],
        compiler_params=pltpu.CompilerParams(vmem_limit_bytes=VMEM_LIMIT_V7X),
    )(mem, mng, mkg, wmkv, wmkvt, dmk, dmv)


def _mm_tn(a, b, name, col0=0, ncols=None):
    S, M = a.shape
    N = b.shape[1] if ncols is None else ncols
    tm, tn, ts = _tile(M, 1024), _tile(N, 1024), _tile(S, 512)
    assert col0 % tn == 0
    jb = col0 // tn

    def body(a_ref, b_ref, o_ref):
        @pl.when(pl.program_id(2) == 0)
        def _():
            o_ref[...] = jnp.zeros(o_ref.shape, F32)

        o_ref[...] += _dot_tn(a_ref[...].astype(BF16), b_ref[...].astype(BF16))

    return pl.pallas_call(
        body, name=name, grid=(M // tm, N // tn, S // ts),
        in_specs=[pl.BlockSpec((ts, tm), lambda i, j, k: (k, i)),
                  pl.BlockSpec((ts, tn), lambda i, j, k: (k, j + jb))],
        out_specs=pl.BlockSpec((tm, tn), lambda i, j, k: (i, j)),
        out_shape=jax.ShapeDtypeStruct((M, N), F32),
        compiler_params=_params(("parallel", "parallel", "arbitrary")),
    )(a, b)


def _adamw(w, g0, g1, m, v, name):
    R, C = w.shape
    tr = R
    for cand in (512, 256, 128, 64, 32, 16, 8):
        if R % cand == 0 and cand * C * 4 <= (1 << 20):
            tr = cand
            break
    c1 = 1.0 / (1.0 - ADAM_B1 ** ADAM_STEP)
    c2 = 1.0 / (1.0 - ADAM_B2 ** ADAM_STEP)

    def body(w_ref, g0_ref, g1_ref, m_ref, v_ref, g_ref, d_ref, nm_ref, nv_ref):
        g = g0_ref[...] + g1_ref[...]
        nm = ADAM_B1 * m_ref[...] + (1.0 - ADAM_B1) * g
        nv = ADAM_B2 * v_ref[...] + (1.0 - ADAM_B2) * (g * g)
        g_ref[...] = g
        nm_ref[...] = nm
        nv_ref[...] = nv
        d_ref[...] = -ADAM_LR * ((nm * c1) / (jnp.sqrt(nv * c2) + ADAM_EPS) + ADAM_WD * w_ref[...])

    blk = pl.BlockSpec((tr, C), lambda i: (i, 0))
    return pl.pallas_call(
        body, name=name, grid=(R // tr,),
        in_specs=[blk] * 5, out_specs=[blk] * 4,
        out_shape=[jax.ShapeDtypeStruct((R, C), F32)] * 4,
        compiler_params=_params(("parallel",)),
    )(w, g0, g1, m, v)


def _sum_slabs(a, name):
    K, R, C = a.shape
    tr = R
    for cand in (512, 256, 128, 64, 32, 16, 8):
        if R % cand == 0 and cand * C * 4 * K <= (4 << 20):
            tr = cand
            break

    def body(a_ref, o_ref):
        t = a_ref[0].astype(F32)
        for k in range(1, K):
            t = t + a_ref[k].astype(F32)
        o_ref[...] = t

    return pl.pallas_call(
        body, name=name, grid=(R // tr,),
        in_specs=[pl.BlockSpec((K, tr, C), lambda i: (0, i, 0))],
        out_specs=pl.BlockSpec((tr, C), lambda i: (i, 0)),
        out_shape=jax.ShapeDtypeStruct((R, C), F32),
        compiler_params=_params(("parallel",)),
    )(a)


MESH = pl.DeviceIdType.MESH
HBM = pl.BlockSpec(memory_space=pltpu.HBM)


def _chip_peer(x, y, c, k):
    return (x ^ (k >> 1), y ^ (k & 1), c)


def _gather_chips(arrs):
    n = len(arrs)
    half = arrs[0].shape[0] // 2
    assert all(a.shape[0] == 2 * half for a in arrs)

    def body(*refs):
        ins, outs = refs[:n], refs[n:2 * n]
        send1, recv1, send2, recv2, loc = refs[2 * n:]
        x, y, c = lax.axis_index("x"), lax.axis_index("y"), lax.axis_index("c")
        me = 2 * x + y
        mine, theirs = pl.ds(c * half, half), pl.ds((1 - c) * half, half)
        sibling = (x, y, 1 - c)
        waits = []
        for a in range(n):
            own = pltpu.make_async_copy(ins[a], outs[a].at[me], loc.at[a])
            own.start()
            waits.append(own.wait)

        def over_ici(a, k, src_chip):
            return pltpu.make_async_remote_copy(
                src_ref=ins[a].at[mine], dst_ref=outs[a].at[src_chip, mine], send_sem=send1.at[3 * a + k - 1],
                recv_sem=recv1.at[3 * a + k - 1], device_id=_chip_peer(x, y, c, k), device_id_type=MESH)

        def to_sibling(a, k, layers):
            block = outs[a].at[me ^ k, layers]
            return pltpu.make_async_remote_copy(
                src_ref=block, dst_ref=block, send_sem=send2.at[3 * a + k - 1], recv_sem=recv2.at[3 * a + k - 1],
                device_id=sibling, device_id_type=MESH)

        for a in range(n):
            for k in (1, 2, 3):
                cp = over_ici(a, k, me)
                cp.start()
                waits.append(cp.wait_send)
        for a in range(n):
            for k in (1, 2, 3):
                over_ici(a, k, me ^ k).wait_recv()
                cp = to_sibling(a, k, mine)
                cp.start()
                waits.append(cp.wait_send)
        for a in range(n):
            for k in (1, 2, 3):
                to_sibling(a, k, theirs).wait_recv()
        for w in waits:
            w()

    return pl.pallas_call(
        body, name="gather_weights",
        in_specs=[HBM] * n, out_specs=[HBM] * n,
        out_shape=[jax.ShapeDtypeStruct((4,) + a.shape, a.dtype) for a in arrs],
        scratch_shapes=[pltpu.SemaphoreType.DMA((3 * n,)), pltpu.SemaphoreType.DMA((3 * n,)),
                        pltpu.SemaphoreType.DMA((3 * n,)), pltpu.SemaphoreType.DMA((3 * n,)),
                        pltpu.SemaphoreType.DMA((n,))],
    )(*arrs)


def _scatter_chips(arrs, small):
    n = len(arrs)

    def body(*refs):
        ins, small_in = refs[:n], refs[n]
        outs, small_out = refs[n + 1:2 * n + 1], refs[2 * n + 1]
        send, recv, loc, ssend, srecv = refs[2 * n + 2:]
        x, y, c = lax.axis_index("x"), lax.axis_index("y"), lax.axis_index("c")
        me = 2 * x + y
        me8 = 4 * x + 2 * y + c
        copies = []
        for a in range(n):
            own = pltpu.make_async_copy(ins[a].at[me], outs[a].at[me], loc.at[a])
            own.start()
            copies.append(own)
        own = pltpu.make_async_copy(small_in, small_out.at[me8], loc.at[n])
        own.start()
        copies.append(own)
        for k in range(1, 8):
            cp = pltpu.make_async_remote_copy(
                src_ref=small_in, dst_ref=small_out.at[me8], send_sem=ssend.at[k - 1], recv_sem=srecv.at[k - 1],
                device_id=(x ^ (k >> 2), y ^ ((k >> 1) & 1), c ^ (k & 1)), device_id_type=MESH)
            cp.start()
            copies.append(cp)
        for a in range(n):
            for k in (1, 2, 3):
                cp = pltpu.make_async_remote_copy(
                    src_ref=ins[a].at[me ^ k], dst_ref=outs[a].at[me], send_sem=send.at[3 * a + k - 1],
                    recv_sem=recv.at[3 * a + k - 1], device_id=_chip_peer(x, y, c, k), device_id_type=MESH)
                cp.start()
                copies.append(cp)
        for cp in copies:
            cp.wait()

    return pl.pallas_call(
        body, name="scatter_grads",
        in_specs=[HBM] * (n + 1), out_specs=[HBM] * (n + 1),
        out_shape=[jax.ShapeDtypeStruct(a.shape, a.dtype) for a in arrs]
        + [jax.ShapeDtypeStruct((8,) + small.shape, small.dtype)],
        scratch_shapes=[pltpu.SemaphoreType.DMA((3 * n,)), pltpu.SemaphoreType.DMA((3 * n,)),
                        pltpu.SemaphoreType.DMA((n + 1,)), pltpu.SemaphoreType.DMA((7,)),
                        pltpu.SemaphoreType.DMA((7,))],
    )(*arrs, small)


def _swap_cores(arrs):
    n = len(arrs)

    def body(*refs):
        ins, outs = refs[:n], refs[n:2 * n]
        send, recv = refs[2 * n:]
        x, y, c = lax.axis_index("x"), lax.axis_index("y"), lax.axis_index("c")
        copies = []
        for a in range(n):
            cp = pltpu.make_async_remote_copy(
                src_ref=ins[a], dst_ref=outs[a], send_sem=send.at[a], recv_sem=recv.at[a],
                device_id=(x, y, 1 - c), device_id_type=MESH)
            cp.start()
            copies.append(cp)
        for cp in copies:
            cp.wait()

    return pl.pallas_call(
        body, name="swap_cores",
        in_specs=[HBM] * n, out_specs=[HBM] * n,
        out_shape=[jax.ShapeDtypeStruct(a.shape, a.dtype) for a in arrs],
        scratch_shapes=[pltpu.SemaphoreType.DMA((n,)), pltpu.SemaphoreType.DMA((n,))],
    )(*arrs)


def _pad_last(a, n):
    return jnp.pad(a, [(0, 0)] * (a.ndim - 1) + [(0, n - a.shape[-1])])


def _pad_w_in(w):
    lead = w.shape[:-1]
    seg = lambda a, b: w[..., a:b]
    ga = _pad_last(seg(2720, 3232).reshape(lead + (N_HEADS, V_DIM)), HP).reshape(lead + (1024,))
    kpe = jnp.pad(seg(640, 672), [(0, 0)] * len(lead) + [(NOPE_DIM, HP - QK_DIM)])
    zero = jnp.zeros(lead + (PW - 7936,), w.dtype)
    return jnp.concatenate(
        [seg(4256, 7328), ga, seg(672, 1184), seg(2208, 2720), seg(3232, 3744), seg(3744, 4256),
         seg(1184, 1696), seg(1696, 2208), seg(0, 384), seg(384, 640), kpe, zero], axis=-1)


def _unpad_w_in(w):
    lead = w.shape[:-1]
    seg = lambda a, n: w[..., a:a + n]
    ga = seg(O_GA, 1024).reshape(lead + (N_HEADS, HP))[..., :V_DIM].reshape(lead + (N_HEADS * V_DIM,))
    return jnp.concatenate(
        [seg(O_QL, 384), seg(O_KVL, 256), seg(O_KPE + NOPE_DIM, ROPE_DIM), seg(O_CB, 512), seg(O_CC, 512),
         seg(O_CU, 512), seg(O_QM, 512), ga, seg(O_GC, 512), seg(O_GM, 512), seg(O_R, 3072)], axis=-1)


def _cols_from_shards(g):
    _, L, R, C = g.shape
    return jnp.transpose(g, (1, 2, 0, 3)).reshape(L, R, 4 * C)


def _cols_to_shards(w):
    L, R, C4 = w.shape
    return jnp.transpose(w.reshape(L, R, 4, C4 // 4), (2, 0, 1, 3))


def _rows_from_shards(g):
    _, L, R, C = g.shape
    return jnp.transpose(g, (1, 0, 2, 3)).reshape(L, 4 * R, C)


def _rows_to_shards(w):
    L, R4, C = w.shape
    return jnp.transpose(w.reshape(L, 4, R4 // 4, C), (1, 0, 2, 3))


def _t(w):
    return jnp.swapaxes(w, -1, -2)


def _layer_fwd(x, mem, tabs, p):
    proj, xn = _inproj(x, p["norm_g"], p["w_in"])
    q, k, v, qn, kvn = _mla_prep(proj, tabs, p["q_norm_g"], p["kv_norm_g"], p["q_head_g"], p["k_head_g"],
                                 p["w_uq"], p["w_uk"], p["w_uv"])
    o, lse = _attn_fwd(q, k, v)
    mk, mv = _memkv(mem, p["mem_norm_g"], p["mem_k_g"], p["w_mkv"])
    x_new, oa, oc, om, u, y = _merge(proj, o, x, p["b_gate"], p["conv_wb"], p["mem_q_g"], mk, mv,
                                     p["w_br_attn"], p["w_br_conv"], p["w_br_mem"], p["w_out"])
    saved = dict(x=x, proj=proj, xn=xn, q=q, k=k, v=v, qn=qn, kvn=kvn, o=o, lse=lse, mk=mk, mv=mv,
                 oa=oa, oc=oc, om=om, u=u, y=y)
    return x_new, saved


def _layer_bwd(g, mem, tabs, p, s):
    S = g.shape[0]
    tq = _tile(S, ATTN_BWD_TQ)
    dmain, dcv, d_o, delta, du, dbg, dmk, dmv, dmqg = _merge_bwd(
        g, s["proj"], s["o"], s["u"], p["b_gate"], p["conv_wb"], p["mem_q_g"], s["mk"], s["mv"],
        p["w_out_t"], p["w_br_attn_t"], p["w_br_conv_t"], p["w_br_mem_t"])
    dccu, dconv = _conv_bwd(dcv, s["proj"], p["conv_wb"])
    as_rows = lambda t: t.reshape(N_HEADS, S // tq, 1, tq)
    dk, dv, dq = _attn_bwd(s["q"], s["k"], s["v"], d_o, as_rows(s["lse"]), as_rows(delta))
    dsega, dqraw, dkraw, dqg, dkvg, dqhg, dkhg = _mla_prep_bwd(
        s["proj"], tabs, dq, dk, dv, p["q_norm_g"], p["kv_norm_g"], p["q_head_g"], p["k_head_g"],
        p["w_uq"], p["w_uk"], p["w_uq_t"], p["w_uk_t"], p["w_uv_t"])
    dx, dng = _inproj_bwd(dmain, dccu, dsega, p["w_in_t"], s["x"], g, p["norm_g"])
    dwmkv, dmng, dmkg = _memkv_bwd(mem, p["mem_norm_g"], p["mem_k_g"], p["w_mkv"], p["w_mkv_t"], dmk, dmv)
    grads = dict(
        norm_g=dng, b_gate=dbg, q_norm_g=dqg, kv_norm_g=dkvg, q_head_g=dqhg, k_head_g=dkhg,
        conv_wb=dconv, mem_norm_g=dmng, mem_q_g=dmqg, mem_k_g=dmkg, w_mkv=dwmkv,
        w_in=jnp.concatenate([_mm_tn(s["xn"], dmain, "grad_w_in"), _mm_tn(s["xn"], dccu, "grad_w_in_conv"),
                              _mm_tn(s["xn"], dsega, "grad_w_in_lat")], axis=1),
        w_uq=_mm_tn(s["qn"], dqraw, "grad_w_uq"),
        w_uk=_mm_tn(s["kvn"], dkraw, "grad_w_uk"),
        w_uv=_mm_tn(s["kvn"], dv, "grad_w_uv"),
        w_br_attn=_mm_tn(s["oa"], du, "grad_w_br_attn", 0, 1024),
        w_br_conv=_mm_tn(s["oc"], du, "grad_w_br_conv", 1024, 1024),
        w_br_mem=_mm_tn(s["om"], du, "grad_w_br_mem", 2048, 1024),
        w_out=_mm_tn(s["y"], g, "grad_w_out"),
    )
    return dx, grads


def _layer_params(full, l):
    p = {}
    w_in = _pad_w_in(full["w_in"][l])
    w_uq = _pad_last(full["w_uq"][l].reshape(Q_RANK, N_HEADS, QK_DIM), HP).reshape(Q_RANK, 1024)
    ukv = full["w_ukv"][l].reshape(KV_RANK, N_HEADS, NOPE_DIM + V_DIM)
    w_uk = _pad_last(ukv[..., :NOPE_DIM], HP).reshape(KV_RANK, 1024)
    w_uv = _pad_last(ukv[..., NOPE_DIM:], HP).reshape(KV_RANK, 1024)
    w_ba = jnp.pad(full["w_br_attn"][l].reshape(N_HEADS, V_DIM, D_MODEL), ((0, 0), (0, HP - V_DIM), (0, 0)))
    w_ba = w_ba.reshape(1024, D_MODEL)
    p.update(w_in=w_in, w_uq=w_uq, w_uk=w_uk, w_uv=w_uv, w_br_attn=w_ba, w_br_conv=full["w_br_conv"][l],
             w_br_mem=full["w_br_mem"][l], w_out=full["w_out"][l], w_mkv=full["w_mkv"][l])
    for n in ("w_in", "w_uq", "w_uk", "w_uv", "w_br_attn", "w_br_conv", "w_br_mem", "w_out", "w_mkv"):
        p[n + "_t"] = _t(p[n])
    for n in ("norm_g", "b_gate", "q_norm_g", "kv_norm_g", "mem_norm_g", "mem_q_g", "mem_k_g"):
        p[n] = full[n][l][None, :]
    p["q_head_g"] = _pad_last(full["q_head_g"][l][None, :], HP)
    p["k_head_g"] = _pad_last(full["k_head_g"][l][None, :], HP)
    p["conv_wb"] = jnp.concatenate(
        [full["conv_w"][l], full["conv_b"][l][None, :], jnp.zeros((4, CONV_W), F32)], axis=0)
    return p


def _local_step(x, mem, positions, full, target):
    S = x.shape[0]
    invf16 = ROPE_BASE ** (-jnp.arange(0, ROPE_DIM, 2, dtype=F32) / ROPE_DIM)
    invf = jnp.concatenate([jnp.zeros((NOPE_DIM,), F32), invf16, invf16, jnp.zeros((HP - QK_DIM,), F32)])[None, :]
    tabs = _rope_tables(jnp.broadcast_to(positions.reshape(S, 1), (S, HP)), invf)
    params = [_layer_params(full, l) for l in range(DEPTH)]
    saved = []
    h = x
    for l in range(DEPTH):
        h, s = _layer_fwd(h, mem, tabs, params[l])
        saved.append(s)
    g, loss_part = _loss_head(h, target)
    per_layer = [None] * DEPTH
    for l in reversed(range(DEPTH)):
        g, per_layer[l] = _layer_bwd(g, mem, tabs, params[l], saved[l])
    st = lambda n: jnp.stack([per_layer[l][n] for l in range(DEPTH)])
    out = {}
    out["w_in"] = _unpad_w_in(st("w_in"))
    out["w_uq"] = st("w_uq").reshape(DEPTH, Q_RANK, N_HEADS, HP)[..., :QK_DIM].reshape(DEPTH, Q_RANK, N_HEADS * QK_DIM)
    duk = st("w_uk").reshape(DEPTH, KV_RANK, N_HEADS, HP)[..., :NOPE_DIM]
    duv = st("w_uv").reshape(DEPTH, KV_RANK, N_HEADS, HP)[..., :V_DIM]
    out["w_ukv"] = jnp.concatenate([duk, duv], axis=-1).reshape(DEPTH, KV_RANK, 1024)
    out["w_br_attn"] = st("w_br_attn").reshape(DEPTH, N_HEADS, HP, D_MODEL)[:, :, :V_DIM].reshape(DEPTH, 512, D_MODEL)
    for n in ("w_br_conv", "w_br_mem", "w_out", "w_mkv"):
        out[n] = st(n)
    for n in ("norm_g", "b_gate", "q_norm_g", "kv_norm_g", "mem_norm_g", "mem_q_g", "mem_k_g"):
        out[n] = st(n)[:, 0, :]
    out["q_head_g"] = st("q_head_g")[:, 0, :QK_DIM]
    out["k_head_g"] = st("k_head_g")[:, 0, :QK_DIM]
    cwb = st("conv_wb")
    out["conv_w"] = cwb[:, 0:3, :]
    out["conv_b"] = cwb[:, 3, :]
    return loss_part, g, out


_COL_SHARDED = ("w_in", "w_uq", "w_ukv", "w_br_attn", "w_br_conv", "w_br_mem")
_ROW_SHARDED = ("w_mkv", "w_out")
_BIG = _COL_SHARDED + _ROW_SHARDED
_SMALL = ("norm_g", "b_gate", "q_norm_g", "kv_norm_g", "q_head_g", "k_head_g", "conv_w", "conv_b",
          "mem_norm_g", "mem_q_g", "mem_k_g")
_ORDER = ("norm_g", "w_in", "b_gate", "q_norm_g", "w_uq", "kv_norm_g", "w_ukv", "q_head_g", "k_head_g",
          "conv_w", "conv_b", "mem_norm_g", "w_mkv", "mem_q_g", "mem_k_g", "w_br_attn", "w_br_conv",
          "w_br_mem", "w_out")


def _pack_small(d, extra):
    flat = jnp.concatenate([d[n].reshape(-1) for n in _SMALL] + [extra.reshape(-1)])
    n = flat.shape[0]
    rows = -(-n // 1024) * 8
    return jnp.pad(flat, (0, rows * 128 - n)).reshape(rows, 128)


def _unpack_small(packed, like):
    flat = packed.reshape(-1)
    out, off = {}, 0
    for n in _SMALL:
        sz = int(np.prod(like[n].shape))
        out[n] = flat[off:off + sz].reshape(like[n].shape)
        off += sz
    return out, flat[off:]


def kernel(x, mem, positions, norm_g, w_in, b_gate, q_norm_g, w_uq, kv_norm_g, w_ukv, q_head_g, k_head_g, conv_w, conv_b, mem_norm_g, w_mkv, mem_q_g, mem_k_g, w_br_attn, w_br_conv, w_br_mem, w_out, loss_target, m_norm_g, m_w_in, m_b_gate, m_q_norm_g, m_w_uq, m_kv_norm_g, m_w_ukv, m_q_head_g, m_k_head_g, m_conv_w, m_conv_b, m_mem_norm_g, m_w_mkv, m_mem_q_g, m_mem_k_g, m_w_br_attn, m_w_br_conv, m_w_br_mem, m_w_out, v_norm_g, v_w_in, v_b_gate, v_q_norm_g, v_w_uq, v_kv_norm_g, v_w_ukv, v_q_head_g, v_k_head_g, v_conv_w, v_conv_b, v_mem_norm_g, v_w_mkv, v_mem_q_g, v_mem_k_g, v_w_br_attn, v_w_br_conv, v_w_br_mem, v_w_out):
    w = dict(norm_g=norm_g, w_in=w_in, b_gate=b_gate, q_norm_g=q_norm_g, w_uq=w_uq, kv_norm_g=kv_norm_g,
             w_ukv=w_ukv, q_head_g=q_head_g, k_head_g=k_head_g, conv_w=conv_w, conv_b=conv_b,
             mem_norm_g=mem_norm_g, w_mkv=w_mkv, mem_q_g=mem_q_g, mem_k_g=mem_k_g, w_br_attn=w_br_attn,
             w_br_conv=w_br_conv, w_br_mem=w_br_mem, w_out=w_out)
    m = dict(norm_g=m_norm_g, w_in=m_w_in, b_gate=m_b_gate, q_norm_g=m_q_norm_g, w_uq=m_w_uq,
             kv_norm_g=m_kv_norm_g, w_ukv=m_w_ukv, q_head_g=m_q_head_g, k_head_g=m_k_head_g, conv_w=m_conv_w,
             conv_b=m_conv_b, mem_norm_g=m_mem_norm_g, w_mkv=m_w_mkv, mem_q_g=m_mem_q_g, mem_k_g=m_mem_k_g,
             w_br_attn=m_w_br_attn, w_br_conv=m_w_br_conv, w_br_mem=m_w_br_mem, w_out=m_w_out)
    v = dict(norm_g=v_norm_g, w_in=v_w_in, b_gate=v_b_gate, q_norm_g=v_q_norm_g, w_uq=v_w_uq,
             kv_norm_g=v_kv_norm_g, w_ukv=v_w_ukv, q_head_g=v_q_head_g, k_head_g=v_k_head_g, conv_w=v_conv_w,
             conv_b=v_conv_b, mem_norm_g=v_mem_norm_g, w_mkv=v_w_mkv, mem_q_g=v_mem_q_g, mem_k_g=v_mem_k_g,
             w_br_attn=v_w_br_attn, w_br_conv=v_w_br_conv, w_br_mem=v_w_br_mem, w_out=v_w_out)
    chip = 2 * lax.axis_index("x") + lax.axis_index("y")

    gathered = _gather_chips([w[n].astype(BF16) for n in _BIG] + [w["conv_w"]])
    full = {n: w[n] for n in _SMALL}
    for n, gth in zip(_BIG, gathered):
        full[n] = _cols_from_shards(gth) if n in _COL_SHARDED else _rows_from_shards(gth)
    full["conv_w"] = _cols_from_shards(gathered[-1])

    loss_part, grad_x, grads = _local_step(x[0], mem[0], positions[0], full, loss_target[0])

    to_owner = [(_cols_to_shards(grads[n]) if n in _COL_SHARDED else _rows_to_shards(grads[n])).astype(BF16)
                for n in _BIG]
    loss_vec = jnp.zeros((128,), F32).at[0].set(0.5 / D_MODEL * jnp.sum(loss_part))
    small = _pack_small(grads, loss_vec)
    scattered = _scatter_chips(to_owner, small)
    small_sum = _sum_slabs(scattered[-1], "sum_small")
    partial = []
    for n, slabs in zip(_BIG, scattered[:-1]):
        sh = slabs.shape
        partial.append(_sum_slabs(slabs.reshape(4, sh[1] * sh[2], sh[3]), "sum_" + n))
    other = _swap_cores(partial)

    small_g, tail = _unpack_small(small_sum, {n: (grads[n]) for n in _SMALL})
    loss = tail[0]
    small_g["conv_w"] = lax.dynamic_slice_in_dim(small_g["conv_w"], chip * (CONV_W // 4), CONV_W // 4, axis=2)

    outs_g, outs_d, outs_m, outs_v = {}, {}, {}, {}
    for n, p0, p1 in zip(_BIG, partial, other):
        shape = w[n].shape
        flat = lambda t: t.reshape(p0.shape)
        g_, d_, m_, v_ = _adamw(flat(w[n]), p0, p1, flat(m[n]), flat(v[n]), "adamw_" + n)
        outs_g[n], outs_d[n], outs_m[n], outs_v[n] = (t.reshape(shape) for t in (g_, d_, m_, v_))
    zero_small = jnp.zeros_like(small_sum)
    pk = lambda d: _pack_small(d, jnp.zeros((128,), F32))
    g_, d_, m_, v_ = _adamw(pk(w), _pack_small(small_g, jnp.zeros((128,), F32)), zero_small, pk(m), pk(v),
                            "adamw_small")
    like = {n: w[n] for n in _SMALL}
    for dst, packed in ((outs_g, g_), (outs_d, d_), (outs_m, m_), (outs_v, v_)):
        dst.update(_unpack_small(packed, like)[0])

    return (loss, grad_x[None], *[outs_g[n] for n in _ORDER], *[outs_d[n] for n in _ORDER],
            *[outs_m[n] for n in _ORDER], *[outs_v[n] for n in _ORDER])
```

```python
import functools

import numpy as np
import jax
import jax.numpy as jnp
from jax import lax
from jax.experimental import pallas as pl
from jax.experimental.pallas import tpu as pltpu

F32 = jnp.float32
BF16 = jnp.bfloat16

D_MODEL = 1024
DEPTH = 4
N_HEADS = 8
QK_DIM = 96
NOPE_DIM = 64
ROPE_DIM = 32
V_DIM = 64
Q_RANK = 384
KV_RANK = 256
CONV_W = 512
MEM_HEADS = 4
MEM_HD = 128
MEM_W = 512
IN_WIDTH = 7328
PW = 8192
HP = 128
PROJ_HALO = 16
F32_HALO = 8
EPS = 1e-6
ROPE_BASE = 10000.0
SCALE = QK_DIM ** -0.5
MEM_SCALE = MEM_HD ** -0.5

ADAM_LR = 0.001
ADAM_B1 = 0.9
ADAM_B2 = 0.999
ADAM_EPS = 1e-08
ADAM_WD = 0.01
ADAM_STEP = 10

VMEM_LIMIT_V7X = 56 * 1024 * 1024

O_R, O_GA, O_CB, O_QM, O_GC, O_GM, O_CC, O_CU, O_QL, O_KVL, O_KPE = (
    0, 3072, 4096, 4608, 5120, 5632, 6144, 6656, 7168, 7552, 7808)


def _params(sem, vmem=VMEM_LIMIT_V7X):
    return pltpu.CompilerParams(dimension_semantics=sem, vmem_limit_bytes=vmem)


def _sigmoid(t):
    return 0.5 * jnp.tanh(0.5 * t) + 0.5


def _silu_and_grad(g):
    sg = _sigmoid(g)
    return g * sg, sg * (1.0 + g * (1.0 - sg))


def _rms(t, g, n=None):
    n = t.shape[-1] if n is None else n
    r = lax.rsqrt(jnp.sum(t * t, axis=-1, keepdims=True) * (1.0 / n) + EPS)
    return (t * r) * g


def _rms_parts(t, n=None):
    n = t.shape[-1] if n is None else n
    r = lax.rsqrt(jnp.sum(t * t, axis=-1, keepdims=True) * (1.0 / n) + EPS)
    return r, t * r


def _rms_bwd(dhat, hat, r, n):
    return r * (dhat - hat * (jnp.sum(dhat * hat, axis=-1, keepdims=True) * (1.0 / n)))


def _rope(t, c, sa, sb):
    return t * c + pltpu.roll(t, HP - 16, 1) * sa + pltpu.roll(t, 16, 1) * sb


def _rope_t(d, c, sa, sb):
    return d * c + pltpu.roll(d * sa, 16, 1) + pltpu.roll(d * sb, HP - 16, 1)


def _dot(a, b):
    return jnp.dot(a, b, preferred_element_type=F32)


def _dot_nt(a, b):
    return lax.dot_general(a, b, (((1,), (1,)), ((), ())), preferred_element_type=F32)


def _dot_tn(a, b):
    return lax.dot_general(a, b, (((0,), (0,)), ((), ())), preferred_element_type=F32)


def _colsum(t):
    return jnp.sum(t, axis=0, keepdims=True)


def _tile(n, t):
    t = min(n, t)
    assert n % t == 0, (n, t)
    return t


def _rope_tables(pos_b, invf):
    S = pos_b.shape[0]
    tm = _tile(S, 1024)

    def body(pos_ref, invf_ref, c_ref, sa_ref, sb_ref):
        ang = pos_ref[...].astype(F32) * invf_ref[...]
        lane = lax.broadcasted_iota(jnp.int32, ang.shape, 1)
        cs = jnp.cos(ang)
        sn = jnp.sin(ang)
        c_ref[...] = jnp.where(lane < NOPE_DIM, 1.0, jnp.where(lane < QK_DIM, cs, 0.0))
        sa_ref[...] = jnp.where((lane >= NOPE_DIM) & (lane < NOPE_DIM + 16), -sn, 0.0)
        sb_ref[...] = jnp.where((lane >= NOPE_DIM + 16) & (lane < QK_DIM), sn, 0.0)

    blk = pl.BlockSpec((tm, HP), lambda i: (i, 0))
    return pl.pallas_call(
        body, name="rope_tables", grid=(S // tm,),
        in_specs=[blk, pl.BlockSpec((1, HP), lambda i: (0, 0))],
        out_specs=[blk, blk, blk],
        out_shape=[jax.ShapeDtypeStruct((S, HP), F32)] * 3,
        compiler_params=_params(("parallel",)),
    )(pos_b, invf)


def _inproj(x, g, w):
    S = x.shape[0]
    tm, tn = _tile(S, 1024), 2048

    def body(x_ref, g_ref, w_ref, proj_ref, xn_ref):
        @pl.when(pl.program_id(1) == 0)
        def _():
            xn_ref[...] = _rms(x_ref[...], g_ref[...]).astype(BF16)

        proj_ref[...] = _dot(xn_ref[...], w_ref[...]).astype(BF16)

    return pl.pallas_call(
        body, name="inproj", grid=(S // tm, PW // tn),
        in_specs=[pl.BlockSpec((tm, D_MODEL), lambda i, j: (i, 0)),
                  pl.BlockSpec((1, D_MODEL), lambda i, j: (0, 0)),
                  pl.BlockSpec((D_MODEL, tn), lambda i, j: (0, j))],
        out_specs=[pl.BlockSpec((tm, tn), lambda i, j: (i, j)),
                   pl.BlockSpec((tm, D_MODEL), lambda i, j: (i, 0))],
        out_shape=[jax.ShapeDtypeStruct((S, PW), BF16), jax.ShapeDtypeStruct((S, D_MODEL), BF16)],
        compiler_params=_params(("parallel", "arbitrary")),
    )(x, g, w)


def _mla_prep(proj, tabs, qg, kvg, qhg, khg, wuq, wuk, wuv):
    S = proj.shape[0]
    tm = _tile(S, 512)

    def body(a_ref, c_ref, sa_ref, sb_ref, qg_ref, kvg_ref, qhg_ref, khg_ref, wuq_ref, wuk_ref, wuv_ref,
             q_ref, k_ref, v_ref, qn_ref, kvn_ref):
        ql = a_ref[:, 0:Q_RANK].astype(F32)
        kvl = a_ref[:, Q_RANK:Q_RANK + KV_RANK].astype(F32)
        kpe = a_ref[:, Q_RANK + KV_RANK:Q_RANK + KV_RANK + HP].astype(F32)
        qn = _rms(ql, qg_ref[...]).astype(BF16)
        kvn = _rms(kvl, kvg_ref[...]).astype(BF16)
        qn_ref[...] = qn
        kvn_ref[...] = kvn
        qraw = _dot(qn, wuq_ref[...])
        kn = _dot(kvn, wuk_ref[...])
        v_ref[...] = _dot(kvn, wuv_ref[...]).astype(BF16)
        c, sa, sb = c_ref[...], sa_ref[...], sb_ref[...]
        for h in range(N_HEADS):
            sl = slice(h * HP, (h + 1) * HP)
            tq = _rms(qraw[:, sl], qhg_ref[...], QK_DIM)
            q_ref[:, sl] = (_rope(tq, c, sa, sb) * SCALE).astype(BF16)
            tk = _rms(kn[:, sl] + kpe, khg_ref[...], QK_DIM)
            k_ref[:, sl] = _rope(tk, c, sa, sb).astype(BF16)

    row = lambda w: pl.BlockSpec((tm, w), lambda i: (i, 0))
    full = lambda a: pl.BlockSpec(a.shape, lambda i: (0,) * a.ndim)
    return pl.pallas_call(
        body, name="mla_prep", grid=(S // tm,),
        in_specs=[pl.BlockSpec((tm, 1024), lambda i: (i, 7)), row(HP), row(HP), row(HP),
                  full(qg), full(kvg), full(qhg), full(khg), full(wuq), full(wuk), full(wuv)],
        out_specs=[row(1024), row(1024), row(1024), row(Q_RANK), row(KV_RANK)],
        out_shape=[jax.ShapeDtypeStruct((S, 1024), BF16)] * 3
        + [jax.ShapeDtypeStruct((S, Q_RANK), BF16), jax.ShapeDtypeStruct((S, KV_RANK), BF16)],
        compiler_params=_params(("parallel",)),
    )(proj, *tabs, qg, kvg, qhg, khg, wuq, wuk, wuv)


def _attn_fwd(q, k, v):
    S = q.shape[0]
    tq, tk = _tile(S, 1024), _tile(S, 2048)
    nk, nb = S // tk, tk // HP

    def body(q_ref, k_ref, v_ref, o_ref, lse_ref, m_s, l_s, acc_s):
        m_s[...] = jnp.full(m_s.shape, -jnp.inf, F32)
        l_s[...] = jnp.zeros(l_s.shape, F32)
        acc_s[...] = jnp.zeros(acc_s.shape, F32)
        qv = q_ref[...]

        def step(c, carry):
            rows = pl.ds(pl.multiple_of(c * tk, tk), tk)
            s = _dot_nt(qv, k_ref[rows, :])
            cm = s[:, 0:HP]
            for j in range(1, nb):
                cm = jnp.maximum(cm, s[:, j * HP:(j + 1) * HP])
            m_prev = m_s[...]
            m_new = jnp.maximum(m_prev, jnp.max(cm, axis=-1, keepdims=True))
            alpha = jnp.exp(m_prev - m_new)
            lsum = alpha * l_s[...]
            ps = []
            for j in range(nb):
                pj = jnp.exp(s[:, j * HP:(j + 1) * HP] - m_new)
                lsum = lsum + pj
                ps.append(pj.astype(BF16))
            l_s[...] = lsum
            acc_s[...] = alpha * acc_s[...] + _dot(jnp.concatenate(ps, axis=1), v_ref[rows, :])
            m_s[...] = m_new
            return carry

        lax.fori_loop(0, nk, step, 0)
        l = jnp.sum(l_s[...], axis=-1, keepdims=True)
        o_ref[...] = acc_s[...] / l
        lse_ref[0] = m_s[:, 0:1] + jnp.log(l)

    return pl.pallas_call(
        body, name="attn_fwd", grid=(N_HEADS, S // tq),
        in_specs=[pl.BlockSpec((tq, HP), lambda h, i: (i, h)),
                  pl.BlockSpec((S, HP), lambda h, i: (0, h)),
                  pl.BlockSpec((S, HP), lambda h, i: (0, h))],
        out_specs=[pl.BlockSpec((tq, HP), lambda h, i: (i, h)),
                   pl.BlockSpec((1, tq, 1), lambda h, i: (h, i, 0))],
        out_shape=[jax.ShapeDtypeStruct((S, N_HEADS * HP), F32),
                   jax.ShapeDtypeStruct((N_HEADS, S, 1), F32)],
        scratch_shapes=[pltpu.VMEM((tq, HP), F32), pltpu.VMEM((tq, HP), F32), pltpu.VMEM((tq, HP), F32)],
        compiler_params=_params(("parallel", "parallel")),
    )(q, k, v)


def _memkv(mem, mng, mkg, wmkv):
    M = mem.shape[0]

    def body(mem_ref, mng_ref, mkg_ref, w_ref, mk_ref, mv_ref):
        mn = _rms(mem_ref[...], mng_ref[...]).astype(BF16)
        mkv = _dot(mn, w_ref[...])
        for h in range(MEM_HEADS):
            kraw = mkv[:, 2 * MEM_HD * h:2 * MEM_HD * h + MEM_HD]
            mk_ref[:, MEM_HD * h:MEM_HD * (h + 1)] = _rms(kraw, mkg_ref[...]).astype(BF16)
            mv_ref[:, MEM_HD * h:MEM_HD * (h + 1)] = mkv[:, 2 * MEM_HD * h + MEM_HD:2 * MEM_HD * (h + 1)].astype(BF16)

    return pl.pallas_call(
        body, name="memkv",
        out_shape=[jax.ShapeDtypeStruct((M, MEM_W), BF16)] * 2,
        compiler_params=pltpu.CompilerParams(vmem_limit_bytes=VMEM_LIMIT_V7X),
    )(mem, mng, mkg, wmkv)


def _conv_shifts(cc, cu, hp_ref, hn_ref, i, n_tiles, tm):
    z = cc * cu
    last = PROJ_HALO - 1
    zp = hp_ref[last:last + 1, 0:CONV_W].astype(F32) * hp_ref[last:last + 1, CONV_W:2 * CONV_W].astype(F32)
    zn = hn_ref[0:1, 0:CONV_W].astype(F32) * hn_ref[0:1, CONV_W:2 * CONV_W].astype(F32)
    zp = jnp.where(i == 0, 0.0, zp)
    zn = jnp.where(i == n_tiles - 1, 0.0, zn)
    row = lax.broadcasted_iota(jnp.int32, z.shape, 0)
    z_up = jnp.where(row == 0, zp, pltpu.roll(z, 1, 0))
    z_dn = jnp.where(row == tm - 1, zn, pltpu.roll(z, tm - 1, 0))
    return z, z_up, z_dn


def _halo_specs(tm, S, width, col, rows):
    per = tm // rows
    prev = pl.BlockSpec((rows, width), lambda i: (jnp.maximum(i * per - 1, 0), col))
    nxt = pl.BlockSpec((rows, width), lambda i: (jnp.minimum((i + 1) * per, S // rows - 1), col))
    return prev, nxt


def _mem_attend(qm, mqg, mk_h, mv_h):
    r, qhat = _rms_parts(qm)
    mq = (qhat * mqg).astype(BF16)
    s = _dot_nt(mq, mk_h) * MEM_SCALE
    e = jnp.exp(s - jnp.max(s, axis=-1, keepdims=True))
    p = e / jnp.sum(e, axis=-1, keepdims=True)
    pv = _dot(p.astype(BF16), mv_h)
    return r, qhat, mq, p, pv


def _merge(proj, o, x, bg, convw, mqg, mk, mv, wba, wbc, wbm, wo):
    S = x.shape[0]
    tm = _tile(S, 256)
    nt = S // tm

    def body(main_ref, ccu_ref, hp_ref, hn_ref, o_ref, x_ref, bg_ref, cw_ref, mqg_ref, mk_ref, mv_ref,
             wba_ref, wbc_ref, wbm_ref, wo_ref, xn_ref, oa_ref, oc_ref, om_ref, u_ref, y_ref):
        i = pl.program_id(0)
        sil_a, _ = _silu_and_grad(main_ref[:, O_GA:O_GA + 1024].astype(F32))
        oa = (o_ref[...] * sil_a).astype(BF16)
        oa_ref[...] = oa
        z, z_up, z_dn = _conv_shifts(ccu_ref[:, 0:CONV_W].astype(F32), ccu_ref[:, CONV_W:].astype(F32), hp_ref, hn_ref, i, nt, tm)
        cv = cw_ref[0:1, :] * z_up + cw_ref[1:2, :] * z + cw_ref[2:3, :] * z_dn + cw_ref[3:4, :]
        sil_c, _ = _silu_and_grad(main_ref[:, O_GC:O_GC + CONV_W].astype(F32))
        oc = (main_ref[:, O_CB:O_CB + CONV_W].astype(F32) * cv * sil_c).astype(BF16)
        oc_ref[...] = oc
        sil_m, _ = _silu_and_grad(main_ref[:, O_GM:O_GM + MEM_W].astype(F32))
        for h in range(MEM_HEADS):
            sl = slice(h * MEM_HD, (h + 1) * MEM_HD)
            qm = main_ref[:, O_QM + h * MEM_HD:O_QM + (h + 1) * MEM_HD].astype(F32)
            pv = _mem_attend(qm, mqg_ref[...], mk_ref[:, sl], mv_ref[:, sl])[4]
            om_ref[:, sl] = (pv * sil_m[:, sl]).astype(BF16)
        ua = _dot(oa, wba_ref[...])
        uc = _dot(oc, wbc_ref[...])
        um = _dot(om_ref[...], wbm_ref[...])
        u_ref[:, 0:1024] = ua.astype(BF16)
        u_ref[:, 1024:2048] = uc.astype(BF16)
        u_ref[:, 2048:3072] = um.astype(BF16)
        rg = _sigmoid(main_ref[:, O_R:O_R + 3072].astype(F32) + bg_ref[...])
        y = (rg[:, 0:1024] * ua + rg[:, 1024:2048] * uc + rg[:, 2048:3072] * um).astype(BF16)
        y_ref[...] = y
        xn_ref[...] = x_ref[...] + _dot(y, wo_ref[...])

    row = lambda w: pl.BlockSpec((tm, w), lambda i: (i, 0))
    full = lambda a: pl.BlockSpec(a.shape, lambda i: (0,) * a.ndim)
    hp, hn = _halo_specs(tm, S, 1024, 6, PROJ_HALO)
    return pl.pallas_call(
        body, name="merge", grid=(nt,),
        in_specs=[row(6144), pl.BlockSpec((tm, 1024), lambda i: (i, 6)), hp, hn, row(1024), row(1024),
                  full(bg), full(convw), full(mqg), full(mk), full(mv), full(wba), full(wbc), full(wbm), full(wo)],
        out_specs=[row(1024), row(1024), row(CONV_W), row(MEM_W), row(3072), row(1024)],
        out_shape=[jax.ShapeDtypeStruct((S, 1024), F32), jax.ShapeDtypeStruct((S, 1024), BF16),
                   jax.ShapeDtypeStruct((S, CONV_W), BF16), jax.ShapeDtypeStruct((S, MEM_W), BF16),
                   jax.ShapeDtypeStruct((S, 3072), BF16), jax.ShapeDtypeStruct((S, 1024), BF16)],
        compiler_params=_params(("parallel",)),
    )(proj, proj, proj, proj, o, x, bg, convw, mqg, mk, mv, wba, wbc, wbm, wo)


def _loss_head(xf, tgt):
    S = xf.shape[0]
    tm = _tile(S, 1024)

    def body(x_ref, t_ref, g_ref, acc_ref):
        @pl.when(pl.program_id(0) == 0)
        def _():
            acc_ref[...] = jnp.zeros(acc_ref.shape, F32)

        e = x_ref[...] - t_ref[...]
        g_ref[...] = e * (1.0 / D_MODEL)
        part = jnp.sum((e * e).reshape(tm // 8, 8, D_MODEL), axis=0)
        tot = part[:, 0:128]
        for k in range(1, D_MODEL // 128):
            tot = tot + part[:, 128 * k:128 * (k + 1)]
        acc_ref[...] += tot

    row = pl.BlockSpec((tm, D_MODEL), lambda i: (i, 0))
    return pl.pallas_call(
        body, name="loss_head", grid=(S // tm,),
        in_specs=[row, row],
        out_specs=[row, pl.BlockSpec((8, 128), lambda i: (0, 0))],
        out_shape=[jax.ShapeDtypeStruct((S, D_MODEL), F32), jax.ShapeDtypeStruct((8, 128), F32)],
        compiler_params=_params(("arbitrary",)),
    )(xf, tgt)


def _merge_bwd(g, proj, o, u, bg, convw, mqg, mk, mv, wot, wbat, wbct, wbmt):
    S = g.shape[0]
    tm = _tile(S, 256)
    nt = S // tm
    M = mk.shape[0]

    def body(g_ref, main_ref, ccu_ref, hp_ref, hn_ref, o_ref, u_ref, bg_ref, cw_ref, mqg_ref, mk_ref, mv_ref,
             wot_ref, wbat_ref, wbct_ref, wbmt_ref,
             dmain_ref, dcv_ref, do_ref, delta_ref, du_ref, dbg_ref, dmk_ref, dmv_ref, dmqg_ref):
        i = pl.program_id(0)

        @pl.when(i == 0)
        def _():
            dbg_ref[...] = jnp.zeros(dbg_ref.shape, F32)
            dmk_ref[...] = jnp.zeros(dmk_ref.shape, F32)
            dmv_ref[...] = jnp.zeros(dmv_ref.shape, F32)
            dmqg_ref[...] = jnp.zeros(dmqg_ref.shape, F32)

        dy = _dot(g_ref[...].astype(BF16), wot_ref[...])
        rg = _sigmoid(main_ref[:, O_R:O_R + 3072].astype(F32) + bg_ref[...])
        dyt = jnp.concatenate([dy, dy, dy], axis=1)
        dr = dyt * u_ref[...].astype(F32) * rg * (1.0 - rg)
        dmain_ref[:, O_R:O_R + 3072] = dr.astype(BF16)
        dbg_ref[...] += _colsum(dr)
        du = (dyt * rg).astype(BF16)
        du_ref[...] = du
        do_a = _dot(du[:, 0:1024], wbat_ref[...])
        do_c = _dot(du[:, 1024:2048], wbct_ref[...])
        do_m = _dot(du[:, 2048:3072], wbmt_ref[...])

        sil_a, dsil_a = _silu_and_grad(main_ref[:, O_GA:O_GA + 1024].astype(F32))
        ov = o_ref[...]
        d_o = do_a * sil_a
        do_ref[...] = d_o.astype(BF16)
        dmain_ref[:, O_GA:O_GA + 1024] = (do_a * ov * dsil_a).astype(BF16)
        prod = d_o * ov
        for h in range(N_HEADS):
            delta_ref[h] = jnp.sum(prod[:, h * HP:(h + 1) * HP], axis=-1, keepdims=True)

        z, z_up, z_dn = _conv_shifts(ccu_ref[:, 0:CONV_W].astype(F32), ccu_ref[:, CONV_W:].astype(F32), hp_ref, hn_ref, i, nt, tm)
        cv = cw_ref[0:1, :] * z_up + cw_ref[1:2, :] * z + cw_ref[2:3, :] * z_dn + cw_ref[3:4, :]
        sil_c, dsil_c = _silu_and_grad(main_ref[:, O_GC:O_GC + CONV_W].astype(F32))
        cb = main_ref[:, O_CB:O_CB + CONV_W].astype(F32)
        dmain_ref[:, O_CB:O_CB + CONV_W] = (do_c * cv * sil_c).astype(BF16)
        dmain_ref[:, O_GC:O_GC + CONV_W] = (do_c * cb * cv * dsil_c).astype(BF16)
        dcv_ref[...] = do_c * cb * sil_c

        sil_m, dsil_m = _silu_and_grad(main_ref[:, O_GM:O_GM + MEM_W].astype(F32))
        for h in range(MEM_HEADS):
            sl = slice(h * MEM_HD, (h + 1) * MEM_HD)
            qm = main_ref[:, O_QM + h * MEM_HD:O_QM + (h + 1) * MEM_HD].astype(F32)
            mk_h, mv_h = mk_ref[:, sl], mv_ref[:, sl]
            r, qhat, mq, p, pv = _mem_attend(qm, mqg_ref[...], mk_h, mv_h)
            dom = do_m[:, sl]
            dmain_ref[:, O_GM + h * MEM_HD:O_GM + (h + 1) * MEM_HD] = (dom * pv * dsil_m[:, sl]).astype(BF16)
            dpv = (dom * sil_m[:, sl]).astype(BF16)
            dp = _dot_nt(dpv, mv_h)
            ds = (p * (dp - jnp.sum(dp * p, axis=-1, keepdims=True)) * MEM_SCALE).astype(BF16)
            dmq = _dot(ds, mk_h)
            dmk_ref[:, sl] += _dot_tn(ds, mq)
            dmv_ref[:, sl] += _dot_tn(p.astype(BF16), dpv)
            dmqg_ref[...] += _colsum(dmq * qhat)
            dqm = _rms_bwd(dmq * mqg_ref[...], qhat, r, MEM_HD)
            dmain_ref[:, O_QM + h * MEM_HD:O_QM + (h + 1) * MEM_HD] = dqm.astype(BF16)

    row = lambda w: pl.BlockSpec((tm, w), lambda i: (i, 0))
    full = lambda a: pl.BlockSpec(a.shape, lambda i: (0,) * a.ndim)
    acc = lambda r, c: pl.BlockSpec((r, c), lambda i: (0, 0))
    hp, hn = _halo_specs(tm, S, 1024, 6, PROJ_HALO)
    return pl.pallas_call(
        body, name="merge_bwd", grid=(nt,),
        in_specs=[row(1024), row(6144), pl.BlockSpec((tm, 1024), lambda i: (i, 6)), hp, hn, row(1024), row(3072),
                  full(bg), full(convw), full(mqg), full(mk), full(mv), full(wot), full(wbat), full(wbct), full(wbmt)],
        out_specs=[row(6144), row(CONV_W), row(1024), pl.BlockSpec((N_HEADS, tm, 1), lambda i: (0, i, 0)), row(3072),
                   acc(1, 3072), acc(M, MEM_W), acc(M, MEM_W), acc(1, MEM_HD)],
        out_shape=[jax.ShapeDtypeStruct((S, 6144), BF16), jax.ShapeDtypeStruct((S, CONV_W), F32),
                   jax.ShapeDtypeStruct((S, 1024), BF16), jax.ShapeDtypeStruct((N_HEADS, S, 1), F32),
                   jax.ShapeDtypeStruct((S, 3072), BF16), jax.ShapeDtypeStruct((1, 3072), F32),
                   jax.ShapeDtypeStruct((M, MEM_W), F32), jax.ShapeDtypeStruct((M, MEM_W), F32),
                   jax.ShapeDtypeStruct((1, MEM_HD), F32)],
        compiler_params=_params(("arbitrary",)),
    )(g, proj, proj, proj, proj, o, u, bg, convw, mqg, mk, mv, wot, wbat, wbct, wbmt)


def _conv_bwd(dcv, proj, convw):
    S = dcv.shape[0]
    tm = _tile(S, 512)
    nt = S // tm

    def body(d_ref, dp_ref, dn_ref, ccu_ref, hp_ref, hn_ref, cw_ref, dccu_ref, dcw_ref):
        i = pl.program_id(0)

        @pl.when(i == 0)
        def _():
            dcw_ref[...] = jnp.zeros(dcw_ref.shape, F32)

        cc, cu = ccu_ref[:, 0:CONV_W].astype(F32), ccu_ref[:, CONV_W:].astype(F32)
        z, z_up, z_dn = _conv_shifts(cc, cu, hp_ref, hn_ref, i, nt, tm)
        d = d_ref[...]
        dprev = jnp.where(i == 0, 0.0, dp_ref[7:8, :])
        dnext = jnp.where(i == nt - 1, 0.0, dn_ref[0:1, :])
        row = lax.broadcasted_iota(jnp.int32, d.shape, 0)
        d_up = jnp.where(row == 0, dprev, pltpu.roll(d, 1, 0))
        d_dn = jnp.where(row == tm - 1, dnext, pltpu.roll(d, tm - 1, 0))
        dz = cw_ref[0:1, :] * d_dn + cw_ref[1:2, :] * d + cw_ref[2:3, :] * d_up
        dccu_ref[:, 0:CONV_W] = (dz * cu).astype(BF16)
        dccu_ref[:, CONV_W:] = (dz * cc).astype(BF16)
        dcw_ref[0:1, :] += _colsum(d * z_up)
        dcw_ref[1:2, :] += _colsum(d * z)
        dcw_ref[2:3, :] += _colsum(d * z_dn)
        dcw_ref[3:4, :] += _colsum(d)

    hp, hn = _halo_specs(tm, S, 1024, 6, PROJ_HALO)
    dp, dn = _halo_specs(tm, S, CONV_W, 0, F32_HALO)
    return pl.pallas_call(
        body, name="conv_bwd", grid=(nt,),
        in_specs=[pl.BlockSpec((tm, CONV_W), lambda i: (i, 0)), dp, dn,
                  pl.BlockSpec((tm, 1024), lambda i: (i, 6)), hp, hn,
                  pl.BlockSpec((8, CONV_W), lambda i: (0, 0))],
        out_specs=[pl.BlockSpec((tm, 1024), lambda i: (i, 0)), pl.BlockSpec((8, CONV_W), lambda i: (0, 0))],
        out_shape=[jax.ShapeDtypeStruct((S, 1024), BF16), jax.ShapeDtypeStruct((8, CONV_W), F32)],
        compiler_params=_params(("arbitrary",)),
    )(dcv, dcv, dcv, proj, proj, proj, convw)


def _attn_bwd(q, k, v, do, qt, dot, lse, delta):
    S = q.shape[0]
    tq, tk = _tile(S, 512), _tile(S, 2048)
    ni, nk = S // tq, S // tk

    def body(q_ref, do_ref, qt_ref, dot_ref, k_ref, v_ref, lse_ref, delta_ref, dq_ref, dkt_hbm, dvt_hbm,
             dq_s, dkt_s, dvt_s, sem):
        h, i = pl.program_id(0), pl.program_id(1)

        @pl.when(i == 0)
        def _():
            dkt_s[...] = jnp.zeros(dkt_s.shape, F32)
            dvt_s[...] = jnp.zeros(dvt_s.shape, F32)

        dq_s[...] = jnp.zeros(dq_s.shape, F32)
        qv, dov, qtv, dotv = q_ref[...], do_ref[...], qt_ref[...], dot_ref[...]
        lse_c, delta_c = lse_ref[0], delta_ref[0]

        def step(c, carry):
            cols = pl.ds(pl.multiple_of(c * tk, tk), tk)
            kc, vc = k_ref[cols, :], v_ref[cols, :]
            p = jnp.exp(_dot_nt(qv, kc) - lse_c)
            dp = _dot_nt(dov, vc)
            ds = (p * (dp - delta_c)).astype(BF16)
            dq_s[...] += _dot(ds, kc)
            dvt_s[:, cols] += _dot(dotv, p.astype(BF16))
            dkt_s[:, cols] += _dot(qtv, ds)
            return carry

        lax.fori_loop(0, nk, step, 0)
        dq_ref[...] = dq_s[...]

        @pl.when(i == ni - 1)
        def _():
            head = pl.ds(pl.multiple_of(h * HP, HP), HP)
            out_k = pltpu.make_async_copy(dkt_s, dkt_hbm.at[head, :], sem.at[0])
            out_v = pltpu.make_async_copy(dvt_s, dvt_hbm.at[head, :], sem.at[1])
            out_k.start()
            out_v.start()
            out_k.wait()
            out_v.wait()

    col = pl.BlockSpec((1, tq, 1), lambda h, i: (h, i, 0))
    blk = pl.BlockSpec((tq, HP), lambda h, i: (i, h))
    blkt = pl.BlockSpec((HP, tq), lambda h, i: (h, i))
    res = pl.BlockSpec((S, HP), lambda h, i: (0, h))
    whole = pl.BlockSpec(memory_space=pl.ANY)
    return pl.pallas_call(
        body, name="attn_bwd", grid=(N_HEADS, ni),
        in_specs=[blk, blk, blkt, blkt, res, res, col, col],
        out_specs=[blk, whole, whole],
        out_shape=[jax.ShapeDtypeStruct((S, N_HEADS * HP), F32), jax.ShapeDtypeStruct((N_HEADS * HP, S), F32),
                   jax.ShapeDtypeStruct((N_HEADS * HP, S), F32)],
        scratch_shapes=[pltpu.VMEM((tq, HP), F32), pltpu.VMEM((HP, S), F32), pltpu.VMEM((HP, S), F32),
                        pltpu.SemaphoreType.DMA((2,))],
        compiler_params=_params(("parallel", "arbitrary")),
    )(q, do, qt, dot, k, v, lse, delta)


def _mla_prep_bwd(proj, tabs, dq, dk, dv, qg, kvg, qhg, khg, wuq, wuk, wuqt, wukt, wuvt):
    S = proj.shape[0]
    tm = _tile(S, 256)

    def body(a_ref, c_ref, sa_ref, sb_ref, dq_ref, dk_ref, dv_ref, qg_ref, kvg_ref, qhg_ref, khg_ref,
             wuq_ref, wuk_ref, wuqt_ref, wukt_ref, wuvt_ref,
             da_ref, dqraw_ref, dkraw_ref, dqg_ref, dkvg_ref, dqhg_ref, dkhg_ref):
        @pl.when(pl.program_id(0) == 0)
        def _():
            dqg_ref[...] = jnp.zeros(dqg_ref.shape, F32)
            dkvg_ref[...] = jnp.zeros(dkvg_ref.shape, F32)
            dqhg_ref[...] = jnp.zeros(dqhg_ref.shape, F32)
            dkhg_ref[...] = jnp.zeros(dkhg_ref.shape, F32)

        ql = a_ref[:, 0:Q_RANK].astype(F32)
        kvl = a_ref[:, Q_RANK:Q_RANK + KV_RANK].astype(F32)
        kpe = a_ref[:, Q_RANK + KV_RANK:Q_RANK + KV_RANK + HP].astype(F32)
        rq, qhat = _rms_parts(ql)
        rkv, kvhat = _rms_parts(kvl)
        qraw = _dot((qhat * qg_ref[...]).astype(BF16), wuq_ref[...])
        kn = _dot((kvhat * kvg_ref[...]).astype(BF16), wuk_ref[...])
        c, sa, sb = c_ref[...], sa_ref[...], sb_ref[...]
        dkpe = jnp.zeros(kpe.shape, F32)
        dqhg = jnp.zeros((1, HP), F32)
        dkhg = jnp.zeros((1, HP), F32)
        for h in range(N_HEADS):
            sl = slice(h * HP, (h + 1) * HP)
            r, that = _rms_parts(qraw[:, sl], QK_DIM)
            dtn = _rope_t(dq_ref[:, sl], c, sa, sb) * SCALE
            dqhg = dqhg + _colsum(dtn * that)
            dqraw_ref[:, sl] = _rms_bwd(dtn * qhg_ref[...], that, r, QK_DIM).astype(BF16)
            r, that = _rms_parts(kn[:, sl] + kpe, QK_DIM)
            dtn = _rope_t(dk_ref[:, sl], c, sa, sb)
            dkhg = dkhg + _colsum(dtn * that)
            dkr = _rms_bwd(dtn * khg_ref[...], that, r, QK_DIM)
            dkraw_ref[:, sl] = dkr.astype(BF16)
            dkpe = dkpe + dkr
        dqhg_ref[...] += dqhg
        dkhg_ref[...] += dkhg
        dqn = _dot(dqraw_ref[...], wuqt_ref[...])
        dqg_ref[...] += _colsum(dqn * qhat)
        da_ref[:, 0:Q_RANK] = _rms_bwd(dqn * qg_ref[...], qhat, rq, Q_RANK).astype(BF16)
        dkvn = _dot(dkraw_ref[...], wukt_ref[...]) + _dot(dv_ref[...], wuvt_ref[...])
        dkvg_ref[...] += _colsum(dkvn * kvhat)
        da_ref[:, Q_RANK:Q_RANK + KV_RANK] = _rms_bwd(dkvn * kvg_ref[...], kvhat, rkv, KV_RANK).astype(BF16)
        da_ref[:, Q_RANK + KV_RANK:Q_RANK + KV_RANK + HP] = dkpe.astype(BF16)
        da_ref[:, Q_RANK + KV_RANK + HP:] = jnp.zeros((tm, 1024 - Q_RANK - KV_RANK - HP), BF16)

    row = lambda w: pl.BlockSpec((tm, w), lambda i: (i, 0))
    full = lambda a: pl.BlockSpec(a.shape, lambda i: (0,) * a.ndim)
    acc = lambda c: pl.BlockSpec((1, c), lambda i: (0, 0))
    return pl.pallas_call(
        body, name="mla_prep_bwd", grid=(S // tm,),
        in_specs=[pl.BlockSpec((tm, 1024), lambda i: (i, 7)), row(HP), row(HP), row(HP),
                  row(1024), row(1024), row(1024), full(qg), full(kvg), full(qhg), full(khg),
                  full(wuq), full(wuk), full(wuqt), full(wukt), full(wuvt)],
        out_specs=[row(1024), row(1024), row(1024), acc(Q_RANK), acc(KV_RANK), acc(HP), acc(HP)],
        out_shape=[jax.ShapeDtypeStruct((S, 1024), BF16)] * 3
        + [jax.ShapeDtypeStruct((1, Q_RANK), F32), jax.ShapeDtypeStruct((1, KV_RANK), F32),
           jax.ShapeDtypeStruct((1, HP), F32), jax.ShapeDtypeStruct((1, HP), F32)],
        compiler_params=_params(("arbitrary",)),
    )(proj, *tabs, dq, dk, dv, qg, kvg, qhg, khg, wuq, wuk, wuqt, wukt, wuvt)


def _inproj_bwd(dmain, dccu, dsega, wint, x, g, ng):
    S = x.shape[0]
    tm, tk = _tile(S, 512), 2048
    nk = PW // tk

    nmain = dmain.shape[1] // tk
    assert dmain.shape[1] % tk == 0 and dccu.shape[1] + dsega.shape[1] == tk and nk == nmain + 1

    def body(dm_ref, dc_ref, da_ref, w_ref, x_ref, g_ref, ng_ref, dx_ref, dng_ref, acc_s):
        i, k = pl.program_id(0), pl.program_id(1)

        @pl.when((i == 0) & (k == 0))
        def _():
            dng_ref[...] = jnp.zeros(dng_ref.shape, F32)

        @pl.when(k == 0)
        def _():
            acc_s[...] = jnp.zeros(acc_s.shape, F32)

        @pl.when(k < nmain)
        def _():
            acc_s[...] += _dot(dm_ref[...], w_ref[...])

        @pl.when(k == nk - 1)
        def _():
            acc_s[...] += _dot(dc_ref[...], w_ref[0:1024, :]) + _dot(da_ref[...], w_ref[1024:2048, :])
            r, xhat = _rms_parts(x_ref[...])
            dh = acc_s[...]
            dng_ref[...] += _colsum(dh * xhat)
            dx_ref[...] = g_ref[...] + _rms_bwd(dh * ng_ref[...], xhat, r, D_MODEL)

    return pl.pallas_call(
        body, name="inproj_bwd", grid=(S // tm, nk),
        in_specs=[pl.BlockSpec((tm, tk), lambda i, k: (i, jnp.minimum(k, nmain - 1))),
                  pl.BlockSpec((tm, 1024), lambda i, k: (i, 0)),
                  pl.BlockSpec((tm, 1024), lambda i, k: (i, 0)),
                  pl.BlockSpec((tk, D_MODEL), lambda i, k: (k, 0)),
                  pl.BlockSpec((tm, D_MODEL), lambda i, k: (i, 0)),
                  pl.BlockSpec((tm, D_MODEL), lambda i, k: (i, 0)),
                  pl.BlockSpec((1, D_MODEL), lambda i, k: (0, 0))],
        out_specs=[pl.BlockSpec((tm, D_MODEL), lambda i, k: (i, 0)),
                   pl.BlockSpec((1, D_MODEL), lambda i, k: (0, 0))],
        out_shape=[jax.ShapeDtypeStruct((S, D_MODEL), F32), jax.ShapeDtypeStruct((1, D_MODEL), F32)],
        scratch_shapes=[pltpu.VMEM((tm, D_MODEL), F32)],
        compiler_params=_params(("arbitrary", "arbitrary")),
    )(dmain, dccu, dsega, wint, x, g, ng)


def _memkv_bwd(mem, mng, mkg, wmkv, wmkvt, dmk, dmv):
    M = mem.shape[0]

    def body(mem_ref, mng_ref, mkg_ref, w_ref, wt_ref, dmk_ref, dmv_ref, dw_ref, dmng_ref, dmkg_ref, d_s):
        r, mhat = _rms_parts(mem_ref[...])
        mn = (mhat * mng_ref[...]).astype(BF16)
        mkv = _dot(mn, w_ref[...])
        dmkg = jnp.zeros((1, MEM_HD), F32)
        for h in range(MEM_HEADS):
            sl = slice(h * MEM_HD, (h + 1) * MEM_HD)
            rk, khat = _rms_parts(mkv[:, 2 * MEM_HD * h:2 * MEM_HD * h + MEM_HD])
            dkn = dmk_ref[:, sl]
            dmkg = dmkg + _colsum(dkn * khat)
            d_s[:, 2 * MEM_HD * h:2 * MEM_HD * h + MEM_HD] = _rms_bwd(dkn * mkg_ref[...], khat, rk, MEM_HD).astype(BF16)
            d_s[:, 2 * MEM_HD * h + MEM_HD:2 * MEM_HD * (h + 1)] = dmv_ref[:, sl].astype(BF16)
        dmkg_ref[...] = dmkg
        dw_ref[...] = _dot_tn(mn, d_s[...])
        dmn = _dot(d_s[...], wt_ref[...])
        dmng_ref[...] = _colsum(dmn * mhat)

    return pl.pallas_call(
        body, name="memkv_bwd",
        out_shape=[jax.ShapeDtypeStruct((D_MODEL, 2 * MEM_W), F32), jax.ShapeDtypeStruct((1, D_MODEL), F32),
                   jax.ShapeDtypeStruct((1, MEM_HD), F32)],
        scratch_shapes=[pltpu.VMEM((M, 2 * MEM_W), BF16)],
        compiler_params=pltpu.CompilerParams(vmem_limit_bytes=VMEM_LIMIT_V7X),
    )(mem, mng, mkg, wmkv, wmkvt, dmk, dmv)


def _mm_tn(a, b, name, col0=0, ncols=None):
    S, M = a.shape
    N = b.shape[1] if ncols is None else ncols
    tm, tn, ts = _tile(M, 1024), _tile(N, 1024), _tile(S, 512)
    assert col0 % tn == 0
    jb = col0 // tn

    def body(a_ref, b_ref, o_ref):
        @pl.when(pl.program_id(2) == 0)
        def _():
            o_ref[...] = jnp.zeros(o_ref.shape, F32)

        o_ref[...] += _dot_tn(a_ref[...].astype(BF16), b_ref[...].astype(BF16))

    return pl.pallas_call(
        body, name=name, grid=(M // tm, N // tn, S // ts),
        in_specs=[pl.BlockSpec((ts, tm), lambda i, j, k: (k, i)),
                  pl.BlockSpec((ts, tn), lambda i, j, k: (k, j + jb))],
        out_specs=pl.BlockSpec((tm, tn), lambda i, j, k: (i, j)),
        out_shape=jax.ShapeDtypeStruct((M, N), F32),
        compiler_params=_params(("parallel", "parallel", "arbitrary")),
    )(a, b)


def _adamw(w, g0, g1, m, v, name):
    R, C = w.shape
    tr = R
    for cand in (512, 256, 128, 64, 32, 16, 8):
        if R % cand == 0 and cand * C * 4 <= (1 << 20):
            tr = cand
            break
    c1 = 1.0 / (1.0 - ADAM_B1 ** ADAM_STEP)
    c2 = 1.0 / (1.0 - ADAM_B2 ** ADAM_STEP)

    def body(w_ref, g0_ref, g1_ref, m_ref, v_ref, g_ref, d_ref, nm_ref, nv_ref):
        g = g0_ref[...] + g1_ref[...]
        nm = ADAM_B1 * m_ref[...] + (1.0 - ADAM_B1) * g
        nv = ADAM_B2 * v_ref[...] + (1.0 - ADAM_B2) * (g * g)
        g_ref[...] = g
        nm_ref[...] = nm
        nv_ref[...] = nv
        d_ref[...] = -ADAM_LR * ((nm * c1) / (jnp.sqrt(nv * c2) + ADAM_EPS) + ADAM_WD * w_ref[...])

    blk = pl.BlockSpec((tr, C), lambda i: (i, 0))
    return pl.pallas_call(
        body, name=name, grid=(R // tr,),
        in_specs=[blk] * 5, out_specs=[blk] * 4,
        out_shape=[jax.ShapeDtypeStruct((R, C), F32)] * 4,
        compiler_params=_params(("parallel",)),
    )(w, g0, g1, m, v)


def _sum_slabs(a, name):
    K, R, C = a.shape
    tr = R
    for cand in (512, 256, 128, 64, 32, 16, 8):
        if R % cand == 0 and cand * C * 4 * K <= (4 << 20):
            tr = cand
            break

    def body(a_ref, o_ref):
        t = a_ref[0].astype(F32)
        for k in range(1, K):
            t = t + a_ref[k].astype(F32)
        o_ref[...] = t

    return pl.pallas_call(
        body, name=name, grid=(R // tr,),
        in_specs=[pl.BlockSpec((K, tr, C), lambda i: (0, i, 0))],
        out_specs=pl.BlockSpec((tr, C), lambda i: (i, 0)),
        out_shape=jax.ShapeDtypeStruct((R, C), F32),
        compiler_params=_params(("parallel",)),
    )(a)


MESH = pl.DeviceIdType.MESH
HBM = pl.BlockSpec(memory_space=pltpu.HBM)


def _chip_peer(x, y, c, k):
    return (x ^ (k >> 1), y ^ (k & 1), c)


def _gather_chips(arrs):
    n = len(arrs)
    half = arrs[0].shape[0] // 2
    assert all(a.shape[0] == 2 * half for a in arrs)

    def body(*refs):
        ins, outs = refs[:n], refs[n:2 * n]
        send1, recv1, send2, recv2, loc = refs[2 * n:]
        x, y, c = lax.axis_index("x"), lax.axis_index("y"), lax.axis_index("c")
        me = 2 * x + y
        mine, theirs = pl.ds(c * half, half), pl.ds((1 - c) * half, half)
        sibling = (x, y, 1 - c)
        waits = []
        for a in range(n):
            own = pltpu.make_async_copy(ins[a], outs[a].at[me], loc.at[a])
            own.start()
            waits.append(own.wait)

        def over_ici(a, k, src_chip):
            return pltpu.make_async_remote_copy(
                src_ref=ins[a].at[mine], dst_ref=outs[a].at[src_chip, mine], send_sem=send1.at[3 * a + k - 1],
                recv_sem=recv1.at[3 * a + k - 1], device_id=_chip_peer(x, y, c, k), device_id_type=MESH)

        def to_sibling(a, k, layers):
            block = outs[a].at[me ^ k, layers]
            return pltpu.make_async_remote_copy(
                src_ref=block, dst_ref=block, send_sem=send2.at[3 * a + k - 1], recv_sem=recv2.at[3 * a + k - 1],
                device_id=sibling, device_id_type=MESH)

        for a in range(n):
            for k in (1, 2, 3):
                cp = over_ici(a, k, me)
                cp.start()
                waits.append(cp.wait_send)
        for a in range(n):
            for k in (1, 2, 3):
                over_ici(a, k, me ^ k).wait_recv()
                cp = to_sibling(a, k, mine)
                cp.start()
                waits.append(cp.wait_send)
        for a in range(n):
            for k in (1, 2, 3):
                to_sibling(a, k, theirs).wait_recv()
        for w in waits:
            w()

    return pl.pallas_call(
        body, name="gather_weights",
        in_specs=[HBM] * n, out_specs=[HBM] * n,
        out_shape=[jax.ShapeDtypeStruct((4,) + a.shape, a.dtype) for a in arrs],
        scratch_shapes=[pltpu.SemaphoreType.DMA((3 * n,)), pltpu.SemaphoreType.DMA((3 * n,)),
                        pltpu.SemaphoreType.DMA((3 * n,)), pltpu.SemaphoreType.DMA((3 * n,)),
                        pltpu.SemaphoreType.DMA((n,))],
    )(*arrs)


def _scatter_chips(arrs, small):
    n = len(arrs)

    def body(*refs):
        ins, small_in = refs[:n], refs[n]
        outs, small_out = refs[n + 1:2 * n + 1], refs[2 * n + 1]
        send, recv, loc, ssend, srecv = refs[2 * n + 2:]
        x, y, c = lax.axis_index("x"), lax.axis_index("y"), lax.axis_index("c")
        me = 2 * x + y
        me8 = 4 * x + 2 * y + c
        copies = []
        for a in range(n):
            own = pltpu.make_async_copy(ins[a].at[me], outs[a].at[me], loc.at[a])
            own.start()
            copies.append(own)
        own = pltpu.make_async_copy(small_in, small_out.at[me8], loc.at[n])
        own.start()
        copies.append(own)
        for k in range(1, 8):
            cp = pltpu.make_async_remote_copy(
                src_ref=small_in, dst_ref=small_out.at[me8], send_sem=ssend.at[k - 1], recv_sem=srecv.at[k - 1],
                device_id=(x ^ (k >> 2), y ^ ((k >> 1) & 1), c ^ (k & 1)), device_id_type=MESH)
            cp.start()
            copies.append(cp)
        for a in range(n):
            for k in (1, 2, 3):
                cp = pltpu.make_async_remote_copy(
                    src_ref=ins[a].at[me ^ k], dst_ref=outs[a].at[me], send_sem=send.at[3 * a + k - 1],
                    recv_sem=recv.at[3 * a + k - 1], device_id=_chip_peer(x, y, c, k), device_id_type=MESH)
                cp.start()
                copies.append(cp)
        for cp in copies:
            cp.wait()

    return pl.pallas_call(
        body, name="scatter_grads",
        in_specs=[HBM] * (n + 1), out_specs=[HBM] * (n + 1),
        out_shape=[jax.ShapeDtypeStruct(a.shape, a.dtype) for a in arrs]
        + [jax.ShapeDtypeStruct((8,) + small.shape, small.dtype)],
        scratch_shapes=[pltpu.SemaphoreType.DMA((3 * n,)), pltpu.SemaphoreType.DMA((3 * n,)),
                        pltpu.SemaphoreType.DMA((n + 1,)), pltpu.SemaphoreType.DMA((7,)),
                        pltpu.SemaphoreType.DMA((7,))],
    )(*arrs, small)


def _swap_cores(arrs):
    n = len(arrs)

    def body(*refs):
        ins, outs = refs[:n], refs[n:2 * n]
        send, recv = refs[2 * n:]
        x, y, c = lax.axis_index("x"), lax.axis_index("y"), lax.axis_index("c")
        copies = []
        for a in range(n):
            cp = pltpu.make_async_remote_copy(
                src_ref=ins[a], dst_ref=outs[a], send_sem=send.at[a], recv_sem=recv.at[a],
                device_id=(x, y, 1 - c), device_id_type=MESH)
            cp.start()
            copies.append(cp)
        for cp in copies:
            cp.wait()

    return pl.pallas_call(
        body, name="swap_cores",
        in_specs=[HBM] * n, out_specs=[HBM] * n,
        out_shape=[jax.ShapeDtypeStruct(a.shape, a.dtype) for a in arrs],
        scratch_shapes=[pltpu.SemaphoreType.DMA((n,)), pltpu.SemaphoreType.DMA((n,))],
    )(*arrs)


def _pad_last(a, n):
    return jnp.pad(a, [(0, 0)] * (a.ndim - 1) + [(0, n - a.shape[-1])])


def _pad_w_in(w):
    lead = w.shape[:-1]
    seg = lambda a, b: w[..., a:b]
    ga = _pad_last(seg(2720, 3232).reshape(lead + (N_HEADS, V_DIM)), HP).reshape(lead + (1024,))
    kpe = jnp.pad(seg(640, 672), [(0, 0)] * len(lead) + [(NOPE_DIM, HP - QK_DIM)])
    zero = jnp.zeros(lead + (PW - 7936,), w.dtype)
    return jnp.concatenate(
        [seg(4256, 7328), ga, seg(672, 1184), seg(2208, 2720), seg(3232, 3744), seg(3744, 4256),
         seg(1184, 1696), seg(1696, 2208), seg(0, 384), seg(384, 640), kpe, zero], axis=-1)


def _unpad_w_in(w):
    lead = w.shape[:-1]
    seg = lambda a, n: w[..., a:a + n]
    ga = seg(O_GA, 1024).reshape(lead + (N_HEADS, HP))[..., :V_DIM].reshape(lead + (N_HEADS * V_DIM,))
    return jnp.concatenate(
        [seg(O_QL, 384), seg(O_KVL, 256), seg(O_KPE + NOPE_DIM, ROPE_DIM), seg(O_CB, 512), seg(O_CC, 512),
         seg(O_CU, 512), seg(O_QM, 512), ga, seg(O_GC, 512), seg(O_GM, 512), seg(O_R, 3072)], axis=-1)


def _cols_from_shards(g):
    _, L, R, C = g.shape
    return jnp.transpose(g, (1, 2, 0, 3)).reshape(L, R, 4 * C)


def _cols_to_shards(w):
    L, R, C4 = w.shape
    return jnp.transpose(w.reshape(L, R, 4, C4 // 4), (2, 0, 1, 3))


def _rows_from_shards(g):
    _, L, R, C = g.shape
    return jnp.transpose(g, (1, 0, 2, 3)).reshape(L, 4 * R, C)


def _rows_to_shards(w):
    L, R4, C = w.shape
    return jnp.transpose(w.reshape(L, 4, R4 // 4, C), (1, 0, 2, 3))


def _t(w):
    return jnp.swapaxes(w, -1, -2)


def _layer_fwd(x, mem, tabs, p):
    proj, xn = _inproj(x, p["norm_g"], p["w_in"])
    q, k, v, qn, kvn = _mla_prep(proj, tabs, p["q_norm_g"], p["kv_norm_g"], p["q_head_g"], p["k_head_g"],
                                 p["w_uq"], p["w_uk"], p["w_uv"])
    o, lse = _attn_fwd(q, k, v)
    mk, mv = _memkv(mem, p["mem_norm_g"], p["mem_k_g"], p["w_mkv"])
    x_new, oa, oc, om, u, y = _merge(proj, o, x, p["b_gate"], p["conv_wb"], p["mem_q_g"], mk, mv,
                                     p["w_br_attn"], p["w_br_conv"], p["w_br_mem"], p["w_out"])
    saved = dict(x=x, proj=proj, xn=xn, q=q, k=k, v=v, qn=qn, kvn=kvn, o=o, lse=lse, mk=mk, mv=mv,
                 oa=oa, oc=oc, om=om, u=u, y=y)
    return x_new, saved


def _layer_bwd(g, mem, tabs, p, s):
    S = g.shape[0]
    dmain, dcv, d_o, delta, du, dbg, dmk, dmv, dmqg = _merge_bwd(
        g, s["proj"], s["o"], s["u"], p["b_gate"], p["conv_wb"], p["mem_q_g"], s["mk"], s["mv"],
        p["w_out_t"], p["w_br_attn_t"], p["w_br_conv_t"], p["w_br_mem_t"])
    dccu, dconv = _conv_bwd(dcv, s["proj"], p["conv_wb"])
    dq, dkt, dvt = _attn_bwd(s["q"], s["k"], s["v"], d_o, _t(s["q"]), _t(d_o), s["lse"], delta)
    dk, dv = _t(dkt), _t(dvt).astype(BF16)
    dsega, dqraw, dkraw, dqg, dkvg, dqhg, dkhg = _mla_prep_bwd(
        s["proj"], tabs, dq, dk, dv, p["q_norm_g"], p["kv_norm_g"], p["q_head_g"], p["k_head_g"],
        p["w_uq"], p["w_uk"], p["w_uq_t"], p["w_uk_t"], p["w_uv_t"])
    dx, dng = _inproj_bwd(dmain, dccu, dsega, p["w_in_t"], s["x"], g, p["norm_g"])
    dwmkv, dmng, dmkg = _memkv_bwd(mem, p["mem_norm_g"], p["mem_k_g"], p["w_mkv"], p["w_mkv_t"], dmk, dmv)
    grads = dict(
        norm_g=dng, b_gate=dbg, q_norm_g=dqg, kv_norm_g=dkvg, q_head_g=dqhg, k_head_g=dkhg,
        conv_wb=dconv, mem_norm_g=dmng, mem_q_g=dmqg, mem_k_g=dmkg, w_mkv=dwmkv,
        w_in=jnp.concatenate([_mm_tn(s["xn"], dmain, "grad_w_in"), _mm_tn(s["xn"], dccu, "grad_w_in_conv"),
                              _mm_tn(s["xn"], dsega, "grad_w_in_lat")], axis=1),
        w_uq=_mm_tn(s["qn"], dqraw, "grad_w_uq"),
        w_uk=_mm_tn(s["kvn"], dkraw, "grad_w_uk"),
        w_uv=_mm_tn(s["kvn"], dv, "grad_w_uv"),
        w_br_attn=_mm_tn(s["oa"], du, "grad_w_br_attn", 0, 1024),
        w_br_conv=_mm_tn(s["oc"], du, "grad_w_br_conv", 1024, 1024),
        w_br_mem=_mm_tn(s["om"], du, "grad_w_br_mem", 2048, 1024),
        w_out=_mm_tn(s["y"], g, "grad_w_out"),
    )
    return dx, grads


def _layer_params(full, l):
    p = {}
    w_in = _pad_w_in(full["w_in"][l])
    w_uq = _pad_last(full["w_uq"][l].reshape(Q_RANK, N_HEADS, QK_DIM), HP).reshape(Q_RANK, 1024)
    ukv = full["w_ukv"][l].reshape(KV_RANK, N_HEADS, NOPE_DIM + V_DIM)
    w_uk = _pad_last(ukv[..., :NOPE_DIM], HP).reshape(KV_RANK, 1024)
    w_uv = _pad_last(ukv[..., NOPE_DIM:], HP).reshape(KV_RANK, 1024)
    w_ba = jnp.pad(full["w_br_attn"][l].reshape(N_HEADS, V_DIM, D_MODEL), ((0, 0), (0, HP - V_DIM), (0, 0)))
    w_ba = w_ba.reshape(1024, D_MODEL)
    p.update(w_in=w_in, w_uq=w_uq, w_uk=w_uk, w_uv=w_uv, w_br_attn=w_ba, w_br_conv=full["w_br_conv"][l],
             w_br_mem=full["w_br_mem"][l], w_out=full["w_out"][l], w_mkv=full["w_mkv"][l])
    for n in ("w_in", "w_uq", "w_uk", "w_uv", "w_br_attn", "w_br_conv", "w_br_mem", "w_out", "w_mkv"):
        p[n + "_t"] = _t(p[n])
    for n in ("norm_g", "b_gate", "q_norm_g", "kv_norm_g", "mem_norm_g", "mem_q_g", "mem_k_g"):
        p[n] = full[n][l][None, :]
    p["q_head_g"] = _pad_last(full["q_head_g"][l][None, :], HP)
    p["k_head_g"] = _pad_last(full["k_head_g"][l][None, :], HP)
    p["conv_wb"] = jnp.concatenate(
        [full["conv_w"][l], full["conv_b"][l][None, :], jnp.zeros((4, CONV_W), F32)], axis=0)
    return p


def _local_step(x, mem, positions, full, target):
    S = x.shape[0]
    invf16 = ROPE_BASE ** (-jnp.arange(0, ROPE_DIM, 2, dtype=F32) / ROPE_DIM)
    invf = jnp.concatenate([jnp.zeros((NOPE_DIM,), F32), invf16, invf16, jnp.zeros((HP - QK_DIM,), F32)])[None, :]
    tabs = _rope_tables(jnp.broadcast_to(positions.reshape(S, 1), (S, HP)), invf)
    params = [_layer_params(full, l) for l in range(DEPTH)]
    saved = []
    h = x
    for l in range(DEPTH):
        h, s = _layer_fwd(h, mem, tabs, params[l])
        saved.append(s)
    g, loss_part = _loss_head(h, target)
    per_layer = [None] * DEPTH
    for l in reversed(range(DEPTH)):
        g, per_layer[l] = _layer_bwd(g, mem, tabs, params[l], saved[l])
    st = lambda n: jnp.stack([per_layer[l][n] for l in range(DEPTH)])
    out = {}
    out["w_in"] = _unpad_w_in(st("w_in"))
    out["w_uq"] = st("w_uq").reshape(DEPTH, Q_RANK, N_HEADS, HP)[..., :QK_DIM].reshape(DEPTH, Q_RANK, N_HEADS * QK_DIM)
    duk = st("w_uk").reshape(DEPTH, KV_RANK, N_HEADS, HP)[..., :NOPE_DIM]
    duv = st("w_uv").reshape(DEPTH, KV_RANK, N_HEADS, HP)[..., :V_DIM]
    out["w_ukv"] = jnp.concatenate([duk, duv], axis=-1).reshape(DEPTH, KV_RANK, 1024)
    out["w_br_attn"] = st("w_br_attn").reshape(DEPTH, N_HEADS, HP, D_MODEL)[:, :, :V_DIM].reshape(DEPTH, 512, D_MODEL)
    for n in ("w_br_conv", "w_br_mem", "w_out", "w_mkv"):
        out[n] = st(n)
    for n in ("norm_g", "b_gate", "q_norm_g", "kv_norm_g", "mem_norm_g", "mem_q_g", "mem_k_g"):
        out[n] = st(n)[:, 0, :]
    out["q_head_g"] = st("q_head_g")[:, 0, :QK_DIM]
    out["k_head_g"] = st("k_head_g")[:, 0, :QK_DIM]
    cwb = st("conv_wb")
    out["conv_w"] = cwb[:, 0:3, :]
    out["conv_b"] = cwb[:, 3, :]
    return loss_part, g, out


_COL_SHARDED = ("w_in", "w_uq", "w_ukv", "w_br_attn", "w_br_conv", "w_br_mem")
_ROW_SHARDED = ("w_mkv", "w_out")
_BIG = _COL_SHARDED + _ROW_SHARDED
_SMALL = ("norm_g", "b_gate", "q_norm_g", "kv_norm_g", "q_head_g", "k_head_g", "conv_w", "conv_b",
          "mem_norm_g", "mem_q_g", "mem_k_g")
_ORDER = ("norm_g", "w_in", "b_gate", "q_norm_g", "w_uq", "kv_norm_g", "w_ukv", "q_head_g", "k_head_g",
          "conv_w", "conv_b", "mem_norm_g", "w_mkv", "mem_q_g", "mem_k_g", "w_br_attn", "w_br_conv",
          "w_br_mem", "w_out")


def _pack_small(d, extra):
    flat = jnp.concatenate([d[n].reshape(-1) for n in _SMALL] + [extra.reshape(-1)])
    n = flat.shape[0]
    rows = -(-n // 1024) * 8
    return jnp.pad(flat, (0, rows * 128 - n)).reshape(rows, 128)


def _unpack_small(packed, like):
    flat = packed.reshape(-1)
    out, off = {}, 0
    for n in _SMALL:
        sz = int(np.prod(like[n].shape))
        out[n] = flat[off:off + sz].reshape(like[n].shape)
        off += sz
    return out, flat[off:]


def kernel(x, mem, positions, norm_g, w_in, b_gate, q_norm_g, w_uq, kv_norm_g, w_ukv, q_head_g, k_head_g, conv_w, conv_b, mem_norm_g, w_mkv, mem_q_g, mem_k_g, w_br_attn, w_br_conv, w_br_mem, w_out, loss_target, m_norm_g, m_w_in, m_b_gate, m_q_norm_g, m_w_uq, m_kv_norm_g, m_w_ukv, m_q_head_g, m_k_head_g, m_conv_w, m_conv_b, m_mem_norm_g, m_w_mkv, m_mem_q_g, m_mem_k_g, m_w_br_attn, m_w_br_conv, m_w_br_mem, m_w_out, v_norm_g, v_w_in, v_b_gate, v_q_norm_g, v_w_uq, v_kv_norm_g, v_w_ukv, v_q_head_g, v_k_head_g, v_conv_w, v_conv_b, v_mem_norm_g, v_w_mkv, v_mem_q_g, v_mem_k_g, v_w_br_attn, v_w_br_conv, v_w_br_mem, v_w_out):
    w = dict(norm_g=norm_g, w_in=w_in, b_gate=b_gate, q_norm_g=q_norm_g, w_uq=w_uq, kv_norm_g=kv_norm_g,
             w_ukv=w_ukv, q_head_g=q_head_g, k_head_g=k_head_g, conv_w=conv_w, conv_b=conv_b,
             mem_norm_g=mem_norm_g, w_mkv=w_mkv, mem_q_g=mem_q_g, mem_k_g=mem_k_g, w_br_attn=w_br_attn,
             w_br_conv=w_br_conv, w_br_mem=w_br_mem, w_out=w_out)
    m = dict(norm_g=m_norm_g, w_in=m_w_in, b_gate=m_b_gate, q_norm_g=m_q_norm_g, w_uq=m_w_uq,
             kv_norm_g=m_kv_norm_g, w_ukv=m_w_ukv, q_head_g=m_q_head_g, k_head_g=m_k_head_g, conv_w=m_conv_w,
             conv_b=m_conv_b, mem_norm_g=m_mem_norm_g, w_mkv=m_w_mkv, mem_q_g=m_mem_q_g, mem_k_g=m_mem_k_g,
             w_br_attn=m_w_br_attn, w_br_conv=m_w_br_conv, w_br_mem=m_w_br_mem, w_out=m_w_out)
    v = dict(norm_g=v_norm_g, w_in=v_w_in, b_gate=v_b_gate, q_norm_g=v_q_norm_g, w_uq=v_w_uq,
             kv_norm_g=v_kv_norm_g, w_ukv=v_w_ukv, q_head_g=v_q_head_g, k_head_g=v_k_head_g, conv_w=v_conv_w,
             conv_b=v_conv_b, mem_norm_g=v_mem_norm_g, w_mkv=v_w_mkv, mem_q_g=v_mem_q_g, mem_k_g=v_mem_k_g,
             w_br_attn=v_w_br_attn, w_br_conv=v_w_br_conv, w_br_mem=v_w_br_mem, w_out=v_w_out)
    chip = 2 * lax.axis_index("x") + lax.axis_index("y")

    gathered = _gather_chips([w[n].astype(BF16) for n in _BIG] + [w["conv_w"]])
    full = {n: w[n] for n in _SMALL}
    for n, gth in zip(_BIG, gathered):
        full[n] = _cols_from_shards(gth) if n in _COL_SHARDED else _rows_from_shards(gth)
    full["conv_w"] = _cols_from_shards(gathered[-1])

    loss_part, grad_x, grads = _local_step(x[0], mem[0], positions[0], full, loss_target[0])

    to_owner = [(_cols_to_shards(grads[n]) if n in _COL_SHARDED else _rows_to_shards(grads[n])).astype(BF16)
                for n in _BIG]
    loss_vec = jnp.zeros((128,), F32).at[0].set(0.5 / D_MODEL * jnp.sum(loss_part))
    small = _pack_small(grads, loss_vec)
    scattered = _scatter_chips(to_owner, small)
    small_sum = _sum_slabs(scattered[-1], "sum_small")
    partial = []
    for n, slabs in zip(_BIG, scattered[:-1]):
        sh = slabs.shape
        partial.append(_sum_slabs(slabs.reshape(4, sh[1] * sh[2], sh[3]), "sum_" + n))
    other = _swap_cores(partial)

    small_g, tail = _unpack_small(small_sum, {n: (grads[n]) for n in _SMALL})
    loss = tail[0]
    small_g["conv_w"] = lax.dynamic_slice_in_dim(small_g["conv_w"], chip * (CONV_W // 4), CONV_W // 4, axis=2)

    outs_g, outs_d, outs_m, outs_v = {}, {}, {}, {}
    for n, p0, p1 in zip(_BIG, partial, other):
        shape = w[n].shape
        flat = lambda t: t.reshape(p0.shape)
        g_, d_, m_, v_ = _adamw(flat(w[n]), p0, p1, flat(m[n]), flat(v[n]), "adamw_" + n)
        outs_g[n], outs_d[n], outs_m[n], outs_v[n] = (t.reshape(shape) for t in (g_, d_, m_, v_))
    zero_small = jnp.zeros_like(small_sum)
    pk = lambda d: _pack_small(d, jnp.zeros((128,), F32))
    g_, d_, m_, v_ = _adamw(pk(w), _pack_small(small_g, jnp.zeros((128,), F32)), zero_small, pk(m), pk(v),
                            "adamw_small")
    like = {n: w[n] for n in _SMALL}
    for dst, packed in ((outs_g, g_), (outs_d, d_), (outs_m, m_), (outs_v, v_)):
        dst.update(_unpack_small(packed, like)[0])

    return (loss, grad_x[None], *[outs_g[n] for n in _ORDER], *[outs_d[n] for n in _ORDER],
            *[outs_m[n] for n in _ORDER], *[outs_v[n] for n in _ORDER])
```

```python
import functools

import numpy as np
import jax
import jax.numpy as jnp
from jax import lax
from jax.experimental import pallas as pl
from jax.experimental.pallas import tpu as pltpu

F32 = jnp.float32
BF16 = jnp.bfloat16

D_MODEL = 1024
DEPTH = 4
N_HEADS = 8
QK_DIM = 96
NOPE_DIM = 64
ROPE_DIM = 32
V_DIM = 64
Q_RANK = 384
KV_RANK = 256
CONV_W = 512
MEM_HEADS = 4
MEM_HD = 128
MEM_W = 512
IN_WIDTH = 7328
PW = 8192
HP = 128
PROJ_HALO = 16
F32_HALO = 8
EPS = 1e-6
ROPE_BASE = 10000.0
SCALE = QK_DIM ** -0.5
MEM_SCALE = MEM_HD ** -0.5

ADAM_LR = 0.001
ADAM_B1 = 0.9
ADAM_B2 = 0.999
ADAM_EPS = 1e-08
ADAM_WD = 0.01
ADAM_STEP = 10

VMEM_LIMIT_V7X = 56 * 1024 * 1024

O_R, O_GA, O_CB, O_QM, O_GC, O_GM, O_CC, O_CU, O_QL, O_KVL, O_KPE = (
    0, 3072, 4096, 4608, 5120, 5632, 6144, 6656, 7168, 7552, 7808)


def _params(sem, vmem=VMEM_LIMIT_V7X):
    return pltpu.CompilerParams(dimension_semantics=sem, vmem_limit_bytes=vmem)


def _sigmoid(t):
    return 0.5 * jnp.tanh(0.5 * t) + 0.5


def _silu_and_grad(g):
    sg = _sigmoid(g)
    return g * sg, sg * (1.0 + g * (1.0 - sg))


def _rms(t, g, n=None):
    n = t.shape[-1] if n is None else n
    r = lax.rsqrt(jnp.sum(t * t, axis=-1, keepdims=True) * (1.0 / n) + EPS)
    return (t * r) * g


def _rms_parts(t, n=None):
    n = t.shape[-1] if n is None else n
    r = lax.rsqrt(jnp.sum(t * t, axis=-1, keepdims=True) * (1.0 / n) + EPS)
    return r, t * r


def _rms_bwd(dhat, hat, r, n):
    return r * (dhat - hat * (jnp.sum(dhat * hat, axis=-1, keepdims=True) * (1.0 / n)))


def _rope(t, c, sa, sb):
    return t * c + pltpu.roll(t, HP - 16, 1) * sa + pltpu.roll(t, 16, 1) * sb


def _rope_t(d, c, sa, sb):
    return d * c + pltpu.roll(d * sa, 16, 1) + pltpu.roll(d * sb, HP - 16, 1)


def _dot(a, b):
    return jnp.dot(a, b, preferred_element_type=F32)


def _dot_nt(a, b):
    return lax.dot_general(a, b, (((1,), (1,)), ((), ())), preferred_element_type=F32)


def _dot_tn(a, b):
    return lax.dot_general(a, b, (((0,), (0,)), ((), ())), preferred_element_type=F32)


def _colsum(t):
    return jnp.sum(t, axis=0, keepdims=True)


def _tile(n, t):
    t = min(n, t)
    assert n % t == 0, (n, t)
    return t


def _rope_tables(pos_b, invf):
    S = pos_b.shape[0]
    tm = _tile(S, 1024)

    def body(pos_ref, invf_ref, c_ref, sa_ref, sb_ref):
        ang = pos_ref[...].astype(F32) * invf_ref[...]
        lane = lax.broadcasted_iota(jnp.int32, ang.shape, 1)
        cs = jnp.cos(ang)
        sn = jnp.sin(ang)
        c_ref[...] = jnp.where(lane < NOPE_DIM, 1.0, jnp.where(lane < QK_DIM, cs, 0.0))
        sa_ref[...] = jnp.where((lane >= NOPE_DIM) & (lane < NOPE_DIM + 16), -sn, 0.0)
        sb_ref[...] = jnp.where((lane >= NOPE_DIM + 16) & (lane < QK_DIM), sn, 0.0)

    blk = pl.BlockSpec((tm, HP), lambda i: (i, 0))
    return pl.pallas_call(
        body, name="rope_tables", grid=(S // tm,),
        in_specs=[blk, pl.BlockSpec((1, HP), lambda i: (0, 0))],
        out_specs=[blk, blk, blk],
        out_shape=[jax.ShapeDtypeStruct((S, HP), F32)] * 3,
        compiler_params=_params(("parallel",)),
    )(pos_b, invf)


def _inproj(x, g, w):
    S = x.shape[0]
    tm = _tile(S, 256)

    def body(x_ref, g_ref, w_ref, proj_ref, xn_ref):
        h = _rms(x_ref[...], g_ref[...]).astype(BF16)
        xn_ref[...] = h
        proj_ref[...] = _dot(h, w_ref[...]).astype(BF16)

    row = lambda n: pl.BlockSpec((tm, n), lambda i: (i, 0))
    return pl.pallas_call(
        body, name="inproj", grid=(S // tm,),
        in_specs=[row(D_MODEL), pl.BlockSpec((1, D_MODEL), lambda i: (0, 0)),
                  pl.BlockSpec((D_MODEL, PW), lambda i: (0, 0))],
        out_specs=[row(PW), row(D_MODEL)],
        out_shape=[jax.ShapeDtypeStruct((S, PW), BF16), jax.ShapeDtypeStruct((S, D_MODEL), BF16)],
        compiler_params=_params(("parallel",)),
    )(x, g, w)


def _mla_prep(proj, tabs, qg, kvg, qhg, khg, wuq, wuk, wuv):
    S = proj.shape[0]
    tm = _tile(S, 512)

    def body(a_ref, c_ref, sa_ref, sb_ref, qg_ref, kvg_ref, qhg_ref, khg_ref, wuq_ref, wuk_ref, wuv_ref,
             q_ref, k_ref, v_ref, qn_ref, kvn_ref):
        ql = a_ref[:, 0:Q_RANK].astype(F32)
        kvl = a_ref[:, Q_RANK:Q_RANK + KV_RANK].astype(F32)
        kpe = a_ref[:, Q_RANK + KV_RANK:Q_RANK + KV_RANK + HP].astype(F32)
        qn = _rms(ql, qg_ref[...]).astype(BF16)
        kvn = _rms(kvl, kvg_ref[...]).astype(BF16)
        qn_ref[...] = qn
        kvn_ref[...] = kvn
        qraw = _dot(qn, wuq_ref[...])
        kn = _dot(kvn, wuk_ref[...])
        v_ref[...] = _dot(kvn, wuv_ref[...]).astype(BF16)
        c, sa, sb = c_ref[...], sa_ref[...], sb_ref[...]
        for h in range(N_HEADS):
            sl = slice(h * HP, (h + 1) * HP)
            tq = _rms(qraw[:, sl], qhg_ref[...], QK_DIM)
            q_ref[:, sl] = (_rope(tq, c, sa, sb) * SCALE).astype(BF16)
            tk = _rms(kn[:, sl] + kpe, khg_ref[...], QK_DIM)
            k_ref[:, sl] = _rope(tk, c, sa, sb).astype(BF16)

    row = lambda w: pl.BlockSpec((tm, w), lambda i: (i, 0))
    full = lambda a: pl.BlockSpec(a.shape, lambda i: (0,) * a.ndim)
    return pl.pallas_call(
        body, name="mla_prep", grid=(S // tm,),
        in_specs=[pl.BlockSpec((tm, 1024), lambda i: (i, 7)), row(HP), row(HP), row(HP),
                  full(qg), full(kvg), full(qhg), full(khg), full(wuq), full(wuk), full(wuv)],
        out_specs=[row(1024), row(1024), row(1024), row(Q_RANK), row(KV_RANK)],
        out_shape=[jax.ShapeDtypeStruct((S, 1024), BF16)] * 3
        + [jax.ShapeDtypeStruct((S, Q_RANK), BF16), jax.ShapeDtypeStruct((S, KV_RANK), BF16)],
        compiler_params=_params(("parallel",)),
    )(proj, *tabs, qg, kvg, qhg, khg, wuq, wuk, wuv)


def _attn_fwd(q, k, v):
    S = q.shape[0]
    tq, tk = _tile(S, 1024), _tile(S, 2048)
    nk, nb = S // tk, tk // HP

    def body(q_ref, k_ref, v_ref, o_ref, lse_ref, m_s, l_s, acc_s):
        m_s[...] = jnp.full(m_s.shape, -jnp.inf, F32)
        l_s[...] = jnp.zeros(l_s.shape, F32)
        acc_s[...] = jnp.zeros(acc_s.shape, F32)
        qv = q_ref[...]

        def step(c, carry):
            rows = pl.ds(pl.multiple_of(c * tk, tk), tk)
            s = _dot_nt(qv, k_ref[rows, :])
            cm = s[:, 0:HP]
            for j in range(1, nb):
                cm = jnp.maximum(cm, s[:, j * HP:(j + 1) * HP])
            m_prev = m_s[...]
            m_new = jnp.maximum(m_prev, jnp.max(cm, axis=-1, keepdims=True))
            alpha = jnp.exp(m_prev - m_new)
            lsum = alpha * l_s[...]
            ps = []
            for j in range(nb):
                pj = jnp.exp(s[:, j * HP:(j + 1) * HP] - m_new)
                lsum = lsum + pj
                ps.append(pj.astype(BF16))
            l_s[...] = lsum
            acc_s[...] = alpha * acc_s[...] + _dot(jnp.concatenate(ps, axis=1), v_ref[rows, :])
            m_s[...] = m_new
            return carry

        lax.fori_loop(0, nk, step, 0)
        l = jnp.sum(l_s[...], axis=-1, keepdims=True)
        o_ref[...] = acc_s[...] / l
        lse_ref[0] = m_s[:, 0:1] + jnp.log(l)

    return pl.pallas_call(
        body, name="attn_fwd", grid=(N_HEADS, S // tq),
        in_specs=[pl.BlockSpec((tq, HP), lambda h, i: (i, h)),
                  pl.BlockSpec((S, HP), lambda h, i: (0, h)),
                  pl.BlockSpec((S, HP), lambda h, i: (0, h))],
        out_specs=[pl.BlockSpec((tq, HP), lambda h, i: (i, h)),
                   pl.BlockSpec((1, tq, 1), lambda h, i: (h, i, 0))],
        out_shape=[jax.ShapeDtypeStruct((S, N_HEADS * HP), F32),
                   jax.ShapeDtypeStruct((N_HEADS, S, 1), F32)],
        scratch_shapes=[pltpu.VMEM((tq, HP), F32), pltpu.VMEM((tq, HP), F32), pltpu.VMEM((tq, HP), F32)],
        compiler_params=_params(("parallel", "parallel")),
    )(q, k, v)


def _memkv(mem, mng, mkg, wmkv):
    M = mem.shape[0]

    def body(mem_ref, mng_ref, mkg_ref, w_ref, mk_ref, mv_ref):
        mn = _rms(mem_ref[...], mng_ref[...]).astype(BF16)
        mkv = _dot(mn, w_ref[...])
        for h in range(MEM_HEADS):
            kraw = mkv[:, 2 * MEM_HD * h:2 * MEM_HD * h + MEM_HD]
            mk_ref[:, MEM_HD * h:MEM_HD * (h + 1)] = _rms(kraw, mkg_ref[...]).astype(BF16)
            mv_ref[:, MEM_HD * h:MEM_HD * (h + 1)] = mkv[:, 2 * MEM_HD * h + MEM_HD:2 * MEM_HD * (h + 1)].astype(BF16)

    return pl.pallas_call(
        body, name="memkv",
        out_shape=[jax.ShapeDtypeStruct((M, MEM_W), BF16)] * 2,
        compiler_params=pltpu.CompilerParams(vmem_limit_bytes=VMEM_LIMIT_V7X),
    )(mem, mng, mkg, wmkv)


def _conv_shifts(cc, cu, hp_ref, hn_ref, i, n_tiles, tm):
    z = cc * cu
    last = PROJ_HALO - 1
    zp = hp_ref[last:last + 1, 0:CONV_W].astype(F32) * hp_ref[last:last + 1, CONV_W:2 * CONV_W].astype(F32)
    zn = hn_ref[0:1, 0:CONV_W].astype(F32) * hn_ref[0:1, CONV_W:2 * CONV_W].astype(F32)
    zp = jnp.where(i == 0, 0.0, zp)
    zn = jnp.where(i == n_tiles - 1, 0.0, zn)
    row = lax.broadcasted_iota(jnp.int32, z.shape, 0)
    z_up = jnp.where(row == 0, zp, pltpu.roll(z, 1, 0))
    z_dn = jnp.where(row == tm - 1, zn, pltpu.roll(z, tm - 1, 0))
    return z, z_up, z_dn


def _halo_specs(tm, S, width, col, rows):
    per = tm // rows
    prev = pl.BlockSpec((rows, width), lambda i: (jnp.maximum(i * per - 1, 0), col))
    nxt = pl.BlockSpec((rows, width), lambda i: (jnp.minimum((i + 1) * per, S // rows - 1), col))
    return prev, nxt


def _mem_attend(qm, mqg, mk_h, mv_h):
    r, qhat = _rms_parts(qm)
    mq = (qhat * mqg).astype(BF16)
    s = _dot_nt(mq, mk_h) * MEM_SCALE
    e = jnp.exp(s - jnp.max(s, axis=-1, keepdims=True))
    p = e / jnp.sum(e, axis=-1, keepdims=True)
    pv = _dot(p.astype(BF16), mv_h)
    return r, qhat, mq, p, pv


def _merge(proj, o, x, bg, convw, mqg, mk, mv, wba, wbc, wbm, wo):
    S = x.shape[0]
    tm = _tile(S, 512)
    nt = S // tm

    def body(main_ref, ccu_ref, hp_ref, hn_ref, o_ref, x_ref, bg_ref, cw_ref, mqg_ref, mk_ref, mv_ref,
             wba_ref, wbc_ref, wbm_ref, wo_ref, xn_ref, oa_ref, oc_ref, om_ref, u_ref, y_ref):
        i = pl.program_id(0)
        sil_a, _ = _silu_and_grad(main_ref[:, O_GA:O_GA + 1024].astype(F32))
        oa = (o_ref[...] * sil_a).astype(BF16)
        oa_ref[...] = oa
        z, z_up, z_dn = _conv_shifts(ccu_ref[:, 0:CONV_W].astype(F32), ccu_ref[:, CONV_W:].astype(F32), hp_ref, hn_ref, i, nt, tm)
        cv = cw_ref[0:1, :] * z_up + cw_ref[1:2, :] * z + cw_ref[2:3, :] * z_dn + cw_ref[3:4, :]
        sil_c, _ = _silu_and_grad(main_ref[:, O_GC:O_GC + CONV_W].astype(F32))
        oc = (main_ref[:, O_CB:O_CB + CONV_W].astype(F32) * cv * sil_c).astype(BF16)
        oc_ref[...] = oc
        sil_m, _ = _silu_and_grad(main_ref[:, O_GM:O_GM + MEM_W].astype(F32))
        for h in range(MEM_HEADS):
            sl = slice(h * MEM_HD, (h + 1) * MEM_HD)
            qm = main_ref[:, O_QM + h * MEM_HD:O_QM + (h + 1) * MEM_HD].astype(F32)
            pv = _mem_attend(qm, mqg_ref[...], mk_ref[:, sl], mv_ref[:, sl])[4]
            om_ref[:, sl] = (pv * sil_m[:, sl]).astype(BF16)
        ua = _dot(oa, wba_ref[...])
        uc = _dot(oc, wbc_ref[...])
        um = _dot(om_ref[...], wbm_ref[...])
        u_ref[:, 0:1024] = ua.astype(BF16)
        u_ref[:, 1024:2048] = uc.astype(BF16)
        u_ref[:, 2048:3072] = um.astype(BF16)
        rg = _sigmoid(main_ref[:, O_R:O_R + 3072].astype(F32) + bg_ref[...])
        y = (rg[:, 0:1024] * ua + rg[:, 1024:2048] * uc + rg[:, 2048:3072] * um).astype(BF16)
        y_ref[...] = y
        xn_ref[...] = x_ref[...] + _dot(y, wo_ref[...])

    row = lambda w: pl.BlockSpec((tm, w), lambda i: (i, 0))
    full = lambda a: pl.BlockSpec(a.shape, lambda i: (0,) * a.ndim)
    hp, hn = _halo_specs(tm, S, 1024, 6, PROJ_HALO)
    return pl.pallas_call(
        body, name="merge", grid=(nt,),
        in_specs=[row(6144), pl.BlockSpec((tm, 1024), lambda i: (i, 6)), hp, hn, row(1024), row(1024),
                  full(bg), full(convw), full(mqg), full(mk), full(mv), full(wba), full(wbc), full(wbm), full(wo)],
        out_specs=[row(1024), row(1024), row(CONV_W), row(MEM_W), row(3072), row(1024)],
        out_shape=[jax.ShapeDtypeStruct((S, 1024), F32), jax.ShapeDtypeStruct((S, 1024), BF16),
                   jax.ShapeDtypeStruct((S, CONV_W), BF16), jax.ShapeDtypeStruct((S, MEM_W), BF16),
                   jax.ShapeDtypeStruct((S, 3072), BF16), jax.ShapeDtypeStruct((S, 1024), BF16)],
        compiler_params=_params(("parallel",)),
    )(proj, proj, proj, proj, o, x, bg, convw, mqg, mk, mv, wba, wbc, wbm, wo)


def _loss_head(xf, tgt):
    S = xf.shape[0]
    tm = _tile(S, 1024)

    def body(x_ref, t_ref, g_ref, acc_ref):
        @pl.when(pl.program_id(0) == 0)
        def _():
            acc_ref[...] = jnp.zeros(acc_ref.shape, F32)

        e = x_ref[...] - t_ref[...]
        g_ref[...] = e * (1.0 / D_MODEL)
        part = jnp.sum((e * e).reshape(tm // 8, 8, D_MODEL), axis=0)
        tot = part[:, 0:128]
        for k in range(1, D_MODEL // 128):
            tot = tot + part[:, 128 * k:128 * (k + 1)]
        acc_ref[...] += tot

    row = pl.BlockSpec((tm, D_MODEL), lambda i: (i, 0))
    return pl.pallas_call(
        body, name="loss_head", grid=(S // tm,),
        in_specs=[row, row],
        out_specs=[row, pl.BlockSpec((8, 128), lambda i: (0, 0))],
        out_shape=[jax.ShapeDtypeStruct((S, D_MODEL), F32), jax.ShapeDtypeStruct((8, 128), F32)],
        compiler_params=_params(("arbitrary",)),
    )(xf, tgt)


def _merge_bwd(g, proj, o, u, bg, convw, mqg, mk, mv, wot, wbat, wbct, wbmt):
    S = g.shape[0]
    tm = _tile(S, 256)
    nt = S // tm
    M = mk.shape[0]

    def body(g_ref, main_ref, ccu_ref, hp_ref, hn_ref, o_ref, u_ref, bg_ref, cw_ref, mqg_ref, mk_ref, mv_ref,
             wot_ref, wbat_ref, wbct_ref, wbmt_ref,
             dmain_ref, dcv_ref, do_ref, delta_ref, du_ref, dbg_ref, dmk_ref, dmv_ref, dmqg_ref):
        i = pl.program_id(0)

        @pl.when(i == 0)
        def _():
            dbg_ref[...] = jnp.zeros(dbg_ref.shape, F32)
            dmk_ref[...] = jnp.zeros(dmk_ref.shape, F32)
            dmv_ref[...] = jnp.zeros(dmv_ref.shape, F32)
            dmqg_ref[...] = jnp.zeros(dmqg_ref.shape, F32)

        dy = _dot(g_ref[...].astype(BF16), wot_ref[...])
        rg = _sigmoid(main_ref[:, O_R:O_R + 3072].astype(F32) + bg_ref[...])
        dyt = jnp.concatenate([dy, dy, dy], axis=1)
        dr = dyt * u_ref[...].astype(F32) * rg * (1.0 - rg)
        dmain_ref[:, O_R:O_R + 3072] = dr.astype(BF16)
        dbg_ref[...] += _colsum(dr)
        du = (dyt * rg).astype(BF16)
        du_ref[...] = du
        do_a = _dot(du[:, 0:1024], wbat_ref[...])
        do_c = _dot(du[:, 1024:2048], wbct_ref[...])
        do_m = _dot(du[:, 2048:3072], wbmt_ref[...])

        sil_a, dsil_a = _silu_and_grad(main_ref[:, O_GA:O_GA + 1024].astype(F32))
        ov = o_ref[...]
        d_o = do_a * sil_a
        do_ref[...] = d_o.astype(BF16)
        dmain_ref[:, O_GA:O_GA + 1024] = (do_a * ov * dsil_a).astype(BF16)
        prod = d_o * ov
        for h in range(N_HEADS):
            delta_ref[h] = jnp.sum(prod[:, h * HP:(h + 1) * HP], axis=-1, keepdims=True)

        z, z_up, z_dn = _conv_shifts(ccu_ref[:, 0:CONV_W].astype(F32), ccu_ref[:, CONV_W:].astype(F32), hp_ref, hn_ref, i, nt, tm)
        cv = cw_ref[0:1, :] * z_up + cw_ref[1:2, :] * z + cw_ref[2:3, :] * z_dn + cw_ref[3:4, :]
        sil_c, dsil_c = _silu_and_grad(main_ref[:, O_GC:O_GC + CONV_W].astype(F32))
        cb = main_ref[:, O_CB:O_CB + CONV_W].astype(F32)
        dmain_ref[:, O_CB:O_CB + CONV_W] = (do_c * cv * sil_c).astype(BF16)
        dmain_ref[:, O_GC:O_GC + CONV_W] = (do_c * cb * cv * dsil_c).astype(BF16)
        dcv_ref[...] = do_c * cb * sil_c

        sil_m, dsil_m = _silu_and_grad(main_ref[:, O_GM:O_GM + MEM_W].astype(F32))
        for h in range(MEM_HEADS):
            sl = slice(h * MEM_HD, (h + 1) * MEM_HD)
            qm = main_ref[:, O_QM + h * MEM_HD:O_QM + (h + 1) * MEM_HD].astype(F32)
            mk_h, mv_h = mk_ref[:, sl], mv_ref[:, sl]
            r, qhat, mq, p, pv = _mem_attend(qm, mqg_ref[...], mk_h, mv_h)
            dom = do_m[:, sl]
            dmain_ref[:, O_GM + h * MEM_HD:O_GM + (h + 1) * MEM_HD] = (dom * pv * dsil_m[:, sl]).astype(BF16)
            dpv = (dom * sil_m[:, sl]).astype(BF16)
            dp = _dot_nt(dpv, mv_h)
            ds = (p * (dp - jnp.sum(dp * p, axis=-1, keepdims=True)) * MEM_SCALE).astype(BF16)
            dmq = _dot(ds, mk_h)
            dmk_ref[:, sl] += _dot_tn(ds, mq)
            dmv_ref[:, sl] += _dot_tn(p.astype(BF16), dpv)
            dmqg_ref[...] += _colsum(dmq * qhat)
            dqm = _rms_bwd(dmq * mqg_ref[...], qhat, r, MEM_HD)
            dmain_ref[:, O_QM + h * MEM_HD:O_QM + (h + 1) * MEM_HD] = dqm.astype(BF16)

    row = lambda w: pl.BlockSpec((tm, w), lambda i: (i, 0))
    full = lambda a: pl.BlockSpec(a.shape, lambda i: (0,) * a.ndim)
    acc = lambda r, c: pl.BlockSpec((r, c), lambda i: (0, 0))
    hp, hn = _halo_specs(tm, S, 1024, 6, PROJ_HALO)
    return pl.pallas_call(
        body, name="merge_bwd", grid=(nt,),
        in_specs=[row(1024), row(6144), pl.BlockSpec((tm, 1024), lambda i: (i, 6)), hp, hn, row(1024), row(3072),
                  full(bg), full(convw), full(mqg), full(mk), full(mv), full(wot), full(wbat), full(wbct), full(wbmt)],
        out_specs=[row(6144), row(CONV_W), row(1024), pl.BlockSpec((N_HEADS, tm, 1), lambda i: (0, i, 0)), row(3072),
                   acc(1, 3072), acc(M, MEM_W), acc(M, MEM_W), acc(1, MEM_HD)],
        out_shape=[jax.ShapeDtypeStruct((S, 6144), BF16), jax.ShapeDtypeStruct((S, CONV_W), F32),
                   jax.ShapeDtypeStruct((S, 1024), BF16), jax.ShapeDtypeStruct((N_HEADS, S, 1), F32),
                   jax.ShapeDtypeStruct((S, 3072), BF16), jax.ShapeDtypeStruct((1, 3072), F32),
                   jax.ShapeDtypeStruct((M, MEM_W), F32), jax.ShapeDtypeStruct((M, MEM_W), F32),
                   jax.ShapeDtypeStruct((1, MEM_HD), F32)],
        compiler_params=_params(("arbitrary",)),
    )(g, proj, proj, proj, proj, o, u, bg, convw, mqg, mk, mv, wot, wbat, wbct, wbmt)


def _conv_bwd(dcv, proj, convw):
    S = dcv.shape[0]
    tm = _tile(S, 512)
    nt = S // tm

    def body(d_ref, dp_ref, dn_ref, ccu_ref, hp_ref, hn_ref, cw_ref, dccu_ref, dcw_ref):
        i = pl.program_id(0)

        @pl.when(i == 0)
        def _():
            dcw_ref[...] = jnp.zeros(dcw_ref.shape, F32)

        cc, cu = ccu_ref[:, 0:CONV_W].astype(F32), ccu_ref[:, CONV_W:].astype(F32)
        z, z_up, z_dn = _conv_shifts(cc, cu, hp_ref, hn_ref, i, nt, tm)
        d = d_ref[...]
        dprev = jnp.where(i == 0, 0.0, dp_ref[7:8, :])
        dnext = jnp.where(i == nt - 1, 0.0, dn_ref[0:1, :])
        row = lax.broadcasted_iota(jnp.int32, d.shape, 0)
        d_up = jnp.where(row == 0, dprev, pltpu.roll(d, 1, 0))
        d_dn = jnp.where(row == tm - 1, dnext, pltpu.roll(d, tm - 1, 0))
        dz = cw_ref[0:1, :] * d_dn + cw_ref[1:2, :] * d + cw_ref[2:3, :] * d_up
        dccu_ref[:, 0:CONV_W] = (dz * cu).astype(BF16)
        dccu_ref[:, CONV_W:] = (dz * cc).astype(BF16)
        dcw_ref[0:1, :] += _colsum(d * z_up)
        dcw_ref[1:2, :] += _colsum(d * z)
        dcw_ref[2:3, :] += _colsum(d * z_dn)
        dcw_ref[3:4, :] += _colsum(d)

    hp, hn = _halo_specs(tm, S, 1024, 6, PROJ_HALO)
    dp, dn = _halo_specs(tm, S, CONV_W, 0, F32_HALO)
    return pl.pallas_call(
        body, name="conv_bwd", grid=(nt,),
        in_specs=[pl.BlockSpec((tm, CONV_W), lambda i: (i, 0)), dp, dn,
                  pl.BlockSpec((tm, 1024), lambda i: (i, 6)), hp, hn,
                  pl.BlockSpec((8, CONV_W), lambda i: (0, 0))],
        out_specs=[pl.BlockSpec((tm, 1024), lambda i: (i, 0)), pl.BlockSpec((8, CONV_W), lambda i: (0, 0))],
        out_shape=[jax.ShapeDtypeStruct((S, 1024), BF16), jax.ShapeDtypeStruct((8, CONV_W), F32)],
        compiler_params=_params(("arbitrary",)),
    )(dcv, dcv, dcv, proj, proj, proj, convw)


def _attn_bwd(q, k, v, do, qt, dot, lse, delta):
    S = q.shape[0]
    tq, tk = _tile(S, 512), _tile(S, 2048)
    ni, nk = S // tq, S // tk

    def body(q_ref, do_ref, qt_ref, dot_ref, k_ref, v_ref, lse_ref, delta_ref, dq_ref, dkt_hbm, dvt_hbm,
             dq_s, dkt_s, dvt_s, sem):
        h, i = pl.program_id(0), pl.program_id(1)

        @pl.when(i == 0)
        def _():
            dkt_s[...] = jnp.zeros(dkt_s.shape, F32)
            dvt_s[...] = jnp.zeros(dvt_s.shape, F32)

        dq_s[...] = jnp.zeros(dq_s.shape, F32)
        qv, dov, qtv, dotv = q_ref[...], do_ref[...], qt_ref[...], dot_ref[...]
        lse_c, delta_c = lse_ref[0], delta_ref[0]

        def step(c, carry):
            cols = pl.ds(pl.multiple_of(c * tk, tk), tk)
            kc, vc = k_ref[cols, :], v_ref[cols, :]
            p = jnp.exp(_dot_nt(qv, kc) - lse_c)
            dp = _dot_nt(dov, vc)
            ds = (p * (dp - delta_c)).astype(BF16)
            dq_s[...] += _dot(ds, kc)
            dvt_s[:, cols] += _dot(dotv, p.astype(BF16))
            dkt_s[:, cols] += _dot(qtv, ds)
            return carry

        lax.fori_loop(0, nk, step, 0)
        dq_ref[...] = dq_s[...]

        @pl.when(i == ni - 1)
        def _():
            head = pl.ds(pl.multiple_of(h * HP, HP), HP)
            out_k = pltpu.make_async_copy(dkt_s, dkt_hbm.at[head, :], sem.at[0])
            out_v = pltpu.make_async_copy(dvt_s, dvt_hbm.at[head, :], sem.at[1])
            out_k.start()
            out_v.start()
            out_k.wait()
            out_v.wait()

    col = pl.BlockSpec((1, tq, 1), lambda h, i: (h, i, 0))
    blk = pl.BlockSpec((tq, HP), lambda h, i: (i, h))
    blkt = pl.BlockSpec((HP, tq), lambda h, i: (h, i))
    res = pl.BlockSpec((S, HP), lambda h, i: (0, h))
    whole = pl.BlockSpec(memory_space=pl.ANY)
    return pl.pallas_call(
        body, name="attn_bwd", grid=(N_HEADS, ni),
        in_specs=[blk, blk, blkt, blkt, res, res, col, col],
        out_specs=[blk, whole, whole],
        out_shape=[jax.ShapeDtypeStruct((S, N_HEADS * HP), F32), jax.ShapeDtypeStruct((N_HEADS * HP, S), F32),
                   jax.ShapeDtypeStruct((N_HEADS * HP, S), F32)],
        scratch_shapes=[pltpu.VMEM((tq, HP), F32), pltpu.VMEM((HP, S), F32), pltpu.VMEM((HP, S), F32),
                        pltpu.SemaphoreType.DMA((2,))],
        compiler_params=_params(("parallel", "arbitrary")),
    )(q, do, qt, dot, k, v, lse, delta)


def _mla_prep_bwd(proj, tabs, dq, dk, dv, qg, kvg, qhg, khg, wuq, wuk, wuqt, wukt, wuvt):
    S = proj.shape[0]
    tm = _tile(S, 512)

    def body(a_ref, c_ref, sa_ref, sb_ref, dq_ref, dk_ref, dv_ref, qg_ref, kvg_ref, qhg_ref, khg_ref,
             wuq_ref, wuk_ref, wuqt_ref, wukt_ref, wuvt_ref,
             da_ref, dqraw_ref, dkraw_ref, dqg_ref, dkvg_ref, dqhg_ref, dkhg_ref):
        @pl.when(pl.program_id(0) == 0)
        def _():
            dqg_ref[...] = jnp.zeros(dqg_ref.shape, F32)
            dkvg_ref[...] = jnp.zeros(dkvg_ref.shape, F32)
            dqhg_ref[...] = jnp.zeros(dqhg_ref.shape, F32)
            dkhg_ref[...] = jnp.zeros(dkhg_ref.shape, F32)

        ql = a_ref[:, 0:Q_RANK].astype(F32)
        kvl = a_ref[:, Q_RANK:Q_RANK + KV_RANK].astype(F32)
        kpe = a_ref[:, Q_RANK + KV_RANK:Q_RANK + KV_RANK + HP].astype(F32)
        rq, qhat = _rms_parts(ql)
        rkv, kvhat = _rms_parts(kvl)
        qraw = _dot((qhat * qg_ref[...]).astype(BF16), wuq_ref[...])
        kn = _dot((kvhat * kvg_ref[...]).astype(BF16), wuk_ref[...])
        c, sa, sb = c_ref[...], sa_ref[...], sb_ref[...]
        dkpe = jnp.zeros(kpe.shape, F32)
        dqhg = jnp.zeros((1, HP), F32)
        dkhg = jnp.zeros((1, HP), F32)
        for h in range(N_HEADS):
            sl = slice(h * HP, (h + 1) * HP)
            r, that = _rms_parts(qraw[:, sl], QK_DIM)
            dtn = _rope_t(dq_ref[:, sl], c, sa, sb) * SCALE
            dqhg = dqhg + _colsum(dtn * that)
            dqraw_ref[:, sl] = _rms_bwd(dtn * qhg_ref[...], that, r, QK_DIM).astype(BF16)
            r, that = _rms_parts(kn[:, sl] + kpe, QK_DIM)
            dtn = _rope_t(dk_ref[:, sl], c, sa, sb)
            dkhg = dkhg + _colsum(dtn * that)
            dkr = _rms_bwd(dtn * khg_ref[...], that, r, QK_DIM)
            dkraw_ref[:, sl] = dkr.astype(BF16)
            dkpe = dkpe + dkr
        dqhg_ref[...] += dqhg
        dkhg_ref[...] += dkhg
        dqn = _dot(dqraw_ref[...], wuqt_ref[...])
        dqg_ref[...] += _colsum(dqn * qhat)
        da_ref[:, 0:Q_RANK] = _rms_bwd(dqn * qg_ref[...], qhat, rq, Q_RANK).astype(BF16)
        dkvn = _dot(dkraw_ref[...], wukt_ref[...]) + _dot(dv_ref[...], wuvt_ref[...])
        dkvg_ref[...] += _colsum(dkvn * kvhat)
        da_ref[:, Q_RANK:Q_RANK + KV_RANK] = _rms_bwd(dkvn * kvg_ref[...], kvhat, rkv, KV_RANK).astype(BF16)
        da_ref[:, Q_RANK + KV_RANK:Q_RANK + KV_RANK + HP] = dkpe.astype(BF16)
        da_ref[:, Q_RANK + KV_RANK + HP:] = jnp.zeros((tm, 1024 - Q_RANK - KV_RANK - HP), BF16)

    row = lambda w: pl.BlockSpec((tm, w), lambda i: (i, 0))
    full = lambda a: pl.BlockSpec(a.shape, lambda i: (0,) * a.ndim)
    acc = lambda c: pl.BlockSpec((1, c), lambda i: (0, 0))
    return pl.pallas_call(
        body, name="mla_prep_bwd", grid=(S // tm,),
        in_specs=[pl.BlockSpec((tm, 1024), lambda i: (i, 7)), row(HP), row(HP), row(HP),
                  row(1024), row(1024), row(1024), full(qg), full(kvg), full(qhg), full(khg),
                  full(wuq), full(wuk), full(wuqt), full(wukt), full(wuvt)],
        out_specs=[row(1024), row(1024), row(1024), acc(Q_RANK), acc(KV_RANK), acc(HP), acc(HP)],
        out_shape=[jax.ShapeDtypeStruct((S, 1024), BF16)] * 3
        + [jax.ShapeDtypeStruct((1, Q_RANK), F32), jax.ShapeDtypeStruct((1, KV_RANK), F32),
           jax.ShapeDtypeStruct((1, HP), F32), jax.ShapeDtypeStruct((1, HP), F32)],
        compiler_params=_params(("arbitrary",)),
    )(proj, *tabs, dq, dk, dv, qg, kvg, qhg, khg, wuq, wuk, wuqt, wukt, wuvt)


def _inproj_bwd(dmain, dccu, dsega, wint, x, g, ng):
    S = x.shape[0]
    tm = _tile(S, 256)
    nm, nc = dmain.shape[1], dccu.shape[1]
    assert nm + nc + dsega.shape[1] == PW

    def body(dm_ref, dc_ref, da_ref, w_ref, x_ref, g_ref, ng_ref, dx_ref, dng_ref):
        @pl.when(pl.program_id(0) == 0)
        def _():
            dng_ref[...] = jnp.zeros(dng_ref.shape, F32)

        dh = (_dot(dm_ref[...], w_ref[0:nm, :]) + _dot(dc_ref[...], w_ref[nm:nm + nc, :])
              + _dot(da_ref[...], w_ref[nm + nc:PW, :]))
        r, xhat = _rms_parts(x_ref[...])
        dng_ref[...] += _colsum(dh * xhat)
        dx_ref[...] = g_ref[...] + _rms_bwd(dh * ng_ref[...], xhat, r, D_MODEL)

    row = lambda w: pl.BlockSpec((tm, w), lambda i: (i, 0))
    return pl.pallas_call(
        body, name="inproj_bwd", grid=(S // tm,),
        in_specs=[row(nm), row(nc), row(dsega.shape[1]), pl.BlockSpec((PW, D_MODEL), lambda i: (0, 0)),
                  row(D_MODEL), row(D_MODEL), pl.BlockSpec((1, D_MODEL), lambda i: (0, 0))],
        out_specs=[row(D_MODEL), pl.BlockSpec((1, D_MODEL), lambda i: (0, 0))],
        out_shape=[jax.ShapeDtypeStruct((S, D_MODEL), F32), jax.ShapeDtypeStruct((1, D_MODEL), F32)],
        compiler_params=_params(("arbitrary",)),
    )(dmain, dccu, dsega, wint, x, g, ng)


def _memkv_bwd(mem, mng, mkg, wmkv, wmkvt, dmk, dmv):
    M = mem.shape[0]

    def body(mem_ref, mng_ref, mkg_ref, w_ref, wt_ref, dmk_ref, dmv_ref, dw_ref, dmng_ref, dmkg_ref, d_s):
        r, mhat = _rms_parts(mem_ref[...])
        mn = (mhat * mng_ref[...]).astype(BF16)
        mkv = _dot(mn, w_ref[...])
        dmkg = jnp.zeros((1, MEM_HD), F32)
        for h in range(MEM_HEADS):
            sl = slice(h * MEM_HD, (h + 1) * MEM_HD)
            rk, khat = _rms_parts(mkv[:, 2 * MEM_HD * h:2 * MEM_HD * h + MEM_HD])
            dkn = dmk_ref[:, sl]
            dmkg = dmkg + _colsum(dkn * khat)
            d_s[:, 2 * MEM_HD * h:2 * MEM_HD * h + MEM_HD] = _rms_bwd(dkn * mkg_ref[...], khat, rk, MEM_HD).astype(BF16)
            d_s[:, 2 * MEM_HD * h + MEM_HD:2 * MEM_HD * (h + 1)] = dmv_ref[:, sl].astype(BF16)
        dmkg_ref[...] = dmkg
        dw_ref[...] = _dot_tn(mn, d_s[...])
        dmn = _dot(d_s[...], wt_ref[...])
        dmng_ref[...] = _colsum(dmn * mhat)

    return pl.pallas_call(
        body, name="memkv_bwd",
        out_shape=[jax.ShapeDtypeStruct((D_MODEL, 2 * MEM_W), F32), jax.ShapeDtypeStruct((1, D_MODEL), F32),
                   jax.ShapeDtypeStruct((1, MEM_HD), F32)],
        scratch_shapes=[pltpu.VMEM((M, 2 * MEM_W), BF16)],
        compiler_params=pltpu.CompilerParams(vmem_limit_bytes=VMEM_LIMIT_V7X),
    )(mem, mng, mkg, wmkv, wmkvt, dmk, dmv)


def _mm_tn(a, b, name, col0=0, ncols=None):
    S, M = a.shape
    N = b.shape[1] if ncols is None else ncols
    tm, tn, ts = _tile(M, 1024), _tile(N, 1024), _tile(S, 2048)
    assert col0 % tn == 0
    jb = col0 // tn

    def body(a_ref, b_ref, o_ref):
        @pl.when(pl.program_id(2) == 0)
        def _():
            o_ref[...] = jnp.zeros(o_ref.shape, F32)

        o_ref[...] += _dot_tn(a_ref[...].astype(BF16), b_ref[...].astype(BF16))

    return pl.pallas_call(
        body, name=name, grid=(M // tm, N // tn, S // ts),
        in_specs=[pl.BlockSpec((ts, tm), lambda i, j, k: (k, i)),
                  pl.BlockSpec((ts, tn), lambda i, j, k: (k, j + jb))],
        out_specs=pl.BlockSpec((tm, tn), lambda i, j, k: (i, j)),
        out_shape=jax.ShapeDtypeStruct((M, N), F32),
        compiler_params=_params(("parallel", "parallel", "arbitrary")),
    )(a, b)


def _adamw(w, g0, g1, m, v, name):
    R, C = w.shape
    tr = R
    for cand in (512, 256, 128, 64, 32, 16, 8):
        if R % cand == 0 and cand * C * 4 <= (1 << 20):
            tr = cand
            break
    c1 = 1.0 / (1.0 - ADAM_B1 ** ADAM_STEP)
    c2 = 1.0 / (1.0 - ADAM_B2 ** ADAM_STEP)

    def body(w_ref, g0_ref, g1_ref, m_ref, v_ref, g_ref, d_ref, nm_ref, nv_ref):
        g = g0_ref[...] + g1_ref[...]
        nm = ADAM_B1 * m_ref[...] + (1.0 - ADAM_B1) * g
        nv = ADAM_B2 * v_ref[...] + (1.0 - ADAM_B2) * (g * g)
        g_ref[...] = g
        nm_ref[...] = nm
        nv_ref[...] = nv
        d_ref[...] = -ADAM_LR * ((nm * c1) / (jnp.sqrt(nv * c2) + ADAM_EPS) + ADAM_WD * w_ref[...])

    blk = pl.BlockSpec((tr, C), lambda i: (i, 0))
    return pl.pallas_call(
        body, name=name, grid=(R // tr,),
        in_specs=[blk] * 5, out_specs=[blk] * 4,
        out_shape=[jax.ShapeDtypeStruct((R, C), F32)] * 4,
        compiler_params=_params(("parallel",)),
    )(w, g0, g1, m, v)


def _sum_slabs(a, name):
    K, R, C = a.shape
    tr = R
    for cand in (512, 256, 128, 64, 32, 16, 8):
        if R % cand == 0 and cand * C * 4 * K <= (4 << 20):
            tr = cand
            break

    def body(a_ref, o_ref):
        t = a_ref[0].astype(F32)
        for k in range(1, K):
            t = t + a_ref[k].astype(F32)
        o_ref[...] = t

    return pl.pallas_call(
        body, name=name, grid=(R // tr,),
        in_specs=[pl.BlockSpec((K, tr, C), lambda i: (0, i, 0))],
        out_specs=pl.BlockSpec((tr, C), lambda i: (i, 0)),
        out_shape=jax.ShapeDtypeStruct((R, C), F32),
        compiler_params=_params(("parallel",)),
    )(a)


MESH = pl.DeviceIdType.MESH
HBM = pl.BlockSpec(memory_space=pltpu.HBM)


def _chip_peer(x, y, c, k):
    return (x ^ (k >> 1), y ^ (k & 1), c)


def _gather_chips(arrs):
    n = len(arrs)
    half = arrs[0].shape[0] // 2
    assert all(a.shape[0] == 2 * half for a in arrs)

    def body(*refs):
        ins, outs = refs[:n], refs[n:2 * n]
        send1, recv1, send2, recv2, loc = refs[2 * n:]
        x, y, c = lax.axis_index("x"), lax.axis_index("y"), lax.axis_index("c")
        me = 2 * x + y
        mine, theirs = pl.ds(c * half, half), pl.ds((1 - c) * half, half)
        sibling = (x, y, 1 - c)
        waits = []
        for a in range(n):
            own = pltpu.make_async_copy(ins[a], outs[a].at[me], loc.at[a])
            own.start()
            waits.append(own.wait)

        def over_ici(a, k, src_chip):
            return pltpu.make_async_remote_copy(
                src_ref=ins[a].at[mine], dst_ref=outs[a].at[src_chip, mine], send_sem=send1.at[3 * a + k - 1],
                recv_sem=recv1.at[3 * a + k - 1], device_id=_chip_peer(x, y, c, k), device_id_type=MESH)

        def to_sibling(a, k, layers):
            block = outs[a].at[me ^ k, layers]
            return pltpu.make_async_remote_copy(
                src_ref=block, dst_ref=block, send_sem=send2.at[3 * a + k - 1], recv_sem=recv2.at[3 * a + k - 1],
                device_id=sibling, device_id_type=MESH)

        for a in range(n):
            for k in (1, 2, 3):
                cp = over_ici(a, k, me)
                cp.start()
                waits.append(cp.wait_send)
        for a in range(n):
            for k in (1, 2, 3):
                over_ici(a, k, me ^ k).wait_recv()
                cp = to_sibling(a, k, mine)
                cp.start()
                waits.append(cp.wait_send)
        for a in range(n):
            for k in (1, 2, 3):
                to_sibling(a, k, theirs).wait_recv()
        for w in waits:
            w()

    return pl.pallas_call(
        body, name="gather_weights",
        in_specs=[HBM] * n, out_specs=[HBM] * n,
        out_shape=[jax.ShapeDtypeStruct((4,) + a.shape, a.dtype) for a in arrs],
        scratch_shapes=[pltpu.SemaphoreType.DMA((3 * n,)), pltpu.SemaphoreType.DMA((3 * n,)),
                        pltpu.SemaphoreType.DMA((3 * n,)), pltpu.SemaphoreType.DMA((3 * n,)),
                        pltpu.SemaphoreType.DMA((n,))],
    )(*arrs)


def _scatter_chips(arrs, small):
    n = len(arrs)

    def body(*refs):
        ins, small_in = refs[:n], refs[n]
        outs, small_out = refs[n + 1:2 * n + 1], refs[2 * n + 1]
        send, recv, loc, ssend, srecv = refs[2 * n + 2:]
        x, y, c = lax.axis_index("x"), lax.axis_index("y"), lax.axis_index("c")
        me = 2 * x + y
        me8 = 4 * x + 2 * y + c
        copies = []
        for a in range(n):
            own = pltpu.make_async_copy(ins[a].at[me], outs[a].at[me], loc.at[a])
            own.start()
            copies.append(own)
        own = pltpu.make_async_copy(small_in, small_out.at[me8], loc.at[n])
        own.start()
        copies.append(own)
        for k in range(1, 8):
            cp = pltpu.make_async_remote_copy(
                src_ref=small_in, dst_ref=small_out.at[me8], send_sem=ssend.at[k - 1], recv_sem=srecv.at[k - 1],
                device_id=(x ^ (k >> 2), y ^ ((k >> 1) & 1), c ^ (k & 1)), device_id_type=MESH)
            cp.start()
            copies.append(cp)
        for a in range(n):
            for k in (1, 2, 3):
                cp = pltpu.make_async_remote_copy(
                    src_ref=ins[a].at[me ^ k], dst_ref=outs[a].at[me], send_sem=send.at[3 * a + k - 1],
                    recv_sem=recv.at[3 * a + k - 1], device_id=_chip_peer(x, y, c, k), device_id_type=MESH)
                cp.start()
                copies.append(cp)
        for cp in copies:
            cp.wait()

    return pl.pallas_call(
        body, name="scatter_grads",
        in_specs=[HBM] * (n + 1), out_specs=[HBM] * (n + 1),
        out_shape=[jax.ShapeDtypeStruct(a.shape, a.dtype) for a in arrs]
        + [jax.ShapeDtypeStruct((8,) + small.shape, small.dtype)],
        scratch_shapes=[pltpu.SemaphoreType.DMA((3 * n,)), pltpu.SemaphoreType.DMA((3 * n,)),
                        pltpu.SemaphoreType.DMA((n + 1,)), pltpu.SemaphoreType.DMA((7,)),
                        pltpu.SemaphoreType.DMA((7,))],
    )(*arrs, small)


def _swap_cores(arrs):
    n = len(arrs)

    def body(*refs):
        ins, outs = refs[:n], refs[n:2 * n]
        send, recv = refs[2 * n:]
        x, y, c = lax.axis_index("x"), lax.axis_index("y"), lax.axis_index("c")
        copies = []
        for a in range(n):
            cp = pltpu.make_async_remote_copy(
                src_ref=ins[a], dst_ref=outs[a], send_sem=send.at[a], recv_sem=recv.at[a],
                device_id=(x, y, 1 - c), device_id_type=MESH)
            cp.start()
            copies.append(cp)
        for cp in copies:
            cp.wait()

    return pl.pallas_call(
        body, name="swap_cores",
        in_specs=[HBM] * n, out_specs=[HBM] * n,
        out_shape=[jax.ShapeDtypeStruct(a.shape, a.dtype) for a in arrs],
        scratch_shapes=[pltpu.SemaphoreType.DMA((n,)), pltpu.SemaphoreType.DMA((n,))],
    )(*arrs)


def _pad_last(a, n):
    return jnp.pad(a, [(0, 0)] * (a.ndim - 1) + [(0, n - a.shape[-1])])


def _pad_w_in(w):
    lead = w.shape[:-1]
    seg = lambda a, b: w[..., a:b]
    ga = _pad_last(seg(2720, 3232).reshape(lead + (N_HEADS, V_DIM)), HP).reshape(lead + (1024,))
    kpe = jnp.pad(seg(640, 672), [(0, 0)] * len(lead) + [(NOPE_DIM, HP - QK_DIM)])
    zero = jnp.zeros(lead + (PW - 7936,), w.dtype)
    return jnp.concatenate(
        [seg(4256, 7328), ga, seg(672, 1184), seg(2208, 2720), seg(3232, 3744), seg(3744, 4256),
         seg(1184, 1696), seg(1696, 2208), seg(0, 384), seg(384, 640), kpe, zero], axis=-1)


def _unpad_w_in(w):
    lead = w.shape[:-1]
    seg = lambda a, n: w[..., a:a + n]
    ga = seg(O_GA, 1024).reshape(lead + (N_HEADS, HP))[..., :V_DIM].reshape(lead + (N_HEADS * V_DIM,))
    return jnp.concatenate(
        [seg(O_QL, 384), seg(O_KVL, 256), seg(O_KPE + NOPE_DIM, ROPE_DIM), seg(O_CB, 512), seg(O_CC, 512),
         seg(O_CU, 512), seg(O_QM, 512), ga, seg(O_GC, 512), seg(O_GM, 512), seg(O_R, 3072)], axis=-1)


def _cols_from_shards(g):
    _, L, R, C = g.shape
    return jnp.transpose(g, (1, 2, 0, 3)).reshape(L, R, 4 * C)


def _cols_to_shards(w):
    L, R, C4 = w.shape
    return jnp.transpose(w.reshape(L, R, 4, C4 // 4), (2, 0, 1, 3))


def _rows_from_shards(g):
    _, L, R, C = g.shape
    return jnp.transpose(g, (1, 0, 2, 3)).reshape(L, 4 * R, C)


def _rows_to_shards(w):
    L, R4, C = w.shape
    return jnp.transpose(w.reshape(L, 4, R4 // 4, C), (1, 0, 2, 3))


def _t(w):
    return jnp.swapaxes(w, -1, -2)


def _layer_fwd(x, mem, tabs, p):
    proj, xn = _inproj(x, p["norm_g"], p["w_in"])
    q, k, v, qn, kvn = _mla_prep(proj, tabs, p["q_norm_g"], p["kv_norm_g"], p["q_head_g"], p["k_head_g"],
                                 p["w_uq"], p["w_uk"], p["w_uv"])
    o, lse = _attn_fwd(q, k, v)
    mk, mv = _memkv(mem, p["mem_norm_g"], p["mem_k_g"], p["w_mkv"])
    x_new, oa, oc, om, u, y = _merge(proj, o, x, p["b_gate"], p["conv_wb"], p["mem_q_g"], mk, mv,
                                     p["w_br_attn"], p["w_br_conv"], p["w_br_mem"], p["w_out"])
    saved = dict(x=x, proj=proj, xn=xn, q=q, k=k, v=v, qn=qn, kvn=kvn, o=o, lse=lse, mk=mk, mv=mv,
                 oa=oa, oc=oc, om=om, u=u, y=y)
    return x_new, saved


def _layer_bwd(g, mem, tabs, p, s):
    S = g.shape[0]
    dmain, dcv, d_o, delta, du, dbg, dmk, dmv, dmqg = _merge_bwd(
        g, s["proj"], s["o"], s["u"], p["b_gate"], p["conv_wb"], p["mem_q_g"], s["mk"], s["mv"],
        p["w_out_t"], p["w_br_attn_t"], p["w_br_conv_t"], p["w_br_mem_t"])
    dccu, dconv = _conv_bwd(dcv, s["proj"], p["conv_wb"])
    dq, dkt, dvt = _attn_bwd(s["q"], s["k"], s["v"], d_o, _t(s["q"]), _t(d_o), s["lse"], delta)
    dk, dv = _t(dkt), _t(dvt).astype(BF16)
    dsega, dqraw, dkraw, dqg, dkvg, dqhg, dkhg = _mla_prep_bwd(
        s["proj"], tabs, dq, dk, dv, p["q_norm_g"], p["kv_norm_g"], p["q_head_g"], p["k_head_g"],
        p["w_uq"], p["w_uk"], p["w_uq_t"], p["w_uk_t"], p["w_uv_t"])
    dx, dng = _inproj_bwd(dmain, dccu, dsega, p["w_in_t"], s["x"], g, p["norm_g"])
    dwmkv, dmng, dmkg = _memkv_bwd(mem, p["mem_norm_g"], p["mem_k_g"], p["w_mkv"], p["w_mkv_t"], dmk, dmv)
    grads = dict(
        norm_g=dng, b_gate=dbg, q_norm_g=dqg, kv_norm_g=dkvg, q_head_g=dqhg, k_head_g=dkhg,
        conv_wb=dconv, mem_norm_g=dmng, mem_q_g=dmqg, mem_k_g=dmkg, w_mkv=dwmkv,
        w_in=jnp.concatenate([_mm_tn(s["xn"], dmain, "grad_w_in"), _mm_tn(s["xn"], dccu, "grad_w_in_conv"),
                              _mm_tn(s["xn"], dsega, "grad_w_in_lat")], axis=1),
        w_uq=_mm_tn(s["qn"], dqraw, "grad_w_uq"),
        w_uk=_mm_tn(s["kvn"], dkraw, "grad_w_uk"),
        w_uv=_mm_tn(s["kvn"], dv, "grad_w_uv"),
        w_br_attn=_mm_tn(s["oa"], du, "grad_w_br_attn", 0, 1024),
        w_br_conv=_mm_tn(s["oc"], du, "grad_w_br_conv", 1024, 1024),
        w_br_mem=_mm_tn(s["om"], du, "grad_w_br_mem", 2048, 1024),
        w_out=_mm_tn(s["y"], g, "grad_w_out"),
    )
    return dx, grads


def _layer_params(full, l):
    p = {}
    w_in = _pad_w_in(full["w_in"][l])
    w_uq = _pad_last(full["w_uq"][l].reshape(Q_RANK, N_HEADS, QK_DIM), HP).reshape(Q_RANK, 1024)
    ukv = full["w_ukv"][l].reshape(KV_RANK, N_HEADS, NOPE_DIM + V_DIM)
    w_uk = _pad_last(ukv[..., :NOPE_DIM], HP).reshape(KV_RANK, 1024)
    w_uv = _pad_last(ukv[..., NOPE_DIM:], HP).reshape(KV_RANK, 1024)
    w_ba = jnp.pad(full["w_br_attn"][l].reshape(N_HEADS, V_DIM, D_MODEL), ((0, 0), (0, HP - V_DIM), (0, 0)))
    w_ba = w_ba.reshape(1024, D_MODEL)
    p.update(w_in=w_in, w_uq=w_uq, w_uk=w_uk, w_uv=w_uv, w_br_attn=w_ba, w_br_conv=full["w_br_conv"][l],
             w_br_mem=full["w_br_mem"][l], w_out=full["w_out"][l], w_mkv=full["w_mkv"][l])
    for n in ("w_in", "w_uq", "w_uk", "w_uv", "w_br_attn", "w_br_conv", "w_br_mem", "w_out", "w_mkv"):
        p[n + "_t"] = _t(p[n])
    for n in ("norm_g", "b_gate", "q_norm_g", "kv_norm_g", "mem_norm_g", "mem_q_g", "mem_k_g"):
        p[n] = full[n][l][None, :]
    p["q_head_g"] = _pad_last(full["q_head_g"][l][None, :], HP)
    p["k_head_g"] = _pad_last(full["k_head_g"][l][None, :], HP)
    p["conv_wb"] = jnp.concatenate(
        [full["conv_w"][l], full["conv_b"][l][None, :], jnp.zeros((4, CONV_W), F32)], axis=0)
    return p


def _local_step(x, mem, positions, full, target):
    S = x.shape[0]
    invf16 = ROPE_BASE ** (-jnp.arange(0, ROPE_DIM, 2, dtype=F32) / ROPE_DIM)
    invf = jnp.concatenate([jnp.zeros((NOPE_DIM,), F32), invf16, invf16, jnp.zeros((HP - QK_DIM,), F32)])[None, :]
    tabs = _rope_tables(jnp.broadcast_to(positions.reshape(S, 1), (S, HP)), invf)
    params = [_layer_params(full, l) for l in range(DEPTH)]
    saved = []
    h = x
    for l in range(DEPTH):
        h, s = _layer_fwd(h, mem, tabs, params[l])
        saved.append(s)
    g, loss_part = _loss_head(h, target)
    per_layer = [None] * DEPTH
    for l in reversed(range(DEPTH)):
        g, per_layer[l] = _layer_bwd(g, mem, tabs, params[l], saved[l])
    st = lambda n: jnp.stack([per_layer[l][n] for l in range(DEPTH)])
    out = {}
    out["w_in"] = _unpad_w_in(st("w_in"))
    out["w_uq"] = st("w_uq").reshape(DEPTH, Q_RANK, N_HEADS, HP)[..., :QK_DIM].reshape(DEPTH, Q_RANK, N_HEADS * QK_DIM)
    duk = st("w_uk").reshape(DEPTH, KV_RANK, N_HEADS, HP)[..., :NOPE_DIM]
    duv = st("w_uv").reshape(DEPTH, KV_RANK, N_HEADS, HP)[..., :V_DIM]
    out["w_ukv"] = jnp.concatenate([duk, duv], axis=-1).reshape(DEPTH, KV_RANK, 1024)
    out["w_br_attn"] = st("w_br_attn").reshape(DEPTH, N_HEADS, HP, D_MODEL)[:, :, :V_DIM].reshape(DEPTH, 512, D_MODEL)
    for n in ("w_br_conv", "w_br_mem", "w_out", "w_mkv"):
        out[n] = st(n)
    for n in ("norm_g", "b_gate", "q_norm_g", "kv_norm_g", "mem_norm_g", "mem_q_g", "mem_k_g"):
        out[n] = st(n)[:, 0, :]
    out["q_head_g"] = st("q_head_g")[:, 0, :QK_DIM]
    out["k_head_g"] = st("k_head_g")[:, 0, :QK_DIM]
    cwb = st("conv_wb")
    out["conv_w"] = cwb[:, 0:3, :]
    out["conv_b"] = cwb[:, 3, :]
    return loss_part, g, out


_COL_SHARDED = ("w_in", "w_uq", "w_ukv", "w_br_attn", "w_br_conv", "w_br_mem")
_ROW_SHARDED = ("w_mkv", "w_out")
_BIG = _COL_SHARDED + _ROW_SHARDED
_SMALL = ("norm_g", "b_gate", "q_norm_g", "kv_norm_g", "q_head_g", "k_head_g", "conv_w", "conv_b",
          "mem_norm_g", "mem_q_g", "mem_k_g")
_ORDER = ("norm_g", "w_in", "b_gate", "q_norm_g", "w_uq", "kv_norm_g", "w_ukv", "q_head_g", "k_head_g",
          "conv_w", "conv_b", "mem_norm_g", "w_mkv", "mem_q_g", "mem_k_g", "w_br_attn", "w_br_conv",
          "w_br_mem", "w_out")


def _pack_small(d, extra):
    flat = jnp.concatenate([d[n].reshape(-1) for n in _SMALL] + [extra.reshape(-1)])
    n = flat.shape[0]
    rows = -(-n // 1024) * 8
    return jnp.pad(flat, (0, rows * 128 - n)).reshape(rows, 128)


def _unpack_small(packed, like):
    flat = packed.reshape(-1)
    out, off = {}, 0
    for n in _SMALL:
        sz = int(np.prod(like[n].shape))
        out[n] = flat[off:off + sz].reshape(like[n].shape)
        off += sz
    return out, flat[off:]


def kernel(x, mem, positions, norm_g, w_in, b_gate, q_norm_g, w_uq, kv_norm_g, w_ukv, q_head_g, k_head_g, conv_w, conv_b, mem_norm_g, w_mkv, mem_q_g, mem_k_g, w_br_attn, w_br_conv, w_br_mem, w_out, loss_target, m_norm_g, m_w_in, m_b_gate, m_q_norm_g, m_w_uq, m_kv_norm_g, m_w_ukv, m_q_head_g, m_k_head_g, m_conv_w, m_conv_b, m_mem_norm_g, m_w_mkv, m_mem_q_g, m_mem_k_g, m_w_br_attn, m_w_br_conv, m_w_br_mem, m_w_out, v_norm_g, v_w_in, v_b_gate, v_q_norm_g, v_w_uq, v_kv_norm_g, v_w_ukv, v_q_head_g, v_k_head_g, v_conv_w, v_conv_b, v_mem_norm_g, v_w_mkv, v_mem_q_g, v_mem_k_g, v_w_br_attn, v_w_br_conv, v_w_br_mem, v_w_out):
    w = dict(norm_g=norm_g, w_in=w_in, b_gate=b_gate, q_norm_g=q_norm_g, w_uq=w_uq, kv_norm_g=kv_norm_g,
             w_ukv=w_ukv, q_head_g=q_head_g, k_head_g=k_head_g, conv_w=conv_w, conv_b=conv_b,
             mem_norm_g=mem_norm_g, w_mkv=w_mkv, mem_q_g=mem_q_g, mem_k_g=mem_k_g, w_br_attn=w_br_attn,
             w_br_conv=w_br_conv, w_br_mem=w_br_mem, w_out=w_out)
    m = dict(norm_g=m_norm_g, w_in=m_w_in, b_gate=m_b_gate, q_norm_g=m_q_norm_g, w_uq=m_w_uq,
             kv_norm_g=m_kv_norm_g, w_ukv=m_w_ukv, q_head_g=m_q_head_g, k_head_g=m_k_head_g, conv_w=m_conv_w,
             conv_b=m_conv_b, mem_norm_g=m_mem_norm_g, w_mkv=m_w_mkv, mem_q_g=m_mem_q_g, mem_k_g=m_mem_k_g,
             w_br_attn=m_w_br_attn, w_br_conv=m_w_br_conv, w_br_mem=m_w_br_mem, w_out=m_w_out)
    v = dict(norm_g=v_norm_g, w_in=v_w_in, b_gate=v_b_gate, q_norm_g=v_q_norm_g, w_uq=v_w_uq,
             kv_norm_g=v_kv_norm_g, w_ukv=v_w_ukv, q_head_g=v_q_head_g, k_head_g=v_k_head_g, conv_w=v_conv_w,
             conv_b=v_conv_b, mem_norm_g=v_mem_norm_g, w_mkv=v_w_mkv, mem_q_g=v_mem_q_g, mem_k_g=v_mem_k_g,
             w_br_attn=v_w_br_attn, w_br_conv=v_w_br_conv, w_br_mem=v_w_br_mem, w_out=v_w_out)
    chip = 2 * lax.axis_index("x") + lax.axis_index("y")

    gathered = _gather_chips([w[n].astype(BF16) for n in _BIG] + [w["conv_w"]])
    full = {n: w[n] for n in _SMALL}
    for n, gth in zip(_BIG, gathered):
        full[n] = _cols_from_shards(gth) if n in _COL_SHARDED else _rows_from_shards(gth)
    full["conv_w"] = _cols_from_shards(gathered[-1])

    loss_part, grad_x, grads = _local_step(x[0], mem[0], positions[0], full, loss_target[0])

    to_owner = [(_cols_to_shards(grads[n]) if n in _COL_SHARDED else _rows_to_shards(grads[n])).astype(BF16)
                for n in _BIG]
    loss_vec = jnp.zeros((128,), F32).at[0].set(0.5 / D_MODEL * jnp.sum(loss_part))
    small = _pack_small(grads, loss_vec)
    scattered = _scatter_chips(to_owner, small)
    small_sum = _sum_slabs(scattered[-1], "sum_small")
    partial = []
    for n, slabs in zip(_BIG, scattered[:-1]):
        sh = slabs.shape
        partial.append(_sum_slabs(slabs.reshape(4, sh[1] * sh[2], sh[3]), "sum_" + n))
    other = _swap_cores(partial)

    small_g, tail = _unpack_small(small_sum, {n: (grads[n]) for n in _SMALL})
    loss = tail[0]
    small_g["conv_w"] = lax.dynamic_slice_in_dim(small_g["conv_w"], chip * (CONV_W // 4), CONV_W // 4, axis=2)

    outs_g, outs_d, outs_m, outs_v = {}, {}, {}, {}
    for n, p0, p1 in zip(_BIG, partial, other):
        shape = w[n].shape
        flat = lambda t: t.reshape(p0.shape)
        g_, d_, m_, v_ = _adamw(flat(w[n]), p0, p1, flat(m[n]), flat(v[n]), "adamw_" + n)
        outs_g[n], outs_d[n], outs_m[n], outs_v[n] = (t.reshape(shape) for t in (g_, d_, m_, v_))
    zero_small = jnp.zeros_like(small_sum)
    pk = lambda d: _pack_small(d, jnp.zeros((128,), F32))
    g_, d_, m_, v_ = _adamw(pk(w), _pack_small(small_g, jnp.zeros((128,), F32)), zero_small, pk(m), pk(v),
                            "adamw_small")
    like = {n: w[n] for n in _SMALL}
    for dst, packed in ((outs_g, g_), (outs_d, d_), (outs_m, m_), (outs_v, v_)):
        dst.update(_unpack_small(packed, like)[0])

    return (loss, grad_x[None], *[outs_g[n] for n in _ORDER], *[outs_d[n] for n in _ORDER],
            *[outs_m[n] for n in _ORDER], *[outs_v[n] for n in _ORDER])
```

```python
import functools

import numpy as np
import jax
import jax.numpy as jnp
from jax import lax
from jax.experimental import pallas as pl
from jax.experimental.pallas import tpu as pltpu

F32 = jnp.float32
BF16 = jnp.bfloat16

D_MODEL = 1024
DEPTH = 4
N_HEADS = 8
QK_DIM = 96
NOPE_DIM = 64
ROPE_DIM = 32
V_DIM = 64
Q_RANK = 384
KV_RANK = 256
CONV_W = 512
MEM_HEADS = 4
MEM_HD = 128
MEM_W = 512
IN_WIDTH = 7328
PW = 8192
HP = 128
PROJ_HALO = 16
F32_HALO = 8
EPS = 1e-6
ROPE_BASE = 10000.0
SCALE = QK_DIM ** -0.5
MEM_SCALE = MEM_HD ** -0.5

ADAM_LR = 0.001
ADAM_B1 = 0.9
ADAM_B2 = 0.999
ADAM_EPS = 1e-08
ADAM_WD = 0.01
ADAM_STEP = 10

VMEM_LIMIT_V7X = 56 * 1024 * 1024

O_R, O_GA, O_CB, O_QM, O_GC, O_GM, O_CC, O_CU, O_QL, O_KVL, O_KPE = (
    0, 3072, 4096, 4608, 5120, 5632, 6144, 6656, 7168, 7552, 7808)


def _params(sem, vmem=VMEM_LIMIT_V7X):
    return pltpu.CompilerParams(dimension_semantics=sem, vmem_limit_bytes=vmem)


def _sigmoid(t):
    return 0.5 * jnp.tanh(0.5 * t) + 0.5


def _silu_and_grad(g):
    sg = _sigmoid(g)
    return g * sg, sg * (1.0 + g * (1.0 - sg))


def _rms(t, g, n=None):
    n = t.shape[-1] if n is None else n
    r = lax.rsqrt(jnp.sum(t * t, axis=-1, keepdims=True) * (1.0 / n) + EPS)
    return (t * r) * g


def _rms_parts(t, n=None):
    n = t.shape[-1] if n is None else n
    r = lax.rsqrt(jnp.sum(t * t, axis=-1, keepdims=True) * (1.0 / n) + EPS)
    return r, t * r


def _rms_bwd(dhat, hat, r, n):
    return r * (dhat - hat * (jnp.sum(dhat * hat, axis=-1, keepdims=True) * (1.0 / n)))


def _rope(t, c, sa, sb):
    return t * c + pltpu.roll(t, HP - 16, 1) * sa + pltpu.roll(t, 16, 1) * sb


def _rope_t(d, c, sa, sb):
    return d * c + pltpu.roll(d * sa, 16, 1) + pltpu.roll(d * sb, HP - 16, 1)


def _dot(a, b):
    return jnp.dot(a, b, preferred_element_type=F32)


def _dot_nt(a, b):
    return lax.dot_general(a, b, (((1,), (1,)), ((), ())), preferred_element_type=F32)


def _dot_tn(a, b):
    return lax.dot_general(a, b, (((0,), (0,)), ((), ())), preferred_element_type=F32)


def _colsum(t):
    return jnp.sum(t, axis=0, keepdims=True)


def _tile(n, t):
    t = min(n, t)
    assert n % t == 0, (n, t)
    return t


def _rope_tables(pos_b, invf):
    S = pos_b.shape[0]
    tm = _tile(S, 1024)

    def body(pos_ref, invf_ref, c_ref, sa_ref, sb_ref):
        ang = pos_ref[...].astype(F32) * invf_ref[...]
        lane = lax.broadcasted_iota(jnp.int32, ang.shape, 1)
        cs = jnp.cos(ang)
        sn = jnp.sin(ang)
        c_ref[...] = jnp.where(lane < NOPE_DIM, 1.0, jnp.where(lane < QK_DIM, cs, 0.0))
        sa_ref[...] = jnp.where((lane >= NOPE_DIM) & (lane < NOPE_DIM + 16), -sn, 0.0)
        sb_ref[...] = jnp.where((lane >= NOPE_DIM + 16) & (lane < QK_DIM), sn, 0.0)

    blk = pl.BlockSpec((tm, HP), lambda i: (i, 0))
    return pl.pallas_call(
        body, name="rope_tables", grid=(S // tm,),
        in_specs=[blk, pl.BlockSpec((1, HP), lambda i: (0, 0))],
        out_specs=[blk, blk, blk],
        out_shape=[jax.ShapeDtypeStruct((S, HP), F32)] * 3,
        compiler_params=_params(("parallel",)),
    )(pos_b, invf)


def _inproj(x, g, w):
    S = x.shape[0]
    tm = _tile(S, 256)

    def body(x_ref, g_ref, w_ref, proj_ref, xn_ref):
        h = _rms(x_ref[...], g_ref[...]).astype(BF16)
        xn_ref[...] = h
        proj_ref[...] = _dot(h, w_ref[...]).astype(BF16)

    row = lambda n: pl.BlockSpec((tm, n), lambda i: (i, 0))
    return pl.pallas_call(
        body, name="inproj", grid=(S // tm,),
        in_specs=[row(D_MODEL), pl.BlockSpec((1, D_MODEL), lambda i: (0, 0)),
                  pl.BlockSpec((D_MODEL, PW), lambda i: (0, 0))],
        out_specs=[row(PW), row(D_MODEL)],
        out_shape=[jax.ShapeDtypeStruct((S, PW), BF16), jax.ShapeDtypeStruct((S, D_MODEL), BF16)],
        compiler_params=_params(("parallel",)),
    )(x, g, w)


def _mla_prep(proj, tabs, qg, kvg, qhg, khg, wuq, wuk, wuv):
    S = proj.shape[0]
    tm = _tile(S, 512)

    def body(a_ref, c_ref, sa_ref, sb_ref, qg_ref, kvg_ref, qhg_ref, khg_ref, wuq_ref, wuk_ref, wuv_ref,
             q_ref, k_ref, v_ref, qn_ref, kvn_ref, kmax_ref):
        @pl.when(pl.program_id(0) == 0)
        def _():
            kmax_ref[...] = jnp.zeros(kmax_ref.shape, F32)

        ql = a_ref[:, 0:Q_RANK].astype(F32)
        kvl = a_ref[:, Q_RANK:Q_RANK + KV_RANK].astype(F32)
        kpe = a_ref[:, Q_RANK + KV_RANK:Q_RANK + KV_RANK + HP].astype(F32)
        qn = _rms(ql, qg_ref[...]).astype(BF16)
        kvn = _rms(kvl, kvg_ref[...]).astype(BF16)
        qn_ref[...] = qn
        kvn_ref[...] = kvn
        qraw = _dot(qn, wuq_ref[...])
        kn = _dot(kvn, wuk_ref[...])
        v_ref[...] = _dot(kvn, wuv_ref[...]).astype(BF16)
        c, sa, sb = c_ref[...], sa_ref[...], sb_ref[...]
        for h in range(N_HEADS):
            sl = slice(h * HP, (h + 1) * HP)
            tq = _rms(qraw[:, sl], qhg_ref[...], QK_DIM)
            q_ref[:, sl] = (_rope(tq, c, sa, sb) * SCALE).astype(BF16)
            tk = _rms(kn[:, sl] + kpe, khg_ref[...], QK_DIM)
            kb = _rope(tk, c, sa, sb).astype(BF16)
            k_ref[:, sl] = kb
            kf = kb.astype(F32)
            norm2 = jnp.max(jnp.sum(kf * kf, axis=-1, keepdims=True), axis=0, keepdims=True)
            kmax_ref[:, sl] = jnp.maximum(kmax_ref[:, sl], norm2)

    row = lambda w: pl.BlockSpec((tm, w), lambda i: (i, 0))
    full = lambda a: pl.BlockSpec(a.shape, lambda i: (0,) * a.ndim)
    return pl.pallas_call(
        body, name="mla_prep", grid=(S // tm,),
        in_specs=[pl.BlockSpec((tm, 1024), lambda i: (i, 7)), row(HP), row(HP), row(HP),
                  full(qg), full(kvg), full(qhg), full(khg), full(wuq), full(wuk), full(wuv)],
        out_specs=[row(1024), row(1024), row(1024), row(Q_RANK), row(KV_RANK),
                   pl.BlockSpec((1, 1024), lambda i: (0, 0))],
        out_shape=[jax.ShapeDtypeStruct((S, 1024), BF16)] * 3
        + [jax.ShapeDtypeStruct((S, Q_RANK), BF16), jax.ShapeDtypeStruct((S, KV_RANK), BF16),
           jax.ShapeDtypeStruct((1, 1024), F32)],
        compiler_params=_params(("arbitrary",)),
    )(proj, *tabs, qg, kvg, qhg, khg, wuq, wuk, wuv)


SAFE_SCORE_BOUND = 30.0


def _attn_fwd(qt, k, vt, kmax2):
    S = k.shape[0]
    tq, tk = _tile(S, 1024), _tile(S, 2048)
    nk = S // tk

    def body(qt_ref, k_ref, vt_ref, kmax_ref, o_ref, lse_ref, m_s, l_s, acc_s):
        qtv = qt_ref[...]
        qf = qtv.astype(F32)
        bound = jnp.sqrt(jnp.sum(qf * qf, axis=0, keepdims=True) * kmax_ref[0:1, 0:1]) * 1.001 + 1e-6
        safe = jnp.max(bound) <= SAFE_SCORE_BOUND
        l_s[...] = jnp.zeros(l_s.shape, F32)
        acc_s[...] = jnp.zeros(acc_s.shape, F32)

        def keys(c):
            return pl.ds(pl.multiple_of(c * tk, tk), tk)

        @pl.when(safe)
        def _():
            def step(c, carry):
                pt = jnp.exp(_dot(k_ref[keys(c), :], qtv) - bound)
                l_s[...] += jnp.sum(pt, axis=0, keepdims=True)
                acc_s[...] += _dot(vt_ref[:, keys(c)], pt.astype(BF16))
                return carry

            lax.fori_loop(0, nk, step, 0)
            m_s[...] = bound

        @pl.when(jnp.logical_not(safe))
        def _():
            m_s[...] = jnp.full(m_s.shape, -jnp.inf, F32)

            def step(c, carry):
                st = _dot(k_ref[keys(c), :], qtv)
                m_prev = m_s[...]
                m_new = jnp.maximum(m_prev, jnp.max(st, axis=0, keepdims=True))
                alpha = jnp.exp(m_prev - m_new)
                pt = jnp.exp(st - m_new)
                l_s[...] = alpha * l_s[...] + jnp.sum(pt, axis=0, keepdims=True)
                acc_s[...] = alpha * acc_s[...] + _dot(vt_ref[:, keys(c)], pt.astype(BF16))
                m_s[...] = m_new
                return carry

            lax.fori_loop(0, nk, step, 0)

        o_ref[...] = (acc_s[...] / l_s[...]).T
        lse_ref[0, 0] = m_s[...] + jnp.log(l_s[...])

    return pl.pallas_call(
        body, name="attn_fwd", grid=(N_HEADS, S // tq),
        in_specs=[pl.BlockSpec((HP, tq), lambda h, i: (h, i)),
                  pl.BlockSpec((S, HP), lambda h, i: (0, h)),
                  pl.BlockSpec((HP, S), lambda h, i: (h, 0)),
                  pl.BlockSpec((1, HP), lambda h, i: (0, h))],
        out_specs=[pl.BlockSpec((tq, HP), lambda h, i: (i, h)),
                   pl.BlockSpec((1, 1, 1, tq), lambda h, i: (h, i, 0, 0))],
        out_shape=[jax.ShapeDtypeStruct((S, N_HEADS * HP), F32),
                   jax.ShapeDtypeStruct((N_HEADS, S // tq, 1, tq), F32)],
        scratch_shapes=[pltpu.VMEM((1, tq), F32), pltpu.VMEM((1, tq), F32), pltpu.VMEM((HP, tq), F32)],
        compiler_params=_params(("parallel", "parallel")),
    )(qt, k, vt, kmax2)


def _memkv(mem, mng, mkg, wmkv):
    M = mem.shape[0]

    def body(mem_ref, mng_ref, mkg_ref, w_ref, mk_ref, mv_ref):
        mn = _rms(mem_ref[...], mng_ref[...]).astype(BF16)
        mkv = _dot(mn, w_ref[...])
        for h in range(MEM_HEADS):
            kraw = mkv[:, 2 * MEM_HD * h:2 * MEM_HD * h + MEM_HD]
            mk_ref[:, MEM_HD * h:MEM_HD * (h + 1)] = _rms(kraw, mkg_ref[...]).astype(BF16)
            mv_ref[:, MEM_HD * h:MEM_HD * (h + 1)] = mkv[:, 2 * MEM_HD * h + MEM_HD:2 * MEM_HD * (h + 1)].astype(BF16)

    return pl.pallas_call(
        body, name="memkv",
        out_shape=[jax.ShapeDtypeStruct((M, MEM_W), BF16)] * 2,
        compiler_params=pltpu.CompilerParams(vmem_limit_bytes=VMEM_LIMIT_V7X),
    )(mem, mng, mkg, wmkv)


def _conv_shifts(cc, cu, hp_ref, hn_ref, i, n_tiles, tm):
    z = cc * cu
    last = PROJ_HALO - 1
    zp = hp_ref[last:last + 1, 0:CONV_W].astype(F32) * hp_ref[last:last + 1, CONV_W:2 * CONV_W].astype(F32)
    zn = hn_ref[0:1, 0:CONV_W].astype(F32) * hn_ref[0:1, CONV_W:2 * CONV_W].astype(F32)
    zp = jnp.where(i == 0, 0.0, zp)
    zn = jnp.where(i == n_tiles - 1, 0.0, zn)
    row = lax.broadcasted_iota(jnp.int32, z.shape, 0)
    z_up = jnp.where(row == 0, zp, pltpu.roll(z, 1, 0))
    z_dn = jnp.where(row == tm - 1, zn, pltpu.roll(z, tm - 1, 0))
    return z, z_up, z_dn


def _halo_specs(tm, S, width, col, rows):
    per = tm // rows
    prev = pl.BlockSpec((rows, width), lambda i: (jnp.maximum(i * per - 1, 0), col))
    nxt = pl.BlockSpec((rows, width), lambda i: (jnp.minimum((i + 1) * per, S // rows - 1), col))
    return prev, nxt


def _mem_attend(qm, mqg, mk_h, mv_h):
    r, qhat = _rms_parts(qm)
    mq = (qhat * mqg).astype(BF16)
    s = _dot_nt(mq, mk_h) * MEM_SCALE
    e = jnp.exp(s - jnp.max(s, axis=-1, keepdims=True))
    p = e / jnp.sum(e, axis=-1, keepdims=True)
    pv = _dot(p.astype(BF16), mv_h)
    return r, qhat, mq, p, pv


def _merge(proj, o, x, bg, convw, mqg, mk, mv, wba, wbc, wbm, wo):
    S = x.shape[0]
    tm = _tile(S, 512)
    nt = S // tm

    def body(main_ref, ccu_ref, hp_ref, hn_ref, o_ref, x_ref, bg_ref, cw_ref, mqg_ref, mk_ref, mv_ref,
             wba_ref, wbc_ref, wbm_ref, wo_ref, xn_ref, oa_ref, oc_ref, om_ref, u_ref, y_ref):
        i = pl.program_id(0)
        sil_a, _ = _silu_and_grad(main_ref[:, O_GA:O_GA + 1024].astype(F32))
        oa = (o_ref[...] * sil_a).astype(BF16)
        oa_ref[...] = oa
        z, z_up, z_dn = _conv_shifts(ccu_ref[:, 0:CONV_W].astype(F32), ccu_ref[:, CONV_W:].astype(F32), hp_ref, hn_ref, i, nt, tm)
        cv = cw_ref[0:1, :] * z_up + cw_ref[1:2, :] * z + cw_ref[2:3, :] * z_dn + cw_ref[3:4, :]
        sil_c, _ = _silu_and_grad(main_ref[:, O_GC:O_GC + CONV_W].astype(F32))
        oc = (main_ref[:, O_CB:O_CB + CONV_W].astype(F32) * cv * sil_c).astype(BF16)
        oc_ref[...] = oc
        sil_m, _ = _silu_and_grad(main_ref[:, O_GM:O_GM + MEM_W].astype(F32))
        for h in range(MEM_HEADS):
            sl = slice(h * MEM_HD, (h + 1) * MEM_HD)
            qm = main_ref[:, O_QM + h * MEM_HD:O_QM + (h + 1) * MEM_HD].astype(F32)
            pv = _mem_attend(qm, mqg_ref[...], mk_ref[:, sl], mv_ref[:, sl])[4]
            om_ref[:, sl] = (pv * sil_m[:, sl]).astype(BF16)
        ua = _dot(oa, wba_ref[...])
        uc = _dot(oc, wbc_ref[...])
        um = _dot(om_ref[...], wbm_ref[...])
        u_ref[:, 0:1024] = ua.astype(BF16)
        u_ref[:, 1024:2048] = uc.astype(BF16)
        u_ref[:, 2048:3072] = um.astype(BF16)
        rg = _sigmoid(main_ref[:, O_R:O_R + 3072].astype(F32) + bg_ref[...])
        y = (rg[:, 0:1024] * ua + rg[:, 1024:2048] * uc + rg[:, 2048:3072] * um).astype(BF16)
        y_ref[...] = y
        xn_ref[...] = x_ref[...] + _dot(y, wo_ref[...])

    row = lambda w: pl.BlockSpec((tm, w), lambda i: (i, 0))
    full = lambda a: pl.BlockSpec(a.shape, lambda i: (0,) * a.ndim)
    hp, hn = _halo_specs(tm, S, 1024, 6, PROJ_HALO)
    return pl.pallas_call(
        body, name="merge", grid=(nt,),
        in_specs=[row(6144), pl.BlockSpec((tm, 1024), lambda i: (i, 6)), hp, hn, row(1024), row(1024),
                  full(bg), full(convw), full(mqg), full(mk), full(mv), full(wba), full(wbc), full(wbm), full(wo)],
        out_specs=[row(1024), row(1024), row(CONV_W), row(MEM_W), row(3072), row(1024)],
        out_shape=[jax.ShapeDtypeStruct((S, 1024), F32), jax.ShapeDtypeStruct((S, 1024), BF16),
                   jax.ShapeDtypeStruct((S, CONV_W), BF16), jax.ShapeDtypeStruct((S, MEM_W), BF16),
                   jax.ShapeDtypeStruct((S, 3072), BF16), jax.ShapeDtypeStruct((S, 1024), BF16)],
        compiler_params=_params(("parallel",)),
    )(proj, proj, proj, proj, o, x, bg, convw, mqg, mk, mv, wba, wbc, wbm, wo)


def _loss_head(xf, tgt):
    S = xf.shape[0]
    tm = _tile(S, 1024)

    def body(x_ref, t_ref, g_ref, acc_ref):
        @pl.when(pl.program_id(0) == 0)
        def _():
            acc_ref[...] = jnp.zeros(acc_ref.shape, F32)

        e = x_ref[...] - t_ref[...]
        g_ref[...] = e * (1.0 / D_MODEL)
        part = jnp.sum((e * e).reshape(tm // 8, 8, D_MODEL), axis=0)
        tot = part[:, 0:128]
        for k in range(1, D_MODEL // 128):
            tot = tot + part[:, 128 * k:128 * (k + 1)]
        acc_ref[...] += tot

    row = pl.BlockSpec((tm, D_MODEL), lambda i: (i, 0))
    return pl.pallas_call(
        body, name="loss_head", grid=(S // tm,),
        in_specs=[row, row],
        out_specs=[row, pl.BlockSpec((8, 128), lambda i: (0, 0))],
        out_shape=[jax.ShapeDtypeStruct((S, D_MODEL), F32), jax.ShapeDtypeStruct((8, 128), F32)],
        compiler_params=_params(("arbitrary",)),
    )(xf, tgt)


def _merge_bwd(g, proj, o, u, bg, convw, mqg, mk, mv, wot, wbat, wbct, wbmt):
    S = g.shape[0]
    tm = _tile(S, 256)
    nt = S // tm
    M = mk.shape[0]

    def body(g_ref, main_ref, ccu_ref, hp_ref, hn_ref, o_ref, u_ref, bg_ref, cw_ref, mqg_ref, mk_ref, mv_ref,
             wot_ref, wbat_ref, wbct_ref, wbmt_ref,
             dmain_ref, dcv_ref, do_ref, delta_ref, du_ref, dbg_ref, dmk_ref, dmv_ref, dmqg_ref):
        i = pl.program_id(0)

        @pl.when(i == 0)
        def _():
            dbg_ref[...] = jnp.zeros(dbg_ref.shape, F32)
            dmk_ref[...] = jnp.zeros(dmk_ref.shape, F32)
            dmv_ref[...] = jnp.zeros(dmv_ref.shape, F32)
            dmqg_ref[...] = jnp.zeros(dmqg_ref.shape, F32)

        dy = _dot(g_ref[...].astype(BF16), wot_ref[...])
        rg = _sigmoid(main_ref[:, O_R:O_R + 3072].astype(F32) + bg_ref[...])
        dyt = jnp.concatenate([dy, dy, dy], axis=1)
        dr = dyt * u_ref[...].astype(F32) * rg * (1.0 - rg)
        dmain_ref[:, O_R:O_R + 3072] = dr.astype(BF16)
        dbg_ref[...] += _colsum(dr)
        du = (dyt * rg).astype(BF16)
        du_ref[...] = du
        do_a = _dot(du[:, 0:1024], wbat_ref[...])
        do_c = _dot(du[:, 1024:2048], wbct_ref[...])
        do_m = _dot(du[:, 2048:3072], wbmt_ref[...])

        sil_a, dsil_a = _silu_and_grad(main_ref[:, O_GA:O_GA + 1024].astype(F32))
        ov = o_ref[...]
        d_o = do_a * sil_a
        do_ref[...] = d_o.astype(BF16)
        dmain_ref[:, O_GA:O_GA + 1024] = (do_a * ov * dsil_a).astype(BF16)
        prod = d_o * ov
        for h in range(N_HEADS):
            delta_ref[h] = jnp.sum(prod[:, h * HP:(h + 1) * HP], axis=-1, keepdims=True)

        z, z_up, z_dn = _conv_shifts(ccu_ref[:, 0:CONV_W].astype(F32), ccu_ref[:, CONV_W:].astype(F32), hp_ref, hn_ref, i, nt, tm)
        cv = cw_ref[0:1, :] * z_up + cw_ref[1:2, :] * z + cw_ref[2:3, :] * z_dn + cw_ref[3:4, :]
        sil_c, dsil_c = _silu_and_grad(main_ref[:, O_GC:O_GC + CONV_W].astype(F32))
        cb = main_ref[:, O_CB:O_CB + CONV_W].astype(F32)
        dmain_ref[:, O_CB:O_CB + CONV_W] = (do_c * cv * sil_c).astype(BF16)
        dmain_ref[:, O_GC:O_GC + CONV_W] = (do_c * cb * cv * dsil_c).astype(BF16)
        dcv_ref[...] = do_c * cb * sil_c

        sil_m, dsil_m = _silu_and_grad(main_ref[:, O_GM:O_GM + MEM_W].astype(F32))
        for h in range(MEM_HEADS):
            sl = slice(h * MEM_HD, (h + 1) * MEM_HD)
            qm = main_ref[:, O_QM + h * MEM_HD:O_QM + (h + 1) * MEM_HD].astype(F32)
            mk_h, mv_h = mk_ref[:, sl], mv_ref[:, sl]
            r, qhat, mq, p, pv = _mem_attend(qm, mqg_ref[...], mk_h, mv_h)
            dom = do_m[:, sl]
            dmain_ref[:, O_GM + h * MEM_HD:O_GM + (h + 1) * MEM_HD] = (dom * pv * dsil_m[:, sl]).astype(BF16)
            dpv = (dom * sil_m[:, sl]).astype(BF16)
            dp = _dot_nt(dpv, mv_h)
            ds = (p * (dp - jnp.sum(dp * p, axis=-1, keepdims=True)) * MEM_SCALE).astype(BF16)
            dmq = _dot(ds, mk_h)
            dmk_ref[:, sl] += _dot_tn(ds, mq)
            dmv_ref[:, sl] += _dot_tn(p.astype(BF16), dpv)
            dmqg_ref[...] += _colsum(dmq * qhat)
            dqm = _rms_bwd(dmq * mqg_ref[...], qhat, r, MEM_HD)
            dmain_ref[:, O_QM + h * MEM_HD:O_QM + (h + 1) * MEM_HD] = dqm.astype(BF16)

    row = lambda w: pl.BlockSpec((tm, w), lambda i: (i, 0))
    full = lambda a: pl.BlockSpec(a.shape, lambda i: (0,) * a.ndim)
    acc = lambda r, c: pl.BlockSpec((r, c), lambda i: (0, 0))
    hp, hn = _halo_specs(tm, S, 1024, 6, PROJ_HALO)
    return pl.pallas_call(
        body, name="merge_bwd", grid=(nt,),
        in_specs=[row(1024), row(6144), pl.BlockSpec((tm, 1024), lambda i: (i, 6)), hp, hn, row(1024), row(3072),
                  full(bg), full(convw), full(mqg), full(mk), full(mv), full(wot), full(wbat), full(wbct), full(wbmt)],
        out_specs=[row(6144), row(CONV_W), row(1024), pl.BlockSpec((N_HEADS, tm, 1), lambda i: (0, i, 0)), row(3072),
                   acc(1, 3072), acc(M, MEM_W), acc(M, MEM_W), acc(1, MEM_HD)],
        out_shape=[jax.ShapeDtypeStruct((S, 6144), BF16), jax.ShapeDtypeStruct((S, CONV_W), F32),
                   jax.ShapeDtypeStruct((S, 1024), BF16), jax.ShapeDtypeStruct((N_HEADS, S, 1), F32),
                   jax.ShapeDtypeStruct((S, 3072), BF16), jax.ShapeDtypeStruct((1, 3072), F32),
                   jax.ShapeDtypeStruct((M, MEM_W), F32), jax.ShapeDtypeStruct((M, MEM_W), F32),
                   jax.ShapeDtypeStruct((1, MEM_HD), F32)],
        compiler_params=_params(("arbitrary",)),
    )(g, proj, proj, proj, proj, o, u, bg, convw, mqg, mk, mv, wot, wbat, wbct, wbmt)


def _conv_bwd(dcv, proj, convw):
    S = dcv.shape[0]
    tm = _tile(S, 512)
    nt = S // tm

    def body(d_ref, dp_ref, dn_ref, ccu_ref, hp_ref, hn_ref, cw_ref, dccu_ref, dcw_ref):
        i = pl.program_id(0)

        @pl.when(i == 0)
        def _():
            dcw_ref[...] = jnp.zeros(dcw_ref.shape, F32)

        cc, cu = ccu_ref[:, 0:CONV_W].astype(F32), ccu_ref[:, CONV_W:].astype(F32)
        z, z_up, z_dn = _conv_shifts(cc, cu, hp_ref, hn_ref, i, nt, tm)
        d = d_ref[...]
        dprev = jnp.where(i == 0, 0.0, dp_ref[7:8, :])
        dnext = jnp.where(i == nt - 1, 0.0, dn_ref[0:1, :])
        row = lax.broadcasted_iota(jnp.int32, d.shape, 0)
        d_up = jnp.where(row == 0, dprev, pltpu.roll(d, 1, 0))
        d_dn = jnp.where(row == tm - 1, dnext, pltpu.roll(d, tm - 1, 0))
        dz = cw_ref[0:1, :] * d_dn + cw_ref[1:2, :] * d + cw_ref[2:3, :] * d_up
        dccu_ref[:, 0:CONV_W] = (dz * cu).astype(BF16)
        dccu_ref[:, CONV_W:] = (dz * cc).astype(BF16)
        dcw_ref[0:1, :] += _colsum(d * z_up)
        dcw_ref[1:2, :] += _colsum(d * z)
        dcw_ref[2:3, :] += _colsum(d * z_dn)
        dcw_ref[3:4, :] += _colsum(d)

    hp, hn = _halo_specs(tm, S, 1024, 6, PROJ_HALO)
    dp, dn = _halo_specs(tm, S, CONV_W, 0, F32_HALO)
    return pl.pallas_call(
        body, name="conv_bwd", grid=(nt,),
        in_specs=[pl.BlockSpec((tm, CONV_W), lambda i: (i, 0)), dp, dn,
                  pl.BlockSpec((tm, 1024), lambda i: (i, 6)), hp, hn,
                  pl.BlockSpec((8, CONV_W), lambda i: (0, 0))],
        out_specs=[pl.BlockSpec((tm, 1024), lambda i: (i, 0)), pl.BlockSpec((8, CONV_W), lambda i: (0, 0))],
        out_shape=[jax.ShapeDtypeStruct((S, 1024), BF16), jax.ShapeDtypeStruct((8, CONV_W), F32)],
        compiler_params=_params(("arbitrary",)),
    )(dcv, dcv, dcv, proj, proj, proj, convw)


def _attn_bwd(q, k, v, do, qt, dot, lse, delta):
    S = q.shape[0]
    tq, tk = _tile(S, 512), _tile(S, 2048)
    ni, nk = S // tq, S // tk

    def body(q_ref, do_ref, qt_ref, dot_ref, k_ref, v_ref, lse_ref, delta_ref, dq_ref, dkt_hbm, dvt_hbm,
             dq_s, dkt_s, dvt_s, sem):
        h, i = pl.program_id(0), pl.program_id(1)

        @pl.when(i == 0)
        def _():
            dkt_s[...] = jnp.zeros(dkt_s.shape, F32)
            dvt_s[...] = jnp.zeros(dvt_s.shape, F32)

        dq_s[...] = jnp.zeros(dq_s.shape, F32)
        qv, dov, qtv, dotv = q_ref[...], do_ref[...], qt_ref[...], dot_ref[...]
        lse_c, delta_c = lse_ref[0], delta_ref[0]

        def step(c, carry):
            cols = pl.ds(pl.multiple_of(c * tk, tk), tk)
            kc, vc = k_ref[cols, :], v_ref[cols, :]
            p = jnp.exp(_dot_nt(qv, kc) - lse_c)
            dp = _dot_nt(dov, vc)
            ds = (p * (dp - delta_c)).astype(BF16)
            dq_s[...] += _dot(ds, kc)
            dvt_s[:, cols] += _dot(dotv, p.astype(BF16))
            dkt_s[:, cols] += _dot(qtv, ds)
            return carry

        lax.fori_loop(0, nk, step, 0)
        dq_ref[...] = dq_s[...]

        @pl.when(i == ni - 1)
        def _():
            head = pl.ds(pl.multiple_of(h * HP, HP), HP)
            out_k = pltpu.make_async_copy(dkt_s, dkt_hbm.at[head, :], sem.at[0])
            out_v = pltpu.make_async_copy(dvt_s, dvt_hbm.at[head, :], sem.at[1])
            out_k.start()
            out_v.start()
            out_k.wait()
            out_v.wait()

    col = pl.BlockSpec((1, tq, 1), lambda h, i: (h, i, 0))
    blk = pl.BlockSpec((tq, HP), lambda h, i: (i, h))
    blkt = pl.BlockSpec((HP, tq), lambda h, i: (h, i))
    res = pl.BlockSpec((S, HP), lambda h, i: (0, h))
    whole = pl.BlockSpec(memory_space=pl.ANY)
    return pl.pallas_call(
        body, name="attn_bwd", grid=(N_HEADS, ni),
        in_specs=[blk, blk, blkt, blkt, res, res, col, col],
        out_specs=[blk, whole, whole],
        out_shape=[jax.ShapeDtypeStruct((S, N_HEADS * HP), F32), jax.ShapeDtypeStruct((N_HEADS * HP, S), F32),
                   jax.ShapeDtypeStruct((N_HEADS * HP, S), F32)],
        scratch_shapes=[pltpu.VMEM((tq, HP), F32), pltpu.VMEM((HP, S), F32), pltpu.VMEM((HP, S), F32),
                        pltpu.SemaphoreType.DMA((2,))],
        compiler_params=_params(("parallel", "arbitrary")),
    )(q, do, qt, dot, k, v, lse, delta)


def _mla_prep_bwd(proj, tabs, dq, dk, dv, qg, kvg, qhg, khg, wuq, wuk, wuqt, wukt, wuvt):
    S = proj.shape[0]
    tm = _tile(S, 512)

    def body(a_ref, c_ref, sa_ref, sb_ref, dq_ref, dk_ref, dv_ref, qg_ref, kvg_ref, qhg_ref, khg_ref,
             wuq_ref, wuk_ref, wuqt_ref, wukt_ref, wuvt_ref,
             da_ref, dqraw_ref, dkraw_ref, dqg_ref, dkvg_ref, dqhg_ref, dkhg_ref):
        @pl.when(pl.program_id(0) == 0)
        def _():
            dqg_ref[...] = jnp.zeros(dqg_ref.shape, F32)
            dkvg_ref[...] = jnp.zeros(dkvg_ref.shape, F32)
            dqhg_ref[...] = jnp.zeros(dqhg_ref.shape, F32)
            dkhg_ref[...] = jnp.zeros(dkhg_ref.shape, F32)

        ql = a_ref[:, 0:Q_RANK].astype(F32)
        kvl = a_ref[:, Q_RANK:Q_RANK + KV_RANK].astype(F32)
        kpe = a_ref[:, Q_RANK + KV_RANK:Q_RANK + KV_RANK + HP].astype(F32)
        rq, qhat = _rms_parts(ql)
        rkv, kvhat = _rms_parts(kvl)
        qraw = _dot((qhat * qg_ref[...]).astype(BF16), wuq_ref[...])
        kn = _dot((kvhat * kvg_ref[...]).astype(BF16), wuk_ref[...])
        c, sa, sb = c_ref[...], sa_ref[...], sb_ref[...]
        dkpe = jnp.zeros(kpe.shape, F32)
        dqhg = jnp.zeros((1, HP), F32)
        dkhg = jnp.zeros((1, HP), F32)
        for h in range(N_HEADS):
            sl = slice(h * HP, (h + 1) * HP)
            r, that = _rms_parts(qraw[:, sl], QK_DIM)
            dtn = _rope_t(dq_ref[:, sl], c, sa, sb) * SCALE
            dqhg = dqhg + _colsum(dtn * that)
            dqraw_ref[:, sl] = _rms_bwd(dtn * qhg_ref[...], that, r, QK_DIM).astype(BF16)
            r, that = _rms_parts(kn[:, sl] + kpe, QK_DIM)
            dtn = _rope_t(dk_ref[:, sl], c, sa, sb)
            dkhg = dkhg + _colsum(dtn * that)
            dkr = _rms_bwd(dtn * khg_ref[...], that, r, QK_DIM)
            dkraw_ref[:, sl] = dkr.astype(BF16)
            dkpe = dkpe + dkr
        dqhg_ref[...] += dqhg
        dkhg_ref[...] += dkhg
        dqn = _dot(dqraw_ref[...], wuqt_ref[...])
        dqg_ref[...] += _colsum(dqn * qhat)
        da_ref[:, 0:Q_RANK] = _rms_bwd(dqn * qg_ref[...], qhat, rq, Q_RANK).astype(BF16)
        dkvn = _dot(dkraw_ref[...], wukt_ref[...]) + _dot(dv_ref[...], wuvt_ref[...])
        dkvg_ref[...] += _colsum(dkvn * kvhat)
        da_ref[:, Q_RANK:Q_RANK + KV_RANK] = _rms_bwd(dkvn * kvg_ref[...], kvhat, rkv, KV_RANK).astype(BF16)
        da_ref[:, Q_RANK + KV_RANK:Q_RANK + KV_RANK + HP] = dkpe.astype(BF16)
        da_ref[:, Q_RANK + KV_RANK + HP:] = jnp.zeros((tm, 1024 - Q_RANK - KV_RANK - HP), BF16)

    row = lambda w: pl.BlockSpec((tm, w), lambda i: (i, 0))
    full = lambda a: pl.BlockSpec(a.shape, lambda i: (0,) * a.ndim)
    acc = lambda c: pl.BlockSpec((1, c), lambda i: (0, 0))
    return pl.pallas_call(
        body, name="mla_prep_bwd", grid=(S // tm,),
        in_specs=[pl.BlockSpec((tm, 1024), lambda i: (i, 7)), row(HP), row(HP), row(HP),
                  row(1024), row(1024), row(1024), full(qg), full(kvg), full(qhg), full(khg),
                  full(wuq), full(wuk), full(wuqt), full(wukt), full(wuvt)],
        out_specs=[row(1024), row(1024), row(1024), acc(Q_RANK), acc(KV_RANK), acc(HP), acc(HP)],
        out_shape=[jax.ShapeDtypeStruct((S, 1024), BF16)] * 3
        + [jax.ShapeDtypeStruct((1, Q_RANK), F32), jax.ShapeDtypeStruct((1, KV_RANK), F32),
           jax.ShapeDtypeStruct((1, HP), F32), jax.ShapeDtypeStruct((1, HP), F32)],
        compiler_params=_params(("arbitrary",)),
    )(proj, *tabs, dq, dk, dv, qg, kvg, qhg, khg, wuq, wuk, wuqt, wukt, wuvt)


def _inproj_bwd(dmain, dccu, dsega, wint, x, g, ng):
    S = x.shape[0]
    tm = _tile(S, 256)
    nm, nc = dmain.shape[1], dccu.shape[1]
    assert nm + nc + dsega.shape[1] == PW

    def body(dm_ref, dc_ref, da_ref, w_ref, x_ref, g_ref, ng_ref, dx_ref, dng_ref):
        @pl.when(pl.program_id(0) == 0)
        def _():
            dng_ref[...] = jnp.zeros(dng_ref.shape, F32)

        dh = (_dot(dm_ref[...], w_ref[0:nm, :]) + _dot(dc_ref[...], w_ref[nm:nm + nc, :])
              + _dot(da_ref[...], w_ref[nm + nc:PW, :]))
        r, xhat = _rms_parts(x_ref[...])
        dng_ref[...] += _colsum(dh * xhat)
        dx_ref[...] = g_ref[...] + _rms_bwd(dh * ng_ref[...], xhat, r, D_MODEL)

    row = lambda w: pl.BlockSpec((tm, w), lambda i: (i, 0))
    return pl.pallas_call(
        body, name="inproj_bwd", grid=(S // tm,),
        in_specs=[row(nm), row(nc), row(dsega.shape[1]), pl.BlockSpec((PW, D_MODEL), lambda i: (0, 0)),
                  row(D_MODEL), row(D_MODEL), pl.BlockSpec((1, D_MODEL), lambda i: (0, 0))],
        out_specs=[row(D_MODEL), pl.BlockSpec((1, D_MODEL), lambda i: (0, 0))],
        out_shape=[jax.ShapeDtypeStruct((S, D_MODEL), F32), jax.ShapeDtypeStruct((1, D_MODEL), F32)],
        compiler_params=_params(("arbitrary",)),
    )(dmain, dccu, dsega, wint, x, g, ng)


def _memkv_bwd(mem, mng, mkg, wmkv, wmkvt, dmk, dmv):
    M = mem.shape[0]

    def body(mem_ref, mng_ref, mkg_ref, w_ref, wt_ref, dmk_ref, dmv_ref, dw_ref, dmng_ref, dmkg_ref, d_s):
        r, mhat = _rms_parts(mem_ref[...])
        mn = (mhat * mng_ref[...]).astype(BF16)
        mkv = _dot(mn, w_ref[...])
        dmkg = jnp.zeros((1, MEM_HD), F32)
        for h in range(MEM_HEADS):
            sl = slice(h * MEM_HD, (h + 1) * MEM_HD)
            rk, khat = _rms_parts(mkv[:, 2 * MEM_HD * h:2 * MEM_HD * h + MEM_HD])
            dkn = dmk_ref[:, sl]
            dmkg = dmkg + _colsum(dkn * khat)
            d_s[:, 2 * MEM_HD * h:2 * MEM_HD * h + MEM_HD] = _rms_bwd(dkn * mkg_ref[...], khat, rk, MEM_HD).astype(BF16)
            d_s[:, 2 * MEM_HD * h + MEM_HD:2 * MEM_HD * (h + 1)] = dmv_ref[:, sl].astype(BF16)
        dmkg_ref[...] = dmkg
        dw_ref[...] = _dot_tn(mn, d_s[...])
        dmn = _dot(d_s[...], wt_ref[...])
        dmng_ref[...] = _colsum(dmn * mhat)

    return pl.pallas_call(
        body, name="memkv_bwd",
        out_shape=[jax.ShapeDtypeStruct((D_MODEL, 2 * MEM_W), F32), jax.ShapeDtypeStruct((1, D_MODEL), F32),
                   jax.ShapeDtypeStruct((1, MEM_HD), F32)],
        scratch_shapes=[pltpu.VMEM((M, 2 * MEM_W), BF16)],
        compiler_params=pltpu.CompilerParams(vmem_limit_bytes=VMEM_LIMIT_V7X),
    )(mem, mng, mkg, wmkv, wmkvt, dmk, dmv)


def _mm_tn(a, b, name, col0=0, ncols=None):
    S, M = a.shape
    N = b.shape[1] if ncols is None else ncols
    tm, tn, ts = _tile(M, 1024), _tile(N, 1024), _tile(S, 2048)
    assert col0 % tn == 0
    jb = col0 // tn

    def body(a_ref, b_ref, o_ref):
        @pl.when(pl.program_id(2) == 0)
        def _():
            o_ref[...] = jnp.zeros(o_ref.shape, F32)

        o_ref[...] += _dot_tn(a_ref[...].astype(BF16), b_ref[...].astype(BF16))

    return pl.pallas_call(
        body, name=name, grid=(M // tm, N // tn, S // ts),
        in_specs=[pl.BlockSpec((ts, tm), lambda i, j, k: (k, i)),
                  pl.BlockSpec((ts, tn), lambda i, j, k: (k, j + jb))],
        out_specs=pl.BlockSpec((tm, tn), lambda i, j, k: (i, j)),
        out_shape=jax.ShapeDtypeStruct((M, N), F32),
        compiler_params=_params(("parallel", "parallel", "arbitrary")),
    )(a, b)


def _adamw(w, g0, g1, m, v, name):
    R, C = w.shape
    tr = R
    for cand in (512, 256, 128, 64, 32, 16, 8):
        if R % cand == 0 and cand * C * 4 <= (1 << 20):
            tr = cand
            break
    c1 = 1.0 / (1.0 - ADAM_B1 ** ADAM_STEP)
    c2 = 1.0 / (1.0 - ADAM_B2 ** ADAM_STEP)

    def body(w_ref, g0_ref, g1_ref, m_ref, v_ref, g_ref, d_ref, nm_ref, nv_ref):
        g = g0_ref[...] + g1_ref[...]
        nm = ADAM_B1 * m_ref[...] + (1.0 - ADAM_B1) * g
        nv = ADAM_B2 * v_ref[...] + (1.0 - ADAM_B2) * (g * g)
        g_ref[...] = g
        nm_ref[...] = nm
        nv_ref[...] = nv
        d_ref[...] = -ADAM_LR * ((nm * c1) / (jnp.sqrt(nv * c2) + ADAM_EPS) + ADAM_WD * w_ref[...])

    blk = pl.BlockSpec((tr, C), lambda i: (i, 0))
    return pl.pallas_call(
        body, name=name, grid=(R // tr,),
        in_specs=[blk] * 5, out_specs=[blk] * 4,
        out_shape=[jax.ShapeDtypeStruct((R, C), F32)] * 4,
        compiler_params=_params(("parallel",)),
    )(w, g0, g1, m, v)


def _sum_slabs(a, name):
    K, R, C = a.shape
    tr = R
    for cand in (512, 256, 128, 64, 32, 16, 8):
        if R % cand == 0 and cand * C * 4 * K <= (4 << 20):
            tr = cand
            break

    def body(a_ref, o_ref):
        t = a_ref[0].astype(F32)
        for k in range(1, K):
            t = t + a_ref[k].astype(F32)
        o_ref[...] = t

    return pl.pallas_call(
        body, name=name, grid=(R // tr,),
        in_specs=[pl.BlockSpec((K, tr, C), lambda i: (0, i, 0))],
        out_specs=pl.BlockSpec((tr, C), lambda i: (i, 0)),
        out_shape=jax.ShapeDtypeStruct((R, C), F32),
        compiler_params=_params(("parallel",)),
    )(a)


MESH = pl.DeviceIdType.MESH
HBM = pl.BlockSpec(memory_space=pltpu.HBM)


def _chip_peer(x, y, c, k):
    return (x ^ (k >> 1), y ^ (k & 1), c)


def _gather_chips(arrs):
    n = len(arrs)
    half = arrs[0].shape[0] // 2
    assert all(a.shape[0] == 2 * half for a in arrs)

    def body(*refs):
        ins, outs = refs[:n], refs[n:2 * n]
        send1, recv1, send2, recv2, loc = refs[2 * n:]
        x, y, c = lax.axis_index("x"), lax.axis_index("y"), lax.axis_index("c")
        me = 2 * x + y
        mine, theirs = pl.ds(c * half, half), pl.ds((1 - c) * half, half)
        sibling = (x, y, 1 - c)
        waits = []
        for a in range(n):
            own = pltpu.make_async_copy(ins[a], outs[a].at[me], loc.at[a])
            own.start()
            waits.append(own.wait)

        def over_ici(a, k, src_chip):
            return pltpu.make_async_remote_copy(
                src_ref=ins[a].at[mine], dst_ref=outs[a].at[src_chip, mine], send_sem=send1.at[3 * a + k - 1],
                recv_sem=recv1.at[3 * a + k - 1], device_id=_chip_peer(x, y, c, k), device_id_type=MESH)

        def to_sibling(a, k, layers):
            block = outs[a].at[me ^ k, layers]
            return pltpu.make_async_remote_copy(
                src_ref=block, dst_ref=block, send_sem=send2.at[3 * a + k - 1], recv_sem=recv2.at[3 * a + k - 1],
                device_id=sibling, device_id_type=MESH)

        for a in range(n):
            for k in (1, 2, 3):
                cp = over_ici(a, k, me)
                cp.start()
                waits.append(cp.wait_send)
        for a in range(n):
            for k in (1, 2, 3):
                over_ici(a, k, me ^ k).wait_recv()
                cp = to_sibling(a, k, mine)
                cp.start()
                waits.append(cp.wait_send)
        for a in range(n):
            for k in (1, 2, 3):
                to_sibling(a, k, theirs).wait_recv()
        for w in waits:
            w()

    return pl.pallas_call(
        body, name="gather_weights",
        in_specs=[HBM] * n, out_specs=[HBM] * n,
        out_shape=[jax.ShapeDtypeStruct((4,) + a.shape, a.dtype) for a in arrs],
        scratch_shapes=[pltpu.SemaphoreType.DMA((3 * n,)), pltpu.SemaphoreType.DMA((3 * n,)),
                        pltpu.SemaphoreType.DMA((3 * n,)), pltpu.SemaphoreType.DMA((3 * n,)),
                        pltpu.SemaphoreType.DMA((n,))],
    )(*arrs)


def _scatter_chips(arrs, small):
    n = len(arrs)

    def body(*refs):
        ins, small_in = refs[:n], refs[n]
        outs, small_out = refs[n + 1:2 * n + 1], refs[2 * n + 1]
        send, recv, loc, ssend, srecv = refs[2 * n + 2:]
        x, y, c = lax.axis_index("x"), lax.axis_index("y"), lax.axis_index("c")
        me = 2 * x + y
        me8 = 4 * x + 2 * y + c
        copies = []
        for a in range(n):
            own = pltpu.make_async_copy(ins[a].at[me], outs[a].at[me], loc.at[a])
            own.start()
            copies.append(own)
        own = pltpu.make_async_copy(small_in, small_out.at[me8], loc.at[n])
        own.start()
        copies.append(own)
        for k in range(1, 8):
            cp = pltpu.make_async_remote_copy(
                src_ref=small_in, dst_ref=small_out.at[me8], send_sem=ssend.at[k - 1], recv_sem=srecv.at[k - 1],
                device_id=(x ^ (k >> 2), y ^ ((k >> 1) & 1), c ^ (k & 1)), device_id_type=MESH)
            cp.start()
            copies.append(cp)
        for a in range(n):
            for k in (1, 2, 3):
                cp = pltpu.make_async_remote_copy(
                    src_ref=ins[a].at[me ^ k], dst_ref=outs[a].at[me], send_sem=send.at[3 * a + k - 1],
                    recv_sem=recv.at[3 * a + k - 1], device_id=_chip_peer(x, y, c, k), device_id_type=MESH)
                cp.start()
                copies.append(cp)
        for cp in copies:
            cp.wait()

    return pl.pallas_call(
        body, name="scatter_grads",
        in_specs=[HBM] * (n + 1), out_specs=[HBM] * (n + 1),
        out_shape=[jax.ShapeDtypeStruct(a.shape, a.dtype) for a in arrs]
        + [jax.ShapeDtypeStruct((8,) + small.shape, small.dtype)],
        scratch_shapes=[pltpu.SemaphoreType.DMA((3 * n,)), pltpu.SemaphoreType.DMA((3 * n,)),
                        pltpu.SemaphoreType.DMA((n + 1,)), pltpu.SemaphoreType.DMA((7,)),
                        pltpu.SemaphoreType.DMA((7,))],
    )(*arrs, small)


def _swap_cores(arrs):
    n = len(arrs)

    def body(*refs):
        ins, outs = refs[:n], refs[n:2 * n]
        send, recv = refs[2 * n:]
        x, y, c = lax.axis_index("x"), lax.axis_index("y"), lax.axis_index("c")
        copies = []
        for a in range(n):
            cp = pltpu.make_async_remote_copy(
                src_ref=ins[a], dst_ref=outs[a], send_sem=send.at[a], recv_sem=recv.at[a],
                device_id=(x, y, 1 - c), device_id_type=MESH)
            cp.start()
            copies.append(cp)
        for cp in copies:
            cp.wait()

    return pl.pallas_call(
        body, name="swap_cores",
        in_specs=[HBM] * n, out_specs=[HBM] * n,
        out_shape=[jax.ShapeDtypeStruct(a.shape, a.dtype) for a in arrs],
        scratch_shapes=[pltpu.SemaphoreType.DMA((n,)), pltpu.SemaphoreType.DMA((n,))],
    )(*arrs)


def _pad_last(a, n):
    return jnp.pad(a, [(0, 0)] * (a.ndim - 1) + [(0, n - a.shape[-1])])


def _pad_w_in(w):
    lead = w.shape[:-1]
    seg = lambda a, b: w[..., a:b]
    ga = _pad_last(seg(2720, 3232).reshape(lead + (N_HEADS, V_DIM)), HP).reshape(lead + (1024,))
    kpe = jnp.pad(seg(640, 672), [(0, 0)] * len(lead) + [(NOPE_DIM, HP - QK_DIM)])
    zero = jnp.zeros(lead + (PW - 7936,), w.dtype)
    return jnp.concatenate(
        [seg(4256, 7328), ga, seg(672, 1184), seg(2208, 2720), seg(3232, 3744), seg(3744, 4256),
         seg(1184, 1696), seg(1696, 2208), seg(0, 384), seg(384, 640), kpe, zero], axis=-1)


def _unpad_w_in(w):
    lead = w.shape[:-1]
    seg = lambda a, n: w[..., a:a + n]
    ga = seg(O_GA, 1024).reshape(lead + (N_HEADS, HP))[..., :V_DIM].reshape(lead + (N_HEADS * V_DIM,))
    return jnp.concatenate(
        [seg(O_QL, 384), seg(O_KVL, 256), seg(O_KPE + NOPE_DIM, ROPE_DIM), seg(O_CB, 512), seg(O_CC, 512),
         seg(O_CU, 512), seg(O_QM, 512), ga, seg(O_GC, 512), seg(O_GM, 512), seg(O_R, 3072)], axis=-1)


def _cols_from_shards(g):
    _, L, R, C = g.shape
    return jnp.transpose(g, (1, 2, 0, 3)).reshape(L, R, 4 * C)


def _cols_to_shards(w):
    L, R, C4 = w.shape
    return jnp.transpose(w.reshape(L, R, 4, C4 // 4), (2, 0, 1, 3))


def _rows_from_shards(g):
    _, L, R, C = g.shape
    return jnp.transpose(g, (1, 0, 2, 3)).reshape(L, 4 * R, C)


def _rows_to_shards(w):
    L, R4, C = w.shape
    return jnp.transpose(w.reshape(L, 4, R4 // 4, C), (1, 0, 2, 3))


def _t(w):
    return jnp.swapaxes(w, -1, -2)


def _layer_fwd(x, mem, tabs, p):
    proj, xn = _inproj(x, p["norm_g"], p["w_in"])
    q, k, v, qn, kvn, kmax2 = _mla_prep(proj, tabs, p["q_norm_g"], p["kv_norm_g"], p["q_head_g"], p["k_head_g"],
                                        p["w_uq"], p["w_uk"], p["w_uv"])
    qt = _t(q)
    o, lse = _attn_fwd(qt, k, _t(v), kmax2)
    lse = lse.reshape(N_HEADS, x.shape[0], 1)
    mk, mv = _memkv(mem, p["mem_norm_g"], p["mem_k_g"], p["w_mkv"])
    x_new, oa, oc, om, u, y = _merge(proj, o, x, p["b_gate"], p["conv_wb"], p["mem_q_g"], mk, mv,
                                     p["w_br_attn"], p["w_br_conv"], p["w_br_mem"], p["w_out"])
    saved = dict(x=x, proj=proj, xn=xn, q=q, qt=qt, k=k, v=v, qn=qn, kvn=kvn, o=o, lse=lse, mk=mk, mv=mv,
                 oa=oa, oc=oc, om=om, u=u, y=y)
    return x_new, saved


def _layer_bwd(g, mem, tabs, p, s):
    S = g.shape[0]
    dmain, dcv, d_o, delta, du, dbg, dmk, dmv, dmqg = _merge_bwd(
        g, s["proj"], s["o"], s["u"], p["b_gate"], p["conv_wb"], p["mem_q_g"], s["mk"], s["mv"],
        p["w_out_t"], p["w_br_attn_t"], p["w_br_conv_t"], p["w_br_mem_t"])
    dccu, dconv = _conv_bwd(dcv, s["proj"], p["conv_wb"])
    dq, dkt, dvt = _attn_bwd(s["q"], s["k"], s["v"], d_o, s["qt"], _t(d_o), s["lse"], delta)
    dk, dv = _t(dkt), _t(dvt).astype(BF16)
    dsega, dqraw, dkraw, dqg, dkvg, dqhg, dkhg = _mla_prep_bwd(
        s["proj"], tabs, dq, dk, dv, p["q_norm_g"], p["kv_norm_g"], p["q_head_g"], p["k_head_g"],
        p["w_uq"], p["w_uk"], p["w_uq_t"], p["w_uk_t"], p["w_uv_t"])
    dx, dng = _inproj_bwd(dmain, dccu, dsega, p["w_in_t"], s["x"], g, p["norm_g"])
    dwmkv, dmng, dmkg = _memkv_bwd(mem, p["mem_norm_g"], p["mem_k_g"], p["w_mkv"], p["w_mkv_t"], dmk, dmv)
    grads = dict(
        norm_g=dng, b_gate=dbg, q_norm_g=dqg, kv_norm_g=dkvg, q_head_g=dqhg, k_head_g=dkhg,
        conv_wb=dconv, mem_norm_g=dmng, mem_q_g=dmqg, mem_k_g=dmkg, w_mkv=dwmkv,
        w_in=jnp.concatenate([_mm_tn(s["xn"], dmain, "grad_w_in"), _mm_tn(s["xn"], dccu, "grad_w_in_conv"),
                              _mm_tn(s["xn"], dsega, "grad_w_in_lat")], axis=1),
        w_uq=_mm_tn(s["qn"], dqraw, "grad_w_uq"),
        w_uk=_mm_tn(s["kvn"], dkraw, "grad_w_uk"),
        w_uv=_mm_tn(s["kvn"], dv, "grad_w_uv"),
        w_br_attn=_mm_tn(s["oa"], du, "grad_w_br_attn", 0, 1024),
        w_br_conv=_mm_tn(s["oc"], du, "grad_w_br_conv", 1024, 1024),
        w_br_mem=_mm_tn(s["om"], du, "grad_w_br_mem", 2048, 1024),
        w_out=_mm_tn(s["y"], g, "grad_w_out"),
    )
    return dx, grads


def _layer_params(full, l):
    p = {}
    w_in = _pad_w_in(full["w_in"][l])
    w_uq = _pad_last(full["w_uq"][l].reshape(Q_RANK, N_HEADS, QK_DIM), HP).reshape(Q_RANK, 1024)
    ukv = full["w_ukv"][l].reshape(KV_RANK, N_HEADS, NOPE_DIM + V_DIM)
    w_uk = _pad_last(ukv[..., :NOPE_DIM], HP).reshape(KV_RANK, 1024)
    w_uv = _pad_last(ukv[..., NOPE_DIM:], HP).reshape(KV_RANK, 1024)
    w_ba = jnp.pad(full["w_br_attn"][l].reshape(N_HEADS, V_DIM, D_MODEL), ((0, 0), (0, HP - V_DIM), (0, 0)))
    w_ba = w_ba.reshape(1024, D_MODEL)
    p.update(w_in=w_in, w_uq=w_uq, w_uk=w_uk, w_uv=w_uv, w_br_attn=w_ba, w_br_conv=full["w_br_conv"][l],
             w_br_mem=full["w_br_mem"][l], w_out=full["w_out"][l], w_mkv=full["w_mkv"][l])
    for n in ("w_in", "w_uq", "w_uk", "w_uv", "w_br_attn", "w_br_conv", "w_br_mem", "w_out", "w_mkv"):
        p[n + "_t"] = _t(p[n])
    for n in ("norm_g", "b_gate", "q_norm_g", "kv_norm_g", "mem_norm_g", "mem_q_g", "mem_k_g"):
        p[n] = full[n][l][None, :]
    p["q_head_g"] = _pad_last(full["q_head_g"][l][None, :], HP)
    p["k_head_g"] = _pad_last(full["k_head_g"][l][None, :], HP)
    p["conv_wb"] = jnp.concatenate(
        [full["conv_w"][l], full["conv_b"][l][None, :], jnp.zeros((4, CONV_W), F32)], axis=0)
    return p


def _local_step(x, mem, positions, full, target):
    S = x.shape[0]
    invf16 = ROPE_BASE ** (-jnp.arange(0, ROPE_DIM, 2, dtype=F32) / ROPE_DIM)
    invf = jnp.concatenate([jnp.zeros((NOPE_DIM,), F32), invf16, invf16, jnp.zeros((HP - QK_DIM,), F32)])[None, :]
    tabs = _rope_tables(jnp.broadcast_to(positions.reshape(S, 1), (S, HP)), invf)
    params = [_layer_params(full, l) for l in range(DEPTH)]
    saved = []
    h = x
    for l in range(DEPTH):
        h, s = _layer_fwd(h, mem, tabs, params[l])
        saved.append(s)
    g, loss_part = _loss_head(h, target)
    per_layer = [None] * DEPTH
    for l in reversed(range(DEPTH)):
        g, per_layer[l] = _layer_bwd(g, mem, tabs, params[l], saved[l])
    st = lambda n: jnp.stack([per_layer[l][n] for l in range(DEPTH)])
    out = {}
    out["w_in"] = _unpad_w_in(st("w_in"))
    out["w_uq"] = st("w_uq").reshape(DEPTH, Q_RANK, N_HEADS, HP)[..., :QK_DIM].reshape(DEPTH, Q_RANK, N_HEADS * QK_DIM)
    duk = st("w_uk").reshape(DEPTH, KV_RANK, N_HEADS, HP)[..., :NOPE_DIM]
    duv = st("w_uv").reshape(DEPTH, KV_RANK, N_HEADS, HP)[..., :V_DIM]
    out["w_ukv"] = jnp.concatenate([duk, duv], axis=-1).reshape(DEPTH, KV_RANK, 1024)
    out["w_br_attn"] = st("w_br_attn").reshape(DEPTH, N_HEADS, HP, D_MODEL)[:, :, :V_DIM].reshape(DEPTH, 512, D_MODEL)
    for n in ("w_br_conv", "w_br_mem", "w_out", "w_mkv"):
        out[n] = st(n)
    for n in ("norm_g", "b_gate", "q_norm_g", "kv_norm_g", "mem_norm_g", "mem_q_g", "mem_k_g"):
        out[n] = st(n)[:, 0, :]
    out["q_head_g"] = st("q_head_g")[:, 0, :QK_DIM]
    out["k_head_g"] = st("k_head_g")[:, 0, :QK_DIM]
    cwb = st("conv_wb")
    out["conv_w"] = cwb[:, 0:3, :]
    out["conv_b"] = cwb[:, 3, :]
    return loss_part, g, out


_COL_SHARDED = ("w_in", "w_uq", "w_ukv", "w_br_attn", "w_br_conv", "w_br_mem")
_ROW_SHARDED = ("w_mkv", "w_out")
_BIG = _COL_SHARDED + _ROW_SHARDED
_SMALL = ("norm_g", "b_gate", "q_norm_g", "kv_norm_g", "q_head_g", "k_head_g", "conv_w", "conv_b",
          "mem_norm_g", "mem_q_g", "mem_k_g")
_ORDER = ("norm_g", "w_in", "b_gate", "q_norm_g", "w_uq", "kv_norm_g", "w_ukv", "q_head_g", "k_head_g",
          "conv_w", "conv_b", "mem_norm_g", "w_mkv", "mem_q_g", "mem_k_g", "w_br_attn", "w_br_conv",
          "w_br_mem", "w_out")


def _pack_small(d, extra):
    flat = jnp.concatenate([d[n].reshape(-1) for n in _SMALL] + [extra.reshape(-1)])
    n = flat.shape[0]
    rows = -(-n // 1024) * 8
    return jnp.pad(flat, (0, rows * 128 - n)).reshape(rows, 128)


def _unpack_small(packed, like):
    flat = packed.reshape(-1)
    out, off = {}, 0
    for n in _SMALL:
        sz = int(np.prod(like[n].shape))
        out[n] = flat[off:off + sz].reshape(like[n].shape)
        off += sz
    return out, flat[off:]


def kernel(x, mem, positions, norm_g, w_in, b_gate, q_norm_g, w_uq, kv_norm_g, w_ukv, q_head_g, k_head_g, conv_w, conv_b, mem_norm_g, w_mkv, mem_q_g, mem_k_g, w_br_attn, w_br_conv, w_br_mem, w_out, loss_target, m_norm_g, m_w_in, m_b_gate, m_q_norm_g, m_w_uq, m_kv_norm_g, m_w_ukv, m_q_head_g, m_k_head_g, m_conv_w, m_conv_b, m_mem_norm_g, m_w_mkv, m_mem_q_g, m_mem_k_g, m_w_br_attn, m_w_br_conv, m_w_br_mem, m_w_out, v_norm_g, v_w_in, v_b_gate, v_q_norm_g, v_w_uq, v_kv_norm_g, v_w_ukv, v_q_head_g, v_k_head_g, v_conv_w, v_conv_b, v_mem_norm_g, v_w_mkv, v_mem_q_g, v_mem_k_g, v_w_br_attn, v_w_br_conv, v_w_br_mem, v_w_out):
    w = dict(norm_g=norm_g, w_in=w_in, b_gate=b_gate, q_norm_g=q_norm_g, w_uq=w_uq, kv_norm_g=kv_norm_g,
             w_ukv=w_ukv, q_head_g=q_head_g, k_head_g=k_head_g, conv_w=conv_w, conv_b=conv_b,
             mem_norm_g=mem_norm_g, w_mkv=w_mkv, mem_q_g=mem_q_g, mem_k_g=mem_k_g, w_br_attn=w_br_attn,
             w_br_conv=w_br_conv, w_br_mem=w_br_mem, w_out=w_out)
    m = dict(norm_g=m_norm_g, w_in=m_w_in, b_gate=m_b_gate, q_norm_g=m_q_norm_g, w_uq=m_w_uq,
             kv_norm_g=m_kv_norm_g, w_ukv=m_w_ukv, q_head_g=m_q_head_g, k_head_g=m_k_head_g, conv_w=m_conv_w,
             conv_b=m_conv_b, mem_norm_g=m_mem_norm_g, w_mkv=m_w_mkv, mem_q_g=m_mem_q_g, mem_k_g=m_mem_k_g,
             w_br_attn=m_w_br_attn, w_br_conv=m_w_br_conv, w_br_mem=m_w_br_mem, w_out=m_w_out)
    v = dict(norm_g=v_norm_g, w_in=v_w_in, b_gate=v_b_gate, q_norm_g=v_q_norm_g, w_uq=v_w_uq,
             kv_norm_g=v_kv_norm_g, w_ukv=v_w_ukv, q_head_g=v_q_head_g, k_head_g=v_k_head_g, conv_w=v_conv_w,
             conv_b=v_conv_b, mem_norm_g=v_mem_norm_g, w_mkv=v_w_mkv, mem_q_g=v_mem_q_g, mem_k_g=v_mem_k_g,
             w_br_attn=v_w_br_attn, w_br_conv=v_w_br_conv, w_br_mem=v_w_br_mem, w_out=v_w_out)
    chip = 2 * lax.axis_index("x") + lax.axis_index("y")

    gathered = _gather_chips([w[n].astype(BF16) for n in _BIG] + [w["conv_w"]])
    full = {n: w[n] for n in _SMALL}
    for n, gth in zip(_BIG, gathered):
        full[n] = _cols_from_shards(gth) if n in _COL_SHARDED else _rows_from_shards(gth)
    full["conv_w"] = _cols_from_shards(gathered[-1])

    loss_part, grad_x, grads = _local_step(x[0], mem[0], positions[0], full, loss_target[0])

    to_owner = [(_cols_to_shards(grads[n]) if n in _COL_SHARDED else _rows_to_shards(grads[n])).astype(BF16)
                for n in _BIG]
    loss_vec = jnp.zeros((128,), F32).at[0].set(0.5 / D_MODEL * jnp.sum(loss_part))
    small = _pack_small(grads, loss_vec)
    scattered = _scatter_chips(to_owner, small)
    small_sum = _sum_slabs(scattered[-1], "sum_small")
    partial = []
    for n, slabs in zip(_BIG, scattered[:-1]):
        sh = slabs.shape
        partial.append(_sum_slabs(slabs.reshape(4, sh[1] * sh[2], sh[3]), "sum_" + n))
    other = _swap_cores(partial)

    small_g, tail = _unpack_small(small_sum, {n: (grads[n]) for n in _SMALL})
    loss = tail[0]
    small_g["conv_w"] = lax.dynamic_slice_in_dim(small_g["conv_w"], chip * (CONV_W // 4), CONV_W // 4, axis=2)

    outs_g, outs_d, outs_m, outs_v = {}, {}, {}, {}
    for n, p0, p1 in zip(_BIG, partial, other):
        shape = w[n].shape
        flat = lambda t: t.reshape(p0.shape)
        g_, d_, m_, v_ = _adamw(flat(w[n]), p0, p1, flat(m[n]), flat(v[n]), "adamw_" + n)
        outs_g[n], outs_d[n], outs_m[n], outs_v[n] = (t.reshape(shape) for t in (g_, d_, m_, v_))
    zero_small = jnp.zeros_like(small_sum)
    pk = lambda d: _pack_small(d, jnp.zeros((128,), F32))
    g_, d_, m_, v_ = _adamw(pk(w), _pack_small(small_g, jnp.zeros((128,), F32)), zero_small, pk(m), pk(v),
                            "adamw_small")
    like = {n: w[n] for n in _SMALL}
    for dst, packed in ((outs_g, g_), (outs_d, d_), (outs_m, m_), (outs_v, v_)):
        dst.update(_unpack_small(packed, like)[0])

    return (loss, grad_x[None], *[outs_g[n] for n in _ORDER], *[outs_d[n] for n in _ORDER],
            *[outs_m[n] for n in _ORDER], *[outs_v[n] for n in _ORDER])
```

```python
import functools

import numpy as np
import jax
import jax.numpy as jnp
from jax import lax
from jax.experimental import pallas as pl
from jax.experimental.pallas import tpu as pltpu

F32 = jnp.float32
BF16 = jnp.bfloat16

D_MODEL = 1024
DEPTH = 4
N_HEADS = 8
QK_DIM = 96
NOPE_DIM = 64
ROPE_DIM = 32
V_DIM = 64
Q_RANK = 384
KV_RANK = 256
CONV_W = 512
MEM_HEADS = 4
MEM_HD = 128
MEM_W = 512
IN_WIDTH = 7328
PW = 8192
HP = 128
PROJ_HALO = 16
F32_HALO = 8
EPS = 1e-6
ROPE_BASE = 10000.0
SCALE = QK_DIM ** -0.5
MEM_SCALE = MEM_HD ** -0.5

ADAM_LR = 0.001
ADAM_B1 = 0.9
ADAM_B2 = 0.999
ADAM_EPS = 1e-08
ADAM_WD = 0.01
ADAM_STEP = 10

VMEM_LIMIT_V7X = 56 * 1024 * 1024

O_R, O_GA, O_CB, O_QM, O_GC, O_GM, O_CC, O_CU, O_QL, O_KVL, O_KPE = (
    0, 3072, 4096, 4608, 5120, 5632, 6144, 6656, 7168, 7552, 7808)


def _params(sem, vmem=VMEM_LIMIT_V7X):
    return pltpu.CompilerParams(dimension_semantics=sem, vmem_limit_bytes=vmem)


def _sigmoid(t):
    return 0.5 * jnp.tanh(0.5 * t) + 0.5


def _silu_and_grad(g):
    sg = _sigmoid(g)
    return g * sg, sg * (1.0 + g * (1.0 - sg))


def _rms(t, g, n=None):
    n = t.shape[-1] if n is None else n
    r = lax.rsqrt(jnp.sum(t * t, axis=-1, keepdims=True) * (1.0 / n) + EPS)
    return (t * r) * g


def _rms_parts(t, n=None):
    n = t.shape[-1] if n is None else n
    r = lax.rsqrt(jnp.sum(t * t, axis=-1, keepdims=True) * (1.0 / n) + EPS)
    return r, t * r


def _rms_bwd(dhat, hat, r, n):
    return r * (dhat - hat * (jnp.sum(dhat * hat, axis=-1, keepdims=True) * (1.0 / n)))


def _rope(t, c, sa, sb):
    return t * c + pltpu.roll(t, HP - 16, 1) * sa + pltpu.roll(t, 16, 1) * sb


def _rope_t(d, c, sa, sb):
    return d * c + pltpu.roll(d * sa, 16, 1) + pltpu.roll(d * sb, HP - 16, 1)


def _dot(a, b):
    return jnp.dot(a, b, preferred_element_type=F32)


def _dot_nt(a, b):
    return lax.dot_general(a, b, (((1,), (1,)), ((), ())), preferred_element_type=F32)


def _dot_tn(a, b):
    return lax.dot_general(a, b, (((0,), (0,)), ((), ())), preferred_element_type=F32)


def _colsum(t):
    return jnp.sum(t, axis=0, keepdims=True)


def _tile(n, t):
    t = min(n, t)
    assert n % t == 0, (n, t)
    return t


def _rope_tables(pos_b, invf):
    S = pos_b.shape[0]
    tm = _tile(S, 1024)

    def body(pos_ref, invf_ref, c_ref, sa_ref, sb_ref):
        ang = pos_ref[...].astype(F32) * invf_ref[...]
        lane = lax.broadcasted_iota(jnp.int32, ang.shape, 1)
        cs = jnp.cos(ang)
        sn = jnp.sin(ang)
        c_ref[...] = jnp.where(lane < NOPE_DIM, 1.0, jnp.where(lane < QK_DIM, cs, 0.0))
        sa_ref[...] = jnp.where((lane >= NOPE_DIM) & (lane < NOPE_DIM + 16), -sn, 0.0)
        sb_ref[...] = jnp.where((lane >= NOPE_DIM + 16) & (lane < QK_DIM), sn, 0.0)

    blk = pl.BlockSpec((tm, HP), lambda i: (i, 0))
    return pl.pallas_call(
        body, name="rope_tables", grid=(S // tm,),
        in_specs=[blk, pl.BlockSpec((1, HP), lambda i: (0, 0))],
        out_specs=[blk, blk, blk],
        out_shape=[jax.ShapeDtypeStruct((S, HP), F32)] * 3,
        compiler_params=_params(("parallel",)),
    )(pos_b, invf)


MESH = pl.DeviceIdType.MESH
HBM = pl.BlockSpec(memory_space=pltpu.HBM)


def _chip_peer(x, y, c, k):
    return (x ^ (k >> 1), y ^ (k & 1), c)


def _chip_copies(ins, outs, send, recv, loc, gather):
    x, y, c = lax.axis_index("x"), lax.axis_index("y"), lax.axis_index("c")
    me = 2 * x + y
    cps = []
    for a in range(len(ins)):
        cps.append(pltpu.make_async_copy(ins[a] if gather else ins[a].at[me], outs[a].at[me], loc.at[a]))
        for k in (1, 2, 3):
            cps.append(pltpu.make_async_remote_copy(
                src_ref=ins[a] if gather else ins[a].at[me ^ k], dst_ref=outs[a].at[me],
                send_sem=send.at[3 * a + k - 1], recv_sem=recv.at[3 * a + k - 1],
                device_id=_chip_peer(x, y, c, k), device_id_type=MESH))
    return cps


def _carried(body, n_in, n_out, carry, gather):
    n = len(carry)

    def wrapped(*refs):
        ins, cin = refs[:n_in], refs[n_in:n_in + n]
        outs, cout = refs[n_in + n:n_in + n + n_out], refs[n_in + n + n_out:n_in + 2 * n + n_out]
        send, recv, loc = refs[n_in + 2 * n + n_out:]
        i = pl.program_id(0)

        @pl.when(i == 0)
        def _():
            for cp in _chip_copies(cin, cout, send, recv, loc, gather):
                cp.start()

        body(*ins, *outs)

        @pl.when(i == pl.num_programs(0) - 1)
        def _():
            for cp in _chip_copies(cin, cout, send, recv, loc, gather):
                cp.wait()

    specs = dict(
        in_specs=[HBM] * n, out_specs=[HBM] * n,
        out_shape=[jax.ShapeDtypeStruct(((4,) + a.shape) if gather else a.shape, a.dtype) for a in carry],
        scratch_shapes=[pltpu.SemaphoreType.DMA((3 * n,)), pltpu.SemaphoreType.DMA((3 * n,)),
                        pltpu.SemaphoreType.DMA((n,))])
    return wrapped, specs


def _inproj(x, g, w, carry=()):
    S = x.shape[0]
    tm = _tile(S, 256)

    def body(x_ref, g_ref, w_ref, proj_ref, xn_ref):
        h = _rms(x_ref[...], g_ref[...]).astype(BF16)
        xn_ref[...] = h
        proj_ref[...] = _dot(h, w_ref[...]).astype(BF16)

    row = lambda n: pl.BlockSpec((tm, n), lambda i: (i, 0))
    in_specs = [row(D_MODEL), pl.BlockSpec((1, D_MODEL), lambda i: (0, 0)), pl.BlockSpec((D_MODEL, PW), lambda i: (0, 0))]
    out_specs = [row(PW), row(D_MODEL)]
    out_shape = [jax.ShapeDtypeStruct((S, PW), BF16), jax.ShapeDtypeStruct((S, D_MODEL), BF16)]
    if not carry:
        proj, xn = pl.pallas_call(
            body, name="inproj", grid=(S // tm,), in_specs=in_specs, out_specs=out_specs, out_shape=out_shape,
            compiler_params=_params(("parallel",)))(x, g, w)
        return proj, xn, []
    wrapped, extra = _carried(body, 3, 2, carry, gather=True)
    res = pl.pallas_call(
        wrapped, name="inproj_gather", grid=(S // tm,), in_specs=in_specs + extra["in_specs"],
        out_specs=out_specs + extra["out_specs"], out_shape=out_shape + extra["out_shape"],
        scratch_shapes=extra["scratch_shapes"], compiler_params=_params(("arbitrary",)))(x, g, w, *carry)
    return res[0], res[1], list(res[2:])


def _mla_prep(proj, tabs, qg, kvg, qhg, khg, wuq, wuk, wuv):
    S = proj.shape[0]
    tm = _tile(S, 512)

    def body(a_ref, c_ref, sa_ref, sb_ref, qg_ref, kvg_ref, qhg_ref, khg_ref, wuq_ref, wuk_ref, wuv_ref,
             q_ref, k_ref, v_ref, qn_ref, kvn_ref, kmax_ref):
        @pl.when(pl.program_id(0) == 0)
        def _():
            kmax_ref[...] = jnp.zeros(kmax_ref.shape, F32)

        ql = a_ref[:, 0:Q_RANK].astype(F32)
        kvl = a_ref[:, Q_RANK:Q_RANK + KV_RANK].astype(F32)
        kpe = a_ref[:, Q_RANK + KV_RANK:Q_RANK + KV_RANK + HP].astype(F32)
        qn = _rms(ql, qg_ref[...]).astype(BF16)
        kvn = _rms(kvl, kvg_ref[...]).astype(BF16)
        qn_ref[...] = qn
        kvn_ref[...] = kvn
        qraw = _dot(qn, wuq_ref[...])
        kn = _dot(kvn, wuk_ref[...])
        v_ref[...] = _dot(kvn, wuv_ref[...]).astype(BF16)
        c, sa, sb = c_ref[...], sa_ref[...], sb_ref[...]
        for h in range(N_HEADS):
            sl = slice(h * HP, (h + 1) * HP)
            tq = _rms(qraw[:, sl], qhg_ref[...], QK_DIM)
            q_ref[:, sl] = (_rope(tq, c, sa, sb) * SCALE).astype(BF16)
            tk = _rms(kn[:, sl] + kpe, khg_ref[...], QK_DIM)
            kb = _rope(tk, c, sa, sb).astype(BF16)
            k_ref[:, sl] = kb
            kf = kb.astype(F32)
            norm2 = jnp.max(jnp.sum(kf * kf, axis=-1, keepdims=True), axis=0, keepdims=True)
            kmax_ref[:, sl] = jnp.maximum(kmax_ref[:, sl], norm2)

    row = lambda w: pl.BlockSpec((tm, w), lambda i: (i, 0))
    full = lambda a: pl.BlockSpec(a.shape, lambda i: (0,) * a.ndim)
    return pl.pallas_call(
        body, name="mla_prep", grid=(S // tm,),
        in_specs=[pl.BlockSpec((tm, 1024), lambda i: (i, 7)), row(HP), row(HP), row(HP),
                  full(qg), full(kvg), full(qhg), full(khg), full(wuq), full(wuk), full(wuv)],
        out_specs=[row(1024), row(1024), row(1024), row(Q_RANK), row(KV_RANK),
                   pl.BlockSpec((1, 1024), lambda i: (0, 0))],
        out_shape=[jax.ShapeDtypeStruct((S, 1024), BF16)] * 3
        + [jax.ShapeDtypeStruct((S, Q_RANK), BF16), jax.ShapeDtypeStruct((S, KV_RANK), BF16),
           jax.ShapeDtypeStruct((1, 1024), F32)],
        compiler_params=_params(("arbitrary",)),
    )(proj, *tabs, qg, kvg, qhg, khg, wuq, wuk, wuv)


SAFE_SCORE_BOUND = 30.0


def _attn_fwd(qt, k, vt, kmax2):
    S = k.shape[0]
    tq, tk = _tile(S, 1024), _tile(S, 2048)
    nk = S // tk

    def body(qt_ref, k_ref, vt_ref, kmax_ref, o_ref, lse_ref, m_s, l_s, acc_s):
        qtv = qt_ref[...]
        qf = qtv.astype(F32)
        bound = jnp.sqrt(jnp.sum(qf * qf, axis=0, keepdims=True) * kmax_ref[0:1, 0:1]) * 1.001 + 1e-6
        safe = jnp.max(bound) <= SAFE_SCORE_BOUND
        l_s[...] = jnp.zeros(l_s.shape, F32)
        acc_s[...] = jnp.zeros(acc_s.shape, F32)

        def keys(c):
            return pl.ds(pl.multiple_of(c * tk, tk), tk)

        @pl.when(safe)
        def _():
            def step(c, carry):
                pt = jnp.exp(_dot(k_ref[keys(c), :], qtv) - bound)
                l_s[...] += jnp.sum(pt, axis=0, keepdims=True)
                acc_s[...] += _dot(vt_ref[:, keys(c)], pt.astype(BF16))
                return carry

            lax.fori_loop(0, nk, step, 0)
            m_s[...] = bound

        @pl.when(jnp.logical_not(safe))
        def _():
            m_s[...] = jnp.full(m_s.shape, -jnp.inf, F32)

            def step(c, carry):
                st = _dot(k_ref[keys(c), :], qtv)
                m_prev = m_s[...]
                m_new = jnp.maximum(m_prev, jnp.max(st, axis=0, keepdims=True))
                alpha = jnp.exp(m_prev - m_new)
                pt = jnp.exp(st - m_new)
                l_s[...] = alpha * l_s[...] + jnp.sum(pt, axis=0, keepdims=True)
                acc_s[...] = alpha * acc_s[...] + _dot(vt_ref[:, keys(c)], pt.astype(BF16))
                m_s[...] = m_new
                return carry

            lax.fori_loop(0, nk, step, 0)

        o_ref[...] = (acc_s[...] / l_s[...]).T
        lse_ref[0, 0] = m_s[...] + jnp.log(l_s[...])

    return pl.pallas_call(
        body, name="attn_fwd", grid=(N_HEADS, S // tq),
        in_specs=[pl.BlockSpec((HP, tq), lambda h, i: (h, i)),
                  pl.BlockSpec((S, HP), lambda h, i: (0, h)),
                  pl.BlockSpec((HP, S), lambda h, i: (h, 0)),
                  pl.BlockSpec((1, HP), lambda h, i: (0, h))],
        out_specs=[pl.BlockSpec((tq, HP), lambda h, i: (i, h)),
                   pl.BlockSpec((1, 1, 1, tq), lambda h, i: (h, i, 0, 0))],
        out_shape=[jax.ShapeDtypeStruct((S, N_HEADS * HP), F32),
                   jax.ShapeDtypeStruct((N_HEADS, S // tq, 1, tq), F32)],
        scratch_shapes=[pltpu.VMEM((1, tq), F32), pltpu.VMEM((1, tq), F32), pltpu.VMEM((HP, tq), F32)],
        compiler_params=_params(("parallel", "parallel")),
    )(qt, k, vt, kmax2)


def _memkv(mem, mng, mkg, wmkv):
    M = mem.shape[0]

    def body(mem_ref, mng_ref, mkg_ref, w_ref, mk_ref, mv_ref):
        mn = _rms(mem_ref[...], mng_ref[...]).astype(BF16)
        mkv = _dot(mn, w_ref[...])
        for h in range(MEM_HEADS):
            kraw = mkv[:, 2 * MEM_HD * h:2 * MEM_HD * h + MEM_HD]
            mk_ref[:, MEM_HD * h:MEM_HD * (h + 1)] = _rms(kraw, mkg_ref[...]).astype(BF16)
            mv_ref[:, MEM_HD * h:MEM_HD * (h + 1)] = mkv[:, 2 * MEM_HD * h + MEM_HD:2 * MEM_HD * (h + 1)].astype(BF16)

    return pl.pallas_call(
        body, name="memkv",
        out_shape=[jax.ShapeDtypeStruct((M, MEM_W), BF16)] * 2,
        compiler_params=pltpu.CompilerParams(vmem_limit_bytes=VMEM_LIMIT_V7X),
    )(mem, mng, mkg, wmkv)


def _conv_shifts(cc, cu, hp_ref, hn_ref, i, n_tiles, tm):
    z = cc * cu
    last = PROJ_HALO - 1
    zp = hp_ref[last:last + 1, 0:CONV_W].astype(F32) * hp_ref[last:last + 1, CONV_W:2 * CONV_W].astype(F32)
    zn = hn_ref[0:1, 0:CONV_W].astype(F32) * hn_ref[0:1, CONV_W:2 * CONV_W].astype(F32)
    zp = jnp.where(i == 0, 0.0, zp)
    zn = jnp.where(i == n_tiles - 1, 0.0, zn)
    row = lax.broadcasted_iota(jnp.int32, z.shape, 0)
    z_up = jnp.where(row == 0, zp, pltpu.roll(z, 1, 0))
    z_dn = jnp.where(row == tm - 1, zn, pltpu.roll(z, tm - 1, 0))
    return z, z_up, z_dn


def _halo_specs(tm, S, width, col, rows):
    per = tm // rows
    prev = pl.BlockSpec((rows, width), lambda i: (jnp.maximum(i * per - 1, 0), col))
    nxt = pl.BlockSpec((rows, width), lambda i: (jnp.minimum((i + 1) * per, S // rows - 1), col))
    return prev, nxt


def _mem_attend(qm, mqg, mk_h, mv_h):
    r, qhat = _rms_parts(qm)
    mq = (qhat * mqg).astype(BF16)
    s = _dot_nt(mq, mk_h) * MEM_SCALE
    e = jnp.exp(s - jnp.max(s, axis=-1, keepdims=True))
    p = e / jnp.sum(e, axis=-1, keepdims=True)
    pv = _dot(p.astype(BF16), mv_h)
    return r, qhat, mq, p, pv


def _merge(proj, o, x, bg, convw, mqg, mk, mv, wba, wbc, wbm, wo):
    S = x.shape[0]
    tm = _tile(S, 512)
    nt = S // tm

    def body(main_ref, ccu_ref, hp_ref, hn_ref, o_ref, x_ref, bg_ref, cw_ref, mqg_ref, mk_ref, mv_ref,
             wba_ref, wbc_ref, wbm_ref, wo_ref, xn_ref, oa_ref, oc_ref, om_ref, u_ref, y_ref):
        i = pl.program_id(0)
        sil_a, _ = _silu_and_grad(main_ref[:, O_GA:O_GA + 1024].astype(F32))
        oa = (o_ref[...] * sil_a).astype(BF16)
        oa_ref[...] = oa
        z, z_up, z_dn = _conv_shifts(ccu_ref[:, 0:CONV_W].astype(F32), ccu_ref[:, CONV_W:].astype(F32), hp_ref, hn_ref, i, nt, tm)
        cv = cw_ref[0:1, :] * z_up + cw_ref[1:2, :] * z + cw_ref[2:3, :] * z_dn + cw_ref[3:4, :]
        sil_c, _ = _silu_and_grad(main_ref[:, O_GC:O_GC + CONV_W].astype(F32))
        oc = (main_ref[:, O_CB:O_CB + CONV_W].astype(F32) * cv * sil_c).astype(BF16)
        oc_ref[...] = oc
        sil_m, _ = _silu_and_grad(main_ref[:, O_GM:O_GM + MEM_W].astype(F32))
        for h in range(MEM_HEADS):
            sl = slice(h * MEM_HD, (h + 1) * MEM_HD)
            qm = main_ref[:, O_QM + h * MEM_HD:O_QM + (h + 1) * MEM_HD].astype(F32)
            pv = _mem_attend(qm, mqg_ref[...], mk_ref[:, sl], mv_ref[:, sl])[4]
            om_ref[:, sl] = (pv * sil_m[:, sl]).astype(BF16)
        ua = _dot(oa, wba_ref[...])
        uc = _dot(oc, wbc_ref[...])
        um = _dot(om_ref[...], wbm_ref[...])
        u_ref[:, 0:1024] = ua.astype(BF16)
        u_ref[:, 1024:2048] = uc.astype(BF16)
        u_ref[:, 2048:3072] = um.astype(BF16)
        rg = _sigmoid(main_ref[:, O_R:O_R + 3072].astype(F32) + bg_ref[...])
        y = (rg[:, 0:1024] * ua + rg[:, 1024:2048] * uc + rg[:, 2048:3072] * um).astype(BF16)
        y_ref[...] = y
        xn_ref[...] = x_ref[...] + _dot(y, wo_ref[...])

    row = lambda w: pl.BlockSpec((tm, w), lambda i: (i, 0))
    full = lambda a: pl.BlockSpec(a.shape, lambda i: (0,) * a.ndim)
    hp, hn = _halo_specs(tm, S, 1024, 6, PROJ_HALO)
    return pl.pallas_call(
        body, name="merge", grid=(nt,),
        in_specs=[row(6144), pl.BlockSpec((tm, 1024), lambda i: (i, 6)), hp, hn, row(1024), row(1024),
                  full(bg), full(convw), full(mqg), full(mk), full(mv), full(wba), full(wbc), full(wbm), full(wo)],
        out_specs=[row(1024), row(1024), row(CONV_W), row(MEM_W), row(3072), row(1024)],
        out_shape=[jax.ShapeDtypeStruct((S, 1024), F32), jax.ShapeDtypeStruct((S, 1024), BF16),
                   jax.ShapeDtypeStruct((S, CONV_W), BF16), jax.ShapeDtypeStruct((S, MEM_W), BF16),
                   jax.ShapeDtypeStruct((S, 3072), BF16), jax.ShapeDtypeStruct((S, 1024), BF16)],
        compiler_params=_params(("parallel",)),
    )(proj, proj, proj, proj, o, x, bg, convw, mqg, mk, mv, wba, wbc, wbm, wo)


def _loss_head(xf, tgt):
    S = xf.shape[0]
    tm = _tile(S, 1024)

    def body(x_ref, t_ref, g_ref, acc_ref):
        @pl.when(pl.program_id(0) == 0)
        def _():
            acc_ref[...] = jnp.zeros(acc_ref.shape, F32)

        e = x_ref[...] - t_ref[...]
        g_ref[...] = e * (1.0 / D_MODEL)
        part = jnp.sum((e * e).reshape(tm // 8, 8, D_MODEL), axis=0)
        tot = part[:, 0:128]
        for k in range(1, D_MODEL // 128):
            tot = tot + part[:, 128 * k:128 * (k + 1)]
        acc_ref[...] += tot

    row = pl.BlockSpec((tm, D_MODEL), lambda i: (i, 0))
    return pl.pallas_call(
        body, name="loss_head", grid=(S // tm,),
        in_specs=[row, row],
        out_specs=[row, pl.BlockSpec((8, 128), lambda i: (0, 0))],
        out_shape=[jax.ShapeDtypeStruct((S, D_MODEL), F32), jax.ShapeDtypeStruct((8, 128), F32)],
        compiler_params=_params(("arbitrary",)),
    )(xf, tgt)


def _merge_bwd(g, proj, o, u, bg, convw, mqg, mk, mv, wot, wbat, wbct, wbmt):
    S = g.shape[0]
    tm = _tile(S, 256)
    nt = S // tm
    M = mk.shape[0]

    def body(g_ref, main_ref, ccu_ref, hp_ref, hn_ref, o_ref, u_ref, bg_ref, cw_ref, mqg_ref, mk_ref, mv_ref,
             wot_ref, wbat_ref, wbct_ref, wbmt_ref,
             dmain_ref, dcv_ref, do_ref, delta_ref, du_ref, dbg_ref, dmk_ref, dmv_ref, dmqg_ref):
        i = pl.program_id(0)

        @pl.when(i == 0)
        def _():
            dbg_ref[...] = jnp.zeros(dbg_ref.shape, F32)
            dmk_ref[...] = jnp.zeros(dmk_ref.shape, F32)
            dmv_ref[...] = jnp.zeros(dmv_ref.shape, F32)
            dmqg_ref[...] = jnp.zeros(dmqg_ref.shape, F32)

        dy = _dot(g_ref[...].astype(BF16), wot_ref[...])
        rg = _sigmoid(main_ref[:, O_R:O_R + 3072].astype(F32) + bg_ref[...])
        dyt = jnp.concatenate([dy, dy, dy], axis=1)
        dr = dyt * u_ref[...].astype(F32) * rg * (1.0 - rg)
        dmain_ref[:, O_R:O_R + 3072] = dr.astype(BF16)
        dbg_ref[...] += _colsum(dr)
        du = (dyt * rg).astype(BF16)
        du_ref[...] = du
        do_a = _dot(du[:, 0:1024], wbat_ref[...])
        do_c = _dot(du[:, 1024:2048], wbct_ref[...])
        do_m = _dot(du[:, 2048:3072], wbmt_ref[...])

        sil_a, dsil_a = _silu_and_grad(main_ref[:, O_GA:O_GA + 1024].astype(F32))
        ov = o_ref[...]
        d_o = do_a * sil_a
        do_ref[...] = d_o.astype(BF16)
        dmain_ref[:, O_GA:O_GA + 1024] = (do_a * ov * dsil_a).astype(BF16)
        prod = d_o * ov
        for h in range(N_HEADS):
            delta_ref[h] = jnp.sum(prod[:, h * HP:(h + 1) * HP], axis=-1, keepdims=True)

        z, z_up, z_dn = _conv_shifts(ccu_ref[:, 0:CONV_W].astype(F32), ccu_ref[:, CONV_W:].astype(F32), hp_ref, hn_ref, i, nt, tm)
        cv = cw_ref[0:1, :] * z_up + cw_ref[1:2, :] * z + cw_ref[2:3, :] * z_dn + cw_ref[3:4, :]
        sil_c, dsil_c = _silu_and_grad(main_ref[:, O_GC:O_GC + CONV_W].astype(F32))
        cb = main_ref[:, O_CB:O_CB + CONV_W].astype(F32)
        dmain_ref[:, O_CB:O_CB + CONV_W] = (do_c * cv * sil_c).astype(BF16)
        dmain_ref[:, O_GC:O_GC + CONV_W] = (do_c * cb * cv * dsil_c).astype(BF16)
        dcv_ref[...] = do_c * cb * sil_c

        sil_m, dsil_m = _silu_and_grad(main_ref[:, O_GM:O_GM + MEM_W].astype(F32))
        for h in range(MEM_HEADS):
            sl = slice(h * MEM_HD, (h + 1) * MEM_HD)
            qm = main_ref[:, O_QM + h * MEM_HD:O_QM + (h + 1) * MEM_HD].astype(F32)
            mk_h, mv_h = mk_ref[:, sl], mv_ref[:, sl]
            r, qhat, mq, p, pv = _mem_attend(qm, mqg_ref[...], mk_h, mv_h)
            dom = do_m[:, sl]
            dmain_ref[:, O_GM + h * MEM_HD:O_GM + (h + 1) * MEM_HD] = (dom * pv * dsil_m[:, sl]).astype(BF16)
            dpv = (dom * sil_m[:, sl]).astype(BF16)
            dp = _dot_nt(dpv, mv_h)
            ds = (p * (dp - jnp.sum(dp * p, axis=-1, keepdims=True)) * MEM_SCALE).astype(BF16)
            dmq = _dot(ds, mk_h)
            dmk_ref[:, sl] += _dot_tn(ds, mq)
            dmv_ref[:, sl] += _dot_tn(p.astype(BF16), dpv)
            dmqg_ref[...] += _colsum(dmq * qhat)
            dqm = _rms_bwd(dmq * mqg_ref[...], qhat, r, MEM_HD)
            dmain_ref[:, O_QM + h * MEM_HD:O_QM + (h + 1) * MEM_HD] = dqm.astype(BF16)

    row = lambda w: pl.BlockSpec((tm, w), lambda i: (i, 0))
    full = lambda a: pl.BlockSpec(a.shape, lambda i: (0,) * a.ndim)
    acc = lambda r, c: pl.BlockSpec((r, c), lambda i: (0, 0))
    hp, hn = _halo_specs(tm, S, 1024, 6, PROJ_HALO)
    return pl.pallas_call(
        body, name="merge_bwd", grid=(nt,),
        in_specs=[row(1024), row(6144), pl.BlockSpec((tm, 1024), lambda i: (i, 6)), hp, hn, row(1024), row(3072),
                  full(bg), full(convw), full(mqg), full(mk), full(mv), full(wot), full(wbat), full(wbct), full(wbmt)],
        out_specs=[row(6144), row(CONV_W), row(1024), pl.BlockSpec((N_HEADS, tm, 1), lambda i: (0, i, 0)), row(3072),
                   acc(1, 3072), acc(M, MEM_W), acc(M, MEM_W), acc(1, MEM_HD)],
        out_shape=[jax.ShapeDtypeStruct((S, 6144), BF16), jax.ShapeDtypeStruct((S, CONV_W), F32),
                   jax.ShapeDtypeStruct((S, 1024), BF16), jax.ShapeDtypeStruct((N_HEADS, S, 1), F32),
                   jax.ShapeDtypeStruct((S, 3072), BF16), jax.ShapeDtypeStruct((1, 3072), F32),
                   jax.ShapeDtypeStruct((M, MEM_W), F32), jax.ShapeDtypeStruct((M, MEM_W), F32),
                   jax.ShapeDtypeStruct((1, MEM_HD), F32)],
        compiler_params=_params(("arbitrary",)),
    )(g, proj, proj, proj, proj, o, u, bg, convw, mqg, mk, mv, wot, wbat, wbct, wbmt)


def _conv_bwd(dcv, proj, convw):
    S = dcv.shape[0]
    tm = _tile(S, 512)
    nt = S // tm

    def body(d_ref, dp_ref, dn_ref, ccu_ref, hp_ref, hn_ref, cw_ref, dccu_ref, dcw_ref):
        i = pl.program_id(0)

        @pl.when(i == 0)
        def _():
            dcw_ref[...] = jnp.zeros(dcw_ref.shape, F32)

        cc, cu = ccu_ref[:, 0:CONV_W].astype(F32), ccu_ref[:, CONV_W:].astype(F32)
        z, z_up, z_dn = _conv_shifts(cc, cu, hp_ref, hn_ref, i, nt, tm)
        d = d_ref[...]
        dprev = jnp.where(i == 0, 0.0, dp_ref[7:8, :])
        dnext = jnp.where(i == nt - 1, 0.0, dn_ref[0:1, :])
        row = lax.broadcasted_iota(jnp.int32, d.shape, 0)
        d_up = jnp.where(row == 0, dprev, pltpu.roll(d, 1, 0))
        d_dn = jnp.where(row == tm - 1, dnext, pltpu.roll(d, tm - 1, 0))
        dz = cw_ref[0:1, :] * d_dn + cw_ref[1:2, :] * d + cw_ref[2:3, :] * d_up
        dccu_ref[:, 0:CONV_W] = (dz * cu).astype(BF16)
        dccu_ref[:, CONV_W:] = (dz * cc).astype(BF16)
        dcw_ref[0:1, :] += _colsum(d * z_up)
        dcw_ref[1:2, :] += _colsum(d * z)
        dcw_ref[2:3, :] += _colsum(d * z_dn)
        dcw_ref[3:4, :] += _colsum(d)

    hp, hn = _halo_specs(tm, S, 1024, 6, PROJ_HALO)
    dp, dn = _halo_specs(tm, S, CONV_W, 0, F32_HALO)
    return pl.pallas_call(
        body, name="conv_bwd", grid=(nt,),
        in_specs=[pl.BlockSpec((tm, CONV_W), lambda i: (i, 0)), dp, dn,
                  pl.BlockSpec((tm, 1024), lambda i: (i, 6)), hp, hn,
                  pl.BlockSpec((8, CONV_W), lambda i: (0, 0))],
        out_specs=[pl.BlockSpec((tm, 1024), lambda i: (i, 0)), pl.BlockSpec((8, CONV_W), lambda i: (0, 0))],
        out_shape=[jax.ShapeDtypeStruct((S, 1024), BF16), jax.ShapeDtypeStruct((8, CONV_W), F32)],
        compiler_params=_params(("arbitrary",)),
    )(dcv, dcv, dcv, proj, proj, proj, convw)


def _attn_bwd(q, k, v, do, qt, dot, lse, delta):
    S = q.shape[0]
    tq, tk = _tile(S, 512), _tile(S, 2048)
    ni, nk = S // tq, S // tk

    def body(q_ref, do_ref, qt_ref, dot_ref, k_ref, v_ref, lse_ref, delta_ref, dq_ref, dkt_hbm, dvt_hbm,
             dq_s, dkt_s, dvt_s, sem):
        h, i = pl.program_id(0), pl.program_id(1)

        @pl.when(i == 0)
        def _():
            dkt_s[...] = jnp.zeros(dkt_s.shape, F32)
            dvt_s[...] = jnp.zeros(dvt_s.shape, F32)

        dq_s[...] = jnp.zeros(dq_s.shape, F32)
        qv, dov, qtv, dotv = q_ref[...], do_ref[...], qt_ref[...], dot_ref[...]
        lse_c, delta_c = lse_ref[0], delta_ref[0]

        def step(c, carry):
            cols = pl.ds(pl.multiple_of(c * tk, tk), tk)
            kc, vc = k_ref[cols, :], v_ref[cols, :]
            p = jnp.exp(_dot_nt(qv, kc) - lse_c)
            dp = _dot_nt(dov, vc)
            ds = (p * (dp - delta_c)).astype(BF16)
            dq_s[...] += _dot(ds, kc)
            dvt_s[:, cols] += _dot(dotv, p.astype(BF16))
            dkt_s[:, cols] += _dot(qtv, ds)
            return carry

        lax.fori_loop(0, nk, step, 0)
        dq_ref[...] = dq_s[...]

        @pl.when(i == ni - 1)
        def _():
            head = pl.ds(pl.multiple_of(h * HP, HP), HP)
            out_k = pltpu.make_async_copy(dkt_s, dkt_hbm.at[head, :], sem.at[0])
            out_v = pltpu.make_async_copy(dvt_s, dvt_hbm.at[head, :], sem.at[1])
            out_k.start()
            out_v.start()
            out_k.wait()
            out_v.wait()

    col = pl.BlockSpec((1, tq, 1), lambda h, i: (h, i, 0))
    blk = pl.BlockSpec((tq, HP), lambda h, i: (i, h))
    blkt = pl.BlockSpec((HP, tq), lambda h, i: (h, i))
    res = pl.BlockSpec((S, HP), lambda h, i: (0, h))
    whole = pl.BlockSpec(memory_space=pl.ANY)
    return pl.pallas_call(
        body, name="attn_bwd", grid=(N_HEADS, ni),
        in_specs=[blk, blk, blkt, blkt, res, res, col, col],
        out_specs=[blk, whole, whole],
        out_shape=[jax.ShapeDtypeStruct((S, N_HEADS * HP), F32), jax.ShapeDtypeStruct((N_HEADS * HP, S), F32),
                   jax.ShapeDtypeStruct((N_HEADS * HP, S), F32)],
        scratch_shapes=[pltpu.VMEM((tq, HP), F32), pltpu.VMEM((HP, S), F32), pltpu.VMEM((HP, S), F32),
                        pltpu.SemaphoreType.DMA((2,))],
        compiler_params=_params(("parallel", "arbitrary")),
    )(q, do, qt, dot, k, v, lse, delta)


def _mla_prep_bwd(proj, tabs, dq, dk, dv, qg, kvg, qhg, khg, wuq, wuk, wuqt, wukt, wuvt):
    S = proj.shape[0]
    tm = _tile(S, 512)

    def body(a_ref, c_ref, sa_ref, sb_ref, dq_ref, dk_ref, dv_ref, qg_ref, kvg_ref, qhg_ref, khg_ref,
             wuq_ref, wuk_ref, wuqt_ref, wukt_ref, wuvt_ref,
             da_ref, dqraw_ref, dkraw_ref, dqg_ref, dkvg_ref, dqhg_ref, dkhg_ref):
        @pl.when(pl.program_id(0) == 0)
        def _():
            dqg_ref[...] = jnp.zeros(dqg_ref.shape, F32)
            dkvg_ref[...] = jnp.zeros(dkvg_ref.shape, F32)
            dqhg_ref[...] = jnp.zeros(dqhg_ref.shape, F32)
            dkhg_ref[...] = jnp.zeros(dkhg_ref.shape, F32)

        ql = a_ref[:, 0:Q_RANK].astype(F32)
        kvl = a_ref[:, Q_RANK:Q_RANK + KV_RANK].astype(F32)
        kpe = a_ref[:, Q_RANK + KV_RANK:Q_RANK + KV_RANK + HP].astype(F32)
        rq, qhat = _rms_parts(ql)
        rkv, kvhat = _rms_parts(kvl)
        qraw = _dot((qhat * qg_ref[...]).astype(BF16), wuq_ref[...])
        kn = _dot((kvhat * kvg_ref[...]).astype(BF16), wuk_ref[...])
        c, sa, sb = c_ref[...], sa_ref[...], sb_ref[...]
        dkpe = jnp.zeros(kpe.shape, F32)
        dqhg = jnp.zeros((1, HP), F32)
        dkhg = jnp.zeros((1, HP), F32)
        for h in range(N_HEADS):
            sl = slice(h * HP, (h + 1) * HP)
            r, that = _rms_parts(qraw[:, sl], QK_DIM)
            dtn = _rope_t(dq_ref[:, sl], c, sa, sb) * SCALE
            dqhg = dqhg + _colsum(dtn * that)
            dqraw_ref[:, sl] = _rms_bwd(dtn * qhg_ref[...], that, r, QK_DIM).astype(BF16)
            r, that = _rms_parts(kn[:, sl] + kpe, QK_DIM)
            dtn = _rope_t(dk_ref[:, sl], c, sa, sb)
            dkhg = dkhg + _colsum(dtn * that)
            dkr = _rms_bwd(dtn * khg_ref[...], that, r, QK_DIM)
            dkraw_ref[:, sl] = dkr.astype(BF16)
            dkpe = dkpe + dkr
        dqhg_ref[...] += dqhg
        dkhg_ref[...] += dkhg
        dqn = _dot(dqraw_ref[...], wuqt_ref[...])
        dqg_ref[...] += _colsum(dqn * qhat)
        da_ref[:, 0:Q_RANK] = _rms_bwd(dqn * qg_ref[...], qhat, rq, Q_RANK).astype(BF16)
        dkvn = _dot(dkraw_ref[...], wukt_ref[...]) + _dot(dv_ref[...], wuvt_ref[...])
        dkvg_ref[...] += _colsum(dkvn * kvhat)
        da_ref[:, Q_RANK:Q_RANK + KV_RANK] = _rms_bwd(dkvn * kvg_ref[...], kvhat, rkv, KV_RANK).astype(BF16)
        da_ref[:, Q_RANK + KV_RANK:Q_RANK + KV_RANK + HP] = dkpe.astype(BF16)
        da_ref[:, Q_RANK + KV_RANK + HP:] = jnp.zeros((tm, 1024 - Q_RANK - KV_RANK - HP), BF16)

    row = lambda w: pl.BlockSpec((tm, w), lambda i: (i, 0))
    full = lambda a: pl.BlockSpec(a.shape, lambda i: (0,) * a.ndim)
    acc = lambda c: pl.BlockSpec((1, c), lambda i: (0, 0))
    return pl.pallas_call(
        body, name="mla_prep_bwd", grid=(S // tm,),
        in_specs=[pl.BlockSpec((tm, 1024), lambda i: (i, 7)), row(HP), row(HP), row(HP),
                  row(1024), row(1024), row(1024), full(qg), full(kvg), full(qhg), full(khg),
                  full(wuq), full(wuk), full(wuqt), full(wukt), full(wuvt)],
        out_specs=[row(1024), row(1024), row(1024), acc(Q_RANK), acc(KV_RANK), acc(HP), acc(HP)],
        out_shape=[jax.ShapeDtypeStruct((S, 1024), BF16)] * 3
        + [jax.ShapeDtypeStruct((1, Q_RANK), F32), jax.ShapeDtypeStruct((1, KV_RANK), F32),
           jax.ShapeDtypeStruct((1, HP), F32), jax.ShapeDtypeStruct((1, HP), F32)],
        compiler_params=_params(("arbitrary",)),
    )(proj, *tabs, dq, dk, dv, qg, kvg, qhg, khg, wuq, wuk, wuqt, wukt, wuvt)


def _inproj_bwd(dmain, dccu, dsega, wint, x, g, ng, carry=()):
    S = x.shape[0]
    tm = _tile(S, 256)
    nm, nc = dmain.shape[1], dccu.shape[1]
    assert nm + nc + dsega.shape[1] == PW

    def body(dm_ref, dc_ref, da_ref, w_ref, x_ref, g_ref, ng_ref, dx_ref, dng_ref):
        @pl.when(pl.program_id(0) == 0)
        def _():
            dng_ref[...] = jnp.zeros(dng_ref.shape, F32)

        dh = (_dot(dm_ref[...], w_ref[0:nm, :]) + _dot(dc_ref[...], w_ref[nm:nm + nc, :])
              + _dot(da_ref[...], w_ref[nm + nc:PW, :]))
        r, xhat = _rms_parts(x_ref[...])
        dng_ref[...] += _colsum(dh * xhat)
        dx_ref[...] = g_ref[...] + _rms_bwd(dh * ng_ref[...], xhat, r, D_MODEL)

    row = lambda w: pl.BlockSpec((tm, w), lambda i: (i, 0))
    in_specs = [row(nm), row(nc), row(dsega.shape[1]), pl.BlockSpec((PW, D_MODEL), lambda i: (0, 0)),
                row(D_MODEL), row(D_MODEL), pl.BlockSpec((1, D_MODEL), lambda i: (0, 0))]
    out_specs = [row(D_MODEL), pl.BlockSpec((1, D_MODEL), lambda i: (0, 0))]
    out_shape = [jax.ShapeDtypeStruct((S, D_MODEL), F32), jax.ShapeDtypeStruct((1, D_MODEL), F32)]
    args = (dmain, dccu, dsega, wint, x, g, ng)
    if not carry:
        dx, dng = pl.pallas_call(
            body, name="inproj_bwd", grid=(S // tm,), in_specs=in_specs, out_specs=out_specs, out_shape=out_shape,
            compiler_params=_params(("arbitrary",)))(*args)
        return dx, dng, []
    wrapped, extra = _carried(body, 7, 2, carry, gather=False)
    res = pl.pallas_call(
        wrapped, name="inproj_bwd_scatter", grid=(S // tm,), in_specs=in_specs + extra["in_specs"],
        out_specs=out_specs + extra["out_specs"], out_shape=out_shape + extra["out_shape"],
        scratch_shapes=extra["scratch_shapes"], compiler_params=_params(("arbitrary",)))(*args, *carry)
    return res[0], res[1], list(res[2:])


def _memkv_bwd(mem, mng, mkg, wmkv, wmkvt, dmk, dmv):
    M = mem.shape[0]

    def body(mem_ref, mng_ref, mkg_ref, w_ref, wt_ref, dmk_ref, dmv_ref, dw_ref, dmng_ref, dmkg_ref, d_s):
        r, mhat = _rms_parts(mem_ref[...])
        mn = (mhat * mng_ref[...]).astype(BF16)
        mkv = _dot(mn, w_ref[...])
        dmkg = jnp.zeros((1, MEM_HD), F32)
        for h in range(MEM_HEADS):
            sl = slice(h * MEM_HD, (h + 1) * MEM_HD)
            rk, khat = _rms_parts(mkv[:, 2 * MEM_HD * h:2 * MEM_HD * h + MEM_HD])
            dkn = dmk_ref[:, sl]
            dmkg = dmkg + _colsum(dkn * khat)
            d_s[:, 2 * MEM_HD * h:2 * MEM_HD * h + MEM_HD] = _rms_bwd(dkn * mkg_ref[...], khat, rk, MEM_HD).astype(BF16)
            d_s[:, 2 * MEM_HD * h + MEM_HD:2 * MEM_HD * (h + 1)] = dmv_ref[:, sl].astype(BF16)
        dmkg_ref[...] = dmkg
        dw_ref[...] = _dot_tn(mn, d_s[...])
        dmn = _dot(d_s[...], wt_ref[...])
        dmng_ref[...] = _colsum(dmn * mhat)

    return pl.pallas_call(
        body, name="memkv_bwd",
        out_shape=[jax.ShapeDtypeStruct((D_MODEL, 2 * MEM_W), F32), jax.ShapeDtypeStruct((1, D_MODEL), F32),
                   jax.ShapeDtypeStruct((1, MEM_HD), F32)],
        scratch_shapes=[pltpu.VMEM((M, 2 * MEM_W), BF16)],
        compiler_params=pltpu.CompilerParams(vmem_limit_bytes=VMEM_LIMIT_V7X),
    )(mem, mng, mkg, wmkv, wmkvt, dmk, dmv)


def _mm_tn(a, b, name, col0=0, ncols=None):
    S, M = a.shape
    N = b.shape[1] if ncols is None else ncols
    tm, tn, ts = _tile(M, 1024), _tile(N, 1024), _tile(S, 2048)
    assert col0 % tn == 0
    jb = col0 // tn

    def body(a_ref, b_ref, o_ref):
        @pl.when(pl.program_id(2) == 0)
        def _():
            o_ref[...] = jnp.zeros(o_ref.shape, F32)

        o_ref[...] += _dot_tn(a_ref[...].astype(BF16), b_ref[...].astype(BF16))

    return pl.pallas_call(
        body, name=name, grid=(M // tm, N // tn, S // ts),
        in_specs=[pl.BlockSpec((ts, tm), lambda i, j, k: (k, i)),
                  pl.BlockSpec((ts, tn), lambda i, j, k: (k, j + jb))],
        out_specs=pl.BlockSpec((tm, tn), lambda i, j, k: (i, j)),
        out_shape=jax.ShapeDtypeStruct((M, N), F32),
        compiler_params=_params(("parallel", "parallel", "arbitrary")),
    )(a, b)


def _adamw(w, g0, g1, m, v, name):
    R, C = w.shape
    tr = R
    for cand in (512, 256, 128, 64, 32, 16, 8):
        if R % cand == 0 and cand * C * 4 <= (1 << 20):
            tr = cand
            break
    c1 = 1.0 / (1.0 - ADAM_B1 ** ADAM_STEP)
    c2 = 1.0 / (1.0 - ADAM_B2 ** ADAM_STEP)

    def body(w_ref, g0_ref, g1_ref, m_ref, v_ref, g_ref, d_ref, nm_ref, nv_ref):
        g = g0_ref[...] + g1_ref[...]
        nm = ADAM_B1 * m_ref[...] + (1.0 - ADAM_B1) * g
        nv = ADAM_B2 * v_ref[...] + (1.0 - ADAM_B2) * (g * g)
        g_ref[...] = g
        nm_ref[...] = nm
        nv_ref[...] = nv
        d_ref[...] = -ADAM_LR * ((nm * c1) / (jnp.sqrt(nv * c2) + ADAM_EPS) + ADAM_WD * w_ref[...])

    blk = pl.BlockSpec((tr, C), lambda i: (i, 0))
    return pl.pallas_call(
        body, name=name, grid=(R // tr,),
        in_specs=[blk] * 5, out_specs=[blk] * 4,
        out_shape=[jax.ShapeDtypeStruct((R, C), F32)] * 4,
        compiler_params=_params(("parallel",)),
    )(w, g0, g1, m, v)


def _sum_slabs(a, name):
    K, R, C = a.shape
    tr = R
    for cand in (512, 256, 128, 64, 32, 16, 8):
        if R % cand == 0 and cand * C * 4 * K <= (4 << 20):
            tr = cand
            break

    def body(a_ref, o_ref):
        t = a_ref[0].astype(F32)
        for k in range(1, K):
            t = t + a_ref[k].astype(F32)
        o_ref[...] = t

    return pl.pallas_call(
        body, name=name, grid=(R // tr,),
        in_specs=[pl.BlockSpec((K, tr, C), lambda i: (0, i, 0))],
        out_specs=pl.BlockSpec((tr, C), lambda i: (i, 0)),
        out_shape=jax.ShapeDtypeStruct((R, C), F32),
        compiler_params=_params(("parallel",)),
    )(a)


def _gather_chips(arrs):
    n = len(arrs)
    halves = [a.shape[0] // 2 for a in arrs]
    assert all(a.shape[0] == 2 * hf for a, hf in zip(arrs, halves))

    def body(*refs):
        ins, outs = refs[:n], refs[n:2 * n]
        send1, recv1, send2, recv2, loc = refs[2 * n:]
        x, y, c = lax.axis_index("x"), lax.axis_index("y"), lax.axis_index("c")
        me = 2 * x + y
        mine = [pl.ds(c * hf, hf) for hf in halves]
        theirs = [pl.ds((1 - c) * hf, hf) for hf in halves]
        sibling = (x, y, 1 - c)
        waits = []
        for a in range(n):
            own = pltpu.make_async_copy(ins[a], outs[a].at[me], loc.at[a])
            own.start()
            waits.append(own.wait)

        def over_ici(a, k, src_chip):
            return pltpu.make_async_remote_copy(
                src_ref=ins[a].at[mine[a]], dst_ref=outs[a].at[src_chip, mine[a]], send_sem=send1.at[3 * a + k - 1],
                recv_sem=recv1.at[3 * a + k - 1], device_id=_chip_peer(x, y, c, k), device_id_type=MESH)

        def to_sibling(a, k, layers):
            block = outs[a].at[me ^ k, layers]
            return pltpu.make_async_remote_copy(
                src_ref=block, dst_ref=block, send_sem=send2.at[3 * a + k - 1], recv_sem=recv2.at[3 * a + k - 1],
                device_id=sibling, device_id_type=MESH)

        for a in range(n):
            for k in (1, 2, 3):
                cp = over_ici(a, k, me)
                cp.start()
                waits.append(cp.wait_send)
        for a in range(n):
            for k in (1, 2, 3):
                over_ici(a, k, me ^ k).wait_recv()
                cp = to_sibling(a, k, mine[a])
                cp.start()
                waits.append(cp.wait_send)
        for a in range(n):
            for k in (1, 2, 3):
                to_sibling(a, k, theirs[a]).wait_recv()
        for w in waits:
            w()

    return pl.pallas_call(
        body, name="gather_weights",
        in_specs=[HBM] * n, out_specs=[HBM] * n,
        out_shape=[jax.ShapeDtypeStruct((4,) + a.shape, a.dtype) for a in arrs],
        scratch_shapes=[pltpu.SemaphoreType.DMA((3 * n,)), pltpu.SemaphoreType.DMA((3 * n,)),
                        pltpu.SemaphoreType.DMA((3 * n,)), pltpu.SemaphoreType.DMA((3 * n,)),
                        pltpu.SemaphoreType.DMA((n,))],
    )(*arrs)


def _scatter_chips(arrs, small):
    n = len(arrs)

    def body(*refs):
        ins, small_in = refs[:n], refs[n]
        outs, small_out = refs[n + 1:2 * n + 1], refs[2 * n + 1]
        send, recv, loc, ssend, srecv = refs[2 * n + 2:]
        x, y, c = lax.axis_index("x"), lax.axis_index("y"), lax.axis_index("c")
        me = 2 * x + y
        me8 = 4 * x + 2 * y + c
        copies = []
        for a in range(n):
            own = pltpu.make_async_copy(ins[a].at[me], outs[a].at[me], loc.at[a])
            own.start()
            copies.append(own)
        own = pltpu.make_async_copy(small_in, small_out.at[me8], loc.at[n])
        own.start()
        copies.append(own)
        for k in range(1, 8):
            cp = pltpu.make_async_remote_copy(
                src_ref=small_in, dst_ref=small_out.at[me8], send_sem=ssend.at[k - 1], recv_sem=srecv.at[k - 1],
                device_id=(x ^ (k >> 2), y ^ ((k >> 1) & 1), c ^ (k & 1)), device_id_type=MESH)
            cp.start()
            copies.append(cp)
        for a in range(n):
            for k in (1, 2, 3):
                cp = pltpu.make_async_remote_copy(
                    src_ref=ins[a].at[me ^ k], dst_ref=outs[a].at[me], send_sem=send.at[3 * a + k - 1],
                    recv_sem=recv.at[3 * a + k - 1], device_id=_chip_peer(x, y, c, k), device_id_type=MESH)
                cp.start()
                copies.append(cp)
        for cp in copies:
            cp.wait()

    return pl.pallas_call(
        body, name="scatter_grads",
        in_specs=[HBM] * (n + 1), out_specs=[HBM] * (n + 1),
        out_shape=[jax.ShapeDtypeStruct(a.shape, a.dtype) for a in arrs]
        + [jax.ShapeDtypeStruct((8,) + small.shape, small.dtype)],
        scratch_shapes=[pltpu.SemaphoreType.DMA((3 * n,)), pltpu.SemaphoreType.DMA((3 * n,)),
                        pltpu.SemaphoreType.DMA((n + 1,)), pltpu.SemaphoreType.DMA((7,)),
                        pltpu.SemaphoreType.DMA((7,))],
    )(*arrs, small)


def _swap_cores(arrs):
    n = len(arrs)

    def body(*refs):
        ins, outs = refs[:n], refs[n:2 * n]
        send, recv = refs[2 * n:]
        x, y, c = lax.axis_index("x"), lax.axis_index("y"), lax.axis_index("c")
        copies = []
        for a in range(n):
            cp = pltpu.make_async_remote_copy(
                src_ref=ins[a], dst_ref=outs[a], send_sem=send.at[a], recv_sem=recv.at[a],
                device_id=(x, y, 1 - c), device_id_type=MESH)
            cp.start()
            copies.append(cp)
        for cp in copies:
            cp.wait()

    return pl.pallas_call(
        body, name="swap_cores",
        in_specs=[HBM] * n, out_specs=[HBM] * n,
        out_shape=[jax.ShapeDtypeStruct(a.shape, a.dtype) for a in arrs],
        scratch_shapes=[pltpu.SemaphoreType.DMA((n,)), pltpu.SemaphoreType.DMA((n,))],
    )(*arrs)


def _pad_last(a, n):
    return jnp.pad(a, [(0, 0)] * (a.ndim - 1) + [(0, n - a.shape[-1])])


def _pad_w_in(w):
    lead = w.shape[:-1]
    seg = lambda a, b: w[..., a:b]
    ga = _pad_last(seg(2720, 3232).reshape(lead + (N_HEADS, V_DIM)), HP).reshape(lead + (1024,))
    kpe = jnp.pad(seg(640, 672), [(0, 0)] * len(lead) + [(NOPE_DIM, HP - QK_DIM)])
    zero = jnp.zeros(lead + (PW - 7936,), w.dtype)
    return jnp.concatenate(
        [seg(4256, 7328), ga, seg(672, 1184), seg(2208, 2720), seg(3232, 3744), seg(3744, 4256),
         seg(1184, 1696), seg(1696, 2208), seg(0, 384), seg(384, 640), kpe, zero], axis=-1)


def _unpad_w_in(w):
    lead = w.shape[:-1]
    seg = lambda a, n: w[..., a:a + n]
    ga = seg(O_GA, 1024).reshape(lead + (N_HEADS, HP))[..., :V_DIM].reshape(lead + (N_HEADS * V_DIM,))
    return jnp.concatenate(
        [seg(O_QL, 384), seg(O_KVL, 256), seg(O_KPE + NOPE_DIM, ROPE_DIM), seg(O_CB, 512), seg(O_CC, 512),
         seg(O_CU, 512), seg(O_QM, 512), ga, seg(O_GC, 512), seg(O_GM, 512), seg(O_R, 3072)], axis=-1)


def _cols_from_shards(g):
    _, L, R, C = g.shape
    return jnp.transpose(g, (1, 2, 0, 3)).reshape(L, R, 4 * C)


def _t(w):
    return jnp.swapaxes(w, -1, -2)


def _layer_fwd(x, mem, tabs, p, next_shards=()):
    proj, xn, gathered = _inproj(x, p["norm_g"], p["w_in"], next_shards)
    q, k, v, qn, kvn, kmax2 = _mla_prep(proj, tabs, p["q_norm_g"], p["kv_norm_g"], p["q_head_g"], p["k_head_g"],
                                        p["w_uq"], p["w_uk"], p["w_uv"])
    qt = _t(q)
    o, lse = _attn_fwd(qt, k, _t(v), kmax2)
    lse = lse.reshape(N_HEADS, x.shape[0], 1)
    mk, mv = _memkv(mem, p["mem_norm_g"], p["mem_k_g"], p["w_mkv"])
    x_new, oa, oc, om, u, y = _merge(proj, o, x, p["b_gate"], p["conv_wb"], p["mem_q_g"], mk, mv,
                                     p["w_br_attn"], p["w_br_conv"], p["w_br_mem"], p["w_out"])
    saved = dict(x=x, proj=proj, xn=xn, q=q, qt=qt, k=k, v=v, qn=qn, kvn=kvn, o=o, lse=lse, mk=mk, mv=mv,
                 oa=oa, oc=oc, om=om, u=u, y=y)
    return x_new, saved, gathered


def _layer_bwd(g, mem, tabs, p, s, to_owner=()):
    S = g.shape[0]
    dmain, dcv, d_o, delta, du, dbg, dmk, dmv, dmqg = _merge_bwd(
        g, s["proj"], s["o"], s["u"], p["b_gate"], p["conv_wb"], p["mem_q_g"], s["mk"], s["mv"],
        p["w_out_t"], p["w_br_attn_t"], p["w_br_conv_t"], p["w_br_mem_t"])
    dccu, dconv = _conv_bwd(dcv, s["proj"], p["conv_wb"])
    dq, dkt, dvt = _attn_bwd(s["q"], s["k"], s["v"], d_o, s["qt"], _t(d_o), s["lse"], delta)
    dk, dv = _t(dkt), _t(dvt).astype(BF16)
    dsega, dqraw, dkraw, dqg, dkvg, dqhg, dkhg = _mla_prep_bwd(
        s["proj"], tabs, dq, dk, dv, p["q_norm_g"], p["kv_norm_g"], p["q_head_g"], p["k_head_g"],
        p["w_uq"], p["w_uk"], p["w_uq_t"], p["w_uk_t"], p["w_uv_t"])
    dx, dng, received = _inproj_bwd(dmain, dccu, dsega, p["w_in_t"], s["x"], g, p["norm_g"], to_owner)
    dwmkv, dmng, dmkg = _memkv_bwd(mem, p["mem_norm_g"], p["mem_k_g"], p["w_mkv"], p["w_mkv_t"], dmk, dmv)
    grads = dict(
        norm_g=dng, b_gate=dbg, q_norm_g=dqg, kv_norm_g=dkvg, q_head_g=dqhg, k_head_g=dkhg,
        conv_wb=dconv, mem_norm_g=dmng, mem_q_g=dmqg, mem_k_g=dmkg, w_mkv=dwmkv,
        w_in=jnp.concatenate([_mm_tn(s["xn"], dmain, "grad_w_in"), _mm_tn(s["xn"], dccu, "grad_w_in_conv"),
                              _mm_tn(s["xn"], dsega, "grad_w_in_lat")], axis=1),
        w_uq=_mm_tn(s["qn"], dqraw, "grad_w_uq"),
        w_uk=_mm_tn(s["kvn"], dkraw, "grad_w_uk"),
        w_uv=_mm_tn(s["kvn"], dv, "grad_w_uv"),
        w_br_attn=_mm_tn(s["oa"], du, "grad_w_br_attn", 0, 1024),
        w_br_conv=_mm_tn(s["oc"], du, "grad_w_br_conv", 1024, 1024),
        w_br_mem=_mm_tn(s["om"], du, "grad_w_br_mem", 2048, 1024),
        w_out=_mm_tn(s["y"], g, "grad_w_out"),
    )
    return dx, grads, received


def _layer_params(big, full, l):
    p = {}
    w_in = _pad_w_in(big["w_in"])
    w_uq = _pad_last(big["w_uq"].reshape(Q_RANK, N_HEADS, QK_DIM), HP).reshape(Q_RANK, 1024)
    ukv = big["w_ukv"].reshape(KV_RANK, N_HEADS, NOPE_DIM + V_DIM)
    w_uk = _pad_last(ukv[..., :NOPE_DIM], HP).reshape(KV_RANK, 1024)
    w_uv = _pad_last(ukv[..., NOPE_DIM:], HP).reshape(KV_RANK, 1024)
    w_ba = jnp.pad(big["w_br_attn"].reshape(N_HEADS, V_DIM, D_MODEL), ((0, 0), (0, HP - V_DIM), (0, 0)))
    w_ba = w_ba.reshape(1024, D_MODEL)
    p.update(w_in=w_in, w_uq=w_uq, w_uk=w_uk, w_uv=w_uv, w_br_attn=w_ba, w_br_conv=big["w_br_conv"],
             w_br_mem=big["w_br_mem"], w_out=big["w_out"], w_mkv=big["w_mkv"])
    for n in ("w_in", "w_uq", "w_uk", "w_uv", "w_br_attn", "w_br_conv", "w_br_mem", "w_out", "w_mkv"):
        p[n + "_t"] = _t(p[n])
    for n in ("norm_g", "b_gate", "q_norm_g", "kv_norm_g", "mem_norm_g", "mem_q_g", "mem_k_g"):
        p[n] = full[n][l][None, :]
    p["q_head_g"] = _pad_last(full["q_head_g"][l][None, :], HP)
    p["k_head_g"] = _pad_last(full["k_head_g"][l][None, :], HP)
    p["conv_wb"] = jnp.concatenate(
        [full["conv_w"][l], full["conv_b"][l][None, :], jnp.zeros((4, CONV_W), F32)], axis=0)
    return p


def _join_shards(name, g):
    _, R, C = g.shape
    if name in _COL_SHARDED:
        return jnp.transpose(g, (1, 0, 2)).reshape(R, 4 * C)
    return g.reshape(4 * R, C)


def _split_shards(name, w):
    R, C = w.shape
    if name in _COL_SHARDED:
        return jnp.transpose(w.reshape(R, 4, C // 4), (1, 0, 2)).astype(BF16)
    return w.reshape(4, R // 4, C).astype(BF16)


def _unpad_grads(gp):
    out = {"w_in": _unpad_w_in(gp["w_in"])}
    out["w_uq"] = gp["w_uq"].reshape(Q_RANK, N_HEADS, HP)[..., :QK_DIM].reshape(Q_RANK, N_HEADS * QK_DIM)
    duk = gp["w_uk"].reshape(KV_RANK, N_HEADS, HP)[..., :NOPE_DIM]
    duv = gp["w_uv"].reshape(KV_RANK, N_HEADS, HP)[..., :V_DIM]
    out["w_ukv"] = jnp.concatenate([duk, duv], axis=-1).reshape(KV_RANK, 1024)
    out["w_br_attn"] = gp["w_br_attn"].reshape(N_HEADS, HP, D_MODEL)[:, :V_DIM].reshape(512, D_MODEL)
    for n in ("w_br_conv", "w_br_mem", "w_out", "w_mkv"):
        out[n] = gp[n]
    return out


def _train_step(x, mem, positions, w, target):
    S = x.shape[0]
    invf16 = ROPE_BASE ** (-jnp.arange(0, ROPE_DIM, 2, dtype=F32) / ROPE_DIM)
    invf = jnp.concatenate([jnp.zeros((NOPE_DIM,), F32), invf16, invf16, jnp.zeros((HP - QK_DIM,), F32)])[None, :]
    tabs = _rope_tables(jnp.broadcast_to(positions.reshape(S, 1), (S, HP)), invf)
    shards = [[w[n][l].astype(BF16) for n in _BIG] for l in range(DEPTH)]
    first = _gather_chips(shards[0] + [w["conv_w"]])
    gathered = first[:-1]
    full = {n: w[n] for n in _SMALL}
    full["conv_w"] = _cols_from_shards(first[-1])
    params, saved = [], []
    h = x
    for l in range(DEPTH):
        big = {n: _join_shards(n, g) for n, g in zip(_BIG, gathered)}
        params.append(_layer_params(big, full, l))
        h, s, gathered = _layer_fwd(h, mem, tabs, params[l], shards[l + 1] if l + 1 < DEPTH else ())
        saved.append(s)
    g, loss_part = _loss_head(h, target)
    per_layer, received = [None] * DEPTH, [None] * DEPTH
    to_owner = ()
    for l in reversed(range(DEPTH)):
        g, per_layer[l], got = _layer_bwd(g, mem, tabs, params[l], saved[l], to_owner)
        if to_owner:
            received[l + 1] = got
        big_g = _unpad_grads(per_layer[l])
        to_owner = [_split_shards(n, big_g[n]) for n in _BIG]
    st = lambda n: jnp.stack([per_layer[l][n] for l in range(DEPTH)])
    small = {}
    for n in ("norm_g", "b_gate", "q_norm_g", "kv_norm_g", "mem_norm_g", "mem_q_g", "mem_k_g"):
        small[n] = st(n)[:, 0, :]
    small["q_head_g"] = st("q_head_g")[:, 0, :QK_DIM]
    small["k_head_g"] = st("k_head_g")[:, 0, :QK_DIM]
    cwb = st("conv_wb")
    small["conv_w"] = cwb[:, 0:3, :]
    small["conv_b"] = cwb[:, 3, :]
    return loss_part, g, received, to_owner, small


_COL_SHARDED = ("w_in", "w_uq", "w_ukv", "w_br_attn", "w_br_conv", "w_br_mem")
_ROW_SHARDED = ("w_mkv", "w_out")
_BIG = _COL_SHARDED + _ROW_SHARDED
_SMALL = ("norm_g", "b_gate", "q_norm_g", "kv_norm_g", "q_head_g", "k_head_g", "conv_w", "conv_b",
          "mem_norm_g", "mem_q_g", "mem_k_g")
_ORDER = ("norm_g", "w_in", "b_gate", "q_norm_g", "w_uq", "kv_norm_g", "w_ukv", "q_head_g", "k_head_g",
          "conv_w", "conv_b", "mem_norm_g", "w_mkv", "mem_q_g", "mem_k_g", "w_br_attn", "w_br_conv",
          "w_br_mem", "w_out")


def _pack_small(d, extra):
    flat = jnp.concatenate([d[n].reshape(-1) for n in _SMALL] + [extra.reshape(-1)])
    n = flat.shape[0]
    rows = -(-n // 1024) * 8
    return jnp.pad(flat, (0, rows * 128 - n)).reshape(rows, 128)


def _unpack_small(packed, like):
    flat = packed.reshape(-1)
    out, off = {}, 0
    for n in _SMALL:
        sz = int(np.prod(like[n].shape))
        out[n] = flat[off:off + sz].reshape(like[n].shape)
        off += sz
    return out, flat[off:]


def kernel(x, mem, positions, norm_g, w_in, b_gate, q_norm_g, w_uq, kv_norm_g, w_ukv, q_head_g, k_head_g, conv_w, conv_b, mem_norm_g, w_mkv, mem_q_g, mem_k_g, w_br_attn, w_br_conv, w_br_mem, w_out, loss_target, m_norm_g, m_w_in, m_b_gate, m_q_norm_g, m_w_uq, m_kv_norm_g, m_w_ukv, m_q_head_g, m_k_head_g, m_conv_w, m_conv_b, m_mem_norm_g, m_w_mkv, m_mem_q_g, m_mem_k_g, m_w_br_attn, m_w_br_conv, m_w_br_mem, m_w_out, v_norm_g, v_w_in, v_b_gate, v_q_norm_g, v_w_uq, v_kv_norm_g, v_w_ukv, v_q_head_g, v_k_head_g, v_conv_w, v_conv_b, v_mem_norm_g, v_w_mkv, v_mem_q_g, v_mem_k_g, v_w_br_attn, v_w_br_conv, v_w_br_mem, v_w_out):
    w = dict(norm_g=norm_g, w_in=w_in, b_gate=b_gate, q_norm_g=q_norm_g, w_uq=w_uq, kv_norm_g=kv_norm_g,
             w_ukv=w_ukv, q_head_g=q_head_g, k_head_g=k_head_g, conv_w=conv_w, conv_b=conv_b,
             mem_norm_g=mem_norm_g, w_mkv=w_mkv, mem_q_g=mem_q_g, mem_k_g=mem_k_g, w_br_attn=w_br_attn,
             w_br_conv=w_br_conv, w_br_mem=w_br_mem, w_out=w_out)
    m = dict(norm_g=m_norm_g, w_in=m_w_in, b_gate=m_b_gate, q_norm_g=m_q_norm_g, w_uq=m_w_uq,
             kv_norm_g=m_kv_norm_g, w_ukv=m_w_ukv, q_head_g=m_q_head_g, k_head_g=m_k_head_g, conv_w=m_conv_w,
             conv_b=m_conv_b, mem_norm_g=m_mem_norm_g, w_mkv=m_w_mkv, mem_q_g=m_mem_q_g, mem_k_g=m_mem_k_g,
             w_br_attn=m_w_br_attn, w_br_conv=m_w_br_conv, w_br_mem=m_w_br_mem, w_out=m_w_out)
    v = dict(norm_g=v_norm_g, w_in=v_w_in, b_gate=v_b_gate, q_norm_g=v_q_norm_g, w_uq=v_w_uq,
             kv_norm_g=v_kv_norm_g, w_ukv=v_w_ukv, q_head_g=v_q_head_g, k_head_g=v_k_head_g, conv_w=v_conv_w,
             conv_b=v_conv_b, mem_norm_g=v_mem_norm_g, w_mkv=v_w_mkv, mem_q_g=v_mem_q_g, mem_k_g=v_mem_k_g,
             w_br_attn=v_w_br_attn, w_br_conv=v_w_br_conv, w_br_mem=v_w_br_mem, w_out=v_w_out)
    chip = 2 * lax.axis_index("x") + lax.axis_index("y")

    loss_part, grad_x, received, last_slabs, grads = _train_step(x[0], mem[0], positions[0], w, loss_target[0])

    loss_vec = jnp.zeros((128,), F32).at[0].set(0.5 / D_MODEL * jnp.sum(loss_part))
    small = _pack_small(grads, loss_vec)
    scattered = _scatter_chips(last_slabs, small)
    received[0] = scattered[:-1]
    small_sum = _sum_slabs(scattered[-1], "sum_small")
    partial = []
    for a, n in enumerate(_BIG):
        partial.append(jnp.concatenate([_sum_slabs(received[l][a], "sum_" + n) for l in range(DEPTH)], axis=0))
    other = _swap_cores(partial)

    small_g, tail = _unpack_small(small_sum, {n: (grads[n]) for n in _SMALL})
    loss = tail[0]
    small_g["conv_w"] = lax.dynamic_slice_in_dim(small_g["conv_w"], chip * (CONV_W // 4), CONV_W // 4, axis=2)

    outs_g, outs_d, outs_m, outs_v = {}, {}, {}, {}
    for n, p0, p1 in zip(_BIG, partial, other):
        shape = w[n].shape
        flat = lambda t: t.reshape(p0.shape)
        g_, d_, m_, v_ = _adamw(flat(w[n]), p0, p1, flat(m[n]), flat(v[n]), "adamw_" + n)
        outs_g[n], outs_d[n], outs_m[n], outs_v[n] = (t.reshape(shape) for t in (g_, d_, m_, v_))
    zero_small = jnp.zeros_like(small_sum)
    pk = lambda d: _pack_small(d, jnp.zeros((128,), F32))
    g_, d_, m_, v_ = _adamw(pk(w), _pack_small(small_g, jnp.zeros((128,), F32)), zero_small, pk(m), pk(v),
                            "adamw_small")
    like = {n: w[n] for n in _SMALL}
    for dst, packed in ((outs_g, g_), (outs_d, d_), (outs_m, m_), (outs_v, v_)):
        dst.update(_unpack_small(packed, like)[0])

    return (loss, grad_x[None], *[outs_g[n] for n in _ORDER], *[outs_d[n] for n in _ORDER],
            *[outs_m[n] for n in _ORDER], *[outs_v[n] for n in _ORDER])
```

```python
import functools

import numpy as np
import jax
import jax.numpy as jnp
from jax import lax
from jax.experimental import pallas as pl
from jax.experimental.pallas import tpu as pltpu

F32 = jnp.float32
BF16 = jnp.bfloat16

D_MODEL = 1024
DEPTH = 4
N_HEADS = 8
QK_DIM = 96
NOPE_DIM = 64
ROPE_DIM = 32
V_DIM = 64
Q_RANK = 384
KV_RANK = 256
CONV_W = 512
MEM_HEADS = 4
MEM_HD = 128
MEM_W = 512
IN_WIDTH = 7328
PW = 8192
HP = 128
PROJ_HALO = 16
F32_HALO = 8
EPS = 1e-6
ROPE_BASE = 10000.0
SCALE = QK_DIM ** -0.5
MEM_SCALE = MEM_HD ** -0.5

ADAM_LR = 0.001
ADAM_B1 = 0.9
ADAM_B2 = 0.999
ADAM_EPS = 1e-08
ADAM_WD = 0.01
ADAM_STEP = 10

VMEM_LIMIT_V7X = 56 * 1024 * 1024

O_R, O_GA, O_CB, O_QM, O_GC, O_GM, O_CC, O_CU, O_QL, O_KVL, O_KPE = (
    0, 3072, 4096, 4608, 5120, 5632, 6144, 6656, 7168, 7552, 7808)


def _params(sem, vmem=VMEM_LIMIT_V7X):
    return pltpu.CompilerParams(dimension_semantics=sem, vmem_limit_bytes=vmem)


def _sigmoid(t):
    return 0.5 * jnp.tanh(0.5 * t) + 0.5


def _silu_and_grad(g):
    sg = _sigmoid(g)
    return g * sg, sg * (1.0 + g * (1.0 - sg))


def _rms(t, g, n=None):
    n = t.shape[-1] if n is None else n
    r = lax.rsqrt(jnp.sum(t * t, axis=-1, keepdims=True) * (1.0 / n) + EPS)
    return (t * r) * g


def _rms_parts(t, n=None):
    n = t.shape[-1] if n is None else n
    r = lax.rsqrt(jnp.sum(t * t, axis=-1, keepdims=True) * (1.0 / n) + EPS)
    return r, t * r


def _rms_bwd(dhat, hat, r, n):
    return r * (dhat - hat * (jnp.sum(dhat * hat, axis=-1, keepdims=True) * (1.0 / n)))


def _rope(t, c, sa, sb):
    return t * c + pltpu.roll(t, HP - 16, 1) * sa + pltpu.roll(t, 16, 1) * sb


def _rope_t(d, c, sa, sb):
    return d * c + pltpu.roll(d * sa, 16, 1) + pltpu.roll(d * sb, HP - 16, 1)


def _dot(a, b):
    return jnp.dot(a, b, preferred_element_type=F32)


def _dot_nt(a, b):
    return lax.dot_general(a, b, (((1,), (1,)), ((), ())), preferred_element_type=F32)


def _dot_tn(a, b):
    return lax.dot_general(a, b, (((0,), (0,)), ((), ())), preferred_element_type=F32)


def _colsum(t):
    return jnp.sum(t, axis=0, keepdims=True)


def _tile(n, t):
    t = min(n, t)
    assert n % t == 0, (n, t)
    return t


def _rope_tables(pos_b, invf):
    S = pos_b.shape[0]
    tm = _tile(S, 1024)

    def body(pos_ref, invf_ref, c_ref, sa_ref, sb_ref):
        ang = pos_ref[...].astype(F32) * invf_ref[...]
        lane = lax.broadcasted_iota(jnp.int32, ang.shape, 1)
        cs = jnp.cos(ang)
        sn = jnp.sin(ang)
        c_ref[...] = jnp.where(lane < NOPE_DIM, 1.0, jnp.where(lane < QK_DIM, cs, 0.0))
        sa_ref[...] = jnp.where((lane >= NOPE_DIM) & (lane < NOPE_DIM + 16), -sn, 0.0)
        sb_ref[...] = jnp.where((lane >= NOPE_DIM + 16) & (lane < QK_DIM), sn, 0.0)

    blk = pl.BlockSpec((tm, HP), lambda i: (i, 0))
    return pl.pallas_call(
        body, name="rope_tables", grid=(S // tm,),
        in_specs=[blk, pl.BlockSpec((1, HP), lambda i: (0, 0))],
        out_specs=[blk, blk, blk],
        out_shape=[jax.ShapeDtypeStruct((S, HP), F32)] * 3,
        compiler_params=_params(("parallel",)),
    )(pos_b, invf)


MESH = pl.DeviceIdType.MESH
HBM = pl.BlockSpec(memory_space=pltpu.HBM)


def _chip_peer(x, y, c, k):
    return (x ^ (k >> 1), y ^ (k & 1), c)


def _chip_copies(ins, outs, send, recv, loc, gather):
    x, y, c = lax.axis_index("x"), lax.axis_index("y"), lax.axis_index("c")
    me = 2 * x + y
    cps = []
    for a in range(len(ins)):
        cps.append(pltpu.make_async_copy(ins[a] if gather else ins[a].at[me], outs[a].at[me], loc.at[a]))
        for k in (1, 2, 3):
            cps.append(pltpu.make_async_remote_copy(
                src_ref=ins[a] if gather else ins[a].at[me ^ k], dst_ref=outs[a].at[me],
                send_sem=send.at[3 * a + k - 1], recv_sem=recv.at[3 * a + k - 1],
                device_id=_chip_peer(x, y, c, k), device_id_type=MESH))
    return cps


def _carried(body, n_in, n_out, carry, gather):
    n = len(carry)

    def wrapped(*refs):
        ins, cin = refs[:n_in], refs[n_in:n_in + n]
        outs, cout = refs[n_in + n:n_in + n + n_out], refs[n_in + n + n_out:n_in + 2 * n + n_out]
        send, recv, loc = refs[n_in + 2 * n + n_out:]
        i = pl.program_id(0)

        @pl.when(i == 0)
        def _():
            for cp in _chip_copies(cin, cout, send, recv, loc, gather):
                cp.start()

        body(*ins, *outs)

        @pl.when(i == pl.num_programs(0) - 1)
        def _():
            for cp in _chip_copies(cin, cout, send, recv, loc, gather):
                cp.wait()

    specs = dict(
        in_specs=[HBM] * n, out_specs=[HBM] * n,
        out_shape=[jax.ShapeDtypeStruct(((4,) + a.shape) if gather else a.shape, a.dtype) for a in carry],
        scratch_shapes=[pltpu.SemaphoreType.DMA((3 * n,)), pltpu.SemaphoreType.DMA((3 * n,)),
                        pltpu.SemaphoreType.DMA((n,))])
    return wrapped, specs


def _inproj(x, g, w, carry=()):
    S = x.shape[0]
    tm = _tile(S, 256)

    def body(x_ref, g_ref, w_ref, proj_ref, xn_ref):
        h = _rms(x_ref[...], g_ref[...]).astype(BF16)
        xn_ref[...] = h
        proj_ref[...] = _dot(h, w_ref[...]).astype(BF16)

    row = lambda n: pl.BlockSpec((tm, n), lambda i: (i, 0))
    in_specs = [row(D_MODEL), pl.BlockSpec((1, D_MODEL), lambda i: (0, 0)), pl.BlockSpec((D_MODEL, PW), lambda i: (0, 0))]
    out_specs = [row(PW), row(D_MODEL)]
    out_shape = [jax.ShapeDtypeStruct((S, PW), BF16), jax.ShapeDtypeStruct((S, D_MODEL), BF16)]
    if not carry:
        proj, xn = pl.pallas_call(
            body, name="inproj", grid=(S // tm,), in_specs=in_specs, out_specs=out_specs, out_shape=out_shape,
            compiler_params=_params(("parallel",)))(x, g, w)
        return proj, xn, []
    wrapped, extra = _carried(body, 3, 2, carry, gather=True)
    res = pl.pallas_call(
        wrapped, name="inproj_gather", grid=(S // tm,), in_specs=in_specs + extra["in_specs"],
        out_specs=out_specs + extra["out_specs"], out_shape=out_shape + extra["out_shape"],
        scratch_shapes=extra["scratch_shapes"], compiler_params=_params(("arbitrary",)))(x, g, w, *carry)
    return res[0], res[1], list(res[2:])


def _mla_prep(proj, tabs, qg, kvg, qhg, khg, wuq, wuk, wuv):
    S = proj.shape[0]
    tm = _tile(S, 512)

    def body(a_ref, c_ref, sa_ref, sb_ref, qg_ref, kvg_ref, qhg_ref, khg_ref, wuq_ref, wuk_ref, wuv_ref,
             q_ref, k_ref, v_ref, qn_ref, kvn_ref, kmax_ref, qt_ref, vt_ref):
        @pl.when(pl.program_id(0) == 0)
        def _():
            kmax_ref[...] = jnp.zeros(kmax_ref.shape, F32)

        ql = a_ref[:, 0:Q_RANK].astype(F32)
        kvl = a_ref[:, Q_RANK:Q_RANK + KV_RANK].astype(F32)
        kpe = a_ref[:, Q_RANK + KV_RANK:Q_RANK + KV_RANK + HP].astype(F32)
        qn = _rms(ql, qg_ref[...]).astype(BF16)
        kvn = _rms(kvl, kvg_ref[...]).astype(BF16)
        qn_ref[...] = qn
        kvn_ref[...] = kvn
        qraw = _dot(qn, wuq_ref[...])
        kn = _dot(kvn, wuk_ref[...])
        vf = _dot(kvn, wuv_ref[...])
        v_ref[...] = vf.astype(BF16)
        c, sa, sb = c_ref[...], sa_ref[...], sb_ref[...]
        for h in range(N_HEADS):
            sl = slice(h * HP, (h + 1) * HP)
            tq = _rms(qraw[:, sl], qhg_ref[...], QK_DIM)
            qh = _rope(tq, c, sa, sb) * SCALE
            q_ref[:, sl] = qh.astype(BF16)
            qt_ref[sl, :] = qh.T.astype(BF16)
            vt_ref[sl, :] = vf[:, sl].T.astype(BF16)
            tk = _rms(kn[:, sl] + kpe, khg_ref[...], QK_DIM)
            kb = _rope(tk, c, sa, sb).astype(BF16)
            k_ref[:, sl] = kb
            kf = kb.astype(F32)
            norm2 = jnp.max(jnp.sum(kf * kf, axis=-1, keepdims=True), axis=0, keepdims=True)
            kmax_ref[:, sl] = jnp.maximum(kmax_ref[:, sl], norm2)

    row = lambda w: pl.BlockSpec((tm, w), lambda i: (i, 0))
    full = lambda a: pl.BlockSpec(a.shape, lambda i: (0,) * a.ndim)
    return pl.pallas_call(
        body, name="mla_prep", grid=(S // tm,),
        in_specs=[pl.BlockSpec((tm, 1024), lambda i: (i, 7)), row(HP), row(HP), row(HP),
                  full(qg), full(kvg), full(qhg), full(khg), full(wuq), full(wuk), full(wuv)],
        out_specs=[row(1024), row(1024), row(1024), row(Q_RANK), row(KV_RANK),
                   pl.BlockSpec((1, 1024), lambda i: (0, 0)),
                   pl.BlockSpec((1024, tm), lambda i: (0, i)), pl.BlockSpec((1024, tm), lambda i: (0, i))],
        out_shape=[jax.ShapeDtypeStruct((S, 1024), BF16)] * 3
        + [jax.ShapeDtypeStruct((S, Q_RANK), BF16), jax.ShapeDtypeStruct((S, KV_RANK), BF16),
           jax.ShapeDtypeStruct((1, 1024), F32)] + [jax.ShapeDtypeStruct((1024, S), BF16)] * 2,
        compiler_params=_params(("arbitrary",)),
    )(proj, *tabs, qg, kvg, qhg, khg, wuq, wuk, wuv)


SAFE_SCORE_BOUND = 30.0


def _attn_fwd(qt, k, vt, kmax2):
    S = k.shape[0]
    tq, tk = _tile(S, 1024), _tile(S, 2048)
    nk = S // tk

    def body(qt_ref, k_ref, vt_ref, kmax_ref, o_ref, lse_ref, m_s, l_s, acc_s):
        qtv = qt_ref[...]
        qf = qtv.astype(F32)
        bound = jnp.sqrt(jnp.sum(qf * qf, axis=0, keepdims=True) * kmax_ref[0:1, 0:1]) * 1.001 + 1e-6
        safe = jnp.max(bound) <= SAFE_SCORE_BOUND
        l_s[...] = jnp.zeros(l_s.shape, F32)
        acc_s[...] = jnp.zeros(acc_s.shape, F32)

        def keys(c):
            return pl.ds(pl.multiple_of(c * tk, tk), tk)

        @pl.when(safe)
        def _():
            def step(c, carry):
                pt = jnp.exp(_dot(k_ref[keys(c), :], qtv) - bound)
                l_s[...] += jnp.sum(pt, axis=0, keepdims=True)
                acc_s[...] += _dot(vt_ref[:, keys(c)], pt.astype(BF16))
                return carry

            lax.fori_loop(0, nk, step, 0)
            m_s[...] = bound

        @pl.when(jnp.logical_not(safe))
        def _():
            m_s[...] = jnp.full(m_s.shape, -jnp.inf, F32)

            def step(c, carry):
                st = _dot(k_ref[keys(c), :], qtv)
                m_prev = m_s[...]
                m_new = jnp.maximum(m_prev, jnp.max(st, axis=0, keepdims=True))
                alpha = jnp.exp(m_prev - m_new)
                pt = jnp.exp(st - m_new)
                l_s[...] = alpha * l_s[...] + jnp.sum(pt, axis=0, keepdims=True)
                acc_s[...] = alpha * acc_s[...] + _dot(vt_ref[:, keys(c)], pt.astype(BF16))
                m_s[...] = m_new
                return carry

            lax.fori_loop(0, nk, step, 0)

        o_ref[...] = (acc_s[...] / l_s[...]).T
        lse_row = m_s[...] + jnp.log(l_s[...])
        lse_ref[0] = jnp.broadcast_to(lse_row, (HP, tq)).T[:, 0:1]

    return pl.pallas_call(
        body, name="attn_fwd", grid=(N_HEADS, S // tq),
        in_specs=[pl.BlockSpec((HP, tq), lambda h, i: (h, i)),
                  pl.BlockSpec((S, HP), lambda h, i: (0, h)),
                  pl.BlockSpec((HP, S), lambda h, i: (h, 0)),
                  pl.BlockSpec((1, HP), lambda h, i: (0, h))],
        out_specs=[pl.BlockSpec((tq, HP), lambda h, i: (i, h)),
                   pl.BlockSpec((1, tq, 1), lambda h, i: (h, i, 0))],
        out_shape=[jax.ShapeDtypeStruct((S, N_HEADS * HP), F32),
                   jax.ShapeDtypeStruct((N_HEADS, S, 1), F32)],
        scratch_shapes=[pltpu.VMEM((1, tq), F32), pltpu.VMEM((1, tq), F32), pltpu.VMEM((HP, tq), F32)],
        compiler_params=_params(("parallel", "parallel")),
    )(qt, k, vt, kmax2)


def _memkv(mem, mng, mkg, wmkv):
    M = mem.shape[0]

    def body(mem_ref, mng_ref, mkg_ref, w_ref, mk_ref, mv_ref):
        mn = _rms(mem_ref[...], mng_ref[...]).astype(BF16)
        mkv = _dot(mn, w_ref[...])
        for h in range(MEM_HEADS):
            kraw = mkv[:, 2 * MEM_HD * h:2 * MEM_HD * h + MEM_HD]
            mk_ref[:, MEM_HD * h:MEM_HD * (h + 1)] = _rms(kraw, mkg_ref[...]).astype(BF16)
            mv_ref[:, MEM_HD * h:MEM_HD * (h + 1)] = mkv[:, 2 * MEM_HD * h + MEM_HD:2 * MEM_HD * (h + 1)].astype(BF16)

    return pl.pallas_call(
        body, name="memkv",
        out_shape=[jax.ShapeDtypeStruct((M, MEM_W), BF16)] * 2,
        compiler_params=pltpu.CompilerParams(vmem_limit_bytes=VMEM_LIMIT_V7X),
    )(mem, mng, mkg, wmkv)


def _conv_shifts(cc, cu, hp_ref, hn_ref, i, n_tiles, tm):
    z = cc * cu
    last = PROJ_HALO - 1
    zp = hp_ref[last:last + 1, 0:CONV_W].astype(F32) * hp_ref[last:last + 1, CONV_W:2 * CONV_W].astype(F32)
    zn = hn_ref[0:1, 0:CONV_W].astype(F32) * hn_ref[0:1, CONV_W:2 * CONV_W].astype(F32)
    zp = jnp.where(i == 0, 0.0, zp)
    zn = jnp.where(i == n_tiles - 1, 0.0, zn)
    row = lax.broadcasted_iota(jnp.int32, z.shape, 0)
    z_up = jnp.where(row == 0, zp, pltpu.roll(z, 1, 0))
    z_dn = jnp.where(row == tm - 1, zn, pltpu.roll(z, tm - 1, 0))
    return z, z_up, z_dn


def _halo_specs(tm, S, width, col, rows):
    per = tm // rows
    prev = pl.BlockSpec((rows, width), lambda i: (jnp.maximum(i * per - 1, 0), col))
    nxt = pl.BlockSpec((rows, width), lambda i: (jnp.minimum((i + 1) * per, S // rows - 1), col))
    return prev, nxt


def _mem_attend(qm, mqg, mk_h, mv_h):
    r, qhat = _rms_parts(qm)
    mq = (qhat * mqg).astype(BF16)
    s = _dot_nt(mq, mk_h) * MEM_SCALE
    e = jnp.exp(s - jnp.max(s, axis=-1, keepdims=True))
    p = e / jnp.sum(e, axis=-1, keepdims=True)
    pv = _dot(p.astype(BF16), mv_h)
    return r, qhat, mq, p, pv


def _merge(proj, o, x, bg, convw, mqg, mk, mv, wba, wbc, wbm, wo):
    S = x.shape[0]
    tm = _tile(S, 512)
    nt = S // tm

    def body(main_ref, ccu_ref, hp_ref, hn_ref, o_ref, x_ref, bg_ref, cw_ref, mqg_ref, mk_ref, mv_ref,
             wba_ref, wbc_ref, wbm_ref, wo_ref, xn_ref, oa_ref, oc_ref, om_ref, u_ref, y_ref):
        i = pl.program_id(0)
        sil_a, _ = _silu_and_grad(main_ref[:, O_GA:O_GA + 1024].astype(F32))
        oa = (o_ref[...] * sil_a).astype(BF16)
        oa_ref[...] = oa
        z, z_up, z_dn = _conv_shifts(ccu_ref[:, 0:CONV_W].astype(F32), ccu_ref[:, CONV_W:].astype(F32), hp_ref, hn_ref, i, nt, tm)
        cv = cw_ref[0:1, :] * z_up + cw_ref[1:2, :] * z + cw_ref[2:3, :] * z_dn + cw_ref[3:4, :]
        sil_c, _ = _silu_and_grad(main_ref[:, O_GC:O_GC + CONV_W].astype(F32))
        oc = (main_ref[:, O_CB:O_CB + CONV_W].astype(F32) * cv * sil_c).astype(BF16)
        oc_ref[...] = oc
        sil_m, _ = _silu_and_grad(main_ref[:, O_GM:O_GM + MEM_W].astype(F32))
        for h in range(MEM_HEADS):
            sl = slice(h * MEM_HD, (h + 1) * MEM_HD)
            qm = main_ref[:, O_QM + h * MEM_HD:O_QM + (h + 1) * MEM_HD].astype(F32)
            pv = _mem_attend(qm, mqg_ref[...], mk_ref[:, sl], mv_ref[:, sl])[4]
            om_ref[:, sl] = (pv * sil_m[:, sl]).astype(BF16)
        ua = _dot(oa, wba_ref[...])
        uc = _dot(oc, wbc_ref[...])
        um = _dot(om_ref[...], wbm_ref[...])
        u_ref[:, 0:1024] = ua.astype(BF16)
        u_ref[:, 1024:2048] = uc.astype(BF16)
        u_ref[:, 2048:3072] = um.astype(BF16)
        rg = _sigmoid(main_ref[:, O_R:O_R + 3072].astype(F32) + bg_ref[...])
        y = (rg[:, 0:1024] * ua + rg[:, 1024:2048] * uc + rg[:, 2048:3072] * um).astype(BF16)
        y_ref[...] = y
        xn_ref[...] = x_ref[...] + _dot(y, wo_ref[...])

    row = lambda w: pl.BlockSpec((tm, w), lambda i: (i, 0))
    full = lambda a: pl.BlockSpec(a.shape, lambda i: (0,) * a.ndim)
    hp, hn = _halo_specs(tm, S, 1024, 6, PROJ_HALO)
    return pl.pallas_call(
        body, name="merge", grid=(nt,),
        in_specs=[row(6144), pl.BlockSpec((tm, 1024), lambda i: (i, 6)), hp, hn, row(1024), row(1024),
                  full(bg), full(convw), full(mqg), full(mk), full(mv), full(wba), full(wbc), full(wbm), full(wo)],
        out_specs=[row(1024), row(1024), row(CONV_W), row(MEM_W), row(3072), row(1024)],
        out_shape=[jax.ShapeDtypeStruct((S, 1024), F32), jax.ShapeDtypeStruct((S, 1024), BF16),
                   jax.ShapeDtypeStruct((S, CONV_W), BF16), jax.ShapeDtypeStruct((S, MEM_W), BF16),
                   jax.ShapeDtypeStruct((S, 3072), BF16), jax.ShapeDtypeStruct((S, 1024), BF16)],
        compiler_params=_params(("parallel",)),
    )(proj, proj, proj, proj, o, x, bg, convw, mqg, mk, mv, wba, wbc, wbm, wo)


def _loss_head(xf, tgt):
    S = xf.shape[0]
    tm = _tile(S, 1024)

    def body(x_ref, t_ref, g_ref, acc_ref):
        @pl.when(pl.program_id(0) == 0)
        def _():
            acc_ref[...] = jnp.zeros(acc_ref.shape, F32)

        e = x_ref[...] - t_ref[...]
        g_ref[...] = e * (1.0 / D_MODEL)
        part = jnp.sum((e * e).reshape(tm // 8, 8, D_MODEL), axis=0)
        tot = part[:, 0:128]
        for k in range(1, D_MODEL // 128):
            tot = tot + part[:, 128 * k:128 * (k + 1)]
        acc_ref[...] += tot

    row = pl.BlockSpec((tm, D_MODEL), lambda i: (i, 0))
    return pl.pallas_call(
        body, name="loss_head", grid=(S // tm,),
        in_specs=[row, row],
        out_specs=[row, pl.BlockSpec((8, 128), lambda i: (0, 0))],
        out_shape=[jax.ShapeDtypeStruct((S, D_MODEL), F32), jax.ShapeDtypeStruct((8, 128), F32)],
        compiler_params=_params(("arbitrary",)),
    )(xf, tgt)


def _merge_bwd(g, proj, o, u, bg, convw, mqg, mk, mv, wot, wbat, wbct, wbmt):
    S = g.shape[0]
    tm = _tile(S, 256)
    nt = S // tm
    M = mk.shape[0]

    def body(g_ref, main_ref, ccu_ref, hp_ref, hn_ref, o_ref, u_ref, bg_ref, cw_ref, mqg_ref, mk_ref, mv_ref,
             wot_ref, wbat_ref, wbct_ref, wbmt_ref,
             dmain_ref, dcv_ref, do_ref, delta_ref, du_ref, dbg_ref, dmk_ref, dmv_ref, dmqg_ref, dot_ref):
        i = pl.program_id(0)

        @pl.when(i == 0)
        def _():
            dbg_ref[...] = jnp.zeros(dbg_ref.shape, F32)
            dmk_ref[...] = jnp.zeros(dmk_ref.shape, F32)
            dmv_ref[...] = jnp.zeros(dmv_ref.shape, F32)
            dmqg_ref[...] = jnp.zeros(dmqg_ref.shape, F32)

        dy = _dot(g_ref[...].astype(BF16), wot_ref[...])
        rg = _sigmoid(main_ref[:, O_R:O_R + 3072].astype(F32) + bg_ref[...])
        dyt = jnp.concatenate([dy, dy, dy], axis=1)
        dr = dyt * u_ref[...].astype(F32) * rg * (1.0 - rg)
        dmain_ref[:, O_R:O_R + 3072] = dr.astype(BF16)
        dbg_ref[...] += _colsum(dr)
        du = (dyt * rg).astype(BF16)
        du_ref[...] = du
        do_a = _dot(du[:, 0:1024], wbat_ref[...])
        do_c = _dot(du[:, 1024:2048], wbct_ref[...])
        do_m = _dot(du[:, 2048:3072], wbmt_ref[...])

        sil_a, dsil_a = _silu_and_grad(main_ref[:, O_GA:O_GA + 1024].astype(F32))
        ov = o_ref[...]
        d_o = do_a * sil_a
        do_ref[...] = d_o.astype(BF16)
        dot_ref[...] = d_o.T.astype(BF16)
        dmain_ref[:, O_GA:O_GA + 1024] = (do_a * ov * dsil_a).astype(BF16)
        prod = d_o * ov
        for h in range(N_HEADS):
            delta_ref[h] = jnp.sum(prod[:, h * HP:(h + 1) * HP], axis=-1, keepdims=True)

        z, z_up, z_dn = _conv_shifts(ccu_ref[:, 0:CONV_W].astype(F32), ccu_ref[:, CONV_W:].astype(F32), hp_ref, hn_ref, i, nt, tm)
        cv = cw_ref[0:1, :] * z_up + cw_ref[1:2, :] * z + cw_ref[2:3, :] * z_dn + cw_ref[3:4, :]
        sil_c, dsil_c = _silu_and_grad(main_ref[:, O_GC:O_GC + CONV_W].astype(F32))
        cb = main_ref[:, O_CB:O_CB + CONV_W].astype(F32)
        dmain_ref[:, O_CB:O_CB + CONV_W] = (do_c * cv * sil_c).astype(BF16)
        dmain_ref[:, O_GC:O_GC + CONV_W] = (do_c * cb * cv * dsil_c).astype(BF16)
        dcv_ref[...] = do_c * cb * sil_c

        sil_m, dsil_m = _silu_and_grad(main_ref[:, O_GM:O_GM + MEM_W].astype(F32))
        for h in range(MEM_HEADS):
            sl = slice(h * MEM_HD, (h + 1) * MEM_HD)
            qm = main_ref[:, O_QM + h * MEM_HD:O_QM + (h + 1) * MEM_HD].astype(F32)
            mk_h, mv_h = mk_ref[:, sl], mv_ref[:, sl]
            r, qhat, mq, p, pv = _mem_attend(qm, mqg_ref[...], mk_h, mv_h)
            dom = do_m[:, sl]
            dmain_ref[:, O_GM + h * MEM_HD:O_GM + (h + 1) * MEM_HD] = (dom * pv * dsil_m[:, sl]).astype(BF16)
            dpv = (dom * sil_m[:, sl]).astype(BF16)
            dp = _dot_nt(dpv, mv_h)
            ds = (p * (dp - jnp.sum(dp * p, axis=-1, keepdims=True)) * MEM_SCALE).astype(BF16)
            dmq = _dot(ds, mk_h)
            dmk_ref[:, sl] += _dot_tn(ds, mq)
            dmv_ref[:, sl] += _dot_tn(p.astype(BF16), dpv)
            dmqg_ref[...] += _colsum(dmq * qhat)
            dqm = _rms_bwd(dmq * mqg_ref[...], qhat, r, MEM_HD)
            dmain_ref[:, O_QM + h * MEM_HD:O_QM + (h + 1) * MEM_HD] = dqm.astype(BF16)

    row = lambda w: pl.BlockSpec((tm, w), lambda i: (i, 0))
    full = lambda a: pl.BlockSpec(a.shape, lambda i: (0,) * a.ndim)
    acc = lambda r, c: pl.BlockSpec((r, c), lambda i: (0, 0))
    hp, hn = _halo_specs(tm, S, 1024, 6, PROJ_HALO)
    return pl.pallas_call(
        body, name="merge_bwd", grid=(nt,),
        in_specs=[row(1024), row(6144), pl.BlockSpec((tm, 1024), lambda i: (i, 6)), hp, hn, row(1024), row(3072),
                  full(bg), full(convw), full(mqg), full(mk), full(mv), full(wot), full(wbat), full(wbct), full(wbmt)],
        out_specs=[row(6144), row(CONV_W), row(1024), pl.BlockSpec((N_HEADS, tm, 1), lambda i: (0, i, 0)), row(3072),
                   acc(1, 3072), acc(M, MEM_W), acc(M, MEM_W), acc(1, MEM_HD),
                   pl.BlockSpec((1024, tm), lambda i: (0, i))],
        out_shape=[jax.ShapeDtypeStruct((S, 6144), BF16), jax.ShapeDtypeStruct((S, CONV_W), F32),
                   jax.ShapeDtypeStruct((S, 1024), BF16), jax.ShapeDtypeStruct((N_HEADS, S, 1), F32),
                   jax.ShapeDtypeStruct((S, 3072), BF16), jax.ShapeDtypeStruct((1, 3072), F32),
                   jax.ShapeDtypeStruct((M, MEM_W), F32), jax.ShapeDtypeStruct((M, MEM_W), F32),
                   jax.ShapeDtypeStruct((1, MEM_HD), F32), jax.ShapeDtypeStruct((1024, S), BF16)],
        compiler_params=_params(("arbitrary",)),
    )(g, proj, proj, proj, proj, o, u, bg, convw, mqg, mk, mv, wot, wbat, wbct, wbmt)


def _conv_bwd(dcv, proj, convw):
    S = dcv.shape[0]
    tm = _tile(S, 512)
    nt = S // tm

    def body(d_ref, dp_ref, dn_ref, ccu_ref, hp_ref, hn_ref, cw_ref, dccu_ref, dcw_ref):
        i = pl.program_id(0)

        @pl.when(i == 0)
        def _():
            dcw_ref[...] = jnp.zeros(dcw_ref.shape, F32)

        cc, cu = ccu_ref[:, 0:CONV_W].astype(F32), ccu_ref[:, CONV_W:].astype(F32)
        z, z_up, z_dn = _conv_shifts(cc, cu, hp_ref, hn_ref, i, nt, tm)
        d = d_ref[...]
        dprev = jnp.where(i == 0, 0.0, dp_ref[7:8, :])
        dnext = jnp.where(i == nt - 1, 0.0, dn_ref[0:1, :])
        row = lax.broadcasted_iota(jnp.int32, d.shape, 0)
        d_up = jnp.where(row == 0, dprev, pltpu.roll(d, 1, 0))
        d_dn = jnp.where(row == tm - 1, dnext, pltpu.roll(d, tm - 1, 0))
        dz = cw_ref[0:1, :] * d_dn + cw_ref[1:2, :] * d + cw_ref[2:3, :] * d_up
        dccu_ref[:, 0:CONV_W] = (dz * cu).astype(BF16)
        dccu_ref[:, CONV_W:] = (dz * cc).astype(BF16)
        dcw_ref[0:1, :] += _colsum(d * z_up)
        dcw_ref[1:2, :] += _colsum(d * z)
        dcw_ref[2:3, :] += _colsum(d * z_dn)
        dcw_ref[3:4, :] += _colsum(d)

    hp, hn = _halo_specs(tm, S, 1024, 6, PROJ_HALO)
    dp, dn = _halo_specs(tm, S, CONV_W, 0, F32_HALO)
    return pl.pallas_call(
        body, name="conv_bwd", grid=(nt,),
        in_specs=[pl.BlockSpec((tm, CONV_W), lambda i: (i, 0)), dp, dn,
                  pl.BlockSpec((tm, 1024), lambda i: (i, 6)), hp, hn,
                  pl.BlockSpec((8, CONV_W), lambda i: (0, 0))],
        out_specs=[pl.BlockSpec((tm, 1024), lambda i: (i, 0)), pl.BlockSpec((8, CONV_W), lambda i: (0, 0))],
        out_shape=[jax.ShapeDtypeStruct((S, 1024), BF16), jax.ShapeDtypeStruct((8, CONV_W), F32)],
        compiler_params=_params(("arbitrary",)),
    )(dcv, dcv, dcv, proj, proj, proj, convw)


def _attn_bwd(q, k, v, do, qt, dot, lse, delta):
    S = q.shape[0]
    tq, tk = _tile(S, 512), _tile(S, 4096)
    ni, nk = S // tq, S // tk

    def body(q_ref, do_ref, qt_ref, dot_ref, k_ref, v_ref, lse_ref, delta_ref, dq_ref, dkt_hbm, dvt_hbm,
             dq_s, dkt_s, dvt_s, sem):
        h, i = pl.program_id(0), pl.program_id(1)

        @pl.when(i == 0)
        def _():
            dkt_s[...] = jnp.zeros(dkt_s.shape, F32)
            dvt_s[...] = jnp.zeros(dvt_s.shape, F32)

        dq_s[...] = jnp.zeros(dq_s.shape, F32)
        qv, dov, qtv, dotv = q_ref[...], do_ref[...], qt_ref[...], dot_ref[...]
        lse_c, delta_c = lse_ref[0], delta_ref[0]

        def step(c, carry):
            cols = pl.ds(pl.multiple_of(c * tk, tk), tk)
            kc, vc = k_ref[cols, :], v_ref[cols, :]
            p = jnp.exp(_dot_nt(qv, kc) - lse_c)
            dp = _dot_nt(dov, vc)
            ds = (p * (dp - delta_c)).astype(BF16)
            dq_s[...] += _dot(ds, kc)
            dvt_s[:, cols] += _dot(dotv, p.astype(BF16))
            dkt_s[:, cols] += _dot(qtv, ds)
            return carry

        lax.fori_loop(0, nk, step, 0)
        dq_ref[...] = dq_s[...]

        @pl.when(i == ni - 1)
        def _():
            head = pl.ds(pl.multiple_of(h * HP, HP), HP)
            out_k = pltpu.make_async_copy(dkt_s, dkt_hbm.at[head, :], sem.at[0])
            out_v = pltpu.make_async_copy(dvt_s, dvt_hbm.at[head, :], sem.at[1])
            out_k.start()
            out_v.start()
            out_k.wait()
            out_v.wait()

    col = pl.BlockSpec((1, tq, 1), lambda h, i: (h, i, 0))
    blk = pl.BlockSpec((tq, HP), lambda h, i: (i, h))
    blkt = pl.BlockSpec((HP, tq), lambda h, i: (h, i))
    res = pl.BlockSpec((S, HP), lambda h, i: (0, h))
    whole = pl.BlockSpec(memory_space=pl.ANY)
    return pl.pallas_call(
        body, name="attn_bwd", grid=(N_HEADS, ni),
        in_specs=[blk, blk, blkt, blkt, res, res, col, col],
        out_specs=[blk, whole, whole],
        out_shape=[jax.ShapeDtypeStruct((S, N_HEADS * HP), F32), jax.ShapeDtypeStruct((N_HEADS * HP, S), F32),
                   jax.ShapeDtypeStruct((N_HEADS * HP, S), F32)],
        scratch_shapes=[pltpu.VMEM((tq, HP), F32), pltpu.VMEM((HP, S), F32), pltpu.VMEM((HP, S), F32),
                        pltpu.SemaphoreType.DMA((2,))],
        compiler_params=_params(("parallel", "arbitrary")),
    )(q, do, qt, dot, k, v, lse, delta)


def _mla_prep_bwd(proj, tabs, dq, dk, dv, qg, kvg, qhg, khg, wuq, wuk, wuqt, wukt, wuvt):
    S = proj.shape[0]
    tm = _tile(S, 512)

    def body(a_ref, c_ref, sa_ref, sb_ref, dq_ref, dk_ref, dv_ref, qg_ref, kvg_ref, qhg_ref, khg_ref,
             wuq_ref, wuk_ref, wuqt_ref, wukt_ref, wuvt_ref,
             da_ref, dqraw_ref, dkraw_ref, dqg_ref, dkvg_ref, dqhg_ref, dkhg_ref):
        @pl.when(pl.program_id(0) == 0)
        def _():
            dqg_ref[...] = jnp.zeros(dqg_ref.shape, F32)
            dkvg_ref[...] = jnp.zeros(dkvg_ref.shape, F32)
            dqhg_ref[...] = jnp.zeros(dqhg_ref.shape, F32)
            dkhg_ref[...] = jnp.zeros(dkhg_ref.shape, F32)

        ql = a_ref[:, 0:Q_RANK].astype(F32)
        kvl = a_ref[:, Q_RANK:Q_RANK + KV_RANK].astype(F32)
        kpe = a_ref[:, Q_RANK + KV_RANK:Q_RANK + KV_RANK + HP].astype(F32)
        rq, qhat = _rms_parts(ql)
        rkv, kvhat = _rms_parts(kvl)
        qraw = _dot((qhat * qg_ref[...]).astype(BF16), wuq_ref[...])
        kn = _dot((kvhat * kvg_ref[...]).astype(BF16), wuk_ref[...])
        c, sa, sb = c_ref[...], sa_ref[...], sb_ref[...]
        dkpe = jnp.zeros(kpe.shape, F32)
        dqhg = jnp.zeros((1, HP), F32)
        dkhg = jnp.zeros((1, HP), F32)
        for h in range(N_HEADS):
            sl = slice(h * HP, (h + 1) * HP)
            r, that = _rms_parts(qraw[:, sl], QK_DIM)
            dtn = _rope_t(dq_ref[:, sl], c, sa, sb) * SCALE
            dqhg = dqhg + _colsum(dtn * that)
            dqraw_ref[:, sl] = _rms_bwd(dtn * qhg_ref[...], that, r, QK_DIM).astype(BF16)
            r, that = _rms_parts(kn[:, sl] + kpe, QK_DIM)
            dtn = _rope_t(dk_ref[:, sl], c, sa, sb)
            dkhg = dkhg + _colsum(dtn * that)
            dkr = _rms_bwd(dtn * khg_ref[...], that, r, QK_DIM)
            dkraw_ref[:, sl] = dkr.astype(BF16)
            dkpe = dkpe + dkr
        dqhg_ref[...] += dqhg
        dkhg_ref[...] += dkhg
        dqn = _dot(dqraw_ref[...], wuqt_ref[...])
        dqg_ref[...] += _colsum(dqn * qhat)
        da_ref[:, 0:Q_RANK] = _rms_bwd(dqn * qg_ref[...], qhat, rq, Q_RANK).astype(BF16)
        dkvn = _dot(dkraw_ref[...], wukt_ref[...]) + _dot(dv_ref[...], wuvt_ref[...])
        dkvg_ref[...] += _colsum(dkvn * kvhat)
        da_ref[:, Q_RANK:Q_RANK + KV_RANK] = _rms_bwd(dkvn * kvg_ref[...], kvhat, rkv, KV_RANK).astype(BF16)
        da_ref[:, Q_RANK + KV_RANK:Q_RANK + KV_RANK + HP] = dkpe.astype(BF16)
        da_ref[:, Q_RANK + KV_RANK + HP:] = jnp.zeros((tm, 1024 - Q_RANK - KV_RANK - HP), BF16)

    row = lambda w: pl.BlockSpec((tm, w), lambda i: (i, 0))
    full = lambda a: pl.BlockSpec(a.shape, lambda i: (0,) * a.ndim)
    acc = lambda c: pl.BlockSpec((1, c), lambda i: (0, 0))
    return pl.pallas_call(
        body, name="mla_prep_bwd", grid=(S // tm,),
        in_specs=[pl.BlockSpec((tm, 1024), lambda i: (i, 7)), row(HP), row(HP), row(HP),
                  row(1024), row(1024), row(1024), full(qg), full(kvg), full(qhg), full(khg),
                  full(wuq), full(wuk), full(wuqt), full(wukt), full(wuvt)],
        out_specs=[row(1024), row(1024), row(1024), acc(Q_RANK), acc(KV_RANK), acc(HP), acc(HP)],
        out_shape=[jax.ShapeDtypeStruct((S, 1024), BF16)] * 3
        + [jax.ShapeDtypeStruct((1, Q_RANK), F32), jax.ShapeDtypeStruct((1, KV_RANK), F32),
           jax.ShapeDtypeStruct((1, HP), F32), jax.ShapeDtypeStruct((1, HP), F32)],
        compiler_params=_params(("arbitrary",)),
    )(proj, *tabs, dq, dk, dv, qg, kvg, qhg, khg, wuq, wuk, wuqt, wukt, wuvt)


def _inproj_bwd(dmain, dccu, dsega, wint, x, g, ng, carry=()):
    S = x.shape[0]
    tm = _tile(S, 256)
    nm, nc = dmain.shape[1], dccu.shape[1]
    assert nm + nc + dsega.shape[1] == PW

    def body(dm_ref, dc_ref, da_ref, w_ref, x_ref, g_ref, ng_ref, dx_ref, dng_ref):
        @pl.when(pl.program_id(0) == 0)
        def _():
            dng_ref[...] = jnp.zeros(dng_ref.shape, F32)

        dh = (_dot(dm_ref[...], w_ref[0:nm, :]) + _dot(dc_ref[...], w_ref[nm:nm + nc, :])
              + _dot(da_ref[...], w_ref[nm + nc:PW, :]))
        r, xhat = _rms_parts(x_ref[...])
        dng_ref[...] += _colsum(dh * xhat)
        dx_ref[...] = g_ref[...] + _rms_bwd(dh * ng_ref[...], xhat, r, D_MODEL)

    row = lambda w: pl.BlockSpec((tm, w), lambda i: (i, 0))
    in_specs = [row(nm), row(nc), row(dsega.shape[1]), pl.BlockSpec((PW, D_MODEL), lambda i: (0, 0)),
                row(D_MODEL), row(D_MODEL), pl.BlockSpec((1, D_MODEL), lambda i: (0, 0))]
    out_specs = [row(D_MODEL), pl.BlockSpec((1, D_MODEL), lambda i: (0, 0))]
    out_shape = [jax.ShapeDtypeStruct((S, D_MODEL), F32), jax.ShapeDtypeStruct((1, D_MODEL), F32)]
    args = (dmain, dccu, dsega, wint, x, g, ng)
    if not carry:
        dx, dng = pl.pallas_call(
            body, name="inproj_bwd", grid=(S // tm,), in_specs=in_specs, out_specs=out_specs, out_shape=out_shape,
            compiler_params=_params(("arbitrary",)))(*args)
        return dx, dng, []
    wrapped, extra = _carried(body, 7, 2, carry, gather=False)
    res = pl.pallas_call(
        wrapped, name="inproj_bwd_scatter", grid=(S // tm,), in_specs=in_specs + extra["in_specs"],
        out_specs=out_specs + extra["out_specs"], out_shape=out_shape + extra["out_shape"],
        scratch_shapes=extra["scratch_shapes"], compiler_params=_params(("arbitrary",)))(*args, *carry)
    return res[0], res[1], list(res[2:])


def _memkv_bwd(mem, mng, mkg, wmkv, wmkvt, dmk, dmv):
    M = mem.shape[0]

    def body(mem_ref, mng_ref, mkg_ref, w_ref, wt_ref, dmk_ref, dmv_ref, dw_ref, dmng_ref, dmkg_ref, d_s):
        r, mhat = _rms_parts(mem_ref[...])
        mn = (mhat * mng_ref[...]).astype(BF16)
        mkv = _dot(mn, w_ref[...])
        dmkg = jnp.zeros((1, MEM_HD), F32)
        for h in range(MEM_HEADS):
            sl = slice(h * MEM_HD, (h + 1) * MEM_HD)
            rk, khat = _rms_parts(mkv[:, 2 * MEM_HD * h:2 * MEM_HD * h + MEM_HD])
            dkn = dmk_ref[:, sl]
            dmkg = dmkg + _colsum(dkn * khat)
            d_s[:, 2 * MEM_HD * h:2 * MEM_HD * h + MEM_HD] = _rms_bwd(dkn * mkg_ref[...], khat, rk, MEM_HD).astype(BF16)
            d_s[:, 2 * MEM_HD * h + MEM_HD:2 * MEM_HD * (h + 1)] = dmv_ref[:, sl].astype(BF16)
        dmkg_ref[...] = dmkg
        dw_ref[...] = _dot_tn(mn, d_s[...])
        dmn = _dot(d_s[...], wt_ref[...])
        dmng_ref[...] = _colsum(dmn * mhat)

    return pl.pallas_call(
        body, name="memkv_bwd",
        out_shape=[jax.ShapeDtypeStruct((D_MODEL, 2 * MEM_W), F32), jax.ShapeDtypeStruct((1, D_MODEL), F32),
                   jax.ShapeDtypeStruct((1, MEM_HD), F32)],
        scratch_shapes=[pltpu.VMEM((M, 2 * MEM_W), BF16)],
        compiler_params=pltpu.CompilerParams(vmem_limit_bytes=VMEM_LIMIT_V7X),
    )(mem, mng, mkg, wmkv, wmkvt, dmk, dmv)


def _mm_tn(a, b, name, col0=0, ncols=None):
    S, M = a.shape
    N = b.shape[1] if ncols is None else ncols
    tm, tn, ts = _tile(M, 1024), _tile(N, 1024), _tile(S, 2048)
    assert col0 % tn == 0
    jb = col0 // tn

    def body(a_ref, b_ref, o_ref):
        @pl.when(pl.program_id(2) == 0)
        def _():
            o_ref[...] = jnp.zeros(o_ref.shape, F32)

        o_ref[...] += _dot_tn(a_ref[...].astype(BF16), b_ref[...].astype(BF16))

    return pl.pallas_call(
        body, name=name, grid=(M // tm, N // tn, S // ts),
        in_specs=[pl.BlockSpec((ts, tm), lambda i, j, k: (k, i)),
                  pl.BlockSpec((ts, tn), lambda i, j, k: (k, j + jb))],
        out_specs=pl.BlockSpec((tm, tn), lambda i, j, k: (i, j)),
        out_shape=jax.ShapeDtypeStruct((M, N), F32),
        compiler_params=_params(("parallel", "parallel", "arbitrary")),
    )(a, b)


def _adamw(w, g0, g1, m, v, name):
    R, C = w.shape
    tr = R
    for cand in (512, 256, 128, 64, 32, 16, 8):
        if R % cand == 0 and cand * C * 4 <= (1 << 20):
            tr = cand
            break
    c1 = 1.0 / (1.0 - ADAM_B1 ** ADAM_STEP)
    c2 = 1.0 / (1.0 - ADAM_B2 ** ADAM_STEP)

    def body(w_ref, g0_ref, g1_ref, m_ref, v_ref, g_ref, d_ref, nm_ref, nv_ref):
        g = g0_ref[...] + g1_ref[...]
        nm = ADAM_B1 * m_ref[...] + (1.0 - ADAM_B1) * g
        nv = ADAM_B2 * v_ref[...] + (1.0 - ADAM_B2) * (g * g)
        g_ref[...] = g
        nm_ref[...] = nm
        nv_ref[...] = nv
        d_ref[...] = -ADAM_LR * ((nm * c1) / (jnp.sqrt(nv * c2) + ADAM_EPS) + ADAM_WD * w_ref[...])

    blk = pl.BlockSpec((tr, C), lambda i: (i, 0))
    return pl.pallas_call(
        body, name=name, grid=(R // tr,),
        in_specs=[blk] * 5, out_specs=[blk] * 4,
        out_shape=[jax.ShapeDtypeStruct((R, C), F32)] * 4,
        compiler_params=_params(("parallel",)),
    )(w, g0, g1, m, v)


def _sum_slabs(a, name):
    K, R, C = a.shape
    tr = R
    for cand in (512, 256, 128, 64, 32, 16, 8):
        if R % cand == 0 and cand * C * 4 * K <= (4 << 20):
            tr = cand
            break

    def body(a_ref, o_ref):
        t = a_ref[0].astype(F32)
        for k in range(1, K):
            t = t + a_ref[k].astype(F32)
        o_ref[...] = t

    return pl.pallas_call(
        body, name=name, grid=(R // tr,),
        in_specs=[pl.BlockSpec((K, tr, C), lambda i: (0, i, 0))],
        out_specs=pl.BlockSpec((tr, C), lambda i: (i, 0)),
        out_shape=jax.ShapeDtypeStruct((R, C), F32),
        compiler_params=_params(("parallel",)),
    )(a)


def _gather_chips(arrs):
    n = len(arrs)
    halves = [a.shape[0] // 2 for a in arrs]
    assert all(a.shape[0] == 2 * hf for a, hf in zip(arrs, halves))

    def body(*refs):
        ins, outs = refs[:n], refs[n:2 * n]
        send1, recv1, send2, recv2, loc = refs[2 * n:]
        x, y, c = lax.axis_index("x"), lax.axis_index("y"), lax.axis_index("c")
        me = 2 * x + y
        mine = [pl.ds(c * hf, hf) for hf in halves]
        theirs = [pl.ds((1 - c) * hf, hf) for hf in halves]
        sibling = (x, y, 1 - c)
        waits = []
        for a in range(n):
            own = pltpu.make_async_copy(ins[a], outs[a].at[me], loc.at[a])
            own.start()
            waits.append(own.wait)

        def over_ici(a, k, src_chip):
            return pltpu.make_async_remote_copy(
                src_ref=ins[a].at[mine[a]], dst_ref=outs[a].at[src_chip, mine[a]], send_sem=send1.at[3 * a + k - 1],
                recv_sem=recv1.at[3 * a + k - 1], device_id=_chip_peer(x, y, c, k), device_id_type=MESH)

        def to_sibling(a, k, layers):
            block = outs[a].at[me ^ k, layers]
            return pltpu.make_async_remote_copy(
                src_ref=block, dst_ref=block, send_sem=send2.at[3 * a + k - 1], recv_sem=recv2.at[3 * a + k - 1],
                device_id=sibling, device_id_type=MESH)

        for a in range(n):
            for k in (1, 2, 3):
                cp = over_ici(a, k, me)
                cp.start()
                waits.append(cp.wait_send)
        for a in range(n):
            for k in (1, 2, 3):
                over_ici(a, k, me ^ k).wait_recv()
                cp = to_sibling(a, k, mine[a])
                cp.start()
                waits.append(cp.wait_send)
        for a in range(n):
            for k in (1, 2, 3):
                to_sibling(a, k, theirs[a]).wait_recv()
        for w in waits:
            w()

    return pl.pallas_call(
        body, name="gather_weights",
        in_specs=[HBM] * n, out_specs=[HBM] * n,
        out_shape=[jax.ShapeDtypeStruct((4,) + a.shape, a.dtype) for a in arrs],
        scratch_shapes=[pltpu.SemaphoreType.DMA((3 * n,)), pltpu.SemaphoreType.DMA((3 * n,)),
                        pltpu.SemaphoreType.DMA((3 * n,)), pltpu.SemaphoreType.DMA((3 * n,)),
                        pltpu.SemaphoreType.DMA((n,))],
    )(*arrs)


def _scatter_chips(arrs, small):
    n = len(arrs)

    def body(*refs):
        ins, small_in = refs[:n], refs[n]
        outs, small_out = refs[n + 1:2 * n + 1], refs[2 * n + 1]
        send, recv, loc, ssend, srecv = refs[2 * n + 2:]
        x, y, c = lax.axis_index("x"), lax.axis_index("y"), lax.axis_index("c")
        me = 2 * x + y
        me8 = 4 * x + 2 * y + c
        copies = []
        for a in range(n):
            own = pltpu.make_async_copy(ins[a].at[me], outs[a].at[me], loc.at[a])
            own.start()
            copies.append(own)
        own = pltpu.make_async_copy(small_in, small_out.at[me8], loc.at[n])
        own.start()
        copies.append(own)
        for k in range(1, 8):
            cp = pltpu.make_async_remote_copy(
                src_ref=small_in, dst_ref=small_out.at[me8], send_sem=ssend.at[k - 1], recv_sem=srecv.at[k - 1],
                device_id=(x ^ (k >> 2), y ^ ((k >> 1) & 1), c ^ (k & 1)), device_id_type=MESH)
            cp.start()
            copies.append(cp)
        for a in range(n):
            for k in (1, 2, 3):
                cp = pltpu.make_async_remote_copy(
                    src_ref=ins[a].at[me ^ k], dst_ref=outs[a].at[me], send_sem=send.at[3 * a + k - 1],
                    recv_sem=recv.at[3 * a + k - 1], device_id=_chip_peer(x, y, c, k), device_id_type=MESH)
                cp.start()
                copies.append(cp)
        for cp in copies:
            cp.wait()

    return pl.pallas_call(
        body, name="scatter_grads",
        in_specs=[HBM] * (n + 1), out_specs=[HBM] * (n + 1),
        out_shape=[jax.ShapeDtypeStruct(a.shape, a.dtype) for a in arrs]
        + [jax.ShapeDtypeStruct((8,) + small.shape, small.dtype)],
        scratch_shapes=[pltpu.SemaphoreType.DMA((3 * n,)), pltpu.SemaphoreType.DMA((3 * n,)),
                        pltpu.SemaphoreType.DMA((n + 1,)), pltpu.SemaphoreType.DMA((7,)),
                        pltpu.SemaphoreType.DMA((7,))],
    )(*arrs, small)


def _swap_cores(arrs):
    n = len(arrs)

    def body(*refs):
        ins, outs = refs[:n], refs[n:2 * n]
        send, recv = refs[2 * n:]
        x, y, c = lax.axis_index("x"), lax.axis_index("y"), lax.axis_index("c")
        copies = []
        for a in range(n):
            cp = pltpu.make_async_remote_copy(
                src_ref=ins[a], dst_ref=outs[a], send_sem=send.at[a], recv_sem=recv.at[a],
                device_id=(x, y, 1 - c), device_id_type=MESH)
            cp.start()
            copies.append(cp)
        for cp in copies:
            cp.wait()

    return pl.pallas_call(
        body, name="swap_cores",
        in_specs=[HBM] * n, out_specs=[HBM] * n,
        out_shape=[jax.ShapeDtypeStruct(a.shape, a.dtype) for a in arrs],
        scratch_shapes=[pltpu.SemaphoreType.DMA((n,)), pltpu.SemaphoreType.DMA((n,))],
    )(*arrs)


def _pad_last(a, n):
    return jnp.pad(a, [(0, 0)] * (a.ndim - 1) + [(0, n - a.shape[-1])])


def _pad_w_in(w):
    lead = w.shape[:-1]
    seg = lambda a, b: w[..., a:b]
    ga = _pad_last(seg(2720, 3232).reshape(lead + (N_HEADS, V_DIM)), HP).reshape(lead + (1024,))
    kpe = jnp.pad(seg(640, 672), [(0, 0)] * len(lead) + [(NOPE_DIM, HP - QK_DIM)])
    zero = jnp.zeros(lead + (PW - 7936,), w.dtype)
    return jnp.concatenate(
        [seg(4256, 7328), ga, seg(672, 1184), seg(2208, 2720), seg(3232, 3744), seg(3744, 4256),
         seg(1184, 1696), seg(1696, 2208), seg(0, 384), seg(384, 640), kpe, zero], axis=-1)


def _unpad_w_in(w):
    lead = w.shape[:-1]
    seg = lambda a, n: w[..., a:a + n]
    ga = seg(O_GA, 1024).reshape(lead + (N_HEADS, HP))[..., :V_DIM].reshape(lead + (N_HEADS * V_DIM,))
    return jnp.concatenate(
        [seg(O_QL, 384), seg(O_KVL, 256), seg(O_KPE + NOPE_DIM, ROPE_DIM), seg(O_CB, 512), seg(O_CC, 512),
         seg(O_CU, 512), seg(O_QM, 512), ga, seg(O_GC, 512), seg(O_GM, 512), seg(O_R, 3072)], axis=-1)


def _cols_from_shards(g):
    _, L, R, C = g.shape
    return jnp.transpose(g, (1, 2, 0, 3)).reshape(L, R, 4 * C)


def _t(w):
    return jnp.swapaxes(w, -1, -2)


def _layer_fwd(x, mem, tabs, p, next_shards=()):
    proj, xn, gathered = _inproj(x, p["norm_g"], p["w_in"], next_shards)
    q, k, v, qn, kvn, kmax2, qt, vt = _mla_prep(proj, tabs, p["q_norm_g"], p["kv_norm_g"], p["q_head_g"], p["k_head_g"],
                                        p["w_uq"], p["w_uk"], p["w_uv"])
    o, lse = _attn_fwd(qt, k, vt, kmax2)
    mk, mv = _memkv(mem, p["mem_norm_g"], p["mem_k_g"], p["w_mkv"])
    x_new, oa, oc, om, u, y = _merge(proj, o, x, p["b_gate"], p["conv_wb"], p["mem_q_g"], mk, mv,
                                     p["w_br_attn"], p["w_br_conv"], p["w_br_mem"], p["w_out"])
    saved = dict(x=x, proj=proj, xn=xn, q=q, qt=qt, k=k, v=v, qn=qn, kvn=kvn, o=o, lse=lse, mk=mk, mv=mv,
                 oa=oa, oc=oc, om=om, u=u, y=y)
    return x_new, saved, gathered


def _layer_bwd(g, mem, tabs, p, s, to_owner=()):
    S = g.shape[0]
    dmain, dcv, d_o, delta, du, dbg, dmk, dmv, dmqg, d_ot = _merge_bwd(
        g, s["proj"], s["o"], s["u"], p["b_gate"], p["conv_wb"], p["mem_q_g"], s["mk"], s["mv"],
        p["w_out_t"], p["w_br_attn_t"], p["w_br_conv_t"], p["w_br_mem_t"])
    dccu, dconv = _conv_bwd(dcv, s["proj"], p["conv_wb"])
    dq, dkt, dvt = _attn_bwd(s["q"], s["k"], s["v"], d_o, s["qt"], d_ot, s["lse"], delta)
    dk, dv = _t(dkt), _t(dvt).astype(BF16)
    dsega, dqraw, dkraw, dqg, dkvg, dqhg, dkhg = _mla_prep_bwd(
        s["proj"], tabs, dq, dk, dv, p["q_norm_g"], p["kv_norm_g"], p["q_head_g"], p["k_head_g"],
        p["w_uq"], p["w_uk"], p["w_uq_t"], p["w_uk_t"], p["w_uv_t"])
    dx, dng, received = _inproj_bwd(dmain, dccu, dsega, p["w_in_t"], s["x"], g, p["norm_g"], to_owner)
    dwmkv, dmng, dmkg = _memkv_bwd(mem, p["mem_norm_g"], p["mem_k_g"], p["w_mkv"], p["w_mkv_t"], dmk, dmv)
    grads = dict(
        norm_g=dng, b_gate=dbg, q_norm_g=dqg, kv_norm_g=dkvg, q_head_g=dqhg, k_head_g=dkhg,
        conv_wb=dconv, mem_norm_g=dmng, mem_q_g=dmqg, mem_k_g=dmkg, w_mkv=dwmkv,
        w_in=jnp.concatenate([_mm_tn(s["xn"], dmain, "grad_w_in"), _mm_tn(s["xn"], dccu, "grad_w_in_conv"),
                              _mm_tn(s["xn"], dsega, "grad_w_in_lat")], axis=1),
        w_uq=_mm_tn(s["qn"], dqraw, "grad_w_uq"),
        w_uk=_mm_tn(s["kvn"], dkraw, "grad_w_uk"),
        w_uv=_mm_tn(s["kvn"], dv, "grad_w_uv"),
        w_br_attn=_mm_tn(s["oa"], du, "grad_w_br_attn", 0, 1024),
        w_br_conv=_mm_tn(s["oc"], du, "grad_w_br_conv", 1024, 1024),
        w_br_mem=_mm_tn(s["om"], du, "grad_w_br_mem", 2048, 1024),
        w_out=_mm_tn(s["y"], g, "grad_w_out"),
    )
    return dx, grads, received


def _layer_params(big, full, l):
    p = {}
    w_in = _pad_w_in(big["w_in"])
    w_uq = _pad_last(big["w_uq"].reshape(Q_RANK, N_HEADS, QK_DIM), HP).reshape(Q_RANK, 1024)
    ukv = big["w_ukv"].reshape(KV_RANK, N_HEADS, NOPE_DIM + V_DIM)
    w_uk = _pad_last(ukv[..., :NOPE_DIM], HP).reshape(KV_RANK, 1024)
    w_uv = _pad_last(ukv[..., NOPE_DIM:], HP).reshape(KV_RANK, 1024)
    w_ba = jnp.pad(big["w_br_attn"].reshape(N_HEADS, V_DIM, D_MODEL), ((0, 0), (0, HP - V_DIM), (0, 0)))
    w_ba = w_ba.reshape(1024, D_MODEL)
    p.update(w_in=w_in, w_uq=w_uq, w_uk=w_uk, w_uv=w_uv, w_br_attn=w_ba, w_br_conv=big["w_br_conv"],
             w_br_mem=big["w_br_mem"], w_out=big["w_out"], w_mkv=big["w_mkv"])
    for n in ("w_in", "w_uq", "w_uk", "w_uv", "w_br_attn", "w_br_conv", "w_br_mem", "w_out", "w_mkv"):
        p[n + "_t"] = _t(p[n])
    for n in ("norm_g", "b_gate", "q_norm_g", "kv_norm_g", "mem_norm_g", "mem_q_g", "mem_k_g"):
        p[n] = full[n][l][None, :]
    p["q_head_g"] = _pad_last(full["q_head_g"][l][None, :], HP)
    p["k_head_g"] = _pad_last(full["k_head_g"][l][None, :], HP)
    p["conv_wb"] = jnp.concatenate(
        [full["conv_w"][l], full["conv_b"][l][None, :], jnp.zeros((4, CONV_W), F32)], axis=0)
    return p


def _join_shards(name, g):
    _, R, C = g.shape
    if name in _COL_SHARDED:
        return jnp.transpose(g, (1, 0, 2)).reshape(R, 4 * C)
    return g.reshape(4 * R, C)


def _split_shards(name, w):
    R, C = w.shape
    if name in _COL_SHARDED:
        return jnp.transpose(w.reshape(R, 4, C // 4), (1, 0, 2)).astype(BF16)
    return w.reshape(4, R // 4, C).astype(BF16)


def _unpad_grads(gp):
    out = {"w_in": _unpad_w_in(gp["w_in"])}
    out["w_uq"] = gp["w_uq"].reshape(Q_RANK, N_HEADS, HP)[..., :QK_DIM].reshape(Q_RANK, N_HEADS * QK_DIM)
    duk = gp["w_uk"].reshape(KV_RANK, N_HEADS, HP)[..., :NOPE_DIM]
    duv = gp["w_uv"].reshape(KV_RANK, N_HEADS, HP)[..., :V_DIM]
    out["w_ukv"] = jnp.concatenate([duk, duv], axis=-1).reshape(KV_RANK, 1024)
    out["w_br_attn"] = gp["w_br_attn"].reshape(N_HEADS, HP, D_MODEL)[:, :V_DIM].reshape(512, D_MODEL)
    for n in ("w_br_conv", "w_br_mem", "w_out", "w_mkv"):
        out[n] = gp[n]
    return out


def _train_step(x, mem, positions, w, target):
    S = x.shape[0]
    invf16 = ROPE_BASE ** (-jnp.arange(0, ROPE_DIM, 2, dtype=F32) / ROPE_DIM)
    invf = jnp.concatenate([jnp.zeros((NOPE_DIM,), F32), invf16, invf16, jnp.zeros((HP - QK_DIM,), F32)])[None, :]
    tabs = _rope_tables(jnp.broadcast_to(positions.reshape(S, 1), (S, HP)), invf)
    shards = [[w[n][l].astype(BF16) for n in _BIG] for l in range(DEPTH)]
    first = _gather_chips(shards[0] + [w["conv_w"]])
    gathered = first[:-1]
    full = {n: w[n] for n in _SMALL}
    full["conv_w"] = _cols_from_shards(first[-1])
    params, saved = [], []
    h = x
    for l in range(DEPTH):
        big = {n: _join_shards(n, g) for n, g in zip(_BIG, gathered)}
        params.append(_layer_params(big, full, l))
        h, s, gathered = _layer_fwd(h, mem, tabs, params[l], shards[l + 1] if l + 1 < DEPTH else ())
        saved.append(s)
    g, loss_part = _loss_head(h, target)
    per_layer, received = [None] * DEPTH, [None] * DEPTH
    to_owner = ()
    for l in reversed(range(DEPTH)):
        g, per_layer[l], got = _layer_bwd(g, mem, tabs, params[l], saved[l], to_owner)
        if to_owner:
            received[l + 1] = got
        big_g = _unpad_grads(per_layer[l])
        to_owner = [_split_shards(n, big_g[n]) for n in _BIG]
    st = lambda n: jnp.stack([per_layer[l][n] for l in range(DEPTH)])
    small = {}
    for n in ("norm_g", "b_gate", "q_norm_g", "kv_norm_g", "mem_norm_g", "mem_q_g", "mem_k_g"):
        small[n] = st(n)[:, 0, :]
    small["q_head_g"] = st("q_head_g")[:, 0, :QK_DIM]
    small["k_head_g"] = st("k_head_g")[:, 0, :QK_DIM]
    cwb = st("conv_wb")
    small["conv_w"] = cwb[:, 0:3, :]
    small["conv_b"] = cwb[:, 3, :]
    return loss_part, g, received, to_owner, small


_COL_SHARDED = ("w_in", "w_uq", "w_ukv", "w_br_attn", "w_br_conv", "w_br_mem")
_ROW_SHARDED = ("w_mkv", "w_out")
_BIG = _COL_SHARDED + _ROW_SHARDED
_SMALL = ("norm_g", "b_gate", "q_norm_g", "kv_norm_g", "q_head_g", "k_head_g", "conv_w", "conv_b",
          "mem_norm_g", "mem_q_g", "mem_k_g")
_ORDER = ("norm_g", "w_in", "b_gate", "q_norm_g", "w_uq", "kv_norm_g", "w_ukv", "q_head_g", "k_head_g",
          "conv_w", "conv_b", "mem_norm_g", "w_mkv", "mem_q_g", "mem_k_g", "w_br_attn", "w_br_conv",
          "w_br_mem", "w_out")


def _pack_small(d, extra):
    flat = jnp.concatenate([d[n].reshape(-1) for n in _SMALL] + [extra.reshape(-1)])
    n = flat.shape[0]
    rows = -(-n // 1024) * 8
    return jnp.pad(flat, (0, rows * 128 - n)).reshape(rows, 128)


def _unpack_small(packed, like):
    flat = packed.reshape(-1)
    out, off = {}, 0
    for n in _SMALL:
        sz = int(np.prod(like[n].shape))
        out[n] = flat[off:off + sz].reshape(like[n].shape)
        off += sz
    return out, flat[off:]


def kernel(x, mem, positions, norm_g, w_in, b_gate, q_norm_g, w_uq, kv_norm_g, w_ukv, q_head_g, k_head_g, conv_w, conv_b, mem_norm_g, w_mkv, mem_q_g, mem_k_g, w_br_attn, w_br_conv, w_br_mem, w_out, loss_target, m_norm_g, m_w_in, m_b_gate, m_q_norm_g, m_w_uq, m_kv_norm_g, m_w_ukv, m_q_head_g, m_k_head_g, m_conv_w, m_conv_b, m_mem_norm_g, m_w_mkv, m_mem_q_g, m_mem_k_g, m_w_br_attn, m_w_br_conv, m_w_br_mem, m_w_out, v_norm_g, v_w_in, v_b_gate, v_q_norm_g, v_w_uq, v_kv_norm_g, v_w_ukv, v_q_head_g, v_k_head_g, v_conv_w, v_conv_b, v_mem_norm_g, v_w_mkv, v_mem_q_g, v_mem_k_g, v_w_br_attn, v_w_br_conv, v_w_br_mem, v_w_out):
    w = dict(norm_g=norm_g, w_in=w_in, b_gate=b_gate, q_norm_g=q_norm_g, w_uq=w_uq, kv_norm_g=kv_norm_g,
             w_ukv=w_ukv, q_head_g=q_head_g, k_head_g=k_head_g, conv_w=conv_w, conv_b=conv_b,
             mem_norm_g=mem_norm_g, w_mkv=w_mkv, mem_q_g=mem_q_g, mem_k_g=mem_k_g, w_br_attn=w_br_attn,
             w_br_conv=w_br_conv, w_br_mem=w_br_mem, w_out=w_out)
    m = dict(norm_g=m_norm_g, w_in=m_w_in, b_gate=m_b_gate, q_norm_g=m_q_norm_g, w_uq=m_w_uq,
             kv_norm_g=m_kv_norm_g, w_ukv=m_w_ukv, q_head_g=m_q_head_g, k_head_g=m_k_head_g, conv_w=m_conv_w,
             conv_b=m_conv_b, mem_norm_g=m_mem_norm_g, w_mkv=m_w_mkv, mem_q_g=m_mem_q_g, mem_k_g=m_mem_k_g,
             w_br_attn=m_w_br_attn, w_br_conv=m_w_br_conv, w_br_mem=m_w_br_mem, w_out=m_w_out)
    v = dict(norm_g=v_norm_g, w_in=v_w_in, b_gate=v_b_gate, q_norm_g=v_q_norm_g, w_uq=v_w_uq,
             kv_norm_g=v_kv_norm_g, w_ukv=v_w_ukv, q_head_g=v_q_head_g, k_head_g=v_k_head_g, conv_w=v_conv_w,
             conv_b=v_conv_b, mem_norm_g=v_mem_norm_g, w_mkv=v_w_mkv, mem_q_g=v_mem_q_g, mem_k_g=v_mem_k_g,
             w_br_attn=v_w_br_attn, w_br_conv=v_w_br_conv, w_br_mem=v_w_br_mem, w_out=v_w_out)
    chip = 2 * lax.axis_index("x") + lax.axis_index("y")

    loss_part, grad_x, received, last_slabs, grads = _train_step(x[0], mem[0], positions[0], w, loss_target[0])

    loss_vec = jnp.zeros((128,), F32).at[0].set(0.5 / D_MODEL * jnp.sum(loss_part))
    small = _pack_small(grads, loss_vec)
    scattered = _scatter_chips(last_slabs, small)
    received[0] = scattered[:-1]
    small_sum = _sum_slabs(scattered[-1], "sum_small")
    partial = []
    for a, n in enumerate(_BIG):
        partial.append(jnp.concatenate([_sum_slabs(received[l][a], "sum_" + n) for l in range(DEPTH)], axis=0))
    other = _swap_cores(partial)

    small_g, tail = _unpack_small(small_sum, {n: (grads[n]) for n in _SMALL})
    loss = tail[0]
    small_g["conv_w"] = lax.dynamic_slice_in_dim(small_g["conv_w"], chip * (CONV_W // 4), CONV_W // 4, axis=2)

    outs_g, outs_d, outs_m, outs_v = {}, {}, {}, {}
    for n, p0, p1 in zip(_BIG, partial, other):
        shape = w[n].shape
        flat = lambda t: t.reshape(p0.shape)
        g_, d_, m_, v_ = _adamw(flat(w[n]), p0, p1, flat(m[n]), flat(v[n]), "adamw_" + n)
        outs_g[n], outs_d[n], outs_m[n], outs_v[n] = (t.reshape(shape) for t in (g_, d_, m_, v_))
    zero_small = jnp.zeros_like(small_sum)
    pk = lambda d: _pack_small(d, jnp.zeros((128,), F32))
    g_, d_, m_, v_ = _adamw(pk(w), _pack_small(small_g, jnp.zeros((128,), F32)), zero_small, pk(m), pk(v),
                            "adamw_small")
    like = {n: w[n] for n in _SMALL}
    for dst, packed in ((outs_g, g_), (outs_d, d_), (outs_m, m_), (outs_v, v_)):
        dst.update(_unpack_small(packed, like)[0])

    return (loss, grad_x[None], *[outs_g[n] for n in _ORDER], *[outs_d[n] for n in _ORDER],
            *[outs_m[n] for n in _ORDER], *[outs_v[n] for n in _ORDER])
```

```python
import functools

import numpy as np
import jax
import jax.numpy as jnp
from jax import lax
from jax.experimental import pallas as pl
from jax.experimental.pallas import tpu as pltpu

F32 = jnp.float32
BF16 = jnp.bfloat16

D_MODEL = 1024
DEPTH = 4
N_HEADS = 8
QK_DIM = 96
NOPE_DIM = 64
ROPE_DIM = 32
V_DIM = 64
Q_RANK = 384
KV_RANK = 256
CONV_W = 512
MEM_HEADS = 4
MEM_HD = 128
MEM_W = 512
IN_WIDTH = 7328
PW = 8192
HP = 128
PROJ_HALO = 16
F32_HALO = 8
EPS = 1e-6
ROPE_BASE = 10000.0
SCALE = QK_DIM ** -0.5
MEM_SCALE = MEM_HD ** -0.5

ADAM_LR = 0.001
ADAM_B1 = 0.9
ADAM_B2 = 0.999
ADAM_EPS = 1e-08
ADAM_WD = 0.01
ADAM_STEP = 10

VMEM_LIMIT_V7X = 56 * 1024 * 1024

O_R, O_GA, O_CB, O_QM, O_GC, O_GM, O_CC, O_CU, O_QL, O_KVL, O_KPE = (
    0, 3072, 4096, 4608, 5120, 5632, 6144, 6656, 7168, 7552, 7808)


def _params(sem, vmem=VMEM_LIMIT_V7X):
    return pltpu.CompilerParams(dimension_semantics=sem, vmem_limit_bytes=vmem)


def _sigmoid(t):
    return 0.5 * jnp.tanh(0.5 * t) + 0.5


def _silu_and_grad(g):
    sg = _sigmoid(g)
    return g * sg, sg * (1.0 + g * (1.0 - sg))


def _rms(t, g, n=None):
    n = t.shape[-1] if n is None else n
    r = lax.rsqrt(jnp.sum(t * t, axis=-1, keepdims=True) * (1.0 / n) + EPS)
    return (t * r) * g


def _rms_parts(t, n=None):
    n = t.shape[-1] if n is None else n
    r = lax.rsqrt(jnp.sum(t * t, axis=-1, keepdims=True) * (1.0 / n) + EPS)
    return r, t * r


def _rms_bwd(dhat, hat, r, n):
    return r * (dhat - hat * (jnp.sum(dhat * hat, axis=-1, keepdims=True) * (1.0 / n)))


def _rope(t, c, sa, sb):
    return t * c + pltpu.roll(t, HP - 16, 1) * sa + pltpu.roll(t, 16, 1) * sb


def _rope_t(d, c, sa, sb):
    return d * c + pltpu.roll(d * sa, 16, 1) + pltpu.roll(d * sb, HP - 16, 1)


def _dot(a, b):
    return jnp.dot(a, b, preferred_element_type=F32)


def _dot_nt(a, b):
    return lax.dot_general(a, b, (((1,), (1,)), ((), ())), preferred_element_type=F32)


def _dot_tn(a, b):
    return lax.dot_general(a, b, (((0,), (0,)), ((), ())), preferred_element_type=F32)


def _colsum(t):
    return jnp.sum(t, axis=0, keepdims=True)


def _tile(n, t):
    t = min(n, t)
    assert n % t == 0, (n, t)
    return t


def _rope_tables(pos_b, invf):
    S = pos_b.shape[0]
    tm = _tile(S, 1024)

    def body(pos_ref, invf_ref, c_ref, sa_ref, sb_ref):
        ang = pos_ref[...].astype(F32) * invf_ref[...]
        lane = lax.broadcasted_iota(jnp.int32, ang.shape, 1)
        cs = jnp.cos(ang)
        sn = jnp.sin(ang)
        c_ref[...] = jnp.where(lane < NOPE_DIM, 1.0, jnp.where(lane < QK_DIM, cs, 0.0))
        sa_ref[...] = jnp.where((lane >= NOPE_DIM) & (lane < NOPE_DIM + 16), -sn, 0.0)
        sb_ref[...] = jnp.where((lane >= NOPE_DIM + 16) & (lane < QK_DIM), sn, 0.0)

    blk = pl.BlockSpec((tm, HP), lambda i: (i, 0))
    return pl.pallas_call(
        body, name="rope_tables", grid=(S // tm,),
        in_specs=[blk, pl.BlockSpec((1, HP), lambda i: (0, 0))],
        out_specs=[blk, blk, blk],
        out_shape=[jax.ShapeDtypeStruct((S, HP), F32)] * 3,
        compiler_params=_params(("parallel",)),
    )(pos_b, invf)


MESH = pl.DeviceIdType.MESH
HBM = pl.BlockSpec(memory_space=pltpu.HBM)


def _chip_peer(x, y, c, k):
    return (x ^ (k >> 1), y ^ (k & 1), c)


def _chip_copies(ins, outs, send, recv, loc, gather):
    x, y, c = lax.axis_index("x"), lax.axis_index("y"), lax.axis_index("c")
    me = 2 * x + y
    cps = []
    for a in range(len(ins)):
        cps.append(pltpu.make_async_copy(ins[a] if gather else ins[a].at[me], outs[a].at[me], loc.at[a]))
        for k in (1, 2, 3):
            cps.append(pltpu.make_async_remote_copy(
                src_ref=ins[a] if gather else ins[a].at[me ^ k], dst_ref=outs[a].at[me],
                send_sem=send.at[3 * a + k - 1], recv_sem=recv.at[3 * a + k - 1],
                device_id=_chip_peer(x, y, c, k), device_id_type=MESH))
    return cps


def _carried(body, n_in, n_out, carry, gather):
    n = len(carry)

    def wrapped(*refs):
        ins, cin = refs[:n_in], refs[n_in:n_in + n]
        outs, cout = refs[n_in + n:n_in + n + n_out], refs[n_in + n + n_out:n_in + 2 * n + n_out]
        send, recv, loc = refs[n_in + 2 * n + n_out:]
        i = pl.program_id(0)

        @pl.when(i == 0)
        def _():
            for cp in _chip_copies(cin, cout, send, recv, loc, gather):
                cp.start()

        body(*ins, *outs)

        @pl.when(i == pl.num_programs(0) - 1)
        def _():
            for cp in _chip_copies(cin, cout, send, recv, loc, gather):
                cp.wait()

    specs = dict(
        in_specs=[HBM] * n, out_specs=[HBM] * n,
        out_shape=[jax.ShapeDtypeStruct(((4,) + a.shape) if gather else a.shape, a.dtype) for a in carry],
        scratch_shapes=[pltpu.SemaphoreType.DMA((3 * n,)), pltpu.SemaphoreType.DMA((3 * n,)),
                        pltpu.SemaphoreType.DMA((n,))])
    return wrapped, specs


def _inproj(x, g, w, carry=()):
    S = x.shape[0]
    tm = _tile(S, 256)

    def body(x_ref, g_ref, w_ref, proj_ref, xn_ref):
        h = _rms(x_ref[...], g_ref[...]).astype(BF16)
        xn_ref[...] = h
        proj_ref[...] = _dot(h, w_ref[...]).astype(BF16)

    row = lambda n: pl.BlockSpec((tm, n), lambda i: (i, 0))
    in_specs = [row(D_MODEL), pl.BlockSpec((1, D_MODEL), lambda i: (0, 0)), pl.BlockSpec((D_MODEL, PW), lambda i: (0, 0))]
    out_specs = [row(PW), row(D_MODEL)]
    out_shape = [jax.ShapeDtypeStruct((S, PW), BF16), jax.ShapeDtypeStruct((S, D_MODEL), BF16)]
    if not carry:
        proj, xn = pl.pallas_call(
            body, name="inproj", grid=(S // tm,), in_specs=in_specs, out_specs=out_specs, out_shape=out_shape,
            compiler_params=_params(("parallel",)))(x, g, w)
        return proj, xn, []
    wrapped, extra = _carried(body, 3, 2, carry, gather=True)
    res = pl.pallas_call(
        wrapped, name="inproj_gather", grid=(S // tm,), in_specs=in_specs + extra["in_specs"],
        out_specs=out_specs + extra["out_specs"], out_shape=out_shape + extra["out_shape"],
        scratch_shapes=extra["scratch_shapes"], compiler_params=_params(("arbitrary",)))(x, g, w, *carry)
    return res[0], res[1], list(res[2:])


def _mla_prep(proj, tabs, qg, kvg, qhg, khg, wuq, wuk, wuv):
    S = proj.shape[0]
    tm = _tile(S, 512)

    def body(a_ref, c_ref, sa_ref, sb_ref, qg_ref, kvg_ref, qhg_ref, khg_ref, wuq_ref, wuk_ref, wuv_ref,
             q_ref, k_ref, v_ref, qn_ref, kvn_ref, kmax_ref, qt_ref, vt_ref):
        @pl.when(pl.program_id(0) == 0)
        def _():
            kmax_ref[...] = jnp.zeros(kmax_ref.shape, F32)

        ql = a_ref[:, 0:Q_RANK].astype(F32)
        kvl = a_ref[:, Q_RANK:Q_RANK + KV_RANK].astype(F32)
        kpe = a_ref[:, Q_RANK + KV_RANK:Q_RANK + KV_RANK + HP].astype(F32)
        qn = _rms(ql, qg_ref[...]).astype(BF16)
        kvn = _rms(kvl, kvg_ref[...]).astype(BF16)
        qn_ref[...] = qn
        kvn_ref[...] = kvn
        qraw = _dot(qn, wuq_ref[...])
        kn = _dot(kvn, wuk_ref[...])
        vf = _dot(kvn, wuv_ref[...])
        v_ref[...] = vf.astype(BF16)
        c, sa, sb = c_ref[...], sa_ref[...], sb_ref[...]
        for h in range(N_HEADS):
            sl = slice(h * HP, (h + 1) * HP)
            tq = _rms(qraw[:, sl], qhg_ref[...], QK_DIM)
            qh = _rope(tq, c, sa, sb) * SCALE
            q_ref[:, sl] = qh.astype(BF16)
            qt_ref[sl, :] = qh.T.astype(BF16)
            vt_ref[sl, :] = vf[:, sl].T.astype(BF16)
            tk = _rms(kn[:, sl] + kpe, khg_ref[...], QK_DIM)
            kb = _rope(tk, c, sa, sb).astype(BF16)
            k_ref[:, sl] = kb
            kf = kb.astype(F32)
            norm2 = jnp.max(jnp.sum(kf * kf, axis=-1, keepdims=True), axis=0, keepdims=True)
            kmax_ref[:, sl] = jnp.maximum(kmax_ref[:, sl], norm2)

    row = lambda w: pl.BlockSpec((tm, w), lambda i: (i, 0))
    full = lambda a: pl.BlockSpec(a.shape, lambda i: (0,) * a.ndim)
    return pl.pallas_call(
        body, name="mla_prep", grid=(S // tm,),
        in_specs=[pl.BlockSpec((tm, 1024), lambda i: (i, 7)), row(HP), row(HP), row(HP),
                  full(qg), full(kvg), full(qhg), full(khg), full(wuq), full(wuk), full(wuv)],
        out_specs=[row(1024), row(1024), row(1024), row(Q_RANK), row(KV_RANK),
                   pl.BlockSpec((1, 1024), lambda i: (0, 0)),
                   pl.BlockSpec((1024, tm), lambda i: (0, i)), pl.BlockSpec((1024, tm), lambda i: (0, i))],
        out_shape=[jax.ShapeDtypeStruct((S, 1024), BF16)] * 3
        + [jax.ShapeDtypeStruct((S, Q_RANK), BF16), jax.ShapeDtypeStruct((S, KV_RANK), BF16),
           jax.ShapeDtypeStruct((1, 1024), F32)] + [jax.ShapeDtypeStruct((1024, S), BF16)] * 2,
        compiler_params=_params(("arbitrary",)),
    )(proj, *tabs, qg, kvg, qhg, khg, wuq, wuk, wuv)


def _attn_fwd(q, k, v):
    S = q.shape[0]
    tq, tk = _tile(S, 1024), _tile(S, 2048)
    nk, nb = S // tk, tk // HP

    def body(q_ref, k_ref, v_ref, o_ref, lse_ref, m_s, l_s, acc_s):
        m_s[...] = jnp.full(m_s.shape, -jnp.inf, F32)
        l_s[...] = jnp.zeros(l_s.shape, F32)
        acc_s[...] = jnp.zeros(acc_s.shape, F32)
        qv = q_ref[...]

        def step(c, carry):
            rows = pl.ds(pl.multiple_of(c * tk, tk), tk)
            s = _dot_nt(qv, k_ref[rows, :])
            cm = s[:, 0:HP]
            for j in range(1, nb):
                cm = jnp.maximum(cm, s[:, j * HP:(j + 1) * HP])
            m_prev = m_s[...]
            m_new = jnp.maximum(m_prev, jnp.max(cm, axis=-1, keepdims=True))
            alpha = jnp.exp(m_prev - m_new)
            lsum = alpha * l_s[...]
            ps = []
            for j in range(nb):
                pj = jnp.exp(s[:, j * HP:(j + 1) * HP] - m_new)
                lsum = lsum + pj
                ps.append(pj.astype(BF16))
            l_s[...] = lsum
            acc_s[...] = alpha * acc_s[...] + _dot(jnp.concatenate(ps, axis=1), v_ref[rows, :])
            m_s[...] = m_new
            return carry

        lax.fori_loop(0, nk, step, 0)
        l = jnp.sum(l_s[...], axis=-1, keepdims=True)
        o_ref[...] = acc_s[...] / l
        lse_ref[0] = m_s[:, 0:1] + jnp.log(l)

    return pl.pallas_call(
        body, name="attn_fwd", grid=(N_HEADS, S // tq),
        in_specs=[pl.BlockSpec((tq, HP), lambda h, i: (i, h)),
                  pl.BlockSpec((S, HP), lambda h, i: (0, h)),
                  pl.BlockSpec((S, HP), lambda h, i: (0, h))],
        out_specs=[pl.BlockSpec((tq, HP), lambda h, i: (i, h)),
                   pl.BlockSpec((1, tq, 1), lambda h, i: (h, i, 0))],
        out_shape=[jax.ShapeDtypeStruct((S, N_HEADS * HP), F32),
                   jax.ShapeDtypeStruct((N_HEADS, S, 1), F32)],
        scratch_shapes=[pltpu.VMEM((tq, HP), F32), pltpu.VMEM((tq, HP), F32), pltpu.VMEM((tq, HP), F32)],
        compiler_params=_params(("parallel", "parallel")),
    )(q, k, v)


def _memkv(mem, mng, mkg, wmkv):
    M = mem.shape[0]

    def body(mem_ref, mng_ref, mkg_ref, w_ref, mk_ref, mv_ref):
        mn = _rms(mem_ref[...], mng_ref[...]).astype(BF16)
        mkv = _dot(mn, w_ref[...])
        for h in range(MEM_HEADS):
            kraw = mkv[:, 2 * MEM_HD * h:2 * MEM_HD * h + MEM_HD]
            mk_ref[:, MEM_HD * h:MEM_HD * (h + 1)] = _rms(kraw, mkg_ref[...]).astype(BF16)
            mv_ref[:, MEM_HD * h:MEM_HD * (h + 1)] = mkv[:, 2 * MEM_HD * h + MEM_HD:2 * MEM_HD * (h + 1)].astype(BF16)

    return pl.pallas_call(
        body, name="memkv",
        out_shape=[jax.ShapeDtypeStruct((M, MEM_W), BF16)] * 2,
        compiler_params=pltpu.CompilerParams(vmem_limit_bytes=VMEM_LIMIT_V7X),
    )(mem, mng, mkg, wmkv)


def _conv_shifts(cc, cu, hp_ref, hn_ref, i, n_tiles, tm):
    z = cc * cu
    last = PROJ_HALO - 1
    zp = hp_ref[last:last + 1, 0:CONV_W].astype(F32) * hp_ref[last:last + 1, CONV_W:2 * CONV_W].astype(F32)
    zn = hn_ref[0:1, 0:CONV_W].astype(F32) * hn_ref[0:1, CONV_W:2 * CONV_W].astype(F32)
    zp = jnp.where(i == 0, 0.0, zp)
    zn = jnp.where(i == n_tiles - 1, 0.0, zn)
    row = lax.broadcasted_iota(jnp.int32, z.shape, 0)
    z_up = jnp.where(row == 0, zp, pltpu.roll(z, 1, 0))
    z_dn = jnp.where(row == tm - 1, zn, pltpu.roll(z, tm - 1, 0))
    return z, z_up, z_dn


def _halo_specs(tm, S, width, col, rows):
    per = tm // rows
    prev = pl.BlockSpec((rows, width), lambda i: (jnp.maximum(i * per - 1, 0), col))
    nxt = pl.BlockSpec((rows, width), lambda i: (jnp.minimum((i + 1) * per, S // rows - 1), col))
    return prev, nxt


def _mem_attend(qm, mqg, mk_h, mv_h):
    r, qhat = _rms_parts(qm)
    mq = (qhat * mqg).astype(BF16)
    s = _dot_nt(mq, mk_h) * MEM_SCALE
    e = jnp.exp(s - jnp.max(s, axis=-1, keepdims=True))
    p = e / jnp.sum(e, axis=-1, keepdims=True)
    pv = _dot(p.astype(BF16), mv_h)
    return r, qhat, mq, p, pv


def _merge(proj, o, x, bg, convw, mqg, mk, mv, wba, wbc, wbm, wo):
    S = x.shape[0]
    tm = _tile(S, 512)
    nt = S // tm

    def body(main_ref, ccu_ref, hp_ref, hn_ref, o_ref, x_ref, bg_ref, cw_ref, mqg_ref, mk_ref, mv_ref,
             wba_ref, wbc_ref, wbm_ref, wo_ref, xn_ref, oa_ref, oc_ref, om_ref, u_ref, y_ref):
        i = pl.program_id(0)
        sil_a, _ = _silu_and_grad(main_ref[:, O_GA:O_GA + 1024].astype(F32))
        oa = (o_ref[...] * sil_a).astype(BF16)
        oa_ref[...] = oa
        z, z_up, z_dn = _conv_shifts(ccu_ref[:, 0:CONV_W].astype(F32), ccu_ref[:, CONV_W:].astype(F32), hp_ref, hn_ref, i, nt, tm)
        cv = cw_ref[0:1, :] * z_up + cw_ref[1:2, :] * z + cw_ref[2:3, :] * z_dn + cw_ref[3:4, :]
        sil_c, _ = _silu_and_grad(main_ref[:, O_GC:O_GC + CONV_W].astype(F32))
        oc = (main_ref[:, O_CB:O_CB + CONV_W].astype(F32) * cv * sil_c).astype(BF16)
        oc_ref[...] = oc
        sil_m, _ = _silu_and_grad(main_ref[:, O_GM:O_GM + MEM_W].astype(F32))
        for h in range(MEM_HEADS):
            sl = slice(h * MEM_HD, (h + 1) * MEM_HD)
            qm = main_ref[:, O_QM + h * MEM_HD:O_QM + (h + 1) * MEM_HD].astype(F32)
            pv = _mem_attend(qm, mqg_ref[...], mk_ref[:, sl], mv_ref[:, sl])[4]
            om_ref[:, sl] = (pv * sil_m[:, sl]).astype(BF16)
        ua = _dot(oa, wba_ref[...])
        uc = _dot(oc, wbc_ref[...])
        um = _dot(om_ref[...], wbm_ref[...])
        u_ref[:, 0:1024] = ua.astype(BF16)
        u_ref[:, 1024:2048] = uc.astype(BF16)
        u_ref[:, 2048:3072] = um.astype(BF16)
        rg = _sigmoid(main_ref[:, O_R:O_R + 3072].astype(F32) + bg_ref[...])
        y = (rg[:, 0:1024] * ua + rg[:, 1024:2048] * uc + rg[:, 2048:3072] * um).astype(BF16)
        y_ref[...] = y
        xn_ref[...] = x_ref[...] + _dot(y, wo_ref[...])

    row = lambda w: pl.BlockSpec((tm, w), lambda i: (i, 0))
    full = lambda a: pl.BlockSpec(a.shape, lambda i: (0,) * a.ndim)
    hp, hn = _halo_specs(tm, S, 1024, 6, PROJ_HALO)
    return pl.pallas_call(
        body, name="merge", grid=(nt,),
        in_specs=[row(6144), pl.BlockSpec((tm, 1024), lambda i: (i, 6)), hp, hn, row(1024), row(1024),
                  full(bg), full(convw), full(mqg), full(mk), full(mv), full(wba), full(wbc), full(wbm), full(wo)],
        out_specs=[row(1024), row(1024), row(CONV_W), row(MEM_W), row(3072), row(1024)],
        out_shape=[jax.ShapeDtypeStruct((S, 1024), F32), jax.ShapeDtypeStruct((S, 1024), BF16),
                   jax.ShapeDtypeStruct((S, CONV_W), BF16), jax.ShapeDtypeStruct((S, MEM_W), BF16),
                   jax.ShapeDtypeStruct((S, 3072), BF16), jax.ShapeDtypeStruct((S, 1024), BF16)],
        compiler_params=_params(("parallel",)),
    )(proj, proj, proj, proj, o, x, bg, convw, mqg, mk, mv, wba, wbc, wbm, wo)


def _loss_head(xf, tgt):
    S = xf.shape[0]
    tm = _tile(S, 1024)

    def body(x_ref, t_ref, g_ref, acc_ref):
        @pl.when(pl.program_id(0) == 0)
        def _():
            acc_ref[...] = jnp.zeros(acc_ref.shape, F32)

        e = x_ref[...] - t_ref[...]
        g_ref[...] = e * (1.0 / D_MODEL)
        part = jnp.sum((e * e).reshape(tm // 8, 8, D_MODEL), axis=0)
        tot = part[:, 0:128]
        for k in range(1, D_MODEL // 128):
            tot = tot + part[:, 128 * k:128 * (k + 1)]
        acc_ref[...] += tot

    row = pl.BlockSpec((tm, D_MODEL), lambda i: (i, 0))
    return pl.pallas_call(
        body, name="loss_head", grid=(S // tm,),
        in_specs=[row, row],
        out_specs=[row, pl.BlockSpec((8, 128), lambda i: (0, 0))],
        out_shape=[jax.ShapeDtypeStruct((S, D_MODEL), F32), jax.ShapeDtypeStruct((8, 128), F32)],
        compiler_params=_params(("arbitrary",)),
    )(xf, tgt)


def _merge_bwd(g, proj, o, u, bg, convw, mqg, mk, mv, wot, wbat, wbct, wbmt):
    S = g.shape[0]
    tm = _tile(S, 256)
    nt = S // tm
    M = mk.shape[0]

    def body(g_ref, main_ref, ccu_ref, hp_ref, hn_ref, o_ref, u_ref, bg_ref, cw_ref, mqg_ref, mk_ref, mv_ref,
             wot_ref, wbat_ref, wbct_ref, wbmt_ref,
             dmain_ref, dcv_ref, do_ref, delta_ref, du_ref, dbg_ref, dmk_ref, dmv_ref, dmqg_ref, dot_ref):
        i = pl.program_id(0)

        @pl.when(i == 0)
        def _():
            dbg_ref[...] = jnp.zeros(dbg_ref.shape, F32)
            dmk_ref[...] = jnp.zeros(dmk_ref.shape, F32)
            dmv_ref[...] = jnp.zeros(dmv_ref.shape, F32)
            dmqg_ref[...] = jnp.zeros(dmqg_ref.shape, F32)

        dy = _dot(g_ref[...].astype(BF16), wot_ref[...])
        rg = _sigmoid(main_ref[:, O_R:O_R + 3072].astype(F32) + bg_ref[...])
        dyt = jnp.concatenate([dy, dy, dy], axis=1)
        dr = dyt * u_ref[...].astype(F32) * rg * (1.0 - rg)
        dmain_ref[:, O_R:O_R + 3072] = dr.astype(BF16)
        dbg_ref[...] += _colsum(dr)
        du = (dyt * rg).astype(BF16)
        du_ref[...] = du
        do_a = _dot(du[:, 0:1024], wbat_ref[...])
        do_c = _dot(du[:, 1024:2048], wbct_ref[...])
        do_m = _dot(du[:, 2048:3072], wbmt_ref[...])

        sil_a, dsil_a = _silu_and_grad(main_ref[:, O_GA:O_GA + 1024].astype(F32))
        ov = o_ref[...]
        d_o = do_a * sil_a
        do_ref[...] = d_o.astype(BF16)
        dot_ref[...] = d_o.T.astype(BF16)
        dmain_ref[:, O_GA:O_GA + 1024] = (do_a * ov * dsil_a).astype(BF16)
        prod = d_o * ov
        for h in range(N_HEADS):
            delta_ref[h] = jnp.sum(prod[:, h * HP:(h + 1) * HP], axis=-1, keepdims=True)

        z, z_up, z_dn = _conv_shifts(ccu_ref[:, 0:CONV_W].astype(F32), ccu_ref[:, CONV_W:].astype(F32), hp_ref, hn_ref, i, nt, tm)
        cv = cw_ref[0:1, :] * z_up + cw_ref[1:2, :] * z + cw_ref[2:3, :] * z_dn + cw_ref[3:4, :]
        sil_c, dsil_c = _silu_and_grad(main_ref[:, O_GC:O_GC + CONV_W].astype(F32))
        cb = main_ref[:, O_CB:O_CB + CONV_W].astype(F32)
        dmain_ref[:, O_CB:O_CB + CONV_W] = (do_c * cv * sil_c).astype(BF16)
        dmain_ref[:, O_GC:O_GC + CONV_W] = (do_c * cb * cv * dsil_c).astype(BF16)
        dcv_ref[...] = do_c * cb * sil_c

        sil_m, dsil_m = _silu_and_grad(main_ref[:, O_GM:O_GM + MEM_W].astype(F32))
        for h in range(MEM_HEADS):
            sl = slice(h * MEM_HD, (h + 1) * MEM_HD)
            qm = main_ref[:, O_QM + h * MEM_HD:O_QM + (h + 1) * MEM_HD].astype(F32)
            mk_h, mv_h = mk_ref[:, sl], mv_ref[:, sl]
            r, qhat, mq, p, pv = _mem_attend(qm, mqg_ref[...], mk_h, mv_h)
            dom = do_m[:, sl]
            dmain_ref[:, O_GM + h * MEM_HD:O_GM + (h + 1) * MEM_HD] = (dom * pv * dsil_m[:, sl]).astype(BF16)
            dpv = (dom * sil_m[:, sl]).astype(BF16)
            dp = _dot_nt(dpv, mv_h)
            ds = (p * (dp - jnp.sum(dp * p, axis=-1, keepdims=True)) * MEM_SCALE).astype(BF16)
            dmq = _dot(ds, mk_h)
            dmk_ref[:, sl] += _dot_tn(ds, mq)
            dmv_ref[:, sl] += _dot_tn(p.astype(BF16), dpv)
            dmqg_ref[...] += _colsum(dmq * qhat)
            dqm = _rms_bwd(dmq * mqg_ref[...], qhat, r, MEM_HD)
            dmain_ref[:, O_QM + h * MEM_HD:O_QM + (h + 1) * MEM_HD] = dqm.astype(BF16)

    row = lambda w: pl.BlockSpec((tm, w), lambda i: (i, 0))
    full = lambda a: pl.BlockSpec(a.shape, lambda i: (0,) * a.ndim)
    acc = lambda r, c: pl.BlockSpec((r, c), lambda i: (0, 0))
    hp, hn = _halo_specs(tm, S, 1024, 6, PROJ_HALO)
    return pl.pallas_call(
        body, name="merge_bwd", grid=(nt,),
        in_specs=[row(1024), row(6144), pl.BlockSpec((tm, 1024), lambda i: (i, 6)), hp, hn, row(1024), row(3072),
                  full(bg), full(convw), full(mqg), full(mk), full(mv), full(wot), full(wbat), full(wbct), full(wbmt)],
        out_specs=[row(6144), row(CONV_W), row(1024), pl.BlockSpec((N_HEADS, tm, 1), lambda i: (0, i, 0)), row(3072),
                   acc(1, 3072), acc(M, MEM_W), acc(M, MEM_W), acc(1, MEM_HD),
                   pl.BlockSpec((1024, tm), lambda i: (0, i))],
        out_shape=[jax.ShapeDtypeStruct((S, 6144), BF16), jax.ShapeDtypeStruct((S, CONV_W), F32),
                   jax.ShapeDtypeStruct((S, 1024), BF16), jax.ShapeDtypeStruct((N_HEADS, S, 1), F32),
                   jax.ShapeDtypeStruct((S, 3072), BF16), jax.ShapeDtypeStruct((1, 3072), F32),
                   jax.ShapeDtypeStruct((M, MEM_W), F32), jax.ShapeDtypeStruct((M, MEM_W), F32),
                   jax.ShapeDtypeStruct((1, MEM_HD), F32), jax.ShapeDtypeStruct((1024, S), BF16)],
        compiler_params=_params(("arbitrary",)),
    )(g, proj, proj, proj, proj, o, u, bg, convw, mqg, mk, mv, wot, wbat, wbct, wbmt)


def _conv_bwd(dcv, proj, convw):
    S = dcv.shape[0]
    tm = _tile(S, 512)
    nt = S // tm

    def body(d_ref, dp_ref, dn_ref, ccu_ref, hp_ref, hn_ref, cw_ref, dccu_ref, dcw_ref):
        i = pl.program_id(0)

        @pl.when(i == 0)
        def _():
            dcw_ref[...] = jnp.zeros(dcw_ref.shape, F32)

        cc, cu = ccu_ref[:, 0:CONV_W].astype(F32), ccu_ref[:, CONV_W:].astype(F32)
        z, z_up, z_dn = _conv_shifts(cc, cu, hp_ref, hn_ref, i, nt, tm)
        d = d_ref[...]
        dprev = jnp.where(i == 0, 0.0, dp_ref[7:8, :])
        dnext = jnp.where(i == nt - 1, 0.0, dn_ref[0:1, :])
        row = lax.broadcasted_iota(jnp.int32, d.shape, 0)
        d_up = jnp.where(row == 0, dprev, pltpu.roll(d, 1, 0))
        d_dn = jnp.where(row == tm - 1, dnext, pltpu.roll(d, tm - 1, 0))
        dz = cw_ref[0:1, :] * d_dn + cw_ref[1:2, :] * d + cw_ref[2:3, :] * d_up
        dccu_ref[:, 0:CONV_W] = (dz * cu).astype(BF16)
        dccu_ref[:, CONV_W:] = (dz * cc).astype(BF16)
        dcw_ref[0:1, :] += _colsum(d * z_up)
        dcw_ref[1:2, :] += _colsum(d * z)
        dcw_ref[2:3, :] += _colsum(d * z_dn)
        dcw_ref[3:4, :] += _colsum(d)

    hp, hn = _halo_specs(tm, S, 1024, 6, PROJ_HALO)
    dp, dn = _halo_specs(tm, S, CONV_W, 0, F32_HALO)
    return pl.pallas_call(
        body, name="conv_bwd", grid=(nt,),
        in_specs=[pl.BlockSpec((tm, CONV_W), lambda i: (i, 0)), dp, dn,
                  pl.BlockSpec((tm, 1024), lambda i: (i, 6)), hp, hn,
                  pl.BlockSpec((8, CONV_W), lambda i: (0, 0))],
        out_specs=[pl.BlockSpec((tm, 1024), lambda i: (i, 0)), pl.BlockSpec((8, CONV_W), lambda i: (0, 0))],
        out_shape=[jax.ShapeDtypeStruct((S, 1024), BF16), jax.ShapeDtypeStruct((8, CONV_W), F32)],
        compiler_params=_params(("arbitrary",)),
    )(dcv, dcv, dcv, proj, proj, proj, convw)


def _attn_bwd(q, k, v, do, qt, dot, lse, delta):
    S = q.shape[0]
    tq, tk = _tile(S, 512), _tile(S, 4096)
    ni, nk = S // tq, S // tk

    def body(q_ref, do_ref, qt_ref, dot_ref, k_ref, v_ref, lse_ref, delta_ref, dq_ref, dkt_hbm, dvt_hbm,
             dq_s, dkt_s, dvt_s, sem):
        h, i = pl.program_id(0), pl.program_id(1)

        @pl.when(i == 0)
        def _():
            dkt_s[...] = jnp.zeros(dkt_s.shape, F32)
            dvt_s[...] = jnp.zeros(dvt_s.shape, F32)

        dq_s[...] = jnp.zeros(dq_s.shape, F32)
        qv, dov, qtv, dotv = q_ref[...], do_ref[...], qt_ref[...], dot_ref[...]
        lse_c, delta_c = lse_ref[0], delta_ref[0]

        def step(c, carry):
            cols = pl.ds(pl.multiple_of(c * tk, tk), tk)
            kc, vc = k_ref[cols, :], v_ref[cols, :]
            p = jnp.exp(_dot_nt(qv, kc) - lse_c)
            dp = _dot_nt(dov, vc)
            ds = (p * (dp - delta_c)).astype(BF16)
            dq_s[...] += _dot(ds, kc)
            dvt_s[:, cols] += _dot(dotv, p.astype(BF16))
            dkt_s[:, cols] += _dot(qtv, ds)
            return carry

        lax.fori_loop(0, nk, step, 0)
        dq_ref[...] = dq_s[...]

        @pl.when(i == ni - 1)
        def _():
            head = pl.ds(pl.multiple_of(h * HP, HP), HP)
            out_k = pltpu.make_async_copy(dkt_s, dkt_hbm.at[head, :], sem.at[0])
            out_v = pltpu.make_async_copy(dvt_s, dvt_hbm.at[head, :], sem.at[1])
            out_k.start()
            out_v.start()
            out_k.wait()
            out_v.wait()

    col = pl.BlockSpec((1, tq, 1), lambda h, i: (h, i, 0))
    blk = pl.BlockSpec((tq, HP), lambda h, i: (i, h))
    blkt = pl.BlockSpec((HP, tq), lambda h, i: (h, i))
    res = pl.BlockSpec((S, HP), lambda h, i: (0, h))
    whole = pl.BlockSpec(memory_space=pl.ANY)
    return pl.pallas_call(
        body, name="attn_bwd", grid=(N_HEADS, ni),
        in_specs=[blk, blk, blkt, blkt, res, res, col, col],
        out_specs=[blk, whole, whole],
        out_shape=[jax.ShapeDtypeStruct((S, N_HEADS * HP), F32), jax.ShapeDtypeStruct((N_HEADS * HP, S), F32),
                   jax.ShapeDtypeStruct((N_HEADS * HP, S), F32)],
        scratch_shapes=[pltpu.VMEM((tq, HP), F32), pltpu.VMEM((HP, S), F32), pltpu.VMEM((HP, S), F32),
                        pltpu.SemaphoreType.DMA((2,))],
        compiler_params=_params(("parallel", "arbitrary")),
    )(q, do, qt, dot, k, v, lse, delta)


def _mla_prep_bwd(proj, tabs, dq, dk, dv, qg, kvg, qhg, khg, wuq, wuk, wuqt, wukt, wuvt):
    S = proj.shape[0]
    tm = _tile(S, 512)

    def body(a_ref, c_ref, sa_ref, sb_ref, dq_ref, dk_ref, dv_ref, qg_ref, kvg_ref, qhg_ref, khg_ref,
             wuq_ref, wuk_ref, wuqt_ref, wukt_ref, wuvt_ref,
             da_ref, dqraw_ref, dkraw_ref, dqg_ref, dkvg_ref, dqhg_ref, dkhg_ref):
        @pl.when(pl.program_id(0) == 0)
        def _():
            dqg_ref[...] = jnp.zeros(dqg_ref.shape, F32)
            dkvg_ref[...] = jnp.zeros(dkvg_ref.shape, F32)
            dqhg_ref[...] = jnp.zeros(dqhg_ref.shape, F32)
            dkhg_ref[...] = jnp.zeros(dkhg_ref.shape, F32)

        ql = a_ref[:, 0:Q_RANK].astype(F32)
        kvl = a_ref[:, Q_RANK:Q_RANK + KV_RANK].astype(F32)
        kpe = a_ref[:, Q_RANK + KV_RANK:Q_RANK + KV_RANK + HP].astype(F32)
        rq, qhat = _rms_parts(ql)
        rkv, kvhat = _rms_parts(kvl)
        qraw = _dot((qhat * qg_ref[...]).astype(BF16), wuq_ref[...])
        kn = _dot((kvhat * kvg_ref[...]).astype(BF16), wuk_ref[...])
        c, sa, sb = c_ref[...], sa_ref[...], sb_ref[...]
        dkpe = jnp.zeros(kpe.shape, F32)
        dqhg = jnp.zeros((1, HP), F32)
        dkhg = jnp.zeros((1, HP), F32)
        for h in range(N_HEADS):
            sl = slice(h * HP, (h + 1) * HP)
            r, that = _rms_parts(qraw[:, sl], QK_DIM)
            dtn = _rope_t(dq_ref[:, sl], c, sa, sb) * SCALE
            dqhg = dqhg + _colsum(dtn * that)
            dqraw_ref[:, sl] = _rms_bwd(dtn * qhg_ref[...], that, r, QK_DIM).astype(BF16)
            r, that = _rms_parts(kn[:, sl] + kpe, QK_DIM)
            dtn = _rope_t(dk_ref[:, sl], c, sa, sb)
            dkhg = dkhg + _colsum(dtn * that)
            dkr = _rms_bwd(dtn * khg_ref[...], that, r, QK_DIM)
            dkraw_ref[:, sl] = dkr.astype(BF16)
            dkpe = dkpe + dkr
        dqhg_ref[...] += dqhg
        dkhg_ref[...] += dkhg
        dqn = _dot(dqraw_ref[...], wuqt_ref[...])
        dqg_ref[...] += _colsum(dqn * qhat)
        da_ref[:, 0:Q_RANK] = _rms_bwd(dqn * qg_ref[...], qhat, rq, Q_RANK).astype(BF16)
        dkvn = _dot(dkraw_ref[...], wukt_ref[...]) + _dot(dv_ref[...], wuvt_ref[...])
        dkvg_ref[...] += _colsum(dkvn * kvhat)
        da_ref[:, Q_RANK:Q_RANK + KV_RANK] = _rms_bwd(dkvn * kvg_ref[...], kvhat, rkv, KV_RANK).astype(BF16)
        da_ref[:, Q_RANK + KV_RANK:Q_RANK + KV_RANK + HP] = dkpe.astype(BF16)
        da_ref[:, Q_RANK + KV_RANK + HP:] = jnp.zeros((tm, 1024 - Q_RANK - KV_RANK - HP), BF16)

    row = lambda w: pl.BlockSpec((tm, w), lambda i: (i, 0))
    full = lambda a: pl.BlockSpec(a.shape, lambda i: (0,) * a.ndim)
    acc = lambda c: pl.BlockSpec((1, c), lambda i: (0, 0))
    return pl.pallas_call(
        body, name="mla_prep_bwd", grid=(S // tm,),
        in_specs=[pl.BlockSpec((tm, 1024), lambda i: (i, 7)), row(HP), row(HP), row(HP),
                  row(1024), row(1024), row(1024), full(qg), full(kvg), full(qhg), full(khg),
                  full(wuq), full(wuk), full(wuqt), full(wukt), full(wuvt)],
        out_specs=[row(1024), row(1024), row(1024), acc(Q_RANK), acc(KV_RANK), acc(HP), acc(HP)],
        out_shape=[jax.ShapeDtypeStruct((S, 1024), BF16)] * 3
        + [jax.ShapeDtypeStruct((1, Q_RANK), F32), jax.ShapeDtypeStruct((1, KV_RANK), F32),
           jax.ShapeDtypeStruct((1, HP), F32), jax.ShapeDtypeStruct((1, HP), F32)],
        compiler_params=_params(("arbitrary",)),
    )(proj, *tabs, dq, dk, dv, qg, kvg, qhg, khg, wuq, wuk, wuqt, wukt, wuvt)


def _inproj_bwd(dmain, dccu, dsega, wint, x, g, ng, carry=()):
    S = x.shape[0]
    tm = _tile(S, 256)
    nm, nc = dmain.shape[1], dccu.shape[1]
    assert nm + nc + dsega.shape[1] == PW

    def body(dm_ref, dc_ref, da_ref, w_ref, x_ref, g_ref, ng_ref, dx_ref, dng_ref):
        @pl.when(pl.program_id(0) == 0)
        def _():
            dng_ref[...] = jnp.zeros(dng_ref.shape, F32)

        dh = (_dot(dm_ref[...], w_ref[0:nm, :]) + _dot(dc_ref[...], w_ref[nm:nm + nc, :])
              + _dot(da_ref[...], w_ref[nm + nc:PW, :]))
        r, xhat = _rms_parts(x_ref[...])
        dng_ref[...] += _colsum(dh * xhat)
        dx_ref[...] = g_ref[...] + _rms_bwd(dh * ng_ref[...], xhat, r, D_MODEL)

    row = lambda w: pl.BlockSpec((tm, w), lambda i: (i, 0))
    in_specs = [row(nm), row(nc), row(dsega.shape[1]), pl.BlockSpec((PW, D_MODEL), lambda i: (0, 0)),
                row(D_MODEL), row(D_MODEL), pl.BlockSpec((1, D_MODEL), lambda i: (0, 0))]
    out_specs = [row(D_MODEL), pl.BlockSpec((1, D_MODEL), lambda i: (0, 0))]
    out_shape = [jax.ShapeDtypeStruct((S, D_MODEL), F32), jax.ShapeDtypeStruct((1, D_MODEL), F32)]
    args = (dmain, dccu, dsega, wint, x, g, ng)
    if not carry:
        dx, dng = pl.pallas_call(
            body, name="inproj_bwd", grid=(S // tm,), in_specs=in_specs, out_specs=out_specs, out_shape=out_shape,
            compiler_params=_params(("arbitrary",)))(*args)
        return dx, dng, []
    wrapped, extra = _carried(body, 7, 2, carry, gather=False)
    res = pl.pallas_call(
        wrapped, name="inproj_bwd_scatter", grid=(S // tm,), in_specs=in_specs + extra["in_specs"],
        out_specs=out_specs + extra["out_specs"], out_shape=out_shape + extra["out_shape"],
        scratch_shapes=extra["scratch_shapes"], compiler_params=_params(("arbitrary",)))(*args, *carry)
    return res[0], res[1], list(res[2:])


def _memkv_bwd(mem, mng, mkg, wmkv, wmkvt, dmk, dmv):
    M = mem.shape[0]

    def body(mem_ref, mng_ref, mkg_ref, w_ref, wt_ref, dmk_ref, dmv_ref, dw_ref, dmng_ref, dmkg_ref, d_s):
        r, mhat = _rms_parts(mem_ref[...])
        mn = (mhat * mng_ref[...]).astype(BF16)
        mkv = _dot(mn, w_ref[...])
        dmkg = jnp.zeros((1, MEM_HD), F32)
        for h in range(MEM_HEADS):
            sl = slice(h * MEM_HD, (h + 1) * MEM_HD)
            rk, khat = _rms_parts(mkv[:, 2 * MEM_HD * h:2 * MEM_HD * h + MEM_HD])
            dkn = dmk_ref[:, sl]
            dmkg = dmkg + _colsum(dkn * khat)
            d_s[:, 2 * MEM_HD * h:2 * MEM_HD * h + MEM_HD] = _rms_bwd(dkn * mkg_ref[...], khat, rk, MEM_HD).astype(BF16)
            d_s[:, 2 * MEM_HD * h + MEM_HD:2 * MEM_HD * (h + 1)] = dmv_ref[:, sl].astype(BF16)
        dmkg_ref[...] = dmkg
        dw_ref[...] = _dot_tn(mn, d_s[...])
        dmn = _dot(d_s[...], wt_ref[...])
        dmng_ref[...] = _colsum(dmn * mhat)

    return pl.pallas_call(
        body, name="memkv_bwd",
        out_shape=[jax.ShapeDtypeStruct((D_MODEL, 2 * MEM_W), F32), jax.ShapeDtypeStruct((1, D_MODEL), F32),
                   jax.ShapeDtypeStruct((1, MEM_HD), F32)],
        scratch_shapes=[pltpu.VMEM((M, 2 * MEM_W), BF16)],
        compiler_params=pltpu.CompilerParams(vmem_limit_bytes=VMEM_LIMIT_V7X),
    )(mem, mng, mkg, wmkv, wmkvt, dmk, dmv)


def _mm_tn(a, b, name, col0=0, ncols=None):
    S, M = a.shape
    N = b.shape[1] if ncols is None else ncols
    tm, tn, ts = _tile(M, 1024), _tile(N, 1024), _tile(S, 2048)
    assert col0 % tn == 0
    jb = col0 // tn

    def body(a_ref, b_ref, o_ref):
        @pl.when(pl.program_id(2) == 0)
        def _():
            o_ref[...] = jnp.zeros(o_ref.shape, F32)

        o_ref[...] += _dot_tn(a_ref[...].astype(BF16), b_ref[...].astype(BF16))

    return pl.pallas_call(
        body, name=name, grid=(M // tm, N // tn, S // ts),
        in_specs=[pl.BlockSpec((ts, tm), lambda i, j, k: (k, i)),
                  pl.BlockSpec((ts, tn), lambda i, j, k: (k, j + jb))],
        out_specs=pl.BlockSpec((tm, tn), lambda i, j, k: (i, j)),
        out_shape=jax.ShapeDtypeStruct((M, N), F32),
        compiler_params=_params(("parallel", "parallel", "arbitrary")),
    )(a, b)


def _adamw(w, g0, g1, m, v, name):
    R, C = w.shape
    tr = R
    for cand in (512, 256, 128, 64, 32, 16, 8):
        if R % cand == 0 and cand * C * 4 <= (1 << 20):
            tr = cand
            break
    c1 = 1.0 / (1.0 - ADAM_B1 ** ADAM_STEP)
    c2 = 1.0 / (1.0 - ADAM_B2 ** ADAM_STEP)

    def body(w_ref, g0_ref, g1_ref, m_ref, v_ref, g_ref, d_ref, nm_ref, nv_ref):
        g = g0_ref[...] + g1_ref[...]
        nm = ADAM_B1 * m_ref[...] + (1.0 - ADAM_B1) * g
        nv = ADAM_B2 * v_ref[...] + (1.0 - ADAM_B2) * (g * g)
        g_ref[...] = g
        nm_ref[...] = nm
        nv_ref[...] = nv
        d_ref[...] = -ADAM_LR * ((nm * c1) / (jnp.sqrt(nv * c2) + ADAM_EPS) + ADAM_WD * w_ref[...])

    blk = pl.BlockSpec((tr, C), lambda i: (i, 0))
    return pl.pallas_call(
        body, name=name, grid=(R // tr,),
        in_specs=[blk] * 5, out_specs=[blk] * 4,
        out_shape=[jax.ShapeDtypeStruct((R, C), F32)] * 4,
        compiler_params=_params(("parallel",)),
    )(w, g0, g1, m, v)


def _sum_slabs(a, name):
    K, R, C = a.shape
    tr = R
    for cand in (512, 256, 128, 64, 32, 16, 8):
        if R % cand == 0 and cand * C * 4 * K <= (4 << 20):
            tr = cand
            break

    def body(a_ref, o_ref):
        t = a_ref[0].astype(F32)
        for k in range(1, K):
            t = t + a_ref[k].astype(F32)
        o_ref[...] = t

    return pl.pallas_call(
        body, name=name, grid=(R // tr,),
        in_specs=[pl.BlockSpec((K, tr, C), lambda i: (0, i, 0))],
        out_specs=pl.BlockSpec((tr, C), lambda i: (i, 0)),
        out_shape=jax.ShapeDtypeStruct((R, C), F32),
        compiler_params=_params(("parallel",)),
    )(a)


def _gather_chips(arrs):
    n = len(arrs)
    halves = [a.shape[0] // 2 for a in arrs]
    assert all(a.shape[0] == 2 * hf for a, hf in zip(arrs, halves))

    def body(*refs):
        ins, outs = refs[:n], refs[n:2 * n]
        send1, recv1, send2, recv2, loc = refs[2 * n:]
        x, y, c = lax.axis_index("x"), lax.axis_index("y"), lax.axis_index("c")
        me = 2 * x + y
        mine = [pl.ds(c * hf, hf) for hf in halves]
        theirs = [pl.ds((1 - c) * hf, hf) for hf in halves]
        sibling = (x, y, 1 - c)
        waits = []
        for a in range(n):
            own = pltpu.make_async_copy(ins[a], outs[a].at[me], loc.at[a])
            own.start()
            waits.append(own.wait)

        def over_ici(a, k, src_chip):
            return pltpu.make_async_remote_copy(
                src_ref=ins[a].at[mine[a]], dst_ref=outs[a].at[src_chip, mine[a]], send_sem=send1.at[3 * a + k - 1],
                recv_sem=recv1.at[3 * a + k - 1], device_id=_chip_peer(x, y, c, k), device_id_type=MESH)

        def to_sibling(a, k, layers):
            block = outs[a].at[me ^ k, layers]
            return pltpu.make_async_remote_copy(
                src_ref=block, dst_ref=block, send_sem=send2.at[3 * a + k - 1], recv_sem=recv2.at[3 * a + k - 1],
                device_id=sibling, device_id_type=MESH)

        for a in range(n):
            for k in (1, 2, 3):
                cp = over_ici(a, k, me)
                cp.start()
                waits.append(cp.wait_send)
        for a in range(n):
            for k in (1, 2, 3):
                over_ici(a, k, me ^ k).wait_recv()
                cp = to_sibling(a, k, mine[a])
                cp.start()
                waits.append(cp.wait_send)
        for a in range(n):
            for k in (1, 2, 3):
                to_sibling(a, k, theirs[a]).wait_recv()
        for w in waits:
            w()

    return pl.pallas_call(
        body, name="gather_weights",
        in_specs=[HBM] * n, out_specs=[HBM] * n,
        out_shape=[jax.ShapeDtypeStruct((4,) + a.shape, a.dtype) for a in arrs],
        scratch_shapes=[pltpu.SemaphoreType.DMA((3 * n,)), pltpu.SemaphoreType.DMA((3 * n,)),
                        pltpu.SemaphoreType.DMA((3 * n,)), pltpu.SemaphoreType.DMA((3 * n,)),
                        pltpu.SemaphoreType.DMA((n,))],
    )(*arrs)


def _scatter_chips(arrs, small):
    n = len(arrs)

    def body(*refs):
        ins, small_in = refs[:n], refs[n]
        outs, small_out = refs[n + 1:2 * n + 1], refs[2 * n + 1]
        send, recv, loc, ssend, srecv = refs[2 * n + 2:]
        x, y, c = lax.axis_index("x"), lax.axis_index("y"), lax.axis_index("c")
        me = 2 * x + y
        me8 = 4 * x + 2 * y + c
        copies = []
        for a in range(n):
            own = pltpu.make_async_copy(ins[a].at[me], outs[a].at[me], loc.at[a])
            own.start()
            copies.append(own)
        own = pltpu.make_async_copy(small_in, small_out.at[me8], loc.at[n])
        own.start()
        copies.append(own)
        for k in range(1, 8):
            cp = pltpu.make_async_remote_copy(
                src_ref=small_in, dst_ref=small_out.at[me8], send_sem=ssend.at[k - 1], recv_sem=srecv.at[k - 1],
                device_id=(x ^ (k >> 2), y ^ ((k >> 1) & 1), c ^ (k & 1)), device_id_type=MESH)
            cp.start()
            copies.append(cp)
        for a in range(n):
            for k in (1, 2, 3):
                cp = pltpu.make_async_remote_copy(
                    src_ref=ins[a].at[me ^ k], dst_ref=outs[a].at[me], send_sem=send.at[3 * a + k - 1],
                    recv_sem=recv.at[3 * a + k - 1], device_id=_chip_peer(x, y, c, k), device_id_type=MESH)
                cp.start()
                copies.append(cp)
        for cp in copies:
            cp.wait()

    return pl.pallas_call(
        body, name="scatter_grads",
        in_specs=[HBM] * (n + 1), out_specs=[HBM] * (n + 1),
        out_shape=[jax.ShapeDtypeStruct(a.shape, a.dtype) for a in arrs]
        + [jax.ShapeDtypeStruct((8,) + small.shape, small.dtype)],
        scratch_shapes=[pltpu.SemaphoreType.DMA((3 * n,)), pltpu.SemaphoreType.DMA((3 * n,)),
                        pltpu.SemaphoreType.DMA((n + 1,)), pltpu.SemaphoreType.DMA((7,)),
                        pltpu.SemaphoreType.DMA((7,))],
    )(*arrs, small)


def _swap_cores(arrs):
    n = len(arrs)

    def body(*refs):
        ins, outs = refs[:n], refs[n:2 * n]
        send, recv = refs[2 * n:]
        x, y, c = lax.axis_index("x"), lax.axis_index("y"), lax.axis_index("c")
        copies = []
        for a in range(n):
            cp = pltpu.make_async_remote_copy(
                src_ref=ins[a], dst_ref=outs[a], send_sem=send.at[a], recv_sem=recv.at[a],
                device_id=(x, y, 1 - c), device_id_type=MESH)
            cp.start()
            copies.append(cp)
        for cp in copies:
            cp.wait()

    return pl.pallas_call(
        body, name="swap_cores",
        in_specs=[HBM] * n, out_specs=[HBM] * n,
        out_shape=[jax.ShapeDtypeStruct(a.shape, a.dtype) for a in arrs],
        scratch_shapes=[pltpu.SemaphoreType.DMA((n,)), pltpu.SemaphoreType.DMA((n,))],
    )(*arrs)


def _pad_last(a, n):
    return jnp.pad(a, [(0, 0)] * (a.ndim - 1) + [(0, n - a.shape[-1])])


def _pad_w_in(w):
    lead = w.shape[:-1]
    seg = lambda a, b: w[..., a:b]
    ga = _pad_last(seg(2720, 3232).reshape(lead + (N_HEADS, V_DIM)), HP).reshape(lead + (1024,))
    kpe = jnp.pad(seg(640, 672), [(0, 0)] * len(lead) + [(NOPE_DIM, HP - QK_DIM)])
    zero = jnp.zeros(lead + (PW - 7936,), w.dtype)
    return jnp.concatenate(
        [seg(4256, 7328), ga, seg(672, 1184), seg(2208, 2720), seg(3232, 3744), seg(3744, 4256),
         seg(1184, 1696), seg(1696, 2208), seg(0, 384), seg(384, 640), kpe, zero], axis=-1)


def _unpad_w_in(w):
    lead = w.shape[:-1]
    seg = lambda a, n: w[..., a:a + n]
    ga = seg(O_GA, 1024).reshape(lead + (N_HEADS, HP))[..., :V_DIM].reshape(lead + (N_HEADS * V_DIM,))
    return jnp.concatenate(
        [seg(O_QL, 384), seg(O_KVL, 256), seg(O_KPE + NOPE_DIM, ROPE_DIM), seg(O_CB, 512), seg(O_CC, 512),
         seg(O_CU, 512), seg(O_QM, 512), ga, seg(O_GC, 512), seg(O_GM, 512), seg(O_R, 3072)], axis=-1)


def _cols_from_shards(g):
    _, L, R, C = g.shape
    return jnp.transpose(g, (1, 2, 0, 3)).reshape(L, R, 4 * C)


def _t(w):
    return jnp.swapaxes(w, -1, -2)


def _layer_fwd(x, mem, tabs, p, next_shards=()):
    proj, xn, gathered = _inproj(x, p["norm_g"], p["w_in"], next_shards)
    q, k, v, qn, kvn, kmax2, qt, vt = _mla_prep(proj, tabs, p["q_norm_g"], p["kv_norm_g"], p["q_head_g"], p["k_head_g"],
                                        p["w_uq"], p["w_uk"], p["w_uv"])
    o, lse = _attn_fwd(q, k, v)
    mk, mv = _memkv(mem, p["mem_norm_g"], p["mem_k_g"], p["w_mkv"])
    x_new, oa, oc, om, u, y = _merge(proj, o, x, p["b_gate"], p["conv_wb"], p["mem_q_g"], mk, mv,
                                     p["w_br_attn"], p["w_br_conv"], p["w_br_mem"], p["w_out"])
    saved = dict(x=x, proj=proj, xn=xn, q=q, qt=qt, k=k, v=v, qn=qn, kvn=kvn, o=o, lse=lse, mk=mk, mv=mv,
                 oa=oa, oc=oc, om=om, u=u, y=y)
    return x_new, saved, gathered


def _layer_bwd(g, mem, tabs, p, s, to_owner=()):
    S = g.shape[0]
    dmain, dcv, d_o, delta, du, dbg, dmk, dmv, dmqg, d_ot = _merge_bwd(
        g, s["proj"], s["o"], s["u"], p["b_gate"], p["conv_wb"], p["mem_q_g"], s["mk"], s["mv"],
        p["w_out_t"], p["w_br_attn_t"], p["w_br_conv_t"], p["w_br_mem_t"])
    dccu, dconv = _conv_bwd(dcv, s["proj"], p["conv_wb"])
    dq, dkt, dvt = _attn_bwd(s["q"], s["k"], s["v"], d_o, s["qt"], d_ot, s["lse"], delta)
    dk, dv = _t(dkt), _t(dvt).astype(BF16)
    dsega, dqraw, dkraw, dqg, dkvg, dqhg, dkhg = _mla_prep_bwd(
        s["proj"], tabs, dq, dk, dv, p["q_norm_g"], p["kv_norm_g"], p["q_head_g"], p["k_head_g"],
        p["w_uq"], p["w_uk"], p["w_uq_t"], p["w_uk_t"], p["w_uv_t"])
    dx, dng, received = _inproj_bwd(dmain, dccu, dsega, p["w_in_t"], s["x"], g, p["norm_g"], to_owner)
    dwmkv, dmng, dmkg = _memkv_bwd(mem, p["mem_norm_g"], p["mem_k_g"], p["w_mkv"], p["w_mkv_t"], dmk, dmv)
    grads = dict(
        norm_g=dng, b_gate=dbg, q_norm_g=dqg, kv_norm_g=dkvg, q_head_g=dqhg, k_head_g=dkhg,
        conv_wb=dconv, mem_norm_g=dmng, mem_q_g=dmqg, mem_k_g=dmkg, w_mkv=dwmkv,
        w_in=jnp.concatenate([_mm_tn(s["xn"], dmain, "grad_w_in"), _mm_tn(s["xn"], dccu, "grad_w_in_conv"),
                              _mm_tn(s["xn"], dsega, "grad_w_in_lat")], axis=1),
        w_uq=_mm_tn(s["qn"], dqraw, "grad_w_uq"),
        w_uk=_mm_tn(s["kvn"], dkraw, "grad_w_uk"),
        w_uv=_mm_tn(s["kvn"], dv, "grad_w_uv"),
        w_br_attn=_mm_tn(s["oa"], du, "grad_w_br_attn", 0, 1024),
        w_br_conv=_mm_tn(s["oc"], du, "grad_w_br_conv", 1024, 1024),
        w_br_mem=_mm_tn(s["om"], du, "grad_w_br_mem", 2048, 1024),
        w_out=_mm_tn(s["y"], g, "grad_w_out"),
    )
    return dx, grads, received


def _layer_params(big, full, l):
    p = {}
    w_in = _pad_w_in(big["w_in"])
    w_uq = _pad_last(big["w_uq"].reshape(Q_RANK, N_HEADS, QK_DIM), HP).reshape(Q_RANK, 1024)
    ukv = big["w_ukv"].reshape(KV_RANK, N_HEADS, NOPE_DIM + V_DIM)
    w_uk = _pad_last(ukv[..., :NOPE_DIM], HP).reshape(KV_RANK, 1024)
    w_uv = _pad_last(ukv[..., NOPE_DIM:], HP).reshape(KV_RANK, 1024)
    w_ba = jnp.pad(big["w_br_attn"].reshape(N_HEADS, V_DIM, D_MODEL), ((0, 0), (0, HP - V_DIM), (0, 0)))
    w_ba = w_ba.reshape(1024, D_MODEL)
    p.update(w_in=w_in, w_uq=w_uq, w_uk=w_uk, w_uv=w_uv, w_br_attn=w_ba, w_br_conv=big["w_br_conv"],
             w_br_mem=big["w_br_mem"], w_out=big["w_out"], w_mkv=big["w_mkv"])
    for n in ("w_in", "w_uq", "w_uk", "w_uv", "w_br_attn", "w_br_conv", "w_br_mem", "w_out", "w_mkv"):
        p[n + "_t"] = _t(p[n])
    for n in ("norm_g", "b_gate", "q_norm_g", "kv_norm_g", "mem_norm_g", "mem_q_g", "mem_k_g"):
        p[n] = full[n][l][None, :]
    p["q_head_g"] = _pad_last(full["q_head_g"][l][None, :], HP)
    p["k_head_g"] = _pad_last(full["k_head_g"][l][None, :], HP)
    p["conv_wb"] = jnp.concatenate(
        [full["conv_w"][l], full["conv_b"][l][None, :], jnp.zeros((4, CONV_W), F32)], axis=0)
    return p


def _join_shards(name, g):
    _, R, C = g.shape
    if name in _COL_SHARDED:
        return jnp.transpose(g, (1, 0, 2)).reshape(R, 4 * C)
    return g.reshape(4 * R, C)


def _split_shards(name, w):
    R, C = w.shape
    if name in _COL_SHARDED:
        return jnp.transpose(w.reshape(R, 4, C // 4), (1, 0, 2)).astype(BF16)
    return w.reshape(4, R // 4, C).astype(BF16)


def _unpad_grads(gp):
    out = {"w_in": _unpad_w_in(gp["w_in"])}
    out["w_uq"] = gp["w_uq"].reshape(Q_RANK, N_HEADS, HP)[..., :QK_DIM].reshape(Q_RANK, N_HEADS * QK_DIM)
    duk = gp["w_uk"].reshape(KV_RANK, N_HEADS, HP)[..., :NOPE_DIM]
    duv = gp["w_uv"].reshape(KV_RANK, N_HEADS, HP)[..., :V_DIM]
    out["w_ukv"] = jnp.concatenate([duk, duv], axis=-1).reshape(KV_RANK, 1024)
    out["w_br_attn"] = gp["w_br_attn"].reshape(N_HEADS, HP, D_MODEL)[:, :V_DIM].reshape(512, D_MODEL)
    for n in ("w_br_conv", "w_br_mem", "w_out", "w_mkv"):
        out[n] = gp[n]
    return out


def _train_step(x, mem, positions, w, target):
    S = x.shape[0]
    invf16 = ROPE_BASE ** (-jnp.arange(0, ROPE_DIM, 2, dtype=F32) / ROPE_DIM)
    invf = jnp.concatenate([jnp.zeros((NOPE_DIM,), F32), invf16, invf16, jnp.zeros((HP - QK_DIM,), F32)])[None, :]
    tabs = _rope_tables(jnp.broadcast_to(positions.reshape(S, 1), (S, HP)), invf)
    shards = [[w[n][l].astype(BF16) for n in _BIG] for l in range(DEPTH)]
    first = _gather_chips(shards[0] + [w["conv_w"]])
    gathered = first[:-1]
    full = {n: w[n] for n in _SMALL}
    full["conv_w"] = _cols_from_shards(first[-1])
    params, saved = [], []
    h = x
    for l in range(DEPTH):
        big = {n: _join_shards(n, g) for n, g in zip(_BIG, gathered)}
        params.append(_layer_params(big, full, l))
        h, s, gathered = _layer_fwd(h, mem, tabs, params[l], shards[l + 1] if l + 1 < DEPTH else ())
        saved.append(s)
    g, loss_part = _loss_head(h, target)
    per_layer, received = [None] * DEPTH, [None] * DEPTH
    to_owner = ()
    for l in reversed(range(DEPTH)):
        g, per_layer[l], got = _layer_bwd(g, mem, tabs, params[l], saved[l], to_owner)
        if to_owner:
            received[l + 1] = got
        big_g = _unpad_grads(per_layer[l])
        to_owner = [_split_shards(n, big_g[n]) for n in _BIG]
    st = lambda n: jnp.stack([per_layer[l][n] for l in range(DEPTH)])
    small = {}
    for n in ("norm_g", "b_gate", "q_norm_g", "kv_norm_g", "mem_norm_g", "mem_q_g", "mem_k_g"):
        small[n] = st(n)[:, 0, :]
    small["q_head_g"] = st("q_head_g")[:, 0, :QK_DIM]
    small["k_head_g"] = st("k_head_g")[:, 0, :QK_DIM]
    cwb = st("conv_wb")
    small["conv_w"] = cwb[:, 0:3, :]
    small["conv_b"] = cwb[:, 3, :]
    return loss_part, g, received, to_owner, small


_COL_SHARDED = ("w_in", "w_uq", "w_ukv", "w_br_attn", "w_br_conv", "w_br_mem")
_ROW_SHARDED = ("w_mkv", "w_out")
_BIG = _COL_SHARDED + _ROW_SHARDED
_SMALL = ("norm_g", "b_gate", "q_norm_g", "kv_norm_g", "q_head_g", "k_head_g", "conv_w", "conv_b",
          "mem_norm_g", "mem_q_g", "mem_k_g")
_ORDER = ("norm_g", "w_in", "b_gate", "q_norm_g", "w_uq", "kv_norm_g", "w_ukv", "q_head_g", "k_head_g",
          "conv_w", "conv_b", "mem_norm_g", "w_mkv", "mem_q_g", "mem_k_g", "w_br_attn", "w_br_conv",
          "w_br_mem", "w_out")


def _pack_small(d, extra):
    flat = jnp.concatenate([d[n].reshape(-1) for n in _SMALL] + [extra.reshape(-1)])
    n = flat.shape[0]
    rows = -(-n // 1024) * 8
    return jnp.pad(flat, (0, rows * 128 - n)).reshape(rows, 128)


def _unpack_small(packed, like):
    flat = packed.reshape(-1)
    out, off = {}, 0
    for n in _SMALL:
        sz = int(np.prod(like[n].shape))
        out[n] = flat[off:off + sz].reshape(like[n].shape)
        off += sz
    return out, flat[off:]


def kernel(x, mem, positions, norm_g, w_in, b_gate, q_norm_g, w_uq, kv_norm_g, w_ukv, q_head_g, k_head_g, conv_w, conv_b, mem_norm_g, w_mkv, mem_q_g, mem_k_g, w_br_attn, w_br_conv, w_br_mem, w_out, loss_target, m_norm_g, m_w_in, m_b_gate, m_q_norm_g, m_w_uq, m_kv_norm_g, m_w_ukv, m_q_head_g, m_k_head_g, m_conv_w, m_conv_b, m_mem_norm_g, m_w_mkv, m_mem_q_g, m_mem_k_g, m_w_br_attn, m_w_br_conv, m_w_br_mem, m_w_out, v_norm_g, v_w_in, v_b_gate, v_q_norm_g, v_w_uq, v_kv_norm_g, v_w_ukv, v_q_head_g, v_k_head_g, v_conv_w, v_conv_b, v_mem_norm_g, v_w_mkv, v_mem_q_g, v_mem_k_g, v_w_br_attn, v_w_br_conv, v_w_br_mem, v_w_out):
    w = dict(norm_g=norm_g, w_in=w_in, b_gate=b_gate, q_norm_g=q_norm_g, w_uq=w_uq, kv_norm_g=kv_norm_g,
             w_ukv=w_ukv, q_head_g=q_head_g, k_head_g=k_head_g, conv_w=conv_w, conv_b=conv_b,
             mem_norm_g=mem_norm_g, w_mkv=w_mkv, mem_q_g=mem_q_g, mem_k_g=mem_k_g, w_br_attn=w_br_attn,
             w_br_conv=w_br_conv, w_br_mem=w_br_mem, w_out=w_out)
    m = dict(norm_g=m_norm_g, w_in=m_w_in, b_gate=m_b_gate, q_norm_g=m_q_norm_g, w_uq=m_w_uq,
             kv_norm_g=m_kv_norm_g, w_ukv=m_w_ukv, q_head_g=m_q_head_g, k_head_g=m_k_head_g, conv_w=m_conv_w,
             conv_b=m_conv_b, mem_norm_g=m_mem_norm_g, w_mkv=m_w_mkv, mem_q_g=m_mem_q_g, mem_k_g=m_mem_k_g,
             w_br_attn=m_w_br_attn, w_br_conv=m_w_br_conv, w_br_mem=m_w_br_mem, w_out=m_w_out)
    v = dict(norm_g=v_norm_g, w_in=v_w_in, b_gate=v_b_gate, q_norm_g=v_q_norm_g, w_uq=v_w_uq,
             kv_norm_g=v_kv_norm_g, w_ukv=v_w_ukv, q_head_g=v_q_head_g, k_head_g=v_k_head_g, conv_w=v_conv_w,
             conv_b=v_conv_b, mem_norm_g=v_mem_norm_g, w_mkv=v_w_mkv, mem_q_g=v_mem_q_g, mem_k_g=v_mem_k_g,
             w_br_attn=v_w_br_attn, w_br_conv=v_w_br_conv, w_br_mem=v_w_br_mem, w_out=v_w_out)
    chip = 2 * lax.axis_index("x") + lax.axis_index("y")

    loss_part, grad_x, received, last_slabs, grads = _train_step(x[0], mem[0], positions[0], w, loss_target[0])

    loss_vec = jnp.zeros((128,), F32).at[0].set(0.5 / D_MODEL * jnp.sum(loss_part))
    small = _pack_small(grads, loss_vec)
    scattered = _scatter_chips(last_slabs, small)
    received[0] = scattered[:-1]
    small_sum = _sum_slabs(scattered[-1], "sum_small")
    partial = []
    for a, n in enumerate(_BIG):
        partial.append(jnp.concatenate([_sum_slabs(received[l][a], "sum_" + n) for l in range(DEPTH)], axis=0))
    other = _swap_cores(partial)

    small_g, tail = _unpack_small(small_sum, {n: (grads[n]) for n in _SMALL})
    loss = tail[0]
    small_g["conv_w"] = lax.dynamic_slice_in_dim(small_g["conv_w"], chip * (CONV_W // 4), CONV_W // 4, axis=2)

    outs_g, outs_d, outs_m, outs_v = {}, {}, {}, {}
    for n, p0, p1 in zip(_BIG, partial, other):
        shape = w[n].shape
        flat = lambda t: t.reshape(p0.shape)
        g_, d_, m_, v_ = _adamw(flat(w[n]), p0, p1, flat(m[n]), flat(v[n]), "adamw_" + n)
        outs_g[n], outs_d[n], outs_m[n], outs_v[n] = (t.reshape(shape) for t in (g_, d_, m_, v_))
    zero_small = jnp.zeros_like(small_sum)
    pk = lambda d: _pack_small(d, jnp.zeros((128,), F32))
    g_, d_, m_, v_ = _adamw(pk(w), _pack_small(small_g, jnp.zeros((128,), F32)), zero_small, pk(m), pk(v),
                            "adamw_small")
    like = {n: w[n] for n in _SMALL}
    for dst, packed in ((outs_g, g_), (outs_d, d_), (outs_m, m_), (outs_v, v_)):
        dst.update(_unpack_small(packed, like)[0])

    return (loss, grad_x[None], *[outs_g[n] for n in _ORDER], *[outs_d[n] for n in _ORDER],
            *[outs_m[n] for n in _ORDER], *[outs_v[n] for n in _ORDER])
```

```python
import functools

import numpy as np
import jax
import jax.numpy as jnp
from jax import lax
from jax.experimental import pallas as pl
from jax.experimental.pallas import tpu as pltpu

F32 = jnp.float32
BF16 = jnp.bfloat16

D_MODEL = 1024
DEPTH = 4
N_HEADS = 8
QK_DIM = 96
NOPE_DIM = 64
ROPE_DIM = 32
V_DIM = 64
Q_RANK = 384
KV_RANK = 256
CONV_W = 512
MEM_HEADS = 4
MEM_HD = 128
MEM_W = 512
IN_WIDTH = 7328
PW = 8192
HP = 128
PROJ_HALO = 16
F32_HALO = 8
EPS = 1e-6
ROPE_BASE = 10000.0
SCALE = QK_DIM ** -0.5
MEM_SCALE = MEM_HD ** -0.5

ADAM_LR = 0.001
ADAM_B1 = 0.9
ADAM_B2 = 0.999
ADAM_EPS = 1e-08
ADAM_WD = 0.01
ADAM_STEP = 10

VMEM_LIMIT_V7X = 56 * 1024 * 1024

O_R, O_GA, O_CB, O_QM, O_GC, O_GM, O_CC, O_CU, O_QL, O_KVL, O_KPE = (
    0, 3072, 4096, 4608, 5120, 5632, 6144, 6656, 7168, 7552, 7808)


def _params(sem, vmem=VMEM_LIMIT_V7X):
    return pltpu.CompilerParams(dimension_semantics=sem, vmem_limit_bytes=vmem)


def _sigmoid(t):
    return 0.5 * jnp.tanh(0.5 * t) + 0.5


def _silu_and_grad(g):
    sg = _sigmoid(g)
    return g * sg, sg * (1.0 + g * (1.0 - sg))


def _rms(t, g, n=None):
    n = t.shape[-1] if n is None else n
    r = lax.rsqrt(jnp.sum(t * t, axis=-1, keepdims=True) * (1.0 / n) + EPS)
    return (t * r) * g


def _rms_parts(t, n=None):
    n = t.shape[-1] if n is None else n
    r = lax.rsqrt(jnp.sum(t * t, axis=-1, keepdims=True) * (1.0 / n) + EPS)
    return r, t * r


def _rms_bwd(dhat, hat, r, n):
    return r * (dhat - hat * (jnp.sum(dhat * hat, axis=-1, keepdims=True) * (1.0 / n)))


def _rope(t, c, sa, sb):
    return t * c + pltpu.roll(t, HP - 16, 1) * sa + pltpu.roll(t, 16, 1) * sb


def _rope_t(d, c, sa, sb):
    return d * c + pltpu.roll(d * sa, 16, 1) + pltpu.roll(d * sb, HP - 16, 1)


def _dot(a, b):
    return jnp.dot(a, b, preferred_element_type=F32)


def _dot_nt(a, b):
    return lax.dot_general(a, b, (((1,), (1,)), ((), ())), preferred_element_type=F32)


def _dot_tn(a, b):
    return lax.dot_general(a, b, (((0,), (0,)), ((), ())), preferred_element_type=F32)


def _colsum(t):
    return jnp.sum(t, axis=0, keepdims=True)


def _tile(n, t):
    t = min(n, t)
    assert n % t == 0, (n, t)
    return t


def _rope_tables(pos_b, invf):
    S = pos_b.shape[0]
    tm = _tile(S, 1024)

    def body(pos_ref, invf_ref, c_ref, sa_ref, sb_ref):
        ang = pos_ref[...].astype(F32) * invf_ref[...]
        lane = lax.broadcasted_iota(jnp.int32, ang.shape, 1)
        cs = jnp.cos(ang)
        sn = jnp.sin(ang)
        c_ref[...] = jnp.where(lane < NOPE_DIM, 1.0, jnp.where(lane < QK_DIM, cs, 0.0))
        sa_ref[...] = jnp.where((lane >= NOPE_DIM) & (lane < NOPE_DIM + 16), -sn, 0.0)
        sb_ref[...] = jnp.where((lane >= NOPE_DIM + 16) & (lane < QK_DIM), sn, 0.0)

    blk = pl.BlockSpec((tm, HP), lambda i: (i, 0))
    return pl.pallas_call(
        body, name="rope_tables", grid=(S // tm,),
        in_specs=[blk, pl.BlockSpec((1, HP), lambda i: (0, 0))],
        out_specs=[blk, blk, blk],
        out_shape=[jax.ShapeDtypeStruct((S, HP), F32)] * 3,
        compiler_params=_params(("parallel",)),
    )(pos_b, invf)


MESH = pl.DeviceIdType.MESH
HBM = pl.BlockSpec(memory_space=pltpu.HBM)


def _chip_peer(x, y, c, k):
    return (x ^ (k >> 1), y ^ (k & 1), c)


def _chip_copies(ins, outs, send, recv, loc, gather):
    x, y, c = lax.axis_index("x"), lax.axis_index("y"), lax.axis_index("c")
    me = 2 * x + y
    cps = []
    for a in range(len(ins)):
        cps.append(pltpu.make_async_copy(ins[a] if gather else ins[a].at[me], outs[a].at[me], loc.at[a]))
        for k in (1, 2, 3):
            cps.append(pltpu.make_async_remote_copy(
                src_ref=ins[a] if gather else ins[a].at[me ^ k], dst_ref=outs[a].at[me],
                send_sem=send.at[3 * a + k - 1], recv_sem=recv.at[3 * a + k - 1],
                device_id=_chip_peer(x, y, c, k), device_id_type=MESH))
    return cps


def _carried(body, n_in, n_out, carry, gather):
    n = len(carry)

    def wrapped(*refs):
        ins, cin = refs[:n_in], refs[n_in:n_in + n]
        outs, cout = refs[n_in + n:n_in + n + n_out], refs[n_in + n + n_out:n_in + 2 * n + n_out]
        send, recv, loc = refs[n_in + 2 * n + n_out:]
        i = pl.program_id(0)

        @pl.when(i == 0)
        def _():
            for cp in _chip_copies(cin, cout, send, recv, loc, gather):
                cp.start()

        body(*ins, *outs)

        @pl.when(i == pl.num_programs(0) - 1)
        def _():
            for cp in _chip_copies(cin, cout, send, recv, loc, gather):
                cp.wait()

    specs = dict(
        in_specs=[HBM] * n, out_specs=[HBM] * n,
        out_shape=[jax.ShapeDtypeStruct(((4,) + a.shape) if gather else a.shape, a.dtype) for a in carry],
        scratch_shapes=[pltpu.SemaphoreType.DMA((3 * n,)), pltpu.SemaphoreType.DMA((3 * n,)),
                        pltpu.SemaphoreType.DMA((n,))])
    return wrapped, specs


def _inproj(x, g, w, carry=()):
    S = x.shape[0]
    tm = _tile(S, 256)

    def body(x_ref, g_ref, w_ref, proj_ref, xn_ref):
        h = _rms(x_ref[...], g_ref[...]).astype(BF16)
        xn_ref[...] = h
        proj_ref[...] = _dot(h, w_ref[...]).astype(BF16)

    row = lambda n: pl.BlockSpec((tm, n), lambda i: (i, 0))
    in_specs = [row(D_MODEL), pl.BlockSpec((1, D_MODEL), lambda i: (0, 0)), pl.BlockSpec((D_MODEL, PW), lambda i: (0, 0))]
    out_specs = [row(PW), row(D_MODEL)]
    out_shape = [jax.ShapeDtypeStruct((S, PW), BF16), jax.ShapeDtypeStruct((S, D_MODEL), BF16)]
    if not carry:
        proj, xn = pl.pallas_call(
            body, name="inproj", grid=(S // tm,), in_specs=in_specs, out_specs=out_specs, out_shape=out_shape,
            compiler_params=_params(("parallel",)))(x, g, w)
        return proj, xn, []
    wrapped, extra = _carried(body, 3, 2, carry, gather=True)
    res = pl.pallas_call(
        wrapped, name="inproj_gather", grid=(S // tm,), in_specs=in_specs + extra["in_specs"],
        out_specs=out_specs + extra["out_specs"], out_shape=out_shape + extra["out_shape"],
        scratch_shapes=extra["scratch_shapes"], compiler_params=_params(("arbitrary",)))(x, g, w, *carry)
    return res[0], res[1], list(res[2:])


def _mla_prep(proj, tabs, qg, kvg, qhg, khg, wuq, wuk, wuv):
    S = proj.shape[0]
    tm = _tile(S, 512)

    def body(a_ref, c_ref, sa_ref, sb_ref, qg_ref, kvg_ref, qhg_ref, khg_ref, wuq_ref, wuk_ref, wuv_ref,
             q_ref, k_ref, v_ref, qn_ref, kvn_ref, bound_ref, qt_ref, vt_ref):
        gq = jnp.max(jnp.abs(qhg_ref[...]), axis=-1, keepdims=True)
        gk = jnp.max(jnp.abs(khg_ref[...]), axis=-1, keepdims=True)
        bound_ref[...] = jnp.broadcast_to(gq * gk * (QK_DIM ** 0.5 * 1.01) + 1e-6, bound_ref.shape)
        ql = a_ref[:, 0:Q_RANK].astype(F32)
        kvl = a_ref[:, Q_RANK:Q_RANK + KV_RANK].astype(F32)
        kpe = a_ref[:, Q_RANK + KV_RANK:Q_RANK + KV_RANK + HP].astype(F32)
        qn = _rms(ql, qg_ref[...]).astype(BF16)
        kvn = _rms(kvl, kvg_ref[...]).astype(BF16)
        qn_ref[...] = qn
        kvn_ref[...] = kvn
        qraw = _dot(qn, wuq_ref[...])
        kn = _dot(kvn, wuk_ref[...])
        vf = _dot(kvn, wuv_ref[...])
        v_ref[...] = vf.astype(BF16)
        c, sa, sb = c_ref[...], sa_ref[...], sb_ref[...]
        for h in range(N_HEADS):
            sl = slice(h * HP, (h + 1) * HP)
            tq = _rms(qraw[:, sl], qhg_ref[...], QK_DIM)
            qh = _rope(tq, c, sa, sb) * SCALE
            q_ref[:, sl] = qh.astype(BF16)
            qt_ref[sl, :] = qh.T.astype(BF16)
            vt_ref[sl, :] = vf[:, sl].T.astype(BF16)
            tk = _rms(kn[:, sl] + kpe, khg_ref[...], QK_DIM)
            k_ref[:, sl] = _rope(tk, c, sa, sb).astype(BF16)

    row = lambda w: pl.BlockSpec((tm, w), lambda i: (i, 0))
    full = lambda a: pl.BlockSpec(a.shape, lambda i: (0,) * a.ndim)
    return pl.pallas_call(
        body, name="mla_prep", grid=(S // tm,),
        in_specs=[pl.BlockSpec((tm, 1024), lambda i: (i, 7)), row(HP), row(HP), row(HP),
                  full(qg), full(kvg), full(qhg), full(khg), full(wuq), full(wuk), full(wuv)],
        out_specs=[row(1024), row(1024), row(1024), row(Q_RANK), row(KV_RANK),
                   pl.BlockSpec((1, HP), lambda i: (0, 0)),
                   pl.BlockSpec((1024, tm), lambda i: (0, i)), pl.BlockSpec((1024, tm), lambda i: (0, i))],
        out_shape=[jax.ShapeDtypeStruct((S, 1024), BF16)] * 3
        + [jax.ShapeDtypeStruct((S, Q_RANK), BF16), jax.ShapeDtypeStruct((S, KV_RANK), BF16),
           jax.ShapeDtypeStruct((1, HP), F32)] + [jax.ShapeDtypeStruct((1024, S), BF16)] * 2,
        compiler_params=_params(("arbitrary",)),
    )(proj, *tabs, qg, kvg, qhg, khg, wuq, wuk, wuv)


SAFE_SCORE_BOUND = 30.0


def _attn_fwd(qt, k, vt, score_bound):
    S = k.shape[0]
    tq, tk = _tile(S, 1024), _tile(S, 2048)
    nk = S // tk

    def body(qt_ref, k_ref, vt_ref, bound_ref, o_ref, lse_ref, m_s, l_s, acc_s):
        qtv = qt_ref[...]
        bound = bound_ref[0:1, 0:1]
        safe = jnp.max(bound) <= SAFE_SCORE_BOUND
        l_s[...] = jnp.zeros(l_s.shape, F32)
        acc_s[...] = jnp.zeros(acc_s.shape, F32)

        def keys(c):
            return pl.ds(pl.multiple_of(c * tk, tk), tk)

        @pl.when(safe)
        def _():
            def step(c, carry):
                pt = jnp.exp(_dot(k_ref[keys(c), :], qtv) - bound)
                l_s[...] += jnp.sum(pt, axis=0, keepdims=True)
                acc_s[...] += _dot(vt_ref[:, keys(c)], pt.astype(BF16))
                return carry

            lax.fori_loop(0, nk, step, 0)
            m_s[...] = jnp.broadcast_to(bound, m_s.shape)

        @pl.when(jnp.logical_not(safe))
        def _():
            m_s[...] = jnp.full(m_s.shape, -jnp.inf, F32)

            def step(c, carry):
                st = _dot(k_ref[keys(c), :], qtv)
                m_prev = m_s[...]
                m_new = jnp.maximum(m_prev, jnp.max(st, axis=0, keepdims=True))
                alpha = jnp.exp(m_prev - m_new)
                pt = jnp.exp(st - m_new)
                l_s[...] = alpha * l_s[...] + jnp.sum(pt, axis=0, keepdims=True)
                acc_s[...] = alpha * acc_s[...] + _dot(vt_ref[:, keys(c)], pt.astype(BF16))
                m_s[...] = m_new
                return carry

            lax.fori_loop(0, nk, step, 0)

        o_ref[...] = (acc_s[...] / l_s[...]).T
        lse_row = m_s[...] + jnp.log(l_s[...])
        lse_ref[0] = jnp.broadcast_to(lse_row, (HP, tq)).T[:, 0:1]

    return pl.pallas_call(
        body, name="attn_fwd", grid=(N_HEADS, S // tq),
        in_specs=[pl.BlockSpec((HP, tq), lambda h, i: (h, i)),
                  pl.BlockSpec((S, HP), lambda h, i: (0, h)),
                  pl.BlockSpec((HP, S), lambda h, i: (h, 0)),
                  pl.BlockSpec((1, HP), lambda h, i: (0, 0))],
        out_specs=[pl.BlockSpec((tq, HP), lambda h, i: (i, h)),
                   pl.BlockSpec((1, tq, 1), lambda h, i: (h, i, 0))],
        out_shape=[jax.ShapeDtypeStruct((S, N_HEADS * HP), F32),
                   jax.ShapeDtypeStruct((N_HEADS, S, 1), F32)],
        scratch_shapes=[pltpu.VMEM((1, tq), F32), pltpu.VMEM((1, tq), F32), pltpu.VMEM((HP, tq), F32)],
        compiler_params=_params(("parallel", "parallel")),
    )(qt, k, vt, score_bound)


def _memkv(mem, mng, mkg, wmkv):
    M = mem.shape[0]

    def body(mem_ref, mng_ref, mkg_ref, w_ref, mk_ref, mv_ref):
        mn = _rms(mem_ref[...], mng_ref[...]).astype(BF16)
        mkv = _dot(mn, w_ref[...])
        for h in range(MEM_HEADS):
            kraw = mkv[:, 2 * MEM_HD * h:2 * MEM_HD * h + MEM_HD]
            mk_ref[:, MEM_HD * h:MEM_HD * (h + 1)] = _rms(kraw, mkg_ref[...]).astype(BF16)
            mv_ref[:, MEM_HD * h:MEM_HD * (h + 1)] = mkv[:, 2 * MEM_HD * h + MEM_HD:2 * MEM_HD * (h + 1)].astype(BF16)

    return pl.pallas_call(
        body, name="memkv",
        out_shape=[jax.ShapeDtypeStruct((M, MEM_W), BF16)] * 2,
        compiler_params=pltpu.CompilerParams(vmem_limit_bytes=VMEM_LIMIT_V7X),
    )(mem, mng, mkg, wmkv)


def _conv_shifts(cc, cu, hp_ref, hn_ref, i, n_tiles, tm):
    z = cc * cu
    last = PROJ_HALO - 1
    zp = hp_ref[last:last + 1, 0:CONV_W].astype(F32) * hp_ref[last:last + 1, CONV_W:2 * CONV_W].astype(F32)
    zn = hn_ref[0:1, 0:CONV_W].astype(F32) * hn_ref[0:1, CONV_W:2 * CONV_W].astype(F32)
    zp = jnp.where(i == 0, 0.0, zp)
    zn = jnp.where(i == n_tiles - 1, 0.0, zn)
    row = lax.broadcasted_iota(jnp.int32, z.shape, 0)
    z_up = jnp.where(row == 0, zp, pltpu.roll(z, 1, 0))
    z_dn = jnp.where(row == tm - 1, zn, pltpu.roll(z, tm - 1, 0))
    return z, z_up, z_dn


def _halo_specs(tm, S, width, col, rows):
    per = tm // rows
    prev = pl.BlockSpec((rows, width), lambda i: (jnp.maximum(i * per - 1, 0), col))
    nxt = pl.BlockSpec((rows, width), lambda i: (jnp.minimum((i + 1) * per, S // rows - 1), col))
    return prev, nxt


def _mem_attend(qm, mqg, mk_h, mv_h):
    r, qhat = _rms_parts(qm)
    mq = (qhat * mqg).astype(BF16)
    s = _dot_nt(mq, mk_h) * MEM_SCALE
    e = jnp.exp(s - jnp.max(s, axis=-1, keepdims=True))
    p = e / jnp.sum(e, axis=-1, keepdims=True)
    pv = _dot(p.astype(BF16), mv_h)
    return r, qhat, mq, p, pv


def _merge(proj, o, x, bg, convw, mqg, mk, mv, wba, wbc, wbm, wo):
    S = x.shape[0]
    tm = _tile(S, 512)
    nt = S // tm

    def body(main_ref, ccu_ref, hp_ref, hn_ref, o_ref, x_ref, bg_ref, cw_ref, mqg_ref, mk_ref, mv_ref,
             wba_ref, wbc_ref, wbm_ref, wo_ref, xn_ref, oa_ref, oc_ref, om_ref, u_ref, y_ref):
        i = pl.program_id(0)
        sil_a, _ = _silu_and_grad(main_ref[:, O_GA:O_GA + 1024].astype(F32))
        oa = (o_ref[...] * sil_a).astype(BF16)
        oa_ref[...] = oa
        z, z_up, z_dn = _conv_shifts(ccu_ref[:, 0:CONV_W].astype(F32), ccu_ref[:, CONV_W:].astype(F32), hp_ref, hn_ref, i, nt, tm)
        cv = cw_ref[0:1, :] * z_up + cw_ref[1:2, :] * z + cw_ref[2:3, :] * z_dn + cw_ref[3:4, :]
        sil_c, _ = _silu_and_grad(main_ref[:, O_GC:O_GC + CONV_W].astype(F32))
        oc = (main_ref[:, O_CB:O_CB + CONV_W].astype(F32) * cv * sil_c).astype(BF16)
        oc_ref[...] = oc
        sil_m, _ = _silu_and_grad(main_ref[:, O_GM:O_GM + MEM_W].astype(F32))
        for h in range(MEM_HEADS):
            sl = slice(h * MEM_HD, (h + 1) * MEM_HD)
            qm = main_ref[:, O_QM + h * MEM_HD:O_QM + (h + 1) * MEM_HD].astype(F32)
            pv = _mem_attend(qm, mqg_ref[...], mk_ref[:, sl], mv_ref[:, sl])[4]
            om_ref[:, sl] = (pv * sil_m[:, sl]).astype(BF16)
        ua = _dot(oa, wba_ref[...])
        uc = _dot(oc, wbc_ref[...])
        um = _dot(om_ref[...], wbm_ref[...])
        u_ref[:, 0:1024] = ua.astype(BF16)
        u_ref[:, 1024:2048] = uc.astype(BF16)
        u_ref[:, 2048:3072] = um.astype(BF16)
        rg = _sigmoid(main_ref[:, O_R:O_R + 3072].astype(F32) + bg_ref[...])
        y = (rg[:, 0:1024] * ua + rg[:, 1024:2048] * uc + rg[:, 2048:3072] * um).astype(BF16)
        y_ref[...] = y
        xn_ref[...] = x_ref[...] + _dot(y, wo_ref[...])

    row = lambda w: pl.BlockSpec((tm, w), lambda i: (i, 0))
    full = lambda a: pl.BlockSpec(a.shape, lambda i: (0,) * a.ndim)
    hp, hn = _halo_specs(tm, S, 1024, 6, PROJ_HALO)
    return pl.pallas_call(
        body, name="merge", grid=(nt,),
        in_specs=[row(6144), pl.BlockSpec((tm, 1024), lambda i: (i, 6)), hp, hn, row(1024), row(1024),
                  full(bg), full(convw), full(mqg), full(mk), full(mv), full(wba), full(wbc), full(wbm), full(wo)],
        out_specs=[row(1024), row(1024), row(CONV_W), row(MEM_W), row(3072), row(1024)],
        out_shape=[jax.ShapeDtypeStruct((S, 1024), F32), jax.ShapeDtypeStruct((S, 1024), BF16),
                   jax.ShapeDtypeStruct((S, CONV_W), BF16), jax.ShapeDtypeStruct((S, MEM_W), BF16),
                   jax.ShapeDtypeStruct((S, 3072), BF16), jax.ShapeDtypeStruct((S, 1024), BF16)],
        compiler_params=_params(("parallel",)),
    )(proj, proj, proj, proj, o, x, bg, convw, mqg, mk, mv, wba, wbc, wbm, wo)


def _loss_head(xf, tgt):
    S = xf.shape[0]
    tm = _tile(S, 1024)

    def body(x_ref, t_ref, g_ref, acc_ref):
        @pl.when(pl.program_id(0) == 0)
        def _():
            acc_ref[...] = jnp.zeros(acc_ref.shape, F32)

        e = x_ref[...] - t_ref[...]
        g_ref[...] = e * (1.0 / D_MODEL)
        part = jnp.sum((e * e).reshape(tm // 8, 8, D_MODEL), axis=0)
        tot = part[:, 0:128]
        for k in range(1, D_MODEL // 128):
            tot = tot + part[:, 128 * k:128 * (k + 1)]
        acc_ref[...] += tot

    row = pl.BlockSpec((tm, D_MODEL), lambda i: (i, 0))
    return pl.pallas_call(
        body, name="loss_head", grid=(S // tm,),
        in_specs=[row, row],
        out_specs=[row, pl.BlockSpec((8, 128), lambda i: (0, 0))],
        out_shape=[jax.ShapeDtypeStruct((S, D_MODEL), F32), jax.ShapeDtypeStruct((8, 128), F32)],
        compiler_params=_params(("arbitrary",)),
    )(xf, tgt)


def _merge_bwd(g, proj, o, u, bg, convw, mqg, mk, mv, wot, wbat, wbct, wbmt):
    S = g.shape[0]
    tm = _tile(S, 256)
    nt = S // tm
    M = mk.shape[0]

    def body(g_ref, main_ref, ccu_ref, hp_ref, hn_ref, o_ref, u_ref, bg_ref, cw_ref, mqg_ref, mk_ref, mv_ref,
             wot_ref, wbat_ref, wbct_ref, wbmt_ref,
             dmain_ref, dcv_ref, do_ref, delta_ref, du_ref, dbg_ref, dmk_ref, dmv_ref, dmqg_ref, dot_ref):
        i = pl.program_id(0)

        @pl.when(i == 0)
        def _():
            dbg_ref[...] = jnp.zeros(dbg_ref.shape, F32)
            dmk_ref[...] = jnp.zeros(dmk_ref.shape, F32)
            dmv_ref[...] = jnp.zeros(dmv_ref.shape, F32)
            dmqg_ref[...] = jnp.zeros(dmqg_ref.shape, F32)

        dy = _dot(g_ref[...].astype(BF16), wot_ref[...])
        d_branch = []
        for b, wt_ref in enumerate((wbat_ref, wbct_ref, wbmt_ref)):
            cols = slice(O_R + b * D_MODEL, O_R + (b + 1) * D_MODEL)
            rg = _sigmoid(main_ref[:, cols].astype(F32) + bg_ref[:, cols])
            dr = dy * u_ref[:, cols].astype(F32) * rg * (1.0 - rg)
            dmain_ref[:, cols] = dr.astype(BF16)
            dbg_ref[:, cols] += _colsum(dr)
            du = (dy * rg).astype(BF16)
            du_ref[:, cols] = du
            d_branch.append(_dot(du, wt_ref[...]))
        do_a, do_c, do_m = d_branch

        sil_a, dsil_a = _silu_and_grad(main_ref[:, O_GA:O_GA + 1024].astype(F32))
        ov = o_ref[...]
        d_o = do_a * sil_a
        do_ref[...] = d_o.astype(BF16)
        dot_ref[...] = d_o.T.astype(BF16)
        dmain_ref[:, O_GA:O_GA + 1024] = (do_a * ov * dsil_a).astype(BF16)
        prod = d_o * ov
        for h in range(N_HEADS):
            delta_ref[h] = jnp.sum(prod[:, h * HP:(h + 1) * HP], axis=-1, keepdims=True)

        z, z_up, z_dn = _conv_shifts(ccu_ref[:, 0:CONV_W].astype(F32), ccu_ref[:, CONV_W:].astype(F32), hp_ref, hn_ref, i, nt, tm)
        cv = cw_ref[0:1, :] * z_up + cw_ref[1:2, :] * z + cw_ref[2:3, :] * z_dn + cw_ref[3:4, :]
        sil_c, dsil_c = _silu_and_grad(main_ref[:, O_GC:O_GC + CONV_W].astype(F32))
        cb = main_ref[:, O_CB:O_CB + CONV_W].astype(F32)
        dmain_ref[:, O_CB:O_CB + CONV_W] = (do_c * cv * sil_c).astype(BF16)
        dmain_ref[:, O_GC:O_GC + CONV_W] = (do_c * cb * cv * dsil_c).astype(BF16)
        dcv_ref[...] = do_c * cb * sil_c

        sil_m, dsil_m = _silu_and_grad(main_ref[:, O_GM:O_GM + MEM_W].astype(F32))
        for h in range(MEM_HEADS):
            sl = slice(h * MEM_HD, (h + 1) * MEM_HD)
            qm = main_ref[:, O_QM + h * MEM_HD:O_QM + (h + 1) * MEM_HD].astype(F32)
            mk_h, mv_h = mk_ref[:, sl], mv_ref[:, sl]
            r, qhat, mq, p, pv = _mem_attend(qm, mqg_ref[...], mk_h, mv_h)
            dom = do_m[:, sl]
            dmain_ref[:, O_GM + h * MEM_HD:O_GM + (h + 1) * MEM_HD] = (dom * pv * dsil_m[:, sl]).astype(BF16)
            dpv = (dom * sil_m[:, sl]).astype(BF16)
            dp = _dot_nt(dpv, mv_h)
            ds = (p * (dp - jnp.sum(dp * p, axis=-1, keepdims=True)) * MEM_SCALE).astype(BF16)
            dmq = _dot(ds, mk_h)
            dmk_ref[:, sl] += _dot_tn(ds, mq)
            dmv_ref[:, sl] += _dot_tn(p.astype(BF16), dpv)
            dmqg_ref[...] += _colsum(dmq * qhat)
            dqm = _rms_bwd(dmq * mqg_ref[...], qhat, r, MEM_HD)
            dmain_ref[:, O_QM + h * MEM_HD:O_QM + (h + 1) * MEM_HD] = dqm.astype(BF16)

    row = lambda w: pl.BlockSpec((tm, w), lambda i: (i, 0))
    full = lambda a: pl.BlockSpec(a.shape, lambda i: (0,) * a.ndim)
    acc = lambda r, c: pl.BlockSpec((r, c), lambda i: (0, 0))
    hp, hn = _halo_specs(tm, S, 1024, 6, PROJ_HALO)
    return pl.pallas_call(
        body, name="merge_bwd", grid=(nt,),
        in_specs=[row(1024), row(6144), pl.BlockSpec((tm, 1024), lambda i: (i, 6)), hp, hn, row(1024), row(3072),
                  full(bg), full(convw), full(mqg), full(mk), full(mv), full(wot), full(wbat), full(wbct), full(wbmt)],
        out_specs=[row(6144), row(CONV_W), row(1024), pl.BlockSpec((N_HEADS, tm, 1), lambda i: (0, i, 0)), row(3072),
                   acc(1, 3072), acc(M, MEM_W), acc(M, MEM_W), acc(1, MEM_HD),
                   pl.BlockSpec((1024, tm), lambda i: (0, i))],
        out_shape=[jax.ShapeDtypeStruct((S, 6144), BF16), jax.ShapeDtypeStruct((S, CONV_W), F32),
                   jax.ShapeDtypeStruct((S, 1024), BF16), jax.ShapeDtypeStruct((N_HEADS, S, 1), F32),
                   jax.ShapeDtypeStruct((S, 3072), BF16), jax.ShapeDtypeStruct((1, 3072), F32),
                   jax.ShapeDtypeStruct((M, MEM_W), F32), jax.ShapeDtypeStruct((M, MEM_W), F32),
                   jax.ShapeDtypeStruct((1, MEM_HD), F32), jax.ShapeDtypeStruct((1024, S), BF16)],
        compiler_params=_params(("arbitrary",)),
    )(g, proj, proj, proj, proj, o, u, bg, convw, mqg, mk, mv, wot, wbat, wbct, wbmt)


def _conv_bwd(dcv, proj, convw):
    S = dcv.shape[0]
    tm = _tile(S, 512)
    nt = S // tm

    def body(d_ref, dp_ref, dn_ref, ccu_ref, hp_ref, hn_ref, cw_ref, dccu_ref, dcw_ref):
        i = pl.program_id(0)

        @pl.when(i == 0)
        def _():
            dcw_ref[...] = jnp.zeros(dcw_ref.shape, F32)

        cc, cu = ccu_ref[:, 0:CONV_W].astype(F32), ccu_ref[:, CONV_W:].astype(F32)
        z, z_up, z_dn = _conv_shifts(cc, cu, hp_ref, hn_ref, i, nt, tm)
        d = d_ref[...]
        dprev = jnp.where(i == 0, 0.0, dp_ref[7:8, :])
        dnext = jnp.where(i == nt - 1, 0.0, dn_ref[0:1, :])
        row = lax.broadcasted_iota(jnp.int32, d.shape, 0)
        d_up = jnp.where(row == 0, dprev, pltpu.roll(d, 1, 0))
        d_dn = jnp.where(row == tm - 1, dnext, pltpu.roll(d, tm - 1, 0))
        dz = cw_ref[0:1, :] * d_dn + cw_ref[1:2, :] * d + cw_ref[2:3, :] * d_up
        dccu_ref[:, 0:CONV_W] = (dz * cu).astype(BF16)
        dccu_ref[:, CONV_W:] = (dz * cc).astype(BF16)
        dcw_ref[0:1, :] += _colsum(d * z_up)
        dcw_ref[1:2, :] += _colsum(d * z)
        dcw_ref[2:3, :] += _colsum(d * z_dn)
        dcw_ref[3:4, :] += _colsum(d)

    hp, hn = _halo_specs(tm, S, 1024, 6, PROJ_HALO)
    dp, dn = _halo_specs(tm, S, CONV_W, 0, F32_HALO)
    return pl.pallas_call(
        body, name="conv_bwd", grid=(nt,),
        in_specs=[pl.BlockSpec((tm, CONV_W), lambda i: (i, 0)), dp, dn,
                  pl.BlockSpec((tm, 1024), lambda i: (i, 6)), hp, hn,
                  pl.BlockSpec((8, CONV_W), lambda i: (0, 0))],
        out_specs=[pl.BlockSpec((tm, 1024), lambda i: (i, 0)), pl.BlockSpec((8, CONV_W), lambda i: (0, 0))],
        out_shape=[jax.ShapeDtypeStruct((S, 1024), BF16), jax.ShapeDtypeStruct((8, CONV_W), F32)],
        compiler_params=_params(("arbitrary",)),
    )(dcv, dcv, dcv, proj, proj, proj, convw)


def _attn_bwd(q, k, v, do, qt, dot, lse, delta):
    S = q.shape[0]
    tq, tk = _tile(S, 512), _tile(S, 4096)
    ni, nk = S // tq, S // tk

    def body(q_ref, do_ref, qt_ref, dot_ref, k_ref, v_ref, lse_ref, delta_ref, dq_ref, dkt_hbm, dvt_hbm,
             dq_s, dkt_s, dvt_s, sem):
        h, i = pl.program_id(0), pl.program_id(1)

        @pl.when(i == 0)
        def _():
            dkt_s[...] = jnp.zeros(dkt_s.shape, F32)
            dvt_s[...] = jnp.zeros(dvt_s.shape, F32)

        dq_s[...] = jnp.zeros(dq_s.shape, F32)
        qv, dov, qtv, dotv = q_ref[...], do_ref[...], qt_ref[...], dot_ref[...]
        lse_c, delta_c = lse_ref[0], delta_ref[0]

        def step(c, carry):
            cols = pl.ds(pl.multiple_of(c * tk, tk), tk)
            kc, vc = k_ref[cols, :], v_ref[cols, :]
            p = jnp.exp(_dot_nt(qv, kc) - lse_c)
            dp = _dot_nt(dov, vc)
            ds = (p * (dp - delta_c)).astype(BF16)
            dq_s[...] += _dot(ds, kc)
            dvt_s[:, cols] += _dot(dotv, p.astype(BF16))
            dkt_s[:, cols] += _dot(qtv, ds)
            return carry

        lax.fori_loop(0, nk, step, 0)
        dq_ref[...] = dq_s[...]

        @pl.when(i == ni - 1)
        def _():
            head = pl.ds(pl.multiple_of(h * HP, HP), HP)
            out_k = pltpu.make_async_copy(dkt_s, dkt_hbm.at[head, :], sem.at[0])
            out_v = pltpu.make_async_copy(dvt_s, dvt_hbm.at[head, :], sem.at[1])
            out_k.start()
            out_v.start()
            out_k.wait()
            out_v.wait()

    col = pl.BlockSpec((1, tq, 1), lambda h, i: (h, i, 0))
    blk = pl.BlockSpec((tq, HP), lambda h, i: (i, h))
    blkt = pl.BlockSpec((HP, tq), lambda h, i: (h, i))
    res = pl.BlockSpec((S, HP), lambda h, i: (0, h))
    whole = pl.BlockSpec(memory_space=pl.ANY)
    return pl.pallas_call(
        body, name="attn_bwd", grid=(N_HEADS, ni),
        in_specs=[blk, blk, blkt, blkt, res, res, col, col],
        out_specs=[blk, whole, whole],
        out_shape=[jax.ShapeDtypeStruct((S, N_HEADS * HP), F32), jax.ShapeDtypeStruct((N_HEADS * HP, S), F32),
                   jax.ShapeDtypeStruct((N_HEADS * HP, S), F32)],
        scratch_shapes=[pltpu.VMEM((tq, HP), F32), pltpu.VMEM((HP, S), F32), pltpu.VMEM((HP, S), F32),
                        pltpu.SemaphoreType.DMA((2,))],
        compiler_params=_params(("parallel", "arbitrary")),
    )(q, do, qt, dot, k, v, lse, delta)


def _mla_prep_bwd(proj, tabs, dq, dk, dv, qg, kvg, qhg, khg, wuq, wuk, wuqt, wukt, wuvt):
    S = proj.shape[0]
    tm = _tile(S, 512)

    def body(a_ref, c_ref, sa_ref, sb_ref, dq_ref, dk_ref, dv_ref, qg_ref, kvg_ref, qhg_ref, khg_ref,
             wuq_ref, wuk_ref, wuqt_ref, wukt_ref, wuvt_ref,
             da_ref, dqraw_ref, dkraw_ref, dqg_ref, dkvg_ref, dqhg_ref, dkhg_ref):
        @pl.when(pl.program_id(0) == 0)
        def _():
            dqg_ref[...] = jnp.zeros(dqg_ref.shape, F32)
            dkvg_ref[...] = jnp.zeros(dkvg_ref.shape, F32)
            dqhg_ref[...] = jnp.zeros(dqhg_ref.shape, F32)
            dkhg_ref[...] = jnp.zeros(dkhg_ref.shape, F32)

        ql = a_ref[:, 0:Q_RANK].astype(F32)
        kvl = a_ref[:, Q_RANK:Q_RANK + KV_RANK].astype(F32)
        kpe = a_ref[:, Q_RANK + KV_RANK:Q_RANK + KV_RANK + HP].astype(F32)
        rq, qhat = _rms_parts(ql)
        rkv, kvhat = _rms_parts(kvl)
        qraw = _dot((qhat * qg_ref[...]).astype(BF16), wuq_ref[...])
        kn = _dot((kvhat * kvg_ref[...]).astype(BF16), wuk_ref[...])
        c, sa, sb = c_ref[...], sa_ref[...], sb_ref[...]
        dkpe = jnp.zeros(kpe.shape, F32)
        dqhg = jnp.zeros((1, HP), F32)
        dkhg = jnp.zeros((1, HP), F32)
        for h in range(N_HEADS):
            sl = slice(h * HP, (h + 1) * HP)
            r, that = _rms_parts(qraw[:, sl], QK_DIM)
            dtn = _rope_t(dq_ref[:, sl], c, sa, sb) * SCALE
            dqhg = dqhg + _colsum(dtn * that)
            dqraw_ref[:, sl] = _rms_bwd(dtn * qhg_ref[...], that, r, QK_DIM).astype(BF16)
            r, that = _rms_parts(kn[:, sl] + kpe, QK_DIM)
            dtn = _rope_t(dk_ref[:, sl], c, sa, sb)
            dkhg = dkhg + _colsum(dtn * that)
            dkr = _rms_bwd(dtn * khg_ref[...], that, r, QK_DIM)
            dkraw_ref[:, sl] = dkr.astype(BF16)
            dkpe = dkpe + dkr
        dqhg_ref[...] += dqhg
        dkhg_ref[...] += dkhg
        dqn = _dot(dqraw_ref[...], wuqt_ref[...])
        dqg_ref[...] += _colsum(dqn * qhat)
        da_ref[:, 0:Q_RANK] = _rms_bwd(dqn * qg_ref[...], qhat, rq, Q_RANK).astype(BF16)
        dkvn = _dot(dkraw_ref[...], wukt_ref[...]) + _dot(dv_ref[...], wuvt_ref[...])
        dkvg_ref[...] += _colsum(dkvn * kvhat)
        da_ref[:, Q_RANK:Q_RANK + KV_RANK] = _rms_bwd(dkvn * kvg_ref[...], kvhat, rkv, KV_RANK).astype(BF16)
        da_ref[:, Q_RANK + KV_RANK:Q_RANK + KV_RANK + HP] = dkpe.astype(BF16)
        da_ref[:, Q_RANK + KV_RANK + HP:] = jnp.zeros((tm, 1024 - Q_RANK - KV_RANK - HP), BF16)

    row = lambda w: pl.BlockSpec((tm, w), lambda i: (i, 0))
    full = lambda a: pl.BlockSpec(a.shape, lambda i: (0,) * a.ndim)
    acc = lambda c: pl.BlockSpec((1, c), lambda i: (0, 0))
    return pl.pallas_call(
        body, name="mla_prep_bwd", grid=(S // tm,),
        in_specs=[pl.BlockSpec((tm, 1024), lambda i: (i, 7)), row(HP), row(HP), row(HP),
                  row(1024), row(1024), row(1024), full(qg), full(kvg), full(qhg), full(khg),
                  full(wuq), full(wuk), full(wuqt), full(wukt), full(wuvt)],
        out_specs=[row(1024), row(1024), row(1024), acc(Q_RANK), acc(KV_RANK), acc(HP), acc(HP)],
        out_shape=[jax.ShapeDtypeStruct((S, 1024), BF16)] * 3
        + [jax.ShapeDtypeStruct((1, Q_RANK), F32), jax.ShapeDtypeStruct((1, KV_RANK), F32),
           jax.ShapeDtypeStruct((1, HP), F32), jax.ShapeDtypeStruct((1, HP), F32)],
        compiler_params=_params(("arbitrary",)),
    )(proj, *tabs, dq, dk, dv, qg, kvg, qhg, khg, wuq, wuk, wuqt, wukt, wuvt)


def _inproj_bwd(dmain, dccu, dsega, wint, x, g, ng, carry=()):
    S = x.shape[0]
    tm = _tile(S, 256)
    nm, nc = dmain.shape[1], dccu.shape[1]
    assert nm + nc + dsega.shape[1] == PW

    def body(dm_ref, dc_ref, da_ref, w_ref, x_ref, g_ref, ng_ref, dx_ref, dng_ref):
        @pl.when(pl.program_id(0) == 0)
        def _():
            dng_ref[...] = jnp.zeros(dng_ref.shape, F32)

        dh = (_dot(dm_ref[...], w_ref[0:nm, :]) + _dot(dc_ref[...], w_ref[nm:nm + nc, :])
              + _dot(da_ref[...], w_ref[nm + nc:PW, :]))
        r, xhat = _rms_parts(x_ref[...])
        dng_ref[...] += _colsum(dh * xhat)
        dx_ref[...] = g_ref[...] + _rms_bwd(dh * ng_ref[...], xhat, r, D_MODEL)

    row = lambda w: pl.BlockSpec((tm, w), lambda i: (i, 0))
    in_specs = [row(nm), row(nc), row(dsega.shape[1]), pl.BlockSpec((PW, D_MODEL), lambda i: (0, 0)),
                row(D_MODEL), row(D_MODEL), pl.BlockSpec((1, D_MODEL), lambda i: (0, 0))]
    out_specs = [row(D_MODEL), pl.BlockSpec((1, D_MODEL), lambda i: (0, 0))]
    out_shape = [jax.ShapeDtypeStruct((S, D_MODEL), F32), jax.ShapeDtypeStruct((1, D_MODEL), F32)]
    args = (dmain, dccu, dsega, wint, x, g, ng)
    if not carry:
        dx, dng = pl.pallas_call(
            body, name="inproj_bwd", grid=(S // tm,), in_specs=in_specs, out_specs=out_specs, out_shape=out_shape,
            compiler_params=_params(("arbitrary",)))(*args)
        return dx, dng, []
    wrapped, extra = _carried(body, 7, 2, carry, gather=False)
    res = pl.pallas_call(
        wrapped, name="inproj_bwd_scatter", grid=(S // tm,), in_specs=in_specs + extra["in_specs"],
        out_specs=out_specs + extra["out_specs"], out_shape=out_shape + extra["out_shape"],
        scratch_shapes=extra["scratch_shapes"], compiler_params=_params(("arbitrary",)))(*args, *carry)
    return res[0], res[1], list(res[2:])


def _memkv_bwd(mem, mng, mkg, wmkv, wmkvt, dmk, dmv):
    M = mem.shape[0]

    def body(mem_ref, mng_ref, mkg_ref, w_ref, wt_ref, dmk_ref, dmv_ref, dw_ref, dmng_ref, dmkg_ref, d_s):
        r, mhat = _rms_parts(mem_ref[...])
        mn = (mhat * mng_ref[...]).astype(BF16)
        mkv = _dot(mn, w_ref[...])
        dmkg = jnp.zeros((1, MEM_HD), F32)
        for h in range(MEM_HEADS):
            sl = slice(h * MEM_HD, (h + 1) * MEM_HD)
            rk, khat = _rms_parts(mkv[:, 2 * MEM_HD * h:2 * MEM_HD * h + MEM_HD])
            dkn = dmk_ref[:, sl]
            dmkg = dmkg + _colsum(dkn * khat)
            d_s[:, 2 * MEM_HD * h:2 * MEM_HD * h + MEM_HD] = _rms_bwd(dkn * mkg_ref[...], khat, rk, MEM_HD).astype(BF16)
            d_s[:, 2 * MEM_HD * h + MEM_HD:2 * MEM_HD * (h + 1)] = dmv_ref[:, sl].astype(BF16)
        dmkg_ref[...] = dmkg
        dw_ref[...] = _dot_tn(mn, d_s[...])
        dmn = _dot(d_s[...], wt_ref[...])
        dmng_ref[...] = _colsum(dmn * mhat)

    return pl.pallas_call(
        body, name="memkv_bwd",
        out_shape=[jax.ShapeDtypeStruct((D_MODEL, 2 * MEM_W), F32), jax.ShapeDtypeStruct((1, D_MODEL), F32),
                   jax.ShapeDtypeStruct((1, MEM_HD), F32)],
        scratch_shapes=[pltpu.VMEM((M, 2 * MEM_W), BF16)],
        compiler_params=pltpu.CompilerParams(vmem_limit_bytes=VMEM_LIMIT_V7X),
    )(mem, mng, mkg, wmkv, wmkvt, dmk, dmv)


def _mm_tn(a, b, name, col0=0, ncols=None):
    S, M = a.shape
    N = b.shape[1] if ncols is None else ncols
    tm, tn, ts = _tile(M, 1024), _tile(N, 1024), _tile(S, 2048)
    assert col0 % tn == 0
    jb = col0 // tn

    def body(a_ref, b_ref, o_ref):
        @pl.when(pl.program_id(2) == 0)
        def _():
            o_ref[...] = jnp.zeros(o_ref.shape, F32)

        o_ref[...] += _dot_tn(a_ref[...].astype(BF16), b_ref[...].astype(BF16))

    return pl.pallas_call(
        body, name=name, grid=(M // tm, N // tn, S // ts),
        in_specs=[pl.BlockSpec((ts, tm), lambda i, j, k: (k, i)),
                  pl.BlockSpec((ts, tn), lambda i, j, k: (k, j + jb))],
        out_specs=pl.BlockSpec((tm, tn), lambda i, j, k: (i, j)),
        out_shape=jax.ShapeDtypeStruct((M, N), F32),
        compiler_params=_params(("parallel", "parallel", "arbitrary")),
    )(a, b)


def _adamw(w, g0, g1, m, v, name):
    R, C = w.shape
    tr = R
    for cand in (512, 256, 128, 64, 32, 16, 8):
        if R % cand == 0 and cand * C * 4 <= (1 << 20):
            tr = cand
            break
    c1 = 1.0 / (1.0 - ADAM_B1 ** ADAM_STEP)
    c2 = 1.0 / (1.0 - ADAM_B2 ** ADAM_STEP)

    def body(w_ref, g0_ref, g1_ref, m_ref, v_ref, g_ref, d_ref, nm_ref, nv_ref):
        g = g0_ref[...] + g1_ref[...]
        nm = ADAM_B1 * m_ref[...] + (1.0 - ADAM_B1) * g
        nv = ADAM_B2 * v_ref[...] + (1.0 - ADAM_B2) * (g * g)
        g_ref[...] = g
        nm_ref[...] = nm
        nv_ref[...] = nv
        d_ref[...] = -ADAM_LR * ((nm * c1) / (jnp.sqrt(nv * c2) + ADAM_EPS) + ADAM_WD * w_ref[...])

    blk = pl.BlockSpec((tr, C), lambda i: (i, 0))
    return pl.pallas_call(
        body, name=name, grid=(R // tr,),
        in_specs=[blk] * 5, out_specs=[blk] * 4,
        out_shape=[jax.ShapeDtypeStruct((R, C), F32)] * 4,
        compiler_params=_params(("parallel",)),
    )(w, g0, g1, m, v)


def _sum_slabs(a, name):
    K, R, C = a.shape
    tr = R
    for cand in (512, 256, 128, 64, 32, 16, 8):
        if R % cand == 0 and cand * C * 4 * K <= (4 << 20):
            tr = cand
            break

    def body(a_ref, o_ref):
        t = a_ref[0].astype(F32)
        for k in range(1, K):
            t = t + a_ref[k].astype(F32)
        o_ref[...] = t

    return pl.pallas_call(
        body, name=name, grid=(R // tr,),
        in_specs=[pl.BlockSpec((K, tr, C), lambda i: (0, i, 0))],
        out_specs=pl.BlockSpec((tr, C), lambda i: (i, 0)),
        out_shape=jax.ShapeDtypeStruct((R, C), F32),
        compiler_params=_params(("parallel",)),
    )(a)


def _gather_chips(arrs):
    n = len(arrs)
    halves = [a.shape[0] // 2 for a in arrs]
    assert all(a.shape[0] == 2 * hf for a, hf in zip(arrs, halves))

    def body(*refs):
        ins, outs = refs[:n], refs[n:2 * n]
        send1, recv1, send2, recv2, loc = refs[2 * n:]
        x, y, c = lax.axis_index("x"), lax.axis_index("y"), lax.axis_index("c")
        me = 2 * x + y
        mine = [pl.ds(c * hf, hf) for hf in halves]
        theirs = [pl.ds((1 - c) * hf, hf) for hf in halves]
        sibling = (x, y, 1 - c)
        waits = []
        for a in range(n):
            own = pltpu.make_async_copy(ins[a], outs[a].at[me], loc.at[a])
            own.start()
            waits.append(own.wait)

        def over_ici(a, k, src_chip):
            return pltpu.make_async_remote_copy(
                src_ref=ins[a].at[mine[a]], dst_ref=outs[a].at[src_chip, mine[a]], send_sem=send1.at[3 * a + k - 1],
                recv_sem=recv1.at[3 * a + k - 1], device_id=_chip_peer(x, y, c, k), device_id_type=MESH)

        def to_sibling(a, k, layers):
            block = outs[a].at[me ^ k, layers]
            return pltpu.make_async_remote_copy(
                src_ref=block, dst_ref=block, send_sem=send2.at[3 * a + k - 1], recv_sem=recv2.at[3 * a + k - 1],
                device_id=sibling, device_id_type=MESH)

        for a in range(n):
            for k in (1, 2, 3):
                cp = over_ici(a, k, me)
                cp.start()
                waits.append(cp.wait_send)
        for a in range(n):
            for k in (1, 2, 3):
                over_ici(a, k, me ^ k).wait_recv()
                cp = to_sibling(a, k, mine[a])
                cp.start()
                waits.append(cp.wait_send)
        for a in range(n):
            for k in (1, 2, 3):
                to_sibling(a, k, theirs[a]).wait_recv()
        for w in waits:
            w()

    return pl.pallas_call(
        body, name="gather_weights",
        in_specs=[HBM] * n, out_specs=[HBM] * n,
        out_shape=[jax.ShapeDtypeStruct((4,) + a.shape, a.dtype) for a in arrs],
        scratch_shapes=[pltpu.SemaphoreType.DMA((3 * n,)), pltpu.SemaphoreType.DMA((3 * n,)),
                        pltpu.SemaphoreType.DMA((3 * n,)), pltpu.SemaphoreType.DMA((3 * n,)),
                        pltpu.SemaphoreType.DMA((n,))],
    )(*arrs)


def _scatter_chips(arrs, small):
    n = len(arrs)

    def body(*refs):
        ins, small_in = refs[:n], refs[n]
        outs, small_out = refs[n + 1:2 * n + 1], refs[2 * n + 1]
        send, recv, loc, ssend, srecv = refs[2 * n + 2:]
        x, y, c = lax.axis_index("x"), lax.axis_index("y"), lax.axis_index("c")
        me = 2 * x + y
        me8 = 4 * x + 2 * y + c
        copies = []
        for a in range(n):
            own = pltpu.make_async_copy(ins[a].at[me], outs[a].at[me], loc.at[a])
            own.start()
            copies.append(own)
        own = pltpu.make_async_copy(small_in, small_out.at[me8], loc.at[n])
        own.start()
        copies.append(own)
        for k in range(1, 8):
            cp = pltpu.make_async_remote_copy(
                src_ref=small_in, dst_ref=small_out.at[me8], send_sem=ssend.at[k - 1], recv_sem=srecv.at[k - 1],
                device_id=(x ^ (k >> 2), y ^ ((k >> 1) & 1), c ^ (k & 1)), device_id_type=MESH)
            cp.start()
            copies.append(cp)
        for a in range(n):
            for k in (1, 2, 3):
                cp = pltpu.make_async_remote_copy(
                    src_ref=ins[a].at[me ^ k], dst_ref=outs[a].at[me], send_sem=send.at[3 * a + k - 1],
                    recv_sem=recv.at[3 * a + k - 1], device_id=_chip_peer(x, y, c, k), device_id_type=MESH)
                cp.start()
                copies.append(cp)
        for cp in copies:
            cp.wait()

    return pl.pallas_call(
        body, name="scatter_grads",
        in_specs=[HBM] * (n + 1), out_specs=[HBM] * (n + 1),
        out_shape=[jax.ShapeDtypeStruct(a.shape, a.dtype) for a in arrs]
        + [jax.ShapeDtypeStruct((8,) + small.shape, small.dtype)],
        scratch_shapes=[pltpu.SemaphoreType.DMA((3 * n,)), pltpu.SemaphoreType.DMA((3 * n,)),
                        pltpu.SemaphoreType.DMA((n + 1,)), pltpu.SemaphoreType.DMA((7,)),
                        pltpu.SemaphoreType.DMA((7,))],
    )(*arrs, small)


def _swap_cores(arrs):
    n = len(arrs)

    def body(*refs):
        ins, outs = refs[:n], refs[n:2 * n]
        send, recv = refs[2 * n:]
        x, y, c = lax.axis_index("x"), lax.axis_index("y"), lax.axis_index("c")
        copies = []
        for a in range(n):
            cp = pltpu.make_async_remote_copy(
                src_ref=ins[a], dst_ref=outs[a], send_sem=send.at[a], recv_sem=recv.at[a],
                device_id=(x, y, 1 - c), device_id_type=MESH)
            cp.start()
            copies.append(cp)
        for cp in copies:
            cp.wait()

    return pl.pallas_call(
        body, name="swap_cores",
        in_specs=[HBM] * n, out_specs=[HBM] * n,
        out_shape=[jax.ShapeDtypeStruct(a.shape, a.dtype) for a in arrs],
        scratch_shapes=[pltpu.SemaphoreType.DMA((n,)), pltpu.SemaphoreType.DMA((n,))],
    )(*arrs)


def _pad_last(a, n):
    return jnp.pad(a, [(0, 0)] * (a.ndim - 1) + [(0, n - a.shape[-1])])


def _pad_w_in(w):
    lead = w.shape[:-1]
    seg = lambda a, b: w[..., a:b]
    ga = _pad_last(seg(2720, 3232).reshape(lead + (N_HEADS, V_DIM)), HP).reshape(lead + (1024,))
    kpe = jnp.pad(seg(640, 672), [(0, 0)] * len(lead) + [(NOPE_DIM, HP - QK_DIM)])
    zero = jnp.zeros(lead + (PW - 7936,), w.dtype)
    return jnp.concatenate(
        [seg(4256, 7328), ga, seg(672, 1184), seg(2208, 2720), seg(3232, 3744), seg(3744, 4256),
         seg(1184, 1696), seg(1696, 2208), seg(0, 384), seg(384, 640), kpe, zero], axis=-1)


def _unpad_w_in(w):
    lead = w.shape[:-1]
    seg = lambda a, n: w[..., a:a + n]
    ga = seg(O_GA, 1024).reshape(lead + (N_HEADS, HP))[..., :V_DIM].reshape(lead + (N_HEADS * V_DIM,))
    return jnp.concatenate(
        [seg(O_QL, 384), seg(O_KVL, 256), seg(O_KPE + NOPE_DIM, ROPE_DIM), seg(O_CB, 512), seg(O_CC, 512),
         seg(O_CU, 512), seg(O_QM, 512), ga, seg(O_GC, 512), seg(O_GM, 512), seg(O_R, 3072)], axis=-1)


def _cols_from_shards(g):
    _, L, R, C = g.shape
    return jnp.transpose(g, (1, 2, 0, 3)).reshape(L, R, 4 * C)


def _t(w):
    return jnp.swapaxes(w, -1, -2)


def _layer_fwd(x, mem, tabs, p, next_shards=()):
    proj, xn, gathered = _inproj(x, p["norm_g"], p["w_in"], next_shards)
    q, k, v, qn, kvn, score_bound, qt, vt = _mla_prep(proj, tabs, p["q_norm_g"], p["kv_norm_g"], p["q_head_g"], p["k_head_g"],
                                        p["w_uq"], p["w_uk"], p["w_uv"])
    o, lse = _attn_fwd(qt, k, vt, score_bound)
    mk, mv = _memkv(mem, p["mem_norm_g"], p["mem_k_g"], p["w_mkv"])
    x_new, oa, oc, om, u, y = _merge(proj, o, x, p["b_gate"], p["conv_wb"], p["mem_q_g"], mk, mv,
                                     p["w_br_attn"], p["w_br_conv"], p["w_br_mem"], p["w_out"])
    saved = dict(x=x, proj=proj, xn=xn, q=q, qt=qt, k=k, v=v, qn=qn, kvn=kvn, o=o, lse=lse, mk=mk, mv=mv,
                 oa=oa, oc=oc, om=om, u=u, y=y)
    return x_new, saved, gathered


def _layer_bwd(g, mem, tabs, p, s, to_owner=()):
    S = g.shape[0]
    dmain, dcv, d_o, delta, du, dbg, dmk, dmv, dmqg, d_ot = _merge_bwd(
        g, s["proj"], s["o"], s["u"], p["b_gate"], p["conv_wb"], p["mem_q_g"], s["mk"], s["mv"],
        p["w_out_t"], p["w_br_attn_t"], p["w_br_conv_t"], p["w_br_mem_t"])
    dccu, dconv = _conv_bwd(dcv, s["proj"], p["conv_wb"])
    dq, dkt, dvt = _attn_bwd(s["q"], s["k"], s["v"], d_o, s["qt"], d_ot, s["lse"], delta)
    dk, dv = _t(dkt), _t(dvt).astype(BF16)
    dsega, dqraw, dkraw, dqg, dkvg, dqhg, dkhg = _mla_prep_bwd(
        s["proj"], tabs, dq, dk, dv, p["q_norm_g"], p["kv_norm_g"], p["q_head_g"], p["k_head_g"],
        p["w_uq"], p["w_uk"], p["w_uq_t"], p["w_uk_t"], p["w_uv_t"])
    dx, dng, received = _inproj_bwd(dmain, dccu, dsega, p["w_in_t"], s["x"], g, p["norm_g"], to_owner)
    dwmkv, dmng, dmkg = _memkv_bwd(mem, p["mem_norm_g"], p["mem_k_g"], p["w_mkv"], p["w_mkv_t"], dmk, dmv)
    grads = dict(
        norm_g=dng, b_gate=dbg, q_norm_g=dqg, kv_norm_g=dkvg, q_head_g=dqhg, k_head_g=dkhg,
        conv_wb=dconv, mem_norm_g=dmng, mem_q_g=dmqg, mem_k_g=dmkg, w_mkv=dwmkv,
        w_in=jnp.concatenate([_mm_tn(s["xn"], dmain, "grad_w_in"), _mm_tn(s["xn"], dccu, "grad_w_in_conv"),
                              _mm_tn(s["xn"], dsega, "grad_w_in_lat")], axis=1),
        w_uq=_mm_tn(s["qn"], dqraw, "grad_w_uq"),
        w_uk=_mm_tn(s["kvn"], dkraw, "grad_w_uk"),
        w_uv=_mm_tn(s["kvn"], dv, "grad_w_uv"),
        w_br_attn=_mm_tn(s["oa"], du, "grad_w_br_attn", 0, 1024),
        w_br_conv=_mm_tn(s["oc"], du, "grad_w_br_conv", 1024, 1024),
        w_br_mem=_mm_tn(s["om"], du, "grad_w_br_mem", 2048, 1024),
        w_out=_mm_tn(s["y"], g, "grad_w_out"),
    )
    return dx, grads, received


def _layer_params(big, full, l):
    p = {}
    w_in = _pad_w_in(big["w_in"])
    w_uq = _pad_last(big["w_uq"].reshape(Q_RANK, N_HEADS, QK_DIM), HP).reshape(Q_RANK, 1024)
    ukv = big["w_ukv"].reshape(KV_RANK, N_HEADS, NOPE_DIM + V_DIM)
    w_uk = _pad_last(ukv[..., :NOPE_DIM], HP).reshape(KV_RANK, 1024)
    w_uv = _pad_last(ukv[..., NOPE_DIM:], HP).reshape(KV_RANK, 1024)
    w_ba = jnp.pad(big["w_br_attn"].reshape(N_HEADS, V_DIM, D_MODEL), ((0, 0), (0, HP - V_DIM), (0, 0)))
    w_ba = w_ba.reshape(1024, D_MODEL)
    p.update(w_in=w_in, w_uq=w_uq, w_uk=w_uk, w_uv=w_uv, w_br_attn=w_ba, w_br_conv=big["w_br_conv"],
             w_br_mem=big["w_br_mem"], w_out=big["w_out"], w_mkv=big["w_mkv"])
    for n in ("w_in", "w_uq", "w_uk", "w_uv", "w_br_attn", "w_br_conv", "w_br_mem", "w_out", "w_mkv"):
        p[n + "_t"] = _t(p[n])
    for n in ("norm_g", "b_gate", "q_norm_g", "kv_norm_g", "mem_norm_g", "mem_q_g", "mem_k_g"):
        p[n] = full[n][l][None, :]
    p["q_head_g"] = _pad_last(full["q_head_g"][l][None, :], HP)
    p["k_head_g"] = _pad_last(full["k_head_g"][l][None, :], HP)
    p["conv_wb"] = jnp.concatenate(
        [full["conv_w"][l], full["conv_b"][l][None, :], jnp.zeros((4, CONV_W), F32)], axis=0)
    return p


def _join_shards(name, g):
    _, R, C = g.shape
    if name in _COL_SHARDED:
        return jnp.transpose(g, (1, 0, 2)).reshape(R, 4 * C)
    return g.reshape(4 * R, C)


def _split_shards(name, w):
    R, C = w.shape
    if name in _COL_SHARDED:
        return jnp.transpose(w.reshape(R, 4, C // 4), (1, 0, 2)).astype(BF16)
    return w.reshape(4, R // 4, C).astype(BF16)


def _unpad_grads(gp):
    out = {"w_in": _unpad_w_in(gp["w_in"])}
    out["w_uq"] = gp["w_uq"].reshape(Q_RANK, N_HEADS, HP)[..., :QK_DIM].reshape(Q_RANK, N_HEADS * QK_DIM)
    duk = gp["w_uk"].reshape(KV_RANK, N_HEADS, HP)[..., :NOPE_DIM]
    duv = gp["w_uv"].reshape(KV_RANK, N_HEADS, HP)[..., :V_DIM]
    out["w_ukv"] = jnp.concatenate([duk, duv], axis=-1).reshape(KV_RANK, 1024)
    out["w_br_attn"] = gp["w_br_attn"].reshape(N_HEADS, HP, D_MODEL)[:, :V_DIM].reshape(512, D_MODEL)
    for n in ("w_br_conv", "w_br_mem", "w_out", "w_mkv"):
        out[n] = gp[n]
    return out


def _train_step(x, mem, positions, w, target):
    S = x.shape[0]
    invf16 = ROPE_BASE ** (-jnp.arange(0, ROPE_DIM, 2, dtype=F32) / ROPE_DIM)
    invf = jnp.concatenate([jnp.zeros((NOPE_DIM,), F32), invf16, invf16, jnp.zeros((HP - QK_DIM,), F32)])[None, :]
    tabs = _rope_tables(jnp.broadcast_to(positions.reshape(S, 1), (S, HP)), invf)
    shards = [[w[n][l].astype(BF16) for n in _BIG] for l in range(DEPTH)]
    first = _gather_chips(shards[0] + [w["conv_w"]])
    gathered = first[:-1]
    full = {n: w[n] for n in _SMALL}
    full["conv_w"] = _cols_from_shards(first[-1])
    params, saved = [], []
    h = x
    for l in range(DEPTH):
        big = {n: _join_shards(n, g) for n, g in zip(_BIG, gathered)}
        params.append(_layer_params(big, full, l))
        h, s, gathered = _layer_fwd(h, mem, tabs, params[l], shards[l + 1] if l + 1 < DEPTH else ())
        saved.append(s)
    g, loss_part = _loss_head(h, target)
    per_layer, received = [None] * DEPTH, [None] * DEPTH
    to_owner = ()
    for l in reversed(range(DEPTH)):
        g, per_layer[l], got = _layer_bwd(g, mem, tabs, params[l], saved[l], to_owner)
        if to_owner:
            received[l + 1] = got
        big_g = _unpad_grads(per_layer[l])
        to_owner = [_split_shards(n, big_g[n]) for n in _BIG]
    st = lambda n: jnp.stack([per_layer[l][n] for l in range(DEPTH)])
    small = {}
    for n in ("norm_g", "b_gate", "q_norm_g", "kv_norm_g", "mem_norm_g", "mem_q_g", "mem_k_g"):
        small[n] = st(n)[:, 0, :]
    small["q_head_g"] = st("q_head_g")[:, 0, :QK_DIM]
    small["k_head_g"] = st("k_head_g")[:, 0, :QK_DIM]
    cwb = st("conv_wb")
    small["conv_w"] = cwb[:, 0:3, :]
    small["conv_b"] = cwb[:, 3, :]
    return loss_part, g, received, to_owner, small


_COL_SHARDED = ("w_in", "w_uq", "w_ukv", "w_br_attn", "w_br_conv", "w_br_mem")
_ROW_SHARDED = ("w_mkv", "w_out")
_BIG = _COL_SHARDED + _ROW_SHARDED
_SMALL = ("norm_g", "b_gate", "q_norm_g", "kv_norm_g", "q_head_g", "k_head_g", "conv_w", "conv_b",
          "mem_norm_g", "mem_q_g", "mem_k_g")
_ORDER = ("norm_g", "w_in", "b_gate", "q_norm_g", "w_uq", "kv_norm_g", "w_ukv", "q_head_g", "k_head_g",
          "conv_w", "conv_b", "mem_norm_g", "w_mkv", "mem_q_g", "mem_k_g", "w_br_attn", "w_br_conv",
          "w_br_mem", "w_out")


def _pack_small(d, extra):
    flat = jnp.concatenate([d[n].reshape(-1) for n in _SMALL] + [extra.reshape(-1)])
    n = flat.shape[0]
    rows = -(-n // 1024) * 8
    return jnp.pad(flat, (0, rows * 128 - n)).reshape(rows, 128)


def _unpack_small(packed, like):
    flat = packed.reshape(-1)
    out, off = {}, 0
    for n in _SMALL:
        sz = int(np.prod(like[n].shape))
        out[n] = flat[off:off + sz].reshape(like[n].shape)
        off += sz
    return out, flat[off:]


def kernel(x, mem, positions, norm_g, w_in, b_gate, q_norm_g, w_uq, kv_norm_g, w_ukv, q_head_g, k_head_g, conv_w, conv_b, mem_norm_g, w_mkv, mem_q_g, mem_k_g, w_br_attn, w_br_conv, w_br_mem, w_out, loss_target, m_norm_g, m_w_in, m_b_gate, m_q_norm_g, m_w_uq, m_kv_norm_g, m_w_ukv, m_q_head_g, m_k_head_g, m_conv_w, m_conv_b, m_mem_norm_g, m_w_mkv, m_mem_q_g, m_mem_k_g, m_w_br_attn, m_w_br_conv, m_w_br_mem, m_w_out, v_norm_g, v_w_in, v_b_gate, v_q_norm_g, v_w_uq, v_kv_norm_g, v_w_ukv, v_q_head_g, v_k_head_g, v_conv_w, v_conv_b, v_mem_norm_g, v_w_mkv, v_mem_q_g, v_mem_k_g, v_w_br_attn, v_w_br_conv, v_w_br_mem, v_w_out):
    w = dict(norm_g=norm_g, w_in=w_in, b_gate=b_gate, q_norm_g=q_norm_g, w_uq=w_uq, kv_norm_g=kv_norm_g,
             w_ukv=w_ukv, q_head_g=q_head_g, k_head_g=k_head_g, conv_w=conv_w, conv_b=conv_b,
             mem_norm_g=mem_norm_g, w_mkv=w_mkv, mem_q_g=mem_q_g, mem_k_g=mem_k_g, w_br_attn=w_br_attn,
             w_br_conv=w_br_conv, w_br_mem=w_br_mem, w_out=w_out)
    m = dict(norm_g=m_norm_g, w_in=m_w_in, b_gate=m_b_gate, q_norm_g=m_q_norm_g, w_uq=m_w_uq,
             kv_norm_g=m_kv_norm_g, w_ukv=m_w_ukv, q_head_g=m_q_head_g, k_head_g=m_k_head_g, conv_w=m_conv_w,
             conv_b=m_conv_b, mem_norm_g=m_mem_norm_g, w_mkv=m_w_mkv, mem_q_g=m_mem_q_g, mem_k_g=m_mem_k_g,
             w_br_attn=m_w_br_attn, w_br_conv=m_w_br_conv, w_br_mem=m_w_br_mem, w_out=m_w_out)
    v = dict(norm_g=v_norm_g, w_in=v_w_in, b_gate=v_b_gate, q_norm_g=v_q_norm_g, w_uq=v_w_uq,
             kv_norm_g=v_kv_norm_g, w_ukv=v_w_ukv, q_head_g=v_q_head_g, k_head_g=v_k_head_g, conv_w=v_conv_w,
             conv_b=v_conv_b, mem_norm_g=v_mem_norm_g, w_mkv=v_w_mkv, mem_q_g=v_mem_q_g, mem_k_g=v_mem_k_g,
             w_br_attn=v_w_br_attn, w_br_conv=v_w_br_conv, w_br_mem=v_w_br_mem, w_out=v_w_out)
    chip = 2 * lax.axis_index("x") + lax.axis_index("y")

    loss_part, grad_x, received, last_slabs, grads = _train_step(x[0], mem[0], positions[0], w, loss_target[0])

    loss_vec = jnp.zeros((128,), F32).at[0].set(0.5 / D_MODEL * jnp.sum(loss_part))
    small = _pack_small(grads, loss_vec)
    scattered = _scatter_chips(last_slabs, small)
    received[0] = scattered[:-1]
    small_sum = _sum_slabs(scattered[-1], "sum_small")
    partial = []
    for a, n in enumerate(_BIG):
        partial.append(jnp.concatenate([_sum_slabs(received[l][a], "sum_" + n) for l in range(DEPTH)], axis=0))
    other = _swap_cores(partial)

    small_g, tail = _unpack_small(small_sum, {n: (grads[n]) for n in _SMALL})
    loss = tail[0]
    small_g["conv_w"] = lax.dynamic_slice_in_dim(small_g["conv_w"], chip * (CONV_W // 4), CONV_W // 4, axis=2)

    outs_g, outs_d, outs_m, outs_v = {}, {}, {}, {}
    for n, p0, p1 in zip(_BIG, partial, other):
        shape = w[n].shape
        flat = lambda t: t.reshape(p0.shape)
        g_, d_, m_, v_ = _adamw(flat(w[n]), p0, p1, flat(m[n]), flat(v[n]), "adamw_" + n)
        outs_g[n], outs_d[n], outs_m[n], outs_v[n] = (t.reshape(shape) for t in (g_, d_, m_, v_))
    zero_small = jnp.zeros_like(small_sum)
    pk = lambda d: _pack_small(d, jnp.zeros((128,), F32))
    g_, d_, m_, v_ = _adamw(pk(w), _pack_small(small_g, jnp.zeros((128,), F32)), zero_small, pk(m), pk(v),
                            "adamw_small")
    like = {n: w[n] for n in _SMALL}
    for dst, packed in ((outs_g, g_), (outs_d, d_), (outs_m, m_), (outs_v, v_)):
        dst.update(_unpack_small(packed, like)[0])

    return (loss, grad_x[None], *[outs_g[n] for n in _ORDER], *[outs_d[n] for n in _ORDER],
            *[outs_m[n] for n in _ORDER], *[outs_v[n] for n in _ORDER])
```

```python
import functools

import numpy as np
import jax
import jax.numpy as jnp
from jax import lax
from jax.experimental import pallas as pl
from jax.experimental.pallas import tpu as pltpu

F32 = jnp.float32
BF16 = jnp.bfloat16

D_MODEL = 1024
DEPTH = 4
N_HEADS = 8
QK_DIM = 96
NOPE_DIM = 64
ROPE_DIM = 32
V_DIM = 64
Q_RANK = 384
KV_RANK = 256
CONV_W = 512
MEM_HEADS = 4
MEM_HD = 128
MEM_W = 512
IN_WIDTH = 7328
PW = 8192
HP = 128
PROJ_HALO = 16
F32_HALO = 8
EPS = 1e-6
ROPE_BASE = 10000.0
SCALE = QK_DIM ** -0.5
MEM_SCALE = MEM_HD ** -0.5

ADAM_LR = 0.001
ADAM_B1 = 0.9
ADAM_B2 = 0.999
ADAM_EPS = 1e-08
ADAM_WD = 0.01
ADAM_STEP = 10

VMEM_LIMIT_V7X = 56 * 1024 * 1024

O_R, O_GA, O_CB, O_QM, O_GC, O_GM, O_CC, O_CU, O_QL, O_KVL, O_KPE = (
    0, 3072, 4096, 4608, 5120, 5632, 6144, 6656, 7168, 7552, 7808)


def _params(sem, vmem=VMEM_LIMIT_V7X):
    return pltpu.CompilerParams(dimension_semantics=sem, vmem_limit_bytes=vmem)


def _sigmoid(t):
    return 0.5 * jnp.tanh(0.5 * t) + 0.5


def _silu_and_grad(g):
    sg = _sigmoid(g)
    return g * sg, sg * (1.0 + g * (1.0 - sg))


def _rms(t, g, n=None):
    n = t.shape[-1] if n is None else n
    r = lax.rsqrt(jnp.sum(t * t, axis=-1, keepdims=True) * (1.0 / n) + EPS)
    return (t * r) * g


def _rms_parts(t, n=None):
    n = t.shape[-1] if n is None else n
    r = lax.rsqrt(jnp.sum(t * t, axis=-1, keepdims=True) * (1.0 / n) + EPS)
    return r, t * r


def _rms_bwd(dhat, hat, r, n):
    return r * (dhat - hat * (jnp.sum(dhat * hat, axis=-1, keepdims=True) * (1.0 / n)))


def _rope(t, c, sa, sb):
    return t * c + pltpu.roll(t, HP - 16, 1) * sa + pltpu.roll(t, 16, 1) * sb


def _rope_t(d, c, sa, sb):
    return d * c + pltpu.roll(d * sa, 16, 1) + pltpu.roll(d * sb, HP - 16, 1)


def _dot(a, b):
    return jnp.dot(a, b, preferred_element_type=F32)


def _dot_nt(a, b):
    return lax.dot_general(a, b, (((1,), (1,)), ((), ())), preferred_element_type=F32)


def _dot_tn(a, b):
    return lax.dot_general(a, b, (((0,), (0,)), ((), ())), preferred_element_type=F32)


def _colsum(t):
    return jnp.sum(t, axis=0, keepdims=True)


def _tile(n, t):
    t = min(n, t)
    assert n % t == 0, (n, t)
    return t


def _rope_tables(pos_b, invf):
    S = pos_b.shape[0]
    tm = _tile(S, 1024)

    def body(pos_ref, invf_ref, c_ref, sa_ref, sb_ref):
        ang = pos_ref[...].astype(F32) * invf_ref[...]
        lane = lax.broadcasted_iota(jnp.int32, ang.shape, 1)
        cs = jnp.cos(ang)
        sn = jnp.sin(ang)
        c_ref[...] = jnp.where(lane < NOPE_DIM, 1.0, jnp.where(lane < QK_DIM, cs, 0.0))
        sa_ref[...] = jnp.where((lane >= NOPE_DIM) & (lane < NOPE_DIM + 16), -sn, 0.0)
        sb_ref[...] = jnp.where((lane >= NOPE_DIM + 16) & (lane < QK_DIM), sn, 0.0)

    blk = pl.BlockSpec((tm, HP), lambda i: (i, 0))
    return pl.pallas_call(
        body, name="rope_tables", grid=(S // tm,),
        in_specs=[blk, pl.BlockSpec((1, HP), lambda i: (0, 0))],
        out_specs=[blk, blk, blk],
        out_shape=[jax.ShapeDtypeStruct((S, HP), F32)] * 3,
        compiler_params=_params(("parallel",)),
    )(pos_b, invf)


MESH = pl.DeviceIdType.MESH
HBM = pl.BlockSpec(memory_space=pltpu.HBM)


def _chip_peer(x, y, c, k):
    return (x ^ (k >> 1), y ^ (k & 1), c)


def _chip_copies(ins, outs, send, recv, loc, gather):
    x, y, c = lax.axis_index("x"), lax.axis_index("y"), lax.axis_index("c")
    me = 2 * x + y
    cps = []
    for a in range(len(ins)):
        cps.append(pltpu.make_async_copy(ins[a] if gather else ins[a].at[me], outs[a].at[me], loc.at[a]))
        for k in (1, 2, 3):
            cps.append(pltpu.make_async_remote_copy(
                src_ref=ins[a] if gather else ins[a].at[me ^ k], dst_ref=outs[a].at[me],
                send_sem=send.at[3 * a + k - 1], recv_sem=recv.at[3 * a + k - 1],
                device_id=_chip_peer(x, y, c, k), device_id_type=MESH))
    return cps


def _carried(body, n_in, n_out, carry, gather):
    n = len(carry)

    def wrapped(*refs):
        ins, cin = refs[:n_in], refs[n_in:n_in + n]
        outs, cout = refs[n_in + n:n_in + n + n_out], refs[n_in + n + n_out:n_in + 2 * n + n_out]
        send, recv, loc = refs[n_in + 2 * n + n_out:]
        i = pl.program_id(0)

        @pl.when(i == 0)
        def _():
            for cp in _chip_copies(cin, cout, send, recv, loc, gather):
                cp.start()

        body(*ins, *outs)

        @pl.when(i == pl.num_programs(0) - 1)
        def _():
            for cp in _chip_copies(cin, cout, send, recv, loc, gather):
                cp.wait()

    specs = dict(
        in_specs=[HBM] * n, out_specs=[HBM] * n,
        out_shape=[jax.ShapeDtypeStruct(((4,) + a.shape) if gather else a.shape, a.dtype) for a in carry],
        scratch_shapes=[pltpu.SemaphoreType.DMA((3 * n,)), pltpu.SemaphoreType.DMA((3 * n,)),
                        pltpu.SemaphoreType.DMA((n,))])
    return wrapped, specs


def _inproj(x, g, w, carry=()):
    S = x.shape[0]
    tm = _tile(S, 256)

    def body(x_ref, g_ref, w_ref, proj_ref, xn_ref):
        h = _rms(x_ref[...], g_ref[...]).astype(BF16)
        xn_ref[...] = h
        proj_ref[...] = _dot(h, w_ref[...]).astype(BF16)

    row = lambda n: pl.BlockSpec((tm, n), lambda i: (i, 0))
    in_specs = [row(D_MODEL), pl.BlockSpec((1, D_MODEL), lambda i: (0, 0)), pl.BlockSpec((D_MODEL, PW), lambda i: (0, 0))]
    out_specs = [row(PW), row(D_MODEL)]
    out_shape = [jax.ShapeDtypeStruct((S, PW), BF16), jax.ShapeDtypeStruct((S, D_MODEL), BF16)]
    if not carry:
        proj, xn = pl.pallas_call(
            body, name="inproj", grid=(S // tm,), in_specs=in_specs, out_specs=out_specs, out_shape=out_shape,
            compiler_params=_params(("parallel",)))(x, g, w)
        return proj, xn, []
    wrapped, extra = _carried(body, 3, 2, carry, gather=True)
    res = pl.pallas_call(
        wrapped, name="inproj_gather", grid=(S // tm,), in_specs=in_specs + extra["in_specs"],
        out_specs=out_specs + extra["out_specs"], out_shape=out_shape + extra["out_shape"],
        scratch_shapes=extra["scratch_shapes"], compiler_params=_params(("arbitrary",)))(x, g, w, *carry)
    return res[0], res[1], list(res[2:])


def _mla_prep(proj, tabs, qg, kvg, qhg, khg, wuq, wuk, wuv):
    S = proj.shape[0]
    tm = _tile(S, 512)

    def body(a_ref, c_ref, sa_ref, sb_ref, qg_ref, kvg_ref, qhg_ref, khg_ref, wuq_ref, wuk_ref, wuv_ref,
             q_ref, k_ref, v_ref, qn_ref, kvn_ref, bound_ref, qt_ref, vt_ref):
        gq = jnp.max(jnp.abs(qhg_ref[...]), axis=-1, keepdims=True)
        gk = jnp.max(jnp.abs(khg_ref[...]), axis=-1, keepdims=True)
        bound_ref[...] = jnp.broadcast_to(gq * gk * (QK_DIM ** 0.5 * 1.01) + 1e-6, bound_ref.shape)
        ql = a_ref[:, 0:Q_RANK].astype(F32)
        kvl = a_ref[:, Q_RANK:Q_RANK + KV_RANK].astype(F32)
        kpe = a_ref[:, Q_RANK + KV_RANK:Q_RANK + KV_RANK + HP].astype(F32)
        qn = _rms(ql, qg_ref[...]).astype(BF16)
        kvn = _rms(kvl, kvg_ref[...]).astype(BF16)
        qn_ref[...] = qn
        kvn_ref[...] = kvn
        qraw = _dot(qn, wuq_ref[...])
        kn = _dot(kvn, wuk_ref[...])
        vf = _dot(kvn, wuv_ref[...])
        v_ref[...] = vf.astype(BF16)
        c, sa, sb = c_ref[...], sa_ref[...], sb_ref[...]
        for h in range(N_HEADS):
            sl = slice(h * HP, (h + 1) * HP)
            tq = _rms(qraw[:, sl], qhg_ref[...], QK_DIM)
            qh = _rope(tq, c, sa, sb) * SCALE
            q_ref[:, sl] = qh.astype(BF16)
            qt_ref[sl, :] = qh.T.astype(BF16)
            vt_ref[sl, :] = vf[:, sl].T.astype(BF16)
            tk = _rms(kn[:, sl] + kpe, khg_ref[...], QK_DIM)
            k_ref[:, sl] = _rope(tk, c, sa, sb).astype(BF16)

    row = lambda w: pl.BlockSpec((tm, w), lambda i: (i, 0))
    full = lambda a: pl.BlockSpec(a.shape, lambda i: (0,) * a.ndim)
    return pl.pallas_call(
        body, name="mla_prep", grid=(S // tm,),
        in_specs=[pl.BlockSpec((tm, 1024), lambda i: (i, 7)), row(HP), row(HP), row(HP),
                  full(qg), full(kvg), full(qhg), full(khg), full(wuq), full(wuk), full(wuv)],
        out_specs=[row(1024), row(1024), row(1024), row(Q_RANK), row(KV_RANK),
                   pl.BlockSpec((1, HP), lambda i: (0, 0)),
                   pl.BlockSpec((1024, tm), lambda i: (0, i)), pl.BlockSpec((1024, tm), lambda i: (0, i))],
        out_shape=[jax.ShapeDtypeStruct((S, 1024), BF16)] * 3
        + [jax.ShapeDtypeStruct((S, Q_RANK), BF16), jax.ShapeDtypeStruct((S, KV_RANK), BF16),
           jax.ShapeDtypeStruct((1, HP), F32)] + [jax.ShapeDtypeStruct((1024, S), BF16)] * 2,
        compiler_params=_params(("arbitrary",)),
    )(proj, *tabs, qg, kvg, qhg, khg, wuq, wuk, wuv)


def _attn_fwd(qt, k, vt):
    S = k.shape[0]
    tq, tk = _tile(S, 2048), _tile(S, 512)
    nk = S // tk
    assert nk % 2 == 0

    def body(qt_ref, k_ref, vt_ref, o_ref, lse_ref, m_s, l_s, acc_s, s_buf, p_buf, cm_buf, a_buf):
        m_s[...] = jnp.full(m_s.shape, -jnp.inf, F32)
        l_s[...] = jnp.zeros(l_s.shape, F32)
        acc_s[...] = jnp.zeros(acc_s.shape, F32)
        p_buf[1] = jnp.zeros(p_buf.shape[1:], BF16)
        a_buf[1] = jnp.ones(a_buf.shape[1:], F32)
        qtv = qt_ref[...]

        def keys(c):
            return pl.ds(pl.multiple_of(c * tk, tk), tk)

        s0 = _dot(k_ref[keys(0), :], qtv)
        s_buf[0] = s0
        cm_buf[0] = jnp.max(s0, axis=0, keepdims=True)

        def stage(c, cur, nxt):
            sn = _dot(k_ref[keys(jnp.minimum(c + 1, nk - 1)), :], qtv)
            s_buf[nxt] = sn
            cm_buf[nxt] = jnp.max(sn, axis=0, keepdims=True)
            acc_s[...] = a_buf[nxt] * acc_s[...] + _dot(vt_ref[:, keys(jnp.maximum(c - 1, 0))], p_buf[nxt])
            m_prev = m_s[...]
            m_new = jnp.maximum(m_prev, cm_buf[cur])
            alpha = jnp.exp(m_prev - m_new)
            pt = jnp.exp(s_buf[cur] - m_new)
            l_s[...] = alpha * l_s[...] + jnp.sum(pt, axis=0, keepdims=True)
            p_buf[cur] = pt.astype(BF16)
            a_buf[cur] = alpha
            m_s[...] = m_new

        def step(c2, carry):
            stage(2 * c2, 0, 1)
            stage(2 * c2 + 1, 1, 0)
            return carry

        lax.fori_loop(0, nk // 2, step, 0)
        acc = a_buf[1] * acc_s[...] + _dot(vt_ref[:, keys(nk - 1)], p_buf[1])
        o_ref[...] = (acc / l_s[...]).T
        lse_row = m_s[...] + jnp.log(l_s[...])
        lse_ref[0] = jnp.broadcast_to(lse_row, (HP, tq)).T[:, 0:1]

    return pl.pallas_call(
        body, name="attn_fwd", grid=(N_HEADS, S // tq),
        in_specs=[pl.BlockSpec((HP, tq), lambda h, i: (h, i)),
                  pl.BlockSpec((S, HP), lambda h, i: (0, h)),
                  pl.BlockSpec((HP, S), lambda h, i: (h, 0))],
        out_specs=[pl.BlockSpec((tq, HP), lambda h, i: (i, h)),
                   pl.BlockSpec((1, tq, 1), lambda h, i: (h, i, 0))],
        out_shape=[jax.ShapeDtypeStruct((S, N_HEADS * HP), F32),
                   jax.ShapeDtypeStruct((N_HEADS, S, 1), F32)],
        scratch_shapes=[pltpu.VMEM((1, tq), F32), pltpu.VMEM((1, tq), F32), pltpu.VMEM((HP, tq), F32),
                        pltpu.VMEM((2, tk, tq), F32), pltpu.VMEM((2, tk, tq), BF16),
                        pltpu.VMEM((2, 1, tq), F32), pltpu.VMEM((2, 1, tq), F32)],
        compiler_params=_params(("parallel", "parallel")),
    )(qt, k, vt)


def _memkv(mem, mng, mkg, wmkv):
    M = mem.shape[0]

    def body(mem_ref, mng_ref, mkg_ref, w_ref, mk_ref, mv_ref):
        mn = _rms(mem_ref[...], mng_ref[...]).astype(BF16)
        mkv = _dot(mn, w_ref[...])
        for h in range(MEM_HEADS):
            kraw = mkv[:, 2 * MEM_HD * h:2 * MEM_HD * h + MEM_HD]
            mk_ref[:, MEM_HD * h:MEM_HD * (h + 1)] = _rms(kraw, mkg_ref[...]).astype(BF16)
            mv_ref[:, MEM_HD * h:MEM_HD * (h + 1)] = mkv[:, 2 * MEM_HD * h + MEM_HD:2 * MEM_HD * (h + 1)].astype(BF16)

    return pl.pallas_call(
        body, name="memkv",
        out_shape=[jax.ShapeDtypeStruct((M, MEM_W), BF16)] * 2,
        compiler_params=pltpu.CompilerParams(vmem_limit_bytes=VMEM_LIMIT_V7X),
    )(mem, mng, mkg, wmkv)


def _conv_shifts(cc, cu, hp_ref, hn_ref, i, n_tiles, tm):
    z = cc * cu
    last = PROJ_HALO - 1
    zp = hp_ref[last:last + 1, 0:CONV_W].astype(F32) * hp_ref[last:last + 1, CONV_W:2 * CONV_W].astype(F32)
    zn = hn_ref[0:1, 0:CONV_W].astype(F32) * hn_ref[0:1, CONV_W:2 * CONV_W].astype(F32)
    zp = jnp.where(i == 0, 0.0, zp)
    zn = jnp.where(i == n_tiles - 1, 0.0, zn)
    row = lax.broadcasted_iota(jnp.int32, z.shape, 0)
    z_up = jnp.where(row == 0, zp, pltpu.roll(z, 1, 0))
    z_dn = jnp.where(row == tm - 1, zn, pltpu.roll(z, tm - 1, 0))
    return z, z_up, z_dn


def _halo_specs(tm, S, width, col, rows):
    per = tm // rows
    prev = pl.BlockSpec((rows, width), lambda i: (jnp.maximum(i * per - 1, 0), col))
    nxt = pl.BlockSpec((rows, width), lambda i: (jnp.minimum((i + 1) * per, S // rows - 1), col))
    return prev, nxt


def _mem_attend(qm, mqg, mk_h, mv_h):
    r, qhat = _rms_parts(qm)
    mq = (qhat * mqg).astype(BF16)
    s = _dot_nt(mq, mk_h) * MEM_SCALE
    e = jnp.exp(s - jnp.max(s, axis=-1, keepdims=True))
    p = e / jnp.sum(e, axis=-1, keepdims=True)
    pv = _dot(p.astype(BF16), mv_h)
    return r, qhat, mq, p, pv


def _merge(proj, o, x, bg, convw, mqg, mk, mv, wba, wbc, wbm, wo):
    S = x.shape[0]
    tm = _tile(S, 512)
    nt = S // tm

    def body(main_ref, ccu_ref, hp_ref, hn_ref, o_ref, x_ref, bg_ref, cw_ref, mqg_ref, mk_ref, mv_ref,
             wba_ref, wbc_ref, wbm_ref, wo_ref, xn_ref, oa_ref, oc_ref, om_ref, u_ref, y_ref):
        i = pl.program_id(0)
        sil_a, _ = _silu_and_grad(main_ref[:, O_GA:O_GA + 1024].astype(F32))
        oa = (o_ref[...] * sil_a).astype(BF16)
        oa_ref[...] = oa
        z, z_up, z_dn = _conv_shifts(ccu_ref[:, 0:CONV_W].astype(F32), ccu_ref[:, CONV_W:].astype(F32), hp_ref, hn_ref, i, nt, tm)
        cv = cw_ref[0:1, :] * z_up + cw_ref[1:2, :] * z + cw_ref[2:3, :] * z_dn + cw_ref[3:4, :]
        sil_c, _ = _silu_and_grad(main_ref[:, O_GC:O_GC + CONV_W].astype(F32))
        oc = (main_ref[:, O_CB:O_CB + CONV_W].astype(F32) * cv * sil_c).astype(BF16)
        oc_ref[...] = oc
        sil_m, _ = _silu_and_grad(main_ref[:, O_GM:O_GM + MEM_W].astype(F32))
        for h in range(MEM_HEADS):
            sl = slice(h * MEM_HD, (h + 1) * MEM_HD)
            qm = main_ref[:, O_QM + h * MEM_HD:O_QM + (h + 1) * MEM_HD].astype(F32)
            pv = _mem_attend(qm, mqg_ref[...], mk_ref[:, sl], mv_ref[:, sl])[4]
            om_ref[:, sl] = (pv * sil_m[:, sl]).astype(BF16)
        ua = _dot(oa, wba_ref[...])
        uc = _dot(oc, wbc_ref[...])
        um = _dot(om_ref[...], wbm_ref[...])
        u_ref[:, 0:1024] = ua.astype(BF16)
        u_ref[:, 1024:2048] = uc.astype(BF16)
        u_ref[:, 2048:3072] = um.astype(BF16)
        rg = _sigmoid(main_ref[:, O_R:O_R + 3072].astype(F32) + bg_ref[...])
        y = (rg[:, 0:1024] * ua + rg[:, 1024:2048] * uc + rg[:, 2048:3072] * um).astype(BF16)
        y_ref[...] = y
        xn_ref[...] = x_ref[...] + _dot(y, wo_ref[...])

    row = lambda w: pl.BlockSpec((tm, w), lambda i: (i, 0))
    full = lambda a: pl.BlockSpec(a.shape, lambda i: (0,) * a.ndim)
    hp, hn = _halo_specs(tm, S, 1024, 6, PROJ_HALO)
    return pl.pallas_call(
        body, name="merge", grid=(nt,),
        in_specs=[row(6144), pl.BlockSpec((tm, 1024), lambda i: (i, 6)), hp, hn, row(1024), row(1024),
                  full(bg), full(convw), full(mqg), full(mk), full(mv), full(wba), full(wbc), full(wbm), full(wo)],
        out_specs=[row(1024), row(1024), row(CONV_W), row(MEM_W), row(3072), row(1024)],
        out_shape=[jax.ShapeDtypeStruct((S, 1024), F32), jax.ShapeDtypeStruct((S, 1024), BF16),
                   jax.ShapeDtypeStruct((S, CONV_W), BF16), jax.ShapeDtypeStruct((S, MEM_W), BF16),
                   jax.ShapeDtypeStruct((S, 3072), BF16), jax.ShapeDtypeStruct((S, 1024), BF16)],
        compiler_params=_params(("parallel",)),
    )(proj, proj, proj, proj, o, x, bg, convw, mqg, mk, mv, wba, wbc, wbm, wo)


def _loss_head(xf, tgt):
    S = xf.shape[0]
    tm = _tile(S, 1024)

    def body(x_ref, t_ref, g_ref, acc_ref):
        @pl.when(pl.program_id(0) == 0)
        def _():
            acc_ref[...] = jnp.zeros(acc_ref.shape, F32)

        e = x_ref[...] - t_ref[...]
        g_ref[...] = e * (1.0 / D_MODEL)
        part = jnp.sum((e * e).reshape(tm // 8, 8, D_MODEL), axis=0)
        tot = part[:, 0:128]
        for k in range(1, D_MODEL // 128):
            tot = tot + part[:, 128 * k:128 * (k + 1)]
        acc_ref[...] += tot

    row = pl.BlockSpec((tm, D_MODEL), lambda i: (i, 0))
    return pl.pallas_call(
        body, name="loss_head", grid=(S // tm,),
        in_specs=[row, row],
        out_specs=[row, pl.BlockSpec((8, 128), lambda i: (0, 0))],
        out_shape=[jax.ShapeDtypeStruct((S, D_MODEL), F32), jax.ShapeDtypeStruct((8, 128), F32)],
        compiler_params=_params(("arbitrary",)),
    )(xf, tgt)


def _merge_bwd(g, proj, o, u, bg, convw, mqg, mk, mv, wot, wbat, wbct, wbmt):
    S = g.shape[0]
    tm = _tile(S, 256)
    nt = S // tm
    M = mk.shape[0]

    def body(g_ref, main_ref, ccu_ref, hp_ref, hn_ref, o_ref, u_ref, bg_ref, cw_ref, mqg_ref, mk_ref, mv_ref,
             wot_ref, wbat_ref, wbct_ref, wbmt_ref,
             dmain_ref, dcv_ref, do_ref, delta_ref, du_ref, dbg_ref, dmk_ref, dmv_ref, dmqg_ref, dot_ref):
        i = pl.program_id(0)

        @pl.when(i == 0)
        def _():
            dbg_ref[...] = jnp.zeros(dbg_ref.shape, F32)
            dmk_ref[...] = jnp.zeros(dmk_ref.shape, F32)
            dmv_ref[...] = jnp.zeros(dmv_ref.shape, F32)
            dmqg_ref[...] = jnp.zeros(dmqg_ref.shape, F32)

        dy = _dot(g_ref[...].astype(BF16), wot_ref[...])
        d_branch = []
        for b, wt_ref in enumerate((wbat_ref, wbct_ref, wbmt_ref)):
            cols = slice(O_R + b * D_MODEL, O_R + (b + 1) * D_MODEL)
            rg = _sigmoid(main_ref[:, cols].astype(F32) + bg_ref[:, cols])
            dr = dy * u_ref[:, cols].astype(F32) * rg * (1.0 - rg)
            dmain_ref[:, cols] = dr.astype(BF16)
            dbg_ref[:, cols] += _colsum(dr)
            du = (dy * rg).astype(BF16)
            du_ref[:, cols] = du
            d_branch.append(_dot(du, wt_ref[...]))
        do_a, do_c, do_m = d_branch

        sil_a, dsil_a = _silu_and_grad(main_ref[:, O_GA:O_GA + 1024].astype(F32))
        ov = o_ref[...]
        d_o = do_a * sil_a
        do_ref[...] = d_o.astype(BF16)
        dot_ref[...] = d_o.T.astype(BF16)
        dmain_ref[:, O_GA:O_GA + 1024] = (do_a * ov * dsil_a).astype(BF16)
        prod = d_o * ov
        for h in range(N_HEADS):
            delta_ref[h] = jnp.sum(prod[:, h * HP:(h + 1) * HP], axis=-1, keepdims=True)

        z, z_up, z_dn = _conv_shifts(ccu_ref[:, 0:CONV_W].astype(F32), ccu_ref[:, CONV_W:].astype(F32), hp_ref, hn_ref, i, nt, tm)
        cv = cw_ref[0:1, :] * z_up + cw_ref[1:2, :] * z + cw_ref[2:3, :] * z_dn + cw_ref[3:4, :]
        sil_c, dsil_c = _silu_and_grad(main_ref[:, O_GC:O_GC + CONV_W].astype(F32))
        cb = main_ref[:, O_CB:O_CB + CONV_W].astype(F32)
        dmain_ref[:, O_CB:O_CB + CONV_W] = (do_c * cv * sil_c).astype(BF16)
        dmain_ref[:, O_GC:O_GC + CONV_W] = (do_c * cb * cv * dsil_c).astype(BF16)
        dcv_ref[...] = do_c * cb * sil_c

        sil_m, dsil_m = _silu_and_grad(main_ref[:, O_GM:O_GM + MEM_W].astype(F32))
        for h in range(MEM_HEADS):
            sl = slice(h * MEM_HD, (h + 1) * MEM_HD)
            qm = main_ref[:, O_QM + h * MEM_HD:O_QM + (h + 1) * MEM_HD].astype(F32)
            mk_h, mv_h = mk_ref[:, sl], mv_ref[:, sl]
            r, qhat, mq, p, pv = _mem_attend(qm, mqg_ref[...], mk_h, mv_h)
            dom = do_m[:, sl]
            dmain_ref[:, O_GM + h * MEM_HD:O_GM + (h + 1) * MEM_HD] = (dom * pv * dsil_m[:, sl]).astype(BF16)
            dpv = (dom * sil_m[:, sl]).astype(BF16)
            dp = _dot_nt(dpv, mv_h)
            ds = (p * (dp - jnp.sum(dp * p, axis=-1, keepdims=True)) * MEM_SCALE).astype(BF16)
            dmq = _dot(ds, mk_h)
            dmk_ref[:, sl] += _dot_tn(ds, mq)
            dmv_ref[:, sl] += _dot_tn(p.astype(BF16), dpv)
            dmqg_ref[...] += _colsum(dmq * qhat)
            dqm = _rms_bwd(dmq * mqg_ref[...], qhat, r, MEM_HD)
            dmain_ref[:, O_QM + h * MEM_HD:O_QM + (h + 1) * MEM_HD] = dqm.astype(BF16)

    row = lambda w: pl.BlockSpec((tm, w), lambda i: (i, 0))
    full = lambda a: pl.BlockSpec(a.shape, lambda i: (0,) * a.ndim)
    acc = lambda r, c: pl.BlockSpec((r, c), lambda i: (0, 0))
    hp, hn = _halo_specs(tm, S, 1024, 6, PROJ_HALO)
    return pl.pallas_call(
        body, name="merge_bwd", grid=(nt,),
        in_specs=[row(1024), row(6144), pl.BlockSpec((tm, 1024), lambda i: (i, 6)), hp, hn, row(1024), row(3072),
                  full(bg), full(convw), full(mqg), full(mk), full(mv), full(wot), full(wbat), full(wbct), full(wbmt)],
        out_specs=[row(6144), row(CONV_W), row(1024), pl.BlockSpec((N_HEADS, tm, 1), lambda i: (0, i, 0)), row(3072),
                   acc(1, 3072), acc(M, MEM_W), acc(M, MEM_W), acc(1, MEM_HD),
                   pl.BlockSpec((1024, tm), lambda i: (0, i))],
        out_shape=[jax.ShapeDtypeStruct((S, 6144), BF16), jax.ShapeDtypeStruct((S, CONV_W), F32),
                   jax.ShapeDtypeStruct((S, 1024), BF16), jax.ShapeDtypeStruct((N_HEADS, S, 1), F32),
                   jax.ShapeDtypeStruct((S, 3072), BF16), jax.ShapeDtypeStruct((1, 3072), F32),
                   jax.ShapeDtypeStruct((M, MEM_W), F32), jax.ShapeDtypeStruct((M, MEM_W), F32),
                   jax.ShapeDtypeStruct((1, MEM_HD), F32), jax.ShapeDtypeStruct((1024, S), BF16)],
        compiler_params=_params(("arbitrary",)),
    )(g, proj, proj, proj, proj, o, u, bg, convw, mqg, mk, mv, wot, wbat, wbct, wbmt)


def _conv_bwd(dcv, proj, convw):
    S = dcv.shape[0]
    tm = _tile(S, 512)
    nt = S // tm

    def body(d_ref, dp_ref, dn_ref, ccu_ref, hp_ref, hn_ref, cw_ref, dccu_ref, dcw_ref):
        i = pl.program_id(0)

        @pl.when(i == 0)
        def _():
            dcw_ref[...] = jnp.zeros(dcw_ref.shape, F32)

        cc, cu = ccu_ref[:, 0:CONV_W].astype(F32), ccu_ref[:, CONV_W:].astype(F32)
        z, z_up, z_dn = _conv_shifts(cc, cu, hp_ref, hn_ref, i, nt, tm)
        d = d_ref[...]
        dprev = jnp.where(i == 0, 0.0, dp_ref[7:8, :])
        dnext = jnp.where(i == nt - 1, 0.0, dn_ref[0:1, :])
        row = lax.broadcasted_iota(jnp.int32, d.shape, 0)
        d_up = jnp.where(row == 0, dprev, pltpu.roll(d, 1, 0))
        d_dn = jnp.where(row == tm - 1, dnext, pltpu.roll(d, tm - 1, 0))
        dz = cw_ref[0:1, :] * d_dn + cw_ref[1:2, :] * d + cw_ref[2:3, :] * d_up
        dccu_ref[:, 0:CONV_W] = (dz * cu).astype(BF16)
        dccu_ref[:, CONV_W:] = (dz * cc).astype(BF16)
        dcw_ref[0:1, :] += _colsum(d * z_up)
        dcw_ref[1:2, :] += _colsum(d * z)
        dcw_ref[2:3, :] += _colsum(d * z_dn)
        dcw_ref[3:4, :] += _colsum(d)

    hp, hn = _halo_specs(tm, S, 1024, 6, PROJ_HALO)
    dp, dn = _halo_specs(tm, S, CONV_W, 0, F32_HALO)
    return pl.pallas_call(
        body, name="conv_bwd", grid=(nt,),
        in_specs=[pl.BlockSpec((tm, CONV_W), lambda i: (i, 0)), dp, dn,
                  pl.BlockSpec((tm, 1024), lambda i: (i, 6)), hp, hn,
                  pl.BlockSpec((8, CONV_W), lambda i: (0, 0))],
        out_specs=[pl.BlockSpec((tm, 1024), lambda i: (i, 0)), pl.BlockSpec((8, CONV_W), lambda i: (0, 0))],
        out_shape=[jax.ShapeDtypeStruct((S, 1024), BF16), jax.ShapeDtypeStruct((8, CONV_W), F32)],
        compiler_params=_params(("arbitrary",)),
    )(dcv, dcv, dcv, proj, proj, proj, convw)


def _attn_bwd(q, k, v, do, qt, dot, lse, delta):
    S = q.shape[0]
    tq, tk = _tile(S, 512), _tile(S, 4096)
    ni, nk = S // tq, S // tk

    def body(q_ref, do_ref, qt_ref, dot_ref, k_ref, v_ref, lse_ref, delta_ref, dq_ref, dkt_hbm, dvt_hbm,
             dq_s, dkt_s, dvt_s, sem):
        h, i = pl.program_id(0), pl.program_id(1)

        @pl.when(i == 0)
        def _():
            dkt_s[...] = jnp.zeros(dkt_s.shape, F32)
            dvt_s[...] = jnp.zeros(dvt_s.shape, F32)

        dq_s[...] = jnp.zeros(dq_s.shape, F32)
        qv, dov, qtv, dotv = q_ref[...], do_ref[...], qt_ref[...], dot_ref[...]
        lse_c, delta_c = lse_ref[0], delta_ref[0]

        def step(c, carry):
            cols = pl.ds(pl.multiple_of(c * tk, tk), tk)
            kc, vc = k_ref[cols, :], v_ref[cols, :]
            p = jnp.exp(_dot_nt(qv, kc) - lse_c)
            dp = _dot_nt(dov, vc)
            ds = (p * (dp - delta_c)).astype(BF16)
            dq_s[...] += _dot(ds, kc)
            dvt_s[:, cols] += _dot(dotv, p.astype(BF16))
            dkt_s[:, cols] += _dot(qtv, ds)
            return carry

        lax.fori_loop(0, nk, step, 0)
        dq_ref[...] = dq_s[...]

        @pl.when(i == ni - 1)
        def _():
            head = pl.ds(pl.multiple_of(h * HP, HP), HP)
            out_k = pltpu.make_async_copy(dkt_s, dkt_hbm.at[head, :], sem.at[0])
            out_v = pltpu.make_async_copy(dvt_s, dvt_hbm.at[head, :], sem.at[1])
            out_k.start()
            out_v.start()
            out_k.wait()
            out_v.wait()

    col = pl.BlockSpec((1, tq, 1), lambda h, i: (h, i, 0))
    blk = pl.BlockSpec((tq, HP), lambda h, i: (i, h))
    blkt = pl.BlockSpec((HP, tq), lambda h, i: (h, i))
    res = pl.BlockSpec((S, HP), lambda h, i: (0, h))
    whole = pl.BlockSpec(memory_space=pl.ANY)
    return pl.pallas_call(
        body, name="attn_bwd", grid=(N_HEADS, ni),
        in_specs=[blk, blk, blkt, blkt, res, res, col, col],
        out_specs=[blk, whole, whole],
        out_shape=[jax.ShapeDtypeStruct((S, N_HEADS * HP), F32), jax.ShapeDtypeStruct((N_HEADS * HP, S), F32),
                   jax.ShapeDtypeStruct((N_HEADS * HP, S), F32)],
        scratch_shapes=[pltpu.VMEM((tq, HP), F32), pltpu.VMEM((HP, S), F32), pltpu.VMEM((HP, S), F32),
                        pltpu.SemaphoreType.DMA((2,))],
        compiler_params=_params(("parallel", "arbitrary")),
    )(q, do, qt, dot, k, v, lse, delta)


def _mla_prep_bwd(proj, tabs, dq, dk, dv, qg, kvg, qhg, khg, wuq, wuk, wuqt, wukt, wuvt):
    S = proj.shape[0]
    tm = _tile(S, 512)

    def body(a_ref, c_ref, sa_ref, sb_ref, dq_ref, dk_ref, dv_ref, qg_ref, kvg_ref, qhg_ref, khg_ref,
             wuq_ref, wuk_ref, wuqt_ref, wukt_ref, wuvt_ref,
             da_ref, dqraw_ref, dkraw_ref, dqg_ref, dkvg_ref, dqhg_ref, dkhg_ref):
        @pl.when(pl.program_id(0) == 0)
        def _():
            dqg_ref[...] = jnp.zeros(dqg_ref.shape, F32)
            dkvg_ref[...] = jnp.zeros(dkvg_ref.shape, F32)
            dqhg_ref[...] = jnp.zeros(dqhg_ref.shape, F32)
            dkhg_ref[...] = jnp.zeros(dkhg_ref.shape, F32)

        ql = a_ref[:, 0:Q_RANK].astype(F32)
        kvl = a_ref[:, Q_RANK:Q_RANK + KV_RANK].astype(F32)
        kpe = a_ref[:, Q_RANK + KV_RANK:Q_RANK + KV_RANK + HP].astype(F32)
        rq, qhat = _rms_parts(ql)
        rkv, kvhat = _rms_parts(kvl)
        qraw = _dot((qhat * qg_ref[...]).astype(BF16), wuq_ref[...])
        kn = _dot((kvhat * kvg_ref[...]).astype(BF16), wuk_ref[...])
        c, sa, sb = c_ref[...], sa_ref[...], sb_ref[...]
        dkpe = jnp.zeros(kpe.shape, F32)
        dqhg = jnp.zeros((1, HP), F32)
        dkhg = jnp.zeros((1, HP), F32)
        for h in range(N_HEADS):
            sl = slice(h * HP, (h + 1) * HP)
            r, that = _rms_parts(qraw[:, sl], QK_DIM)
            dtn = _rope_t(dq_ref[:, sl], c, sa, sb) * SCALE
            dqhg = dqhg + _colsum(dtn * that)
            dqraw_ref[:, sl] = _rms_bwd(dtn * qhg_ref[...], that, r, QK_DIM).astype(BF16)
            r, that = _rms_parts(kn[:, sl] + kpe, QK_DIM)
            dtn = _rope_t(dk_ref[:, sl], c, sa, sb)
            dkhg = dkhg + _colsum(dtn * that)
            dkr = _rms_bwd(dtn * khg_ref[...], that, r, QK_DIM)
            dkraw_ref[:, sl] = dkr.astype(BF16)
            dkpe = dkpe + dkr
        dqhg_ref[...] += dqhg
        dkhg_ref[...] += dkhg
        dqn = _dot(dqraw_ref[...], wuqt_ref[...])
        dqg_ref[...] += _colsum(dqn * qhat)
        da_ref[:, 0:Q_RANK] = _rms_bwd(dqn * qg_ref[...], qhat, rq, Q_RANK).astype(BF16)
        dkvn = _dot(dkraw_ref[...], wukt_ref[...]) + _dot(dv_ref[...], wuvt_ref[...])
        dkvg_ref[...] += _colsum(dkvn * kvhat)
        da_ref[:, Q_RANK:Q_RANK + KV_RANK] = _rms_bwd(dkvn * kvg_ref[...], kvhat, rkv, KV_RANK).astype(BF16)
        da_ref[:, Q_RANK + KV_RANK:Q_RANK + KV_RANK + HP] = dkpe.astype(BF16)
        da_ref[:, Q_RANK + KV_RANK + HP:] = jnp.zeros((tm, 1024 - Q_RANK - KV_RANK - HP), BF16)

    row = lambda w: pl.BlockSpec((tm, w), lambda i: (i, 0))
    full = lambda a: pl.BlockSpec(a.shape, lambda i: (0,) * a.ndim)
    acc = lambda c: pl.BlockSpec((1, c), lambda i: (0, 0))
    return pl.pallas_call(
        body, name="mla_prep_bwd", grid=(S // tm,),
        in_specs=[pl.BlockSpec((tm, 1024), lambda i: (i, 7)), row(HP), row(HP), row(HP),
                  row(1024), row(1024), row(1024), full(qg), full(kvg), full(qhg), full(khg),
                  full(wuq), full(wuk), full(wuqt), full(wukt), full(wuvt)],
        out_specs=[row(1024), row(1024), row(1024), acc(Q_RANK), acc(KV_RANK), acc(HP), acc(HP)],
        out_shape=[jax.ShapeDtypeStruct((S, 1024), BF16)] * 3
        + [jax.ShapeDtypeStruct((1, Q_RANK), F32), jax.ShapeDtypeStruct((1, KV_RANK), F32),
           jax.ShapeDtypeStruct((1, HP), F32), jax.ShapeDtypeStruct((1, HP), F32)],
        compiler_params=_params(("arbitrary",)),
    )(proj, *tabs, dq, dk, dv, qg, kvg, qhg, khg, wuq, wuk, wuqt, wukt, wuvt)


def _inproj_bwd(dmain, dccu, dsega, wint, x, g, ng, carry=()):
    S = x.shape[0]
    tm = _tile(S, 256)
    nm, nc = dmain.shape[1], dccu.shape[1]
    assert nm + nc + dsega.shape[1] == PW

    def body(dm_ref, dc_ref, da_ref, w_ref, x_ref, g_ref, ng_ref, dx_ref, dng_ref):
        @pl.when(pl.program_id(0) == 0)
        def _():
            dng_ref[...] = jnp.zeros(dng_ref.shape, F32)

        dh = (_dot(dm_ref[...], w_ref[0:nm, :]) + _dot(dc_ref[...], w_ref[nm:nm + nc, :])
              + _dot(da_ref[...], w_ref[nm + nc:PW, :]))
        r, xhat = _rms_parts(x_ref[...])
        dng_ref[...] += _colsum(dh * xhat)
        dx_ref[...] = g_ref[...] + _rms_bwd(dh * ng_ref[...], xhat, r, D_MODEL)

    row = lambda w: pl.BlockSpec((tm, w), lambda i: (i, 0))
    in_specs = [row(nm), row(nc), row(dsega.shape[1]), pl.BlockSpec((PW, D_MODEL), lambda i: (0, 0)),
                row(D_MODEL), row(D_MODEL), pl.BlockSpec((1, D_MODEL), lambda i: (0, 0))]
    out_specs = [row(D_MODEL), pl.BlockSpec((1, D_MODEL), lambda i: (0, 0))]
    out_shape = [jax.ShapeDtypeStruct((S, D_MODEL), F32), jax.ShapeDtypeStruct((1, D_MODEL), F32)]
    args = (dmain, dccu, dsega, wint, x, g, ng)
    if not carry:
        dx, dng = pl.pallas_call(
            body, name="inproj_bwd", grid=(S // tm,), in_specs=in_specs, out_specs=out_specs, out_shape=out_shape,
            compiler_params=_params(("arbitrary",)))(*args)
        return dx, dng, []
    wrapped, extra = _carried(body, 7, 2, carry, gather=False)
    res = pl.pallas_call(
        wrapped, name="inproj_bwd_scatter", grid=(S // tm,), in_specs=in_specs + extra["in_specs"],
        out_specs=out_specs + extra["out_specs"], out_shape=out_shape + extra["out_shape"],
        scratch_shapes=extra["scratch_shapes"], compiler_params=_params(("arbitrary",)))(*args, *carry)
    return res[0], res[1], list(res[2:])


def _memkv_bwd(mem, mng, mkg, wmkv, wmkvt, dmk, dmv):
    M = mem.shape[0]

    def body(mem_ref, mng_ref, mkg_ref, w_ref, wt_ref, dmk_ref, dmv_ref, dw_ref, dmng_ref, dmkg_ref, d_s):
        r, mhat = _rms_parts(mem_ref[...])
        mn = (mhat * mng_ref[...]).astype(BF16)
        mkv = _dot(mn, w_ref[...])
        dmkg = jnp.zeros((1, MEM_HD), F32)
        for h in range(MEM_HEADS):
            sl = slice(h * MEM_HD, (h + 1) * MEM_HD)
            rk, khat = _rms_parts(mkv[:, 2 * MEM_HD * h:2 * MEM_HD * h + MEM_HD])
            dkn = dmk_ref[:, sl]
            dmkg = dmkg + _colsum(dkn * khat)
            d_s[:, 2 * MEM_HD * h:2 * MEM_HD * h + MEM_HD] = _rms_bwd(dkn * mkg_ref[...], khat, rk, MEM_HD).astype(BF16)
            d_s[:, 2 * MEM_HD * h + MEM_HD:2 * MEM_HD * (h + 1)] = dmv_ref[:, sl].astype(BF16)
        dmkg_ref[...] = dmkg
        dw_ref[...] = _dot_tn(mn, d_s[...])
        dmn = _dot(d_s[...], wt_ref[...])
        dmng_ref[...] = _colsum(dmn * mhat)

    return pl.pallas_call(
        body, name="memkv_bwd",
        out_shape=[jax.ShapeDtypeStruct((D_MODEL, 2 * MEM_W), F32), jax.ShapeDtypeStruct((1, D_MODEL), F32),
                   jax.ShapeDtypeStruct((1, MEM_HD), F32)],
        scratch_shapes=[pltpu.VMEM((M, 2 * MEM_W), BF16)],
        compiler_params=pltpu.CompilerParams(vmem_limit_bytes=VMEM_LIMIT_V7X),
    )(mem, mng, mkg, wmkv, wmkvt, dmk, dmv)


def _mm_tn(a, b, name, col0=0, ncols=None):
    S, M = a.shape
    N = b.shape[1] if ncols is None else ncols
    tm, tn, ts = _tile(M, 1024), _tile(N, 1024), _tile(S, 2048)
    assert col0 % tn == 0
    jb = col0 // tn

    def body(a_ref, b_ref, o_ref):
        @pl.when(pl.program_id(2) == 0)
        def _():
            o_ref[...] = jnp.zeros(o_ref.shape, F32)

        o_ref[...] += _dot_tn(a_ref[...].astype(BF16), b_ref[...].astype(BF16))

    return pl.pallas_call(
        body, name=name, grid=(M // tm, N // tn, S // ts),
        in_specs=[pl.BlockSpec((ts, tm), lambda i, j, k: (k, i)),
                  pl.BlockSpec((ts, tn), lambda i, j, k: (k, j + jb))],
        out_specs=pl.BlockSpec((tm, tn), lambda i, j, k: (i, j)),
        out_shape=jax.ShapeDtypeStruct((M, N), F32),
        compiler_params=_params(("parallel", "parallel", "arbitrary")),
    )(a, b)


def _adamw(w, g0, g1, m, v, name):
    R, C = w.shape
    tr = R
    for cand in (512, 256, 128, 64, 32, 16, 8):
        if R % cand == 0 and cand * C * 4 <= (1 << 20):
            tr = cand
            break
    c1 = 1.0 / (1.0 - ADAM_B1 ** ADAM_STEP)
    c2 = 1.0 / (1.0 - ADAM_B2 ** ADAM_STEP)

    def body(w_ref, g0_ref, g1_ref, m_ref, v_ref, g_ref, d_ref, nm_ref, nv_ref):
        g = g0_ref[...] + g1_ref[...]
        nm = ADAM_B1 * m_ref[...] + (1.0 - ADAM_B1) * g
        nv = ADAM_B2 * v_ref[...] + (1.0 - ADAM_B2) * (g * g)
        g_ref[...] = g
        nm_ref[...] = nm
        nv_ref[...] = nv
        d_ref[...] = -ADAM_LR * ((nm * c1) / (jnp.sqrt(nv * c2) + ADAM_EPS) + ADAM_WD * w_ref[...])

    blk = pl.BlockSpec((tr, C), lambda i: (i, 0))
    return pl.pallas_call(
        body, name=name, grid=(R // tr,),
        in_specs=[blk] * 5, out_specs=[blk] * 4,
        out_shape=[jax.ShapeDtypeStruct((R, C), F32)] * 4,
        compiler_params=_params(("parallel",)),
    )(w, g0, g1, m, v)


def _sum_slabs(a, name):
    K, R, C = a.shape
    tr = R
    for cand in (512, 256, 128, 64, 32, 16, 8):
        if R % cand == 0 and cand * C * 4 * K <= (4 << 20):
            tr = cand
            break

    def body(a_ref, o_ref):
        t = a_ref[0].astype(F32)
        for k in range(1, K):
            t = t + a_ref[k].astype(F32)
        o_ref[...] = t

    return pl.pallas_call(
        body, name=name, grid=(R // tr,),
        in_specs=[pl.BlockSpec((K, tr, C), lambda i: (0, i, 0))],
        out_specs=pl.BlockSpec((tr, C), lambda i: (i, 0)),
        out_shape=jax.ShapeDtypeStruct((R, C), F32),
        compiler_params=_params(("parallel",)),
    )(a)


def _gather_chips(arrs):
    n = len(arrs)
    halves = [a.shape[0] // 2 for a in arrs]
    assert all(a.shape[0] == 2 * hf for a, hf in zip(arrs, halves))

    def body(*refs):
        ins, outs = refs[:n], refs[n:2 * n]
        send1, recv1, send2, recv2, loc = refs[2 * n:]
        x, y, c = lax.axis_index("x"), lax.axis_index("y"), lax.axis_index("c")
        me = 2 * x + y
        mine = [pl.ds(c * hf, hf) for hf in halves]
        theirs = [pl.ds((1 - c) * hf, hf) for hf in halves]
        sibling = (x, y, 1 - c)
        waits = []
        for a in range(n):
            own = pltpu.make_async_copy(ins[a], outs[a].at[me], loc.at[a])
            own.start()
            waits.append(own.wait)

        def over_ici(a, k, src_chip):
            return pltpu.make_async_remote_copy(
                src_ref=ins[a].at[mine[a]], dst_ref=outs[a].at[src_chip, mine[a]], send_sem=send1.at[3 * a + k - 1],
                recv_sem=recv1.at[3 * a + k - 1], device_id=_chip_peer(x, y, c, k), device_id_type=MESH)

        def to_sibling(a, k, layers):
            block = outs[a].at[me ^ k, layers]
            return pltpu.make_async_remote_copy(
                src_ref=block, dst_ref=block, send_sem=send2.at[3 * a + k - 1], recv_sem=recv2.at[3 * a + k - 1],
                device_id=sibling, device_id_type=MESH)

        for a in range(n):
            for k in (1, 2, 3):
                cp = over_ici(a, k, me)
                cp.start()
                waits.append(cp.wait_send)
        for a in range(n):
            for k in (1, 2, 3):
                over_ici(a, k, me ^ k).wait_recv()
                cp = to_sibling(a, k, mine[a])
                cp.start()
                waits.append(cp.wait_send)
        for a in range(n):
            for k in (1, 2, 3):
                to_sibling(a, k, theirs[a]).wait_recv()
        for w in waits:
            w()

    return pl.pallas_call(
        body, name="gather_weights",
        in_specs=[HBM] * n, out_specs=[HBM] * n,
        out_shape=[jax.ShapeDtypeStruct((4,) + a.shape, a.dtype) for a in arrs],
        scratch_shapes=[pltpu.SemaphoreType.DMA((3 * n,)), pltpu.SemaphoreType.DMA((3 * n,)),
                        pltpu.SemaphoreType.DMA((3 * n,)), pltpu.SemaphoreType.DMA((3 * n,)),
                        pltpu.SemaphoreType.DMA((n,))],
    )(*arrs)


def _scatter_chips(arrs, small):
    n = len(arrs)

    def body(*refs):
        ins, small_in = refs[:n], refs[n]
        outs, small_out = refs[n + 1:2 * n + 1], refs[2 * n + 1]
        send, recv, loc, ssend, srecv = refs[2 * n + 2:]
        x, y, c = lax.axis_index("x"), lax.axis_index("y"), lax.axis_index("c")
        me = 2 * x + y
        me8 = 4 * x + 2 * y + c
        copies = []
        for a in range(n):
            own = pltpu.make_async_copy(ins[a].at[me], outs[a].at[me], loc.at[a])
            own.start()
            copies.append(own)
        own = pltpu.make_async_copy(small_in, small_out.at[me8], loc.at[n])
        own.start()
        copies.append(own)
        for k in range(1, 8):
            cp = pltpu.make_async_remote_copy(
                src_ref=small_in, dst_ref=small_out.at[me8], send_sem=ssend.at[k - 1], recv_sem=srecv.at[k - 1],
                device_id=(x ^ (k >> 2), y ^ ((k >> 1) & 1), c ^ (k & 1)), device_id_type=MESH)
            cp.start()
            copies.append(cp)
        for a in range(n):
            for k in (1, 2, 3):
                cp = pltpu.make_async_remote_copy(
                    src_ref=ins[a].at[me ^ k], dst_ref=outs[a].at[me], send_sem=send.at[3 * a + k - 1],
                    recv_sem=recv.at[3 * a + k - 1], device_id=_chip_peer(x, y, c, k), device_id_type=MESH)
                cp.start()
                copies.append(cp)
        for cp in copies:
            cp.wait()

    return pl.pallas_call(
        body, name="scatter_grads",
        in_specs=[HBM] * (n + 1), out_specs=[HBM] * (n + 1),
        out_shape=[jax.ShapeDtypeStruct(a.shape, a.dtype) for a in arrs]
        + [jax.ShapeDtypeStruct((8,) + small.shape, small.dtype)],
        scratch_shapes=[pltpu.SemaphoreType.DMA((3 * n,)), pltpu.SemaphoreType.DMA((3 * n,)),
                        pltpu.SemaphoreType.DMA((n + 1,)), pltpu.SemaphoreType.DMA((7,)),
                        pltpu.SemaphoreType.DMA((7,))],
    )(*arrs, small)


def _swap_cores(arrs):
    n = len(arrs)

    def body(*refs):
        ins, outs = refs[:n], refs[n:2 * n]
        send, recv = refs[2 * n:]
        x, y, c = lax.axis_index("x"), lax.axis_index("y"), lax.axis_index("c")
        copies = []
        for a in range(n):
            cp = pltpu.make_async_remote_copy(
                src_ref=ins[a], dst_ref=outs[a], send_sem=send.at[a], recv_sem=recv.at[a],
                device_id=(x, y, 1 - c), device_id_type=MESH)
            cp.start()
            copies.append(cp)
        for cp in copies:
            cp.wait()

    return pl.pallas_call(
        body, name="swap_cores",
        in_specs=[HBM] * n, out_specs=[HBM] * n,
        out_shape=[jax.ShapeDtypeStruct(a.shape, a.dtype) for a in arrs],
        scratch_shapes=[pltpu.SemaphoreType.DMA((n,)), pltpu.SemaphoreType.DMA((n,))],
    )(*arrs)


def _pad_last(a, n):
    return jnp.pad(a, [(0, 0)] * (a.ndim - 1) + [(0, n - a.shape[-1])])


def _pad_w_in(w):
    lead = w.shape[:-1]
    seg = lambda a, b: w[..., a:b]
    ga = _pad_last(seg(2720, 3232).reshape(lead + (N_HEADS, V_DIM)), HP).reshape(lead + (1024,))
    kpe = jnp.pad(seg(640, 672), [(0, 0)] * len(lead) + [(NOPE_DIM, HP - QK_DIM)])
    zero = jnp.zeros(lead + (PW - 7936,), w.dtype)
    return jnp.concatenate(
        [seg(4256, 7328), ga, seg(672, 1184), seg(2208, 2720), seg(3232, 3744), seg(3744, 4256),
         seg(1184, 1696), seg(1696, 2208), seg(0, 384), seg(384, 640), kpe, zero], axis=-1)


def _unpad_w_in(w):
    lead = w.shape[:-1]
    seg = lambda a, n: w[..., a:a + n]
    ga = seg(O_GA, 1024).reshape(lead + (N_HEADS, HP))[..., :V_DIM].reshape(lead + (N_HEADS * V_DIM,))
    return jnp.concatenate(
        [seg(O_QL, 384), seg(O_KVL, 256), seg(O_KPE + NOPE_DIM, ROPE_DIM), seg(O_CB, 512), seg(O_CC, 512),
         seg(O_CU, 512), seg(O_QM, 512), ga, seg(O_GC, 512), seg(O_GM, 512), seg(O_R, 3072)], axis=-1)


def _cols_from_shards(g):
    _, L, R, C = g.shape
    return jnp.transpose(g, (1, 2, 0, 3)).reshape(L, R, 4 * C)


def _t(w):
    return jnp.swapaxes(w, -1, -2)


def _layer_fwd(x, mem, tabs, p, next_shards=()):
    proj, xn, gathered = _inproj(x, p["norm_g"], p["w_in"], next_shards)
    q, k, v, qn, kvn, score_bound, qt, vt = _mla_prep(proj, tabs, p["q_norm_g"], p["kv_norm_g"], p["q_head_g"], p["k_head_g"],
                                        p["w_uq"], p["w_uk"], p["w_uv"])
    o, lse = _attn_fwd(qt, k, vt)
    mk, mv = _memkv(mem, p["mem_norm_g"], p["mem_k_g"], p["w_mkv"])
    x_new, oa, oc, om, u, y = _merge(proj, o, x, p["b_gate"], p["conv_wb"], p["mem_q_g"], mk, mv,
                                     p["w_br_attn"], p["w_br_conv"], p["w_br_mem"], p["w_out"])
    saved = dict(x=x, proj=proj, xn=xn, q=q, qt=qt, k=k, v=v, qn=qn, kvn=kvn, o=o, lse=lse, mk=mk, mv=mv,
                 oa=oa, oc=oc, om=om, u=u, y=y)
    return x_new, saved, gathered


def _layer_bwd(g, mem, tabs, p, s, to_owner=()):
    S = g.shape[0]
    dmain, dcv, d_o, delta, du, dbg, dmk, dmv, dmqg, d_ot = _merge_bwd(
        g, s["proj"], s["o"], s["u"], p["b_gate"], p["conv_wb"], p["mem_q_g"], s["mk"], s["mv"],
        p["w_out_t"], p["w_br_attn_t"], p["w_br_conv_t"], p["w_br_mem_t"])
    dccu, dconv = _conv_bwd(dcv, s["proj"], p["conv_wb"])
    dq, dkt, dvt = _attn_bwd(s["q"], s["k"], s["v"], d_o, s["qt"], d_ot, s["lse"], delta)
    dk, dv = _t(dkt), _t(dvt).astype(BF16)
    dsega, dqraw, dkraw, dqg, dkvg, dqhg, dkhg = _mla_prep_bwd(
        s["proj"], tabs, dq, dk, dv, p["q_norm_g"], p["kv_norm_g"], p["q_head_g"], p["k_head_g"],
        p["w_uq"], p["w_uk"], p["w_uq_t"], p["w_uk_t"], p["w_uv_t"])
    dx, dng, received = _inproj_bwd(dmain, dccu, dsega, p["w_in_t"], s["x"], g, p["norm_g"], to_owner)
    dwmkv, dmng, dmkg = _memkv_bwd(mem, p["mem_norm_g"], p["mem_k_g"], p["w_mkv"], p["w_mkv_t"], dmk, dmv)
    grads = dict(
        norm_g=dng, b_gate=dbg, q_norm_g=dqg, kv_norm_g=dkvg, q_head_g=dqhg, k_head_g=dkhg,
        conv_wb=dconv, mem_norm_g=dmng, mem_q_g=dmqg, mem_k_g=dmkg, w_mkv=dwmkv,
        w_in=jnp.concatenate([_mm_tn(s["xn"], dmain, "grad_w_in"), _mm_tn(s["xn"], dccu, "grad_w_in_conv"),
                              _mm_tn(s["xn"], dsega, "grad_w_in_lat")], axis=1),
        w_uq=_mm_tn(s["qn"], dqraw, "grad_w_uq"),
        w_uk=_mm_tn(s["kvn"], dkraw, "grad_w_uk"),
        w_uv=_mm_tn(s["kvn"], dv, "grad_w_uv"),
        w_br_attn=_mm_tn(s["oa"], du, "grad_w_br_attn", 0, 1024),
        w_br_conv=_mm_tn(s["oc"], du, "grad_w_br_conv", 1024, 1024),
        w_br_mem=_mm_tn(s["om"], du, "grad_w_br_mem", 2048, 1024),
        w_out=_mm_tn(s["y"], g, "grad_w_out"),
    )
    return dx, grads, received


def _layer_params(big, full, l):
    p = {}
    w_in = _pad_w_in(big["w_in"])
    w_uq = _pad_last(big["w_uq"].reshape(Q_RANK, N_HEADS, QK_DIM), HP).reshape(Q_RANK, 1024)
    ukv = big["w_ukv"].reshape(KV_RANK, N_HEADS, NOPE_DIM + V_DIM)
    w_uk = _pad_last(ukv[..., :NOPE_DIM], HP).reshape(KV_RANK, 1024)
    w_uv = _pad_last(ukv[..., NOPE_DIM:], HP).reshape(KV_RANK, 1024)
    w_ba = jnp.pad(big["w_br_attn"].reshape(N_HEADS, V_DIM, D_MODEL), ((0, 0), (0, HP - V_DIM), (0, 0)))
    w_ba = w_ba.reshape(1024, D_MODEL)
    p.update(w_in=w_in, w_uq=w_uq, w_uk=w_uk, w_uv=w_uv, w_br_attn=w_ba, w_br_conv=big["w_br_conv"],
             w_br_mem=big["w_br_mem"], w_out=big["w_out"], w_mkv=big["w_mkv"])
    for n in ("w_in", "w_uq", "w_uk", "w_uv", "w_br_attn", "w_br_conv", "w_br_mem", "w_out", "w_mkv"):
        p[n + "_t"] = _t(p[n])
    for n in ("norm_g", "b_gate", "q_norm_g", "kv_norm_g", "mem_norm_g", "mem_q_g", "mem_k_g"):
        p[n] = full[n][l][None, :]
    p["q_head_g"] = _pad_last(full["q_head_g"][l][None, :], HP)
    p["k_head_g"] = _pad_last(full["k_head_g"][l][None, :], HP)
    p["conv_wb"] = jnp.concatenate(
        [full["conv_w"][l], full["conv_b"][l][None, :], jnp.zeros((4, CONV_W), F32)], axis=0)
    return p


def _join_shards(name, g):
    _, R, C = g.shape
    if name in _COL_SHARDED:
        return jnp.transpose(g, (1, 0, 2)).reshape(R, 4 * C)
    return g.reshape(4 * R, C)


def _split_shards(name, w):
    R, C = w.shape
    if name in _COL_SHARDED:
        return jnp.transpose(w.reshape(R, 4, C // 4), (1, 0, 2)).astype(BF16)
    return w.reshape(4, R // 4, C).astype(BF16)


def _unpad_grads(gp):
    out = {"w_in": _unpad_w_in(gp["w_in"])}
    out["w_uq"] = gp["w_uq"].reshape(Q_RANK, N_HEADS, HP)[..., :QK_DIM].reshape(Q_RANK, N_HEADS * QK_DIM)
    duk = gp["w_uk"].reshape(KV_RANK, N_HEADS, HP)[..., :NOPE_DIM]
    duv = gp["w_uv"].reshape(KV_RANK, N_HEADS, HP)[..., :V_DIM]
    out["w_ukv"] = jnp.concatenate([duk, duv], axis=-1).reshape(KV_RANK, 1024)
    out["w_br_attn"] = gp["w_br_attn"].reshape(N_HEADS, HP, D_MODEL)[:, :V_DIM].reshape(512, D_MODEL)
    for n in ("w_br_conv", "w_br_mem", "w_out", "w_mkv"):
        out[n] = gp[n]
    return out


def _train_step(x, mem, positions, w, target):
    S = x.shape[0]
    invf16 = ROPE_BASE ** (-jnp.arange(0, ROPE_DIM, 2, dtype=F32) / ROPE_DIM)
    invf = jnp.concatenate([jnp.zeros((NOPE_DIM,), F32), invf16, invf16, jnp.zeros((HP - QK_DIM,), F32)])[None, :]
    tabs = _rope_tables(jnp.broadcast_to(positions.reshape(S, 1), (S, HP)), invf)
    shards = [[w[n][l].astype(BF16) for n in _BIG] for l in range(DEPTH)]
    first = _gather_chips(shards[0] + [w["conv_w"]])
    gathered = first[:-1]
    full = {n: w[n] for n in _SMALL}
    full["conv_w"] = _cols_from_shards(first[-1])
    params, saved = [], []
    h = x
    for l in range(DEPTH):
        big = {n: _join_shards(n, g) for n, g in zip(_BIG, gathered)}
        params.append(_layer_params(big, full, l))
        h, s, gathered = _layer_fwd(h, mem, tabs, params[l], shards[l + 1] if l + 1 < DEPTH else ())
        saved.append(s)
    g, loss_part = _loss_head(h, target)
    per_layer, received = [None] * DEPTH, [None] * DEPTH
    to_owner = ()
    for l in reversed(range(DEPTH)):
        g, per_layer[l], got = _layer_bwd(g, mem, tabs, params[l], saved[l], to_owner)
        if to_owner:
            received[l + 1] = got
        big_g = _unpad_grads(per_layer[l])
        to_owner = [_split_shards(n, big_g[n]) for n in _BIG]
    st = lambda n: jnp.stack([per_layer[l][n] for l in range(DEPTH)])
    small = {}
    for n in ("norm_g", "b_gate", "q_norm_g", "kv_norm_g", "mem_norm_g", "mem_q_g", "mem_k_g"):
        small[n] = st(n)[:, 0, :]
    small["q_head_g"] = st("q_head_g")[:, 0, :QK_DIM]
    small["k_head_g"] = st("k_head_g")[:, 0, :QK_DIM]
    cwb = st("conv_wb")
    small["conv_w"] = cwb[:, 0:3, :]
    small["conv_b"] = cwb[:, 3, :]
    return loss_part, g, received, to_owner, small


_COL_SHARDED = ("w_in", "w_uq", "w_ukv", "w_br_attn", "w_br_conv", "w_br_mem")
_ROW_SHARDED = ("w_mkv", "w_out")
_BIG = _COL_SHARDED + _ROW_SHARDED
_SMALL = ("norm_g", "b_gate", "q_norm_g", "kv_norm_g", "q_head_g", "k_head_g", "conv_w", "conv_b",
          "mem_norm_g", "mem_q_g", "mem_k_g")
_ORDER = ("norm_g", "w_in", "b_gate", "q_norm_g", "w_uq", "kv_norm_g", "w_ukv", "q_head_g", "k_head_g",
          "conv_w", "conv_b", "mem_norm_g", "w_mkv", "mem_q_g", "mem_k_g", "w_br_attn", "w_br_conv",
          "w_br_mem", "w_out")


def _pack_small(d, extra):
    flat = jnp.concatenate([d[n].reshape(-1) for n in _SMALL] + [extra.reshape(-1)])
    n = flat.shape[0]
    rows = -(-n // 1024) * 8
    return jnp.pad(flat, (0, rows * 128 - n)).reshape(rows, 128)


def _unpack_small(packed, like):
    flat = packed.reshape(-1)
    out, off = {}, 0
    for n in _SMALL:
        sz = int(np.prod(like[n].shape))
        out[n] = flat[off:off + sz].reshape(like[n].shape)
        off += sz
    return out, flat[off:]


def kernel(x, mem, positions, norm_g, w_in, b_gate, q_norm_g, w_uq, kv_norm_g, w_ukv, q_head_g, k_head_g, conv_w, conv_b, mem_norm_g, w_mkv, mem_q_g, mem_k_g, w_br_attn, w_br_conv, w_br_mem, w_out, loss_target, m_norm_g, m_w_in, m_b_gate, m_q_norm_g, m_w_uq, m_kv_norm_g, m_w_ukv, m_q_head_g, m_k_head_g, m_conv_w, m_conv_b, m_mem_norm_g, m_w_mkv, m_mem_q_g, m_mem_k_g, m_w_br_attn, m_w_br_conv, m_w_br_mem, m_w_out, v_norm_g, v_w_in, v_b_gate, v_q_norm_g, v_w_uq, v_kv_norm_g, v_w_ukv, v_q_head_g, v_k_head_g, v_conv_w, v_conv_b, v_mem_norm_g, v_w_mkv, v_mem_q_g, v_mem_k_g, v_w_br_attn, v_w_br_conv, v_w_br_mem, v_w_out):
    w = dict(norm_g=norm_g, w_in=w_in, b_gate=b_gate, q_norm_g=q_norm_g, w_uq=w_uq, kv_norm_g=kv_norm_g,
             w_ukv=w_ukv, q_head_g=q_head_g, k_head_g=k_head_g, conv_w=conv_w, conv_b=conv_b,
             mem_norm_g=mem_norm_g, w_mkv=w_mkv, mem_q_g=mem_q_g, mem_k_g=mem_k_g, w_br_attn=w_br_attn,
             w_br_conv=w_br_conv, w_br_mem=w_br_mem, w_out=w_out)
    m = dict(norm_g=m_norm_g, w_in=m_w_in, b_gate=m_b_gate, q_norm_g=m_q_norm_g, w_uq=m_w_uq,
             kv_norm_g=m_kv_norm_g, w_ukv=m_w_ukv, q_head_g=m_q_head_g, k_head_g=m_k_head_g, conv_w=m_conv_w,
             conv_b=m_conv_b, mem_norm_g=m_mem_norm_g, w_mkv=m_w_mkv, mem_q_g=m_mem_q_g, mem_k_g=m_mem_k_g,
             w_br_attn=m_w_br_attn, w_br_conv=m_w_br_conv, w_br_mem=m_w_br_mem, w_out=m_w_out)
    v = dict(norm_g=v_norm_g, w_in=v_w_in, b_gate=v_b_gate, q_norm_g=v_q_norm_g, w_uq=v_w_uq,
             kv_norm_g=v_kv_norm_g, w_ukv=v_w_ukv, q_head_g=v_q_head_g, k_head_g=v_k_head_g, conv_w=v_conv_w,
             conv_b=v_conv_b, mem_norm_g=v_mem_norm_g, w_mkv=v_w_mkv, mem_q_g=v_mem_q_g, mem_k_g=v_mem_k_g,
             w_br_attn=v_w_br_attn, w_br_conv=v_w_br_conv, w_br_mem=v_w_br_mem, w_out=v_w_out)
    chip = 2 * lax.axis_index("x") + lax.axis_index("y")

    loss_part, grad_x, received, last_slabs, grads = _train_step(x[0], mem[0], positions[0], w, loss_target[0])

    loss_vec = jnp.zeros((128,), F32).at[0].set(0.5 / D_MODEL * jnp.sum(loss_part))
    small = _pack_small(grads, loss_vec)
    scattered = _scatter_chips(last_slabs, small)
    received[0] = scattered[:-1]
    small_sum = _sum_slabs(scattered[-1], "sum_small")
    partial = []
    for a, n in enumerate(_BIG):
        partial.append(jnp.concatenate([_sum_slabs(received[l][a], "sum_" + n) for l in range(DEPTH)], axis=0))
    other = _swap_cores(partial)

    small_g, tail = _unpack_small(small_sum, {n: (grads[n]) for n in _SMALL})
    loss = tail[0]
    small_g["conv_w"] = lax.dynamic_slice_in_dim(small_g["conv_w"], chip * (CONV_W // 4), CONV_W // 4, axis=2)

    outs_g, outs_d, outs_m, outs_v = {}, {}, {}, {}
    for n, p0, p1 in zip(_BIG, partial, other):
        shape = w[n].shape
        flat = lambda t: t.reshape(p0.shape)
        g_, d_, m_, v_ = _adamw(flat(w[n]), p0, p1, flat(m[n]), flat(v[n]), "adamw_" + n)
        outs_g[n], outs_d[n], outs_m[n], outs_v[n] = (t.reshape(shape) for t in (g_, d_, m_, v_))
    zero_small = jnp.zeros_like(small_sum)
    pk = lambda d: _pack_small(d, jnp.zeros((128,), F32))
    g_, d_, m_, v_ = _adamw(pk(w), _pack_small(small_g, jnp.zeros((128,), F32)), zero_small, pk(m), pk(v),
                            "adamw_small")
    like = {n: w[n] for n in _SMALL}
    for dst, packed in ((outs_g, g_), (outs_d, d_), (outs_m, m_), (outs_v, v_)):
        dst.update(_unpack_small(packed, like)[0])

    return (loss, grad_x[None], *[outs_g[n] for n in _ORDER], *[outs_d[n] for n in _ORDER],
            *[outs_m[n] for n in _ORDER], *[outs_v[n] for n in _ORDER])
```

```python
import functools

import numpy as np
import jax
import jax.numpy as jnp
from jax import lax
from jax.experimental import pallas as pl
from jax.experimental.pallas import tpu as pltpu

F32 = jnp.float32
BF16 = jnp.bfloat16

D_MODEL = 1024
DEPTH = 4
N_HEADS = 8
QK_DIM = 96
NOPE_DIM = 64
ROPE_DIM = 32
V_DIM = 64
Q_RANK = 384
KV_RANK = 256
CONV_W = 512
MEM_HEADS = 4
MEM_HD = 128
MEM_W = 512
IN_WIDTH = 7328
PW = 8192
HP = 128
PROJ_HALO = 16
F32_HALO = 8
EPS = 1e-6
ROPE_BASE = 10000.0
SCALE = QK_DIM ** -0.5
MEM_SCALE = MEM_HD ** -0.5

ADAM_LR = 0.001
ADAM_B1 = 0.9
ADAM_B2 = 0.999
ADAM_EPS = 1e-08
ADAM_WD = 0.01
ADAM_STEP = 10

VMEM_LIMIT_V7X = 56 * 1024 * 1024

O_R, O_GA, O_CB, O_QM, O_GC, O_GM, O_CC, O_CU, O_QL, O_KVL, O_KPE = (
    0, 3072, 4096, 4608, 5120, 5632, 6144, 6656, 7168, 7552, 7808)


def _params(sem, vmem=VMEM_LIMIT_V7X):
    return pltpu.CompilerParams(dimension_semantics=sem, vmem_limit_bytes=vmem)


def _sigmoid(t):
    return 0.5 * jnp.tanh(0.5 * t) + 0.5


def _silu_and_grad(g):
    sg = _sigmoid(g)
    return g * sg, sg * (1.0 + g * (1.0 - sg))


def _rms(t, g, n=None):
    n = t.shape[-1] if n is None else n
    r = lax.rsqrt(jnp.sum(t * t, axis=-1, keepdims=True) * (1.0 / n) + EPS)
    return (t * r) * g


def _rms_parts(t, n=None):
    n = t.shape[-1] if n is None else n
    r = lax.rsqrt(jnp.sum(t * t, axis=-1, keepdims=True) * (1.0 / n) + EPS)
    return r, t * r


def _rms_bwd(dhat, hat, r, n):
    return r * (dhat - hat * (jnp.sum(dhat * hat, axis=-1, keepdims=True) * (1.0 / n)))


def _rope(t, c, sa, sb):
    return t * c + pltpu.roll(t, HP - 16, 1) * sa + pltpu.roll(t, 16, 1) * sb


def _rope_t(d, c, sa, sb):
    return d * c + pltpu.roll(d * sa, 16, 1) + pltpu.roll(d * sb, HP - 16, 1)


def _dot(a, b):
    return jnp.dot(a, b, preferred_element_type=F32)


def _dot_nt(a, b):
    return lax.dot_general(a, b, (((1,), (1,)), ((), ())), preferred_element_type=F32)


def _dot_tn(a, b):
    return lax.dot_general(a, b, (((0,), (0,)), ((), ())), preferred_element_type=F32)


def _colsum(t):
    return jnp.sum(t, axis=0, keepdims=True)


def _tile(n, t):
    t = min(n, t)
    assert n % t == 0, (n, t)
    return t


def _rope_tables(pos_b, invf):
    S = pos_b.shape[0]
    tm = _tile(S, 1024)

    def body(pos_ref, invf_ref, c_ref, sa_ref, sb_ref):
        ang = pos_ref[...].astype(F32) * invf_ref[...]
        lane = lax.broadcasted_iota(jnp.int32, ang.shape, 1)
        cs = jnp.cos(ang)
        sn = jnp.sin(ang)
        c_ref[...] = jnp.where(lane < NOPE_DIM, 1.0, jnp.where(lane < QK_DIM, cs, 0.0))
        sa_ref[...] = jnp.where((lane >= NOPE_DIM) & (lane < NOPE_DIM + 16), -sn, 0.0)
        sb_ref[...] = jnp.where((lane >= NOPE_DIM + 16) & (lane < QK_DIM), sn, 0.0)

    blk = pl.BlockSpec((tm, HP), lambda i: (i, 0))
    return pl.pallas_call(
        body, name="rope_tables", grid=(S // tm,),
        in_specs=[blk, pl.BlockSpec((1, HP), lambda i: (0, 0))],
        out_specs=[blk, blk, blk],
        out_shape=[jax.ShapeDtypeStruct((S, HP), F32)] * 3,
        compiler_params=_params(("parallel",)),
    )(pos_b, invf)


MESH = pl.DeviceIdType.MESH
HBM = pl.BlockSpec(memory_space=pltpu.HBM)


def _chip_peer(x, y, c, k):
    return (x ^ (k >> 1), y ^ (k & 1), c)


def _chip_copies(ins, outs, send, recv, loc, gather):
    x, y, c = lax.axis_index("x"), lax.axis_index("y"), lax.axis_index("c")
    me = 2 * x + y
    cps = []
    for a in range(len(ins)):
        cps.append(pltpu.make_async_copy(ins[a] if gather else ins[a].at[me], outs[a].at[me], loc.at[a]))
        for k in (1, 2, 3):
            cps.append(pltpu.make_async_remote_copy(
                src_ref=ins[a] if gather else ins[a].at[me ^ k], dst_ref=outs[a].at[me],
                send_sem=send.at[3 * a + k - 1], recv_sem=recv.at[3 * a + k - 1],
                device_id=_chip_peer(x, y, c, k), device_id_type=MESH))
    return cps


def _carried(body, n_in, n_out, carry, gather):
    n = len(carry)

    def wrapped(*refs):
        ins, cin = refs[:n_in], refs[n_in:n_in + n]
        outs, cout = refs[n_in + n:n_in + n + n_out], refs[n_in + n + n_out:n_in + 2 * n + n_out]
        send, recv, loc = refs[n_in + 2 * n + n_out:]
        i = pl.program_id(0)

        @pl.when(i == 0)
        def _():
            for cp in _chip_copies(cin, cout, send, recv, loc, gather):
                cp.start()

        body(*ins, *outs)

        @pl.when(i == pl.num_programs(0) - 1)
        def _():
            for cp in _chip_copies(cin, cout, send, recv, loc, gather):
                cp.wait()

    specs = dict(
        in_specs=[HBM] * n, out_specs=[HBM] * n,
        out_shape=[jax.ShapeDtypeStruct(((4,) + a.shape) if gather else a.shape, a.dtype) for a in carry],
        scratch_shapes=[pltpu.SemaphoreType.DMA((3 * n,)), pltpu.SemaphoreType.DMA((3 * n,)),
                        pltpu.SemaphoreType.DMA((n,))])
    return wrapped, specs


def _inproj(x, g, w, carry=()):
    S = x.shape[0]
    tm = _tile(S, 256)

    def body(x_ref, g_ref, w_ref, proj_ref, xn_ref):
        h = _rms(x_ref[...], g_ref[...]).astype(BF16)
        xn_ref[...] = h
        proj_ref[...] = _dot(h, w_ref[...]).astype(BF16)

    row = lambda n: pl.BlockSpec((tm, n), lambda i: (i, 0))
    in_specs = [row(D_MODEL), pl.BlockSpec((1, D_MODEL), lambda i: (0, 0)), pl.BlockSpec((D_MODEL, PW), lambda i: (0, 0))]
    out_specs = [row(PW), row(D_MODEL)]
    out_shape = [jax.ShapeDtypeStruct((S, PW), BF16), jax.ShapeDtypeStruct((S, D_MODEL), BF16)]
    if not carry:
        proj, xn = pl.pallas_call(
            body, name="inproj", grid=(S // tm,), in_specs=in_specs, out_specs=out_specs, out_shape=out_shape,
            compiler_params=_params(("parallel",)))(x, g, w)
        return proj, xn, []
    wrapped, extra = _carried(body, 3, 2, carry, gather=True)
    res = pl.pallas_call(
        wrapped, name="inproj_gather", grid=(S // tm,), in_specs=in_specs + extra["in_specs"],
        out_specs=out_specs + extra["out_specs"], out_shape=out_shape + extra["out_shape"],
        scratch_shapes=extra["scratch_shapes"], compiler_params=_params(("arbitrary",)))(x, g, w, *carry)
    return res[0], res[1], list(res[2:])


def _mla_prep(proj, tabs, qg, kvg, qhg, khg, wuq, wuk, wuv):
    S = proj.shape[0]
    tm = _tile(S, 512)

    def body(a_ref, c_ref, sa_ref, sb_ref, qg_ref, kvg_ref, qhg_ref, khg_ref, wuq_ref, wuk_ref, wuv_ref,
             q_ref, k_ref, v_ref, qn_ref, kvn_ref, bound_ref, qt_ref, vt_ref):
        gq = jnp.max(jnp.abs(qhg_ref[...]), axis=-1, keepdims=True)
        gk = jnp.max(jnp.abs(khg_ref[...]), axis=-1, keepdims=True)
        bound_ref[...] = jnp.broadcast_to(gq * gk * (QK_DIM ** 0.5 * 1.01) + 1e-6, bound_ref.shape)
        ql = a_ref[:, 0:Q_RANK].astype(F32)
        kvl = a_ref[:, Q_RANK:Q_RANK + KV_RANK].astype(F32)
        kpe = a_ref[:, Q_RANK + KV_RANK:Q_RANK + KV_RANK + HP].astype(F32)
        qn = _rms(ql, qg_ref[...]).astype(BF16)
        kvn = _rms(kvl, kvg_ref[...]).astype(BF16)
        qn_ref[...] = qn
        kvn_ref[...] = kvn
        qraw = _dot(qn, wuq_ref[...])
        kn = _dot(kvn, wuk_ref[...])
        vf = _dot(kvn, wuv_ref[...])
        v_ref[...] = vf.astype(BF16)
        c, sa, sb = c_ref[...], sa_ref[...], sb_ref[...]
        for h in range(N_HEADS):
            sl = slice(h * HP, (h + 1) * HP)
            tq = _rms(qraw[:, sl], qhg_ref[...], QK_DIM)
            qh = _rope(tq, c, sa, sb) * SCALE
            q_ref[:, sl] = qh.astype(BF16)
            qt_ref[sl, :] = qh.T.astype(BF16)
            vt_ref[sl, :] = vf[:, sl].T.astype(BF16)
            tk = _rms(kn[:, sl] + kpe, khg_ref[...], QK_DIM)
            k_ref[:, sl] = _rope(tk, c, sa, sb).astype(BF16)

    row = lambda w: pl.BlockSpec((tm, w), lambda i: (i, 0))
    full = lambda a: pl.BlockSpec(a.shape, lambda i: (0,) * a.ndim)
    return pl.pallas_call(
        body, name="mla_prep", grid=(S // tm,),
        in_specs=[pl.BlockSpec((tm, 1024), lambda i: (i, 7)), row(HP), row(HP), row(HP),
                  full(qg), full(kvg), full(qhg), full(khg), full(wuq), full(wuk), full(wuv)],
        out_specs=[row(1024), row(1024), row(1024), row(Q_RANK), row(KV_RANK),
                   pl.BlockSpec((1, HP), lambda i: (0, 0)),
                   pl.BlockSpec((1024, tm), lambda i: (0, i)), pl.BlockSpec((1024, tm), lambda i: (0, i))],
        out_shape=[jax.ShapeDtypeStruct((S, 1024), BF16)] * 3
        + [jax.ShapeDtypeStruct((S, Q_RANK), BF16), jax.ShapeDtypeStruct((S, KV_RANK), BF16),
           jax.ShapeDtypeStruct((1, HP), F32)] + [jax.ShapeDtypeStruct((1024, S), BF16)] * 2,
        compiler_params=_params(("arbitrary",)),
    )(proj, *tabs, qg, kvg, qhg, khg, wuq, wuk, wuv)


SAFE_SCORE_BOUND = 30.0


def _attn_fwd(qt, k, vt, score_bound):
    S = k.shape[0]
    tq, tk = _tile(S, 1024), _tile(S, 2048)
    nk = S // tk

    def body(qt_ref, k_ref, vt_ref, bound_ref, o_ref, lse_ref, m_s, l_s, acc_s):
        qtv = qt_ref[...]
        bound = bound_ref[0:1, 0:1]
        safe = jnp.max(bound) <= SAFE_SCORE_BOUND
        l_s[...] = jnp.zeros(l_s.shape, F32)
        acc_s[...] = jnp.zeros(acc_s.shape, F32)

        def keys(c):
            return pl.ds(pl.multiple_of(c * tk, tk), tk)

        @pl.when(safe)
        def _():
            def step(c, carry):
                pt = jnp.exp(_dot(k_ref[keys(c), :], qtv) - bound)
                l_s[...] += jnp.sum(pt, axis=0, keepdims=True)
                acc_s[...] += _dot(vt_ref[:, keys(c)], pt.astype(BF16))
                return carry

            lax.fori_loop(0, nk, step, 0)
            m_s[...] = jnp.broadcast_to(bound, m_s.shape)

        @pl.when(jnp.logical_not(safe))
        def _():
            m_s[...] = jnp.full(m_s.shape, -jnp.inf, F32)

            def step(c, carry):
                st = _dot(k_ref[keys(c), :], qtv)
                m_prev = m_s[...]
                m_new = jnp.maximum(m_prev, jnp.max(st, axis=0, keepdims=True))
                alpha = jnp.exp(m_prev - m_new)
                pt = jnp.exp(st - m_new)
                l_s[...] = alpha * l_s[...] + jnp.sum(pt, axis=0, keepdims=True)
                acc_s[...] = alpha * acc_s[...] + _dot(vt_ref[:, keys(c)], pt.astype(BF16))
                m_s[...] = m_new
                return carry

            lax.fori_loop(0, nk, step, 0)

        o_ref[...] = (acc_s[...] / l_s[...]).T
        lse_row = m_s[...] + jnp.log(l_s[...])
        lse_ref[0] = jnp.broadcast_to(lse_row, (HP, tq)).T[:, 0:1]

    return pl.pallas_call(
        body, name="attn_fwd", grid=(N_HEADS, S // tq),
        in_specs=[pl.BlockSpec((HP, tq), lambda h, i: (h, i)),
                  pl.BlockSpec((S, HP), lambda h, i: (0, h)),
                  pl.BlockSpec((HP, S), lambda h, i: (h, 0)),
                  pl.BlockSpec((1, HP), lambda h, i: (0, 0))],
        out_specs=[pl.BlockSpec((tq, HP), lambda h, i: (i, h)),
                   pl.BlockSpec((1, tq, 1), lambda h, i: (h, i, 0))],
        out_shape=[jax.ShapeDtypeStruct((S, N_HEADS * HP), F32),
                   jax.ShapeDtypeStruct((N_HEADS, S, 1), F32)],
        scratch_shapes=[pltpu.VMEM((1, tq), F32), pltpu.VMEM((1, tq), F32), pltpu.VMEM((HP, tq), F32)],
        compiler_params=_params(("parallel", "parallel")),
    )(qt, k, vt, score_bound)


def _memkv(mem, mng, mkg, wmkv):
    M = mem.shape[0]

    def body(mem_ref, mng_ref, mkg_ref, w_ref, mk_ref, mv_ref):
        mn = _rms(mem_ref[...], mng_ref[...]).astype(BF16)
        mkv = _dot(mn, w_ref[...])
        for h in range(MEM_HEADS):
            kraw = mkv[:, 2 * MEM_HD * h:2 * MEM_HD * h + MEM_HD]
            mk_ref[:, MEM_HD * h:MEM_HD * (h + 1)] = _rms(kraw, mkg_ref[...]).astype(BF16)
            mv_ref[:, MEM_HD * h:MEM_HD * (h + 1)] = mkv[:, 2 * MEM_HD * h + MEM_HD:2 * MEM_HD * (h + 1)].astype(BF16)

    return pl.pallas_call(
        body, name="memkv",
        out_shape=[jax.ShapeDtypeStruct((M, MEM_W), BF16)] * 2,
        compiler_params=pltpu.CompilerParams(vmem_limit_bytes=VMEM_LIMIT_V7X),
    )(mem, mng, mkg, wmkv)


def _conv_shifts(cc, cu, hp_ref, hn_ref, i, n_tiles, tm):
    z = cc * cu
    last = PROJ_HALO - 1
    zp = hp_ref[last:last + 1, 0:CONV_W].astype(F32) * hp_ref[last:last + 1, CONV_W:2 * CONV_W].astype(F32)
    zn = hn_ref[0:1, 0:CONV_W].astype(F32) * hn_ref[0:1, CONV_W:2 * CONV_W].astype(F32)
    zp = jnp.where(i == 0, 0.0, zp)
    zn = jnp.where(i == n_tiles - 1, 0.0, zn)
    row = lax.broadcasted_iota(jnp.int32, z.shape, 0)
    z_up = jnp.where(row == 0, zp, pltpu.roll(z, 1, 0))
    z_dn = jnp.where(row == tm - 1, zn, pltpu.roll(z, tm - 1, 0))
    return z, z_up, z_dn


def _halo_specs(tm, S, width, col, rows):
    per = tm // rows
    prev = pl.BlockSpec((rows, width), lambda i: (jnp.maximum(i * per - 1, 0), col))
    nxt = pl.BlockSpec((rows, width), lambda i: (jnp.minimum((i + 1) * per, S // rows - 1), col))
    return prev, nxt


def _mem_attend(qm, mqg, mk_h, mv_h):
    r, qhat = _rms_parts(qm)
    mq = (qhat * mqg).astype(BF16)
    s = _dot_nt(mq, mk_h) * MEM_SCALE
    e = jnp.exp(s - jnp.max(s, axis=-1, keepdims=True))
    p = e / jnp.sum(e, axis=-1, keepdims=True)
    pv = _dot(p.astype(BF16), mv_h)
    return r, qhat, mq, p, pv


def _merge(proj, o, x, bg, convw, mqg, mk, mv, wba, wbc, wbm, wo):
    S = x.shape[0]
    tm = _tile(S, 512)
    nt = S // tm

    def body(main_ref, ccu_ref, hp_ref, hn_ref, o_ref, x_ref, bg_ref, cw_ref, mqg_ref, mk_ref, mv_ref,
             wba_ref, wbc_ref, wbm_ref, wo_ref, xn_ref, oa_ref, oc_ref, om_ref, u_ref, y_ref):
        i = pl.program_id(0)
        sil_a, _ = _silu_and_grad(main_ref[:, O_GA:O_GA + 1024].astype(F32))
        oa = (o_ref[...] * sil_a).astype(BF16)
        oa_ref[...] = oa
        z, z_up, z_dn = _conv_shifts(ccu_ref[:, 0:CONV_W].astype(F32), ccu_ref[:, CONV_W:].astype(F32), hp_ref, hn_ref, i, nt, tm)
        cv = cw_ref[0:1, :] * z_up + cw_ref[1:2, :] * z + cw_ref[2:3, :] * z_dn + cw_ref[3:4, :]
        sil_c, _ = _silu_and_grad(main_ref[:, O_GC:O_GC + CONV_W].astype(F32))
        oc = (main_ref[:, O_CB:O_CB + CONV_W].astype(F32) * cv * sil_c).astype(BF16)
        oc_ref[...] = oc
        sil_m, _ = _silu_and_grad(main_ref[:, O_GM:O_GM + MEM_W].astype(F32))
        for h in range(MEM_HEADS):
            sl = slice(h * MEM_HD, (h + 1) * MEM_HD)
            qm = main_ref[:, O_QM + h * MEM_HD:O_QM + (h + 1) * MEM_HD].astype(F32)
            pv = _mem_attend(qm, mqg_ref[...], mk_ref[:, sl], mv_ref[:, sl])[4]
            om_ref[:, sl] = (pv * sil_m[:, sl]).astype(BF16)
        ua = _dot(oa, wba_ref[...])
        uc = _dot(oc, wbc_ref[...])
        um = _dot(om_ref[...], wbm_ref[...])
        u_ref[:, 0:1024] = ua.astype(BF16)
        u_ref[:, 1024:2048] = uc.astype(BF16)
        u_ref[:, 2048:3072] = um.astype(BF16)
        rg = _sigmoid(main_ref[:, O_R:O_R + 3072].astype(F32) + bg_ref[...])
        y = (rg[:, 0:1024] * ua + rg[:, 1024:2048] * uc + rg[:, 2048:3072] * um).astype(BF16)
        y_ref[...] = y
        xn_ref[...] = x_ref[...] + _dot(y, wo_ref[...])

    row = lambda w: pl.BlockSpec((tm, w), lambda i: (i, 0))
    full = lambda a: pl.BlockSpec(a.shape, lambda i: (0,) * a.ndim)
    hp, hn = _halo_specs(tm, S, 1024, 6, PROJ_HALO)
    return pl.pallas_call(
        body, name="merge", grid=(nt,),
        in_specs=[row(6144), pl.BlockSpec((tm, 1024), lambda i: (i, 6)), hp, hn, row(1024), row(1024),
                  full(bg), full(convw), full(mqg), full(mk), full(mv), full(wba), full(wbc), full(wbm), full(wo)],
        out_specs=[row(1024), row(1024), row(CONV_W), row(MEM_W), row(3072), row(1024)],
        out_shape=[jax.ShapeDtypeStruct((S, 1024), F32), jax.ShapeDtypeStruct((S, 1024), BF16),
                   jax.ShapeDtypeStruct((S, CONV_W), BF16), jax.ShapeDtypeStruct((S, MEM_W), BF16),
                   jax.ShapeDtypeStruct((S, 3072), BF16), jax.ShapeDtypeStruct((S, 1024), BF16)],
        compiler_params=_params(("parallel",)),
    )(proj, proj, proj, proj, o, x, bg, convw, mqg, mk, mv, wba, wbc, wbm, wo)


def _loss_head(xf, tgt):
    S = xf.shape[0]
    tm = _tile(S, 1024)

    def body(x_ref, t_ref, g_ref, acc_ref):
        @pl.when(pl.program_id(0) == 0)
        def _():
            acc_ref[...] = jnp.zeros(acc_ref.shape, F32)

        e = x_ref[...] - t_ref[...]
        g_ref[...] = e * (1.0 / D_MODEL)
        part = jnp.sum((e * e).reshape(tm // 8, 8, D_MODEL), axis=0)
        tot = part[:, 0:128]
        for k in range(1, D_MODEL // 128):
            tot = tot + part[:, 128 * k:128 * (k + 1)]
        acc_ref[...] += tot

    row = pl.BlockSpec((tm, D_MODEL), lambda i: (i, 0))
    return pl.pallas_call(
        body, name="loss_head", grid=(S // tm,),
        in_specs=[row, row],
        out_specs=[row, pl.BlockSpec((8, 128), lambda i: (0, 0))],
        out_shape=[jax.ShapeDtypeStruct((S, D_MODEL), F32), jax.ShapeDtypeStruct((8, 128), F32)],
        compiler_params=_params(("arbitrary",)),
    )(xf, tgt)


def _merge_bwd(g, proj, o, u, bg, convw, mqg, mk, mv, wot, wbat, wbct, wbmt):
    S = g.shape[0]
    tm = _tile(S, 256)
    nt = S // tm
    M = mk.shape[0]

    def body(g_ref, main_ref, ccu_ref, hp_ref, hn_ref, o_ref, u_ref, bg_ref, cw_ref, mqg_ref, mk_ref, mv_ref,
             wot_ref, wbat_ref, wbct_ref, wbmt_ref,
             dmain_ref, dcv_ref, do_ref, delta_ref, du_ref, dbg_ref, dmk_ref, dmv_ref, dmqg_ref, dot_ref):
        i = pl.program_id(0)

        @pl.when(i == 0)
        def _():
            dbg_ref[...] = jnp.zeros(dbg_ref.shape, F32)
            dmk_ref[...] = jnp.zeros(dmk_ref.shape, F32)
            dmv_ref[...] = jnp.zeros(dmv_ref.shape, F32)
            dmqg_ref[...] = jnp.zeros(dmqg_ref.shape, F32)

        dy = _dot(g_ref[...].astype(BF16), wot_ref[...])
        d_branch = []
        for b, wt_ref in enumerate((wbat_ref, wbct_ref, wbmt_ref)):
            cols = slice(O_R + b * D_MODEL, O_R + (b + 1) * D_MODEL)
            rg = _sigmoid(main_ref[:, cols].astype(F32) + bg_ref[:, cols])
            dr = dy * u_ref[:, cols].astype(F32) * rg * (1.0 - rg)
            dmain_ref[:, cols] = dr.astype(BF16)
            dbg_ref[:, cols] += _colsum(dr)
            du = (dy * rg).astype(BF16)
            du_ref[:, cols] = du
            d_branch.append(_dot(du, wt_ref[...]))
        do_a, do_c, do_m = d_branch

        sil_a, dsil_a = _silu_and_grad(main_ref[:, O_GA:O_GA + 1024].astype(F32))
        ov = o_ref[...]
        d_o = do_a * sil_a
        do_ref[...] = d_o.astype(BF16)
        dot_ref[...] = d_o.T.astype(BF16)
        dmain_ref[:, O_GA:O_GA + 1024] = (do_a * ov * dsil_a).astype(BF16)
        prod = d_o * ov
        for h in range(N_HEADS):
            delta_ref[h] = jnp.sum(prod[:, h * HP:(h + 1) * HP], axis=-1, keepdims=True)

        z, z_up, z_dn = _conv_shifts(ccu_ref[:, 0:CONV_W].astype(F32), ccu_ref[:, CONV_W:].astype(F32), hp_ref, hn_ref, i, nt, tm)
        cv = cw_ref[0:1, :] * z_up + cw_ref[1:2, :] * z + cw_ref[2:3, :] * z_dn + cw_ref[3:4, :]
        sil_c, dsil_c = _silu_and_grad(main_ref[:, O_GC:O_GC + CONV_W].astype(F32))
        cb = main_ref[:, O_CB:O_CB + CONV_W].astype(F32)
        dmain_ref[:, O_CB:O_CB + CONV_W] = (do_c * cv * sil_c).astype(BF16)
        dmain_ref[:, O_GC:O_GC + CONV_W] = (do_c * cb * cv * dsil_c).astype(BF16)
        dcv_ref[...] = do_c * cb * sil_c

        sil_m, dsil_m = _silu_and_grad(main_ref[:, O_GM:O_GM + MEM_W].astype(F32))
        for h in range(MEM_HEADS):
            sl = slice(h * MEM_HD, (h + 1) * MEM_HD)
            qm = main_ref[:, O_QM + h * MEM_HD:O_QM + (h + 1) * MEM_HD].astype(F32)
            mk_h, mv_h = mk_ref[:, sl], mv_ref[:, sl]
            r, qhat, mq, p, pv = _mem_attend(qm, mqg_ref[...], mk_h, mv_h)
            dom = do_m[:, sl]
            dmain_ref[:, O_GM + h * MEM_HD:O_GM + (h + 1) * MEM_HD] = (dom * pv * dsil_m[:, sl]).astype(BF16)
            dpv = (dom * sil_m[:, sl]).astype(BF16)
            dp = _dot_nt(dpv, mv_h)
            ds = (p * (dp - jnp.sum(dp * p, axis=-1, keepdims=True)) * MEM_SCALE).astype(BF16)
            dmq = _dot(ds, mk_h)
            dmk_ref[:, sl] += _dot_tn(ds, mq)
            dmv_ref[:, sl] += _dot_tn(p.astype(BF16), dpv)
            dmqg_ref[...] += _colsum(dmq * qhat)
            dqm = _rms_bwd(dmq * mqg_ref[...], qhat, r, MEM_HD)
            dmain_ref[:, O_QM + h * MEM_HD:O_QM + (h + 1) * MEM_HD] = dqm.astype(BF16)

    row = lambda w: pl.BlockSpec((tm, w), lambda i: (i, 0))
    full = lambda a: pl.BlockSpec(a.shape, lambda i: (0,) * a.ndim)
    acc = lambda r, c: pl.BlockSpec((r, c), lambda i: (0, 0))
    hp, hn = _halo_specs(tm, S, 1024, 6, PROJ_HALO)
    return pl.pallas_call(
        body, name="merge_bwd", grid=(nt,),
        in_specs=[row(1024), row(6144), pl.BlockSpec((tm, 1024), lambda i: (i, 6)), hp, hn, row(1024), row(3072),
                  full(bg), full(convw), full(mqg), full(mk), full(mv), full(wot), full(wbat), full(wbct), full(wbmt)],
        out_specs=[row(6144), row(CONV_W), row(1024), pl.BlockSpec((N_HEADS, tm, 1), lambda i: (0, i, 0)), row(3072),
                   acc(1, 3072), acc(M, MEM_W), acc(M, MEM_W), acc(1, MEM_HD),
                   pl.BlockSpec((1024, tm), lambda i: (0, i))],
        out_shape=[jax.ShapeDtypeStruct((S, 6144), BF16), jax.ShapeDtypeStruct((S, CONV_W), F32),
                   jax.ShapeDtypeStruct((S, 1024), BF16), jax.ShapeDtypeStruct((N_HEADS, S, 1), F32),
                   jax.ShapeDtypeStruct((S, 3072), BF16), jax.ShapeDtypeStruct((1, 3072), F32),
                   jax.ShapeDtypeStruct((M, MEM_W), F32), jax.ShapeDtypeStruct((M, MEM_W), F32),
                   jax.ShapeDtypeStruct((1, MEM_HD), F32), jax.ShapeDtypeStruct((1024, S), BF16)],
        compiler_params=_params(("arbitrary",)),
    )(g, proj, proj, proj, proj, o, u, bg, convw, mqg, mk, mv, wot, wbat, wbct, wbmt)


def _conv_bwd(dcv, proj, convw):
    S = dcv.shape[0]
    tm = _tile(S, 512)
    nt = S // tm

    def body(d_ref, dp_ref, dn_ref, ccu_ref, hp_ref, hn_ref, cw_ref, dccu_ref, dcw_ref):
        i = pl.program_id(0)

        @pl.when(i == 0)
        def _():
            dcw_ref[...] = jnp.zeros(dcw_ref.shape, F32)

        cc, cu = ccu_ref[:, 0:CONV_W].astype(F32), ccu_ref[:, CONV_W:].astype(F32)
        z, z_up, z_dn = _conv_shifts(cc, cu, hp_ref, hn_ref, i, nt, tm)
        d = d_ref[...]
        dprev = jnp.where(i == 0, 0.0, dp_ref[7:8, :])
        dnext = jnp.where(i == nt - 1, 0.0, dn_ref[0:1, :])
        row = lax.broadcasted_iota(jnp.int32, d.shape, 0)
        d_up = jnp.where(row == 0, dprev, pltpu.roll(d, 1, 0))
        d_dn = jnp.where(row == tm - 1, dnext, pltpu.roll(d, tm - 1, 0))
        dz = cw_ref[0:1, :] * d_dn + cw_ref[1:2, :] * d + cw_ref[2:3, :] * d_up
        dccu_ref[:, 0:CONV_W] = (dz * cu).astype(BF16)
        dccu_ref[:, CONV_W:] = (dz * cc).astype(BF16)
        dcw_ref[0:1, :] += _colsum(d * z_up)
        dcw_ref[1:2, :] += _colsum(d * z)
        dcw_ref[2:3, :] += _colsum(d * z_dn)
        dcw_ref[3:4, :] += _colsum(d)

    hp, hn = _halo_specs(tm, S, 1024, 6, PROJ_HALO)
    dp, dn = _halo_specs(tm, S, CONV_W, 0, F32_HALO)
    return pl.pallas_call(
        body, name="conv_bwd", grid=(nt,),
        in_specs=[pl.BlockSpec((tm, CONV_W), lambda i: (i, 0)), dp, dn,
                  pl.BlockSpec((tm, 1024), lambda i: (i, 6)), hp, hn,
                  pl.BlockSpec((8, CONV_W), lambda i: (0, 0))],
        out_specs=[pl.BlockSpec((tm, 1024), lambda i: (i, 0)), pl.BlockSpec((8, CONV_W), lambda i: (0, 0))],
        out_shape=[jax.ShapeDtypeStruct((S, 1024), BF16), jax.ShapeDtypeStruct((8, CONV_W), F32)],
        compiler_params=_params(("arbitrary",)),
    )(dcv, dcv, dcv, proj, proj, proj, convw)


def _attn_bwd(q, k, v, do, qt, dot, lse, delta):
    S = q.shape[0]
    tq, tk = _tile(S, 512), _tile(S, 4096)
    ni, nk = S // tq, S // tk

    def body(q_ref, do_ref, qt_ref, dot_ref, k_ref, v_ref, lse_ref, delta_ref, dq_ref, dkt_hbm, dvt_hbm,
             dq_s, dkt_s, dvt_s, sem):
        h, i = pl.program_id(0), pl.program_id(1)

        @pl.when(i == 0)
        def _():
            dkt_s[...] = jnp.zeros(dkt_s.shape, F32)
            dvt_s[...] = jnp.zeros(dvt_s.shape, F32)

        dq_s[...] = jnp.zeros(dq_s.shape, F32)
        qv, dov, qtv, dotv = q_ref[...], do_ref[...], qt_ref[...], dot_ref[...]
        lse_c, delta_c = lse_ref[0], delta_ref[0]

        def step(c, carry):
            cols = pl.ds(pl.multiple_of(c * tk, tk), tk)
            kc, vc = k_ref[cols, :], v_ref[cols, :]
            p = jnp.exp(_dot_nt(qv, kc) - lse_c)
            dp = _dot_nt(dov, vc)
            ds = (p * (dp - delta_c)).astype(BF16)
            dq_s[...] += _dot(ds, kc)
            dvt_s[:, cols] += _dot(dotv, p.astype(BF16))
            dkt_s[:, cols] += _dot(qtv, ds)
            return carry

        lax.fori_loop(0, nk, step, 0)
        dq_ref[...] = dq_s[...]

        @pl.when(i == ni - 1)
        def _():
            head = pl.ds(pl.multiple_of(h * HP, HP), HP)
            out_k = pltpu.make_async_copy(dkt_s, dkt_hbm.at[head, :], sem.at[0])
            out_v = pltpu.make_async_copy(dvt_s, dvt_hbm.at[head, :], sem.at[1])
            out_k.start()
            out_v.start()
            out_k.wait()
            out_v.wait()

    col = pl.BlockSpec((1, tq, 1), lambda h, i: (h, i, 0))
    blk = pl.BlockSpec((tq, HP), lambda h, i: (i, h))
    blkt = pl.BlockSpec((HP, tq), lambda h, i: (h, i))
    res = pl.BlockSpec((S, HP), lambda h, i: (0, h))
    whole = pl.BlockSpec(memory_space=pl.ANY)
    return pl.pallas_call(
        body, name="attn_bwd", grid=(N_HEADS, ni),
        in_specs=[blk, blk, blkt, blkt, res, res, col, col],
        out_specs=[blk, whole, whole],
        out_shape=[jax.ShapeDtypeStruct((S, N_HEADS * HP), F32), jax.ShapeDtypeStruct((N_HEADS * HP, S), F32),
                   jax.ShapeDtypeStruct((N_HEADS * HP, S), F32)],
        scratch_shapes=[pltpu.VMEM((tq, HP), F32), pltpu.VMEM((HP, S), F32), pltpu.VMEM((HP, S), F32),
                        pltpu.SemaphoreType.DMA((2,))],
        compiler_params=_params(("parallel", "arbitrary")),
    )(q, do, qt, dot, k, v, lse, delta)


def _mla_prep_bwd(proj, tabs, dq, dkt, dvt, qg, kvg, qhg, khg, wuq, wuk, wuqt, wukt, wuvt):
    S = proj.shape[0]
    tm = _tile(S, 512)

    def body(a_ref, c_ref, sa_ref, sb_ref, dq_ref, dkt_ref, dvt_ref, qg_ref, kvg_ref, qhg_ref, khg_ref,
             wuq_ref, wuk_ref, wuqt_ref, wukt_ref, wuvt_ref,
             da_ref, dqraw_ref, dkraw_ref, dqg_ref, dkvg_ref, dqhg_ref, dkhg_ref, dv_ref):
        @pl.when(pl.program_id(0) == 0)
        def _():
            dqg_ref[...] = jnp.zeros(dqg_ref.shape, F32)
            dkvg_ref[...] = jnp.zeros(dkvg_ref.shape, F32)
            dqhg_ref[...] = jnp.zeros(dqhg_ref.shape, F32)
            dkhg_ref[...] = jnp.zeros(dkhg_ref.shape, F32)

        ql = a_ref[:, 0:Q_RANK].astype(F32)
        kvl = a_ref[:, Q_RANK:Q_RANK + KV_RANK].astype(F32)
        kpe = a_ref[:, Q_RANK + KV_RANK:Q_RANK + KV_RANK + HP].astype(F32)
        rq, qhat = _rms_parts(ql)
        rkv, kvhat = _rms_parts(kvl)
        qraw = _dot((qhat * qg_ref[...]).astype(BF16), wuq_ref[...])
        kn = _dot((kvhat * kvg_ref[...]).astype(BF16), wuk_ref[...])
        c, sa, sb = c_ref[...], sa_ref[...], sb_ref[...]
        dkpe = jnp.zeros(kpe.shape, F32)
        dqhg = jnp.zeros((1, HP), F32)
        dkhg = jnp.zeros((1, HP), F32)
        for h in range(N_HEADS):
            sl = slice(h * HP, (h + 1) * HP)
            r, that = _rms_parts(qraw[:, sl], QK_DIM)
            dtn = _rope_t(dq_ref[:, sl], c, sa, sb) * SCALE
            dqhg = dqhg + _colsum(dtn * that)
            dqraw_ref[:, sl] = _rms_bwd(dtn * qhg_ref[...], that, r, QK_DIM).astype(BF16)
            r, that = _rms_parts(kn[:, sl] + kpe, QK_DIM)
            dtn = _rope_t(dkt_ref[sl, :].T, c, sa, sb)
            dv_ref[:, sl] = dvt_ref[sl, :].T.astype(BF16)
            dkhg = dkhg + _colsum(dtn * that)
            dkr = _rms_bwd(dtn * khg_ref[...], that, r, QK_DIM)
            dkraw_ref[:, sl] = dkr.astype(BF16)
            dkpe = dkpe + dkr
        dqhg_ref[...] += dqhg
        dkhg_ref[...] += dkhg
        dqn = _dot(dqraw_ref[...], wuqt_ref[...])
        dqg_ref[...] += _colsum(dqn * qhat)
        da_ref[:, 0:Q_RANK] = _rms_bwd(dqn * qg_ref[...], qhat, rq, Q_RANK).astype(BF16)
        dkvn = _dot(dkraw_ref[...], wukt_ref[...]) + _dot(dv_ref[...], wuvt_ref[...])
        dkvg_ref[...] += _colsum(dkvn * kvhat)
        da_ref[:, Q_RANK:Q_RANK + KV_RANK] = _rms_bwd(dkvn * kvg_ref[...], kvhat, rkv, KV_RANK).astype(BF16)
        da_ref[:, Q_RANK + KV_RANK:Q_RANK + KV_RANK + HP] = dkpe.astype(BF16)
        da_ref[:, Q_RANK + KV_RANK + HP:] = jnp.zeros((tm, 1024 - Q_RANK - KV_RANK - HP), BF16)

    row = lambda w: pl.BlockSpec((tm, w), lambda i: (i, 0))
    full = lambda a: pl.BlockSpec(a.shape, lambda i: (0,) * a.ndim)
    acc = lambda c: pl.BlockSpec((1, c), lambda i: (0, 0))
    return pl.pallas_call(
        body, name="mla_prep_bwd", grid=(S // tm,),
        in_specs=[pl.BlockSpec((tm, 1024), lambda i: (i, 7)), row(HP), row(HP), row(HP),
                  row(1024), pl.BlockSpec((1024, tm), lambda i: (0, i)), pl.BlockSpec((1024, tm), lambda i: (0, i)),
                  full(qg), full(kvg), full(qhg), full(khg),
                  full(wuq), full(wuk), full(wuqt), full(wukt), full(wuvt)],
        out_specs=[row(1024), row(1024), row(1024), acc(Q_RANK), acc(KV_RANK), acc(HP), acc(HP), row(1024)],
        out_shape=[jax.ShapeDtypeStruct((S, 1024), BF16)] * 3
        + [jax.ShapeDtypeStruct((1, Q_RANK), F32), jax.ShapeDtypeStruct((1, KV_RANK), F32),
           jax.ShapeDtypeStruct((1, HP), F32), jax.ShapeDtypeStruct((1, HP), F32),
           jax.ShapeDtypeStruct((S, 1024), BF16)],
        compiler_params=_params(("arbitrary",)),
    )(proj, *tabs, dq, dkt, dvt, qg, kvg, qhg, khg, wuq, wuk, wuqt, wukt, wuvt)


def _inproj_bwd(dmain, dccu, dsega, wint, x, g, ng, carry=()):
    S = x.shape[0]
    tm = _tile(S, 256)
    nm, nc = dmain.shape[1], dccu.shape[1]
    assert nm + nc + dsega.shape[1] == PW

    def body(dm_ref, dc_ref, da_ref, w_ref, x_ref, g_ref, ng_ref, dx_ref, dng_ref):
        @pl.when(pl.program_id(0) == 0)
        def _():
            dng_ref[...] = jnp.zeros(dng_ref.shape, F32)

        dh = (_dot(dm_ref[...], w_ref[0:nm, :]) + _dot(dc_ref[...], w_ref[nm:nm + nc, :])
              + _dot(da_ref[...], w_ref[nm + nc:PW, :]))
        r, xhat = _rms_parts(x_ref[...])
        dng_ref[...] += _colsum(dh * xhat)
        dx_ref[...] = g_ref[...] + _rms_bwd(dh * ng_ref[...], xhat, r, D_MODEL)

    row = lambda w: pl.BlockSpec((tm, w), lambda i: (i, 0))
    in_specs = [row(nm), row(nc), row(dsega.shape[1]), pl.BlockSpec((PW, D_MODEL), lambda i: (0, 0)),
                row(D_MODEL), row(D_MODEL), pl.BlockSpec((1, D_MODEL), lambda i: (0, 0))]
    out_specs = [row(D_MODEL), pl.BlockSpec((1, D_MODEL), lambda i: (0, 0))]
    out_shape = [jax.ShapeDtypeStruct((S, D_MODEL), F32), jax.ShapeDtypeStruct((1, D_MODEL), F32)]
    args = (dmain, dccu, dsega, wint, x, g, ng)
    if not carry:
        dx, dng = pl.pallas_call(
            body, name="inproj_bwd", grid=(S // tm,), in_specs=in_specs, out_specs=out_specs, out_shape=out_shape,
            compiler_params=_params(("arbitrary",)))(*args)
        return dx, dng, []
    wrapped, extra = _carried(body, 7, 2, carry, gather=False)
    res = pl.pallas_call(
        wrapped, name="inproj_bwd_scatter", grid=(S // tm,), in_specs=in_specs + extra["in_specs"],
        out_specs=out_specs + extra["out_specs"], out_shape=out_shape + extra["out_shape"],
        scratch_shapes=extra["scratch_shapes"], compiler_params=_params(("arbitrary",)))(*args, *carry)
    return res[0], res[1], list(res[2:])


def _memkv_bwd(mem, mng, mkg, wmkv, wmkvt, dmk, dmv):
    M = mem.shape[0]

    def body(mem_ref, mng_ref, mkg_ref, w_ref, wt_ref, dmk_ref, dmv_ref, dw_ref, dmng_ref, dmkg_ref, d_s):
        r, mhat = _rms_parts(mem_ref[...])
        mn = (mhat * mng_ref[...]).astype(BF16)
        mkv = _dot(mn, w_ref[...])
        dmkg = jnp.zeros((1, MEM_HD), F32)
        for h in range(MEM_HEADS):
            sl = slice(h * MEM_HD, (h + 1) * MEM_HD)
            rk, khat = _rms_parts(mkv[:, 2 * MEM_HD * h:2 * MEM_HD * h + MEM_HD])
            dkn = dmk_ref[:, sl]
            dmkg = dmkg + _colsum(dkn * khat)
            d_s[:, 2 * MEM_HD * h:2 * MEM_HD * h + MEM_HD] = _rms_bwd(dkn * mkg_ref[...], khat, rk, MEM_HD).astype(BF16)
            d_s[:, 2 * MEM_HD * h + MEM_HD:2 * MEM_HD * (h + 1)] = dmv_ref[:, sl].astype(BF16)
        dmkg_ref[...] = dmkg
        dw_ref[...] = _dot_tn(mn, d_s[...])
        dmn = _dot(d_s[...], wt_ref[...])
        dmng_ref[...] = _colsum(dmn * mhat)

    return pl.pallas_call(
        body, name="memkv_bwd",
        out_shape=[jax.ShapeDtypeStruct((D_MODEL, 2 * MEM_W), F32), jax.ShapeDtypeStruct((1, D_MODEL), F32),
                   jax.ShapeDtypeStruct((1, MEM_HD), F32)],
        scratch_shapes=[pltpu.VMEM((M, 2 * MEM_W), BF16)],
        compiler_params=pltpu.CompilerParams(vmem_limit_bytes=VMEM_LIMIT_V7X),
    )(mem, mng, mkg, wmkv, wmkvt, dmk, dmv)


def _mm_tn(a, b, name, col0=0, ncols=None):
    S, M = a.shape
    N = b.shape[1] if ncols is None else ncols
    tm, tn, ts = _tile(M, 1024), _tile(N, 1024), _tile(S, 2048)
    assert col0 % tn == 0
    jb = col0 // tn

    def body(a_ref, b_ref, o_ref):
        @pl.when(pl.program_id(2) == 0)
        def _():
            o_ref[...] = jnp.zeros(o_ref.shape, F32)

        o_ref[...] += _dot_tn(a_ref[...].astype(BF16), b_ref[...].astype(BF16))

    return pl.pallas_call(
        body, name=name, grid=(M // tm, N // tn, S // ts),
        in_specs=[pl.BlockSpec((ts, tm), lambda i, j, k: (k, i)),
                  pl.BlockSpec((ts, tn), lambda i, j, k: (k, j + jb))],
        out_specs=pl.BlockSpec((tm, tn), lambda i, j, k: (i, j)),
        out_shape=jax.ShapeDtypeStruct((M, N), F32),
        compiler_params=_params(("parallel", "parallel", "arbitrary")),
    )(a, b)


def _adamw(w, g0, g1, m, v, name):
    R, C = w.shape
    tr = R
    for cand in (512, 256, 128, 64, 32, 16, 8):
        if R % cand == 0 and cand * C * 4 <= (1 << 20):
            tr = cand
            break
    c1 = 1.0 / (1.0 - ADAM_B1 ** ADAM_STEP)
    c2 = 1.0 / (1.0 - ADAM_B2 ** ADAM_STEP)

    def body(w_ref, g0_ref, g1_ref, m_ref, v_ref, g_ref, d_ref, nm_ref, nv_ref):
        g = g0_ref[...] + g1_ref[...]
        nm = ADAM_B1 * m_ref[...] + (1.0 - ADAM_B1) * g
        nv = ADAM_B2 * v_ref[...] + (1.0 - ADAM_B2) * (g * g)
        g_ref[...] = g
        nm_ref[...] = nm
        nv_ref[...] = nv
        d_ref[...] = -ADAM_LR * ((nm * c1) / (jnp.sqrt(nv * c2) + ADAM_EPS) + ADAM_WD * w_ref[...])

    blk = pl.BlockSpec((tr, C), lambda i: (i, 0))
    return pl.pallas_call(
        body, name=name, grid=(R // tr,),
        in_specs=[blk] * 5, out_specs=[blk] * 4,
        out_shape=[jax.ShapeDtypeStruct((R, C), F32)] * 4,
        compiler_params=_params(("parallel",)),
    )(w, g0, g1, m, v)


def _sum_slabs(a, name):
    K, R, C = a.shape
    tr = R
    for cand in (512, 256, 128, 64, 32, 16, 8):
        if R % cand == 0 and cand * C * 4 * K <= (4 << 20):
            tr = cand
            break

    def body(a_ref, o_ref):
        t = a_ref[0].astype(F32)
        for k in range(1, K):
            t = t + a_ref[k].astype(F32)
        o_ref[...] = t

    return pl.pallas_call(
        body, name=name, grid=(R // tr,),
        in_specs=[pl.BlockSpec((K, tr, C), lambda i: (0, i, 0))],
        out_specs=pl.BlockSpec((tr, C), lambda i: (i, 0)),
        out_shape=jax.ShapeDtypeStruct((R, C), F32),
        compiler_params=_params(("parallel",)),
    )(a)


def _gather_chips(arrs):
    n = len(arrs)
    halves = [a.shape[0] // 2 for a in arrs]
    assert all(a.shape[0] == 2 * hf for a, hf in zip(arrs, halves))

    def body(*refs):
        ins, outs = refs[:n], refs[n:2 * n]
        send1, recv1, send2, recv2, loc = refs[2 * n:]
        x, y, c = lax.axis_index("x"), lax.axis_index("y"), lax.axis_index("c")
        me = 2 * x + y
        mine = [pl.ds(c * hf, hf) for hf in halves]
        theirs = [pl.ds((1 - c) * hf, hf) for hf in halves]
        sibling = (x, y, 1 - c)
        waits = []
        for a in range(n):
            own = pltpu.make_async_copy(ins[a], outs[a].at[me], loc.at[a])
            own.start()
            waits.append(own.wait)

        def over_ici(a, k, src_chip):
            return pltpu.make_async_remote_copy(
                src_ref=ins[a].at[mine[a]], dst_ref=outs[a].at[src_chip, mine[a]], send_sem=send1.at[3 * a + k - 1],
                recv_sem=recv1.at[3 * a + k - 1], device_id=_chip_peer(x, y, c, k), device_id_type=MESH)

        def to_sibling(a, k, layers):
            block = outs[a].at[me ^ k, layers]
            return pltpu.make_async_remote_copy(
                src_ref=block, dst_ref=block, send_sem=send2.at[3 * a + k - 1], recv_sem=recv2.at[3 * a + k - 1],
                device_id=sibling, device_id_type=MESH)

        for a in range(n):
            for k in (1, 2, 3):
                cp = over_ici(a, k, me)
                cp.start()
                waits.append(cp.wait_send)
        for a in range(n):
            for k in (1, 2, 3):
                over_ici(a, k, me ^ k).wait_recv()
                cp = to_sibling(a, k, mine[a])
                cp.start()
                waits.append(cp.wait_send)
        for a in range(n):
            for k in (1, 2, 3):
                to_sibling(a, k, theirs[a]).wait_recv()
        for w in waits:
            w()

    return pl.pallas_call(
        body, name="gather_weights",
        in_specs=[HBM] * n, out_specs=[HBM] * n,
        out_shape=[jax.ShapeDtypeStruct((4,) + a.shape, a.dtype) for a in arrs],
        scratch_shapes=[pltpu.SemaphoreType.DMA((3 * n,)), pltpu.SemaphoreType.DMA((3 * n,)),
                        pltpu.SemaphoreType.DMA((3 * n,)), pltpu.SemaphoreType.DMA((3 * n,)),
                        pltpu.SemaphoreType.DMA((n,))],
    )(*arrs)


def _scatter_chips(arrs, small):
    n = len(arrs)

    def body(*refs):
        ins, small_in = refs[:n], refs[n]
        outs, small_out = refs[n + 1:2 * n + 1], refs[2 * n + 1]
        send, recv, loc, ssend, srecv = refs[2 * n + 2:]
        x, y, c = lax.axis_index("x"), lax.axis_index("y"), lax.axis_index("c")
        me = 2 * x + y
        me8 = 4 * x + 2 * y + c
        copies = []
        for a in range(n):
            own = pltpu.make_async_copy(ins[a].at[me], outs[a].at[me], loc.at[a])
            own.start()
            copies.append(own)
        own = pltpu.make_async_copy(small_in, small_out.at[me8], loc.at[n])
        own.start()
        copies.append(own)
        for k in range(1, 8):
            cp = pltpu.make_async_remote_copy(
                src_ref=small_in, dst_ref=small_out.at[me8], send_sem=ssend.at[k - 1], recv_sem=srecv.at[k - 1],
                device_id=(x ^ (k >> 2), y ^ ((k >> 1) & 1), c ^ (k & 1)), device_id_type=MESH)
            cp.start()
            copies.append(cp)
        for a in range(n):
            for k in (1, 2, 3):
                cp = pltpu.make_async_remote_copy(
                    src_ref=ins[a].at[me ^ k], dst_ref=outs[a].at[me], send_sem=send.at[3 * a + k - 1],
                    recv_sem=recv.at[3 * a + k - 1], device_id=_chip_peer(x, y, c, k), device_id_type=MESH)
                cp.start()
                copies.append(cp)
        for cp in copies:
            cp.wait()

    return pl.pallas_call(
        body, name="scatter_grads",
        in_specs=[HBM] * (n + 1), out_specs=[HBM] * (n + 1),
        out_shape=[jax.ShapeDtypeStruct(a.shape, a.dtype) for a in arrs]
        + [jax.ShapeDtypeStruct((8,) + small.shape, small.dtype)],
        scratch_shapes=[pltpu.SemaphoreType.DMA((3 * n,)), pltpu.SemaphoreType.DMA((3 * n,)),
                        pltpu.SemaphoreType.DMA((n + 1,)), pltpu.SemaphoreType.DMA((7,)),
                        pltpu.SemaphoreType.DMA((7,))],
    )(*arrs, small)


def _swap_cores(arrs):
    n = len(arrs)

    def body(*refs):
        ins, outs = refs[:n], refs[n:2 * n]
        send, recv = refs[2 * n:]
        x, y, c = lax.axis_index("x"), lax.axis_index("y"), lax.axis_index("c")
        copies = []
        for a in range(n):
            cp = pltpu.make_async_remote_copy(
                src_ref=ins[a], dst_ref=outs[a], send_sem=send.at[a], recv_sem=recv.at[a],
                device_id=(x, y, 1 - c), device_id_type=MESH)
            cp.start()
            copies.append(cp)
        for cp in copies:
            cp.wait()

    return pl.pallas_call(
        body, name="swap_cores",
        in_specs=[HBM] * n, out_specs=[HBM] * n,
        out_shape=[jax.ShapeDtypeStruct(a.shape, a.dtype) for a in arrs],
        scratch_shapes=[pltpu.SemaphoreType.DMA((n,)), pltpu.SemaphoreType.DMA((n,))],
    )(*arrs)


def _pad_last(a, n):
    return jnp.pad(a, [(0, 0)] * (a.ndim - 1) + [(0, n - a.shape[-1])])


def _pad_w_in(w):
    lead = w.shape[:-1]
    seg = lambda a, b: w[..., a:b]
    ga = _pad_last(seg(2720, 3232).reshape(lead + (N_HEADS, V_DIM)), HP).reshape(lead + (1024,))
    kpe = jnp.pad(seg(640, 672), [(0, 0)] * len(lead) + [(NOPE_DIM, HP - QK_DIM)])
    zero = jnp.zeros(lead + (PW - 7936,), w.dtype)
    return jnp.concatenate(
        [seg(4256, 7328), ga, seg(672, 1184), seg(2208, 2720), seg(3232, 3744), seg(3744, 4256),
         seg(1184, 1696), seg(1696, 2208), seg(0, 384), seg(384, 640), kpe, zero], axis=-1)


def _unpad_w_in(w):
    lead = w.shape[:-1]
    seg = lambda a, n: w[..., a:a + n]
    ga = seg(O_GA, 1024).reshape(lead + (N_HEADS, HP))[..., :V_DIM].reshape(lead + (N_HEADS * V_DIM,))
    return jnp.concatenate(
        [seg(O_QL, 384), seg(O_KVL, 256), seg(O_KPE + NOPE_DIM, ROPE_DIM), seg(O_CB, 512), seg(O_CC, 512),
         seg(O_CU, 512), seg(O_QM, 512), ga, seg(O_GC, 512), seg(O_GM, 512), seg(O_R, 3072)], axis=-1)


def _cols_from_shards(g):
    _, L, R, C = g.shape
    return jnp.transpose(g, (1, 2, 0, 3)).reshape(L, R, 4 * C)


def _t(w):
    return jnp.swapaxes(w, -1, -2)


def _layer_fwd(x, mem, tabs, p, next_shards=()):
    proj, xn, gathered = _inproj(x, p["norm_g"], p["w_in"], next_shards)
    q, k, v, qn, kvn, score_bound, qt, vt = _mla_prep(proj, tabs, p["q_norm_g"], p["kv_norm_g"], p["q_head_g"], p["k_head_g"],
                                        p["w_uq"], p["w_uk"], p["w_uv"])
    o, lse = _attn_fwd(qt, k, vt, score_bound)
    mk, mv = _memkv(mem, p["mem_norm_g"], p["mem_k_g"], p["w_mkv"])
    x_new, oa, oc, om, u, y = _merge(proj, o, x, p["b_gate"], p["conv_wb"], p["mem_q_g"], mk, mv,
                                     p["w_br_attn"], p["w_br_conv"], p["w_br_mem"], p["w_out"])
    saved = dict(x=x, proj=proj, xn=xn, q=q, qt=qt, k=k, v=v, qn=qn, kvn=kvn, o=o, lse=lse, mk=mk, mv=mv,
                 oa=oa, oc=oc, om=om, u=u, y=y)
    return x_new, saved, gathered


def _layer_bwd(g, mem, tabs, p, s, to_owner=()):
    S = g.shape[0]
    dmain, dcv, d_o, delta, du, dbg, dmk, dmv, dmqg, d_ot = _merge_bwd(
        g, s["proj"], s["o"], s["u"], p["b_gate"], p["conv_wb"], p["mem_q_g"], s["mk"], s["mv"],
        p["w_out_t"], p["w_br_attn_t"], p["w_br_conv_t"], p["w_br_mem_t"])
    dccu, dconv = _conv_bwd(dcv, s["proj"], p["conv_wb"])
    dq, dkt, dvt = _attn_bwd(s["q"], s["k"], s["v"], d_o, s["qt"], d_ot, s["lse"], delta)
    dsega, dqraw, dkraw, dqg, dkvg, dqhg, dkhg, dv = _mla_prep_bwd(
        s["proj"], tabs, dq, dkt, dvt, p["q_norm_g"], p["kv_norm_g"], p["q_head_g"], p["k_head_g"],
        p["w_uq"], p["w_uk"], p["w_uq_t"], p["w_uk_t"], p["w_uv_t"])
    dx, dng, received = _inproj_bwd(dmain, dccu, dsega, p["w_in_t"], s["x"], g, p["norm_g"], to_owner)
    dwmkv, dmng, dmkg = _memkv_bwd(mem, p["mem_norm_g"], p["mem_k_g"], p["w_mkv"], p["w_mkv_t"], dmk, dmv)
    grads = dict(
        norm_g=dng, b_gate=dbg, q_norm_g=dqg, kv_norm_g=dkvg, q_head_g=dqhg, k_head_g=dkhg,
        conv_wb=dconv, mem_norm_g=dmng, mem_q_g=dmqg, mem_k_g=dmkg, w_mkv=dwmkv,
        w_in=jnp.concatenate([_mm_tn(s["xn"], dmain, "grad_w_in"), _mm_tn(s["xn"], dccu, "grad_w_in_conv"),
                              _mm_tn(s["xn"], dsega, "grad_w_in_lat")], axis=1),
        w_uq=_mm_tn(s["qn"], dqraw, "grad_w_uq"),
        w_uk=_mm_tn(s["kvn"], dkraw, "grad_w_uk"),
        w_uv=_mm_tn(s["kvn"], dv, "grad_w_uv"),
        w_br_attn=_mm_tn(s["oa"], du, "grad_w_br_attn", 0, 1024),
        w_br_conv=_mm_tn(s["oc"], du, "grad_w_br_conv", 1024, 1024),
        w_br_mem=_mm_tn(s["om"], du, "grad_w_br_mem", 2048, 1024),
        w_out=_mm_tn(s["y"], g, "grad_w_out"),
    )
    return dx, grads, received


def _layer_params(big, full, l):
    p = {}
    w_in = _pad_w_in(big["w_in"])
    w_uq = _pad_last(big["w_uq"].reshape(Q_RANK, N_HEADS, QK_DIM), HP).reshape(Q_RANK, 1024)
    ukv = big["w_ukv"].reshape(KV_RANK, N_HEADS, NOPE_DIM + V_DIM)
    w_uk = _pad_last(ukv[..., :NOPE_DIM], HP).reshape(KV_RANK, 1024)
    w_uv = _pad_last(ukv[..., NOPE_DIM:], HP).reshape(KV_RANK, 1024)
    w_ba = jnp.pad(big["w_br_attn"].reshape(N_HEADS, V_DIM, D_MODEL), ((0, 0), (0, HP - V_DIM), (0, 0)))
    w_ba = w_ba.reshape(1024, D_MODEL)
    p.update(w_in=w_in, w_uq=w_uq, w_uk=w_uk, w_uv=w_uv, w_br_attn=w_ba, w_br_conv=big["w_br_conv"],
             w_br_mem=big["w_br_mem"], w_out=big["w_out"], w_mkv=big["w_mkv"])
    for n in ("w_in", "w_uq", "w_uk", "w_uv", "w_br_attn", "w_br_conv", "w_br_mem", "w_out", "w_mkv"):
        p[n + "_t"] = _t(p[n])
    for n in ("norm_g", "b_gate", "q_norm_g", "kv_norm_g", "mem_norm_g", "mem_q_g", "mem_k_g"):
        p[n] = full[n][l][None, :]
    p["q_head_g"] = _pad_last(full["q_head_g"][l][None, :], HP)
    p["k_head_g"] = _pad_last(full["k_head_g"][l][None, :], HP)
    p["conv_wb"] = jnp.concatenate(
        [full["conv_w"][l], full["conv_b"][l][None, :], jnp.zeros((4, CONV_W), F32)], axis=0)
    return p


def _join_shards(name, g):
    _, R, C = g.shape
    if name in _COL_SHARDED:
        return jnp.transpose(g, (1, 0, 2)).reshape(R, 4 * C)
    return g.reshape(4 * R, C)


def _split_shards(name, w):
    R, C = w.shape
    if name in _COL_SHARDED:
        return jnp.transpose(w.reshape(R, 4, C // 4), (1, 0, 2)).astype(BF16)
    return w.reshape(4, R // 4, C).astype(BF16)


def _unpad_grads(gp):
    out = {"w_in": _unpad_w_in(gp["w_in"])}
    out["w_uq"] = gp["w_uq"].reshape(Q_RANK, N_HEADS, HP)[..., :QK_DIM].reshape(Q_RANK, N_HEADS * QK_DIM)
    duk = gp["w_uk"].reshape(KV_RANK, N_HEADS, HP)[..., :NOPE_DIM]
    duv = gp["w_uv"].reshape(KV_RANK, N_HEADS, HP)[..., :V_DIM]
    out["w_ukv"] = jnp.concatenate([duk, duv], axis=-1).reshape(KV_RANK, 1024)
    out["w_br_attn"] = gp["w_br_attn"].reshape(N_HEADS, HP, D_MODEL)[:, :V_DIM].reshape(512, D_MODEL)
    for n in ("w_br_conv", "w_br_mem", "w_out", "w_mkv"):
        out[n] = gp[n]
    return out


def _train_step(x, mem, positions, w, target):
    S = x.shape[0]
    invf16 = ROPE_BASE ** (-jnp.arange(0, ROPE_DIM, 2, dtype=F32) / ROPE_DIM)
    invf = jnp.concatenate([jnp.zeros((NOPE_DIM,), F32), invf16, invf16, jnp.zeros((HP - QK_DIM,), F32)])[None, :]
    tabs = _rope_tables(jnp.broadcast_to(positions.reshape(S, 1), (S, HP)), invf)
    shards = [[w[n][l].astype(BF16) for n in _BIG] for l in range(DEPTH)]
    first = _gather_chips(shards[0] + [w["conv_w"]])
    gathered = first[:-1]
    full = {n: w[n] for n in _SMALL}
    full["conv_w"] = _cols_from_shards(first[-1])
    params, saved = [], []
    h = x
    for l in range(DEPTH):
        big = {n: _join_shards(n, g) for n, g in zip(_BIG, gathered)}
        params.append(_layer_params(big, full, l))
        h, s, gathered = _layer_fwd(h, mem, tabs, params[l], shards[l + 1] if l + 1 < DEPTH else ())
        saved.append(s)
    g, loss_part = _loss_head(h, target)
    per_layer, received = [None] * DEPTH, [None] * DEPTH
    to_owner = ()
    for l in reversed(range(DEPTH)):
        g, per_layer[l], got = _layer_bwd(g, mem, tabs, params[l], saved[l], to_owner)
        if to_owner:
            received[l + 1] = got
        big_g = _unpad_grads(per_layer[l])
        to_owner = [_split_shards(n, big_g[n]) for n in _BIG]
    st = lambda n: jnp.stack([per_layer[l][n] for l in range(DEPTH)])
    small = {}
    for n in ("norm_g", "b_gate", "q_norm_g", "kv_norm_g", "mem_norm_g", "mem_q_g", "mem_k_g"):
        small[n] = st(n)[:, 0, :]
    small["q_head_g"] = st("q_head_g")[:, 0, :QK_DIM]
    small["k_head_g"] = st("k_head_g")[:, 0, :QK_DIM]
    cwb = st("conv_wb")
    small["conv_w"] = cwb[:, 0:3, :]
    small["conv_b"] = cwb[:, 3, :]
    return loss_part, g, received, to_owner, small


_COL_SHARDED = ("w_in", "w_uq", "w_ukv", "w_br_attn", "w_br_conv", "w_br_mem")
_ROW_SHARDED = ("w_mkv", "w_out")
_BIG = _COL_SHARDED + _ROW_SHARDED
_SMALL = ("norm_g", "b_gate", "q_norm_g", "kv_norm_g", "q_head_g", "k_head_g", "conv_w", "conv_b",
          "mem_norm_g", "mem_q_g", "mem_k_g")
_ORDER = ("norm_g", "w_in", "b_gate", "q_norm_g", "w_uq", "kv_norm_g", "w_ukv", "q_head_g", "k_head_g",
          "conv_w", "conv_b", "mem_norm_g", "w_mkv", "mem_q_g", "mem_k_g", "w_br_attn", "w_br_conv",
          "w_br_mem", "w_out")


def _pack_small(d, extra):
    flat = jnp.concatenate([d[n].reshape(-1) for n in _SMALL] + [extra.reshape(-1)])
    n = flat.shape[0]
    rows = -(-n // 1024) * 8
    return jnp.pad(flat, (0, rows * 128 - n)).reshape(rows, 128)


def _unpack_small(packed, like):
    flat = packed.reshape(-1)
    out, off = {}, 0
    for n in _SMALL:
        sz = int(np.prod(like[n].shape))
        out[n] = flat[off:off + sz].reshape(like[n].shape)
        off += sz
    return out, flat[off:]


def kernel(x, mem, positions, norm_g, w_in, b_gate, q_norm_g, w_uq, kv_norm_g, w_ukv, q_head_g, k_head_g, conv_w, conv_b, mem_norm_g, w_mkv, mem_q_g, mem_k_g, w_br_attn, w_br_conv, w_br_mem, w_out, loss_target, m_norm_g, m_w_in, m_b_gate, m_q_norm_g, m_w_uq, m_kv_norm_g, m_w_ukv, m_q_head_g, m_k_head_g, m_conv_w, m_conv_b, m_mem_norm_g, m_w_mkv, m_mem_q_g, m_mem_k_g, m_w_br_attn, m_w_br_conv, m_w_br_mem, m_w_out, v_norm_g, v_w_in, v_b_gate, v_q_norm_g, v_w_uq, v_kv_norm_g, v_w_ukv, v_q_head_g, v_k_head_g, v_conv_w, v_conv_b, v_mem_norm_g, v_w_mkv, v_mem_q_g, v_mem_k_g, v_w_br_attn, v_w_br_conv, v_w_br_mem, v_w_out):
    w = dict(norm_g=norm_g, w_in=w_in, b_gate=b_gate, q_norm_g=q_norm_g, w_uq=w_uq, kv_norm_g=kv_norm_g,
             w_ukv=w_ukv, q_head_g=q_head_g, k_head_g=k_head_g, conv_w=conv_w, conv_b=conv_b,
             mem_norm_g=mem_norm_g, w_mkv=w_mkv, mem_q_g=mem_q_g, mem_k_g=mem_k_g, w_br_attn=w_br_attn,
             w_br_conv=w_br_conv, w_br_mem=w_br_mem, w_out=w_out)
    m = dict(norm_g=m_norm_g, w_in=m_w_in, b_gate=m_b_gate, q_norm_g=m_q_norm_g, w_uq=m_w_uq,
             kv_norm_g=m_kv_norm_g, w_ukv=m_w_ukv, q_head_g=m_q_head_g, k_head_g=m_k_head_g, conv_w=m_conv_w,
             conv_b=m_conv_b, mem_norm_g=m_mem_norm_g, w_mkv=m_w_mkv, mem_q_g=m_mem_q_g, mem_k_g=m_mem_k_g,
             w_br_attn=m_w_br_attn, w_br_conv=m_w_br_conv, w_br_mem=m_w_br_mem, w_out=m_w_out)
    v = dict(norm_g=v_norm_g, w_in=v_w_in, b_gate=v_b_gate, q_norm_g=v_q_norm_g, w_uq=v_w_uq,
             kv_norm_g=v_kv_norm_g, w_ukv=v_w_ukv, q_head_g=v_q_head_g, k_head_g=v_k_head_g, conv_w=v_conv_w,
             conv_b=v_conv_b, mem_norm_g=v_mem_norm_g, w_mkv=v_w_mkv, mem_q_g=v_mem_q_g, mem_k_g=v_mem_k_g,
             w_br_attn=v_w_br_attn, w_br_conv=v_w_br_conv, w_br_mem=v_w_br_mem, w_out=v_w_out)
    chip = 2 * lax.axis_index("x") + lax.axis_index("y")

    loss_part, grad_x, received, last_slabs, grads = _train_step(x[0], mem[0], positions[0], w, loss_target[0])

    loss_vec = jnp.zeros((128,), F32).at[0].set(0.5 / D_MODEL * jnp.sum(loss_part))
    small = _pack_small(grads, loss_vec)
    scattered = _scatter_chips(last_slabs, small)
    received[0] = scattered[:-1]
    small_sum = _sum_slabs(scattered[-1], "sum_small")
    partial = []
    for a, n in enumerate(_BIG):
        partial.append(jnp.concatenate([_sum_slabs(received[l][a], "sum_" + n) for l in range(DEPTH)], axis=0))
    other = _swap_cores(partial)

    small_g, tail = _unpack_small(small_sum, {n: (grads[n]) for n in _SMALL})
    loss = tail[0]
    small_g["conv_w"] = lax.dynamic_slice_in_dim(small_g["conv_w"], chip * (CONV_W // 4), CONV_W // 4, axis=2)

    outs_g, outs_d, outs_m, outs_v = {}, {}, {}, {}
    for n, p0, p1 in zip(_BIG, partial, other):
        shape = w[n].shape
        flat = lambda t: t.reshape(p0.shape)
        g_, d_, m_, v_ = _adamw(flat(w[n]), p0, p1, flat(m[n]), flat(v[n]), "adamw_" + n)
        outs_g[n], outs_d[n], outs_m[n], outs_v[n] = (t.reshape(shape) for t in (g_, d_, m_, v_))
    zero_small = jnp.zeros_like(small_sum)
    pk = lambda d: _pack_small(d, jnp.zeros((128,), F32))
    g_, d_, m_, v_ = _adamw(pk(w), _pack_small(small_g, jnp.zeros((128,), F32)), zero_small, pk(m), pk(v),
                            "adamw_small")
    like = {n: w[n] for n in _SMALL}
    for dst, packed in ((outs_g, g_), (outs_d, d_), (outs_m, m_), (outs_v, v_)):
        dst.update(_unpack_small(packed, like)[0])

    return (loss, grad_x[None], *[outs_g[n] for n in _ORDER], *[outs_d[n] for n in _ORDER],
            *[outs_m[n] for n in _ORDER], *[outs_v[n] for n in _ORDER])
```

```python
import functools

import numpy as np
import jax
import jax.numpy as jnp
from jax import lax
from jax.experimental import pallas as pl
from jax.experimental.pallas import tpu as pltpu

F32 = jnp.float32
BF16 = jnp.bfloat16

D_MODEL = 1024
DEPTH = 4
N_HEADS = 8
QK_DIM = 96
NOPE_DIM = 64
ROPE_DIM = 32
V_DIM = 64
Q_RANK = 384
KV_RANK = 256
CONV_W = 512
MEM_HEADS = 4
MEM_HD = 128
MEM_W = 512
IN_WIDTH = 7328
PW = 8192
HP = 128
PROJ_HALO = 16
F32_HALO = 8
EPS = 1e-6
ROPE_BASE = 10000.0
SCALE = QK_DIM ** -0.5
MEM_SCALE = MEM_HD ** -0.5

ADAM_LR = 0.001
ADAM_B1 = 0.9
ADAM_B2 = 0.999
ADAM_EPS = 1e-08
ADAM_WD = 0.01
ADAM_STEP = 10

VMEM_LIMIT_V7X = 56 * 1024 * 1024

O_R, O_GA, O_CB, O_QM, O_GC, O_GM, O_CC, O_CU, O_QL, O_KVL, O_KPE = (
    0, 3072, 4096, 4608, 5120, 5632, 6144, 6656, 7168, 7552, 7808)


def _params(sem, vmem=VMEM_LIMIT_V7X):
    return pltpu.CompilerParams(dimension_semantics=sem, vmem_limit_bytes=vmem)


def _sigmoid(t):
    return 0.5 * jnp.tanh(0.5 * t) + 0.5


def _silu_and_grad(g):
    sg = _sigmoid(g)
    return g * sg, sg * (1.0 + g * (1.0 - sg))


def _rms(t, g, n=None):
    n = t.shape[-1] if n is None else n
    r = lax.rsqrt(jnp.sum(t * t, axis=-1, keepdims=True) * (1.0 / n) + EPS)
    return (t * r) * g


def _rms_parts(t, n=None):
    n = t.shape[-1] if n is None else n
    r = lax.rsqrt(jnp.sum(t * t, axis=-1, keepdims=True) * (1.0 / n) + EPS)
    return r, t * r


def _rms_bwd(dhat, hat, r, n):
    return r * (dhat - hat * (jnp.sum(dhat * hat, axis=-1, keepdims=True) * (1.0 / n)))


def _rope(t, c, sa, sb):
    return t * c + pltpu.roll(t, HP - 16, 1) * sa + pltpu.roll(t, 16, 1) * sb


def _rope_t(d, c, sa, sb):
    return d * c + pltpu.roll(d * sa, 16, 1) + pltpu.roll(d * sb, HP - 16, 1)


def _dot(a, b):
    return jnp.dot(a, b, preferred_element_type=F32)


def _dot_nt(a, b):
    return lax.dot_general(a, b, (((1,), (1,)), ((), ())), preferred_element_type=F32)


def _dot_tn(a, b):
    return lax.dot_general(a, b, (((0,), (0,)), ((), ())), preferred_element_type=F32)


def _colsum(t):
    return jnp.sum(t, axis=0, keepdims=True)


def _tile(n, t):
    t = min(n, t)
    assert n % t == 0, (n, t)
    return t


def _rope_tables(pos_b, invf):
    S = pos_b.shape[0]
    tm = _tile(S, 1024)

    def body(pos_ref, invf_ref, c_ref, sa_ref, sb_ref):
        ang = pos_ref[...].astype(F32) * invf_ref[...]
        lane = lax.broadcasted_iota(jnp.int32, ang.shape, 1)
        cs = jnp.cos(ang)
        sn = jnp.sin(ang)
        c_ref[...] = jnp.where(lane < NOPE_DIM, 1.0, jnp.where(lane < QK_DIM, cs, 0.0))
        sa_ref[...] = jnp.where((lane >= NOPE_DIM) & (lane < NOPE_DIM + 16), -sn, 0.0)
        sb_ref[...] = jnp.where((lane >= NOPE_DIM + 16) & (lane < QK_DIM), sn, 0.0)

    blk = pl.BlockSpec((tm, HP), lambda i: (i, 0))
    return pl.pallas_call(
        body, name="rope_tables", grid=(S // tm,),
        in_specs=[blk, pl.BlockSpec((1, HP), lambda i: (0, 0))],
        out_specs=[blk, blk, blk],
        out_shape=[jax.ShapeDtypeStruct((S, HP), F32)] * 3,
        compiler_params=_params(("parallel",)),
    )(pos_b, invf)


MESH = pl.DeviceIdType.MESH
HBM = pl.BlockSpec(memory_space=pltpu.HBM)


def _chip_peer(x, y, c, k):
    return (x ^ (k >> 1), y ^ (k & 1), c)


def _chip_copies(ins, outs, send, recv, loc, gather):
    x, y, c = lax.axis_index("x"), lax.axis_index("y"), lax.axis_index("c")
    me = 2 * x + y
    cps = []
    for a in range(len(ins)):
        cps.append(pltpu.make_async_copy(ins[a] if gather else ins[a].at[me], outs[a].at[me], loc.at[a]))
        for k in (1, 2, 3):
            cps.append(pltpu.make_async_remote_copy(
                src_ref=ins[a] if gather else ins[a].at[me ^ k], dst_ref=outs[a].at[me],
                send_sem=send.at[3 * a + k - 1], recv_sem=recv.at[3 * a + k - 1],
                device_id=_chip_peer(x, y, c, k), device_id_type=MESH))
    return cps


def _carried(body, n_in, n_out, carry, gather):
    n = len(carry)

    def wrapped(*refs):
        ins, cin = refs[:n_in], refs[n_in:n_in + n]
        outs, cout = refs[n_in + n:n_in + n + n_out], refs[n_in + n + n_out:n_in + 2 * n + n_out]
        send, recv, loc = refs[n_in + 2 * n + n_out:]
        i = pl.program_id(0)

        @pl.when(i == 0)
        def _():
            for cp in _chip_copies(cin, cout, send, recv, loc, gather):
                cp.start()

        body(*ins, *outs)

        @pl.when(i == pl.num_programs(0) - 1)
        def _():
            for cp in _chip_copies(cin, cout, send, recv, loc, gather):
                cp.wait()

    specs = dict(
        in_specs=[HBM] * n, out_specs=[HBM] * n,
        out_shape=[jax.ShapeDtypeStruct(((4,) + a.shape) if gather else a.shape, a.dtype) for a in carry],
        scratch_shapes=[pltpu.SemaphoreType.DMA((3 * n,)), pltpu.SemaphoreType.DMA((3 * n,)),
                        pltpu.SemaphoreType.DMA((n,))])
    return wrapped, specs


def _inproj(x, g, w, carry=()):
    S = x.shape[0]
    tm = _tile(S, 256)

    def body(x_ref, g_ref, w_ref, proj_ref, xn_ref):
        h = _rms(x_ref[...], g_ref[...]).astype(BF16)
        xn_ref[...] = h
        proj_ref[...] = _dot(h, w_ref[...]).astype(BF16)

    row = lambda n: pl.BlockSpec((tm, n), lambda i: (i, 0))
    in_specs = [row(D_MODEL), pl.BlockSpec((1, D_MODEL), lambda i: (0, 0)), pl.BlockSpec((D_MODEL, PW), lambda i: (0, 0))]
    out_specs = [row(PW), row(D_MODEL)]
    out_shape = [jax.ShapeDtypeStruct((S, PW), BF16), jax.ShapeDtypeStruct((S, D_MODEL), BF16)]
    if not carry:
        proj, xn = pl.pallas_call(
            body, name="inproj", grid=(S // tm,), in_specs=in_specs, out_specs=out_specs, out_shape=out_shape,
            compiler_params=_params(("parallel",)))(x, g, w)
        return proj, xn, []
    wrapped, extra = _carried(body, 3, 2, carry, gather=True)
    res = pl.pallas_call(
        wrapped, name="inproj_gather", grid=(S // tm,), in_specs=in_specs + extra["in_specs"],
        out_specs=out_specs + extra["out_specs"], out_shape=out_shape + extra["out_shape"],
        scratch_shapes=extra["scratch_shapes"], compiler_params=_params(("arbitrary",)))(x, g, w, *carry)
    return res[0], res[1], list(res[2:])


def _mla_prep(proj, tabs, qg, kvg, qhg, khg, wuq, wuk, wuv):
    S = proj.shape[0]
    tm = _tile(S, 512)

    def body(a_ref, c_ref, sa_ref, sb_ref, qg_ref, kvg_ref, qhg_ref, khg_ref, wuq_ref, wuk_ref, wuv_ref,
             q_ref, k_ref, v_ref, qn_ref, kvn_ref, bound_ref, qt_ref, vt_ref):
        gq = jnp.max(jnp.abs(qhg_ref[...]), axis=-1, keepdims=True)
        gk = jnp.max(jnp.abs(khg_ref[...]), axis=-1, keepdims=True)
        bound_ref[...] = jnp.broadcast_to(gq * gk * (QK_DIM ** 0.5 * 1.01) + 1e-6, bound_ref.shape)
        ql = a_ref[:, 0:Q_RANK].astype(F32)
        kvl = a_ref[:, Q_RANK:Q_RANK + KV_RANK].astype(F32)
        kpe = a_ref[:, Q_RANK + KV_RANK:Q_RANK + KV_RANK + HP].astype(F32)
        qn = _rms(ql, qg_ref[...]).astype(BF16)
        kvn = _rms(kvl, kvg_ref[...]).astype(BF16)
        qn_ref[...] = qn
        kvn_ref[...] = kvn
        qraw = _dot(qn, wuq_ref[...])
        kn = _dot(kvn, wuk_ref[...])
        vf = _dot(kvn, wuv_ref[...])
        v_ref[...] = vf.astype(BF16)
        c, sa, sb = c_ref[...], sa_ref[...], sb_ref[...]
        for h in range(N_HEADS):
            sl = slice(h * HP, (h + 1) * HP)
            tq = _rms(qraw[:, sl], qhg_ref[...], QK_DIM)
            qh = _rope(tq, c, sa, sb) * SCALE
            q_ref[:, sl] = qh.astype(BF16)
            qt_ref[sl, :] = qh.T.astype(BF16)
            vt_ref[sl, :] = vf[:, sl].T.astype(BF16)
            tk = _rms(kn[:, sl] + kpe, khg_ref[...], QK_DIM)
            k_ref[:, sl] = _rope(tk, c, sa, sb).astype(BF16)

    row = lambda w: pl.BlockSpec((tm, w), lambda i: (i, 0))
    full = lambda a: pl.BlockSpec(a.shape, lambda i: (0,) * a.ndim)
    return pl.pallas_call(
        body, name="mla_prep", grid=(S // tm,),
        in_specs=[pl.BlockSpec((tm, 1024), lambda i: (i, 7)), row(HP), row(HP), row(HP),
                  full(qg), full(kvg), full(qhg), full(khg), full(wuq), full(wuk), full(wuv)],
        out_specs=[row(1024), row(1024), row(1024), row(Q_RANK), row(KV_RANK),
                   pl.BlockSpec((1, HP), lambda i: (0, 0)),
                   pl.BlockSpec((1024, tm), lambda i: (0, i)), pl.BlockSpec((1024, tm), lambda i: (0, i))],
        out_shape=[jax.ShapeDtypeStruct((S, 1024), BF16)] * 3
        + [jax.ShapeDtypeStruct((S, Q_RANK), BF16), jax.ShapeDtypeStruct((S, KV_RANK), BF16),
           jax.ShapeDtypeStruct((1, HP), F32)] + [jax.ShapeDtypeStruct((1024, S), BF16)] * 2,
        compiler_params=_params(("arbitrary",)),
    )(proj, *tabs, qg, kvg, qhg, khg, wuq, wuk, wuv)


SAFE_SCORE_BOUND = 30.0


def _attn_fwd(qt, k, vt, score_bound):
    S = k.shape[0]
    tq, tk = _tile(S, 1024), _tile(S, 4096)
    nk = S // tk

    def body(qt_ref, k_ref, vt_ref, bound_ref, o_ref, lse_ref, m_s, l_s, acc_s):
        qtv = qt_ref[...]
        bound = bound_ref[0:1, 0:1]
        safe = jnp.max(bound) <= SAFE_SCORE_BOUND
        l_s[...] = jnp.zeros(l_s.shape, F32)
        acc_s[...] = jnp.zeros(acc_s.shape, F32)

        def keys(c):
            return pl.ds(pl.multiple_of(c * tk, tk), tk)

        @pl.when(safe)
        def _():
            def step(c, carry):
                pt = jnp.exp(_dot(k_ref[keys(c), :], qtv) - bound)
                l_s[...] += jnp.sum(pt, axis=0, keepdims=True)
                acc_s[...] += _dot(vt_ref[:, keys(c)], pt.astype(BF16))
                return carry

            lax.fori_loop(0, nk, step, 0)
            m_s[...] = jnp.broadcast_to(bound, m_s.shape)

        @pl.when(jnp.logical_not(safe))
        def _():
            m_s[...] = jnp.full(m_s.shape, -jnp.inf, F32)

            def step(c, carry):
                st = _dot(k_ref[keys(c), :], qtv)
                m_prev = m_s[...]
                m_new = jnp.maximum(m_prev, jnp.max(st, axis=0, keepdims=True))
                alpha = jnp.exp(m_prev - m_new)
                pt = jnp.exp(st - m_new)
                l_s[...] = alpha * l_s[...] + jnp.sum(pt, axis=0, keepdims=True)
                acc_s[...] = alpha * acc_s[...] + _dot(vt_ref[:, keys(c)], pt.astype(BF16))
                m_s[...] = m_new
                return carry

            lax.fori_loop(0, nk, step, 0)

        o_ref[...] = (acc_s[...] / l_s[...]).T
        lse_row = m_s[...] + jnp.log(l_s[...])
        lse_ref[0] = jnp.broadcast_to(lse_row, (HP, tq)).T[:, 0:1]

    return pl.pallas_call(
        body, name="attn_fwd", grid=(N_HEADS, S // tq),
        in_specs=[pl.BlockSpec((HP, tq), lambda h, i: (h, i)),
                  pl.BlockSpec((S, HP), lambda h, i: (0, h)),
                  pl.BlockSpec((HP, S), lambda h, i: (h, 0)),
                  pl.BlockSpec((1, HP), lambda h, i: (0, 0))],
        out_specs=[pl.BlockSpec((tq, HP), lambda h, i: (i, h)),
                   pl.BlockSpec((1, tq, 1), lambda h, i: (h, i, 0))],
        out_shape=[jax.ShapeDtypeStruct((S, N_HEADS * HP), F32),
                   jax.ShapeDtypeStruct((N_HEADS, S, 1), F32)],
        scratch_shapes=[pltpu.VMEM((1, tq), F32), pltpu.VMEM((1, tq), F32), pltpu.VMEM((HP, tq), F32)],
        compiler_params=_params(("parallel", "parallel")),
    )(qt, k, vt, score_bound)


def _memkv(mem, mng, mkg, wmkv):
    M = mem.shape[0]

    def body(mem_ref, mng_ref, mkg_ref, w_ref, mk_ref, mv_ref):
        mn = _rms(mem_ref[...], mng_ref[...]).astype(BF16)
        mkv = _dot(mn, w_ref[...])
        for h in range(MEM_HEADS):
            kraw = mkv[:, 2 * MEM_HD * h:2 * MEM_HD * h + MEM_HD]
            mk_ref[:, MEM_HD * h:MEM_HD * (h + 1)] = _rms(kraw, mkg_ref[...]).astype(BF16)
            mv_ref[:, MEM_HD * h:MEM_HD * (h + 1)] = mkv[:, 2 * MEM_HD * h + MEM_HD:2 * MEM_HD * (h + 1)].astype(BF16)

    return pl.pallas_call(
        body, name="memkv",
        out_shape=[jax.ShapeDtypeStruct((M, MEM_W), BF16)] * 2,
        compiler_params=pltpu.CompilerParams(vmem_limit_bytes=VMEM_LIMIT_V7X),
    )(mem, mng, mkg, wmkv)


def _conv_shifts(cc, cu, hp_ref, hn_ref, i, n_tiles, tm):
    z = cc * cu
    last = PROJ_HALO - 1
    zp = hp_ref[last:last + 1, 0:CONV_W].astype(F32) * hp_ref[last:last + 1, CONV_W:2 * CONV_W].astype(F32)
    zn = hn_ref[0:1, 0:CONV_W].astype(F32) * hn_ref[0:1, CONV_W:2 * CONV_W].astype(F32)
    zp = jnp.where(i == 0, 0.0, zp)
    zn = jnp.where(i == n_tiles - 1, 0.0, zn)
    row = lax.broadcasted_iota(jnp.int32, z.shape, 0)
    z_up = jnp.where(row == 0, zp, pltpu.roll(z, 1, 0))
    z_dn = jnp.where(row == tm - 1, zn, pltpu.roll(z, tm - 1, 0))
    return z, z_up, z_dn


def _halo_specs(tm, S, width, col, rows):
    per = tm // rows
    prev = pl.BlockSpec((rows, width), lambda i: (jnp.maximum(i * per - 1, 0), col))
    nxt = pl.BlockSpec((rows, width), lambda i: (jnp.minimum((i + 1) * per, S // rows - 1), col))
    return prev, nxt


def _mem_attend(qm, mqg, mk_h, mv_h):
    r, qhat = _rms_parts(qm)
    mq = (qhat * mqg).astype(BF16)
    s = _dot_nt(mq, mk_h) * MEM_SCALE
    e = jnp.exp(s - jnp.max(s, axis=-1, keepdims=True))
    p = e / jnp.sum(e, axis=-1, keepdims=True)
    pv = _dot(p.astype(BF16), mv_h)
    return r, qhat, mq, p, pv


def _merge(proj, o, x, bg, convw, mqg, mk, mv, wba, wbc, wbm, wo):
    S = x.shape[0]
    tm = _tile(S, 512)
    nt = S // tm

    def body(main_ref, ccu_ref, hp_ref, hn_ref, o_ref, x_ref, bg_ref, cw_ref, mqg_ref, mk_ref, mv_ref,
             wba_ref, wbc_ref, wbm_ref, wo_ref, xn_ref, oa_ref, oc_ref, om_ref, u_ref, y_ref):
        i = pl.program_id(0)
        sil_a, _ = _silu_and_grad(main_ref[:, O_GA:O_GA + 1024].astype(F32))
        oa = (o_ref[...] * sil_a).astype(BF16)
        oa_ref[...] = oa
        z, z_up, z_dn = _conv_shifts(ccu_ref[:, 0:CONV_W].astype(F32), ccu_ref[:, CONV_W:].astype(F32), hp_ref, hn_ref, i, nt, tm)
        cv = cw_ref[0:1, :] * z_up + cw_ref[1:2, :] * z + cw_ref[2:3, :] * z_dn + cw_ref[3:4, :]
        sil_c, _ = _silu_and_grad(main_ref[:, O_GC:O_GC + CONV_W].astype(F32))
        oc = (main_ref[:, O_CB:O_CB + CONV_W].astype(F32) * cv * sil_c).astype(BF16)
        oc_ref[...] = oc
        sil_m, _ = _silu_and_grad(main_ref[:, O_GM:O_GM + MEM_W].astype(F32))
        for h in range(MEM_HEADS):
            sl = slice(h * MEM_HD, (h + 1) * MEM_HD)
            qm = main_ref[:, O_QM + h * MEM_HD:O_QM + (h + 1) * MEM_HD].astype(F32)
            pv = _mem_attend(qm, mqg_ref[...], mk_ref[:, sl], mv_ref[:, sl])[4]
            om_ref[:, sl] = (pv * sil_m[:, sl]).astype(BF16)
        ua = _dot(oa, wba_ref[...])
        uc = _dot(oc, wbc_ref[...])
        um = _dot(om_ref[...], wbm_ref[...])
        u_ref[:, 0:1024] = ua.astype(BF16)
        u_ref[:, 1024:2048] = uc.astype(BF16)
        u_ref[:, 2048:3072] = um.astype(BF16)
        rg = _sigmoid(main_ref[:, O_R:O_R + 3072].astype(F32) + bg_ref[...])
        y = (rg[:, 0:1024] * ua + rg[:, 1024:2048] * uc + rg[:, 2048:3072] * um).astype(BF16)
        y_ref[...] = y
        xn_ref[...] = x_ref[...] + _dot(y, wo_ref[...])

    row = lambda w: pl.BlockSpec((tm, w), lambda i: (i, 0))
    full = lambda a: pl.BlockSpec(a.shape, lambda i: (0,) * a.ndim)
    hp, hn = _halo_specs(tm, S, 1024, 6, PROJ_HALO)
    return pl.pallas_call(
        body, name="merge", grid=(nt,),
        in_specs=[row(6144), pl.BlockSpec((tm, 1024), lambda i: (i, 6)), hp, hn, row(1024), row(1024),
                  full(bg), full(convw), full(mqg), full(mk), full(mv), full(wba), full(wbc), full(wbm), full(wo)],
        out_specs=[row(1024), row(1024), row(CONV_W), row(MEM_W), row(3072), row(1024)],
        out_shape=[jax.ShapeDtypeStruct((S, 1024), F32), jax.ShapeDtypeStruct((S, 1024), BF16),
                   jax.ShapeDtypeStruct((S, CONV_W), BF16), jax.ShapeDtypeStruct((S, MEM_W), BF16),
                   jax.ShapeDtypeStruct((S, 3072), BF16), jax.ShapeDtypeStruct((S, 1024), BF16)],
        compiler_params=_params(("parallel",)),
    )(proj, proj, proj, proj, o, x, bg, convw, mqg, mk, mv, wba, wbc, wbm, wo)


def _loss_head(xf, tgt):
    S = xf.shape[0]
    tm = _tile(S, 1024)

    def body(x_ref, t_ref, g_ref, acc_ref):
        @pl.when(pl.program_id(0) == 0)
        def _():
            acc_ref[...] = jnp.zeros(acc_ref.shape, F32)

        e = x_ref[...] - t_ref[...]
        g_ref[...] = e * (1.0 / D_MODEL)
        part = jnp.sum((e * e).reshape(tm // 8, 8, D_MODEL), axis=0)
        tot = part[:, 0:128]
        for k in range(1, D_MODEL // 128):
            tot = tot + part[:, 128 * k:128 * (k + 1)]
        acc_ref[...] += tot

    row = pl.BlockSpec((tm, D_MODEL), lambda i: (i, 0))
    return pl.pallas_call(
        body, name="loss_head", grid=(S // tm,),
        in_specs=[row, row],
        out_specs=[row, pl.BlockSpec((8, 128), lambda i: (0, 0))],
        out_shape=[jax.ShapeDtypeStruct((S, D_MODEL), F32), jax.ShapeDtypeStruct((8, 128), F32)],
        compiler_params=_params(("arbitrary",)),
    )(xf, tgt)


def _merge_bwd(g, proj, o, u, bg, convw, mqg, mk, mv, wot, wbat, wbct, wbmt):
    S = g.shape[0]
    tm = _tile(S, 256)
    nt = S // tm
    M = mk.shape[0]

    def body(g_ref, main_ref, ccu_ref, hp_ref, hn_ref, o_ref, u_ref, bg_ref, cw_ref, mqg_ref, mk_ref, mv_ref,
             wot_ref, wbat_ref, wbct_ref, wbmt_ref,
             dmain_ref, dcv_ref, do_ref, delta_ref, du_ref, dbg_ref, dmk_ref, dmv_ref, dmqg_ref, dot_ref):
        i = pl.program_id(0)

        @pl.when(i == 0)
        def _():
            dbg_ref[...] = jnp.zeros(dbg_ref.shape, F32)
            dmk_ref[...] = jnp.zeros(dmk_ref.shape, F32)
            dmv_ref[...] = jnp.zeros(dmv_ref.shape, F32)
            dmqg_ref[...] = jnp.zeros(dmqg_ref.shape, F32)

        dy = _dot(g_ref[...].astype(BF16), wot_ref[...])
        d_branch = []
        for b, wt_ref in enumerate((wbat_ref, wbct_ref, wbmt_ref)):
            cols = slice(O_R + b * D_MODEL, O_R + (b + 1) * D_MODEL)
            rg = _sigmoid(main_ref[:, cols].astype(F32) + bg_ref[:, cols])
            dr = dy * u_ref[:, cols].astype(F32) * rg * (1.0 - rg)
            dmain_ref[:, cols] = dr.astype(BF16)
            dbg_ref[:, cols] += _colsum(dr)
            du = (dy * rg).astype(BF16)
            du_ref[:, cols] = du
            d_branch.append(_dot(du, wt_ref[...]))
        do_a, do_c, do_m = d_branch

        sil_a, dsil_a = _silu_and_grad(main_ref[:, O_GA:O_GA + 1024].astype(F32))
        ov = o_ref[...]
        d_o = do_a * sil_a
        do_ref[...] = d_o.astype(BF16)
        dot_ref[...] = d_o.T.astype(BF16)
        dmain_ref[:, O_GA:O_GA + 1024] = (do_a * ov * dsil_a).astype(BF16)
        prod = d_o * ov
        for h in range(N_HEADS):
            delta_ref[h] = jnp.sum(prod[:, h * HP:(h + 1) * HP], axis=-1, keepdims=True)

        z, z_up, z_dn = _conv_shifts(ccu_ref[:, 0:CONV_W].astype(F32), ccu_ref[:, CONV_W:].astype(F32), hp_ref, hn_ref, i, nt, tm)
        cv = cw_ref[0:1, :] * z_up + cw_ref[1:2, :] * z + cw_ref[2:3, :] * z_dn + cw_ref[3:4, :]
        sil_c, dsil_c = _silu_and_grad(main_ref[:, O_GC:O_GC + CONV_W].astype(F32))
        cb = main_ref[:, O_CB:O_CB + CONV_W].astype(F32)
        dmain_ref[:, O_CB:O_CB + CONV_W] = (do_c * cv * sil_c).astype(BF16)
        dmain_ref[:, O_GC:O_GC + CONV_W] = (do_c * cb * cv * dsil_c).astype(BF16)
        dcv_ref[...] = do_c * cb * sil_c

        sil_m, dsil_m = _silu_and_grad(main_ref[:, O_GM:O_GM + MEM_W].astype(F32))
        for h in range(MEM_HEADS):
            sl = slice(h * MEM_HD, (h + 1) * MEM_HD)
            qm = main_ref[:, O_QM + h * MEM_HD:O_QM + (h + 1) * MEM_HD].astype(F32)
            mk_h, mv_h = mk_ref[:, sl], mv_ref[:, sl]
            r, qhat, mq, p, pv = _mem_attend(qm, mqg_ref[...], mk_h, mv_h)
            dom = do_m[:, sl]
            dmain_ref[:, O_GM + h * MEM_HD:O_GM + (h + 1) * MEM_HD] = (dom * pv * dsil_m[:, sl]).astype(BF16)
            dpv = (dom * sil_m[:, sl]).astype(BF16)
            dp = _dot_nt(dpv, mv_h)
            ds = (p * (dp - jnp.sum(dp * p, axis=-1, keepdims=True)) * MEM_SCALE).astype(BF16)
            dmq = _dot(ds, mk_h)
            dmk_ref[:, sl] += _dot_tn(ds, mq)
            dmv_ref[:, sl] += _dot_tn(p.astype(BF16), dpv)
            dmqg_ref[...] += _colsum(dmq * qhat)
            dqm = _rms_bwd(dmq * mqg_ref[...], qhat, r, MEM_HD)
            dmain_ref[:, O_QM + h * MEM_HD:O_QM + (h + 1) * MEM_HD] = dqm.astype(BF16)

    row = lambda w: pl.BlockSpec((tm, w), lambda i: (i, 0))
    full = lambda a: pl.BlockSpec(a.shape, lambda i: (0,) * a.ndim)
    acc = lambda r, c: pl.BlockSpec((r, c), lambda i: (0, 0))
    hp, hn = _halo_specs(tm, S, 1024, 6, PROJ_HALO)
    return pl.pallas_call(
        body, name="merge_bwd", grid=(nt,),
        in_specs=[row(1024), row(6144), pl.BlockSpec((tm, 1024), lambda i: (i, 6)), hp, hn, row(1024), row(3072),
                  full(bg), full(convw), full(mqg), full(mk), full(mv), full(wot), full(wbat), full(wbct), full(wbmt)],
        out_specs=[row(6144), row(CONV_W), row(1024), pl.BlockSpec((N_HEADS, tm, 1), lambda i: (0, i, 0)), row(3072),
                   acc(1, 3072), acc(M, MEM_W), acc(M, MEM_W), acc(1, MEM_HD),
                   pl.BlockSpec((1024, tm), lambda i: (0, i))],
        out_shape=[jax.ShapeDtypeStruct((S, 6144), BF16), jax.ShapeDtypeStruct((S, CONV_W), F32),
                   jax.ShapeDtypeStruct((S, 1024), BF16), jax.ShapeDtypeStruct((N_HEADS, S, 1), F32),
                   jax.ShapeDtypeStruct((S, 3072), BF16), jax.ShapeDtypeStruct((1, 3072), F32),
                   jax.ShapeDtypeStruct((M, MEM_W), F32), jax.ShapeDtypeStruct((M, MEM_W), F32),
                   jax.ShapeDtypeStruct((1, MEM_HD), F32), jax.ShapeDtypeStruct((1024, S), BF16)],
        compiler_params=_params(("arbitrary",)),
    )(g, proj, proj, proj, proj, o, u, bg, convw, mqg, mk, mv, wot, wbat, wbct, wbmt)


def _conv_bwd(dcv, proj, convw):
    S = dcv.shape[0]
    tm = _tile(S, 512)
    nt = S // tm

    def body(d_ref, dp_ref, dn_ref, ccu_ref, hp_ref, hn_ref, cw_ref, dccu_ref, dcw_ref):
        i = pl.program_id(0)

        @pl.when(i == 0)
        def _():
            dcw_ref[...] = jnp.zeros(dcw_ref.shape, F32)

        cc, cu = ccu_ref[:, 0:CONV_W].astype(F32), ccu_ref[:, CONV_W:].astype(F32)
        z, z_up, z_dn = _conv_shifts(cc, cu, hp_ref, hn_ref, i, nt, tm)
        d = d_ref[...]
        dprev = jnp.where(i == 0, 0.0, dp_ref[7:8, :])
        dnext = jnp.where(i == nt - 1, 0.0, dn_ref[0:1, :])
        row = lax.broadcasted_iota(jnp.int32, d.shape, 0)
        d_up = jnp.where(row == 0, dprev, pltpu.roll(d, 1, 0))
        d_dn = jnp.where(row == tm - 1, dnext, pltpu.roll(d, tm - 1, 0))
        dz = cw_ref[0:1, :] * d_dn + cw_ref[1:2, :] * d + cw_ref[2:3, :] * d_up
        dccu_ref[:, 0:CONV_W] = (dz * cu).astype(BF16)
        dccu_ref[:, CONV_W:] = (dz * cc).astype(BF16)
        dcw_ref[0:1, :] += _colsum(d * z_up)
        dcw_ref[1:2, :] += _colsum(d * z)
        dcw_ref[2:3, :] += _colsum(d * z_dn)
        dcw_ref[3:4, :] += _colsum(d)

    hp, hn = _halo_specs(tm, S, 1024, 6, PROJ_HALO)
    dp, dn = _halo_specs(tm, S, CONV_W, 0, F32_HALO)
    return pl.pallas_call(
        body, name="conv_bwd", grid=(nt,),
        in_specs=[pl.BlockSpec((tm, CONV_W), lambda i: (i, 0)), dp, dn,
                  pl.BlockSpec((tm, 1024), lambda i: (i, 6)), hp, hn,
                  pl.BlockSpec((8, CONV_W), lambda i: (0, 0))],
        out_specs=[pl.BlockSpec((tm, 1024), lambda i: (i, 0)), pl.BlockSpec((8, CONV_W), lambda i: (0, 0))],
        out_shape=[jax.ShapeDtypeStruct((S, 1024), BF16), jax.ShapeDtypeStruct((8, CONV_W), F32)],
        compiler_params=_params(("arbitrary",)),
    )(dcv, dcv, dcv, proj, proj, proj, convw)


def _attn_bwd(q, k, v, do, qt, dot, lse, delta):
    S = q.shape[0]
    tq, tk = _tile(S, 512), _tile(S, 4096)
    ni, nk = S // tq, S // tk

    def body(q_ref, do_ref, qt_ref, dot_ref, k_ref, v_ref, lse_ref, delta_ref, dq_ref, dkt_hbm, dvt_hbm,
             dq_s, dkt_s, dvt_s, sem):
        h, i = pl.program_id(0), pl.program_id(1)

        @pl.when(i == 0)
        def _():
            dkt_s[...] = jnp.zeros(dkt_s.shape, F32)
            dvt_s[...] = jnp.zeros(dvt_s.shape, F32)

        dq_s[...] = jnp.zeros(dq_s.shape, F32)
        qv, dov, qtv, dotv = q_ref[...], do_ref[...], qt_ref[...], dot_ref[...]
        lse_c, delta_c = lse_ref[0], delta_ref[0]

        def step(c, carry):
            cols = pl.ds(pl.multiple_of(c * tk, tk), tk)
            kc, vc = k_ref[cols, :], v_ref[cols, :]
            p = jnp.exp(_dot_nt(qv, kc) - lse_c)
            dp = _dot_nt(dov, vc)
            ds = (p * (dp - delta_c)).astype(BF16)
            dq_s[...] += _dot(ds, kc)
            dvt_s[:, cols] += _dot(dotv, p.astype(BF16))
            dkt_s[:, cols] += _dot(qtv, ds)
            return carry

        lax.fori_loop(0, nk, step, 0)
        dq_ref[...] = dq_s[...]

        @pl.when(i == ni - 1)
        def _():
            head = pl.ds(pl.multiple_of(h * HP, HP), HP)
            out_k = pltpu.make_async_copy(dkt_s, dkt_hbm.at[head, :], sem.at[0])
            out_v = pltpu.make_async_copy(dvt_s, dvt_hbm.at[head, :], sem.at[1])
            out_k.start()
            out_v.start()
            out_k.wait()
            out_v.wait()

    col = pl.BlockSpec((1, tq, 1), lambda h, i: (h, i, 0))
    blk = pl.BlockSpec((tq, HP), lambda h, i: (i, h))
    blkt = pl.BlockSpec((HP, tq), lambda h, i: (h, i))
    res = pl.BlockSpec((S, HP), lambda h, i: (0, h))
    whole = pl.BlockSpec(memory_space=pl.ANY)
    return pl.pallas_call(
        body, name="attn_bwd", grid=(N_HEADS, ni),
        in_specs=[blk, blk, blkt, blkt, res, res, col, col],
        out_specs=[blk, whole, whole],
        out_shape=[jax.ShapeDtypeStruct((S, N_HEADS * HP), F32), jax.ShapeDtypeStruct((N_HEADS * HP, S), F32),
                   jax.ShapeDtypeStruct((N_HEADS * HP, S), F32)],
        scratch_shapes=[pltpu.VMEM((tq, HP), F32), pltpu.VMEM((HP, S), F32), pltpu.VMEM((HP, S), F32),
                        pltpu.SemaphoreType.DMA((2,))],
        compiler_params=_params(("parallel", "arbitrary")),
    )(q, do, qt, dot, k, v, lse, delta)


def _mla_prep_bwd(proj, tabs, dq, dkt, dvt, qg, kvg, qhg, khg, wuq, wuk, wuqt, wukt, wuvt):
    S = proj.shape[0]
    tm = _tile(S, 512)

    def body(a_ref, c_ref, sa_ref, sb_ref, dq_ref, dkt_ref, dvt_ref, qg_ref, kvg_ref, qhg_ref, khg_ref,
             wuq_ref, wuk_ref, wuqt_ref, wukt_ref, wuvt_ref,
             da_ref, dqraw_ref, dkraw_ref, dqg_ref, dkvg_ref, dqhg_ref, dkhg_ref, dv_ref):
        @pl.when(pl.program_id(0) == 0)
        def _():
            dqg_ref[...] = jnp.zeros(dqg_ref.shape, F32)
            dkvg_ref[...] = jnp.zeros(dkvg_ref.shape, F32)
            dqhg_ref[...] = jnp.zeros(dqhg_ref.shape, F32)
            dkhg_ref[...] = jnp.zeros(dkhg_ref.shape, F32)

        ql = a_ref[:, 0:Q_RANK].astype(F32)
        kvl = a_ref[:, Q_RANK:Q_RANK + KV_RANK].astype(F32)
        kpe = a_ref[:, Q_RANK + KV_RANK:Q_RANK + KV_RANK + HP].astype(F32)
        rq, qhat = _rms_parts(ql)
        rkv, kvhat = _rms_parts(kvl)
        qraw = _dot((qhat * qg_ref[...]).astype(BF16), wuq_ref[...])
        kn = _dot((kvhat * kvg_ref[...]).astype(BF16), wuk_ref[...])
        c, sa, sb = c_ref[...], sa_ref[...], sb_ref[...]
        dkpe = jnp.zeros(kpe.shape, F32)
        dqhg = jnp.zeros((1, HP), F32)
        dkhg = jnp.zeros((1, HP), F32)
        for h in range(N_HEADS):
            sl = slice(h * HP, (h + 1) * HP)
            r, that = _rms_parts(qraw[:, sl], QK_DIM)
            dtn = _rope_t(dq_ref[:, sl], c, sa, sb) * SCALE
            dqhg = dqhg + _colsum(dtn * that)
            dqraw_ref[:, sl] = _rms_bwd(dtn * qhg_ref[...], that, r, QK_DIM).astype(BF16)
            r, that = _rms_parts(kn[:, sl] + kpe, QK_DIM)
            dtn = _rope_t(dkt_ref[sl, :].T, c, sa, sb)
            dv_ref[:, sl] = dvt_ref[sl, :].T.astype(BF16)
            dkhg = dkhg + _colsum(dtn * that)
            dkr = _rms_bwd(dtn * khg_ref[...], that, r, QK_DIM)
            dkraw_ref[:, sl] = dkr.astype(BF16)
            dkpe = dkpe + dkr
        dqhg_ref[...] += dqhg
        dkhg_ref[...] += dkhg
        dqn = _dot(dqraw_ref[...], wuqt_ref[...])
        dqg_ref[...] += _colsum(dqn * qhat)
        da_ref[:, 0:Q_RANK] = _rms_bwd(dqn * qg_ref[...], qhat, rq, Q_RANK).astype(BF16)
        dkvn = _dot(dkraw_ref[...], wukt_ref[...]) + _dot(dv_ref[...], wuvt_ref[...])
        dkvg_ref[...] += _colsum(dkvn * kvhat)
        da_ref[:, Q_RANK:Q_RANK + KV_RANK] = _rms_bwd(dkvn * kvg_ref[...], kvhat, rkv, KV_RANK).astype(BF16)
        da_ref[:, Q_RANK + KV_RANK:Q_RANK + KV_RANK + HP] = dkpe.astype(BF16)
        da_ref[:, Q_RANK + KV_RANK + HP:] = jnp.zeros((tm, 1024 - Q_RANK - KV_RANK - HP), BF16)

    row = lambda w: pl.BlockSpec((tm, w), lambda i: (i, 0))
    full = lambda a: pl.BlockSpec(a.shape, lambda i: (0,) * a.ndim)
    acc = lambda c: pl.BlockSpec((1, c), lambda i: (0, 0))
    return pl.pallas_call(
        body, name="mla_prep_bwd", grid=(S // tm,),
        in_specs=[pl.BlockSpec((tm, 1024), lambda i: (i, 7)), row(HP), row(HP), row(HP),
                  row(1024), pl.BlockSpec((1024, tm), lambda i: (0, i)), pl.BlockSpec((1024, tm), lambda i: (0, i)),
                  full(qg), full(kvg), full(qhg), full(khg),
                  full(wuq), full(wuk), full(wuqt), full(wukt), full(wuvt)],
        out_specs=[row(1024), row(1024), row(1024), acc(Q_RANK), acc(KV_RANK), acc(HP), acc(HP), row(1024)],
        out_shape=[jax.ShapeDtypeStruct((S, 1024), BF16)] * 3
        + [jax.ShapeDtypeStruct((1, Q_RANK), F32), jax.ShapeDtypeStruct((1, KV_RANK), F32),
           jax.ShapeDtypeStruct((1, HP), F32), jax.ShapeDtypeStruct((1, HP), F32),
           jax.ShapeDtypeStruct((S, 1024), BF16)],
        compiler_params=_params(("arbitrary",)),
    )(proj, *tabs, dq, dkt, dvt, qg, kvg, qhg, khg, wuq, wuk, wuqt, wukt, wuvt)


def _inproj_bwd(dmain, dccu, dsega, wint, x, g, ng, carry=()):
    S = x.shape[0]
    tm = _tile(S, 256)
    nm, nc = dmain.shape[1], dccu.shape[1]
    assert nm + nc + dsega.shape[1] == PW

    def body(dm_ref, dc_ref, da_ref, w_ref, x_ref, g_ref, ng_ref, dx_ref, dng_ref):
        @pl.when(pl.program_id(0) == 0)
        def _():
            dng_ref[...] = jnp.zeros(dng_ref.shape, F32)

        dh = (_dot(dm_ref[...], w_ref[0:nm, :]) + _dot(dc_ref[...], w_ref[nm:nm + nc, :])
              + _dot(da_ref[...], w_ref[nm + nc:PW, :]))
        r, xhat = _rms_parts(x_ref[...])
        dng_ref[...] += _colsum(dh * xhat)
        dx_ref[...] = g_ref[...] + _rms_bwd(dh * ng_ref[...], xhat, r, D_MODEL)

    row = lambda w: pl.BlockSpec((tm, w), lambda i: (i, 0))
    in_specs = [row(nm), row(nc), row(dsega.shape[1]), pl.BlockSpec((PW, D_MODEL), lambda i: (0, 0)),
                row(D_MODEL), row(D_MODEL), pl.BlockSpec((1, D_MODEL), lambda i: (0, 0))]
    out_specs = [row(D_MODEL), pl.BlockSpec((1, D_MODEL), lambda i: (0, 0))]
    out_shape = [jax.ShapeDtypeStruct((S, D_MODEL), F32), jax.ShapeDtypeStruct((1, D_MODEL), F32)]
    args = (dmain, dccu, dsega, wint, x, g, ng)
    if not carry:
        dx, dng = pl.pallas_call(
            body, name="inproj_bwd", grid=(S // tm,), in_specs=in_specs, out_specs=out_specs, out_shape=out_shape,
            compiler_params=_params(("arbitrary",)))(*args)
        return dx, dng, []
    wrapped, extra = _carried(body, 7, 2, carry, gather=False)
    res = pl.pallas_call(
        wrapped, name="inproj_bwd_scatter", grid=(S // tm,), in_specs=in_specs + extra["in_specs"],
        out_specs=out_specs + extra["out_specs"], out_shape=out_shape + extra["out_shape"],
        scratch_shapes=extra["scratch_shapes"], compiler_params=_params(("arbitrary",)))(*args, *carry)
    return res[0], res[1], list(res[2:])


def _memkv_bwd(mem, mng, mkg, wmkv, wmkvt, dmk, dmv):
    M = mem.shape[0]

    def body(mem_ref, mng_ref, mkg_ref, w_ref, wt_ref, dmk_ref, dmv_ref, dw_ref, dmng_ref, dmkg_ref, d_s):
        r, mhat = _rms_parts(mem_ref[...])
        mn = (mhat * mng_ref[...]).astype(BF16)
        mkv = _dot(mn, w_ref[...])
        dmkg = jnp.zeros((1, MEM_HD), F32)
        for h in range(MEM_HEADS):
            sl = slice(h * MEM_HD, (h + 1) * MEM_HD)
            rk, khat = _rms_parts(mkv[:, 2 * MEM_HD * h:2 * MEM_HD * h + MEM_HD])
            dkn = dmk_ref[:, sl]
            dmkg = dmkg + _colsum(dkn * khat)
            d_s[:, 2 * MEM_HD * h:2 * MEM_HD * h + MEM_HD] = _rms_bwd(dkn * mkg_ref[...], khat, rk, MEM_HD).astype(BF16)
            d_s[:, 2 * MEM_HD * h + MEM_HD:2 * MEM_HD * (h + 1)] = dmv_ref[:, sl].astype(BF16)
        dmkg_ref[...] = dmkg
        dw_ref[...] = _dot_tn(mn, d_s[...])
        dmn = _dot(d_s[...], wt_ref[...])
        dmng_ref[...] = _colsum(dmn * mhat)

    return pl.pallas_call(
        body, name="memkv_bwd",
        out_shape=[jax.ShapeDtypeStruct((D_MODEL, 2 * MEM_W), F32), jax.ShapeDtypeStruct((1, D_MODEL), F32),
                   jax.ShapeDtypeStruct((1, MEM_HD), F32)],
        scratch_shapes=[pltpu.VMEM((M, 2 * MEM_W), BF16)],
        compiler_params=pltpu.CompilerParams(vmem_limit_bytes=VMEM_LIMIT_V7X),
    )(mem, mng, mkg, wmkv, wmkvt, dmk, dmv)


def _mm_tn(a, b, name, col0=0, ncols=None):
    S, M = a.shape
    N = b.shape[1] if ncols is None else ncols
    tm, tn, ts = _tile(M, 1024), _tile(N, 1024), _tile(S, 2048)
    assert col0 % tn == 0
    jb = col0 // tn

    def body(a_ref, b_ref, o_ref):
        @pl.when(pl.program_id(2) == 0)
        def _():
            o_ref[...] = jnp.zeros(o_ref.shape, F32)

        o_ref[...] += _dot_tn(a_ref[...].astype(BF16), b_ref[...].astype(BF16))

    return pl.pallas_call(
        body, name=name, grid=(M // tm, N // tn, S // ts),
        in_specs=[pl.BlockSpec((ts, tm), lambda i, j, k: (k, i)),
                  pl.BlockSpec((ts, tn), lambda i, j, k: (k, j + jb))],
        out_specs=pl.BlockSpec((tm, tn), lambda i, j, k: (i, j)),
        out_shape=jax.ShapeDtypeStruct((M, N), F32),
        compiler_params=_params(("parallel", "parallel", "arbitrary")),
    )(a, b)


def _adamw(w, g0, g1, m, v, name):
    R, C = w.shape
    tr = R
    for cand in (512, 256, 128, 64, 32, 16, 8):
        if R % cand == 0 and cand * C * 4 <= (1 << 20):
            tr = cand
            break
    c1 = 1.0 / (1.0 - ADAM_B1 ** ADAM_STEP)
    c2 = 1.0 / (1.0 - ADAM_B2 ** ADAM_STEP)

    def body(w_ref, g0_ref, g1_ref, m_ref, v_ref, g_ref, d_ref, nm_ref, nv_ref):
        g = g0_ref[...] + g1_ref[...]
        nm = ADAM_B1 * m_ref[...] + (1.0 - ADAM_B1) * g
        nv = ADAM_B2 * v_ref[...] + (1.0 - ADAM_B2) * (g * g)
        g_ref[...] = g
        nm_ref[...] = nm
        nv_ref[...] = nv
        d_ref[...] = -ADAM_LR * ((nm * c1) / (jnp.sqrt(nv * c2) + ADAM_EPS) + ADAM_WD * w_ref[...])

    blk = pl.BlockSpec((tr, C), lambda i: (i, 0))
    return pl.pallas_call(
        body, name=name, grid=(R // tr,),
        in_specs=[blk] * 5, out_specs=[blk] * 4,
        out_shape=[jax.ShapeDtypeStruct((R, C), F32)] * 4,
        compiler_params=_params(("parallel",)),
    )(w, g0, g1, m, v)


def _sum_slabs(a, name):
    K, R, C = a.shape
    tr = R
    for cand in (512, 256, 128, 64, 32, 16, 8):
        if R % cand == 0 and cand * C * 4 * K <= (4 << 20):
            tr = cand
            break

    def body(a_ref, o_ref):
        t = a_ref[0].astype(F32)
        for k in range(1, K):
            t = t + a_ref[k].astype(F32)
        o_ref[...] = t

    return pl.pallas_call(
        body, name=name, grid=(R // tr,),
        in_specs=[pl.BlockSpec((K, tr, C), lambda i: (0, i, 0))],
        out_specs=pl.BlockSpec((tr, C), lambda i: (i, 0)),
        out_shape=jax.ShapeDtypeStruct((R, C), F32),
        compiler_params=_params(("parallel",)),
    )(a)


def _gather_chips(arrs):
    n = len(arrs)
    halves = [a.shape[0] // 2 for a in arrs]
    assert all(a.shape[0] == 2 * hf for a, hf in zip(arrs, halves))

    def body(*refs):
        ins, outs = refs[:n], refs[n:2 * n]
        send1, recv1, send2, recv2, loc = refs[2 * n:]
        x, y, c = lax.axis_index("x"), lax.axis_index("y"), lax.axis_index("c")
        me = 2 * x + y
        mine = [pl.ds(c * hf, hf) for hf in halves]
        theirs = [pl.ds((1 - c) * hf, hf) for hf in halves]
        sibling = (x, y, 1 - c)
        waits = []
        for a in range(n):
            own = pltpu.make_async_copy(ins[a], outs[a].at[me], loc.at[a])
            own.start()
            waits.append(own.wait)

        def over_ici(a, k, src_chip):
            return pltpu.make_async_remote_copy(
                src_ref=ins[a].at[mine[a]], dst_ref=outs[a].at[src_chip, mine[a]], send_sem=send1.at[3 * a + k - 1],
                recv_sem=recv1.at[3 * a + k - 1], device_id=_chip_peer(x, y, c, k), device_id_type=MESH)

        def to_sibling(a, k, layers):
            block = outs[a].at[me ^ k, layers]
            return pltpu.make_async_remote_copy(
                src_ref=block, dst_ref=block, send_sem=send2.at[3 * a + k - 1], recv_sem=recv2.at[3 * a + k - 1],
                device_id=sibling, device_id_type=MESH)

        for a in range(n):
            for k in (1, 2, 3):
                cp = over_ici(a, k, me)
                cp.start()
                waits.append(cp.wait_send)
        for a in range(n):
            for k in (1, 2, 3):
                over_ici(a, k, me ^ k).wait_recv()
                cp = to_sibling(a, k, mine[a])
                cp.start()
                waits.append(cp.wait_send)
        for a in range(n):
            for k in (1, 2, 3):
                to_sibling(a, k, theirs[a]).wait_recv()
        for w in waits:
            w()

    return pl.pallas_call(
        body, name="gather_weights",
        in_specs=[HBM] * n, out_specs=[HBM] * n,
        out_shape=[jax.ShapeDtypeStruct((4,) + a.shape, a.dtype) for a in arrs],
        scratch_shapes=[pltpu.SemaphoreType.DMA((3 * n,)), pltpu.SemaphoreType.DMA((3 * n,)),
                        pltpu.SemaphoreType.DMA((3 * n,)), pltpu.SemaphoreType.DMA((3 * n,)),
                        pltpu.SemaphoreType.DMA((n,))],
    )(*arrs)


def _scatter_chips(arrs, small):
    n = len(arrs)

    def body(*refs):
        ins, small_in = refs[:n], refs[n]
        outs, small_out = refs[n + 1:2 * n + 1], refs[2 * n + 1]
        send, recv, loc, ssend, srecv = refs[2 * n + 2:]
        x, y, c = lax.axis_index("x"), lax.axis_index("y"), lax.axis_index("c")
        me = 2 * x + y
        me8 = 4 * x + 2 * y + c
        copies = []
        for a in range(n):
            own = pltpu.make_async_copy(ins[a].at[me], outs[a].at[me], loc.at[a])
            own.start()
            copies.append(own)
        own = pltpu.make_async_copy(small_in, small_out.at[me8], loc.at[n])
        own.start()
        copies.append(own)
        for k in range(1, 8):
            cp = pltpu.make_async_remote_copy(
                src_ref=small_in, dst_ref=small_out.at[me8], send_sem=ssend.at[k - 1], recv_sem=srecv.at[k - 1],
                device_id=(x ^ (k >> 2), y ^ ((k >> 1) & 1), c ^ (k & 1)), device_id_type=MESH)
            cp.start()
            copies.append(cp)
        for a in range(n):
            for k in (1, 2, 3):
                cp = pltpu.make_async_remote_copy(
                    src_ref=ins[a].at[me ^ k], dst_ref=outs[a].at[me], send_sem=send.at[3 * a + k - 1],
                    recv_sem=recv.at[3 * a + k - 1], device_id=_chip_peer(x, y, c, k), device_id_type=MESH)
                cp.start()
                copies.append(cp)
        for cp in copies:
            cp.wait()

    return pl.pallas_call(
        body, name="scatter_grads",
        in_specs=[HBM] * (n + 1), out_specs=[HBM] * (n + 1),
        out_shape=[jax.ShapeDtypeStruct(a.shape, a.dtype) for a in arrs]
        + [jax.ShapeDtypeStruct((8,) + small.shape, small.dtype)],
        scratch_shapes=[pltpu.SemaphoreType.DMA((3 * n,)), pltpu.SemaphoreType.DMA((3 * n,)),
                        pltpu.SemaphoreType.DMA((n + 1,)), pltpu.SemaphoreType.DMA((7,)),
                        pltpu.SemaphoreType.DMA((7,))],
    )(*arrs, small)


def _swap_cores(arrs):
    n = len(arrs)

    def body(*refs):
        ins, outs = refs[:n], refs[n:2 * n]
        send, recv = refs[2 * n:]
        x, y, c = lax.axis_index("x"), lax.axis_index("y"), lax.axis_index("c")
        copies = []
        for a in range(n):
            cp = pltpu.make_async_remote_copy(
                src_ref=ins[a], dst_ref=outs[a], send_sem=send.at[a], recv_sem=recv.at[a],
                device_id=(x, y, 1 - c), device_id_type=MESH)
            cp.start()
            copies.append(cp)
        for cp in copies:
            cp.wait()

    return pl.pallas_call(
        body, name="swap_cores",
        in_specs=[HBM] * n, out_specs=[HBM] * n,
        out_shape=[jax.ShapeDtypeStruct(a.shape, a.dtype) for a in arrs],
        scratch_shapes=[pltpu.SemaphoreType.DMA((n,)), pltpu.SemaphoreType.DMA((n,))],
    )(*arrs)


def _pad_last(a, n):
    return jnp.pad(a, [(0, 0)] * (a.ndim - 1) + [(0, n - a.shape[-1])])


def _pad_w_in(w):
    lead = w.shape[:-1]
    seg = lambda a, b: w[..., a:b]
    ga = _pad_last(seg(2720, 3232).reshape(lead + (N_HEADS, V_DIM)), HP).reshape(lead + (1024,))
    kpe = jnp.pad(seg(640, 672), [(0, 0)] * len(lead) + [(NOPE_DIM, HP - QK_DIM)])
    zero = jnp.zeros(lead + (PW - 7936,), w.dtype)
    return jnp.concatenate(
        [seg(4256, 7328), ga, seg(672, 1184), seg(2208, 2720), seg(3232, 3744), seg(3744, 4256),
         seg(1184, 1696), seg(1696, 2208), seg(0, 384), seg(384, 640), kpe, zero], axis=-1)


def _unpad_w_in(w):
    lead = w.shape[:-1]
    seg = lambda a, n: w[..., a:a + n]
    ga = seg(O_GA, 1024).reshape(lead + (N_HEADS, HP))[..., :V_DIM].reshape(lead + (N_HEADS * V_DIM,))
    return jnp.concatenate(
        [seg(O_QL, 384), seg(O_KVL, 256), seg(O_KPE + NOPE_DIM, ROPE_DIM), seg(O_CB, 512), seg(O_CC, 512),
         seg(O_CU, 512), seg(O_QM, 512), ga, seg(O_GC, 512), seg(O_GM, 512), seg(O_R, 3072)], axis=-1)


def _cols_from_shards(g):
    _, L, R, C = g.shape
    return jnp.transpose(g, (1, 2, 0, 3)).reshape(L, R, 4 * C)


def _t(w):
    return jnp.swapaxes(w, -1, -2)


def _layer_fwd(x, mem, tabs, p, next_shards=()):
    proj, xn, gathered = _inproj(x, p["norm_g"], p["w_in"], next_shards)
    q, k, v, qn, kvn, score_bound, qt, vt = _mla_prep(proj, tabs, p["q_norm_g"], p["kv_norm_g"], p["q_head_g"], p["k_head_g"],
                                        p["w_uq"], p["w_uk"], p["w_uv"])
    o, lse = _attn_fwd(qt, k, vt, score_bound)
    mk, mv = _memkv(mem, p["mem_norm_g"], p["mem_k_g"], p["w_mkv"])
    x_new, oa, oc, om, u, y = _merge(proj, o, x, p["b_gate"], p["conv_wb"], p["mem_q_g"], mk, mv,
                                     p["w_br_attn"], p["w_br_conv"], p["w_br_mem"], p["w_out"])
    saved = dict(x=x, proj=proj, xn=xn, q=q, qt=qt, k=k, v=v, qn=qn, kvn=kvn, o=o, lse=lse, mk=mk, mv=mv,
                 oa=oa, oc=oc, om=om, u=u, y=y)
    return x_new, saved, gathered


def _layer_bwd(g, mem, tabs, p, s, to_owner=()):
    S = g.shape[0]
    dmain, dcv, d_o, delta, du, dbg, dmk, dmv, dmqg, d_ot = _merge_bwd(
        g, s["proj"], s["o"], s["u"], p["b_gate"], p["conv_wb"], p["mem_q_g"], s["mk"], s["mv"],
        p["w_out_t"], p["w_br_attn_t"], p["w_br_conv_t"], p["w_br_mem_t"])
    dccu, dconv = _conv_bwd(dcv, s["proj"], p["conv_wb"])
    dq, dkt, dvt = _attn_bwd(s["q"], s["k"], s["v"], d_o, s["qt"], d_ot, s["lse"], delta)
    dsega, dqraw, dkraw, dqg, dkvg, dqhg, dkhg, dv = _mla_prep_bwd(
        s["proj"], tabs, dq, dkt, dvt, p["q_norm_g"], p["kv_norm_g"], p["q_head_g"], p["k_head_g"],
        p["w_uq"], p["w_uk"], p["w_uq_t"], p["w_uk_t"], p["w_uv_t"])
    dx, dng, received = _inproj_bwd(dmain, dccu, dsega, p["w_in_t"], s["x"], g, p["norm_g"], to_owner)
    dwmkv, dmng, dmkg = _memkv_bwd(mem, p["mem_norm_g"], p["mem_k_g"], p["w_mkv"], p["w_mkv_t"], dmk, dmv)
    grads = dict(
        norm_g=dng, b_gate=dbg, q_norm_g=dqg, kv_norm_g=dkvg, q_head_g=dqhg, k_head_g=dkhg,
        conv_wb=dconv, mem_norm_g=dmng, mem_q_g=dmqg, mem_k_g=dmkg, w_mkv=dwmkv,
        w_in=jnp.concatenate([_mm_tn(s["xn"], dmain, "grad_w_in"), _mm_tn(s["xn"], dccu, "grad_w_in_conv"),
                              _mm_tn(s["xn"], dsega, "grad_w_in_lat")], axis=1),
        w_uq=_mm_tn(s["qn"], dqraw, "grad_w_uq"),
        w_uk=_mm_tn(s["kvn"], dkraw, "grad_w_uk"),
        w_uv=_mm_tn(s["kvn"], dv, "grad_w_uv"),
        w_br_attn=_mm_tn(s["oa"], du, "grad_w_br_attn", 0, 1024),
        w_br_conv=_mm_tn(s["oc"], du, "grad_w_br_conv", 1024, 1024),
        w_br_mem=_mm_tn(s["om"], du, "grad_w_br_mem", 2048, 1024),
        w_out=_mm_tn(s["y"], g, "grad_w_out"),
    )
    return dx, grads, received


def _layer_params(big, full, l):
    p = {}
    w_in = _pad_w_in(big["w_in"])
    w_uq = _pad_last(big["w_uq"].reshape(Q_RANK, N_HEADS, QK_DIM), HP).reshape(Q_RANK, 1024)
    ukv = big["w_ukv"].reshape(KV_RANK, N_HEADS, NOPE_DIM + V_DIM)
    w_uk = _pad_last(ukv[..., :NOPE_DIM], HP).reshape(KV_RANK, 1024)
    w_uv = _pad_last(ukv[..., NOPE_DIM:], HP).reshape(KV_RANK, 1024)
    w_ba = jnp.pad(big["w_br_attn"].reshape(N_HEADS, V_DIM, D_MODEL), ((0, 0), (0, HP - V_DIM), (0, 0)))
    w_ba = w_ba.reshape(1024, D_MODEL)
    p.update(w_in=w_in, w_uq=w_uq, w_uk=w_uk, w_uv=w_uv, w_br_attn=w_ba, w_br_conv=big["w_br_conv"],
             w_br_mem=big["w_br_mem"], w_out=big["w_out"], w_mkv=big["w_mkv"])
    for n in ("w_in", "w_uq", "w_uk", "w_uv", "w_br_attn", "w_br_conv", "w_br_mem", "w_out", "w_mkv"):
        p[n + "_t"] = _t(p[n])
    for n in ("norm_g", "b_gate", "q_norm_g", "kv_norm_g", "mem_norm_g", "mem_q_g", "mem_k_g"):
        p[n] = full[n][l][None, :]
    p["q_head_g"] = _pad_last(full["q_head_g"][l][None, :], HP)
    p["k_head_g"] = _pad_last(full["k_head_g"][l][None, :], HP)
    p["conv_wb"] = jnp.concatenate(
        [full["conv_w"][l], full["conv_b"][l][None, :], jnp.zeros((4, CONV_W), F32)], axis=0)
    return p


def _join_shards(name, g):
    _, R, C = g.shape
    if name in _COL_SHARDED:
        return jnp.transpose(g, (1, 0, 2)).reshape(R, 4 * C)
    return g.reshape(4 * R, C)


def _split_shards(name, w):
    R, C = w.shape
    if name in _COL_SHARDED:
        return jnp.transpose(w.reshape(R, 4, C // 4), (1, 0, 2)).astype(BF16)
    return w.reshape(4, R // 4, C).astype(BF16)


def _unpad_grads(gp):
    out = {"w_in": _unpad_w_in(gp["w_in"])}
    out["w_uq"] = gp["w_uq"].reshape(Q_RANK, N_HEADS, HP)[..., :QK_DIM].reshape(Q_RANK, N_HEADS * QK_DIM)
    duk = gp["w_uk"].reshape(KV_RANK, N_HEADS, HP)[..., :NOPE_DIM]
    duv = gp["w_uv"].reshape(KV_RANK, N_HEADS, HP)[..., :V_DIM]
    out["w_ukv"] = jnp.concatenate([duk, duv], axis=-1).reshape(KV_RANK, 1024)
    out["w_br_attn"] = gp["w_br_attn"].reshape(N_HEADS, HP, D_MODEL)[:, :V_DIM].reshape(512, D_MODEL)
    for n in ("w_br_conv", "w_br_mem", "w_out", "w_mkv"):
        out[n] = gp[n]
    return out


def _train_step(x, mem, positions, w, target):
    S = x.shape[0]
    invf16 = ROPE_BASE ** (-jnp.arange(0, ROPE_DIM, 2, dtype=F32) / ROPE_DIM)
    invf = jnp.concatenate([jnp.zeros((NOPE_DIM,), F32), invf16, invf16, jnp.zeros((HP - QK_DIM,), F32)])[None, :]
    tabs = _rope_tables(jnp.broadcast_to(positions.reshape(S, 1), (S, HP)), invf)
    shards = [[w[n][l].astype(BF16) for n in _BIG] for l in range(DEPTH)]
    first = _gather_chips(shards[0] + [w["conv_w"]])
    gathered = first[:-1]
    full = {n: w[n] for n in _SMALL}
    full["conv_w"] = _cols_from_shards(first[-1])
    params, saved = [], []
    h = x
    for l in range(DEPTH):
        big = {n: _join_shards(n, g) for n, g in zip(_BIG, gathered)}
        params.append(_layer_params(big, full, l))
        h, s, gathered = _layer_fwd(h, mem, tabs, params[l], shards[l + 1] if l + 1 < DEPTH else ())
        saved.append(s)
    g, loss_part = _loss_head(h, target)
    per_layer, received = [None] * DEPTH, [None] * DEPTH
    to_owner = ()
    for l in reversed(range(DEPTH)):
        g, per_layer[l], got = _layer_bwd(g, mem, tabs, params[l], saved[l], to_owner)
        if to_owner:
            received[l + 1] = got
        big_g = _unpad_grads(per_layer[l])
        to_owner = [_split_shards(n, big_g[n]) for n in _BIG]
    st = lambda n: jnp.stack([per_layer[l][n] for l in range(DEPTH)])
    small = {}
    for n in ("norm_g", "b_gate", "q_norm_g", "kv_norm_g", "mem_norm_g", "mem_q_g", "mem_k_g"):
        small[n] = st(n)[:, 0, :]
    small["q_head_g"] = st("q_head_g")[:, 0, :QK_DIM]
    small["k_head_g"] = st("k_head_g")[:, 0, :QK_DIM]
    cwb = st("conv_wb")
    small["conv_w"] = cwb[:, 0:3, :]
    small["conv_b"] = cwb[:, 3, :]
    return loss_part, g, received, to_owner, small


_COL_SHARDED = ("w_in", "w_uq", "w_ukv", "w_br_attn", "w_br_conv", "w_br_mem")
_ROW_SHARDED = ("w_mkv", "w_out")
_BIG = _COL_SHARDED + _ROW_SHARDED
_SMALL = ("norm_g", "b_gate", "q_norm_g", "kv_norm_g", "q_head_g", "k_head_g", "conv_w", "conv_b",
          "mem_norm_g", "mem_q_g", "mem_k_g")
_ORDER = ("norm_g", "w_in", "b_gate", "q_norm_g", "w_uq", "kv_norm_g", "w_ukv", "q_head_g", "k_head_g",
          "conv_w", "conv_b", "mem_norm_g", "w_mkv", "mem_q_g", "mem_k_g", "w_br_attn", "w_br_conv",
          "w_br_mem", "w_out")


def _pack_small(d, extra):
    flat = jnp.concatenate([d[n].reshape(-1) for n in _SMALL] + [extra.reshape(-1)])
    n = flat.shape[0]
    rows = -(-n // 1024) * 8
    return jnp.pad(flat, (0, rows * 128 - n)).reshape(rows, 128)


def _unpack_small(packed, like):
    flat = packed.reshape(-1)
    out, off = {}, 0
    for n in _SMALL:
        sz = int(np.prod(like[n].shape))
        out[n] = flat[off:off + sz].reshape(like[n].shape)
        off += sz
    return out, flat[off:]


def kernel(x, mem, positions, norm_g, w_in, b_gate, q_norm_g, w_uq, kv_norm_g, w_ukv, q_head_g, k_head_g, conv_w, conv_b, mem_norm_g, w_mkv, mem_q_g, mem_k_g, w_br_attn, w_br_conv, w_br_mem, w_out, loss_target, m_norm_g, m_w_in, m_b_gate, m_q_norm_g, m_w_uq, m_kv_norm_g, m_w_ukv, m_q_head_g, m_k_head_g, m_conv_w, m_conv_b, m_mem_norm_g, m_w_mkv, m_mem_q_g, m_mem_k_g, m_w_br_attn, m_w_br_conv, m_w_br_mem, m_w_out, v_norm_g, v_w_in, v_b_gate, v_q_norm_g, v_w_uq, v_kv_norm_g, v_w_ukv, v_q_head_g, v_k_head_g, v_conv_w, v_conv_b, v_mem_norm_g, v_w_mkv, v_mem_q_g, v_mem_k_g, v_w_br_attn, v_w_br_conv, v_w_br_mem, v_w_out):
    w = dict(norm_g=norm_g, w_in=w_in, b_gate=b_gate, q_norm_g=q_norm_g, w_uq=w_uq, kv_norm_g=kv_norm_g,
             w_ukv=w_ukv, q_head_g=q_head_g, k_head_g=k_head_g, conv_w=conv_w, conv_b=conv_b,
             mem_norm_g=mem_norm_g, w_mkv=w_mkv, mem_q_g=mem_q_g, mem_k_g=mem_k_g, w_br_attn=w_br_attn,
             w_br_conv=w_br_conv, w_br_mem=w_br_mem, w_out=w_out)
    m = dict(norm_g=m_norm_g, w_in=m_w_in, b_gate=m_b_gate, q_norm_g=m_q_norm_g, w_uq=m_w_uq,
             kv_norm_g=m_kv_norm_g, w_ukv=m_w_ukv, q_head_g=m_q_head_g, k_head_g=m_k_head_g, conv_w=m_conv_w,
             conv_b=m_conv_b, mem_norm_g=m_mem_norm_g, w_mkv=m_w_mkv, mem_q_g=m_mem_q_g, mem_k_g=m_mem_k_g,
             w_br_attn=m_w_br_attn, w_br_conv=m_w_br_conv, w_br_mem=m_w_br_mem, w_out=m_w_out)
    v = dict(norm_g=v_norm_g, w_in=v_w_in, b_gate=v_b_gate, q_norm_g=v_q_norm_g, w_uq=v_w_uq,
             kv_norm_g=v_kv_norm_g, w_ukv=v_w_ukv, q_head_g=v_q_head_g, k_head_g=v_k_head_g, conv_w=v_conv_w,
             conv_b=v_conv_b, mem_norm_g=v_mem_norm_g, w_mkv=v_w_mkv, mem_q_g=v_mem_q_g, mem_k_g=v_mem_k_g,
             w_br_attn=v_w_br_attn, w_br_conv=v_w_br_conv, w_br_mem=v_w_br_mem, w_out=v_w_out)
    chip = 2 * lax.axis_index("x") + lax.axis_index("y")

    loss_part, grad_x, received, last_slabs, grads = _train_step(x[0], mem[0], positions[0], w, loss_target[0])

    loss_vec = jnp.zeros((128,), F32).at[0].set(0.5 / D_MODEL * jnp.sum(loss_part))
    small = _pack_small(grads, loss_vec)
    scattered = _scatter_chips(last_slabs, small)
    received[0] = scattered[:-1]
    small_sum = _sum_slabs(scattered[-1], "sum_small")
    partial = []
    for a, n in enumerate(_BIG):
        partial.append(jnp.concatenate([_sum_slabs(received[l][a], "sum_" + n) for l in range(DEPTH)], axis=0))
    other = _swap_cores(partial)

    small_g, tail = _unpack_small(small_sum, {n: (grads[n]) for n in _SMALL})
    loss = tail[0]
    small_g["conv_w"] = lax.dynamic_slice_in_dim(small_g["conv_w"], chip * (CONV_W // 4), CONV_W // 4, axis=2)

    outs_g, outs_d, outs_m, outs_v = {}, {}, {}, {}
    for n, p0, p1 in zip(_BIG, partial, other):
        shape = w[n].shape
        flat = lambda t: t.reshape(p0.shape)
        g_, d_, m_, v_ = _adamw(flat(w[n]), p0, p1, flat(m[n]), flat(v[n]), "adamw_" + n)
        outs_g[n], outs_d[n], outs_m[n], outs_v[n] = (t.reshape(shape) for t in (g_, d_, m_, v_))
    zero_small = jnp.zeros_like(small_sum)
    pk = lambda d: _pack_small(d, jnp.zeros((128,), F32))
    g_, d_, m_, v_ = _adamw(pk(w), _pack_small(small_g, jnp.zeros((128,), F32)), zero_small, pk(m), pk(v),
                            "adamw_small")
    like = {n: w[n] for n in _SMALL}
    for dst, packed in ((outs_g, g_), (outs_d, d_), (outs_m, m_), (outs_v, v_)):
        dst.update(_unpack_small(packed, like)[0])

    return (loss, grad_x[None], *[outs_g[n] for n in _ORDER], *[outs_d[n] for n in _ORDER],
            *[outs_m[n] for n in _ORDER], *[outs_v[n] for n in _ORDER])
```

```python
import functools

import numpy as np
import jax
import jax.numpy as jnp
from jax import lax
from jax.experimental import pallas as pl
from jax.experimental.pallas import tpu as pltpu

F32 = jnp.float32
BF16 = jnp.bfloat16

D_MODEL = 1024
DEPTH = 4
N_HEADS = 8
QK_DIM = 96
NOPE_DIM = 64
ROPE_DIM = 32
V_DIM = 64
Q_RANK = 384
KV_RANK = 256
CONV_W = 512
MEM_HEADS = 4
MEM_HD = 128
MEM_W = 512
IN_WIDTH = 7328
PW = 8192
HP = 128
PROJ_HALO = 16
F32_HALO = 8
EPS = 1e-6
ROPE_BASE = 10000.0
SCALE = QK_DIM ** -0.5
MEM_SCALE = MEM_HD ** -0.5

ADAM_LR = 0.001
ADAM_B1 = 0.9
ADAM_B2 = 0.999
ADAM_EPS = 1e-08
ADAM_WD = 0.01
ADAM_STEP = 10

VMEM_LIMIT_V7X = 56 * 1024 * 1024

O_R, O_GA, O_CB, O_QM, O_GC, O_GM, O_CC, O_CU, O_QL, O_KVL, O_KPE = (
    0, 3072, 4096, 4608, 5120, 5632, 6144, 6656, 7168, 7552, 7808)


def _params(sem, vmem=VMEM_LIMIT_V7X):
    return pltpu.CompilerParams(dimension_semantics=sem, vmem_limit_bytes=vmem)


def _sigmoid(t):
    return 0.5 * jnp.tanh(0.5 * t) + 0.5


def _silu_and_grad(g):
    sg = _sigmoid(g)
    return g * sg, sg * (1.0 + g * (1.0 - sg))


def _rms(t, g, n=None):
    n = t.shape[-1] if n is None else n
    r = lax.rsqrt(jnp.sum(t * t, axis=-1, keepdims=True) * (1.0 / n) + EPS)
    return (t * r) * g


def _rms_parts(t, n=None):
    n = t.shape[-1] if n is None else n
    r = lax.rsqrt(jnp.sum(t * t, axis=-1, keepdims=True) * (1.0 / n) + EPS)
    return r, t * r


def _rms_bwd(dhat, hat, r, n):
    return r * (dhat - hat * (jnp.sum(dhat * hat, axis=-1, keepdims=True) * (1.0 / n)))


def _rope(t, c, sa, sb):
    return t * c + pltpu.roll(t, HP - 16, 1) * sa + pltpu.roll(t, 16, 1) * sb


def _rope_t(d, c, sa, sb):
    return d * c + pltpu.roll(d * sa, 16, 1) + pltpu.roll(d * sb, HP - 16, 1)


def _dot(a, b):
    return jnp.dot(a, b, preferred_element_type=F32)


def _dot_nt(a, b):
    return lax.dot_general(a, b, (((1,), (1,)), ((), ())), preferred_element_type=F32)


def _dot_tn(a, b):
    return lax.dot_general(a, b, (((0,), (0,)), ((), ())), preferred_element_type=F32)


def _colsum(t):
    return jnp.sum(t, axis=0, keepdims=True)


def _tile(n, t):
    t = min(n, t)
    assert n % t == 0, (n, t)
    return t


def _rope_tables(pos_b, invf):
    S = pos_b.shape[0]
    tm = _tile(S, 1024)

    def body(pos_ref, invf_ref, c_ref, sa_ref, sb_ref):
        ang = pos_ref[...].astype(F32) * invf_ref[...]
        lane = lax.broadcasted_iota(jnp.int32, ang.shape, 1)
        cs = jnp.cos(ang)
        sn = jnp.sin(ang)
        c_ref[...] = jnp.where(lane < NOPE_DIM, 1.0, jnp.where(lane < QK_DIM, cs, 0.0))
        sa_ref[...] = jnp.where((lane >= NOPE_DIM) & (lane < NOPE_DIM + 16), -sn, 0.0)
        sb_ref[...] = jnp.where((lane >= NOPE_DIM + 16) & (lane < QK_DIM), sn, 0.0)

    blk = pl.BlockSpec((tm, HP), lambda i: (i, 0))
    return pl.pallas_call(
        body, name="rope_tables", grid=(S // tm,),
        in_specs=[blk, pl.BlockSpec((1, HP), lambda i: (0, 0))],
        out_specs=[blk, blk, blk],
        out_shape=[jax.ShapeDtypeStruct((S, HP), F32)] * 3,
        compiler_params=_params(("parallel",)),
    )(pos_b, invf)


MESH = pl.DeviceIdType.MESH
HBM = pl.BlockSpec(memory_space=pltpu.HBM)


def _chip_peer(x, y, c, k):
    return (x ^ (k >> 1), y ^ (k & 1), c)


def _chip_copies(ins, outs, send, recv, loc, gather):
    x, y, c = lax.axis_index("x"), lax.axis_index("y"), lax.axis_index("c")
    me = 2 * x + y
    cps = []
    for a in range(len(ins)):
        cps.append(pltpu.make_async_copy(ins[a] if gather else ins[a].at[me], outs[a].at[me], loc.at[a]))
        for k in (1, 2, 3):
            cps.append(pltpu.make_async_remote_copy(
                src_ref=ins[a] if gather else ins[a].at[me ^ k], dst_ref=outs[a].at[me],
                send_sem=send.at[3 * a + k - 1], recv_sem=recv.at[3 * a + k - 1],
                device_id=_chip_peer(x, y, c, k), device_id_type=MESH))
    return cps


def _carried(body, n_in, n_out, carry, gather):
    n = len(carry)

    def wrapped(*refs):
        ins, cin = refs[:n_in], refs[n_in:n_in + n]
        outs, cout = refs[n_in + n:n_in + n + n_out], refs[n_in + n + n_out:n_in + 2 * n + n_out]
        send, recv, loc = refs[n_in + 2 * n + n_out:]
        i = pl.program_id(0)

        @pl.when(i == 0)
        def _():
            for cp in _chip_copies(cin, cout, send, recv, loc, gather):
                cp.start()

        body(*ins, *outs)

        @pl.when(i == pl.num_programs(0) - 1)
        def _():
            for cp in _chip_copies(cin, cout, send, recv, loc, gather):
                cp.wait()

    specs = dict(
        in_specs=[HBM] * n, out_specs=[HBM] * n,
        out_shape=[jax.ShapeDtypeStruct(((4,) + a.shape) if gather else a.shape, a.dtype) for a in carry],
        scratch_shapes=[pltpu.SemaphoreType.DMA((3 * n,)), pltpu.SemaphoreType.DMA((3 * n,)),
                        pltpu.SemaphoreType.DMA((n,))])
    return wrapped, specs


def _inproj(x, g, w, carry=()):
    S = x.shape[0]
    tm = _tile(S, 256)

    def body(x_ref, g_ref, w_ref, proj_ref, xn_ref):
        h = _rms(x_ref[...], g_ref[...]).astype(BF16)
        xn_ref[...] = h
        proj_ref[...] = _dot(h, w_ref[...]).astype(BF16)

    row = lambda n: pl.BlockSpec((tm, n), lambda i: (i, 0))
    in_specs = [row(D_MODEL), pl.BlockSpec((1, D_MODEL), lambda i: (0, 0)), pl.BlockSpec((D_MODEL, PW), lambda i: (0, 0))]
    out_specs = [row(PW), row(D_MODEL)]
    out_shape = [jax.ShapeDtypeStruct((S, PW), BF16), jax.ShapeDtypeStruct((S, D_MODEL), BF16)]
    if not carry:
        proj, xn = pl.pallas_call(
            body, name="inproj", grid=(S // tm,), in_specs=in_specs, out_specs=out_specs, out_shape=out_shape,
            compiler_params=_params(("parallel",)))(x, g, w)
        return proj, xn, []
    wrapped, extra = _carried(body, 3, 2, carry, gather=True)
    res = pl.pallas_call(
        wrapped, name="inproj_gather", grid=(S // tm,), in_specs=in_specs + extra["in_specs"],
        out_specs=out_specs + extra["out_specs"], out_shape=out_shape + extra["out_shape"],
        scratch_shapes=extra["scratch_shapes"], compiler_params=_params(("arbitrary",)))(x, g, w, *carry)
    return res[0], res[1], list(res[2:])


def _mla_prep(proj, tabs, qg, kvg, qhg, khg, wuq, wuk, wuv):
    S = proj.shape[0]
    tm = _tile(S, 512)

    def body(a_ref, c_ref, sa_ref, sb_ref, qg_ref, kvg_ref, qhg_ref, khg_ref, wuq_ref, wuk_ref, wuv_ref,
             q_ref, k_ref, v_ref, qn_ref, kvn_ref, bound_ref, qt_ref, vt_ref):
        gq = jnp.max(jnp.abs(qhg_ref[...]), axis=-1, keepdims=True)
        gk = jnp.max(jnp.abs(khg_ref[...]), axis=-1, keepdims=True)
        bound_ref[...] = jnp.broadcast_to(gq * gk * (QK_DIM ** 0.5 * 1.01) + 1e-6, bound_ref.shape)
        ql = a_ref[:, 0:Q_RANK].astype(F32)
        kvl = a_ref[:, Q_RANK:Q_RANK + KV_RANK].astype(F32)
        kpe = a_ref[:, Q_RANK + KV_RANK:Q_RANK + KV_RANK + HP].astype(F32)
        qn = _rms(ql, qg_ref[...]).astype(BF16)
        kvn = _rms(kvl, kvg_ref[...]).astype(BF16)
        qn_ref[...] = qn
        kvn_ref[...] = kvn
        qraw = _dot(qn, wuq_ref[...])
        kn = _dot(kvn, wuk_ref[...])
        vf = _dot(kvn, wuv_ref[...])
        v_ref[...] = vf.astype(BF16)
        c, sa, sb = c_ref[...], sa_ref[...], sb_ref[...]
        for h in range(N_HEADS):
            sl = slice(h * HP, (h + 1) * HP)
            tq = _rms(qraw[:, sl], qhg_ref[...], QK_DIM)
            qh = _rope(tq, c, sa, sb) * SCALE
            q_ref[:, sl] = qh.astype(BF16)
            qt_ref[sl, :] = qh.T.astype(BF16)
            vt_ref[sl, :] = vf[:, sl].T.astype(BF16)
            tk = _rms(kn[:, sl] + kpe, khg_ref[...], QK_DIM)
            k_ref[:, sl] = _rope(tk, c, sa, sb).astype(BF16)

    row = lambda w: pl.BlockSpec((tm, w), lambda i: (i, 0))
    full = lambda a: pl.BlockSpec(a.shape, lambda i: (0,) * a.ndim)
    return pl.pallas_call(
        body, name="mla_prep", grid=(S // tm,),
        in_specs=[pl.BlockSpec((tm, 1024), lambda i: (i, 7)), row(HP), row(HP), row(HP),
                  full(qg), full(kvg), full(qhg), full(khg), full(wuq), full(wuk), full(wuv)],
        out_specs=[row(1024), row(1024), row(1024), row(Q_RANK), row(KV_RANK),
                   pl.BlockSpec((1, HP), lambda i: (0, 0)),
                   pl.BlockSpec((1024, tm), lambda i: (0, i)), pl.BlockSpec((1024, tm), lambda i: (0, i))],
        out_shape=[jax.ShapeDtypeStruct((S, 1024), BF16)] * 3
        + [jax.ShapeDtypeStruct((S, Q_RANK), BF16), jax.ShapeDtypeStruct((S, KV_RANK), BF16),
           jax.ShapeDtypeStruct((1, HP), F32)] + [jax.ShapeDtypeStruct((1024, S), BF16)] * 2,
        compiler_params=_params(("arbitrary",)),
    )(proj, *tabs, qg, kvg, qhg, khg, wuq, wuk, wuv)


def _attn_fwd(q, k, v):
    S = q.shape[0]
    tq, tk = _tile(S, 1024), _tile(S, 2048)
    nk, nb = S // tk, tk // HP

    def body(q_ref, k_ref, v_ref, o_ref, lse_ref, m_s, l_s, acc_s):
        m_s[...] = jnp.full(m_s.shape, -jnp.inf, F32)
        l_s[...] = jnp.zeros(l_s.shape, F32)
        acc_s[...] = jnp.zeros(acc_s.shape, F32)
        qv = q_ref[...]

        def step(c, carry):
            rows = pl.ds(pl.multiple_of(c * tk, tk), tk)
            s = _dot_nt(qv, k_ref[rows, :])
            cm = s[:, 0:HP]
            for j in range(1, nb):
                cm = jnp.maximum(cm, s[:, j * HP:(j + 1) * HP])
            m_prev = m_s[...]
            m_new = jnp.maximum(m_prev, jnp.max(cm, axis=-1, keepdims=True))
            alpha = jnp.exp(m_prev - m_new)
            lsum = alpha * l_s[...]
            ps = []
            for j in range(nb):
                pj = jnp.exp(s[:, j * HP:(j + 1) * HP] - m_new)
                lsum = lsum + pj
                ps.append(pj.astype(BF16))
            l_s[...] = lsum
            acc_s[...] = alpha * acc_s[...] + _dot(jnp.concatenate(ps, axis=1), v_ref[rows, :])
            m_s[...] = m_new
            return carry

        lax.fori_loop(0, nk, step, 0)
        l = jnp.sum(l_s[...], axis=-1, keepdims=True)
        o_ref[...] = acc_s[...] / l
        lse_ref[0] = m_s[:, 0:1] + jnp.log(l)

    return pl.pallas_call(
        body, name="attn_fwd", grid=(N_HEADS, S // tq),
        in_specs=[pl.BlockSpec((tq, HP), lambda h, i: (i, h)),
                  pl.BlockSpec((S, HP), lambda h, i: (0, h)),
                  pl.BlockSpec((S, HP), lambda h, i: (0, h))],
        out_specs=[pl.BlockSpec((tq, HP), lambda h, i: (i, h)),
                   pl.BlockSpec((1, tq, 1), lambda h, i: (h, i, 0))],
        out_shape=[jax.ShapeDtypeStruct((S, N_HEADS * HP), F32),
                   jax.ShapeDtypeStruct((N_HEADS, S, 1), F32)],
        scratch_shapes=[pltpu.VMEM((tq, HP), F32), pltpu.VMEM((tq, HP), F32), pltpu.VMEM((tq, HP), F32)],
        compiler_params=_params(("parallel", "parallel")),
    )(q, k, v)


def _memkv(mem, mng, mkg, wmkv):
    M = mem.shape[0]

    def body(mem_ref, mng_ref, mkg_ref, w_ref, mk_ref, mv_ref):
        mn = _rms(mem_ref[...], mng_ref[...]).astype(BF16)
        mkv = _dot(mn, w_ref[...])
        for h in range(MEM_HEADS):
            kraw = mkv[:, 2 * MEM_HD * h:2 * MEM_HD * h + MEM_HD]
            mk_ref[:, MEM_HD * h:MEM_HD * (h + 1)] = _rms(kraw, mkg_ref[...]).astype(BF16)
            mv_ref[:, MEM_HD * h:MEM_HD * (h + 1)] = mkv[:, 2 * MEM_HD * h + MEM_HD:2 * MEM_HD * (h + 1)].astype(BF16)

    return pl.pallas_call(
        body, name="memkv",
        out_shape=[jax.ShapeDtypeStruct((M, MEM_W), BF16)] * 2,
        compiler_params=pltpu.CompilerParams(vmem_limit_bytes=VMEM_LIMIT_V7X),
    )(mem, mng, mkg, wmkv)


def _conv_shifts(cc, cu, hp_ref, hn_ref, i, n_tiles, tm):
    z = cc * cu
    last = PROJ_HALO - 1
    zp = hp_ref[last:last + 1, 0:CONV_W].astype(F32) * hp_ref[last:last + 1, CONV_W:2 * CONV_W].astype(F32)
    zn = hn_ref[0:1, 0:CONV_W].astype(F32) * hn_ref[0:1, CONV_W:2 * CONV_W].astype(F32)
    zp = jnp.where(i == 0, 0.0, zp)
    zn = jnp.where(i == n_tiles - 1, 0.0, zn)
    row = lax.broadcasted_iota(jnp.int32, z.shape, 0)
    z_up = jnp.where(row == 0, zp, pltpu.roll(z, 1, 0))
    z_dn = jnp.where(row == tm - 1, zn, pltpu.roll(z, tm - 1, 0))
    return z, z_up, z_dn


def _halo_specs(tm, S, width, col, rows):
    per = tm // rows
    prev = pl.BlockSpec((rows, width), lambda i: (jnp.maximum(i * per - 1, 0), col))
    nxt = pl.BlockSpec((rows, width), lambda i: (jnp.minimum((i + 1) * per, S // rows - 1), col))
    return prev, nxt


def _mem_attend(qm, mqg, mk_h, mv_h):
    r, qhat = _rms_parts(qm)
    mq = (qhat * mqg).astype(BF16)
    s = _dot_nt(mq, mk_h) * MEM_SCALE
    e = jnp.exp(s - jnp.max(s, axis=-1, keepdims=True))
    p = e / jnp.sum(e, axis=-1, keepdims=True)
    pv = _dot(p.astype(BF16), mv_h)
    return r, qhat, mq, p, pv


def _merge(proj, o, x, bg, convw, mqg, mk, mv, wba, wbc, wbm, wo):
    S = x.shape[0]
    tm = _tile(S, 512)
    nt = S // tm

    def body(main_ref, ccu_ref, hp_ref, hn_ref, o_ref, x_ref, bg_ref, cw_ref, mqg_ref, mk_ref, mv_ref,
             wba_ref, wbc_ref, wbm_ref, wo_ref, xn_ref, oa_ref, oc_ref, om_ref, u_ref, y_ref):
        i = pl.program_id(0)
        sil_a, _ = _silu_and_grad(main_ref[:, O_GA:O_GA + 1024].astype(F32))
        oa = (o_ref[...] * sil_a).astype(BF16)
        oa_ref[...] = oa
        z, z_up, z_dn = _conv_shifts(ccu_ref[:, 0:CONV_W].astype(F32), ccu_ref[:, CONV_W:].astype(F32), hp_ref, hn_ref, i, nt, tm)
        cv = cw_ref[0:1, :] * z_up + cw_ref[1:2, :] * z + cw_ref[2:3, :] * z_dn + cw_ref[3:4, :]
        sil_c, _ = _silu_and_grad(main_ref[:, O_GC:O_GC + CONV_W].astype(F32))
        oc = (main_ref[:, O_CB:O_CB + CONV_W].astype(F32) * cv * sil_c).astype(BF16)
        oc_ref[...] = oc
        sil_m, _ = _silu_and_grad(main_ref[:, O_GM:O_GM + MEM_W].astype(F32))
        for h in range(MEM_HEADS):
            sl = slice(h * MEM_HD, (h + 1) * MEM_HD)
            qm = main_ref[:, O_QM + h * MEM_HD:O_QM + (h + 1) * MEM_HD].astype(F32)
            pv = _mem_attend(qm, mqg_ref[...], mk_ref[:, sl], mv_ref[:, sl])[4]
            om_ref[:, sl] = (pv * sil_m[:, sl]).astype(BF16)
        ua = _dot(oa, wba_ref[...])
        uc = _dot(oc, wbc_ref[...])
        um = _dot(om_ref[...], wbm_ref[...])
        u_ref[:, 0:1024] = ua.astype(BF16)
        u_ref[:, 1024:2048] = uc.astype(BF16)
        u_ref[:, 2048:3072] = um.astype(BF16)
        rg = _sigmoid(main_ref[:, O_R:O_R + 3072].astype(F32) + bg_ref[...])
        y = (rg[:, 0:1024] * ua + rg[:, 1024:2048] * uc + rg[:, 2048:3072] * um).astype(BF16)
        y_ref[...] = y
        xn_ref[...] = x_ref[...] + _dot(y, wo_ref[...])

    row = lambda w: pl.BlockSpec((tm, w), lambda i: (i, 0))
    full = lambda a: pl.BlockSpec(a.shape, lambda i: (0,) * a.ndim)
    hp, hn = _halo_specs(tm, S, 1024, 6, PROJ_HALO)
    return pl.pallas_call(
        body, name="merge", grid=(nt,),
        in_specs=[row(6144), pl.BlockSpec((tm, 1024), lambda i: (i, 6)), hp, hn, row(1024), row(1024),
                  full(bg), full(convw), full(mqg), full(mk), full(mv), full(wba), full(wbc), full(wbm), full(wo)],
        out_specs=[row(1024), row(1024), row(CONV_W), row(MEM_W), row(3072), row(1024)],
        out_shape=[jax.ShapeDtypeStruct((S, 1024), F32), jax.ShapeDtypeStruct((S, 1024), BF16),
                   jax.ShapeDtypeStruct((S, CONV_W), BF16), jax.ShapeDtypeStruct((S, MEM_W), BF16),
                   jax.ShapeDtypeStruct((S, 3072), BF16), jax.ShapeDtypeStruct((S, 1024), BF16)],
        compiler_params=_params(("parallel",)),
    )(proj, proj, proj, proj, o, x, bg, convw, mqg, mk, mv, wba, wbc, wbm, wo)


def _loss_head(xf, tgt):
    S = xf.shape[0]
    tm = _tile(S, 1024)

    def body(x_ref, t_ref, g_ref, acc_ref):
        @pl.when(pl.program_id(0) == 0)
        def _():
            acc_ref[...] = jnp.zeros(acc_ref.shape, F32)

        e = x_ref[...] - t_ref[...]
        g_ref[...] = e * (1.0 / D_MODEL)
        part = jnp.sum((e * e).reshape(tm // 8, 8, D_MODEL), axis=0)
        tot = part[:, 0:128]
        for k in range(1, D_MODEL // 128):
            tot = tot + part[:, 128 * k:128 * (k + 1)]
        acc_ref[...] += tot

    row = pl.BlockSpec((tm, D_MODEL), lambda i: (i, 0))
    return pl.pallas_call(
        body, name="loss_head", grid=(S // tm,),
        in_specs=[row, row],
        out_specs=[row, pl.BlockSpec((8, 128), lambda i: (0, 0))],
        out_shape=[jax.ShapeDtypeStruct((S, D_MODEL), F32), jax.ShapeDtypeStruct((8, 128), F32)],
        compiler_params=_params(("arbitrary",)),
    )(xf, tgt)


def _merge_bwd(g, proj, o, u, bg, convw, mqg, mk, mv, wot, wbat, wbct, wbmt):
    S = g.shape[0]
    tm = _tile(S, 256)
    nt = S // tm
    M = mk.shape[0]

    def body(g_ref, main_ref, ccu_ref, hp_ref, hn_ref, o_ref, u_ref, bg_ref, cw_ref, mqg_ref, mk_ref, mv_ref,
             wot_ref, wbat_ref, wbct_ref, wbmt_ref,
             dmain_ref, dcv_ref, do_ref, delta_ref, du_ref, dbg_ref, dmk_ref, dmv_ref, dmqg_ref, dot_ref):
        i = pl.program_id(0)

        @pl.when(i == 0)
        def _():
            dbg_ref[...] = jnp.zeros(dbg_ref.shape, F32)
            dmk_ref[...] = jnp.zeros(dmk_ref.shape, F32)
            dmv_ref[...] = jnp.zeros(dmv_ref.shape, F32)
            dmqg_ref[...] = jnp.zeros(dmqg_ref.shape, F32)

        dy = _dot(g_ref[...].astype(BF16), wot_ref[...])
        d_branch = []
        for b, wt_ref in enumerate((wbat_ref, wbct_ref, wbmt_ref)):
            cols = slice(O_R + b * D_MODEL, O_R + (b + 1) * D_MODEL)
            rg = _sigmoid(main_ref[:, cols].astype(F32) + bg_ref[:, cols])
            dr = dy * u_ref[:, cols].astype(F32) * rg * (1.0 - rg)
            dmain_ref[:, cols] = dr.astype(BF16)
            dbg_ref[:, cols] += _colsum(dr)
            du = (dy * rg).astype(BF16)
            du_ref[:, cols] = du
            d_branch.append(_dot(du, wt_ref[...]))
        do_a, do_c, do_m = d_branch

        sil_a, dsil_a = _silu_and_grad(main_ref[:, O_GA:O_GA + 1024].astype(F32))
        ov = o_ref[...]
        d_o = do_a * sil_a
        do_ref[...] = d_o.astype(BF16)
        dot_ref[...] = d_o.T.astype(BF16)
        dmain_ref[:, O_GA:O_GA + 1024] = (do_a * ov * dsil_a).astype(BF16)
        prod = d_o * ov
        for h in range(N_HEADS):
            delta_ref[h] = jnp.sum(prod[:, h * HP:(h + 1) * HP], axis=-1, keepdims=True)

        z, z_up, z_dn = _conv_shifts(ccu_ref[:, 0:CONV_W].astype(F32), ccu_ref[:, CONV_W:].astype(F32), hp_ref, hn_ref, i, nt, tm)
        cv = cw_ref[0:1, :] * z_up + cw_ref[1:2, :] * z + cw_ref[2:3, :] * z_dn + cw_ref[3:4, :]
        sil_c, dsil_c = _silu_and_grad(main_ref[:, O_GC:O_GC + CONV_W].astype(F32))
        cb = main_ref[:, O_CB:O_CB + CONV_W].astype(F32)
        dmain_ref[:, O_CB:O_CB + CONV_W] = (do_c * cv * sil_c).astype(BF16)
        dmain_ref[:, O_GC:O_GC + CONV_W] = (do_c * cb * cv * dsil_c).astype(BF16)
        dcv_ref[...] = do_c * cb * sil_c

        sil_m, dsil_m = _silu_and_grad(main_ref[:, O_GM:O_GM + MEM_W].astype(F32))
        for h in range(MEM_HEADS):
            sl = slice(h * MEM_HD, (h + 1) * MEM_HD)
            qm = main_ref[:, O_QM + h * MEM_HD:O_QM + (h + 1) * MEM_HD].astype(F32)
            mk_h, mv_h = mk_ref[:, sl], mv_ref[:, sl]
            r, qhat, mq, p, pv = _mem_attend(qm, mqg_ref[...], mk_h, mv_h)
            dom = do_m[:, sl]
            dmain_ref[:, O_GM + h * MEM_HD:O_GM + (h + 1) * MEM_HD] = (dom * pv * dsil_m[:, sl]).astype(BF16)
            dpv = (dom * sil_m[:, sl]).astype(BF16)
            dp = _dot_nt(dpv, mv_h)
            ds = (p * (dp - jnp.sum(dp * p, axis=-1, keepdims=True)) * MEM_SCALE).astype(BF16)
            dmq = _dot(ds, mk_h)
            dmk_ref[:, sl] += _dot_tn(ds, mq)
            dmv_ref[:, sl] += _dot_tn(p.astype(BF16), dpv)
            dmqg_ref[...] += _colsum(dmq * qhat)
            dqm = _rms_bwd(dmq * mqg_ref[...], qhat, r, MEM_HD)
            dmain_ref[:, O_QM + h * MEM_HD:O_QM + (h + 1) * MEM_HD] = dqm.astype(BF16)

    row = lambda w: pl.BlockSpec((tm, w), lambda i: (i, 0))
    full = lambda a: pl.BlockSpec(a.shape, lambda i: (0,) * a.ndim)
    acc = lambda r, c: pl.BlockSpec((r, c), lambda i: (0, 0))
    hp, hn = _halo_specs(tm, S, 1024, 6, PROJ_HALO)
    return pl.pallas_call(
        body, name="merge_bwd", grid=(nt,),
        in_specs=[row(1024), row(6144), pl.BlockSpec((tm, 1024), lambda i: (i, 6)), hp, hn, row(1024), row(3072),
                  full(bg), full(convw), full(mqg), full(mk), full(mv), full(wot), full(wbat), full(wbct), full(wbmt)],
        out_specs=[row(6144), row(CONV_W), row(1024), pl.BlockSpec((N_HEADS, tm, 1), lambda i: (0, i, 0)), row(3072),
                   acc(1, 3072), acc(M, MEM_W), acc(M, MEM_W), acc(1, MEM_HD),
                   pl.BlockSpec((1024, tm), lambda i: (0, i))],
        out_shape=[jax.ShapeDtypeStruct((S, 6144), BF16), jax.ShapeDtypeStruct((S, CONV_W), F32),
                   jax.ShapeDtypeStruct((S, 1024), BF16), jax.ShapeDtypeStruct((N_HEADS, S, 1), F32),
                   jax.ShapeDtypeStruct((S, 3072), BF16), jax.ShapeDtypeStruct((1, 3072), F32),
                   jax.ShapeDtypeStruct((M, MEM_W), F32), jax.ShapeDtypeStruct((M, MEM_W), F32),
                   jax.ShapeDtypeStruct((1, MEM_HD), F32), jax.ShapeDtypeStruct((1024, S), BF16)],
        compiler_params=_params(("arbitrary",)),
    )(g, proj, proj, proj, proj, o, u, bg, convw, mqg, mk, mv, wot, wbat, wbct, wbmt)


def _conv_bwd(dcv, proj, convw):
    S = dcv.shape[0]
    tm = _tile(S, 512)
    nt = S // tm

    def body(d_ref, dp_ref, dn_ref, ccu_ref, hp_ref, hn_ref, cw_ref, dccu_ref, dcw_ref):
        i = pl.program_id(0)

        @pl.when(i == 0)
        def _():
            dcw_ref[...] = jnp.zeros(dcw_ref.shape, F32)

        cc, cu = ccu_ref[:, 0:CONV_W].astype(F32), ccu_ref[:, CONV_W:].astype(F32)
        z, z_up, z_dn = _conv_shifts(cc, cu, hp_ref, hn_ref, i, nt, tm)
        d = d_ref[...]
        dprev = jnp.where(i == 0, 0.0, dp_ref[7:8, :])
        dnext = jnp.where(i == nt - 1, 0.0, dn_ref[0:1, :])
        row = lax.broadcasted_iota(jnp.int32, d.shape, 0)
        d_up = jnp.where(row == 0, dprev, pltpu.roll(d, 1, 0))
        d_dn = jnp.where(row == tm - 1, dnext, pltpu.roll(d, tm - 1, 0))
        dz = cw_ref[0:1, :] * d_dn + cw_ref[1:2, :] * d + cw_ref[2:3, :] * d_up
        dccu_ref[:, 0:CONV_W] = (dz * cu).astype(BF16)
        dccu_ref[:, CONV_W:] = (dz * cc).astype(BF16)
        dcw_ref[0:1, :] += _colsum(d * z_up)
        dcw_ref[1:2, :] += _colsum(d * z)
        dcw_ref[2:3, :] += _colsum(d * z_dn)
        dcw_ref[3:4, :] += _colsum(d)

    hp, hn = _halo_specs(tm, S, 1024, 6, PROJ_HALO)
    dp, dn = _halo_specs(tm, S, CONV_W, 0, F32_HALO)
    return pl.pallas_call(
        body, name="conv_bwd", grid=(nt,),
        in_specs=[pl.BlockSpec((tm, CONV_W), lambda i: (i, 0)), dp, dn,
                  pl.BlockSpec((tm, 1024), lambda i: (i, 6)), hp, hn,
                  pl.BlockSpec((8, CONV_W), lambda i: (0, 0))],
        out_specs=[pl.BlockSpec((tm, 1024), lambda i: (i, 0)), pl.BlockSpec((8, CONV_W), lambda i: (0, 0))],
        out_shape=[jax.ShapeDtypeStruct((S, 1024), BF16), jax.ShapeDtypeStruct((8, CONV_W), F32)],
        compiler_params=_params(("arbitrary",)),
    )(dcv, dcv, dcv, proj, proj, proj, convw)


def _attn_bwd(q, k, v, do, qt, dot, lse, delta):
    S = q.shape[0]
    tq, tk = _tile(S, 512), _tile(S, 4096)
    ni, nk = S // tq, S // tk

    def body(q_ref, do_ref, qt_ref, dot_ref, k_ref, v_ref, lse_ref, delta_ref, dq_ref, dkt_hbm, dvt_hbm,
             dq_s, dkt_s, dvt_s, sem):
        h, i = pl.program_id(0), pl.program_id(1)

        @pl.when(i == 0)
        def _():
            dkt_s[...] = jnp.zeros(dkt_s.shape, F32)
            dvt_s[...] = jnp.zeros(dvt_s.shape, F32)

        dq_s[...] = jnp.zeros(dq_s.shape, F32)
        qv, dov, qtv, dotv = q_ref[...], do_ref[...], qt_ref[...], dot_ref[...]
        lse_c, delta_c = lse_ref[0], delta_ref[0]

        def step(c, carry):
            cols = pl.ds(pl.multiple_of(c * tk, tk), tk)
            kc, vc = k_ref[cols, :], v_ref[cols, :]
            p = jnp.exp(_dot_nt(qv, kc) - lse_c)
            dp = _dot_nt(dov, vc)
            ds = (p * (dp - delta_c)).astype(BF16)
            dq_s[...] += _dot(ds, kc)
            dvt_s[:, cols] += _dot(dotv, p.astype(BF16))
            dkt_s[:, cols] += _dot(qtv, ds)
            return carry

        lax.fori_loop(0, nk, step, 0)
        dq_ref[...] = dq_s[...]

        @pl.when(i == ni - 1)
        def _():
            head = pl.ds(pl.multiple_of(h * HP, HP), HP)
            out_k = pltpu.make_async_copy(dkt_s, dkt_hbm.at[head, :], sem.at[0])
            out_v = pltpu.make_async_copy(dvt_s, dvt_hbm.at[head, :], sem.at[1])
            out_k.start()
            out_v.start()
            out_k.wait()
            out_v.wait()

    col = pl.BlockSpec((1, tq, 1), lambda h, i: (h, i, 0))
    blk = pl.BlockSpec((tq, HP), lambda h, i: (i, h))
    blkt = pl.BlockSpec((HP, tq), lambda h, i: (h, i))
    res = pl.BlockSpec((S, HP), lambda h, i: (0, h))
    whole = pl.BlockSpec(memory_space=pl.ANY)
    return pl.pallas_call(
        body, name="attn_bwd", grid=(N_HEADS, ni),
        in_specs=[blk, blk, blkt, blkt, res, res, col, col],
        out_specs=[blk, whole, whole],
        out_shape=[jax.ShapeDtypeStruct((S, N_HEADS * HP), F32), jax.ShapeDtypeStruct((N_HEADS * HP, S), F32),
                   jax.ShapeDtypeStruct((N_HEADS * HP, S), F32)],
        scratch_shapes=[pltpu.VMEM((tq, HP), F32), pltpu.VMEM((HP, S), F32), pltpu.VMEM((HP, S), F32),
                        pltpu.SemaphoreType.DMA((2,))],
        compiler_params=_params(("parallel", "arbitrary")),
    )(q, do, qt, dot, k, v, lse, delta)


def _mla_prep_bwd(proj, tabs, dq, dkt, dvt, qg, kvg, qhg, khg, wuq, wuk, wuqt, wukt, wuvt):
    S = proj.shape[0]
    tm = _tile(S, 512)

    def body(a_ref, c_ref, sa_ref, sb_ref, dq_ref, dkt_ref, dvt_ref, qg_ref, kvg_ref, qhg_ref, khg_ref,
             wuq_ref, wuk_ref, wuqt_ref, wukt_ref, wuvt_ref,
             da_ref, dqraw_ref, dkraw_ref, dqg_ref, dkvg_ref, dqhg_ref, dkhg_ref, dv_ref):
        @pl.when(pl.program_id(0) == 0)
        def _():
            dqg_ref[...] = jnp.zeros(dqg_ref.shape, F32)
            dkvg_ref[...] = jnp.zeros(dkvg_ref.shape, F32)
            dqhg_ref[...] = jnp.zeros(dqhg_ref.shape, F32)
            dkhg_ref[...] = jnp.zeros(dkhg_ref.shape, F32)

        ql = a_ref[:, 0:Q_RANK].astype(F32)
        kvl = a_ref[:, Q_RANK:Q_RANK + KV_RANK].astype(F32)
        kpe = a_ref[:, Q_RANK + KV_RANK:Q_RANK + KV_RANK + HP].astype(F32)
        rq, qhat = _rms_parts(ql)
        rkv, kvhat = _rms_parts(kvl)
        qraw = _dot((qhat * qg_ref[...]).astype(BF16), wuq_ref[...])
        kn = _dot((kvhat * kvg_ref[...]).astype(BF16), wuk_ref[...])
        c, sa, sb = c_ref[...], sa_ref[...], sb_ref[...]
        dkpe = jnp.zeros(kpe.shape, F32)
        dqhg = jnp.zeros((1, HP), F32)
        dkhg = jnp.zeros((1, HP), F32)
        for h in range(N_HEADS):
            sl = slice(h * HP, (h + 1) * HP)
            r, that = _rms_parts(qraw[:, sl], QK_DIM)
            dtn = _rope_t(dq_ref[:, sl], c, sa, sb) * SCALE
            dqhg = dqhg + _colsum(dtn * that)
            dqraw_ref[:, sl] = _rms_bwd(dtn * qhg_ref[...], that, r, QK_DIM).astype(BF16)
            r, that = _rms_parts(kn[:, sl] + kpe, QK_DIM)
            dtn = _rope_t(dkt_ref[sl, :].T, c, sa, sb)
            dv_ref[:, sl] = dvt_ref[sl, :].T.astype(BF16)
            dkhg = dkhg + _colsum(dtn * that)
            dkr = _rms_bwd(dtn * khg_ref[...], that, r, QK_DIM)
            dkraw_ref[:, sl] = dkr.astype(BF16)
            dkpe = dkpe + dkr
        dqhg_ref[...] += dqhg
        dkhg_ref[...] += dkhg
        dqn = _dot(dqraw_ref[...], wuqt_ref[...])
        dqg_ref[...] += _colsum(dqn * qhat)
        da_ref[:, 0:Q_RANK] = _rms_bwd(dqn * qg_ref[...], qhat, rq, Q_RANK).astype(BF16)
        dkvn = _dot(dkraw_ref[...], wukt_ref[...]) + _dot(dv_ref[...], wuvt_ref[...])
        dkvg_ref[...] += _colsum(dkvn * kvhat)
        da_ref[:, Q_RANK:Q_RANK + KV_RANK] = _rms_bwd(dkvn * kvg_ref[...], kvhat, rkv, KV_RANK).astype(BF16)
        da_ref[:, Q_RANK + KV_RANK:Q_RANK + KV_RANK + HP] = dkpe.astype(BF16)
        da_ref[:, Q_RANK + KV_RANK + HP:] = jnp.zeros((tm, 1024 - Q_RANK - KV_RANK - HP), BF16)

    row = lambda w: pl.BlockSpec((tm, w), lambda i: (i, 0))
    full = lambda a: pl.BlockSpec(a.shape, lambda i: (0,) * a.ndim)
    acc = lambda c: pl.BlockSpec((1, c), lambda i: (0, 0))
    return pl.pallas_call(
        body, name="mla_prep_bwd", grid=(S // tm,),
        in_specs=[pl.BlockSpec((tm, 1024), lambda i: (i, 7)), row(HP), row(HP), row(HP),
                  row(1024), pl.BlockSpec((1024, tm), lambda i: (0, i)), pl.BlockSpec((1024, tm), lambda i: (0, i)),
                  full(qg), full(kvg), full(qhg), full(khg),
                  full(wuq), full(wuk), full(wuqt), full(wukt), full(wuvt)],
        out_specs=[row(1024), row(1024), row(1024), acc(Q_RANK), acc(KV_RANK), acc(HP), acc(HP), row(1024)],
        out_shape=[jax.ShapeDtypeStruct((S, 1024), BF16)] * 3
        + [jax.ShapeDtypeStruct((1, Q_RANK), F32), jax.ShapeDtypeStruct((1, KV_RANK), F32),
           jax.ShapeDtypeStruct((1, HP), F32), jax.ShapeDtypeStruct((1, HP), F32),
           jax.ShapeDtypeStruct((S, 1024), BF16)],
        compiler_params=_params(("arbitrary",)),
    )(proj, *tabs, dq, dkt, dvt, qg, kvg, qhg, khg, wuq, wuk, wuqt, wukt, wuvt)


def _inproj_bwd(dmain, dccu, dsega, wint, x, g, ng, carry=()):
    S = x.shape[0]
    tm = _tile(S, 256)
    nm, nc = dmain.shape[1], dccu.shape[1]
    assert nm + nc + dsega.shape[1] == PW

    def body(dm_ref, dc_ref, da_ref, w_ref, x_ref, g_ref, ng_ref, dx_ref, dng_ref):
        @pl.when(pl.program_id(0) == 0)
        def _():
            dng_ref[...] = jnp.zeros(dng_ref.shape, F32)

        dh = (_dot(dm_ref[...], w_ref[0:nm, :]) + _dot(dc_ref[...], w_ref[nm:nm + nc, :])
              + _dot(da_ref[...], w_ref[nm + nc:PW, :]))
        r, xhat = _rms_parts(x_ref[...])
        dng_ref[...] += _colsum(dh * xhat)
        dx_ref[...] = g_ref[...] + _rms_bwd(dh * ng_ref[...], xhat, r, D_MODEL)

    row = lambda w: pl.BlockSpec((tm, w), lambda i: (i, 0))
    in_specs = [row(nm), row(nc), row(dsega.shape[1]), pl.BlockSpec((PW, D_MODEL), lambda i: (0, 0)),
                row(D_MODEL), row(D_MODEL), pl.BlockSpec((1, D_MODEL), lambda i: (0, 0))]
    out_specs = [row(D_MODEL), pl.BlockSpec((1, D_MODEL), lambda i: (0, 0))]
    out_shape = [jax.ShapeDtypeStruct((S, D_MODEL), F32), jax.ShapeDtypeStruct((1, D_MODEL), F32)]
    args = (dmain, dccu, dsega, wint, x, g, ng)
    if not carry:
        dx, dng = pl.pallas_call(
            body, name="inproj_bwd", grid=(S // tm,), in_specs=in_specs, out_specs=out_specs, out_shape=out_shape,
            compiler_params=_params(("arbitrary",)))(*args)
        return dx, dng, []
    wrapped, extra = _carried(body, 7, 2, carry, gather=False)
    res = pl.pallas_call(
        wrapped, name="inproj_bwd_scatter", grid=(S // tm,), in_specs=in_specs + extra["in_specs"],
        out_specs=out_specs + extra["out_specs"], out_shape=out_shape + extra["out_shape"],
        scratch_shapes=extra["scratch_shapes"], compiler_params=_params(("arbitrary",)))(*args, *carry)
    return res[0], res[1], list(res[2:])


def _memkv_bwd(mem, mng, mkg, wmkv, wmkvt, dmk, dmv):
    M = mem.shape[0]

    def body(mem_ref, mng_ref, mkg_ref, w_ref, wt_ref, dmk_ref, dmv_ref, dw_ref, dmng_ref, dmkg_ref, d_s):
        r, mhat = _rms_parts(mem_ref[...])
        mn = (mhat * mng_ref[...]).astype(BF16)
        mkv = _dot(mn, w_ref[...])
        dmkg = jnp.zeros((1, MEM_HD), F32)
        for h in range(MEM_HEADS):
            sl = slice(h * MEM_HD, (h + 1) * MEM_HD)
            rk, khat = _rms_parts(mkv[:, 2 * MEM_HD * h:2 * MEM_HD * h + MEM_HD])
            dkn = dmk_ref[:, sl]
            dmkg = dmkg + _colsum(dkn * khat)
            d_s[:, 2 * MEM_HD * h:2 * MEM_HD * h + MEM_HD] = _rms_bwd(dkn * mkg_ref[...], khat, rk, MEM_HD).astype(BF16)
            d_s[:, 2 * MEM_HD * h + MEM_HD:2 * MEM_HD * (h + 1)] = dmv_ref[:, sl].astype(BF16)
        dmkg_ref[...] = dmkg
        dw_ref[...] = _dot_tn(mn, d_s[...])
        dmn = _dot(d_s[...], wt_ref[...])
        dmng_ref[...] = _colsum(dmn * mhat)

    return pl.pallas_call(
        body, name="memkv_bwd",
        out_shape=[jax.ShapeDtypeStruct((D_MODEL, 2 * MEM_W), F32), jax.ShapeDtypeStruct((1, D_MODEL), F32),
                   jax.ShapeDtypeStruct((1, MEM_HD), F32)],
        scratch_shapes=[pltpu.VMEM((M, 2 * MEM_W), BF16)],
        compiler_params=pltpu.CompilerParams(vmem_limit_bytes=VMEM_LIMIT_V7X),
    )(mem, mng, mkg, wmkv, wmkvt, dmk, dmv)


def _mm_tn(a, b, name, col0=0, ncols=None):
    S, M = a.shape
    N = b.shape[1] if ncols is None else ncols
    tm, tn, ts = _tile(M, 1024), _tile(N, 1024), _tile(S, 2048)
    assert col0 % tn == 0
    jb = col0 // tn

    def body(a_ref, b_ref, o_ref):
        @pl.when(pl.program_id(2) == 0)
        def _():
            o_ref[...] = jnp.zeros(o_ref.shape, F32)

        o_ref[...] += _dot_tn(a_ref[...].astype(BF16), b_ref[...].astype(BF16))

    return pl.pallas_call(
        body, name=name, grid=(M // tm, N // tn, S // ts),
        in_specs=[pl.BlockSpec((ts, tm), lambda i, j, k: (k, i)),
                  pl.BlockSpec((ts, tn), lambda i, j, k: (k, j + jb))],
        out_specs=pl.BlockSpec((tm, tn), lambda i, j, k: (i, j)),
        out_shape=jax.ShapeDtypeStruct((M, N), F32),
        compiler_params=_params(("parallel", "parallel", "arbitrary")),
    )(a, b)


def _adamw(w, g0, g1, m, v, name):
    R, C = w.shape
    tr = R
    for cand in (512, 256, 128, 64, 32, 16, 8):
        if R % cand == 0 and cand * C * 4 <= (1 << 20):
            tr = cand
            break
    c1 = 1.0 / (1.0 - ADAM_B1 ** ADAM_STEP)
    c2 = 1.0 / (1.0 - ADAM_B2 ** ADAM_STEP)

    def body(w_ref, g0_ref, g1_ref, m_ref, v_ref, g_ref, d_ref, nm_ref, nv_ref):
        g = g0_ref[...] + g1_ref[...]
        nm = ADAM_B1 * m_ref[...] + (1.0 - ADAM_B1) * g
        nv = ADAM_B2 * v_ref[...] + (1.0 - ADAM_B2) * (g * g)
        g_ref[...] = g
        nm_ref[...] = nm
        nv_ref[...] = nv
        d_ref[...] = -ADAM_LR * ((nm * c1) / (jnp.sqrt(nv * c2) + ADAM_EPS) + ADAM_WD * w_ref[...])

    blk = pl.BlockSpec((tr, C), lambda i: (i, 0))
    return pl.pallas_call(
        body, name=name, grid=(R // tr,),
        in_specs=[blk] * 5, out_specs=[blk] * 4,
        out_shape=[jax.ShapeDtypeStruct((R, C), F32)] * 4,
        compiler_params=_params(("parallel",)),
    )(w, g0, g1, m, v)


def _sum_slabs(a, name):
    K, R, C = a.shape
    tr = R
    for cand in (512, 256, 128, 64, 32, 16, 8):
        if R % cand == 0 and cand * C * 4 * K <= (4 << 20):
            tr = cand
            break

    def body(a_ref, o_ref):
        t = a_ref[0].astype(F32)
        for k in range(1, K):
            t = t + a_ref[k].astype(F32)
        o_ref[...] = t

    return pl.pallas_call(
        body, name=name, grid=(R // tr,),
        in_specs=[pl.BlockSpec((K, tr, C), lambda i: (0, i, 0))],
        out_specs=pl.BlockSpec((tr, C), lambda i: (i, 0)),
        out_shape=jax.ShapeDtypeStruct((R, C), F32),
        compiler_params=_params(("parallel",)),
    )(a)


def _gather_chips(arrs):
    n = len(arrs)
    halves = [a.shape[0] // 2 for a in arrs]
    assert all(a.shape[0] == 2 * hf for a, hf in zip(arrs, halves))

    def body(*refs):
        ins, outs = refs[:n], refs[n:2 * n]
        send1, recv1, send2, recv2, loc = refs[2 * n:]
        x, y, c = lax.axis_index("x"), lax.axis_index("y"), lax.axis_index("c")
        me = 2 * x + y
        mine = [pl.ds(c * hf, hf) for hf in halves]
        theirs = [pl.ds((1 - c) * hf, hf) for hf in halves]
        sibling = (x, y, 1 - c)
        waits = []
        for a in range(n):
            own = pltpu.make_async_copy(ins[a], outs[a].at[me], loc.at[a])
            own.start()
            waits.append(own.wait)

        def over_ici(a, k, src_chip):
            return pltpu.make_async_remote_copy(
                src_ref=ins[a].at[mine[a]], dst_ref=outs[a].at[src_chip, mine[a]], send_sem=send1.at[3 * a + k - 1],
                recv_sem=recv1.at[3 * a + k - 1], device_id=_chip_peer(x, y, c, k), device_id_type=MESH)

        def to_sibling(a, k, layers):
            block = outs[a].at[me ^ k, layers]
            return pltpu.make_async_remote_copy(
                src_ref=block, dst_ref=block, send_sem=send2.at[3 * a + k - 1], recv_sem=recv2.at[3 * a + k - 1],
                device_id=sibling, device_id_type=MESH)

        for a in range(n):
            for k in (1, 2, 3):
                cp = over_ici(a, k, me)
                cp.start()
                waits.append(cp.wait_send)
        for a in range(n):
            for k in (1, 2, 3):
                over_ici(a, k, me ^ k).wait_recv()
                cp = to_sibling(a, k, mine[a])
                cp.start()
                waits.append(cp.wait_send)
        for a in range(n):
            for k in (1, 2, 3):
                to_sibling(a, k, theirs[a]).wait_recv()
        for w in waits:
            w()

    return pl.pallas_call(
        body, name="gather_weights",
        in_specs=[HBM] * n, out_specs=[HBM] * n,
        out_shape=[jax.ShapeDtypeStruct((4,) + a.shape, a.dtype) for a in arrs],
        scratch_shapes=[pltpu.SemaphoreType.DMA((3 * n,)), pltpu.SemaphoreType.DMA((3 * n,)),
                        pltpu.SemaphoreType.DMA((3 * n,)), pltpu.SemaphoreType.DMA((3 * n,)),
                        pltpu.SemaphoreType.DMA((n,))],
    )(*arrs)


def _scatter_chips(arrs, small):
    n = len(arrs)

    def body(*refs):
        ins, small_in = refs[:n], refs[n]
        outs, small_out = refs[n + 1:2 * n + 1], refs[2 * n + 1]
        send, recv, loc, ssend, srecv = refs[2 * n + 2:]
        x, y, c = lax.axis_index("x"), lax.axis_index("y"), lax.axis_index("c")
        me = 2 * x + y
        me8 = 4 * x + 2 * y + c
        copies = []
        for a in range(n):
            own = pltpu.make_async_copy(ins[a].at[me], outs[a].at[me], loc.at[a])
            own.start()
            copies.append(own)
        own = pltpu.make_async_copy(small_in, small_out.at[me8], loc.at[n])
        own.start()
        copies.append(own)
        for k in range(1, 8):
            cp = pltpu.make_async_remote_copy(
                src_ref=small_in, dst_ref=small_out.at[me8], send_sem=ssend.at[k - 1], recv_sem=srecv.at[k - 1],
                device_id=(x ^ (k >> 2), y ^ ((k >> 1) & 1), c ^ (k & 1)), device_id_type=MESH)
            cp.start()
            copies.append(cp)
        for a in range(n):
            for k in (1, 2, 3):
                cp = pltpu.make_async_remote_copy(
                    src_ref=ins[a].at[me ^ k], dst_ref=outs[a].at[me], send_sem=send.at[3 * a + k - 1],
                    recv_sem=recv.at[3 * a + k - 1], device_id=_chip_peer(x, y, c, k), device_id_type=MESH)
                cp.start()
                copies.append(cp)
        for cp in copies:
            cp.wait()

    return pl.pallas_call(
        body, name="scatter_grads",
        in_specs=[HBM] * (n + 1), out_specs=[HBM] * (n + 1),
        out_shape=[jax.ShapeDtypeStruct(a.shape, a.dtype) for a in arrs]
        + [jax.ShapeDtypeStruct((8,) + small.shape, small.dtype)],
        scratch_shapes=[pltpu.SemaphoreType.DMA((3 * n,)), pltpu.SemaphoreType.DMA((3 * n,)),
                        pltpu.SemaphoreType.DMA((n + 1,)), pltpu.SemaphoreType.DMA((7,)),
                        pltpu.SemaphoreType.DMA((7,))],
    )(*arrs, small)


def _swap_cores(arrs):
    n = len(arrs)

    def body(*refs):
        ins, outs = refs[:n], refs[n:2 * n]
        send, recv = refs[2 * n:]
        x, y, c = lax.axis_index("x"), lax.axis_index("y"), lax.axis_index("c")
        copies = []
        for a in range(n):
            cp = pltpu.make_async_remote_copy(
                src_ref=ins[a], dst_ref=outs[a], send_sem=send.at[a], recv_sem=recv.at[a],
                device_id=(x, y, 1 - c), device_id_type=MESH)
            cp.start()
            copies.append(cp)
        for cp in copies:
            cp.wait()

    return pl.pallas_call(
        body, name="swap_cores",
        in_specs=[HBM] * n, out_specs=[HBM] * n,
        out_shape=[jax.ShapeDtypeStruct(a.shape, a.dtype) for a in arrs],
        scratch_shapes=[pltpu.SemaphoreType.DMA((n,)), pltpu.SemaphoreType.DMA((n,))],
    )(*arrs)


def _pad_last(a, n):
    return jnp.pad(a, [(0, 0)] * (a.ndim - 1) + [(0, n - a.shape[-1])])


def _pad_w_in(w):
    lead = w.shape[:-1]
    seg = lambda a, b: w[..., a:b]
    ga = _pad_last(seg(2720, 3232).reshape(lead + (N_HEADS, V_DIM)), HP).reshape(lead + (1024,))
    kpe = jnp.pad(seg(640, 672), [(0, 0)] * len(lead) + [(NOPE_DIM, HP - QK_DIM)])
    zero = jnp.zeros(lead + (PW - 7936,), w.dtype)
    return jnp.concatenate(
        [seg(4256, 7328), ga, seg(672, 1184), seg(2208, 2720), seg(3232, 3744), seg(3744, 4256),
         seg(1184, 1696), seg(1696, 2208), seg(0, 384), seg(384, 640), kpe, zero], axis=-1)


def _unpad_w_in(w):
    lead = w.shape[:-1]
    seg = lambda a, n: w[..., a:a + n]
    ga = seg(O_GA, 1024).reshape(lead + (N_HEADS, HP))[..., :V_DIM].reshape(lead + (N_HEADS * V_DIM,))
    return jnp.concatenate(
        [seg(O_QL, 384), seg(O_KVL, 256), seg(O_KPE + NOPE_DIM, ROPE_DIM), seg(O_CB, 512), seg(O_CC, 512),
         seg(O_CU, 512), seg(O_QM, 512), ga, seg(O_GC, 512), seg(O_GM, 512), seg(O_R, 3072)], axis=-1)


def _cols_from_shards(g):
    _, L, R, C = g.shape
    return jnp.transpose(g, (1, 2, 0, 3)).reshape(L, R, 4 * C)


def _t(w):
    return jnp.swapaxes(w, -1, -2)


def _layer_fwd(x, mem, tabs, p, next_shards=()):
    proj, xn, gathered = _inproj(x, p["norm_g"], p["w_in"], next_shards)
    q, k, v, qn, kvn, score_bound, qt, vt = _mla_prep(proj, tabs, p["q_norm_g"], p["kv_norm_g"], p["q_head_g"], p["k_head_g"],
                                        p["w_uq"], p["w_uk"], p["w_uv"])
    o, lse = _attn_fwd(q, k, v)
    mk, mv = _memkv(mem, p["mem_norm_g"], p["mem_k_g"], p["w_mkv"])
    x_new, oa, oc, om, u, y = _merge(proj, o, x, p["b_gate"], p["conv_wb"], p["mem_q_g"], mk, mv,
                                     p["w_br_attn"], p["w_br_conv"], p["w_br_mem"], p["w_out"])
    saved = dict(x=x, proj=proj, xn=xn, q=q, qt=qt, k=k, v=v, qn=qn, kvn=kvn, o=o, lse=lse, mk=mk, mv=mv,
                 oa=oa, oc=oc, om=om, u=u, y=y)
    return x_new, saved, gathered


def _layer_bwd(g, mem, tabs, p, s, to_owner=()):
    S = g.shape[0]
    dmain, dcv, d_o, delta, du, dbg, dmk, dmv, dmqg, d_ot = _merge_bwd(
        g, s["proj"], s["o"], s["u"], p["b_gate"], p["conv_wb"], p["mem_q_g"], s["mk"], s["mv"],
        p["w_out_t"], p["w_br_attn_t"], p["w_br_conv_t"], p["w_br_mem_t"])
    dccu, dconv = _conv_bwd(dcv, s["proj"], p["conv_wb"])
    dq, dkt, dvt = _attn_bwd(s["q"], s["k"], s["v"], d_o, s["qt"], d_ot, s["lse"], delta)
    dsega, dqraw, dkraw, dqg, dkvg, dqhg, dkhg, dv = _mla_prep_bwd(
        s["proj"], tabs, dq, dkt, dvt, p["q_norm_g"], p["kv_norm_g"], p["q_head_g"], p["k_head_g"],
        p["w_uq"], p["w_uk"], p["w_uq_t"], p["w_uk_t"], p["w_uv_t"])
    dx, dng, received = _inproj_bwd(dmain, dccu, dsega, p["w_in_t"], s["x"], g, p["norm_g"], to_owner)
    dwmkv, dmng, dmkg = _memkv_bwd(mem, p["mem_norm_g"], p["mem_k_g"], p["w_mkv"], p["w_mkv_t"], dmk, dmv)
    grads = dict(
        norm_g=dng, b_gate=dbg, q_norm_g=dqg, kv_norm_g=dkvg, q_head_g=dqhg, k_head_g=dkhg,
        conv_wb=dconv, mem_norm_g=dmng, mem_q_g=dmqg, mem_k_g=dmkg, w_mkv=dwmkv,
        w_in=jnp.concatenate([_mm_tn(s["xn"], dmain, "grad_w_in"), _mm_tn(s["xn"], dccu, "grad_w_in_conv"),
                              _mm_tn(s["xn"], dsega, "grad_w_in_lat")], axis=1),
        w_uq=_mm_tn(s["qn"], dqraw, "grad_w_uq"),
        w_uk=_mm_tn(s["kvn"], dkraw, "grad_w_uk"),
        w_uv=_mm_tn(s["kvn"], dv, "grad_w_uv"),
        w_br_attn=_mm_tn(s["oa"], du, "grad_w_br_attn", 0, 1024),
        w_br_conv=_mm_tn(s["oc"], du, "grad_w_br_conv", 1024, 1024),
        w_br_mem=_mm_tn(s["om"], du, "grad_w_br_mem", 2048, 1024),
        w_out=_mm_tn(s["y"], g, "grad_w_out"),
    )
    return dx, grads, received


def _layer_params(big, full, l):
    p = {}
    w_in = _pad_w_in(big["w_in"])
    w_uq = _pad_last(big["w_uq"].reshape(Q_RANK, N_HEADS, QK_DIM), HP).reshape(Q_RANK, 1024)
    ukv = big["w_ukv"].reshape(KV_RANK, N_HEADS, NOPE_DIM + V_DIM)
    w_uk = _pad_last(ukv[..., :NOPE_DIM], HP).reshape(KV_RANK, 1024)
    w_uv = _pad_last(ukv[..., NOPE_DIM:], HP).reshape(KV_RANK, 1024)
    w_ba = jnp.pad(big["w_br_attn"].reshape(N_HEADS, V_DIM, D_MODEL), ((0, 0), (0, HP - V_DIM), (0, 0)))
    w_ba = w_ba.reshape(1024, D_MODEL)
    p.update(w_in=w_in, w_uq=w_uq, w_uk=w_uk, w_uv=w_uv, w_br_attn=w_ba, w_br_conv=big["w_br_conv"],
             w_br_mem=big["w_br_mem"], w_out=big["w_out"], w_mkv=big["w_mkv"])
    for n in ("w_in", "w_uq", "w_uk", "w_uv", "w_br_attn", "w_br_conv", "w_br_mem", "w_out", "w_mkv"):
        p[n + "_t"] = _t(p[n])
    for n in ("norm_g", "b_gate", "q_norm_g", "kv_norm_g", "mem_norm_g", "mem_q_g", "mem_k_g"):
        p[n] = full[n][l][None, :]
    p["q_head_g"] = _pad_last(full["q_head_g"][l][None, :], HP)
    p["k_head_g"] = _pad_last(full["k_head_g"][l][None, :], HP)
    p["conv_wb"] = jnp.concatenate(
        [full["conv_w"][l], full["conv_b"][l][None, :], jnp.zeros((4, CONV_W), F32)], axis=0)
    return p


def _join_shards(name, g):
    _, R, C = g.shape
    if name in _COL_SHARDED:
        return jnp.transpose(g, (1, 0, 2)).reshape(R, 4 * C)
    return g.reshape(4 * R, C)


def _split_shards(name, w):
    R, C = w.shape
    if name in _COL_SHARDED:
        return jnp.transpose(w.reshape(R, 4, C // 4), (1, 0, 2)).astype(BF16)
    return w.reshape(4, R // 4, C).astype(BF16)


def _unpad_grads(gp):
    out = {"w_in": _unpad_w_in(gp["w_in"])}
    out["w_uq"] = gp["w_uq"].reshape(Q_RANK, N_HEADS, HP)[..., :QK_DIM].reshape(Q_RANK, N_HEADS * QK_DIM)
    duk = gp["w_uk"].reshape(KV_RANK, N_HEADS, HP)[..., :NOPE_DIM]
    duv = gp["w_uv"].reshape(KV_RANK, N_HEADS, HP)[..., :V_DIM]
    out["w_ukv"] = jnp.concatenate([duk, duv], axis=-1).reshape(KV_RANK, 1024)
    out["w_br_attn"] = gp["w_br_attn"].reshape(N_HEADS, HP, D_MODEL)[:, :V_DIM].reshape(512, D_MODEL)
    for n in ("w_br_conv", "w_br_mem", "w_out", "w_mkv"):
        out[n] = gp[n]
    return out


def _train_step(x, mem, positions, w, target):
    S = x.shape[0]
    invf16 = ROPE_BASE ** (-jnp.arange(0, ROPE_DIM, 2, dtype=F32) / ROPE_DIM)
    invf = jnp.concatenate([jnp.zeros((NOPE_DIM,), F32), invf16, invf16, jnp.zeros((HP - QK_DIM,), F32)])[None, :]
    tabs = _rope_tables(jnp.broadcast_to(positions.reshape(S, 1), (S, HP)), invf)
    shards = [[w[n][l].astype(BF16) for n in _BIG] for l in range(DEPTH)]
    first = _gather_chips(shards[0] + [w["conv_w"]])
    gathered = first[:-1]
    full = {n: w[n] for n in _SMALL}
    full["conv_w"] = _cols_from_shards(first[-1])
    params, saved = [], []
    h = x
    for l in range(DEPTH):
        big = {n: _join_shards(n, g) for n, g in zip(_BIG, gathered)}
        params.append(_layer_params(big, full, l))
        h, s, gathered = _layer_fwd(h, mem, tabs, params[l], shards[l + 1] if l + 1 < DEPTH else ())
        saved.append(s)
    g, loss_part = _loss_head(h, target)
    per_layer, received = [None] * DEPTH, [None] * DEPTH
    to_owner = ()
    for l in reversed(range(DEPTH)):
        g, per_layer[l], got = _layer_bwd(g, mem, tabs, params[l], saved[l], to_owner)
        if to_owner:
            received[l + 1] = got
        big_g = _unpad_grads(per_layer[l])
        to_owner = [_split_shards(n, big_g[n]) for n in _BIG]
    st = lambda n: jnp.stack([per_layer[l][n] for l in range(DEPTH)])
    small = {}
    for n in ("norm_g", "b_gate", "q_norm_g", "kv_norm_g", "mem_norm_g", "mem_q_g", "mem_k_g"):
        small[n] = st(n)[:, 0, :]
    small["q_head_g"] = st("q_head_g")[:, 0, :QK_DIM]
    small["k_head_g"] = st("k_head_g")[:, 0, :QK_DIM]
    cwb = st("conv_wb")
    small["conv_w"] = cwb[:, 0:3, :]
    small["conv_b"] = cwb[:, 3, :]
    return loss_part, g, received, to_owner, small


_COL_SHARDED = ("w_in", "w_uq", "w_ukv", "w_br_attn", "w_br_conv", "w_br_mem")
_ROW_SHARDED = ("w_mkv", "w_out")
_BIG = _COL_SHARDED + _ROW_SHARDED
_SMALL = ("norm_g", "b_gate", "q_norm_g", "kv_norm_g", "q_head_g", "k_head_g", "conv_w", "conv_b",
          "mem_norm_g", "mem_q_g", "mem_k_g")
_ORDER = ("norm_g", "w_in", "b_gate", "q_norm_g", "w_uq", "kv_norm_g", "w_ukv", "q_head_g", "k_head_g",
          "conv_w", "conv_b", "mem_norm_g", "w_mkv", "mem_q_g", "mem_k_g", "w_br_attn", "w_br_conv",
          "w_br_mem", "w_out")


def _pack_small(d, extra):
    flat = jnp.concatenate([d[n].reshape(-1) for n in _SMALL] + [extra.reshape(-1)])
    n = flat.shape[0]
    rows = -(-n // 1024) * 8
    return jnp.pad(flat, (0, rows * 128 - n)).reshape(rows, 128)


def _unpack_small(packed, like):
    flat = packed.reshape(-1)
    out, off = {}, 0
    for n in _SMALL:
        sz = int(np.prod(like[n].shape))
        out[n] = flat[off:off + sz].reshape(like[n].shape)
        off += sz
    return out, flat[off:]


def kernel(x, mem, positions, norm_g, w_in, b_gate, q_norm_g, w_uq, kv_norm_g, w_ukv, q_head_g, k_head_g, conv_w, conv_b, mem_norm_g, w_mkv, mem_q_g, mem_k_g, w_br_attn, w_br_conv, w_br_mem, w_out, loss_target, m_norm_g, m_w_in, m_b_gate, m_q_norm_g, m_w_uq, m_kv_norm_g, m_w_ukv, m_q_head_g, m_k_head_g, m_conv_w, m_conv_b, m_mem_norm_g, m_w_mkv, m_mem_q_g, m_mem_k_g, m_w_br_attn, m_w_br_conv, m_w_br_mem, m_w_out, v_norm_g, v_w_in, v_b_gate, v_q_norm_g, v_w_uq, v_kv_norm_g, v_w_ukv, v_q_head_g, v_k_head_g, v_conv_w, v_conv_b, v_mem_norm_g, v_w_mkv, v_mem_q_g, v_mem_k_g, v_w_br_attn, v_w_br_conv, v_w_br_mem, v_w_out):
    w = dict(norm_g=norm_g, w_in=w_in, b_gate=b_gate, q_norm_g=q_norm_g, w_uq=w_uq, kv_norm_g=kv_norm_g,
             w_ukv=w_ukv, q_head_g=q_head_g, k_head_g=k_head_g, conv_w=conv_w, conv_b=conv_b,
             mem_norm_g=mem_norm_g, w_mkv=w_mkv, mem_q_g=mem_q_g, mem_k_g=mem_k_g, w_br_attn=w_br_attn,
             w_br_conv=w_br_conv, w_br_mem=w_br_mem, w_out=w_out)
    m = dict(norm_g=m_norm_g, w_in=m_w_in, b_gate=m_b_gate, q_norm_g=m_q_norm_g, w_uq=m_w_uq,
             kv_norm_g=m_kv_norm_g, w_ukv=m_w_ukv, q_head_g=m_q_head_g, k_head_g=m_k_head_g, conv_w=m_conv_w,
             conv_b=m_conv_b, mem_norm_g=m_mem_norm_g, w_mkv=m_w_mkv, mem_q_g=m_mem_q_g, mem_k_g=m_mem_k_g,
             w_br_attn=m_w_br_attn, w_br_conv=m_w_br_conv, w_br_mem=m_w_br_mem, w_out=m_w_out)
    v = dict(norm_g=v_norm_g, w_in=v_w_in, b_gate=v_b_gate, q_norm_g=v_q_norm_g, w_uq=v_w_uq,
             kv_norm_g=v_kv_norm_g, w_ukv=v_w_ukv, q_head_g=v_q_head_g, k_head_g=v_k_head_g, conv_w=v_conv_w,
             conv_b=v_conv_b, mem_norm_g=v_mem_norm_g, w_mkv=v_w_mkv, mem_q_g=v_mem_q_g, mem_k_g=v_mem_k_g,
             w_br_attn=v_w_br_attn, w_br_conv=v_w_br_conv, w_br_mem=v_w_br_mem, w_out=v_w_out)
    chip = 2 * lax.axis_index("x") + lax.axis_index("y")

    loss_part, grad_x, received, last_slabs, grads = _train_step(x[0], mem[0], positions[0], w, loss_target[0])

    loss_vec = jnp.zeros((128,), F32).at[0].set(0.5 / D_MODEL * jnp.sum(loss_part))
    small = _pack_small(grads, loss_vec)
    scattered = _scatter_chips(last_slabs, small)
    received[0] = scattered[:-1]
    small_sum = _sum_slabs(scattered[-1], "sum_small")
    partial = []
    for a, n in enumerate(_BIG):
        partial.append(jnp.concatenate([_sum_slabs(received[l][a], "sum_" + n) for l in range(DEPTH)], axis=0))
    other = _swap_cores(partial)

    small_g, tail = _unpack_small(small_sum, {n: (grads[n]) for n in _SMALL})
    loss = tail[0]
    small_g["conv_w"] = lax.dynamic_slice_in_dim(small_g["conv_w"], chip * (CONV_W // 4), CONV_W // 4, axis=2)

    outs_g, outs_d, outs_m, outs_v = {}, {}, {}, {}
    for n, p0, p1 in zip(_BIG, partial, other):
        shape = w[n].shape
        flat = lambda t: t.reshape(p0.shape)
        g_, d_, m_, v_ = _adamw(flat(w[n]), p0, p1, flat(m[n]), flat(v[n]), "adamw_" + n)
        outs_g[n], outs_d[n], outs_m[n], outs_v[n] = (t.reshape(shape) for t in (g_, d_, m_, v_))
    zero_small = jnp.zeros_like(small_sum)
    pk = lambda d: _pack_small(d, jnp.zeros((128,), F32))
    g_, d_, m_, v_ = _adamw(pk(w), _pack_small(small_g, jnp.zeros((128,), F32)), zero_small, pk(m), pk(v),
                            "adamw_small")
    like = {n: w[n] for n in _SMALL}
    for dst, packed in ((outs_g, g_), (outs_d, d_), (outs_m, m_), (outs_v, v_)):
        dst.update(_unpack_small(packed, like)[0])

    return (loss, grad_x[None], *[outs_g[n] for n in _ORDER], *[outs_d[n] for n in _ORDER],
            *[outs_m[n] for n in _ORDER], *[outs_v[n] for n in _ORDER])
```

```python
import functools

import numpy as np
import jax
import jax.numpy as jnp
from jax import lax
from jax.experimental import pallas as pl
from jax.experimental.pallas import tpu as pltpu

F32 = jnp.float32
BF16 = jnp.bfloat16

D_MODEL = 1024
DEPTH = 4
N_HEADS = 8
QK_DIM = 96
NOPE_DIM = 64
ROPE_DIM = 32
V_DIM = 64
Q_RANK = 384
KV_RANK = 256
CONV_W = 512
MEM_HEADS = 4
MEM_HD = 128
MEM_W = 512
IN_WIDTH = 7328
PW = 8192
HP = 128
PROJ_HALO = 16
F32_HALO = 8
EPS = 1e-6
ROPE_BASE = 10000.0
SCALE = QK_DIM ** -0.5
MEM_SCALE = MEM_HD ** -0.5

ADAM_LR = 0.001
ADAM_B1 = 0.9
ADAM_B2 = 0.999
ADAM_EPS = 1e-08
ADAM_WD = 0.01
ADAM_STEP = 10

VMEM_LIMIT_V7X = 56 * 1024 * 1024

O_R, O_GA, O_CB, O_QM, O_GC, O_GM, O_CC, O_CU, O_QL, O_KVL, O_KPE = (
    0, 3072, 4096, 4608, 5120, 5632, 6144, 6656, 7168, 7552, 7808)


def _params(sem, vmem=VMEM_LIMIT_V7X):
    return pltpu.CompilerParams(dimension_semantics=sem, vmem_limit_bytes=vmem)


def _sigmoid(t):
    return 0.5 * jnp.tanh(0.5 * t) + 0.5


def _silu_and_grad(g):
    sg = _sigmoid(g)
    return g * sg, sg * (1.0 + g * (1.0 - sg))


def _rms(t, g, n=None):
    n = t.shape[-1] if n is None else n
    r = lax.rsqrt(jnp.sum(t * t, axis=-1, keepdims=True) * (1.0 / n) + EPS)
    return (t * r) * g


def _rms_parts(t, n=None):
    n = t.shape[-1] if n is None else n
    r = lax.rsqrt(jnp.sum(t * t, axis=-1, keepdims=True) * (1.0 / n) + EPS)
    return r, t * r


def _rms_bwd(dhat, hat, r, n):
    return r * (dhat - hat * (jnp.sum(dhat * hat, axis=-1, keepdims=True) * (1.0 / n)))


def _rope(t, c, sa, sb):
    return t * c + pltpu.roll(t, HP - 16, 1) * sa + pltpu.roll(t, 16, 1) * sb


def _rope_t(d, c, sa, sb):
    return d * c + pltpu.roll(d * sa, 16, 1) + pltpu.roll(d * sb, HP - 16, 1)


def _dot(a, b):
    return jnp.dot(a, b, preferred_element_type=F32)


def _dot_nt(a, b):
    return lax.dot_general(a, b, (((1,), (1,)), ((), ())), preferred_element_type=F32)


def _dot_tn(a, b):
    return lax.dot_general(a, b, (((0,), (0,)), ((), ())), preferred_element_type=F32)


def _colsum(t):
    return jnp.sum(t, axis=0, keepdims=True)


def _tile(n, t):
    t = min(n, t)
    assert n % t == 0, (n, t)
    return t


def _rope_tables(pos_b, invf):
    S = pos_b.shape[0]
    tm = _tile(S, 1024)

    def body(pos_ref, invf_ref, c_ref, sa_ref, sb_ref):
        ang = pos_ref[...].astype(F32) * invf_ref[...]
        lane = lax.broadcasted_iota(jnp.int32, ang.shape, 1)
        cs = jnp.cos(ang)
        sn = jnp.sin(ang)
        c_ref[...] = jnp.where(lane < NOPE_DIM, 1.0, jnp.where(lane < QK_DIM, cs, 0.0))
        sa_ref[...] = jnp.where((lane >= NOPE_DIM) & (lane < NOPE_DIM + 16), -sn, 0.0)
        sb_ref[...] = jnp.where((lane >= NOPE_DIM + 16) & (lane < QK_DIM), sn, 0.0)

    blk = pl.BlockSpec((tm, HP), lambda i: (i, 0))
    return pl.pallas_call(
        body, name="rope_tables", grid=(S // tm,),
        in_specs=[blk, pl.BlockSpec((1, HP), lambda i: (0, 0))],
        out_specs=[blk, blk, blk],
        out_shape=[jax.ShapeDtypeStruct((S, HP), F32)] * 3,
        compiler_params=_params(("parallel",)),
    )(pos_b, invf)


MESH = pl.DeviceIdType.MESH
HBM = pl.BlockSpec(memory_space=pltpu.HBM)


def _chip_peer(x, y, c, k):
    return (x ^ (k >> 1), y ^ (k & 1), c)


def _chip_copies(ins, outs, send, recv, loc, gather):
    x, y, c = lax.axis_index("x"), lax.axis_index("y"), lax.axis_index("c")
    me = 2 * x + y
    cps = []
    for a in range(len(ins)):
        cps.append(pltpu.make_async_copy(ins[a] if gather else ins[a].at[me], outs[a].at[me], loc.at[a]))
        for k in (1, 2, 3):
            cps.append(pltpu.make_async_remote_copy(
                src_ref=ins[a] if gather else ins[a].at[me ^ k], dst_ref=outs[a].at[me],
                send_sem=send.at[3 * a + k - 1], recv_sem=recv.at[3 * a + k - 1],
                device_id=_chip_peer(x, y, c, k), device_id_type=MESH))
    return cps


def _carried(body, n_in, n_out, carry, gather):
    n = len(carry)

    def wrapped(*refs):
        ins, cin = refs[:n_in], refs[n_in:n_in + n]
        outs, cout = refs[n_in + n:n_in + n + n_out], refs[n_in + n + n_out:n_in + 2 * n + n_out]
        send, recv, loc = refs[n_in + 2 * n + n_out:]
        i = pl.program_id(0)

        @pl.when(i == 0)
        def _():
            for cp in _chip_copies(cin, cout, send, recv, loc, gather):
                cp.start()

        body(*ins, *outs)

        @pl.when(i == pl.num_programs(0) - 1)
        def _():
            for cp in _chip_copies(cin, cout, send, recv, loc, gather):
                cp.wait()

    specs = dict(
        in_specs=[HBM] * n, out_specs=[HBM] * n,
        out_shape=[jax.ShapeDtypeStruct(((4,) + a.shape) if gather else a.shape, a.dtype) for a in carry],
        scratch_shapes=[pltpu.SemaphoreType.DMA((3 * n,)), pltpu.SemaphoreType.DMA((3 * n,)),
                        pltpu.SemaphoreType.DMA((n,))])
    return wrapped, specs


def _inproj(x, g, w, carry=()):
    S = x.shape[0]
    tm = _tile(S, 512)

    def body(x_ref, g_ref, w_ref, proj_ref, xn_ref):
        h = _rms(x_ref[...], g_ref[...]).astype(BF16)
        xn_ref[...] = h
        proj_ref[...] = _dot(h, w_ref[...]).astype(BF16)

    row = lambda n: pl.BlockSpec((tm, n), lambda i: (i, 0))
    in_specs = [row(D_MODEL), pl.BlockSpec((1, D_MODEL), lambda i: (0, 0)), pl.BlockSpec((D_MODEL, PW), lambda i: (0, 0), pipeline_mode=pl.Buffered(1))]
    out_specs = [row(PW), row(D_MODEL)]
    out_shape = [jax.ShapeDtypeStruct((S, PW), BF16), jax.ShapeDtypeStruct((S, D_MODEL), BF16)]
    if not carry:
        proj, xn = pl.pallas_call(
            body, name="inproj", grid=(S // tm,), in_specs=in_specs, out_specs=out_specs, out_shape=out_shape,
            compiler_params=_params(("parallel",)))(x, g, w)
        return proj, xn, []
    wrapped, extra = _carried(body, 3, 2, carry, gather=True)
    res = pl.pallas_call(
        wrapped, name="inproj_gather", grid=(S // tm,), in_specs=in_specs + extra["in_specs"],
        out_specs=out_specs + extra["out_specs"], out_shape=out_shape + extra["out_shape"],
        scratch_shapes=extra["scratch_shapes"], compiler_params=_params(("arbitrary",)))(x, g, w, *carry)
    return res[0], res[1], list(res[2:])


def _mla_prep(proj, tabs, qg, kvg, qhg, khg, wuq, wuk, wuv):
    S = proj.shape[0]
    tm = _tile(S, 512)

    def body(a_ref, c_ref, sa_ref, sb_ref, qg_ref, kvg_ref, qhg_ref, khg_ref, wuq_ref, wuk_ref, wuv_ref,
             q_ref, k_ref, v_ref, qn_ref, kvn_ref, bound_ref, qt_ref, vt_ref):
        gq = jnp.max(jnp.abs(qhg_ref[...]), axis=-1, keepdims=True)
        gk = jnp.max(jnp.abs(khg_ref[...]), axis=-1, keepdims=True)
        bound_ref[...] = jnp.broadcast_to(gq * gk * (QK_DIM ** 0.5 * 1.01) + 1e-6, bound_ref.shape)
        ql = a_ref[:, 0:Q_RANK].astype(F32)
        kvl = a_ref[:, Q_RANK:Q_RANK + KV_RANK].astype(F32)
        kpe = a_ref[:, Q_RANK + KV_RANK:Q_RANK + KV_RANK + HP].astype(F32)
        qn = _rms(ql, qg_ref[...]).astype(BF16)
        kvn = _rms(kvl, kvg_ref[...]).astype(BF16)
        qn_ref[...] = qn
        kvn_ref[...] = kvn
        qraw = _dot(qn, wuq_ref[...])
        kn = _dot(kvn, wuk_ref[...])
        vf = _dot(kvn, wuv_ref[...])
        v_ref[...] = vf.astype(BF16)
        c, sa, sb = c_ref[...], sa_ref[...], sb_ref[...]
        for h in range(N_HEADS):
            sl = slice(h * HP, (h + 1) * HP)
            tq = _rms(qraw[:, sl], qhg_ref[...], QK_DIM)
            qh = _rope(tq, c, sa, sb) * SCALE
            q_ref[:, sl] = qh.astype(BF16)
            qt_ref[sl, :] = qh.T.astype(BF16)
            vt_ref[sl, :] = vf[:, sl].T.astype(BF16)
            tk = _rms(kn[:, sl] + kpe, khg_ref[...], QK_DIM)
            k_ref[:, sl] = _rope(tk, c, sa, sb).astype(BF16)

    row = lambda w: pl.BlockSpec((tm, w), lambda i: (i, 0))
    full = lambda a: pl.BlockSpec(a.shape, lambda i: (0,) * a.ndim)
    return pl.pallas_call(
        body, name="mla_prep", grid=(S // tm,),
        in_specs=[pl.BlockSpec((tm, 1024), lambda i: (i, 7)), row(HP), row(HP), row(HP),
                  full(qg), full(kvg), full(qhg), full(khg), full(wuq), full(wuk), full(wuv)],
        out_specs=[row(1024), row(1024), row(1024), row(Q_RANK), row(KV_RANK),
                   pl.BlockSpec((1, HP), lambda i: (0, 0)),
                   pl.BlockSpec((1024, tm), lambda i: (0, i)), pl.BlockSpec((1024, tm), lambda i: (0, i))],
        out_shape=[jax.ShapeDtypeStruct((S, 1024), BF16)] * 3
        + [jax.ShapeDtypeStruct((S, Q_RANK), BF16), jax.ShapeDtypeStruct((S, KV_RANK), BF16),
           jax.ShapeDtypeStruct((1, HP), F32)] + [jax.ShapeDtypeStruct((1024, S), BF16)] * 2,
        compiler_params=_params(("arbitrary",)),
    )(proj, *tabs, qg, kvg, qhg, khg, wuq, wuk, wuv)


SAFE_SCORE_BOUND = 30.0


def _attn_fwd(qt, k, vt, score_bound):
    S = k.shape[0]
    tq, tk = _tile(S, 1024), _tile(S, 4096)
    nk = S // tk

    def body(qt_ref, k_ref, vt_ref, bound_ref, o_ref, lse_ref, m_s, l_s, acc_s):
        qtv = qt_ref[...]
        bound = bound_ref[0:1, 0:1]
        safe = jnp.max(bound) <= SAFE_SCORE_BOUND
        l_s[...] = jnp.zeros(l_s.shape, F32)
        acc_s[...] = jnp.zeros(acc_s.shape, F32)

        def keys(c):
            return pl.ds(pl.multiple_of(c * tk, tk), tk)

        @pl.when(safe)
        def _():
            def step(c, carry):
                pt = jnp.exp(_dot(k_ref[keys(c), :], qtv) - bound)
                l_s[...] += jnp.sum(pt, axis=0, keepdims=True)
                acc_s[...] += _dot(vt_ref[:, keys(c)], pt.astype(BF16))
                return carry

            lax.fori_loop(0, nk, step, 0)
            m_s[...] = jnp.broadcast_to(bound, m_s.shape)

        @pl.when(jnp.logical_not(safe))
        def _():
            m_s[...] = jnp.full(m_s.shape, -jnp.inf, F32)

            def step(c, carry):
                st = _dot(k_ref[keys(c), :], qtv)
                m_prev = m_s[...]
                m_new = jnp.maximum(m_prev, jnp.max(st, axis=0, keepdims=True))
                alpha = jnp.exp(m_prev - m_new)
                pt = jnp.exp(st - m_new)
                l_s[...] = alpha * l_s[...] + jnp.sum(pt, axis=0, keepdims=True)
                acc_s[...] = alpha * acc_s[...] + _dot(vt_ref[:, keys(c)], pt.astype(BF16))
                m_s[...] = m_new
                return carry

            lax.fori_loop(0, nk, step, 0)

        o_ref[...] = (acc_s[...] / l_s[...]).T
        lse_row = m_s[...] + jnp.log(l_s[...])
        lse_ref[0] = jnp.broadcast_to(lse_row, (HP, tq)).T[:, 0:1]

    return pl.pallas_call(
        body, name="attn_fwd", grid=(N_HEADS, S // tq),
        in_specs=[pl.BlockSpec((HP, tq), lambda h, i: (h, i)),
                  pl.BlockSpec((S, HP), lambda h, i: (0, h)),
                  pl.BlockSpec((HP, S), lambda h, i: (h, 0)),
                  pl.BlockSpec((1, HP), lambda h, i: (0, 0))],
        out_specs=[pl.BlockSpec((tq, HP), lambda h, i: (i, h)),
                   pl.BlockSpec((1, tq, 1), lambda h, i: (h, i, 0))],
        out_shape=[jax.ShapeDtypeStruct((S, N_HEADS * HP), F32),
                   jax.ShapeDtypeStruct((N_HEADS, S, 1), F32)],
        scratch_shapes=[pltpu.VMEM((1, tq), F32), pltpu.VMEM((1, tq), F32), pltpu.VMEM((HP, tq), F32)],
        compiler_params=_params(("parallel", "parallel")),
    )(qt, k, vt, score_bound)


def _memkv(mem, mng, mkg, wmkv):
    M = mem.shape[0]

    def body(mem_ref, mng_ref, mkg_ref, w_ref, mk_ref, mv_ref):
        mn = _rms(mem_ref[...], mng_ref[...]).astype(BF16)
        mkv = _dot(mn, w_ref[...])
        for h in range(MEM_HEADS):
            kraw = mkv[:, 2 * MEM_HD * h:2 * MEM_HD * h + MEM_HD]
            mk_ref[:, MEM_HD * h:MEM_HD * (h + 1)] = _rms(kraw, mkg_ref[...]).astype(BF16)
            mv_ref[:, MEM_HD * h:MEM_HD * (h + 1)] = mkv[:, 2 * MEM_HD * h + MEM_HD:2 * MEM_HD * (h + 1)].astype(BF16)

    return pl.pallas_call(
        body, name="memkv",
        out_shape=[jax.ShapeDtypeStruct((M, MEM_W), BF16)] * 2,
        compiler_params=pltpu.CompilerParams(vmem_limit_bytes=VMEM_LIMIT_V7X),
    )(mem, mng, mkg, wmkv)


def _conv_shifts(cc, cu, hp_ref, hn_ref, i, n_tiles, tm):
    z = cc * cu
    last = PROJ_HALO - 1
    zp = hp_ref[last:last + 1, 0:CONV_W].astype(F32) * hp_ref[last:last + 1, CONV_W:2 * CONV_W].astype(F32)
    zn = hn_ref[0:1, 0:CONV_W].astype(F32) * hn_ref[0:1, CONV_W:2 * CONV_W].astype(F32)
    zp = jnp.where(i == 0, 0.0, zp)
    zn = jnp.where(i == n_tiles - 1, 0.0, zn)
    row = lax.broadcasted_iota(jnp.int32, z.shape, 0)
    z_up = jnp.where(row == 0, zp, pltpu.roll(z, 1, 0))
    z_dn = jnp.where(row == tm - 1, zn, pltpu.roll(z, tm - 1, 0))
    return z, z_up, z_dn


def _halo_specs(tm, S, width, col, rows):
    per = tm // rows
    prev = pl.BlockSpec((rows, width), lambda i: (jnp.maximum(i * per - 1, 0), col))
    nxt = pl.BlockSpec((rows, width), lambda i: (jnp.minimum((i + 1) * per, S // rows - 1), col))
    return prev, nxt


def _mem_attend(qm, mqg, mk_h, mv_h):
    r, qhat = _rms_parts(qm)
    mq = (qhat * mqg).astype(BF16)
    s = _dot_nt(mq, mk_h) * MEM_SCALE
    e = jnp.exp(s - jnp.max(s, axis=-1, keepdims=True))
    p = e / jnp.sum(e, axis=-1, keepdims=True)
    pv = _dot(p.astype(BF16), mv_h)
    return r, qhat, mq, p, pv


def _merge(proj, o, x, bg, convw, mqg, mk, mv, wba, wbc, wbm, wo):
    S = x.shape[0]
    tm = _tile(S, 512)
    nt = S // tm

    def body(main_ref, ccu_ref, hp_ref, hn_ref, o_ref, x_ref, bg_ref, cw_ref, mqg_ref, mk_ref, mv_ref,
             wba_ref, wbc_ref, wbm_ref, wo_ref, xn_ref, oa_ref, oc_ref, om_ref, u_ref, y_ref):
        i = pl.program_id(0)
        sil_a, _ = _silu_and_grad(main_ref[:, O_GA:O_GA + 1024].astype(F32))
        oa = (o_ref[...] * sil_a).astype(BF16)
        oa_ref[...] = oa
        z, z_up, z_dn = _conv_shifts(ccu_ref[:, 0:CONV_W].astype(F32), ccu_ref[:, CONV_W:].astype(F32), hp_ref, hn_ref, i, nt, tm)
        cv = cw_ref[0:1, :] * z_up + cw_ref[1:2, :] * z + cw_ref[2:3, :] * z_dn + cw_ref[3:4, :]
        sil_c, _ = _silu_and_grad(main_ref[:, O_GC:O_GC + CONV_W].astype(F32))
        oc = (main_ref[:, O_CB:O_CB + CONV_W].astype(F32) * cv * sil_c).astype(BF16)
        oc_ref[...] = oc
        sil_m, _ = _silu_and_grad(main_ref[:, O_GM:O_GM + MEM_W].astype(F32))
        for h in range(MEM_HEADS):
            sl = slice(h * MEM_HD, (h + 1) * MEM_HD)
            qm = main_ref[:, O_QM + h * MEM_HD:O_QM + (h + 1) * MEM_HD].astype(F32)
            pv = _mem_attend(qm, mqg_ref[...], mk_ref[:, sl], mv_ref[:, sl])[4]
            om_ref[:, sl] = (pv * sil_m[:, sl]).astype(BF16)
        ua = _dot(oa, wba_ref[...])
        uc = _dot(oc, wbc_ref[...])
        um = _dot(om_ref[...], wbm_ref[...])
        u_ref[:, 0:1024] = ua.astype(BF16)
        u_ref[:, 1024:2048] = uc.astype(BF16)
        u_ref[:, 2048:3072] = um.astype(BF16)
        rg = _sigmoid(main_ref[:, O_R:O_R + 3072].astype(F32) + bg_ref[...])
        y = (rg[:, 0:1024] * ua + rg[:, 1024:2048] * uc + rg[:, 2048:3072] * um).astype(BF16)
        y_ref[...] = y
        xn_ref[...] = x_ref[...] + _dot(y, wo_ref[...])

    row = lambda w: pl.BlockSpec((tm, w), lambda i: (i, 0))
    full = lambda a: pl.BlockSpec(a.shape, lambda i: (0,) * a.ndim)
    hp, hn = _halo_specs(tm, S, 1024, 6, PROJ_HALO)
    return pl.pallas_call(
        body, name="merge", grid=(nt,),
        in_specs=[row(6144), pl.BlockSpec((tm, 1024), lambda i: (i, 6)), hp, hn, row(1024), row(1024),
                  full(bg), full(convw), full(mqg), full(mk), full(mv), full(wba), full(wbc), full(wbm), full(wo)],
        out_specs=[row(1024), row(1024), row(CONV_W), row(MEM_W), row(3072), row(1024)],
        out_shape=[jax.ShapeDtypeStruct((S, 1024), F32), jax.ShapeDtypeStruct((S, 1024), BF16),
                   jax.ShapeDtypeStruct((S, CONV_W), BF16), jax.ShapeDtypeStruct((S, MEM_W), BF16),
                   jax.ShapeDtypeStruct((S, 3072), BF16), jax.ShapeDtypeStruct((S, 1024), BF16)],
        compiler_params=_params(("parallel",)),
    )(proj, proj, proj, proj, o, x, bg, convw, mqg, mk, mv, wba, wbc, wbm, wo)


def _loss_head(xf, tgt):
    S = xf.shape[0]
    tm = _tile(S, 1024)

    def body(x_ref, t_ref, g_ref, acc_ref):
        @pl.when(pl.program_id(0) == 0)
        def _():
            acc_ref[...] = jnp.zeros(acc_ref.shape, F32)

        e = x_ref[...] - t_ref[...]
        g_ref[...] = e * (1.0 / D_MODEL)
        part = jnp.sum((e * e).reshape(tm // 8, 8, D_MODEL), axis=0)
        tot = part[:, 0:128]
        for k in range(1, D_MODEL // 128):
            tot = tot + part[:, 128 * k:128 * (k + 1)]
        acc_ref[...] += tot

    row = pl.BlockSpec((tm, D_MODEL), lambda i: (i, 0))
    return pl.pallas_call(
        body, name="loss_head", grid=(S // tm,),
        in_specs=[row, row],
        out_specs=[row, pl.BlockSpec((8, 128), lambda i: (0, 0))],
        out_shape=[jax.ShapeDtypeStruct((S, D_MODEL), F32), jax.ShapeDtypeStruct((8, 128), F32)],
        compiler_params=_params(("arbitrary",)),
    )(xf, tgt)


def _merge_bwd(g, proj, o, u, bg, convw, mqg, mk, mv, wot, wbat, wbct, wbmt):
    S = g.shape[0]
    tm = _tile(S, 256)
    nt = S // tm
    M = mk.shape[0]

    def body(g_ref, main_ref, ccu_ref, hp_ref, hn_ref, o_ref, u_ref, bg_ref, cw_ref, mqg_ref, mk_ref, mv_ref,
             wot_ref, wbat_ref, wbct_ref, wbmt_ref,
             dmain_ref, dcv_ref, do_ref, delta_ref, du_ref, dbg_ref, dmk_ref, dmv_ref, dmqg_ref, dot_ref):
        i = pl.program_id(0)

        @pl.when(i == 0)
        def _():
            dbg_ref[...] = jnp.zeros(dbg_ref.shape, F32)
            dmk_ref[...] = jnp.zeros(dmk_ref.shape, F32)
            dmv_ref[...] = jnp.zeros(dmv_ref.shape, F32)
            dmqg_ref[...] = jnp.zeros(dmqg_ref.shape, F32)

        dy = _dot(g_ref[...].astype(BF16), wot_ref[...])
        d_branch = []
        for b, wt_ref in enumerate((wbat_ref, wbct_ref, wbmt_ref)):
            cols = slice(O_R + b * D_MODEL, O_R + (b + 1) * D_MODEL)
            rg = _sigmoid(main_ref[:, cols].astype(F32) + bg_ref[:, cols])
            dr = dy * u_ref[:, cols].astype(F32) * rg * (1.0 - rg)
            dmain_ref[:, cols] = dr.astype(BF16)
            dbg_ref[:, cols] += _colsum(dr)
            du = (dy * rg).astype(BF16)
            du_ref[:, cols] = du
            d_branch.append(_dot(du, wt_ref[...]))
        do_a, do_c, do_m = d_branch

        sil_a, dsil_a = _silu_and_grad(main_ref[:, O_GA:O_GA + 1024].astype(F32))
        ov = o_ref[...]
        d_o = do_a * sil_a
        do_ref[...] = d_o.astype(BF16)
        dot_ref[...] = d_o.T.astype(BF16)
        dmain_ref[:, O_GA:O_GA + 1024] = (do_a * ov * dsil_a).astype(BF16)
        prod = d_o * ov
        for h in range(N_HEADS):
            delta_ref[h] = jnp.sum(prod[:, h * HP:(h + 1) * HP], axis=-1, keepdims=True)

        z, z_up, z_dn = _conv_shifts(ccu_ref[:, 0:CONV_W].astype(F32), ccu_ref[:, CONV_W:].astype(F32), hp_ref, hn_ref, i, nt, tm)
        cv = cw_ref[0:1, :] * z_up + cw_ref[1:2, :] * z + cw_ref[2:3, :] * z_dn + cw_ref[3:4, :]
        sil_c, dsil_c = _silu_and_grad(main_ref[:, O_GC:O_GC + CONV_W].astype(F32))
        cb = main_ref[:, O_CB:O_CB + CONV_W].astype(F32)
        dmain_ref[:, O_CB:O_CB + CONV_W] = (do_c * cv * sil_c).astype(BF16)
        dmain_ref[:, O_GC:O_GC + CONV_W] = (do_c * cb * cv * dsil_c).astype(BF16)
        dcv_ref[...] = do_c * cb * sil_c

        sil_m, dsil_m = _silu_and_grad(main_ref[:, O_GM:O_GM + MEM_W].astype(F32))
        for h in range(MEM_HEADS):
            sl = slice(h * MEM_HD, (h + 1) * MEM_HD)
            qm = main_ref[:, O_QM + h * MEM_HD:O_QM + (h + 1) * MEM_HD].astype(F32)
            mk_h, mv_h = mk_ref[:, sl], mv_ref[:, sl]
            r, qhat, mq, p, pv = _mem_attend(qm, mqg_ref[...], mk_h, mv_h)
            dom = do_m[:, sl]
            dmain_ref[:, O_GM + h * MEM_HD:O_GM + (h + 1) * MEM_HD] = (dom * pv * dsil_m[:, sl]).astype(BF16)
            dpv = (dom * sil_m[:, sl]).astype(BF16)
            dp = _dot_nt(dpv, mv_h)
            ds = (p * (dp - jnp.sum(dp * p, axis=-1, keepdims=True)) * MEM_SCALE).astype(BF16)
            dmq = _dot(ds, mk_h)
            dmk_ref[:, sl] += _dot_tn(ds, mq)
            dmv_ref[:, sl] += _dot_tn(p.astype(BF16), dpv)
            dmqg_ref[...] += _colsum(dmq * qhat)
            dqm = _rms_bwd(dmq * mqg_ref[...], qhat, r, MEM_HD)
            dmain_ref[:, O_QM + h * MEM_HD:O_QM + (h + 1) * MEM_HD] = dqm.astype(BF16)

    row = lambda w: pl.BlockSpec((tm, w), lambda i: (i, 0))
    full = lambda a: pl.BlockSpec(a.shape, lambda i: (0,) * a.ndim)
    acc = lambda r, c: pl.BlockSpec((r, c), lambda i: (0, 0))
    hp, hn = _halo_specs(tm, S, 1024, 6, PROJ_HALO)
    return pl.pallas_call(
        body, name="merge_bwd", grid=(nt,),
        in_specs=[row(1024), row(6144), pl.BlockSpec((tm, 1024), lambda i: (i, 6)), hp, hn, row(1024), row(3072),
                  full(bg), full(convw), full(mqg), full(mk), full(mv), full(wot), full(wbat), full(wbct), full(wbmt)],
        out_specs=[row(6144), row(CONV_W), row(1024), pl.BlockSpec((N_HEADS, tm, 1), lambda i: (0, i, 0)), row(3072),
                   acc(1, 3072), acc(M, MEM_W), acc(M, MEM_W), acc(1, MEM_HD),
                   pl.BlockSpec((1024, tm), lambda i: (0, i))],
        out_shape=[jax.ShapeDtypeStruct((S, 6144), BF16), jax.ShapeDtypeStruct((S, CONV_W), F32),
                   jax.ShapeDtypeStruct((S, 1024), BF16), jax.ShapeDtypeStruct((N_HEADS, S, 1), F32),
                   jax.ShapeDtypeStruct((S, 3072), BF16), jax.ShapeDtypeStruct((1, 3072), F32),
                   jax.ShapeDtypeStruct((M, MEM_W), F32), jax.ShapeDtypeStruct((M, MEM_W), F32),
                   jax.ShapeDtypeStruct((1, MEM_HD), F32), jax.ShapeDtypeStruct((1024, S), BF16)],
        compiler_params=_params(("arbitrary",)),
    )(g, proj, proj, proj, proj, o, u, bg, convw, mqg, mk, mv, wot, wbat, wbct, wbmt)


def _conv_bwd(dcv, proj, convw):
    S = dcv.shape[0]
    tm = _tile(S, 512)
    nt = S // tm

    def body(d_ref, dp_ref, dn_ref, ccu_ref, hp_ref, hn_ref, cw_ref, dccu_ref, dcw_ref):
        i = pl.program_id(0)

        @pl.when(i == 0)
        def _():
            dcw_ref[...] = jnp.zeros(dcw_ref.shape, F32)

        cc, cu = ccu_ref[:, 0:CONV_W].astype(F32), ccu_ref[:, CONV_W:].astype(F32)
        z, z_up, z_dn = _conv_shifts(cc, cu, hp_ref, hn_ref, i, nt, tm)
        d = d_ref[...]
        dprev = jnp.where(i == 0, 0.0, dp_ref[7:8, :])
        dnext = jnp.where(i == nt - 1, 0.0, dn_ref[0:1, :])
        row = lax.broadcasted_iota(jnp.int32, d.shape, 0)
        d_up = jnp.where(row == 0, dprev, pltpu.roll(d, 1, 0))
        d_dn = jnp.where(row == tm - 1, dnext, pltpu.roll(d, tm - 1, 0))
        dz = cw_ref[0:1, :] * d_dn + cw_ref[1:2, :] * d + cw_ref[2:3, :] * d_up
        dccu_ref[:, 0:CONV_W] = (dz * cu).astype(BF16)
        dccu_ref[:, CONV_W:] = (dz * cc).astype(BF16)
        dcw_ref[0:1, :] += _colsum(d * z_up)
        dcw_ref[1:2, :] += _colsum(d * z)
        dcw_ref[2:3, :] += _colsum(d * z_dn)
        dcw_ref[3:4, :] += _colsum(d)

    hp, hn = _halo_specs(tm, S, 1024, 6, PROJ_HALO)
    dp, dn = _halo_specs(tm, S, CONV_W, 0, F32_HALO)
    return pl.pallas_call(
        body, name="conv_bwd", grid=(nt,),
        in_specs=[pl.BlockSpec((tm, CONV_W), lambda i: (i, 0)), dp, dn,
                  pl.BlockSpec((tm, 1024), lambda i: (i, 6)), hp, hn,
                  pl.BlockSpec((8, CONV_W), lambda i: (0, 0))],
        out_specs=[pl.BlockSpec((tm, 1024), lambda i: (i, 0)), pl.BlockSpec((8, CONV_W), lambda i: (0, 0))],
        out_shape=[jax.ShapeDtypeStruct((S, 1024), BF16), jax.ShapeDtypeStruct((8, CONV_W), F32)],
        compiler_params=_params(("arbitrary",)),
    )(dcv, dcv, dcv, proj, proj, proj, convw)


def _attn_bwd(q, k, v, do, qt, dot, lse, delta):
    S = q.shape[0]
    tq, tk = _tile(S, 512), _tile(S, 4096)
    ni, nk = S // tq, S // tk

    def body(q_ref, do_ref, qt_ref, dot_ref, k_ref, v_ref, lse_ref, delta_ref, dq_ref, dkt_hbm, dvt_hbm,
             dq_s, dkt_s, dvt_s, sem):
        h, i = pl.program_id(0), pl.program_id(1)

        @pl.when(i == 0)
        def _():
            dkt_s[...] = jnp.zeros(dkt_s.shape, F32)
            dvt_s[...] = jnp.zeros(dvt_s.shape, F32)

        dq_s[...] = jnp.zeros(dq_s.shape, F32)
        qv, dov, qtv, dotv = q_ref[...], do_ref[...], qt_ref[...], dot_ref[...]
        lse_c, delta_c = lse_ref[0], delta_ref[0]

        def step(c, carry):
            cols = pl.ds(pl.multiple_of(c * tk, tk), tk)
            kc, vc = k_ref[cols, :], v_ref[cols, :]
            p = jnp.exp(_dot_nt(qv, kc) - lse_c)
            dp = _dot_nt(dov, vc)
            ds = (p * (dp - delta_c)).astype(BF16)
            dq_s[...] += _dot(ds, kc)
            dvt_s[:, cols] += _dot(dotv, p.astype(BF16))
            dkt_s[:, cols] += _dot(qtv, ds)
            return carry

        lax.fori_loop(0, nk, step, 0)
        dq_ref[...] = dq_s[...]

        @pl.when(i == ni - 1)
        def _():
            head = pl.ds(pl.multiple_of(h * HP, HP), HP)
            out_k = pltpu.make_async_copy(dkt_s, dkt_hbm.at[head, :], sem.at[0])
            out_v = pltpu.make_async_copy(dvt_s, dvt_hbm.at[head, :], sem.at[1])
            out_k.start()
            out_v.start()
            out_k.wait()
            out_v.wait()

    col = pl.BlockSpec((1, tq, 1), lambda h, i: (h, i, 0))
    blk = pl.BlockSpec((tq, HP), lambda h, i: (i, h))
    blkt = pl.BlockSpec((HP, tq), lambda h, i: (h, i))
    res = pl.BlockSpec((S, HP), lambda h, i: (0, h))
    whole = pl.BlockSpec(memory_space=pl.ANY)
    return pl.pallas_call(
        body, name="attn_bwd", grid=(N_HEADS, ni),
        in_specs=[blk, blk, blkt, blkt, res, res, col, col],
        out_specs=[blk, whole, whole],
        out_shape=[jax.ShapeDtypeStruct((S, N_HEADS * HP), F32), jax.ShapeDtypeStruct((N_HEADS * HP, S), F32),
                   jax.ShapeDtypeStruct((N_HEADS * HP, S), F32)],
        scratch_shapes=[pltpu.VMEM((tq, HP), F32), pltpu.VMEM((HP, S), F32), pltpu.VMEM((HP, S), F32),
                        pltpu.SemaphoreType.DMA((2,))],
        compiler_params=_params(("parallel", "arbitrary")),
    )(q, do, qt, dot, k, v, lse, delta)


def _mla_prep_bwd(proj, tabs, dq, dkt, dvt, qg, kvg, qhg, khg, wuq, wuk, wuqt, wukt, wuvt):
    S = proj.shape[0]
    tm = _tile(S, 512)

    def body(a_ref, c_ref, sa_ref, sb_ref, dq_ref, dkt_ref, dvt_ref, qg_ref, kvg_ref, qhg_ref, khg_ref,
             wuq_ref, wuk_ref, wuqt_ref, wukt_ref, wuvt_ref,
             da_ref, dqraw_ref, dkraw_ref, dqg_ref, dkvg_ref, dqhg_ref, dkhg_ref, dv_ref):
        @pl.when(pl.program_id(0) == 0)
        def _():
            dqg_ref[...] = jnp.zeros(dqg_ref.shape, F32)
            dkvg_ref[...] = jnp.zeros(dkvg_ref.shape, F32)
            dqhg_ref[...] = jnp.zeros(dqhg_ref.shape, F32)
            dkhg_ref[...] = jnp.zeros(dkhg_ref.shape, F32)

        ql = a_ref[:, 0:Q_RANK].astype(F32)
        kvl = a_ref[:, Q_RANK:Q_RANK + KV_RANK].astype(F32)
        kpe = a_ref[:, Q_RANK + KV_RANK:Q_RANK + KV_RANK + HP].astype(F32)
        rq, qhat = _rms_parts(ql)
        rkv, kvhat = _rms_parts(kvl)
        qraw = _dot((qhat * qg_ref[...]).astype(BF16), wuq_ref[...])
        kn = _dot((kvhat * kvg_ref[...]).astype(BF16), wuk_ref[...])
        c, sa, sb = c_ref[...], sa_ref[...], sb_ref[...]
        dkpe = jnp.zeros(kpe.shape, F32)
        dqhg = jnp.zeros((1, HP), F32)
        dkhg = jnp.zeros((1, HP), F32)
        for h in range(N_HEADS):
            sl = slice(h * HP, (h + 1) * HP)
            r, that = _rms_parts(qraw[:, sl], QK_DIM)
            dtn = _rope_t(dq_ref[:, sl], c, sa, sb) * SCALE
            dqhg = dqhg + _colsum(dtn * that)
            dqraw_ref[:, sl] = _rms_bwd(dtn * qhg_ref[...], that, r, QK_DIM).astype(BF16)
            r, that = _rms_parts(kn[:, sl] + kpe, QK_DIM)
            dtn = _rope_t(dkt_ref[sl, :].T, c, sa, sb)
            dv_ref[:, sl] = dvt_ref[sl, :].T.astype(BF16)
            dkhg = dkhg + _colsum(dtn * that)
            dkr = _rms_bwd(dtn * khg_ref[...], that, r, QK_DIM)
            dkraw_ref[:, sl] = dkr.astype(BF16)
            dkpe = dkpe + dkr
        dqhg_ref[...] += dqhg
        dkhg_ref[...] += dkhg
        dqn = _dot(dqraw_ref[...], wuqt_ref[...])
        dqg_ref[...] += _colsum(dqn * qhat)
        da_ref[:, 0:Q_RANK] = _rms_bwd(dqn * qg_ref[...], qhat, rq, Q_RANK).astype(BF16)
        dkvn = _dot(dkraw_ref[...], wukt_ref[...]) + _dot(dv_ref[...], wuvt_ref[...])
        dkvg_ref[...] += _colsum(dkvn * kvhat)
        da_ref[:, Q_RANK:Q_RANK + KV_RANK] = _rms_bwd(dkvn * kvg_ref[...], kvhat, rkv, KV_RANK).astype(BF16)
        da_ref[:, Q_RANK + KV_RANK:Q_RANK + KV_RANK + HP] = dkpe.astype(BF16)
        da_ref[:, Q_RANK + KV_RANK + HP:] = jnp.zeros((tm, 1024 - Q_RANK - KV_RANK - HP), BF16)

    row = lambda w: pl.BlockSpec((tm, w), lambda i: (i, 0))
    full = lambda a: pl.BlockSpec(a.shape, lambda i: (0,) * a.ndim)
    acc = lambda c: pl.BlockSpec((1, c), lambda i: (0, 0))
    return pl.pallas_call(
        body, name="mla_prep_bwd", grid=(S // tm,),
        in_specs=[pl.BlockSpec((tm, 1024), lambda i: (i, 7)), row(HP), row(HP), row(HP),
                  row(1024), pl.BlockSpec((1024, tm), lambda i: (0, i)), pl.BlockSpec((1024, tm), lambda i: (0, i)),
                  full(qg), full(kvg), full(qhg), full(khg),
                  full(wuq), full(wuk), full(wuqt), full(wukt), full(wuvt)],
        out_specs=[row(1024), row(1024), row(1024), acc(Q_RANK), acc(KV_RANK), acc(HP), acc(HP), row(1024)],
        out_shape=[jax.ShapeDtypeStruct((S, 1024), BF16)] * 3
        + [jax.ShapeDtypeStruct((1, Q_RANK), F32), jax.ShapeDtypeStruct((1, KV_RANK), F32),
           jax.ShapeDtypeStruct((1, HP), F32), jax.ShapeDtypeStruct((1, HP), F32),
           jax.ShapeDtypeStruct((S, 1024), BF16)],
        compiler_params=_params(("arbitrary",)),
    )(proj, *tabs, dq, dkt, dvt, qg, kvg, qhg, khg, wuq, wuk, wuqt, wukt, wuvt)


def _inproj_bwd(dmain, dccu, dsega, wint, x, g, ng, carry=()):
    S = x.shape[0]
    tm = _tile(S, 512)
    nm, nc = dmain.shape[1], dccu.shape[1]
    assert nm + nc + dsega.shape[1] == PW

    def body(dm_ref, dc_ref, da_ref, w_ref, x_ref, g_ref, ng_ref, dx_ref, dng_ref):
        @pl.when(pl.program_id(0) == 0)
        def _():
            dng_ref[...] = jnp.zeros(dng_ref.shape, F32)

        dh = (_dot(dm_ref[...], w_ref[0:nm, :]) + _dot(dc_ref[...], w_ref[nm:nm + nc, :])
              + _dot(da_ref[...], w_ref[nm + nc:PW, :]))
        r, xhat = _rms_parts(x_ref[...])
        dng_ref[...] += _colsum(dh * xhat)
        dx_ref[...] = g_ref[...] + _rms_bwd(dh * ng_ref[...], xhat, r, D_MODEL)

    row = lambda w: pl.BlockSpec((tm, w), lambda i: (i, 0))
    in_specs = [row(nm), row(nc), row(dsega.shape[1]), pl.BlockSpec((PW, D_MODEL), lambda i: (0, 0), pipeline_mode=pl.Buffered(1)),
                row(D_MODEL), row(D_MODEL), pl.BlockSpec((1, D_MODEL), lambda i: (0, 0))]
    out_specs = [row(D_MODEL), pl.BlockSpec((1, D_MODEL), lambda i: (0, 0))]
    out_shape = [jax.ShapeDtypeStruct((S, D_MODEL), F32), jax.ShapeDtypeStruct((1, D_MODEL), F32)]
    args = (dmain, dccu, dsega, wint, x, g, ng)
    if not carry:
        dx, dng = pl.pallas_call(
            body, name="inproj_bwd", grid=(S // tm,), in_specs=in_specs, out_specs=out_specs, out_shape=out_shape,
            compiler_params=_params(("arbitrary",)))(*args)
        return dx, dng, []
    wrapped, extra = _carried(body, 7, 2, carry, gather=False)
    res = pl.pallas_call(
        wrapped, name="inproj_bwd_scatter", grid=(S // tm,), in_specs=in_specs + extra["in_specs"],
        out_specs=out_specs + extra["out_specs"], out_shape=out_shape + extra["out_shape"],
        scratch_shapes=extra["scratch_shapes"], compiler_params=_params(("arbitrary",)))(*args, *carry)
    return res[0], res[1], list(res[2:])


def _memkv_bwd(mem, mng, mkg, wmkv, wmkvt, dmk, dmv):
    M = mem.shape[0]

    def body(mem_ref, mng_ref, mkg_ref, w_ref, wt_ref, dmk_ref, dmv_ref, dw_ref, dmng_ref, dmkg_ref, d_s):
        r, mhat = _rms_parts(mem_ref[...])
        mn = (mhat * mng_ref[...]).astype(BF16)
        mkv = _dot(mn, w_ref[...])
        dmkg = jnp.zeros((1, MEM_HD), F32)
        for h in range(MEM_HEADS):
            sl = slice(h * MEM_HD, (h + 1) * MEM_HD)
            rk, khat = _rms_parts(mkv[:, 2 * MEM_HD * h:2 * MEM_HD * h + MEM_HD])
            dkn = dmk_ref[:, sl]
            dmkg = dmkg + _colsum(dkn * khat)
            d_s[:, 2 * MEM_HD * h:2 * MEM_HD * h + MEM_HD] = _rms_bwd(dkn * mkg_ref[...], khat, rk, MEM_HD).astype(BF16)
            d_s[:, 2 * MEM_HD * h + MEM_HD:2 * MEM_HD * (h + 1)] = dmv_ref[:, sl].astype(BF16)
        dmkg_ref[...] = dmkg
        dw_ref[...] = _dot_tn(mn, d_s[...])
        dmn = _dot(d_s[...], wt_ref[...])
        dmng_ref[...] = _colsum(dmn * mhat)

    return pl.pallas_call(
        body, name="memkv_bwd",
        out_shape=[jax.ShapeDtypeStruct((D_MODEL, 2 * MEM_W), F32), jax.ShapeDtypeStruct((1, D_MODEL), F32),
                   jax.ShapeDtypeStruct((1, MEM_HD), F32)],
        scratch_shapes=[pltpu.VMEM((M, 2 * MEM_W), BF16)],
        compiler_params=pltpu.CompilerParams(vmem_limit_bytes=VMEM_LIMIT_V7X),
    )(mem, mng, mkg, wmkv, wmkvt, dmk, dmv)


def _mm_tn(a, b, name, col0=0, ncols=None):
    S, M = a.shape
    N = b.shape[1] if ncols is None else ncols
    tm, tn, ts = _tile(M, 1024), _tile(N, 1024), _tile(S, 2048)
    assert col0 % tn == 0
    jb = col0 // tn

    def body(a_ref, b_ref, o_ref):
        @pl.when(pl.program_id(2) == 0)
        def _():
            o_ref[...] = jnp.zeros(o_ref.shape, F32)

        o_ref[...] += _dot_tn(a_ref[...].astype(BF16), b_ref[...].astype(BF16))

    return pl.pallas_call(
        body, name=name, grid=(M // tm, N // tn, S // ts),
        in_specs=[pl.BlockSpec((ts, tm), lambda i, j, k: (k, i)),
                  pl.BlockSpec((ts, tn), lambda i, j, k: (k, j + jb))],
        out_specs=pl.BlockSpec((tm, tn), lambda i, j, k: (i, j)),
        out_shape=jax.ShapeDtypeStruct((M, N), F32),
        compiler_params=_params(("parallel", "parallel", "arbitrary")),
    )(a, b)


def _adamw(w, g0, g1, m, v, name):
    R, C = w.shape
    tr = R
    for cand in (512, 256, 128, 64, 32, 16, 8):
        if R % cand == 0 and cand * C * 4 <= (1 << 20):
            tr = cand
            break
    c1 = 1.0 / (1.0 - ADAM_B1 ** ADAM_STEP)
    c2 = 1.0 / (1.0 - ADAM_B2 ** ADAM_STEP)

    def body(w_ref, g0_ref, g1_ref, m_ref, v_ref, g_ref, d_ref, nm_ref, nv_ref):
        g = g0_ref[...] + g1_ref[...]
        nm = ADAM_B1 * m_ref[...] + (1.0 - ADAM_B1) * g
        nv = ADAM_B2 * v_ref[...] + (1.0 - ADAM_B2) * (g * g)
        g_ref[...] = g
        nm_ref[...] = nm
        nv_ref[...] = nv
        d_ref[...] = -ADAM_LR * ((nm * c1) / (jnp.sqrt(nv * c2) + ADAM_EPS) + ADAM_WD * w_ref[...])

    blk = pl.BlockSpec((tr, C), lambda i: (i, 0))
    return pl.pallas_call(
        body, name=name, grid=(R // tr,),
        in_specs=[blk] * 5, out_specs=[blk] * 4,
        out_shape=[jax.ShapeDtypeStruct((R, C), F32)] * 4,
        compiler_params=_params(("parallel",)),
    )(w, g0, g1, m, v)


def _sum_slabs(a, name):
    K, R, C = a.shape
    tr = R
    for cand in (512, 256, 128, 64, 32, 16, 8):
        if R % cand == 0 and cand * C * 4 * K <= (4 << 20):
            tr = cand
            break

    def body(a_ref, o_ref):
        t = a_ref[0].astype(F32)
        for k in range(1, K):
            t = t + a_ref[k].astype(F32)
        o_ref[...] = t

    return pl.pallas_call(
        body, name=name, grid=(R // tr,),
        in_specs=[pl.BlockSpec((K, tr, C), lambda i: (0, i, 0))],
        out_specs=pl.BlockSpec((tr, C), lambda i: (i, 0)),
        out_shape=jax.ShapeDtypeStruct((R, C), F32),
        compiler_params=_params(("parallel",)),
    )(a)


def _gather_chips(arrs):
    n = len(arrs)
    halves = [a.shape[0] // 2 for a in arrs]
    assert all(a.shape[0] == 2 * hf for a, hf in zip(arrs, halves))

    def body(*refs):
        ins, outs = refs[:n], refs[n:2 * n]
        send1, recv1, send2, recv2, loc = refs[2 * n:]
        x, y, c = lax.axis_index("x"), lax.axis_index("y"), lax.axis_index("c")
        me = 2 * x + y
        mine = [pl.ds(c * hf, hf) for hf in halves]
        theirs = [pl.ds((1 - c) * hf, hf) for hf in halves]
        sibling = (x, y, 1 - c)
        waits = []
        for a in range(n):
            own = pltpu.make_async_copy(ins[a], outs[a].at[me], loc.at[a])
            own.start()
            waits.append(own.wait)

        def over_ici(a, k, src_chip):
            return pltpu.make_async_remote_copy(
                src_ref=ins[a].at[mine[a]], dst_ref=outs[a].at[src_chip, mine[a]], send_sem=send1.at[3 * a + k - 1],
                recv_sem=recv1.at[3 * a + k - 1], device_id=_chip_peer(x, y, c, k), device_id_type=MESH)

        def to_sibling(a, k, layers):
            block = outs[a].at[me ^ k, layers]
            return pltpu.make_async_remote_copy(
                src_ref=block, dst_ref=block, send_sem=send2.at[3 * a + k - 1], recv_sem=recv2.at[3 * a + k - 1],
                device_id=sibling, device_id_type=MESH)

        for a in range(n):
            for k in (1, 2, 3):
                cp = over_ici(a, k, me)
                cp.start()
                waits.append(cp.wait_send)
        for a in range(n):
            for k in (1, 2, 3):
                over_ici(a, k, me ^ k).wait_recv()
                cp = to_sibling(a, k, mine[a])
                cp.start()
                waits.append(cp.wait_send)
        for a in range(n):
            for k in (1, 2, 3):
                to_sibling(a, k, theirs[a]).wait_recv()
        for w in waits:
            w()

    return pl.pallas_call(
        body, name="gather_weights",
        in_specs=[HBM] * n, out_specs=[HBM] * n,
        out_shape=[jax.ShapeDtypeStruct((4,) + a.shape, a.dtype) for a in arrs],
        scratch_shapes=[pltpu.SemaphoreType.DMA((3 * n,)), pltpu.SemaphoreType.DMA((3 * n,)),
                        pltpu.SemaphoreType.DMA((3 * n,)), pltpu.SemaphoreType.DMA((3 * n,)),
                        pltpu.SemaphoreType.DMA((n,))],
    )(*arrs)


def _scatter_chips(arrs, small):
    n = len(arrs)

    def body(*refs):
        ins, small_in = refs[:n], refs[n]
        outs, small_out = refs[n + 1:2 * n + 1], refs[2 * n + 1]
        send, recv, loc, ssend, srecv = refs[2 * n + 2:]
        x, y, c = lax.axis_index("x"), lax.axis_index("y"), lax.axis_index("c")
        me = 2 * x + y
        me8 = 4 * x + 2 * y + c
        copies = []
        for a in range(n):
            own = pltpu.make_async_copy(ins[a].at[me], outs[a].at[me], loc.at[a])
            own.start()
            copies.append(own)
        own = pltpu.make_async_copy(small_in, small_out.at[me8], loc.at[n])
        own.start()
        copies.append(own)
        for k in range(1, 8):
            cp = pltpu.make_async_remote_copy(
                src_ref=small_in, dst_ref=small_out.at[me8], send_sem=ssend.at[k - 1], recv_sem=srecv.at[k - 1],
                device_id=(x ^ (k >> 2), y ^ ((k >> 1) & 1), c ^ (k & 1)), device_id_type=MESH)
            cp.start()
            copies.append(cp)
        for a in range(n):
            for k in (1, 2, 3):
                cp = pltpu.make_async_remote_copy(
                    src_ref=ins[a].at[me ^ k], dst_ref=outs[a].at[me], send_sem=send.at[3 * a + k - 1],
                    recv_sem=recv.at[3 * a + k - 1], device_id=_chip_peer(x, y, c, k), device_id_type=MESH)
                cp.start()
                copies.append(cp)
        for cp in copies:
            cp.wait()

    return pl.pallas_call(
        body, name="scatter_grads",
        in_specs=[HBM] * (n + 1), out_specs=[HBM] * (n + 1),
        out_shape=[jax.ShapeDtypeStruct(a.shape, a.dtype) for a in arrs]
        + [jax.ShapeDtypeStruct((8,) + small.shape, small.dtype)],
        scratch_shapes=[pltpu.SemaphoreType.DMA((3 * n,)), pltpu.SemaphoreType.DMA((3 * n,)),
                        pltpu.SemaphoreType.DMA((n + 1,)), pltpu.SemaphoreType.DMA((7,)),
                        pltpu.SemaphoreType.DMA((7,))],
    )(*arrs, small)


def _swap_cores(arrs):
    n = len(arrs)

    def body(*refs):
        ins, outs = refs[:n], refs[n:2 * n]
        send, recv = refs[2 * n:]
        x, y, c = lax.axis_index("x"), lax.axis_index("y"), lax.axis_index("c")
        copies = []
        for a in range(n):
            cp = pltpu.make_async_remote_copy(
                src_ref=ins[a], dst_ref=outs[a], send_sem=send.at[a], recv_sem=recv.at[a],
                device_id=(x, y, 1 - c), device_id_type=MESH)
            cp.start()
            copies.append(cp)
        for cp in copies:
            cp.wait()

    return pl.pallas_call(
        body, name="swap_cores",
        in_specs=[HBM] * n, out_specs=[HBM] * n,
        out_shape=[jax.ShapeDtypeStruct(a.shape, a.dtype) for a in arrs],
        scratch_shapes=[pltpu.SemaphoreType.DMA((n,)), pltpu.SemaphoreType.DMA((n,))],
    )(*arrs)


def _pad_last(a, n):
    return jnp.pad(a, [(0, 0)] * (a.ndim - 1) + [(0, n - a.shape[-1])])


def _pad_w_in(w):
    lead = w.shape[:-1]
    seg = lambda a, b: w[..., a:b]
    ga = _pad_last(seg(2720, 3232).reshape(lead + (N_HEADS, V_DIM)), HP).reshape(lead + (1024,))
    kpe = jnp.pad(seg(640, 672), [(0, 0)] * len(lead) + [(NOPE_DIM, HP - QK_DIM)])
    zero = jnp.zeros(lead + (PW - 7936,), w.dtype)
    return jnp.concatenate(
        [seg(4256, 7328), ga, seg(672, 1184), seg(2208, 2720), seg(3232, 3744), seg(3744, 4256),
         seg(1184, 1696), seg(1696, 2208), seg(0, 384), seg(384, 640), kpe, zero], axis=-1)


def _unpad_w_in(w):
    lead = w.shape[:-1]
    seg = lambda a, n: w[..., a:a + n]
    ga = seg(O_GA, 1024).reshape(lead + (N_HEADS, HP))[..., :V_DIM].reshape(lead + (N_HEADS * V_DIM,))
    return jnp.concatenate(
        [seg(O_QL, 384), seg(O_KVL, 256), seg(O_KPE + NOPE_DIM, ROPE_DIM), seg(O_CB, 512), seg(O_CC, 512),
         seg(O_CU, 512), seg(O_QM, 512), ga, seg(O_GC, 512), seg(O_GM, 512), seg(O_R, 3072)], axis=-1)


def _cols_from_shards(g):
    _, L, R, C = g.shape
    return jnp.transpose(g, (1, 2, 0, 3)).reshape(L, R, 4 * C)


def _t(w):
    return jnp.swapaxes(w, -1, -2)


def _layer_fwd(x, mem, tabs, p, next_shards=()):
    proj, xn, gathered = _inproj(x, p["norm_g"], p["w_in"], next_shards)
    q, k, v, qn, kvn, score_bound, qt, vt = _mla_prep(proj, tabs, p["q_norm_g"], p["kv_norm_g"], p["q_head_g"], p["k_head_g"],
                                        p["w_uq"], p["w_uk"], p["w_uv"])
    o, lse = _attn_fwd(qt, k, vt, score_bound)
    mk, mv = _memkv(mem, p["mem_norm_g"], p["mem_k_g"], p["w_mkv"])
    x_new, oa, oc, om, u, y = _merge(proj, o, x, p["b_gate"], p["conv_wb"], p["mem_q_g"], mk, mv,
                                     p["w_br_attn"], p["w_br_conv"], p["w_br_mem"], p["w_out"])
    saved = dict(x=x, proj=proj, xn=xn, q=q, qt=qt, k=k, v=v, qn=qn, kvn=kvn, o=o, lse=lse, mk=mk, mv=mv,
                 oa=oa, oc=oc, om=om, u=u, y=y)
    return x_new, saved, gathered


def _layer_bwd(g, mem, tabs, p, s, to_owner=()):
    S = g.shape[0]
    dmain, dcv, d_o, delta, du, dbg, dmk, dmv, dmqg, d_ot = _merge_bwd(
        g, s["proj"], s["o"], s["u"], p["b_gate"], p["conv_wb"], p["mem_q_g"], s["mk"], s["mv"],
        p["w_out_t"], p["w_br_attn_t"], p["w_br_conv_t"], p["w_br_mem_t"])
    dccu, dconv = _conv_bwd(dcv, s["proj"], p["conv_wb"])
    dq, dkt, dvt = _attn_bwd(s["q"], s["k"], s["v"], d_o, s["qt"], d_ot, s["lse"], delta)
    dsega, dqraw, dkraw, dqg, dkvg, dqhg, dkhg, dv = _mla_prep_bwd(
        s["proj"], tabs, dq, dkt, dvt, p["q_norm_g"], p["kv_norm_g"], p["q_head_g"], p["k_head_g"],
        p["w_uq"], p["w_uk"], p["w_uq_t"], p["w_uk_t"], p["w_uv_t"])
    dx, dng, received = _inproj_bwd(dmain, dccu, dsega, p["w_in_t"], s["x"], g, p["norm_g"], to_owner)
    dwmkv, dmng, dmkg = _memkv_bwd(mem, p["mem_norm_g"], p["mem_k_g"], p["w_mkv"], p["w_mkv_t"], dmk, dmv)
    grads = dict(
        norm_g=dng, b_gate=dbg, q_norm_g=dqg, kv_norm_g=dkvg, q_head_g=dqhg, k_head_g=dkhg,
        conv_wb=dconv, mem_norm_g=dmng, mem_q_g=dmqg, mem_k_g=dmkg, w_mkv=dwmkv,
        w_in=jnp.concatenate([_mm_tn(s["xn"], dmain, "grad_w_in"), _mm_tn(s["xn"], dccu, "grad_w_in_conv"),
                              _mm_tn(s["xn"], dsega, "grad_w_in_lat")], axis=1),
        w_uq=_mm_tn(s["qn"], dqraw, "grad_w_uq"),
        w_uk=_mm_tn(s["kvn"], dkraw, "grad_w_uk"),
        w_uv=_mm_tn(s["kvn"], dv, "grad_w_uv"),
        w_br_attn=_mm_tn(s["oa"], du, "grad_w_br_attn", 0, 1024),
        w_br_conv=_mm_tn(s["oc"], du, "grad_w_br_conv", 1024, 1024),
        w_br_mem=_mm_tn(s["om"], du, "grad_w_br_mem", 2048, 1024),
        w_out=_mm_tn(s["y"], g, "grad_w_out"),
    )
    return dx, grads, received


def _layer_params(big, full, l):
    p = {}
    w_in = _pad_w_in(big["w_in"])
    w_uq = _pad_last(big["w_uq"].reshape(Q_RANK, N_HEADS, QK_DIM), HP).reshape(Q_RANK, 1024)
    ukv = big["w_ukv"].reshape(KV_RANK, N_HEADS, NOPE_DIM + V_DIM)
    w_uk = _pad_last(ukv[..., :NOPE_DIM], HP).reshape(KV_RANK, 1024)
    w_uv = _pad_last(ukv[..., NOPE_DIM:], HP).reshape(KV_RANK, 1024)
    w_ba = jnp.pad(big["w_br_attn"].reshape(N_HEADS, V_DIM, D_MODEL), ((0, 0), (0, HP - V_DIM), (0, 0)))
    w_ba = w_ba.reshape(1024, D_MODEL)
    p.update(w_in=w_in, w_uq=w_uq, w_uk=w_uk, w_uv=w_uv, w_br_attn=w_ba, w_br_conv=big["w_br_conv"],
             w_br_mem=big["w_br_mem"], w_out=big["w_out"], w_mkv=big["w_mkv"])
    for n in ("w_in", "w_uq", "w_uk", "w_uv", "w_br_attn", "w_br_conv", "w_br_mem", "w_out", "w_mkv"):
        p[n + "_t"] = _t(p[n])
    for n in ("norm_g", "b_gate", "q_norm_g", "kv_norm_g", "mem_norm_g", "mem_q_g", "mem_k_g"):
        p[n] = full[n][l][None, :]
    p["q_head_g"] = _pad_last(full["q_head_g"][l][None, :], HP)
    p["k_head_g"] = _pad_last(full["k_head_g"][l][None, :], HP)
    p["conv_wb"] = jnp.concatenate(
        [full["conv_w"][l], full["conv_b"][l][None, :], jnp.zeros((4, CONV_W), F32)], axis=0)
    return p


def _join_shards(name, g):
    _, R, C = g.shape
    if name in _COL_SHARDED:
        return jnp.transpose(g, (1, 0, 2)).reshape(R, 4 * C)
    return g.reshape(4 * R, C)


def _split_shards(name, w):
    R, C = w.shape
    if name in _COL_SHARDED:
        return jnp.transpose(w.reshape(R, 4, C // 4), (1, 0, 2)).astype(BF16)
    return w.reshape(4, R // 4, C).astype(BF16)


def _unpad_grads(gp):
    out = {"w_in": _unpad_w_in(gp["w_in"])}
    out["w_uq"] = gp["w_uq"].reshape(Q_RANK, N_HEADS, HP)[..., :QK_DIM].reshape(Q_RANK, N_HEADS * QK_DIM)
    duk = gp["w_uk"].reshape(KV_RANK, N_HEADS, HP)[..., :NOPE_DIM]
    duv = gp["w_uv"].reshape(KV_RANK, N_HEADS, HP)[..., :V_DIM]
    out["w_ukv"] = jnp.concatenate([duk, duv], axis=-1).reshape(KV_RANK, 1024)
    out["w_br_attn"] = gp["w_br_attn"].reshape(N_HEADS, HP, D_MODEL)[:, :V_DIM].reshape(512, D_MODEL)
    for n in ("w_br_conv", "w_br_mem", "w_out", "w_mkv"):
        out[n] = gp[n]
    return out


def _train_step(x, mem, positions, w, target):
    S = x.shape[0]
    invf16 = ROPE_BASE ** (-jnp.arange(0, ROPE_DIM, 2, dtype=F32) / ROPE_DIM)
    invf = jnp.concatenate([jnp.zeros((NOPE_DIM,), F32), invf16, invf16, jnp.zeros((HP - QK_DIM,), F32)])[None, :]
    tabs = _rope_tables(jnp.broadcast_to(positions.reshape(S, 1), (S, HP)), invf)
    shards = [[w[n][l].astype(BF16) for n in _BIG] for l in range(DEPTH)]
    first = _gather_chips(shards[0] + [w["conv_w"]])
    gathered = first[:-1]
    full = {n: w[n] for n in _SMALL}
    full["conv_w"] = _cols_from_shards(first[-1])
    params, saved = [], []
    h = x
    for l in range(DEPTH):
        big = {n: _join_shards(n, g) for n, g in zip(_BIG, gathered)}
        params.append(_layer_params(big, full, l))
        h, s, gathered = _layer_fwd(h, mem, tabs, params[l], shards[l + 1] if l + 1 < DEPTH else ())
        saved.append(s)
    g, loss_part = _loss_head(h, target)
    per_layer, received = [None] * DEPTH, [None] * DEPTH
    to_owner = ()
    for l in reversed(range(DEPTH)):
        g, per_layer[l], got = _layer_bwd(g, mem, tabs, params[l], saved[l], to_owner)
        if to_owner:
            received[l + 1] = got
        big_g = _unpad_grads(per_layer[l])
        to_owner = [_split_shards(n, big_g[n]) for n in _BIG]
    st = lambda n: jnp.stack([per_layer[l][n] for l in range(DEPTH)])
    small = {}
    for n in ("norm_g", "b_gate", "q_norm_g", "kv_norm_g", "mem_norm_g", "mem_q_g", "mem_k_g"):
        small[n] = st(n)[:, 0, :]
    small["q_head_g"] = st("q_head_g")[:, 0, :QK_DIM]
    small["k_head_g"] = st("k_head_g")[:, 0, :QK_DIM]
    cwb = st("conv_wb")
    small["conv_w"] = cwb[:, 0:3, :]
    small["conv_b"] = cwb[:, 3, :]
    return loss_part, g, received, to_owner, small


_COL_SHARDED = ("w_in", "w_uq", "w_ukv", "w_br_attn", "w_br_conv", "w_br_mem")
_ROW_SHARDED = ("w_mkv", "w_out")
_BIG = _COL_SHARDED + _ROW_SHARDED
_SMALL = ("norm_g", "b_gate", "q_norm_g", "kv_norm_g", "q_head_g", "k_head_g", "conv_w", "conv_b",
          "mem_norm_g", "mem_q_g", "mem_k_g")
_ORDER = ("norm_g", "w_in", "b_gate", "q_norm_g", "w_uq", "kv_norm_g", "w_ukv", "q_head_g", "k_head_g",
          "conv_w", "conv_b", "mem_norm_g", "w_mkv", "mem_q_g", "mem_k_g", "w_br_attn", "w_br_conv",
          "w_br_mem", "w_out")


def _pack_small(d, extra):
    flat = jnp.concatenate([d[n].reshape(-1) for n in _SMALL] + [extra.reshape(-1)])
    n = flat.shape[0]
    rows = -(-n // 1024) * 8
    return jnp.pad(flat, (0, rows * 128 - n)).reshape(rows, 128)


def _unpack_small(packed, like):
    flat = packed.reshape(-1)
    out, off = {}, 0
    for n in _SMALL:
        sz = int(np.prod(like[n].shape))
        out[n] = flat[off:off + sz].reshape(like[n].shape)
        off += sz
    return out, flat[off:]


def kernel(x, mem, positions, norm_g, w_in, b_gate, q_norm_g, w_uq, kv_norm_g, w_ukv, q_head_g, k_head_g, conv_w, conv_b, mem_norm_g, w_mkv, mem_q_g, mem_k_g, w_br_attn, w_br_conv, w_br_mem, w_out, loss_target, m_norm_g, m_w_in, m_b_gate, m_q_norm_g, m_w_uq, m_kv_norm_g, m_w_ukv, m_q_head_g, m_k_head_g, m_conv_w, m_conv_b, m_mem_norm_g, m_w_mkv, m_mem_q_g, m_mem_k_g, m_w_br_attn, m_w_br_conv, m_w_br_mem, m_w_out, v_norm_g, v_w_in, v_b_gate, v_q_norm_g, v_w_uq, v_kv_norm_g, v_w_ukv, v_q_head_g, v_k_head_g, v_conv_w, v_conv_b, v_mem_norm_g, v_w_mkv, v_mem_q_g, v_mem_k_g, v_w_br_attn, v_w_br_conv, v_w_br_mem, v_w_out):
    w = dict(norm_g=norm_g, w_in=w_in, b_gate=b_gate, q_norm_g=q_norm_g, w_uq=w_uq, kv_norm_g=kv_norm_g,
             w_ukv=w_ukv, q_head_g=q_head_g, k_head_g=k_head_g, conv_w=conv_w, conv_b=conv_b,
             mem_norm_g=mem_norm_g, w_mkv=w_mkv, mem_q_g=mem_q_g, mem_k_g=mem_k_g, w_br_attn=w_br_attn,
             w_br_conv=w_br_conv, w_br_mem=w_br_mem, w_out=w_out)
    m = dict(norm_g=m_norm_g, w_in=m_w_in, b_gate=m_b_gate, q_norm_g=m_q_norm_g, w_uq=m_w_uq,
             kv_norm_g=m_kv_norm_g, w_ukv=m_w_ukv, q_head_g=m_q_head_g, k_head_g=m_k_head_g, conv_w=m_conv_w,
             conv_b=m_conv_b, mem_norm_g=m_mem_norm_g, w_mkv=m_w_mkv, mem_q_g=m_mem_q_g, mem_k_g=m_mem_k_g,
             w_br_attn=m_w_br_attn, w_br_conv=m_w_br_conv, w_br_mem=m_w_br_mem, w_out=m_w_out)
    v = dict(norm_g=v_norm_g, w_in=v_w_in, b_gate=v_b_gate, q_norm_g=v_q_norm_g, w_uq=v_w_uq,
             kv_norm_g=v_kv_norm_g, w_ukv=v_w_ukv, q_head_g=v_q_head_g, k_head_g=v_k_head_g, conv_w=v_conv_w,
             conv_b=v_conv_b, mem_norm_g=v_mem_norm_g, w_mkv=v_w_mkv, mem_q_g=v_mem_q_g, mem_k_g=v_mem_k_g,
             w_br_attn=v_w_br_attn, w_br_conv=v_w_br_conv, w_br_mem=v_w_br_mem, w_out=v_w_out)
    chip = 2 * lax.axis_index("x") + lax.axis_index("y")

    loss_part, grad_x, received, last_slabs, grads = _train_step(x[0], mem[0], positions[0], w, loss_target[0])

    loss_vec = jnp.zeros((128,), F32).at[0].set(0.5 / D_MODEL * jnp.sum(loss_part))
    small = _pack_small(grads, loss_vec)
    scattered = _scatter_chips(last_slabs, small)
    received[0] = scattered[:-1]
    small_sum = _sum_slabs(scattered[-1], "sum_small")
    partial = []
    for a, n in enumerate(_BIG):
        partial.append(jnp.concatenate([_sum_slabs(received[l][a], "sum_" + n) for l in range(DEPTH)], axis=0))
    other = _swap_cores(partial)

    small_g, tail = _unpack_small(small_sum, {n: (grads[n]) for n in _SMALL})
    loss = tail[0]
    small_g["conv_w"] = lax.dynamic_slice_in_dim(small_g["conv_w"], chip * (CONV_W // 4), CONV_W // 4, axis=2)

    outs_g, outs_d, outs_m, outs_v = {}, {}, {}, {}
    for n, p0, p1 in zip(_BIG, partial, other):
        shape = w[n].shape
        flat = lambda t: t.reshape(p0.shape)
        g_, d_, m_, v_ = _adamw(flat(w[n]), p0, p1, flat(m[n]), flat(v[n]), "adamw_" + n)
        outs_g[n], outs_d[n], outs_m[n], outs_v[n] = (t.reshape(shape) for t in (g_, d_, m_, v_))
    zero_small = jnp.zeros_like(small_sum)
    pk = lambda d: _pack_small(d, jnp.zeros((128,), F32))
    g_, d_, m_, v_ = _adamw(pk(w), _pack_small(small_g, jnp.zeros((128,), F32)), zero_small, pk(m), pk(v),
                            "adamw_small")
    like = {n: w[n] for n in _SMALL}
    for dst, packed in ((outs_g, g_), (outs_d, d_), (outs_m, m_), (outs_v, v_)):
        dst.update(_unpack_small(packed, like)[0])

    return (loss, grad_x[None], *[outs_g[n] for n in _ORDER], *[outs_d[n] for n in _ORDER],
            *[outs_m[n] for n in _ORDER], *[outs_v[n] for n in _ORDER])
```

```python
import functools

import numpy as np
import jax
import jax.numpy as jnp
from jax import lax
from jax.experimental import pallas as pl
from jax.experimental.pallas import tpu as pltpu

F32 = jnp.float32
BF16 = jnp.bfloat16

D_MODEL = 1024
DEPTH = 4
N_HEADS = 8
QK_DIM = 96
NOPE_DIM = 64
ROPE_DIM = 32
V_DIM = 64
Q_RANK = 384
KV_RANK = 256
CONV_W = 512
MEM_HEADS = 4
MEM_HD = 128
MEM_W = 512
IN_WIDTH = 7328
PW = 8192
HP = 128
PROJ_HALO = 16
F32_HALO = 8
EPS = 1e-6
ROPE_BASE = 10000.0
SCALE = QK_DIM ** -0.5
MEM_SCALE = MEM_HD ** -0.5

ADAM_LR = 0.001
ADAM_B1 = 0.9
ADAM_B2 = 0.999
ADAM_EPS = 1e-08
ADAM_WD = 0.01
ADAM_STEP = 10

VMEM_LIMIT_V7X = 56 * 1024 * 1024

O_R, O_GA, O_CB, O_QM, O_GC, O_GM, O_CC, O_CU, O_QL, O_KVL, O_KPE = (
    0, 3072, 4096, 4608, 5120, 5632, 6144, 6656, 7168, 7552, 7808)


def _params(sem, vmem=VMEM_LIMIT_V7X):
    return pltpu.CompilerParams(dimension_semantics=sem, vmem_limit_bytes=vmem)


def _sigmoid(t):
    return 0.5 * jnp.tanh(0.5 * t) + 0.5


def _silu_and_grad(g):
    sg = _sigmoid(g)
    return g * sg, sg * (1.0 + g * (1.0 - sg))


def _rms(t, g, n=None):
    n = t.shape[-1] if n is None else n
    r = lax.rsqrt(jnp.sum(t * t, axis=-1, keepdims=True) * (1.0 / n) + EPS)
    return (t * r) * g


def _rms_parts(t, n=None):
    n = t.shape[-1] if n is None else n
    r = lax.rsqrt(jnp.sum(t * t, axis=-1, keepdims=True) * (1.0 / n) + EPS)
    return r, t * r


def _rms_bwd(dhat, hat, r, n):
    return r * (dhat - hat * (jnp.sum(dhat * hat, axis=-1, keepdims=True) * (1.0 / n)))


def _rope(t, c, sa, sb):
    return t * c + pltpu.roll(t, HP - 16, 1) * sa + pltpu.roll(t, 16, 1) * sb


def _rope_t(d, c, sa, sb):
    return d * c + pltpu.roll(d * sa, 16, 1) + pltpu.roll(d * sb, HP - 16, 1)


def _dot(a, b):
    return jnp.dot(a, b, preferred_element_type=F32)


def _dot_nt(a, b):
    return lax.dot_general(a, b, (((1,), (1,)), ((), ())), preferred_element_type=F32)


def _dot_tn(a, b):
    return lax.dot_general(a, b, (((0,), (0,)), ((), ())), preferred_element_type=F32)


def _colsum(t):
    return jnp.sum(t, axis=0, keepdims=True)


def _tile(n, t):
    t = min(n, t)
    assert n % t == 0, (n, t)
    return t


def _rope_tables(pos_b, invf):
    S = pos_b.shape[0]
    tm = _tile(S, 1024)

    def body(pos_ref, invf_ref, c_ref, sa_ref, sb_ref):
        ang = pos_ref[...].astype(F32) * invf_ref[...]
        lane = lax.broadcasted_iota(jnp.int32, ang.shape, 1)
        cs = jnp.cos(ang)
        sn = jnp.sin(ang)
        c_ref[...] = jnp.where(lane < NOPE_DIM, 1.0, jnp.where(lane < QK_DIM, cs, 0.0))
        sa_ref[...] = jnp.where((lane >= NOPE_DIM) & (lane < NOPE_DIM + 16), -sn, 0.0)
        sb_ref[...] = jnp.where((lane >= NOPE_DIM + 16) & (lane < QK_DIM), sn, 0.0)

    blk = pl.BlockSpec((tm, HP), lambda i: (i, 0))
    return pl.pallas_call(
        body, name="rope_tables", grid=(S // tm,),
        in_specs=[blk, pl.BlockSpec((1, HP), lambda i: (0, 0))],
        out_specs=[blk, blk, blk],
        out_shape=[jax.ShapeDtypeStruct((S, HP), F32)] * 3,
        compiler_params=_params(("parallel",)),
    )(pos_b, invf)


MESH = pl.DeviceIdType.MESH
HBM = pl.BlockSpec(memory_space=pltpu.HBM)


def _chip_peer(x, y, c, k):
    return (x ^ (k >> 1), y ^ (k & 1), c)


def _chip_copies(ins, outs, send, recv, loc, gather):
    x, y, c = lax.axis_index("x"), lax.axis_index("y"), lax.axis_index("c")
    me = 2 * x + y
    cps = []
    for a in range(len(ins)):
        cps.append(pltpu.make_async_copy(ins[a] if gather else ins[a].at[me], outs[a].at[me], loc.at[a]))
        for k in (1, 2, 3):
            cps.append(pltpu.make_async_remote_copy(
                src_ref=ins[a] if gather else ins[a].at[me ^ k], dst_ref=outs[a].at[me],
                send_sem=send.at[3 * a + k - 1], recv_sem=recv.at[3 * a + k - 1],
                device_id=_chip_peer(x, y, c, k), device_id_type=MESH))
    return cps


def _carried(body, n_in, n_out, carry, gather):
    n = len(carry)

    def wrapped(*refs):
        ins, cin = refs[:n_in], refs[n_in:n_in + n]
        outs, cout = refs[n_in + n:n_in + n + n_out], refs[n_in + n + n_out:n_in + 2 * n + n_out]
        send, recv, loc = refs[n_in + 2 * n + n_out:]
        i = pl.program_id(0)

        @pl.when(i == 0)
        def _():
            for cp in _chip_copies(cin, cout, send, recv, loc, gather):
                cp.start()

        body(*ins, *outs)

        @pl.when(i == pl.num_programs(0) - 1)
        def _():
            for cp in _chip_copies(cin, cout, send, recv, loc, gather):
                cp.wait()

    specs = dict(
        in_specs=[HBM] * n, out_specs=[HBM] * n,
        out_shape=[jax.ShapeDtypeStruct(((4,) + a.shape) if gather else a.shape, a.dtype) for a in carry],
        scratch_shapes=[pltpu.SemaphoreType.DMA((3 * n,)), pltpu.SemaphoreType.DMA((3 * n,)),
                        pltpu.SemaphoreType.DMA((n,))])
    return wrapped, specs


def _inproj(x, g, w, carry=()):
    S = x.shape[0]
    tm = _tile(S, 512)

    def body(x_ref, g_ref, w_ref, proj_ref, xn_ref):
        h = _rms(x_ref[...], g_ref[...]).astype(BF16)
        xn_ref[...] = h
        proj_ref[...] = _dot(h, w_ref[...]).astype(BF16)

    row = lambda n: pl.BlockSpec((tm, n), lambda i: (i, 0))
    in_specs = [row(D_MODEL), pl.BlockSpec((1, D_MODEL), lambda i: (0, 0)), pl.BlockSpec((D_MODEL, PW), lambda i: (0, 0), pipeline_mode=pl.Buffered(1))]
    out_specs = [row(PW), row(D_MODEL)]
    out_shape = [jax.ShapeDtypeStruct((S, PW), BF16), jax.ShapeDtypeStruct((S, D_MODEL), BF16)]
    if not carry:
        proj, xn = pl.pallas_call(
            body, name="inproj", grid=(S // tm,), in_specs=in_specs, out_specs=out_specs, out_shape=out_shape,
            compiler_params=_params(("parallel",)))(x, g, w)
        return proj, xn, []
    wrapped, extra = _carried(body, 3, 2, carry, gather=True)
    res = pl.pallas_call(
        wrapped, name="inproj_gather", grid=(S // tm,), in_specs=in_specs + extra["in_specs"],
        out_specs=out_specs + extra["out_specs"], out_shape=out_shape + extra["out_shape"],
        scratch_shapes=extra["scratch_shapes"], compiler_params=_params(("arbitrary",)))(x, g, w, *carry)
    return res[0], res[1], list(res[2:])


def _mla_prep(proj, tabs, qg, kvg, qhg, khg, wuq, wuk, wuv):
    S = proj.shape[0]
    tm = _tile(S, 512)

    def body(a_ref, c_ref, sa_ref, sb_ref, qg_ref, kvg_ref, qhg_ref, khg_ref, wuq_ref, wuk_ref, wuv_ref,
             q_ref, k_ref, v_ref, qn_ref, kvn_ref, bound_ref, qt_ref, vt_ref):
        gq = jnp.max(jnp.abs(qhg_ref[...]), axis=-1, keepdims=True)
        gk = jnp.max(jnp.abs(khg_ref[...]), axis=-1, keepdims=True)
        bound_ref[...] = jnp.broadcast_to(gq * gk * (QK_DIM ** 0.5 * 1.01) + 1e-6, bound_ref.shape)
        ql = a_ref[:, 0:Q_RANK].astype(F32)
        kvl = a_ref[:, Q_RANK:Q_RANK + KV_RANK].astype(F32)
        kpe = a_ref[:, Q_RANK + KV_RANK:Q_RANK + KV_RANK + HP].astype(F32)
        qn = _rms(ql, qg_ref[...]).astype(BF16)
        kvn = _rms(kvl, kvg_ref[...]).astype(BF16)
        qn_ref[...] = qn
        kvn_ref[...] = kvn
        qraw = _dot(qn, wuq_ref[...])
        kn = _dot(kvn, wuk_ref[...])
        vf = _dot(kvn, wuv_ref[...])
        v_ref[...] = vf.astype(BF16)
        c, sa, sb = c_ref[...], sa_ref[...], sb_ref[...]
        for h in range(N_HEADS):
            sl = slice(h * HP, (h + 1) * HP)
            tq = _rms(qraw[:, sl], qhg_ref[...], QK_DIM)
            qh = _rope(tq, c, sa, sb) * SCALE
            q_ref[:, sl] = qh.astype(BF16)
            qt_ref[sl, :] = qh.T.astype(BF16)
            vt_ref[sl, :] = vf[:, sl].T.astype(BF16)
            tk = _rms(kn[:, sl] + kpe, khg_ref[...], QK_DIM)
            k_ref[:, sl] = _rope(tk, c, sa, sb).astype(BF16)

    row = lambda w: pl.BlockSpec((tm, w), lambda i: (i, 0))
    full = lambda a: pl.BlockSpec(a.shape, lambda i: (0,) * a.ndim)
    return pl.pallas_call(
        body, name="mla_prep", grid=(S // tm,),
        in_specs=[pl.BlockSpec((tm, 1024), lambda i: (i, 7)), row(HP), row(HP), row(HP),
                  full(qg), full(kvg), full(qhg), full(khg), full(wuq), full(wuk), full(wuv)],
        out_specs=[row(1024), row(1024), row(1024), row(Q_RANK), row(KV_RANK),
                   pl.BlockSpec((1, HP), lambda i: (0, 0)),
                   pl.BlockSpec((1024, tm), lambda i: (0, i)), pl.BlockSpec((1024, tm), lambda i: (0, i))],
        out_shape=[jax.ShapeDtypeStruct((S, 1024), BF16)] * 3
        + [jax.ShapeDtypeStruct((S, Q_RANK), BF16), jax.ShapeDtypeStruct((S, KV_RANK), BF16),
           jax.ShapeDtypeStruct((1, HP), F32)] + [jax.ShapeDtypeStruct((1024, S), BF16)] * 2,
        compiler_params=_params(("arbitrary",)),
    )(proj, *tabs, qg, kvg, qhg, khg, wuq, wuk, wuv)


SAFE_SCORE_BOUND = 30.0


def _attn_fwd(qt, k, vt, score_bound):
    S = k.shape[0]
    tq, tk = _tile(S, 1024), _tile(S, 8192)
    nk = S // tk

    def body(qt_ref, k_ref, vt_ref, bound_ref, o_ref, lse_ref, m_s, l_s, acc_s):
        qtv = qt_ref[...]
        bound = bound_ref[0:1, 0:1]
        safe = jnp.max(bound) <= SAFE_SCORE_BOUND
        l_s[...] = jnp.zeros(l_s.shape, F32)
        acc_s[...] = jnp.zeros(acc_s.shape, F32)

        def keys(c):
            return pl.ds(pl.multiple_of(c * tk, tk), tk)

        @pl.when(safe)
        def _():
            def step(c, carry):
                pt = jnp.exp(_dot(k_ref[keys(c), :], qtv) - bound)
                l_s[...] += jnp.sum(pt, axis=0, keepdims=True)
                acc_s[...] += _dot(vt_ref[:, keys(c)], pt.astype(BF16))
                return carry

            lax.fori_loop(0, nk, step, 0)
            m_s[...] = jnp.broadcast_to(bound, m_s.shape)

        @pl.when(jnp.logical_not(safe))
        def _():
            m_s[...] = jnp.full(m_s.shape, -jnp.inf, F32)

            def step(c, carry):
                st = _dot(k_ref[keys(c), :], qtv)
                m_prev = m_s[...]
                m_new = jnp.maximum(m_prev, jnp.max(st, axis=0, keepdims=True))
                alpha = jnp.exp(m_prev - m_new)
                pt = jnp.exp(st - m_new)
                l_s[...] = alpha * l_s[...] + jnp.sum(pt, axis=0, keepdims=True)
                acc_s[...] = alpha * acc_s[...] + _dot(vt_ref[:, keys(c)], pt.astype(BF16))
                m_s[...] = m_new
                return carry

            lax.fori_loop(0, nk, step, 0)

        o_ref[...] = (acc_s[...] / l_s[...]).T
        lse_row = m_s[...] + jnp.log(l_s[...])
        lse_ref[0] = jnp.broadcast_to(lse_row, (HP, tq)).T[:, 0:1]

    return pl.pallas_call(
        body, name="attn_fwd", grid=(N_HEADS, S // tq),
        in_specs=[pl.BlockSpec((HP, tq), lambda h, i: (h, i)),
                  pl.BlockSpec((S, HP), lambda h, i: (0, h)),
                  pl.BlockSpec((HP, S), lambda h, i: (h, 0)),
                  pl.BlockSpec((1, HP), lambda h, i: (0, 0))],
        out_specs=[pl.BlockSpec((tq, HP), lambda h, i: (i, h)),
                   pl.BlockSpec((1, tq, 1), lambda h, i: (h, i, 0))],
        out_shape=[jax.ShapeDtypeStruct((S, N_HEADS * HP), F32),
                   jax.ShapeDtypeStruct((N_HEADS, S, 1), F32)],
        scratch_shapes=[pltpu.VMEM((1, tq), F32), pltpu.VMEM((1, tq), F32), pltpu.VMEM((HP, tq), F32)],
        compiler_params=_params(("parallel", "parallel")),
    )(qt, k, vt, score_bound)


def _memkv(mem, mng, mkg, wmkv):
    M = mem.shape[0]

    def body(mem_ref, mng_ref, mkg_ref, w_ref, mk_ref, mv_ref):
        mn = _rms(mem_ref[...], mng_ref[...]).astype(BF16)
        mkv = _dot(mn, w_ref[...])
        for h in range(MEM_HEADS):
            kraw = mkv[:, 2 * MEM_HD * h:2 * MEM_HD * h + MEM_HD]
            mk_ref[:, MEM_HD * h:MEM_HD * (h + 1)] = _rms(kraw, mkg_ref[...]).astype(BF16)
            mv_ref[:, MEM_HD * h:MEM_HD * (h + 1)] = mkv[:, 2 * MEM_HD * h + MEM_HD:2 * MEM_HD * (h + 1)].astype(BF16)

    return pl.pallas_call(
        body, name="memkv",
        out_shape=[jax.ShapeDtypeStruct((M, MEM_W), BF16)] * 2,
        compiler_params=pltpu.CompilerParams(vmem_limit_bytes=VMEM_LIMIT_V7X),
    )(mem, mng, mkg, wmkv)


def _conv_shifts(cc, cu, hp_ref, hn_ref, i, n_tiles, tm):
    z = cc * cu
    last = PROJ_HALO - 1
    zp = hp_ref[last:last + 1, 0:CONV_W].astype(F32) * hp_ref[last:last + 1, CONV_W:2 * CONV_W].astype(F32)
    zn = hn_ref[0:1, 0:CONV_W].astype(F32) * hn_ref[0:1, CONV_W:2 * CONV_W].astype(F32)
    zp = jnp.where(i == 0, 0.0, zp)
    zn = jnp.where(i == n_tiles - 1, 0.0, zn)
    row = lax.broadcasted_iota(jnp.int32, z.shape, 0)
    z_up = jnp.where(row == 0, zp, pltpu.roll(z, 1, 0))
    z_dn = jnp.where(row == tm - 1, zn, pltpu.roll(z, tm - 1, 0))
    return z, z_up, z_dn


def _halo_specs(tm, S, width, col, rows):
    per = tm // rows
    prev = pl.BlockSpec((rows, width), lambda i: (jnp.maximum(i * per - 1, 0), col))
    nxt = pl.BlockSpec((rows, width), lambda i: (jnp.minimum((i + 1) * per, S // rows - 1), col))
    return prev, nxt


def _mem_attend(qm, mqg, mk_h, mv_h):
    r, qhat = _rms_parts(qm)
    mq = (qhat * mqg).astype(BF16)
    s = _dot_nt(mq, mk_h) * MEM_SCALE
    e = jnp.exp(s - jnp.max(s, axis=-1, keepdims=True))
    p = e / jnp.sum(e, axis=-1, keepdims=True)
    pv = _dot(p.astype(BF16), mv_h)
    return r, qhat, mq, p, pv


def _merge(proj, o, x, bg, convw, mqg, mk, mv, wba, wbc, wbm, wo):
    S = x.shape[0]
    tm = _tile(S, 512)
    nt = S // tm

    def body(main_ref, ccu_ref, hp_ref, hn_ref, o_ref, x_ref, bg_ref, cw_ref, mqg_ref, mk_ref, mv_ref,
             wba_ref, wbc_ref, wbm_ref, wo_ref, xn_ref, oa_ref, oc_ref, om_ref, u_ref, y_ref):
        i = pl.program_id(0)
        sil_a, _ = _silu_and_grad(main_ref[:, O_GA:O_GA + 1024].astype(F32))
        oa = (o_ref[...] * sil_a).astype(BF16)
        oa_ref[...] = oa
        z, z_up, z_dn = _conv_shifts(ccu_ref[:, 0:CONV_W].astype(F32), ccu_ref[:, CONV_W:].astype(F32), hp_ref, hn_ref, i, nt, tm)
        cv = cw_ref[0:1, :] * z_up + cw_ref[1:2, :] * z + cw_ref[2:3, :] * z_dn + cw_ref[3:4, :]
        sil_c, _ = _silu_and_grad(main_ref[:, O_GC:O_GC + CONV_W].astype(F32))
        oc = (main_ref[:, O_CB:O_CB + CONV_W].astype(F32) * cv * sil_c).astype(BF16)
        oc_ref[...] = oc
        sil_m, _ = _silu_and_grad(main_ref[:, O_GM:O_GM + MEM_W].astype(F32))
        for h in range(MEM_HEADS):
            sl = slice(h * MEM_HD, (h + 1) * MEM_HD)
            qm = main_ref[:, O_QM + h * MEM_HD:O_QM + (h + 1) * MEM_HD].astype(F32)
            pv = _mem_attend(qm, mqg_ref[...], mk_ref[:, sl], mv_ref[:, sl])[4]
            om_ref[:, sl] = (pv * sil_m[:, sl]).astype(BF16)
        ua = _dot(oa, wba_ref[...])
        uc = _dot(oc, wbc_ref[...])
        um = _dot(om_ref[...], wbm_ref[...])
        u_ref[:, 0:1024] = ua.astype(BF16)
        u_ref[:, 1024:2048] = uc.astype(BF16)
        u_ref[:, 2048:3072] = um.astype(BF16)
        rg = _sigmoid(main_ref[:, O_R:O_R + 3072].astype(F32) + bg_ref[...])
        y = (rg[:, 0:1024] * ua + rg[:, 1024:2048] * uc + rg[:, 2048:3072] * um).astype(BF16)
        y_ref[...] = y
        xn_ref[...] = x_ref[...] + _dot(y, wo_ref[...])

    row = lambda w: pl.BlockSpec((tm, w), lambda i: (i, 0))
    full = lambda a: pl.BlockSpec(a.shape, lambda i: (0,) * a.ndim)
    hp, hn = _halo_specs(tm, S, 1024, 6, PROJ_HALO)
    return pl.pallas_call(
        body, name="merge", grid=(nt,),
        in_specs=[row(6144), pl.BlockSpec((tm, 1024), lambda i: (i, 6)), hp, hn, row(1024), row(1024),
                  full(bg), full(convw), full(mqg), full(mk), full(mv), full(wba), full(wbc), full(wbm), full(wo)],
        out_specs=[row(1024), row(1024), row(CONV_W), row(MEM_W), row(3072), row(1024)],
        out_shape=[jax.ShapeDtypeStruct((S, 1024), F32), jax.ShapeDtypeStruct((S, 1024), BF16),
                   jax.ShapeDtypeStruct((S, CONV_W), BF16), jax.ShapeDtypeStruct((S, MEM_W), BF16),
                   jax.ShapeDtypeStruct((S, 3072), BF16), jax.ShapeDtypeStruct((S, 1024), BF16)],
        compiler_params=_params(("parallel",)),
    )(proj, proj, proj, proj, o, x, bg, convw, mqg, mk, mv, wba, wbc, wbm, wo)


def _loss_head(xf, tgt):
    S = xf.shape[0]
    tm = _tile(S, 1024)

    def body(x_ref, t_ref, g_ref, acc_ref):
        @pl.when(pl.program_id(0) == 0)
        def _():
            acc_ref[...] = jnp.zeros(acc_ref.shape, F32)

        e = x_ref[...] - t_ref[...]
        g_ref[...] = e * (1.0 / D_MODEL)
        part = jnp.sum((e * e).reshape(tm // 8, 8, D_MODEL), axis=0)
        tot = part[:, 0:128]
        for k in range(1, D_MODEL // 128):
            tot = tot + part[:, 128 * k:128 * (k + 1)]
        acc_ref[...] += tot

    row = pl.BlockSpec((tm, D_MODEL), lambda i: (i, 0))
    return pl.pallas_call(
        body, name="loss_head", grid=(S // tm,),
        in_specs=[row, row],
        out_specs=[row, pl.BlockSpec((8, 128), lambda i: (0, 0))],
        out_shape=[jax.ShapeDtypeStruct((S, D_MODEL), F32), jax.ShapeDtypeStruct((8, 128), F32)],
        compiler_params=_params(("arbitrary",)),
    )(xf, tgt)


def _merge_bwd(g, proj, o, u, bg, convw, mqg, mk, mv, wot, wbat, wbct, wbmt):
    S = g.shape[0]
    tm = _tile(S, 256)
    nt = S // tm
    M = mk.shape[0]

    def body(g_ref, main_ref, ccu_ref, hp_ref, hn_ref, o_ref, u_ref, bg_ref, cw_ref, mqg_ref, mk_ref, mv_ref,
             wot_ref, wbat_ref, wbct_ref, wbmt_ref,
             dmain_ref, dcv_ref, do_ref, delta_ref, du_ref, dbg_ref, dmk_ref, dmv_ref, dmqg_ref, dot_ref):
        i = pl.program_id(0)

        @pl.when(i == 0)
        def _():
            dbg_ref[...] = jnp.zeros(dbg_ref.shape, F32)
            dmk_ref[...] = jnp.zeros(dmk_ref.shape, F32)
            dmv_ref[...] = jnp.zeros(dmv_ref.shape, F32)
            dmqg_ref[...] = jnp.zeros(dmqg_ref.shape, F32)

        dy = _dot(g_ref[...].astype(BF16), wot_ref[...])
        d_branch = []
        for b, wt_ref in enumerate((wbat_ref, wbct_ref, wbmt_ref)):
            cols = slice(O_R + b * D_MODEL, O_R + (b + 1) * D_MODEL)
            rg = _sigmoid(main_ref[:, cols].astype(F32) + bg_ref[:, cols])
            dr = dy * u_ref[:, cols].astype(F32) * rg * (1.0 - rg)
            dmain_ref[:, cols] = dr.astype(BF16)
            dbg_ref[:, cols] += _colsum(dr)
            du = (dy * rg).astype(BF16)
            du_ref[:, cols] = du
            d_branch.append(_dot(du, wt_ref[...]))
        do_a, do_c, do_m = d_branch

        sil_a, dsil_a = _silu_and_grad(main_ref[:, O_GA:O_GA + 1024].astype(F32))
        ov = o_ref[...]
        d_o = do_a * sil_a
        do_ref[...] = d_o.astype(BF16)
        dot_ref[...] = d_o.T.astype(BF16)
        dmain_ref[:, O_GA:O_GA + 1024] = (do_a * ov * dsil_a).astype(BF16)
        prod = d_o * ov
        for h in range(N_HEADS):
            delta_ref[h] = jnp.sum(prod[:, h * HP:(h + 1) * HP], axis=-1, keepdims=True)

        z, z_up, z_dn = _conv_shifts(ccu_ref[:, 0:CONV_W].astype(F32), ccu_ref[:, CONV_W:].astype(F32), hp_ref, hn_ref, i, nt, tm)
        cv = cw_ref[0:1, :] * z_up + cw_ref[1:2, :] * z + cw_ref[2:3, :] * z_dn + cw_ref[3:4, :]
        sil_c, dsil_c = _silu_and_grad(main_ref[:, O_GC:O_GC + CONV_W].astype(F32))
        cb = main_ref[:, O_CB:O_CB + CONV_W].astype(F32)
        dmain_ref[:, O_CB:O_CB + CONV_W] = (do_c * cv * sil_c).astype(BF16)
        dmain_ref[:, O_GC:O_GC + CONV_W] = (do_c * cb * cv * dsil_c).astype(BF16)
        dcv_ref[...] = do_c * cb * sil_c

        sil_m, dsil_m = _silu_and_grad(main_ref[:, O_GM:O_GM + MEM_W].astype(F32))
        for h in range(MEM_HEADS):
            sl = slice(h * MEM_HD, (h + 1) * MEM_HD)
            qm = main_ref[:, O_QM + h * MEM_HD:O_QM + (h + 1) * MEM_HD].astype(F32)
            mk_h, mv_h = mk_ref[:, sl], mv_ref[:, sl]
            r, qhat, mq, p, pv = _mem_attend(qm, mqg_ref[...], mk_h, mv_h)
            dom = do_m[:, sl]
            dmain_ref[:, O_GM + h * MEM_HD:O_GM + (h + 1) * MEM_HD] = (dom * pv * dsil_m[:, sl]).astype(BF16)
            dpv = (dom * sil_m[:, sl]).astype(BF16)
            dp = _dot_nt(dpv, mv_h)
            ds = (p * (dp - jnp.sum(dp * p, axis=-1, keepdims=True)) * MEM_SCALE).astype(BF16)
            dmq = _dot(ds, mk_h)
            dmk_ref[:, sl] += _dot_tn(ds, mq)
            dmv_ref[:, sl] += _dot_tn(p.astype(BF16), dpv)
            dmqg_ref[...] += _colsum(dmq * qhat)
            dqm = _rms_bwd(dmq * mqg_ref[...], qhat, r, MEM_HD)
            dmain_ref[:, O_QM + h * MEM_HD:O_QM + (h + 1) * MEM_HD] = dqm.astype(BF16)

    row = lambda w: pl.BlockSpec((tm, w), lambda i: (i, 0))
    full = lambda a: pl.BlockSpec(a.shape, lambda i: (0,) * a.ndim)
    acc = lambda r, c: pl.BlockSpec((r, c), lambda i: (0, 0))
    hp, hn = _halo_specs(tm, S, 1024, 6, PROJ_HALO)
    return pl.pallas_call(
        body, name="merge_bwd", grid=(nt,),
        in_specs=[row(1024), row(6144), pl.BlockSpec((tm, 1024), lambda i: (i, 6)), hp, hn, row(1024), row(3072),
                  full(bg), full(convw), full(mqg), full(mk), full(mv), full(wot), full(wbat), full(wbct), full(wbmt)],
        out_specs=[row(6144), row(CONV_W), row(1024), pl.BlockSpec((N_HEADS, tm, 1), lambda i: (0, i, 0)), row(3072),
                   acc(1, 3072), acc(M, MEM_W), acc(M, MEM_W), acc(1, MEM_HD),
                   pl.BlockSpec((1024, tm), lambda i: (0, i))],
        out_shape=[jax.ShapeDtypeStruct((S, 6144), BF16), jax.ShapeDtypeStruct((S, CONV_W), F32),
                   jax.ShapeDtypeStruct((S, 1024), BF16), jax.ShapeDtypeStruct((N_HEADS, S, 1), F32),
                   jax.ShapeDtypeStruct((S, 3072), BF16), jax.ShapeDtypeStruct((1, 3072), F32),
                   jax.ShapeDtypeStruct((M, MEM_W), F32), jax.ShapeDtypeStruct((M, MEM_W), F32),
                   jax.ShapeDtypeStruct((1, MEM_HD), F32), jax.ShapeDtypeStruct((1024, S), BF16)],
        compiler_params=_params(("arbitrary",)),
    )(g, proj, proj, proj, proj, o, u, bg, convw, mqg, mk, mv, wot, wbat, wbct, wbmt)


def _conv_bwd(dcv, proj, convw):
    S = dcv.shape[0]
    tm = _tile(S, 512)
    nt = S // tm

    def body(d_ref, dp_ref, dn_ref, ccu_ref, hp_ref, hn_ref, cw_ref, dccu_ref, dcw_ref):
        i = pl.program_id(0)

        @pl.when(i == 0)
        def _():
            dcw_ref[...] = jnp.zeros(dcw_ref.shape, F32)

        cc, cu = ccu_ref[:, 0:CONV_W].astype(F32), ccu_ref[:, CONV_W:].astype(F32)
        z, z_up, z_dn = _conv_shifts(cc, cu, hp_ref, hn_ref, i, nt, tm)
        d = d_ref[...]
        dprev = jnp.where(i == 0, 0.0, dp_ref[7:8, :])
        dnext = jnp.where(i == nt - 1, 0.0, dn_ref[0:1, :])
        row = lax.broadcasted_iota(jnp.int32, d.shape, 0)
        d_up = jnp.where(row == 0, dprev, pltpu.roll(d, 1, 0))
        d_dn = jnp.where(row == tm - 1, dnext, pltpu.roll(d, tm - 1, 0))
        dz = cw_ref[0:1, :] * d_dn + cw_ref[1:2, :] * d + cw_ref[2:3, :] * d_up
        dccu_ref[:, 0:CONV_W] = (dz * cu).astype(BF16)
        dccu_ref[:, CONV_W:] = (dz * cc).astype(BF16)
        dcw_ref[0:1, :] += _colsum(d * z_up)
        dcw_ref[1:2, :] += _colsum(d * z)
        dcw_ref[2:3, :] += _colsum(d * z_dn)
        dcw_ref[3:4, :] += _colsum(d)

    hp, hn = _halo_specs(tm, S, 1024, 6, PROJ_HALO)
    dp, dn = _halo_specs(tm, S, CONV_W, 0, F32_HALO)
    return pl.pallas_call(
        body, name="conv_bwd", grid=(nt,),
        in_specs=[pl.BlockSpec((tm, CONV_W), lambda i: (i, 0)), dp, dn,
                  pl.BlockSpec((tm, 1024), lambda i: (i, 6)), hp, hn,
                  pl.BlockSpec((8, CONV_W), lambda i: (0, 0))],
        out_specs=[pl.BlockSpec((tm, 1024), lambda i: (i, 0)), pl.BlockSpec((8, CONV_W), lambda i: (0, 0))],
        out_shape=[jax.ShapeDtypeStruct((S, 1024), BF16), jax.ShapeDtypeStruct((8, CONV_W), F32)],
        compiler_params=_params(("arbitrary",)),
    )(dcv, dcv, dcv, proj, proj, proj, convw)


def _attn_bwd(q, k, v, do, qt, dot, lse, delta):
    S = q.shape[0]
    tq, tk = _tile(S, 512), _tile(S, 4096)
    ni, nk = S // tq, S // tk

    def body(q_ref, do_ref, qt_ref, dot_ref, k_ref, v_ref, lse_ref, delta_ref, dq_ref, dkt_hbm, dvt_hbm,
             dq_s, dkt_s, dvt_s, sem):
        h, i = pl.program_id(0), pl.program_id(1)

        @pl.when(i == 0)
        def _():
            dkt_s[...] = jnp.zeros(dkt_s.shape, F32)
            dvt_s[...] = jnp.zeros(dvt_s.shape, F32)

        dq_s[...] = jnp.zeros(dq_s.shape, F32)
        qv, dov, qtv, dotv = q_ref[...], do_ref[...], qt_ref[...], dot_ref[...]
        lse_c, delta_c = lse_ref[0], delta_ref[0]

        def step(c, carry):
            cols = pl.ds(pl.multiple_of(c * tk, tk), tk)
            kc, vc = k_ref[cols, :], v_ref[cols, :]
            p = jnp.exp(_dot_nt(qv, kc) - lse_c)
            dp = _dot_nt(dov, vc)
            ds = (p * (dp - delta_c)).astype(BF16)
            dq_s[...] += _dot(ds, kc)
            dvt_s[:, cols] += _dot(dotv, p.astype(BF16))
            dkt_s[:, cols] += _dot(qtv, ds)
            return carry

        lax.fori_loop(0, nk, step, 0)
        dq_ref[...] = dq_s[...]

        @pl.when(i == ni - 1)
        def _():
            head = pl.ds(pl.multiple_of(h * HP, HP), HP)
            out_k = pltpu.make_async_copy(dkt_s, dkt_hbm.at[head, :], sem.at[0])
            out_v = pltpu.make_async_copy(dvt_s, dvt_hbm.at[head, :], sem.at[1])
            out_k.start()
            out_v.start()
            out_k.wait()
            out_v.wait()

    col = pl.BlockSpec((1, tq, 1), lambda h, i: (h, i, 0))
    blk = pl.BlockSpec((tq, HP), lambda h, i: (i, h))
    blkt = pl.BlockSpec((HP, tq), lambda h, i: (h, i))
    res = pl.BlockSpec((S, HP), lambda h, i: (0, h))
    whole = pl.BlockSpec(memory_space=pl.ANY)
    return pl.pallas_call(
        body, name="attn_bwd", grid=(N_HEADS, ni),
        in_specs=[blk, blk, blkt, blkt, res, res, col, col],
        out_specs=[blk, whole, whole],
        out_shape=[jax.ShapeDtypeStruct((S, N_HEADS * HP), F32), jax.ShapeDtypeStruct((N_HEADS * HP, S), F32),
                   jax.ShapeDtypeStruct((N_HEADS * HP, S), F32)],
        scratch_shapes=[pltpu.VMEM((tq, HP), F32), pltpu.VMEM((HP, S), F32), pltpu.VMEM((HP, S), F32),
                        pltpu.SemaphoreType.DMA((2,))],
        compiler_params=_params(("parallel", "arbitrary")),
    )(q, do, qt, dot, k, v, lse, delta)


def _mla_prep_bwd(proj, tabs, dq, dkt, dvt, qg, kvg, qhg, khg, wuq, wuk, wuqt, wukt, wuvt):
    S = proj.shape[0]
    tm = _tile(S, 512)

    def body(a_ref, c_ref, sa_ref, sb_ref, dq_ref, dkt_ref, dvt_ref, qg_ref, kvg_ref, qhg_ref, khg_ref,
             wuq_ref, wuk_ref, wuqt_ref, wukt_ref, wuvt_ref,
             da_ref, dqraw_ref, dkraw_ref, dqg_ref, dkvg_ref, dqhg_ref, dkhg_ref, dv_ref):
        @pl.when(pl.program_id(0) == 0)
        def _():
            dqg_ref[...] = jnp.zeros(dqg_ref.shape, F32)
            dkvg_ref[...] = jnp.zeros(dkvg_ref.shape, F32)
            dqhg_ref[...] = jnp.zeros(dqhg_ref.shape, F32)
            dkhg_ref[...] = jnp.zeros(dkhg_ref.shape, F32)

        ql = a_ref[:, 0:Q_RANK].astype(F32)
        kvl = a_ref[:, Q_RANK:Q_RANK + KV_RANK].astype(F32)
        kpe = a_ref[:, Q_RANK + KV_RANK:Q_RANK + KV_RANK + HP].astype(F32)
        rq, qhat = _rms_parts(ql)
        rkv, kvhat = _rms_parts(kvl)
        qraw = _dot((qhat * qg_ref[...]).astype(BF16), wuq_ref[...])
        kn = _dot((kvhat * kvg_ref[...]).astype(BF16), wuk_ref[...])
        c, sa, sb = c_ref[...], sa_ref[...], sb_ref[...]
        dkpe = jnp.zeros(kpe.shape, F32)
        dqhg = jnp.zeros((1, HP), F32)
        dkhg = jnp.zeros((1, HP), F32)
        for h in range(N_HEADS):
            sl = slice(h * HP, (h + 1) * HP)
            r, that = _rms_parts(qraw[:, sl], QK_DIM)
            dtn = _rope_t(dq_ref[:, sl], c, sa, sb) * SCALE
            dqhg = dqhg + _colsum(dtn * that)
            dqraw_ref[:, sl] = _rms_bwd(dtn * qhg_ref[...], that, r, QK_DIM).astype(BF16)
            r, that = _rms_parts(kn[:, sl] + kpe, QK_DIM)
            dtn = _rope_t(dkt_ref[sl, :].T, c, sa, sb)
            dv_ref[:, sl] = dvt_ref[sl, :].T.astype(BF16)
            dkhg = dkhg + _colsum(dtn * that)
            dkr = _rms_bwd(dtn * khg_ref[...], that, r, QK_DIM)
            dkraw_ref[:, sl] = dkr.astype(BF16)
            dkpe = dkpe + dkr
        dqhg_ref[...] += dqhg
        dkhg_ref[...] += dkhg
        dqn = _dot(dqraw_ref[...], wuqt_ref[...])
        dqg_ref[...] += _colsum(dqn * qhat)
        da_ref[:, 0:Q_RANK] = _rms_bwd(dqn * qg_ref[...], qhat, rq, Q_RANK).astype(BF16)
        dkvn = _dot(dkraw_ref[...], wukt_ref[...]) + _dot(dv_ref[...], wuvt_ref[...])
        dkvg_ref[...] += _colsum(dkvn * kvhat)
        da_ref[:, Q_RANK:Q_RANK + KV_RANK] = _rms_bwd(dkvn * kvg_ref[...], kvhat, rkv, KV_RANK).astype(BF16)
        da_ref[:, Q_RANK + KV_RANK:Q_RANK + KV_RANK + HP] = dkpe.astype(BF16)
        da_ref[:, Q_RANK + KV_RANK + HP:] = jnp.zeros((tm, 1024 - Q_RANK - KV_RANK - HP), BF16)

    row = lambda w: pl.BlockSpec((tm, w), lambda i: (i, 0))
    full = lambda a: pl.BlockSpec(a.shape, lambda i: (0,) * a.ndim)
    acc = lambda c: pl.BlockSpec((1, c), lambda i: (0, 0))
    return pl.pallas_call(
        body, name="mla_prep_bwd", grid=(S // tm,),
        in_specs=[pl.BlockSpec((tm, 1024), lambda i: (i, 7)), row(HP), row(HP), row(HP),
                  row(1024), pl.BlockSpec((1024, tm), lambda i: (0, i)), pl.BlockSpec((1024, tm), lambda i: (0, i)),
                  full(qg), full(kvg), full(qhg), full(khg),
                  full(wuq), full(wuk), full(wuqt), full(wukt), full(wuvt)],
        out_specs=[row(1024), row(1024), row(1024), acc(Q_RANK), acc(KV_RANK), acc(HP), acc(HP), row(1024)],
        out_shape=[jax.ShapeDtypeStruct((S, 1024), BF16)] * 3
        + [jax.ShapeDtypeStruct((1, Q_RANK), F32), jax.ShapeDtypeStruct((1, KV_RANK), F32),
           jax.ShapeDtypeStruct((1, HP), F32), jax.ShapeDtypeStruct((1, HP), F32),
           jax.ShapeDtypeStruct((S, 1024), BF16)],
        compiler_params=_params(("arbitrary",)),
    )(proj, *tabs, dq, dkt, dvt, qg, kvg, qhg, khg, wuq, wuk, wuqt, wukt, wuvt)


def _inproj_bwd(dmain, dccu, dsega, wint, x, g, ng, carry=()):
    S = x.shape[0]
    tm = _tile(S, 512)
    nm, nc = dmain.shape[1], dccu.shape[1]
    assert nm + nc + dsega.shape[1] == PW

    def body(dm_ref, dc_ref, da_ref, w_ref, x_ref, g_ref, ng_ref, dx_ref, dng_ref):
        @pl.when(pl.program_id(0) == 0)
        def _():
            dng_ref[...] = jnp.zeros(dng_ref.shape, F32)

        dh = (_dot(dm_ref[...], w_ref[0:nm, :]) + _dot(dc_ref[...], w_ref[nm:nm + nc, :])
              + _dot(da_ref[...], w_ref[nm + nc:PW, :]))
        r, xhat = _rms_parts(x_ref[...])
        dng_ref[...] += _colsum(dh * xhat)
        dx_ref[...] = g_ref[...] + _rms_bwd(dh * ng_ref[...], xhat, r, D_MODEL)

    row = lambda w: pl.BlockSpec((tm, w), lambda i: (i, 0))
    in_specs = [row(nm), row(nc), row(dsega.shape[1]), pl.BlockSpec((PW, D_MODEL), lambda i: (0, 0), pipeline_mode=pl.Buffered(1)),
                row(D_MODEL), row(D_MODEL), pl.BlockSpec((1, D_MODEL), lambda i: (0, 0))]
    out_specs = [row(D_MODEL), pl.BlockSpec((1, D_MODEL), lambda i: (0, 0))]
    out_shape = [jax.ShapeDtypeStruct((S, D_MODEL), F32), jax.ShapeDtypeStruct((1, D_MODEL), F32)]
    args = (dmain, dccu, dsega, wint, x, g, ng)
    if not carry:
        dx, dng = pl.pallas_call(
            body, name="inproj_bwd", grid=(S // tm,), in_specs=in_specs, out_specs=out_specs, out_shape=out_shape,
            compiler_params=_params(("arbitrary",)))(*args)
        return dx, dng, []
    wrapped, extra = _carried(body, 7, 2, carry, gather=False)
    res = pl.pallas_call(
        wrapped, name="inproj_bwd_scatter", grid=(S // tm,), in_specs=in_specs + extra["in_specs"],
        out_specs=out_specs + extra["out_specs"], out_shape=out_shape + extra["out_shape"],
        scratch_shapes=extra["scratch_shapes"], compiler_params=_params(("arbitrary",)))(*args, *carry)
    return res[0], res[1], list(res[2:])


def _memkv_bwd(mem, mng, mkg, wmkv, wmkvt, dmk, dmv):
    M = mem.shape[0]

    def body(mem_ref, mng_ref, mkg_ref, w_ref, wt_ref, dmk_ref, dmv_ref, dw_ref, dmng_ref, dmkg_ref, d_s):
        r, mhat = _rms_parts(mem_ref[...])
        mn = (mhat * mng_ref[...]).astype(BF16)
        mkv = _dot(mn, w_ref[...])
        dmkg = jnp.zeros((1, MEM_HD), F32)
        for h in range(MEM_HEADS):
            sl = slice(h * MEM_HD, (h + 1) * MEM_HD)
            rk, khat = _rms_parts(mkv[:, 2 * MEM_HD * h:2 * MEM_HD * h + MEM_HD])
            dkn = dmk_ref[:, sl]
            dmkg = dmkg + _colsum(dkn * khat)
            d_s[:, 2 * MEM_HD * h:2 * MEM_HD * h + MEM_HD] = _rms_bwd(dkn * mkg_ref[...], khat, rk, MEM_HD).astype(BF16)
            d_s[:, 2 * MEM_HD * h + MEM_HD:2 * MEM_HD * (h + 1)] = dmv_ref[:, sl].astype(BF16)
        dmkg_ref[...] = dmkg
        dw_ref[...] = _dot_tn(mn, d_s[...])
        dmn = _dot(d_s[...], wt_ref[...])
        dmng_ref[...] = _colsum(dmn * mhat)

    return pl.pallas_call(
        body, name="memkv_bwd",
        out_shape=[jax.ShapeDtypeStruct((D_MODEL, 2 * MEM_W), F32), jax.ShapeDtypeStruct((1, D_MODEL), F32),
                   jax.ShapeDtypeStruct((1, MEM_HD), F32)],
        scratch_shapes=[pltpu.VMEM((M, 2 * MEM_W), BF16)],
        compiler_params=pltpu.CompilerParams(vmem_limit_bytes=VMEM_LIMIT_V7X),
    )(mem, mng, mkg, wmkv, wmkvt, dmk, dmv)


def _mm_tn(a, b, name, col0=0, ncols=None):
    S, M = a.shape
    N = b.shape[1] if ncols is None else ncols
    tm, tn, ts = _tile(M, 1024), _tile(N, 1024), _tile(S, 2048)
    assert col0 % tn == 0
    jb = col0 // tn

    def body(a_ref, b_ref, o_ref):
        @pl.when(pl.program_id(2) == 0)
        def _():
            o_ref[...] = jnp.zeros(o_ref.shape, F32)

        o_ref[...] += _dot_tn(a_ref[...].astype(BF16), b_ref[...].astype(BF16))

    return pl.pallas_call(
        body, name=name, grid=(M // tm, N // tn, S // ts),
        in_specs=[pl.BlockSpec((ts, tm), lambda i, j, k: (k, i)),
                  pl.BlockSpec((ts, tn), lambda i, j, k: (k, j + jb))],
        out_specs=pl.BlockSpec((tm, tn), lambda i, j, k: (i, j)),
        out_shape=jax.ShapeDtypeStruct((M, N), F32),
        compiler_params=_params(("parallel", "parallel", "arbitrary")),
    )(a, b)


def _adamw(w, g0, g1, m, v, name):
    R, C = w.shape
    tr = R
    for cand in (512, 256, 128, 64, 32, 16, 8):
        if R % cand == 0 and cand * C * 4 <= (1 << 20):
            tr = cand
            break
    c1 = 1.0 / (1.0 - ADAM_B1 ** ADAM_STEP)
    c2 = 1.0 / (1.0 - ADAM_B2 ** ADAM_STEP)

    def body(w_ref, g0_ref, g1_ref, m_ref, v_ref, g_ref, d_ref, nm_ref, nv_ref):
        g = g0_ref[...] + g1_ref[...]
        nm = ADAM_B1 * m_ref[...] + (1.0 - ADAM_B1) * g
        nv = ADAM_B2 * v_ref[...] + (1.0 - ADAM_B2) * (g * g)
        g_ref[...] = g
        nm_ref[...] = nm
        nv_ref[...] = nv
        d_ref[...] = -ADAM_LR * ((nm * c1) / (jnp.sqrt(nv * c2) + ADAM_EPS) + ADAM_WD * w_ref[...])

    blk = pl.BlockSpec((tr, C), lambda i: (i, 0))
    return pl.pallas_call(
        body, name=name, grid=(R // tr,),
        in_specs=[blk] * 5, out_specs=[blk] * 4,
        out_shape=[jax.ShapeDtypeStruct((R, C), F32)] * 4,
        compiler_params=_params(("parallel",)),
    )(w, g0, g1, m, v)


def _sum_slabs(a, name):
    K, R, C = a.shape
    tr = R
    for cand in (512, 256, 128, 64, 32, 16, 8):
        if R % cand == 0 and cand * C * 4 * K <= (4 << 20):
            tr = cand
            break

    def body(a_ref, o_ref):
        t = a_ref[0].astype(F32)
        for k in range(1, K):
            t = t + a_ref[k].astype(F32)
        o_ref[...] = t

    return pl.pallas_call(
        body, name=name, grid=(R // tr,),
        in_specs=[pl.BlockSpec((K, tr, C), lambda i: (0, i, 0))],
        out_specs=pl.BlockSpec((tr, C), lambda i: (i, 0)),
        out_shape=jax.ShapeDtypeStruct((R, C), F32),
        compiler_params=_params(("parallel",)),
    )(a)


def _gather_chips(arrs):
    n = len(arrs)
    halves = [a.shape[0] // 2 for a in arrs]
    assert all(a.shape[0] == 2 * hf for a, hf in zip(arrs, halves))

    def body(*refs):
        ins, outs = refs[:n], refs[n:2 * n]
        send1, recv1, send2, recv2, loc = refs[2 * n:]
        x, y, c = lax.axis_index("x"), lax.axis_index("y"), lax.axis_index("c")
        me = 2 * x + y
        mine = [pl.ds(c * hf, hf) for hf in halves]
        theirs = [pl.ds((1 - c) * hf, hf) for hf in halves]
        sibling = (x, y, 1 - c)
        waits = []
        for a in range(n):
            own = pltpu.make_async_copy(ins[a], outs[a].at[me], loc.at[a])
            own.start()
            waits.append(own.wait)

        def over_ici(a, k, src_chip):
            return pltpu.make_async_remote_copy(
                src_ref=ins[a].at[mine[a]], dst_ref=outs[a].at[src_chip, mine[a]], send_sem=send1.at[3 * a + k - 1],
                recv_sem=recv1.at[3 * a + k - 1], device_id=_chip_peer(x, y, c, k), device_id_type=MESH)

        def to_sibling(a, k, layers):
            block = outs[a].at[me ^ k, layers]
            return pltpu.make_async_remote_copy(
                src_ref=block, dst_ref=block, send_sem=send2.at[3 * a + k - 1], recv_sem=recv2.at[3 * a + k - 1],
                device_id=sibling, device_id_type=MESH)

        for a in range(n):
            for k in (1, 2, 3):
                cp = over_ici(a, k, me)
                cp.start()
                waits.append(cp.wait_send)
        for a in range(n):
            for k in (1, 2, 3):
                over_ici(a, k, me ^ k).wait_recv()
                cp = to_sibling(a, k, mine[a])
                cp.start()
                waits.append(cp.wait_send)
        for a in range(n):
            for k in (1, 2, 3):
                to_sibling(a, k, theirs[a]).wait_recv()
        for w in waits:
            w()

    return pl.pallas_call(
        body, name="gather_weights",
        in_specs=[HBM] * n, out_specs=[HBM] * n,
        out_shape=[jax.ShapeDtypeStruct((4,) + a.shape, a.dtype) for a in arrs],
        scratch_shapes=[pltpu.SemaphoreType.DMA((3 * n,)), pltpu.SemaphoreType.DMA((3 * n,)),
                        pltpu.SemaphoreType.DMA((3 * n,)), pltpu.SemaphoreType.DMA((3 * n,)),
                        pltpu.SemaphoreType.DMA((n,))],
    )(*arrs)


def _scatter_chips(arrs, small):
    n = len(arrs)

    def body(*refs):
        ins, small_in = refs[:n], refs[n]
        outs, small_out = refs[n + 1:2 * n + 1], refs[2 * n + 1]
        send, recv, loc, ssend, srecv = refs[2 * n + 2:]
        x, y, c = lax.axis_index("x"), lax.axis_index("y"), lax.axis_index("c")
        me = 2 * x + y
        me8 = 4 * x + 2 * y + c
        copies = []
        for a in range(n):
            own = pltpu.make_async_copy(ins[a].at[me], outs[a].at[me], loc.at[a])
            own.start()
            copies.append(own)
        own = pltpu.make_async_copy(small_in, small_out.at[me8], loc.at[n])
        own.start()
        copies.append(own)
        for k in range(1, 8):
            cp = pltpu.make_async_remote_copy(
                src_ref=small_in, dst_ref=small_out.at[me8], send_sem=ssend.at[k - 1], recv_sem=srecv.at[k - 1],
                device_id=(x ^ (k >> 2), y ^ ((k >> 1) & 1), c ^ (k & 1)), device_id_type=MESH)
            cp.start()
            copies.append(cp)
        for a in range(n):
            for k in (1, 2, 3):
                cp = pltpu.make_async_remote_copy(
                    src_ref=ins[a].at[me ^ k], dst_ref=outs[a].at[me], send_sem=send.at[3 * a + k - 1],
                    recv_sem=recv.at[3 * a + k - 1], device_id=_chip_peer(x, y, c, k), device_id_type=MESH)
                cp.start()
                copies.append(cp)
        for cp in copies:
            cp.wait()

    return pl.pallas_call(
        body, name="scatter_grads",
        in_specs=[HBM] * (n + 1), out_specs=[HBM] * (n + 1),
        out_shape=[jax.ShapeDtypeStruct(a.shape, a.dtype) for a in arrs]
        + [jax.ShapeDtypeStruct((8,) + small.shape, small.dtype)],
        scratch_shapes=[pltpu.SemaphoreType.DMA((3 * n,)), pltpu.SemaphoreType.DMA((3 * n,)),
                        pltpu.SemaphoreType.DMA((n + 1,)), pltpu.SemaphoreType.DMA((7,)),
                        pltpu.SemaphoreType.DMA((7,))],
    )(*arrs, small)


def _swap_cores(arrs):
    n = len(arrs)

    def body(*refs):
        ins, outs = refs[:n], refs[n:2 * n]
        send, recv = refs[2 * n:]
        x, y, c = lax.axis_index("x"), lax.axis_index("y"), lax.axis_index("c")
        copies = []
        for a in range(n):
            cp = pltpu.make_async_remote_copy(
                src_ref=ins[a], dst_ref=outs[a], send_sem=send.at[a], recv_sem=recv.at[a],
                device_id=(x, y, 1 - c), device_id_type=MESH)
            cp.start()
            copies.append(cp)
        for cp in copies:
            cp.wait()

    return pl.pallas_call(
        body, name="swap_cores",
        in_specs=[HBM] * n, out_specs=[HBM] * n,
        out_shape=[jax.ShapeDtypeStruct(a.shape, a.dtype) for a in arrs],
        scratch_shapes=[pltpu.SemaphoreType.DMA((n,)), pltpu.SemaphoreType.DMA((n,))],
    )(*arrs)


def _pad_last(a, n):
    return jnp.pad(a, [(0, 0)] * (a.ndim - 1) + [(0, n - a.shape[-1])])


def _pad_w_in(w):
    lead = w.shape[:-1]
    seg = lambda a, b: w[..., a:b]
    ga = _pad_last(seg(2720, 3232).reshape(lead + (N_HEADS, V_DIM)), HP).reshape(lead + (1024,))
    kpe = jnp.pad(seg(640, 672), [(0, 0)] * len(lead) + [(NOPE_DIM, HP - QK_DIM)])
    zero = jnp.zeros(lead + (PW - 7936,), w.dtype)
    return jnp.concatenate(
        [seg(4256, 7328), ga, seg(672, 1184), seg(2208, 2720), seg(3232, 3744), seg(3744, 4256),
         seg(1184, 1696), seg(1696, 2208), seg(0, 384), seg(384, 640), kpe, zero], axis=-1)


def _unpad_w_in(w):
    lead = w.shape[:-1]
    seg = lambda a, n: w[..., a:a + n]
    ga = seg(O_GA, 1024).reshape(lead + (N_HEADS, HP))[..., :V_DIM].reshape(lead + (N_HEADS * V_DIM,))
    return jnp.concatenate(
        [seg(O_QL, 384), seg(O_KVL, 256), seg(O_KPE + NOPE_DIM, ROPE_DIM), seg(O_CB, 512), seg(O_CC, 512),
         seg(O_CU, 512), seg(O_QM, 512), ga, seg(O_GC, 512), seg(O_GM, 512), seg(O_R, 3072)], axis=-1)


def _cols_from_shards(g):
    _, L, R, C = g.shape
    return jnp.transpose(g, (1, 2, 0, 3)).reshape(L, R, 4 * C)


def _t(w):
    return jnp.swapaxes(w, -1, -2)


def _layer_fwd(x, mem, tabs, p, next_shards=()):
    proj, xn, gathered = _inproj(x, p["norm_g"], p["w_in"], next_shards)
    q, k, v, qn, kvn, score_bound, qt, vt = _mla_prep(proj, tabs, p["q_norm_g"], p["kv_norm_g"], p["q_head_g"], p["k_head_g"],
                                        p["w_uq"], p["w_uk"], p["w_uv"])
    o, lse = _attn_fwd(qt, k, vt, score_bound)
    mk, mv = _memkv(mem, p["mem_norm_g"], p["mem_k_g"], p["w_mkv"])
    x_new, oa, oc, om, u, y = _merge(proj, o, x, p["b_gate"], p["conv_wb"], p["mem_q_g"], mk, mv,
                                     p["w_br_attn"], p["w_br_conv"], p["w_br_mem"], p["w_out"])
    saved = dict(x=x, proj=proj, xn=xn, q=q, qt=qt, k=k, v=v, qn=qn, kvn=kvn, o=o, lse=lse, mk=mk, mv=mv,
                 oa=oa, oc=oc, om=om, u=u, y=y)
    return x_new, saved, gathered


def _layer_bwd(g, mem, tabs, p, s, to_owner=()):
    S = g.shape[0]
    dmain, dcv, d_o, delta, du, dbg, dmk, dmv, dmqg, d_ot = _merge_bwd(
        g, s["proj"], s["o"], s["u"], p["b_gate"], p["conv_wb"], p["mem_q_g"], s["mk"], s["mv"],
        p["w_out_t"], p["w_br_attn_t"], p["w_br_conv_t"], p["w_br_mem_t"])
    dccu, dconv = _conv_bwd(dcv, s["proj"], p["conv_wb"])
    dq, dkt, dvt = _attn_bwd(s["q"], s["k"], s["v"], d_o, s["qt"], d_ot, s["lse"], delta)
    dsega, dqraw, dkraw, dqg, dkvg, dqhg, dkhg, dv = _mla_prep_bwd(
        s["proj"], tabs, dq, dkt, dvt, p["q_norm_g"], p["kv_norm_g"], p["q_head_g"], p["k_head_g"],
        p["w_uq"], p["w_uk"], p["w_uq_t"], p["w_uk_t"], p["w_uv_t"])
    dx, dng, received = _inproj_bwd(dmain, dccu, dsega, p["w_in_t"], s["x"], g, p["norm_g"], to_owner)
    dwmkv, dmng, dmkg = _memkv_bwd(mem, p["mem_norm_g"], p["mem_k_g"], p["w_mkv"], p["w_mkv_t"], dmk, dmv)
    grads = dict(
        norm_g=dng, b_gate=dbg, q_norm_g=dqg, kv_norm_g=dkvg, q_head_g=dqhg, k_head_g=dkhg,
        conv_wb=dconv, mem_norm_g=dmng, mem_q_g=dmqg, mem_k_g=dmkg, w_mkv=dwmkv,
        w_in=jnp.concatenate([_mm_tn(s["xn"], dmain, "grad_w_in"), _mm_tn(s["xn"], dccu, "grad_w_in_conv"),
                              _mm_tn(s["xn"], dsega, "grad_w_in_lat")], axis=1),
        w_uq=_mm_tn(s["qn"], dqraw, "grad_w_uq"),
        w_uk=_mm_tn(s["kvn"], dkraw, "grad_w_uk"),
        w_uv=_mm_tn(s["kvn"], dv, "grad_w_uv"),
        w_br_attn=_mm_tn(s["oa"], du, "grad_w_br_attn", 0, 1024),
        w_br_conv=_mm_tn(s["oc"], du, "grad_w_br_conv", 1024, 1024),
        w_br_mem=_mm_tn(s["om"], du, "grad_w_br_mem", 2048, 1024),
        w_out=_mm_tn(s["y"], g, "grad_w_out"),
    )
    return dx, grads, received


def _layer_params(big, full, l):
    p = {}
    w_in = _pad_w_in(big["w_in"])
    w_uq = _pad_last(big["w_uq"].reshape(Q_RANK, N_HEADS, QK_DIM), HP).reshape(Q_RANK, 1024)
    ukv = big["w_ukv"].reshape(KV_RANK, N_HEADS, NOPE_DIM + V_DIM)
    w_uk = _pad_last(ukv[..., :NOPE_DIM], HP).reshape(KV_RANK, 1024)
    w_uv = _pad_last(ukv[..., NOPE_DIM:], HP).reshape(KV_RANK, 1024)
    w_ba = jnp.pad(big["w_br_attn"].reshape(N_HEADS, V_DIM, D_MODEL), ((0, 0), (0, HP - V_DIM), (0, 0)))
    w_ba = w_ba.reshape(1024, D_MODEL)
    p.update(w_in=w_in, w_uq=w_uq, w_uk=w_uk, w_uv=w_uv, w_br_attn=w_ba, w_br_conv=big["w_br_conv"],
             w_br_mem=big["w_br_mem"], w_out=big["w_out"], w_mkv=big["w_mkv"])
    for n in ("w_in", "w_uq", "w_uk", "w_uv", "w_br_attn", "w_br_conv", "w_br_mem", "w_out", "w_mkv"):
        p[n + "_t"] = _t(p[n])
    for n in ("norm_g", "b_gate", "q_norm_g", "kv_norm_g", "mem_norm_g", "mem_q_g", "mem_k_g"):
        p[n] = full[n][l][None, :]
    p["q_head_g"] = _pad_last(full["q_head_g"][l][None, :], HP)
    p["k_head_g"] = _pad_last(full["k_head_g"][l][None, :], HP)
    p["conv_wb"] = jnp.concatenate(
        [full["conv_w"][l], full["conv_b"][l][None, :], jnp.zeros((4, CONV_W), F32)], axis=0)
    return p


def _join_shards(name, g):
    _, R, C = g.shape
    if name in _COL_SHARDED:
        return jnp.transpose(g, (1, 0, 2)).reshape(R, 4 * C)
    return g.reshape(4 * R, C)


def _split_shards(name, w):
    R, C = w.shape
    if name in _COL_SHARDED:
        return jnp.transpose(w.reshape(R, 4, C // 4), (1, 0, 2)).astype(BF16)
    return w.reshape(4, R // 4, C).astype(BF16)


def _unpad_grads(gp):
    out = {"w_in": _unpad_w_in(gp["w_in"])}
    out["w_uq"] = gp["w_uq"].reshape(Q_RANK, N_HEADS, HP)[..., :QK_DIM].reshape(Q_RANK, N_HEADS * QK_DIM)
    duk = gp["w_uk"].reshape(KV_RANK, N_HEADS, HP)[..., :NOPE_DIM]
    duv = gp["w_uv"].reshape(KV_RANK, N_HEADS, HP)[..., :V_DIM]
    out["w_ukv"] = jnp.concatenate([duk, duv], axis=-1).reshape(KV_RANK, 1024)
    out["w_br_attn"] = gp["w_br_attn"].reshape(N_HEADS, HP, D_MODEL)[:, :V_DIM].reshape(512, D_MODEL)
    for n in ("w_br_conv", "w_br_mem", "w_out", "w_mkv"):
        out[n] = gp[n]
    return out


def _train_step(x, mem, positions, w, target):
    S = x.shape[0]
    invf16 = ROPE_BASE ** (-jnp.arange(0, ROPE_DIM, 2, dtype=F32) / ROPE_DIM)
    invf = jnp.concatenate([jnp.zeros((NOPE_DIM,), F32), invf16, invf16, jnp.zeros((HP - QK_DIM,), F32)])[None, :]
    tabs = _rope_tables(jnp.broadcast_to(positions.reshape(S, 1), (S, HP)), invf)
    shards = [[w[n][l].astype(BF16) for n in _BIG] for l in range(DEPTH)]
    first = _gather_chips(shards[0] + [w["conv_w"]])
    gathered = first[:-1]
    full = {n: w[n] for n in _SMALL}
    full["conv_w"] = _cols_from_shards(first[-1])
    params, saved = [], []
    h = x
    for l in range(DEPTH):
        big = {n: _join_shards(n, g) for n, g in zip(_BIG, gathered)}
        params.append(_layer_params(big, full, l))
        h, s, gathered = _layer_fwd(h, mem, tabs, params[l], shards[l + 1] if l + 1 < DEPTH else ())
        saved.append(s)
    g, loss_part = _loss_head(h, target)
    per_layer, received = [None] * DEPTH, [None] * DEPTH
    to_owner = ()
    for l in reversed(range(DEPTH)):
        g, per_layer[l], got = _layer_bwd(g, mem, tabs, params[l], saved[l], to_owner)
        if to_owner:
            received[l + 1] = got
        big_g = _unpad_grads(per_layer[l])
        to_owner = [_split_shards(n, big_g[n]) for n in _BIG]
    st = lambda n: jnp.stack([per_layer[l][n] for l in range(DEPTH)])
    small = {}
    for n in ("norm_g", "b_gate", "q_norm_g", "kv_norm_g", "mem_norm_g", "mem_q_g", "mem_k_g"):
        small[n] = st(n)[:, 0, :]
    small["q_head_g"] = st("q_head_g")[:, 0, :QK_DIM]
    small["k_head_g"] = st("k_head_g")[:, 0, :QK_DIM]
    cwb = st("conv_wb")
    small["conv_w"] = cwb[:, 0:3, :]
    small["conv_b"] = cwb[:, 3, :]
    return loss_part, g, received, to_owner, small


_COL_SHARDED = ("w_in", "w_uq", "w_ukv", "w_br_attn", "w_br_conv", "w_br_mem")
_ROW_SHARDED = ("w_mkv", "w_out")
_BIG = _COL_SHARDED + _ROW_SHARDED
_SMALL = ("norm_g", "b_gate", "q_norm_g", "kv_norm_g", "q_head_g", "k_head_g", "conv_w", "conv_b",
          "mem_norm_g", "mem_q_g", "mem_k_g")
_ORDER = ("norm_g", "w_in", "b_gate", "q_norm_g", "w_uq", "kv_norm_g", "w_ukv", "q_head_g", "k_head_g",
          "conv_w", "conv_b", "mem_norm_g", "w_mkv", "mem_q_g", "mem_k_g", "w_br_attn", "w_br_conv",
          "w_br_mem", "w_out")


def _pack_small(d, extra):
    flat = jnp.concatenate([d[n].reshape(-1) for n in _SMALL] + [extra.reshape(-1)])
    n = flat.shape[0]
    rows = -(-n // 1024) * 8
    return jnp.pad(flat, (0, rows * 128 - n)).reshape(rows, 128)


def _unpack_small(packed, like):
    flat = packed.reshape(-1)
    out, off = {}, 0
    for n in _SMALL:
        sz = int(np.prod(like[n].shape))
        out[n] = flat[off:off + sz].reshape(like[n].shape)
        off += sz
    return out, flat[off:]


def kernel(x, mem, positions, norm_g, w_in, b_gate, q_norm_g, w_uq, kv_norm_g, w_ukv, q_head_g, k_head_g, conv_w, conv_b, mem_norm_g, w_mkv, mem_q_g, mem_k_g, w_br_attn, w_br_conv, w_br_mem, w_out, loss_target, m_norm_g, m_w_in, m_b_gate, m_q_norm_g, m_w_uq, m_kv_norm_g, m_w_ukv, m_q_head_g, m_k_head_g, m_conv_w, m_conv_b, m_mem_norm_g, m_w_mkv, m_mem_q_g, m_mem_k_g, m_w_br_attn, m_w_br_conv, m_w_br_mem, m_w_out, v_norm_g, v_w_in, v_b_gate, v_q_norm_g, v_w_uq, v_kv_norm_g, v_w_ukv, v_q_head_g, v_k_head_g, v_conv_w, v_conv_b, v_mem_norm_g, v_w_mkv, v_mem_q_g, v_mem_k_g, v_w_br_attn, v_w_br_conv, v_w_br_mem, v_w_out):
    w = dict(norm_g=norm_g, w_in=w_in, b_gate=b_gate, q_norm_g=q_norm_g, w_uq=w_uq, kv_norm_g=kv_norm_g,
             w_ukv=w_ukv, q_head_g=q_head_g, k_head_g=k_head_g, conv_w=conv_w, conv_b=conv_b,
             mem_norm_g=mem_norm_g, w_mkv=w_mkv, mem_q_g=mem_q_g, mem_k_g=mem_k_g, w_br_attn=w_br_attn,
             w_br_conv=w_br_conv, w_br_mem=w_br_mem, w_out=w_out)
    m = dict(norm_g=m_norm_g, w_in=m_w_in, b_gate=m_b_gate, q_norm_g=m_q_norm_g, w_uq=m_w_uq,
             kv_norm_g=m_kv_norm_g, w_ukv=m_w_ukv, q_head_g=m_q_head_g, k_head_g=m_k_head_g, conv_w=m_conv_w,
             conv_b=m_conv_b, mem_norm_g=m_mem_norm_g, w_mkv=m_w_mkv, mem_q_g=m_mem_q_g, mem_k_g=m_mem_k_g,
             w_br_attn=m_w_br_attn, w_br_conv=m_w_br_conv, w_br_mem=m_w_br_mem, w_out=m_w_out)
    v = dict(norm_g=v_norm_g, w_in=v_w_in, b_gate=v_b_gate, q_norm_g=v_q_norm_g, w_uq=v_w_uq,
             kv_norm_g=v_kv_norm_g, w_ukv=v_w_ukv, q_head_g=v_q_head_g, k_head_g=v_k_head_g, conv_w=v_conv_w,
             conv_b=v_conv_b, mem_norm_g=v_mem_norm_g, w_mkv=v_w_mkv, mem_q_g=v_mem_q_g, mem_k_g=v_mem_k_g,
             w_br_attn=v_w_br_attn, w_br_conv=v_w_br_conv, w_br_mem=v_w_br_mem, w_out=v_w_out)
    chip = 2 * lax.axis_index("x") + lax.axis_index("y")

    loss_part, grad_x, received, last_slabs, grads = _train_step(x[0], mem[0], positions[0], w, loss_target[0])

    loss_vec = jnp.zeros((128,), F32).at[0].set(0.5 / D_MODEL * jnp.sum(loss_part))
    small = _pack_small(grads, loss_vec)
    scattered = _scatter_chips(last_slabs, small)
    received[0] = scattered[:-1]
    small_sum = _sum_slabs(scattered[-1], "sum_small")
    partial = []
    for a, n in enumerate(_BIG):
        partial.append(jnp.concatenate([_sum_slabs(received[l][a], "sum_" + n) for l in range(DEPTH)], axis=0))
    other = _swap_cores(partial)

    small_g, tail = _unpack_small(small_sum, {n: (grads[n]) for n in _SMALL})
    loss = tail[0]
    small_g["conv_w"] = lax.dynamic_slice_in_dim(small_g["conv_w"], chip * (CONV_W // 4), CONV_W // 4, axis=2)

    outs_g, outs_d, outs_m, outs_v = {}, {}, {}, {}
    for n, p0, p1 in zip(_BIG, partial, other):
        shape = w[n].shape
        flat = lambda t: t.reshape(p0.shape)
        g_, d_, m_, v_ = _adamw(flat(w[n]), p0, p1, flat(m[n]), flat(v[n]), "adamw_" + n)
        outs_g[n], outs_d[n], outs_m[n], outs_v[n] = (t.reshape(shape) for t in (g_, d_, m_, v_))
    zero_small = jnp.zeros_like(small_sum)
    pk = lambda d: _pack_small(d, jnp.zeros((128,), F32))
    g_, d_, m_, v_ = _adamw(pk(w), _pack_small(small_g, jnp.zeros((128,), F32)), zero_small, pk(m), pk(v),
                            "adamw_small")
    like = {n: w[n] for n in _SMALL}
    for dst, packed in ((outs_g, g_), (outs_d, d_), (outs_m, m_), (outs_v, v_)):
        dst.update(_unpack_small(packed, like)[0])

    return (loss, grad_x[None], *[outs_g[n] for n in _ORDER], *[outs_d[n] for n in _ORDER],
            *[outs_m[n] for n in _ORDER], *[outs_v[n] for n in _ORDER])
```
